```python
import jax, jax.numpy as jnp
from jax import lax
import numpy as np

D_MODEL = 1024
BATCH = 8
SEQ = 8192
DEPTH = 2

W_A = D_MODEL
K_A = 3
W_B = D_MODEL
K_B = 31
W_C = D_MODEL
POOL_WINDOWS = (2, 4, 8, 16)
N_POOL_GROUPS = len(POOL_WINDOWS)
GC = W_C // N_POOL_GROUPS
N_BRANCH = 3
D_FF = 4 * D_MODEL
EPS = 1e-6
COLS_A = 3 * W_A
COLS_B = 2 * W_B
COLS_C = W_C
COLS_G = N_BRANCH * D_MODEL
P_IN = COLS_A + COLS_B + COLS_C + COLS_G
SPLITS = (W_A, 2 * W_A, 3 * W_A, 3 * W_A + W_B, COLS_A + COLS_B, COLS_A + COLS_B + COLS_C)

kernel_name = "hybrid_conv_pool_gated_block"


def rmsnorm(x, g):
    x32 = x.astype(jnp.float32)
    y = x32 * lax.rsqrt(jnp.mean(x32 * x32, axis=-1, keepdims=True) + EPS)
    return (y * g.astype(jnp.float32)).astype(x.dtype)


def layernorm(x, g, b):
    x32 = x.astype(jnp.float32)
    mu = jnp.mean(x32, axis=-1, keepdims=True)
    xc = x32 - mu
    var = jnp.mean(xc * xc, axis=-1, keepdims=True)
    y = xc * lax.rsqrt(var + EPS) * g.astype(jnp.float32) + b.astype(jnp.float32)
    return y.astype(x.dtype)


def causal_depthwise_conv(u, w):
    k, c = w.shape
    return lax.conv_general_dilated(
        u, w[:, None, :].astype(u.dtype), window_strides=(1,), padding=[(k - 1, 0)],
        dimension_numbers=("NWC", "WIO", "NWC"), feature_group_count=c)


def short_conv_mixer(b_gate, c_gate, xh, conv_w, w_out):
    z = causal_depthwise_conv(c_gate * xh, conv_w)
    return (b_gate * z) @ w_out


def conformer_conv_mixer(val, gate, conv_w, conv_b, ln_g, ln_b, w_out, b_out):
    u = val * jax.nn.sigmoid(gate)
    u = causal_depthwise_conv(u, conv_w) + conv_b
    u = jax.nn.silu(layernorm(u, ln_g, ln_b))
    return u @ w_out + b_out


def pool_mixer(u, w_pool, scale):
    seq = u.shape[1]
    u32 = u.astype(jnp.float32)
    cs = jnp.cumsum(u32, axis=1)
    t = jnp.arange(seq)
    parts = []
    for g, w in enumerate(POOL_WINDOWS):
        sl = slice(g * GC, (g + 1) * GC)
        c = cs[..., sl]
        prev = jnp.pad(c, ((0, 0), (w, 0), (0, 0)))[:, :seq]
        cnt = jnp.minimum(t + 1, w).astype(jnp.float32)[None, :, None]
        parts.append((c - prev) / cnt - u32[..., sl])
    p = jnp.stack(parts, axis=2).astype(u.dtype)
    y = jnp.einsum('bsgc,gcd->bsgd', p, w_pool)
    return y.reshape(u.shape[0], seq, W_C) * scale


def _fwd_setup_inputs(seed: int = 0) -> dict:
    key = jax.random.key(seed)
    ks = jax.random.split(key, 20)
    L, D = DEPTH, D_MODEL
    f32 = jnp.float32

    def nrm(k, shape, fan_in):
        return jax.random.normal(k, shape, f32) * (fan_in ** -0.5)

    def gain(k, shape):
        return 1.0 + 0.02 * jax.random.normal(k, shape, f32)

    return {
        "x": jax.random.normal(ks[0], (BATCH, SEQ, D), f32),
        "g_mix": gain(ks[1], (L, D)),
        "w_in": nrm(ks[2], (L, D, P_IN), D),
        "b_in": 0.02 * jax.random.normal(ks[3], (L, P_IN), f32),
        "conv_a": nrm(ks[4], (L, K_A, W_A), K_A),
        "w_out_a": nrm(ks[5], (L, W_A, D), W_A),
        "conv_b": nrm(ks[6], (L, K_B, W_B), K_B),
        "conv_b_bias": 0.02 * jax.random.normal(ks[7], (L, W_B), f32),
        "ln_b_g": gain(ks[8], (L, W_B)),
        "ln_b_b": 0.02 * jax.random.normal(ks[9], (L, W_B), f32),
        "w_out_b": nrm(ks[10], (L, W_B, D), W_B),
        "b_out_b": 0.02 * jax.random.normal(ks[11], (L, D), f32),
        "w_pool": nrm(ks[12], (L, N_POOL_GROUPS, GC, GC), GC),
        "pool_scale": gain(ks[13], (L, W_C)),
        "w_o": nrm(ks[14], (L, D, D), D),
        "g_mlp": gain(ks[15], (L, D)),
        "w_mlp1": nrm(ks[16], (L, D, D_FF), D),
        "w_mlp2": nrm(ks[17], (L, D_FF, D), D_FF),
        "g_final": gain(ks[18], (D,)),
    }


def _fwd_reference(x, g_mix, w_in, b_in, conv_a, w_out_a, conv_b, conv_b_bias, ln_b_g, ln_b_b,
              w_out_b, b_out_b, w_pool, pool_scale, w_o, g_mlp, w_mlp1, w_mlp2, g_final):
    bsz, seq, d = x.shape
    for l in range(DEPTH):
        h = rmsnorm(x, g_mix[l])
        proj = h @ w_in[l] + b_in[l]
        a_b, a_c, a_x, b_val, b_gate, c_in, gates = jnp.split(proj, SPLITS, axis=-1)
        y_a = short_conv_mixer(a_b, a_c, a_x, conv_a[l], w_out_a[l])
        y_b = conformer_conv_mixer(b_val, b_gate, conv_b[l], conv_b_bias[l], ln_b_g[l],
                                   ln_b_b[l], w_out_b[l], b_out_b[l])
        y_c = pool_mixer(c_in, w_pool[l], pool_scale[l])
        g = jax.nn.sigmoid(gates).reshape(bsz, seq, N_BRANCH, d)
        merged = g[:, :, 0] * y_a + g[:, :, 1] * y_b + g[:, :, 2] * y_c
        x = x + merged @ w_o[l]
        h = rmsnorm(x, g_mlp[l])
        x = x + jnp.square(jax.nn.relu(h @ w_mlp1[l])) @ w_mlp2[l]
    return rmsnorm(x, g_final)


import jax as _jax
import jax.numpy as _jnp

TWIN_FORMAT = 'train_step'
FWD_PARAMS = ['x', 'g_mix', 'w_in', 'b_in', 'conv_a', 'w_out_a', 'conv_b', 'conv_b_bias', 'ln_b_g', 'ln_b_b', 'w_out_b', 'b_out_b', 'w_pool', 'pool_scale', 'w_o', 'g_mlp', 'w_mlp1', 'w_mlp2', 'g_final']
TWIN_WEIGHTS = ['g_mix', 'w_in', 'b_in', 'conv_a', 'w_out_a', 'conv_b', 'conv_b_bias', 'ln_b_g', 'ln_b_b', 'w_out_b', 'b_out_b', 'w_pool', 'pool_scale', 'w_o', 'g_mlp', 'w_mlp1', 'w_mlp2', 'g_final']
TWIN_DIFF_INPUT = 'x'
TWIN_INPUTS = ['x', 'g_mix', 'w_in', 'b_in', 'conv_a', 'w_out_a', 'conv_b', 'conv_b_bias', 'ln_b_g', 'ln_b_b', 'w_out_b', 'b_out_b', 'w_pool', 'pool_scale', 'w_o', 'g_mlp', 'w_mlp1', 'w_mlp2', 'g_final', 'loss_target', 'm_g_mix', 'm_w_in', 'm_b_in', 'm_conv_a', 'm_w_out_a', 'm_conv_b', 'm_conv_b_bias', 'm_ln_b_g', 'm_ln_b_b', 'm_w_out_b', 'm_b_out_b', 'm_w_pool', 'm_pool_scale', 'm_w_o', 'm_g_mlp', 'm_w_mlp1', 'm_w_mlp2', 'm_g_final', 'v_g_mix', 'v_w_in', 'v_b_in', 'v_conv_a', 'v_w_out_a', 'v_conv_b', 'v_conv_b_bias', 'v_ln_b_g', 'v_ln_b_b', 'v_w_out_b', 'v_b_out_b', 'v_w_pool', 'v_pool_scale', 'v_w_o', 'v_g_mlp', 'v_w_mlp1', 'v_w_mlp2', 'v_g_final']
TWIN_OUTPUTS = ['loss', 'grad_x', 'grad_g_mix', 'grad_w_in', 'grad_b_in', 'grad_conv_a', 'grad_w_out_a', 'grad_conv_b', 'grad_conv_b_bias', 'grad_ln_b_g', 'grad_ln_b_b', 'grad_w_out_b', 'grad_b_out_b', 'grad_w_pool', 'grad_pool_scale', 'grad_w_o', 'grad_g_mlp', 'grad_w_mlp1', 'grad_w_mlp2', 'grad_g_final', 'delta_g_mix', 'delta_w_in', 'delta_b_in', 'delta_conv_a', 'delta_w_out_a', 'delta_conv_b', 'delta_conv_b_bias', 'delta_ln_b_g', 'delta_ln_b_b', 'delta_w_out_b', 'delta_b_out_b', 'delta_w_pool', 'delta_pool_scale', 'delta_w_o', 'delta_g_mlp', 'delta_w_mlp1', 'delta_w_mlp2', 'delta_g_final', 'new_m_g_mix', 'new_m_w_in', 'new_m_b_in', 'new_m_conv_a', 'new_m_w_out_a', 'new_m_conv_b', 'new_m_conv_b_bias', 'new_m_ln_b_g', 'new_m_ln_b_b', 'new_m_w_out_b', 'new_m_b_out_b', 'new_m_w_pool', 'new_m_pool_scale', 'new_m_w_o', 'new_m_g_mlp', 'new_m_w_mlp1', 'new_m_w_mlp2', 'new_m_g_final', 'new_v_g_mix', 'new_v_w_in', 'new_v_b_in', 'new_v_conv_a', 'new_v_w_out_a', 'new_v_conv_b', 'new_v_conv_b_bias', 'new_v_ln_b_g', 'new_v_ln_b_b', 'new_v_w_out_b', 'new_v_b_out_b', 'new_v_w_pool', 'new_v_pool_scale', 'new_v_w_o', 'new_v_g_mlp', 'new_v_w_mlp1', 'new_v_w_mlp2', 'new_v_g_final']
TWIN_LEAF_KINDS = {'loss': 'loss', 'grad_x': 'grad_x', 'grad_g_mix': 'grad_w', 'grad_w_in': 'grad_w', 'grad_b_in': 'grad_w', 'grad_conv_a': 'grad_w', 'grad_w_out_a': 'grad_w', 'grad_conv_b': 'grad_w', 'grad_conv_b_bias': 'grad_w', 'grad_ln_b_g': 'grad_w', 'grad_ln_b_b': 'grad_w', 'grad_w_out_b': 'grad_w', 'grad_b_out_b': 'grad_w', 'grad_w_pool': 'grad_w', 'grad_pool_scale': 'grad_w', 'grad_w_o': 'grad_w', 'grad_g_mlp': 'grad_w', 'grad_w_mlp1': 'grad_w', 'grad_w_mlp2': 'grad_w', 'grad_g_final': 'grad_w', 'delta_g_mix': 'delta_w', 'delta_w_in': 'delta_w', 'delta_b_in': 'delta_w', 'delta_conv_a': 'delta_w', 'delta_w_out_a': 'delta_w', 'delta_conv_b': 'delta_w', 'delta_conv_b_bias': 'delta_w', 'delta_ln_b_g': 'delta_w', 'delta_ln_b_b': 'delta_w', 'delta_w_out_b': 'delta_w', 'delta_b_out_b': 'delta_w', 'delta_w_pool': 'delta_w', 'delta_pool_scale': 'delta_w', 'delta_w_o': 'delta_w', 'delta_g_mlp': 'delta_w', 'delta_w_mlp1': 'delta_w', 'delta_w_mlp2': 'delta_w', 'delta_g_final': 'delta_w', 'new_m_g_mix': 'new_m', 'new_m_w_in': 'new_m', 'new_m_b_in': 'new_m', 'new_m_conv_a': 'new_m', 'new_m_w_out_a': 'new_m', 'new_m_conv_b': 'new_m', 'new_m_conv_b_bias': 'new_m', 'new_m_ln_b_g': 'new_m', 'new_m_ln_b_b': 'new_m', 'new_m_w_out_b': 'new_m', 'new_m_b_out_b': 'new_m', 'new_m_w_pool': 'new_m', 'new_m_pool_scale': 'new_m', 'new_m_w_o': 'new_m', 'new_m_g_mlp': 'new_m', 'new_m_w_mlp1': 'new_m', 'new_m_w_mlp2': 'new_m', 'new_m_g_final': 'new_m', 'new_v_g_mix': 'new_v', 'new_v_w_in': 'new_v', 'new_v_b_in': 'new_v', 'new_v_conv_a': 'new_v', 'new_v_w_out_a': 'new_v', 'new_v_conv_b': 'new_v', 'new_v_conv_b_bias': 'new_v', 'new_v_ln_b_g': 'new_v', 'new_v_ln_b_b': 'new_v', 'new_v_w_out_b': 'new_v', 'new_v_b_out_b': 'new_v', 'new_v_w_pool': 'new_v', 'new_v_pool_scale': 'new_v', 'new_v_w_o': 'new_v', 'new_v_g_mlp': 'new_v', 'new_v_w_mlp1': 'new_v', 'new_v_w_mlp2': 'new_v', 'new_v_g_final': 'new_v'}


def _forward(args):
    return _fwd_reference(*[args[k] for k in FWD_PARAMS])


def _output_shape():
    def fwd():
        inp = _fwd_setup_inputs(0)
        return _fwd_reference(*[inp[k] for k in FWD_PARAMS])
    out = _jax.eval_shape(fwd)
    return out.shape, out.dtype

N_MICROBATCH = 1
ADAM_LR = 0.001
ADAM_B1 = 0.9
ADAM_B2 = 0.999
ADAM_EPS = 1e-08
ADAM_WD = 0.01
ADAM_STEP = 10
PER_EXAMPLE_BATCH_AXIS = {'x': 0, 'loss_target': 0}
SHARED_INPUTS = []
_WEIGHT_DTYPES = {'g_mix': _jnp.float32, 'w_in': _jnp.float32, 'b_in': _jnp.float32, 'conv_a': _jnp.float32, 'w_out_a': _jnp.float32, 'conv_b': _jnp.float32, 'conv_b_bias': _jnp.float32, 'ln_b_g': _jnp.float32, 'ln_b_b': _jnp.float32, 'w_out_b': _jnp.float32, 'b_out_b': _jnp.float32, 'w_pool': _jnp.float32, 'pool_scale': _jnp.float32, 'w_o': _jnp.float32, 'g_mlp': _jnp.float32, 'w_mlp1': _jnp.float32, 'w_mlp2': _jnp.float32, 'g_final': _jnp.float32}
MOMENT_SCALE = {'g_mix': 2.263859e-01, 'w_in': 7.580916e-02, 'b_in': 6.922611e-02, 'conv_a': 1.076586e-01, 'w_out_a': 1.071850e-01, 'conv_b': 6.826318e-02, 'conv_b_bias': 1.451511e-01, 'ln_b_g': 8.456096e-02, 'ln_b_b': 8.843433e-02, 'w_out_b': 6.907655e-02, 'b_out_b': 1.624490e-01, 'w_pool': 9.325443e-02, 'pool_scale': 9.511304e-02, 'w_o': 1.581226e-01, 'g_mlp': 2.026550e-01, 'w_mlp1': 9.982021e-02, 'w_mlp2': 2.014732e-01, 'g_final': 6.482325e+01}


def _to_microbatches(a, axis):
    t = _jnp.moveaxis(a, axis, 0)
    t = t.reshape((N_MICROBATCH, t.shape[0] // N_MICROBATCH) + t.shape[1:])
    return _jnp.moveaxis(t, 1, axis + 1)


def setup_inputs(seed: int = 0) -> dict:
    inp = _fwd_setup_inputs(seed)
    key = _jax.random.fold_in(_jax.random.key(seed), 7919)
    shape, _ = _output_shape()
    out = dict(inp)
    out["loss_target"] = _jax.random.normal(_jax.random.fold_in(key, 0), shape, _jnp.float32)
    for i, name in enumerate(TWIN_WEIGHTS):
        w = inp[name].astype(_jnp.float32)
        if MOMENT_SCALE is None:
            s = _jnp.sqrt(_jnp.mean(_jnp.square(w)) + 1e-30)
        else:
            s = MOMENT_SCALE[name]
        km, kv = _jax.random.split(_jax.random.fold_in(key, i + 1))
        out[name] = w
        out["m_" + name] = s * _jax.random.normal(km, w.shape, _jnp.float32)
        out["v_" + name] = (s * s) * _jax.random.uniform(kv, w.shape, _jnp.float32, 0.5, 1.5)
    if N_MICROBATCH > 1:
        for name, axis in PER_EXAMPLE_BATCH_AXIS.items():
            out[name] = _to_microbatches(out[name], axis)
    return {'x': out['x'], 'g_mix': out['g_mix'], 'w_in': out['w_in'], 'b_in': out['b_in'], 'conv_a': out['conv_a'], 'w_out_a': out['w_out_a'], 'conv_b': out['conv_b'], 'conv_b_bias': out['conv_b_bias'], 'ln_b_g': out['ln_b_g'], 'ln_b_b': out['ln_b_b'], 'w_out_b': out['w_out_b'], 'b_out_b': out['b_out_b'], 'w_pool': out['w_pool'], 'pool_scale': out['pool_scale'], 'w_o': out['w_o'], 'g_mlp': out['g_mlp'], 'w_mlp1': out['w_mlp1'], 'w_mlp2': out['w_mlp2'], 'g_final': out['g_final'], 'loss_target': out['loss_target'], 'm_g_mix': out['m_g_mix'], 'm_w_in': out['m_w_in'], 'm_b_in': out['m_b_in'], 'm_conv_a': out['m_conv_a'], 'm_w_out_a': out['m_w_out_a'], 'm_conv_b': out['m_conv_b'], 'm_conv_b_bias': out['m_conv_b_bias'], 'm_ln_b_g': out['m_ln_b_g'], 'm_ln_b_b': out['m_ln_b_b'], 'm_w_out_b': out['m_w_out_b'], 'm_b_out_b': out['m_b_out_b'], 'm_w_pool': out['m_w_pool'], 'm_pool_scale': out['m_pool_scale'], 'm_w_o': out['m_w_o'], 'm_g_mlp': out['m_g_mlp'], 'm_w_mlp1': out['m_w_mlp1'], 'm_w_mlp2': out['m_w_mlp2'], 'm_g_final': out['m_g_final'], 'v_g_mix': out['v_g_mix'], 'v_w_in': out['v_w_in'], 'v_b_in': out['v_b_in'], 'v_conv_a': out['v_conv_a'], 'v_w_out_a': out['v_w_out_a'], 'v_conv_b': out['v_conv_b'], 'v_conv_b_bias': out['v_conv_b_bias'], 'v_ln_b_g': out['v_ln_b_g'], 'v_ln_b_b': out['v_ln_b_b'], 'v_w_out_b': out['v_w_out_b'], 'v_b_out_b': out['v_b_out_b'], 'v_w_pool': out['v_w_pool'], 'v_pool_scale': out['v_pool_scale'], 'v_w_o': out['v_w_o'], 'v_g_mlp': out['v_g_mlp'], 'v_w_mlp1': out['v_w_mlp1'], 'v_w_mlp2': out['v_w_mlp2'], 'v_g_final': out['v_g_final']}


def _loss(weights, diff, rest, loss_target):
    with _jax.named_scope("forward"):
        args = {**rest, TWIN_DIFF_INPUT: diff, **{k: w.astype(_WEIGHT_DTYPES[k]) for k, w in weights.items()}}
        y = _forward(args)
    with _jax.named_scope("loss_head"):
        err = _jnp.square(y.astype(_jnp.float32) - loss_target)
        return 0.5 * _jnp.sum(_jnp.mean(err, axis=-1)) if err.ndim else 0.5 * err


def _adamw(w, g, m, v):
    m = ADAM_B1 * m + (1.0 - ADAM_B1) * g
    v = ADAM_B2 * v + (1.0 - ADAM_B2) * _jnp.square(g)
    m_hat = m / (1.0 - ADAM_B1 ** ADAM_STEP)
    v_hat = v / (1.0 - ADAM_B2 ** ADAM_STEP)
    delta = -ADAM_LR * (m_hat / (_jnp.sqrt(v_hat) + ADAM_EPS) + ADAM_WD * w)
    return delta, m, v


def reference(x, g_mix, w_in, b_in, conv_a, w_out_a, conv_b, conv_b_bias, ln_b_g, ln_b_b, w_out_b, b_out_b, w_pool, pool_scale, w_o, g_mlp, w_mlp1, w_mlp2, g_final, loss_target, m_g_mix, m_w_in, m_b_in, m_conv_a, m_w_out_a, m_conv_b, m_conv_b_bias, m_ln_b_g, m_ln_b_b, m_w_out_b, m_b_out_b, m_w_pool, m_pool_scale, m_w_o, m_g_mlp, m_w_mlp1, m_w_mlp2, m_g_final, v_g_mix, v_w_in, v_b_in, v_conv_a, v_w_out_a, v_conv_b, v_conv_b_bias, v_ln_b_g, v_ln_b_b, v_w_out_b, v_b_out_b, v_w_pool, v_pool_scale, v_w_o, v_g_mlp, v_w_mlp1, v_w_mlp2, v_g_final):
    given = dict(x=x, g_mix=g_mix, w_in=w_in, b_in=b_in, conv_a=conv_a, w_out_a=w_out_a, conv_b=conv_b, conv_b_bias=conv_b_bias, ln_b_g=ln_b_g, ln_b_b=ln_b_b, w_out_b=w_out_b, b_out_b=b_out_b, w_pool=w_pool, pool_scale=pool_scale, w_o=w_o, g_mlp=g_mlp, w_mlp1=w_mlp1, w_mlp2=w_mlp2, g_final=g_final, loss_target=loss_target, m_g_mix=m_g_mix, m_w_in=m_w_in, m_b_in=m_b_in, m_conv_a=m_conv_a, m_w_out_a=m_w_out_a, m_conv_b=m_conv_b, m_conv_b_bias=m_conv_b_bias, m_ln_b_g=m_ln_b_g, m_ln_b_b=m_ln_b_b, m_w_out_b=m_w_out_b, m_b_out_b=m_b_out_b, m_w_pool=m_w_pool, m_pool_scale=m_pool_scale, m_w_o=m_w_o, m_g_mlp=m_g_mlp, m_w_mlp1=m_w_mlp1, m_w_mlp2=m_w_mlp2, m_g_final=m_g_final, v_g_mix=v_g_mix, v_w_in=v_w_in, v_b_in=v_b_in, v_conv_a=v_conv_a, v_w_out_a=v_w_out_a, v_conv_b=v_conv_b, v_conv_b_bias=v_conv_b_bias, v_ln_b_g=v_ln_b_g, v_ln_b_b=v_ln_b_b, v_w_out_b=v_w_out_b, v_b_out_b=v_b_out_b, v_w_pool=v_w_pool, v_pool_scale=v_pool_scale, v_w_o=v_w_o, v_g_mlp=v_g_mlp, v_w_mlp1=v_w_mlp1, v_w_mlp2=v_w_mlp2, v_g_final=v_g_final)
    weights = {n: given[n] for n in TWIN_WEIGHTS}
    shared = {n: given[n] for n in SHARED_INPUTS}
    per_example = {n: given[n] for n in ['x']}
    grad_fn = _jax.value_and_grad(_loss, argnums=(0, 1))

    def one_microbatch(ex, loss_target):
        ex = dict(ex)
        diff = ex.pop(TWIN_DIFF_INPUT)
        return grad_fn(weights, diff, {**shared, **ex}, loss_target)

    if N_MICROBATCH == 1:
        loss, (grad_w, grad_x) = one_microbatch(per_example, given["loss_target"])
    else:
        def body(carry, xs):
            loss_sum, grad_sum = carry
            l_k, (gw_k, gx_k) = one_microbatch(xs[0], xs[1])
            with _jax.named_scope("update"):
                return (loss_sum + l_k, _jax.tree.map(_jnp.add, grad_sum, gw_k)), gx_k

        init = (_jnp.zeros((), _jnp.float32), _jax.tree.map(_jnp.zeros_like, weights))
        (loss, grad_w), grad_x = _jax.lax.scan(body, init, (per_example, given["loss_target"]))
    with _jax.named_scope("update"):
        delta_w, new_m, new_v = {}, {}, {}
        for n in TWIN_WEIGHTS:
            delta_w[n], new_m[n], new_v[n] = _adamw(weights[n], grad_w[n], given["m_" + n], given["v_" + n])
    return (loss, grad_x, *[grad_w[n] for n in TWIN_WEIGHTS], *[delta_w[n] for n in TWIN_WEIGHTS],
            *[new_m[n] for n in TWIN_WEIGHTS], *[new_v[n] for n in TWIN_WEIGHTS])
```

```python
import functools

import jax
import jax.numpy as jnp
from jax import lax
from jax.experimental import pallas as pl
from jax.experimental.pallas import tpu as pltpu

F32 = jnp.float32
BF16 = jnp.bfloat16
EPS = 1e-6
HALO = 32
SUB = 32
K_A = 3
K_B = 31
POOL_WINDOWS = (2, 4, 8, 16)
N_GROUPS = len(POOL_WINDOWS)
N_CHIPS = 4
N_DEV = 8
ADAM_LR = 0.001
ADAM_B1 = 0.9
ADAM_B2 = 0.999
ADAM_EPS = 1e-08
ADAM_WD = 0.01
ADAM_STEP = 10
VMEM_LIMIT = 56 * 1024 * 1024
MESH = pl.DeviceIdType.MESH
ANY = pl.BlockSpec(memory_space=pl.ANY)


def _params(*sem):
    return pltpu.CompilerParams(dimension_semantics=sem, vmem_limit_bytes=VMEM_LIMIT)


def _tile(n, pref, mult=8):
    if n <= pref:
        return n
    t = (pref // mult) * mult
    while t > mult and n % t:
        t -= mult
    assert n % t == 0, (n, pref, mult)
    return t


def _dot(a, b):
    return jnp.dot(a, b, preferred_element_type=F32)


def _sigmoid(x):
    return 1.0 / (1.0 + jnp.exp(-x))


def _fold8(v):
    r, d = v.shape
    return v.reshape(r // 8, 8, d).sum(axis=0)


def _rms_bwd(dh, xv, g):
    r = lax.rsqrt(jnp.mean(xv * xv, axis=-1, keepdims=True) + EPS)
    xn = xv * r
    dxn = dh * g
    dx = r * (dxn - xn * jnp.mean(dxn * xn, axis=-1, keepdims=True))
    return dx, _fold8(dh * xn)


def _rms_fwd(x, g, name):
    s, d = x.shape
    tm = _tile(s, 512)

    def body(x_ref, g_ref, h_ref):
        xv = x_ref[...]
        r = lax.rsqrt(jnp.mean(xv * xv, axis=-1, keepdims=True) + EPS)
        h_ref[...] = (xv * r * g_ref[...]).astype(BF16)

    return pl.pallas_call(
        body, name=name, grid=(s // tm,),
        in_specs=[pl.BlockSpec((tm, d), lambda i: (i, 0)), pl.BlockSpec((1, d), lambda i: (0, 0))],
        out_specs=pl.BlockSpec((tm, d), lambda i: (i, 0)),
        out_shape=jax.ShapeDtypeStruct((s, d), BF16),
        compiler_params=_params("parallel"),
    )(x, g.reshape(1, d))


def _proj(h, w3, b, name):
    s, k = h.shape
    nb, _, nw = w3.shape
    tm = _tile(s, 1024)
    tn = nw // 3
    per = nw // tn

    def body(h_ref, w_ref, b_ref, o_ref):
        o_ref[...] = _dot(h_ref[...], w_ref[...]) + b_ref[...]

    return pl.pallas_call(
        body, name=name, grid=(s // tm, nb * per),
        in_specs=[
            pl.BlockSpec((tm, k), lambda i, j: (i, 0)),
            pl.BlockSpec((None, k, tn), lambda i, j: (j // per, 0, j % per)),
            pl.BlockSpec((1, tn), lambda i, j: (0, j)),
        ],
        out_specs=pl.BlockSpec((tm, tn), lambda i, j: (i, j)),
        out_shape=jax.ShapeDtypeStruct((s, nb * nw), F32),
        compiler_params=_params("parallel", "parallel"),
    )(h, w3, b)


def _prev_halo(tm, d, col):
    r = tm // HALO
    return pl.BlockSpec((HALO, d), lambda i: (jnp.maximum(i * r - 1, 0), col))


def _next_halo(tm, d, col, n_halo_blocks):
    r = tm // HALO
    return pl.BlockSpec((HALO, d), lambda i: (jnp.minimum((i + 1) * r, n_halo_blocks - 1), col))


def _mix_fwd(proj, conv_a, conv_b, cb, lg, lb, name):
    s, d9 = proj.shape
    d = d9 // 9
    gc = d // N_GROUPS
    tm = _tile(s, 256, HALO)

    def tile(col):
        return pl.BlockSpec((tm, d), lambda i: (i, col))

    def vec(rows):
        return pl.BlockSpec((rows, d), lambda i: (0, 0))

    def body(ab, ac, ax, bv, bg, ci, ac_h, ax_h, bv_h, bg_h, ci_h, wa, wb, cb_r, lg_r, lb_r,
             pa_o, s_o, p_o, v_o, ua_e, ub_e, ci_e):
        i = pl.program_id(0)
        keep = (i > 0).astype(F32)
        ua_e[0:HALO, :] = ac_h[...] * ax_h[...] * keep
        ua_e[HALO:HALO + tm, :] = ac[...] * ax[...]
        ub_e[0:HALO, :] = bv_h[...] * _sigmoid(bg_h[...]) * keep
        ub_e[HALO:HALO + tm, :] = bv[...] * _sigmoid(bg[...])
        ci_e[0:HALO, :] = ci_h[...] * keep
        ci_e[HALO:HALO + tm, :] = ci[...]
        for r0 in range(0, tm, SUB):
            rows = slice(r0, r0 + SUB)
            z = None
            for k in range(K_A):
                t = wa[k:k + 1, :] * ua_e[pl.ds(HALO - (K_A - 1) + k + r0, SUB), :]
                z = t if z is None else z + t
            pa_o[rows, :] = (ab[rows, :] * z).astype(BF16)
            v = None
            for k in range(K_B):
                t = wb[k:k + 1, :] * ub_e[pl.ds(HALO - (K_B - 1) + k + r0, SUB), :]
                v = t if v is None else v + t
            v = v + cb_r[...]
            v_o[rows, :] = v
            mu = jnp.mean(v, axis=-1, keepdims=True)
            vc = v - mu
            rstd = lax.rsqrt(jnp.mean(vc * vc, axis=-1, keepdims=True) + EPS)
            ln = vc * rstd * lg_r[...] + lb_r[...]
            s_o[rows, :] = (ln * _sigmoid(ln)).astype(BF16)
            t_idx = i * tm + r0 + lax.broadcasted_iota(jnp.int32, (SUB, gc), 0)
            for g, w in enumerate(POOL_WINDOWS):
                cols = slice(g * gc, (g + 1) * gc)
                cur = ci_e[pl.ds(HALO + r0, SUB), cols]
                acc = cur
                for j in range(1, w):
                    acc = acc + ci_e[pl.ds(HALO + r0 - j, SUB), cols]
                cnt = jnp.minimum(t_idx + 1, w).astype(F32)
                p_o[rows, cols] = (acc / cnt - cur).astype(BF16)

    return pl.pallas_call(
        body, name=name, grid=(s // tm,),
        in_specs=[tile(0), tile(1), tile(2), tile(3), tile(4), tile(5),
                  _prev_halo(tm, d, 1), _prev_halo(tm, d, 2), _prev_halo(tm, d, 3), _prev_halo(tm, d, 4),
                  _prev_halo(tm, d, 5),
                  vec(K_A), vec(K_B), vec(1), vec(1), vec(1)],
        out_specs=[pl.BlockSpec((tm, d), lambda i: (i, 0))] * 4,
        out_shape=[jax.ShapeDtypeStruct((s, d), BF16)] * 3 + [jax.ShapeDtypeStruct((s, d), F32)],
        scratch_shapes=[pltpu.VMEM((HALO + tm, d), F32)] * 3,
        compiler_params=_params("parallel"),
    )(proj, proj, proj, proj, proj, proj, proj, proj, proj, proj, proj,
      conv_a, conv_b, cb.reshape(1, d), lg.reshape(1, d), lb.reshape(1, d))


def _branch_out(pa, sb, p, proj, x0, woa, wob, wp, wo, bob, ps, gm, name):
    s, d = pa.shape
    gc = d // N_GROUPS
    tm = _tile(s, 256)

    def tile(col=0):
        return pl.BlockSpec((tm, d), lambda i: (i, col))

    def const(shape):
        return pl.BlockSpec(shape, lambda i: (0,) * len(shape))

    def body(pa_r, s_r, p_r, g0, g1, g2, x0_r, woa_r, wob_r, wp_r, wo_r, bob_r, ps_r, gm_r,
             ya_o, yb_o, yc_o, mg_o, x1_o, h2_o):
        ya = _dot(pa_r[...], woa_r[...])
        yb = _dot(s_r[...], wob_r[...]) + bob_r[...]
        for g in range(N_GROUPS):
            cols = slice(g * gc, (g + 1) * gc)
            yc_o[:, cols] = _dot(p_r[:, cols], wp_r[g])
        ya_o[...] = ya
        yb_o[...] = yb
        m = _sigmoid(g0[...]) * ya + _sigmoid(g1[...]) * yb + _sigmoid(g2[...]) * (yc_o[...] * ps_r[...])
        mb = m.astype(BF16)
        mg_o[...] = mb
        x1 = x0_r[...] + _dot(mb, wo_r[...])
        x1_o[...] = x1
        r = lax.rsqrt(jnp.mean(x1 * x1, axis=-1, keepdims=True) + EPS)
        h2_o[...] = (x1 * r * gm_r[...]).astype(BF16)

    return pl.pallas_call(
        body, name=name, grid=(s // tm,),
        in_specs=[tile(), tile(), tile(), tile(6), tile(7), tile(8), tile(),
                  const((d, d)), const((d, d)), const((N_GROUPS, gc, gc)), const((d, d)),
                  const((1, d)), const((1, d)), const((1, d))],
        out_specs=[tile()] * 6,
        out_shape=[jax.ShapeDtypeStruct((s, d), F32)] * 3 + [jax.ShapeDtypeStruct((s, d), BF16),
                                                              jax.ShapeDtypeStruct((s, d), F32),
                                                              jax.ShapeDtypeStruct((s, d), BF16)],
        compiler_params=_params("parallel"),
    )(pa, sb, p, proj, proj, proj, x0, woa, wob, wp, wo, bob.reshape(1, d), ps.reshape(1, d), gm.reshape(1, d))


def _mlp_fwd(h2, x1, w1_3, w2, name):
    s, d = h2.shape
    nf, _, tf = w1_3.shape
    tm = _tile(s, 512)

    def body(h_r, x1_r, w1_r, w2_r, f_o, a_o, x2_o, acc):
        j = pl.program_id(1)

        @pl.when(j == 0)
        def _():
            acc[...] = jnp.zeros_like(acc)

        f = _dot(h_r[...], w1_r[...])
        f_o[...] = f.astype(BF16)
        rl = jnp.maximum(f, 0.0)
        a = (rl * rl).astype(BF16)
        a_o[...] = a
        acc[...] += _dot(a, w2_r[...])

        @pl.when(j == nf - 1)
        def _():
            x2_o[...] = x1_r[...] + acc[...]

    return pl.pallas_call(
        body, name=name, grid=(s // tm, nf),
        in_specs=[pl.BlockSpec((tm, d), lambda i, j: (i, 0)), pl.BlockSpec((tm, d), lambda i, j: (i, 0)),
                  pl.BlockSpec((None, d, tf), lambda i, j: (j, 0, 0)), pl.BlockSpec((tf, d), lambda i, j: (j, 0))],
        out_specs=[pl.BlockSpec((tm, tf), lambda i, j: (i, j)), pl.BlockSpec((tm, tf), lambda i, j: (i, j)),
                   pl.BlockSpec((tm, d), lambda i, j: (i, 0))],
        out_shape=[jax.ShapeDtypeStruct((s, nf * tf), BF16), jax.ShapeDtypeStruct((s, nf * tf), BF16),
                   jax.ShapeDtypeStruct((s, d), F32)],
        scratch_shapes=[pltpu.VMEM((tm, d), F32)],
        compiler_params=_params("parallel", "arbitrary"),
    )(h2, x1, w1_3, w2)


def _loss_head(x, target, gf, name):
    s, d = x.shape
    tm = _tile(s, 512)
    n = s // tm

    def body(x_r, t_r, g_r, dx_o, loss_o, dg_o, lacc, gacc):
        i = pl.program_id(0)

        @pl.when(i == 0)
        def _():
            lacc[...] = jnp.zeros_like(lacc)
            gacc[...] = jnp.zeros_like(gacc)

        xv = x_r[...]
        r = lax.rsqrt(jnp.mean(xv * xv, axis=-1, keepdims=True) + EPS)
        xn = xv * r
        e = xn * g_r[...] - t_r[...]
        lacc[...] += _fold8(e * e)
        dy = e * (1.0 / d)
        gacc[...] += _fold8(dy * xn)
        dxn = dy * g_r[...]
        dx_o[...] = r * (dxn - xn * jnp.mean(dxn * xn, axis=-1, keepdims=True))

        @pl.when(i == n - 1)
        def _():
            loss_o[...] = jnp.sum(lacc[...]).reshape(1, 1) * (0.5 / d)
            dg_o[...] = jnp.sum(gacc[...], axis=0, keepdims=True)

    return pl.pallas_call(
        body, name=name, grid=(n,),
        in_specs=[pl.BlockSpec((tm, d), lambda i: (i, 0)), pl.BlockSpec((tm, d), lambda i: (i, 0)),
                  pl.BlockSpec((1, d), lambda i: (0, 0))],
        out_specs=[pl.BlockSpec((tm, d), lambda i: (i, 0)), pl.BlockSpec((1, 1), lambda i: (0, 0)),
                   pl.BlockSpec((1, d), lambda i: (0, 0))],
        out_shape=[jax.ShapeDtypeStruct((s, d), F32), jax.ShapeDtypeStruct((1, 1), F32),
                   jax.ShapeDtypeStruct((1, d), F32)],
        scratch_shapes=[pltpu.VMEM((8, d), F32), pltpu.VMEM((8, d), F32)],
        compiler_params=_params("arbitrary"),
    )(x, target, gf.reshape(1, d))


def _tn(a, b, nb, name):
    s, m = a.shape
    n = b.shape[1]
    nw = n // nb
    tmm = _tile(m, 1024, 128)
    tn = nw if nw <= 1024 else nw // 3
    per = nw // tn
    ts = _tile(s, 512)
    ns = s // ts

    def body(a_r, b_r, o_r, acc):
        k = pl.program_id(2)

        @pl.when(k == 0)
        def _():
            acc[...] = jnp.zeros_like(acc)

        acc[...] += lax.dot_general(a_r[...].astype(BF16), b_r[...].astype(BF16), (((0,), (0,)), ((), ())),
                                    preferred_element_type=F32)

        @pl.when(k == ns - 1)
        def _():
            o_r[...] = acc[...]

    return pl.pallas_call(
        body, name=name, grid=(m // tmm, nb * per, ns),
        in_specs=[pl.BlockSpec((ts, tmm), lambda i, j, k: (k, i)), pl.BlockSpec((ts, tn), lambda i, j, k: (k, j))],
        out_specs=pl.BlockSpec((None, tmm, tn), lambda i, j, k: (j // per, i, j % per)),
        out_shape=jax.ShapeDtypeStruct((nb, m, nw), F32),
        scratch_shapes=[pltpu.VMEM((tmm, tn), F32)],
        compiler_params=_params("parallel", "parallel", "arbitrary"),
    )(a, b)


def _tn_groups(a, b, name):
    s, d = a.shape
    gc = d // N_GROUPS
    ts = _tile(s, 512)
    ns = s // ts

    def body(a_r, b_r, o_r, acc):
        k = pl.program_id(1)

        @pl.when(k == 0)
        def _():
            acc[...] = jnp.zeros_like(acc)

        acc[...] += lax.dot_general(a_r[...], b_r[...], (((0,), (0,)), ((), ())), preferred_element_type=F32)

        @pl.when(k == ns - 1)
        def _():
            o_r[...] = acc[...]

    return pl.pallas_call(
        body, name=name, grid=(N_GROUPS, ns),
        in_specs=[pl.BlockSpec((ts, gc), lambda g, k: (k, g)), pl.BlockSpec((ts, gc), lambda g, k: (k, g))],
        out_specs=pl.BlockSpec((None, gc, gc), lambda g, k: (g, 0, 0)),
        out_shape=jax.ShapeDtypeStruct((N_GROUPS, gc, gc), F32),
        scratch_shapes=[pltpu.VMEM((gc, gc), F32)],
        compiler_params=_params("parallel", "arbitrary"),
    )(a, b)


def _mlp_bwd(dx2, f, x1, w2t, w1t, gm, name):
    s, d = dx2.shape
    ff = f.shape[1]
    tf = _tile(ff, 1024, 128)
    nf = ff // tf
    tm = _tile(s, 512)
    n = s // tm

    def body(dx2_r, f_r, x1_r, w2t_r, w1t_r, gm_r, df_o, dx1_o, dg_o, dxb, acc, gacc):
        i = pl.program_id(0)
        j = pl.program_id(1)

        @pl.when((i == 0) & (j == 0))
        def _():
            gacc[...] = jnp.zeros_like(gacc)

        @pl.when(j == 0)
        def _():
            dxb[...] = dx2_r[...].astype(BF16)
            acc[...] = jnp.zeros_like(acc)

        da = _dot(dxb[...], w2t_r[...])
        df = (da * (2.0 * jnp.maximum(f_r[...].astype(F32), 0.0))).astype(BF16)
        df_o[...] = df
        acc[...] += _dot(df, w1t_r[...])

        @pl.when(j == nf - 1)
        def _():
            dx, dg = _rms_bwd(acc[...], x1_r[...], gm_r[...])
            gacc[...] += dg
            dx1_o[...] = dx2_r[...] + dx

        @pl.when((i == n - 1) & (j == nf - 1))
        def _():
            dg_o[...] = jnp.sum(gacc[...], axis=0, keepdims=True)

    return pl.pallas_call(
        body, name=name, grid=(n, nf),
        in_specs=[pl.BlockSpec((tm, d), lambda i, j: (i, 0)), pl.BlockSpec((tm, tf), lambda i, j: (i, j)),
                  pl.BlockSpec((tm, d), lambda i, j: (i, 0)), pl.BlockSpec((d, tf), lambda i, j: (0, j)),
                  pl.BlockSpec((tf, d), lambda i, j: (j, 0)), pl.BlockSpec((1, d), lambda i, j: (0, 0))],
        out_specs=[pl.BlockSpec((tm, tf), lambda i, j: (i, j)), pl.BlockSpec((tm, d), lambda i, j: (i, 0)),
                   pl.BlockSpec((1, d), lambda i, j: (0, 0))],
        out_shape=[jax.ShapeDtypeStruct((s, ff), BF16), jax.ShapeDtypeStruct((s, d), F32),
                   jax.ShapeDtypeStruct((1, d), F32)],
        scratch_shapes=[pltpu.VMEM((tm, d), BF16), pltpu.VMEM((tm, d), F32), pltpu.VMEM((8, d), F32)],
        compiler_params=_params("arbitrary", "arbitrary"),
    )(dx2, f, x1, w2t, w1t, gm.reshape(1, d))


def _branch_out_bwd(dx1, proj, ya, yb, yc, wot, woat, wobt, wpt, ps, name):
    s, d = dx1.shape
    gc = d // N_GROUPS
    tm = _tile(s, 256)
    n = s // tm

    def tile(col=0):
        return pl.BlockSpec((tm, d), lambda i: (i, col))

    def const(shape):
        return pl.BlockSpec(shape, lambda i: (0,) * len(shape))

    def body(dx1_r, g0, g1, g2, ya_r, yb_r, yc_r, wot_r, woat_r, wobt_r, wpt_r, ps_r,
             dpa_o, ds_o, dp_o, dpj_o, dya_o, dyb_o, dyc_o, dbob_o, dps_o, dbg_o, sacc, gacc):
        i = pl.program_id(0)

        @pl.when(i == 0)
        def _():
            sacc[...] = jnp.zeros_like(sacc)
            gacc[...] = jnp.zeros_like(gacc)

        dm = _dot(dx1_r[...].astype(BF16), wot_r[...])
        ycp = yc_r[...]
        ys = (ya_r[...], yb_r[...], ycp * ps_r[...])
        dys = []
        for b, g_r in enumerate((g0, g1, g2)):
            sg = _sigmoid(g_r[...])
            dgt = dm * ys[b] * sg * (1.0 - sg)
            dpj_o[:, b * d:(b + 1) * d] = dgt.astype(BF16)
            gacc[b] += _fold8(dgt)
            dys.append(dm * sg)
        dya, dyb, dyc = dys
        sacc[0] += _fold8(dyb)
        sacc[1] += _fold8(dyc * ycp)
        dyab = dya.astype(BF16)
        dybb = dyb.astype(BF16)
        dycb = (dyc * ps_r[...]).astype(BF16)
        dya_o[...] = dyab
        dyb_o[...] = dybb
        dyc_o[...] = dycb
        dpa_o[...] = _dot(dyab, woat_r[...])
        ds_o[...] = _dot(dybb, wobt_r[...])
        for g in range(N_GROUPS):
            cols = slice(g * gc, (g + 1) * gc)
            dp_o[:, cols] = _dot(dycb[:, cols], wpt_r[g])

        @pl.when(i == n - 1)
        def _():
            dbob_o[...] = jnp.sum(sacc[0], axis=0, keepdims=True)
            dps_o[...] = jnp.sum(sacc[1], axis=0, keepdims=True)
            for b in range(3):
                dbg_o[:, b * d:(b + 1) * d] = jnp.sum(gacc[b], axis=0, keepdims=True)

    return pl.pallas_call(
        body, name=name, grid=(n,),
        in_specs=[tile(), tile(6), tile(7), tile(8), tile(), tile(), tile(),
                  const((d, d)), const((d, d)), const((d, d)), const((N_GROUPS, gc, gc)), const((1, d))],
        out_specs=[tile(), tile(), tile(), pl.BlockSpec((tm, 3 * d), lambda i: (i, 2)), tile(), tile(), tile(),
                   const((1, d)), const((1, d)), const((1, 3 * d))],
        out_shape=[jax.ShapeDtypeStruct((s, d), F32)] * 3 + [jax.ShapeDtypeStruct((s, 9 * d), BF16)]
        + [jax.ShapeDtypeStruct((s, d), BF16)] * 3
        + [jax.ShapeDtypeStruct((1, d), F32), jax.ShapeDtypeStruct((1, d), F32), jax.ShapeDtypeStruct((1, 3 * d), F32)],
        scratch_shapes=[pltpu.VMEM((2, 8, d), F32), pltpu.VMEM((3, 8, d), F32)],
        compiler_params=_params("arbitrary"),
    )(dx1, proj, proj, proj, ya, yb, yc, wot, woat, wobt, wpt, ps.reshape(1, d))


def _ln_silu_bwd(v, ds, lg, lb, name):
    s, d = v.shape
    tm = _tile(s, 256)
    n = s // tm

    def tile():
        return pl.BlockSpec((tm, d), lambda i: (i, 0))

    def vec():
        return pl.BlockSpec((1, d), lambda i: (0, 0))

    def body(v_r, ds_r, lg_r, lb_r, dv_o, dlg_o, dlb_o, dcb_o, acc):
        i = pl.program_id(0)

        @pl.when(i == 0)
        def _():
            acc[...] = jnp.zeros_like(acc)

        for r0 in range(0, tm, SUB):
            rows = slice(r0, r0 + SUB)
            vv = v_r[rows, :]
            mu = jnp.mean(vv, axis=-1, keepdims=True)
            vc = vv - mu
            rstd = lax.rsqrt(jnp.mean(vc * vc, axis=-1, keepdims=True) + EPS)
            nrm = vc * rstd
            ln = nrm * lg_r[...] + lb_r[...]
            sg = _sigmoid(ln)
            dln = ds_r[rows, :] * (sg * (1.0 + ln * (1.0 - sg)))
            acc[0] += _fold8(dln * nrm)
            acc[1] += _fold8(dln)
            dn = dln * lg_r[...]
            dv = rstd * (dn - jnp.mean(dn, axis=-1, keepdims=True)
                         - nrm * jnp.mean(dn * nrm, axis=-1, keepdims=True))
            acc[2] += _fold8(dv)
            dv_o[rows, :] = dv

        @pl.when(i == n - 1)
        def _():
            dlg_o[...] = jnp.sum(acc[0], axis=0, keepdims=True)
            dlb_o[...] = jnp.sum(acc[1], axis=0, keepdims=True)
            dcb_o[...] = jnp.sum(acc[2], axis=0, keepdims=True)

    return pl.pallas_call(
        body, name=name, grid=(n,),
        in_specs=[tile(), tile(), vec(), vec()],
        out_specs=[tile(), vec(), vec(), vec()],
        out_shape=[jax.ShapeDtypeStruct((s, d), F32)] + [jax.ShapeDtypeStruct((1, d), F32)] * 3,
        scratch_shapes=[pltpu.VMEM((3, 8, d), F32)],
        compiler_params=_params("arbitrary"),
    )(v, ds, lg.reshape(1, d), lb.reshape(1, d))


def _mix_bwd(proj, dpa, dv, dp, dproj, conv_a, conv_b, name):
    s, d9 = proj.shape
    d = d9 // 9
    gc = d // N_GROUPS
    tm = _tile(s, 256, HALO)
    n = s // tm
    nh = s // HALO

    def tile(col=0):
        return pl.BlockSpec((tm, d), lambda i: (i, col))

    def vec(rows):
        return pl.BlockSpec((rows, d), lambda i: (0, 0))

    def body(ab, ac, ax, bv, bg, ac_h, ax_h, bv_h, bg_h, dpa_r, dv_r, dp_r, ab_n, dpa_n, dv_n, dp_n, wa, wb, _alias,
             dpj_o, dbin_o, dca_o, dcb_o, ua_e, ub_e, dz_e, dv_e, q_e, bacc, cacc_a, cacc_b):
        i = pl.program_id(0)

        @pl.when(i == 0)
        def _():
            bacc[...] = jnp.zeros_like(bacc)
            cacc_a[...] = jnp.zeros_like(cacc_a)
            cacc_b[...] = jnp.zeros_like(cacc_b)

        keep_p = (i > 0).astype(F32)
        keep_n = (i < n - 1).astype(F32)
        ua_e[0:HALO, :] = ac_h[...] * ax_h[...] * keep_p
        ua_e[HALO:HALO + tm, :] = ac[...] * ax[...]
        ub_e[0:HALO, :] = bv_h[...] * _sigmoid(bg_h[...]) * keep_p
        ub_e[HALO:HALO + tm, :] = bv[...] * _sigmoid(bg[...])
        dz_e[0:tm, :] = dpa_r[...] * ab[...]
        dz_e[tm:tm + HALO, :] = dpa_n[...] * ab_n[...] * keep_n
        dv_e[0:tm, :] = dv_r[...]
        dv_e[tm:tm + HALO, :] = dv_n[...] * keep_n
        for g, w in enumerate(POOL_WINDOWS):
            cols = slice(g * gc, (g + 1) * gc)
            t_idx = i * tm + lax.broadcasted_iota(jnp.int32, (tm + HALO, gc), 0)
            cnt = jnp.minimum(t_idx + 1, w).astype(F32)
            q_e[0:tm, cols] = dp_r[:, cols] / cnt[0:tm]
            q_e[tm:tm + HALO, cols] = dp_n[:, cols] * keep_n / cnt[tm:tm + HALO]
        for r0 in range(0, tm, SUB):
            rows = slice(r0, r0 + SUB)
            dz = dz_e[pl.ds(r0, SUB), :]
            z = None
            du = None
            for k in range(K_A):
                u_k = ua_e[pl.ds(HALO - (K_A - 1) + k + r0, SUB), :]
                t = wa[k:k + 1, :] * u_k
                z = t if z is None else z + t
                cacc_a[k] += _fold8(dz * u_k)
                t = wa[k:k + 1, :] * dz_e[pl.ds(r0 + (K_A - 1) - k, SUB), :]
                du = t if du is None else du + t
            outs = [dpa_r[rows, :] * z, du * ax[rows, :], du * ac[rows, :]]
            dvv = dv_e[pl.ds(r0, SUB), :]
            du = None
            for k in range(K_B):
                cacc_b[k] += _fold8(dvv * ub_e[pl.ds(HALO - (K_B - 1) + k + r0, SUB), :])
                t = wb[k:k + 1, :] * dv_e[pl.ds(r0 + (K_B - 1) - k, SUB), :]
                du = t if du is None else du + t
            sg = _sigmoid(bg[rows, :])
            outs.append(du * sg)
            outs.append(du * bv[rows, :] * sg * (1.0 - sg))
            for b, o in enumerate(outs):
                dpj_o[rows, b * d:(b + 1) * d] = o.astype(BF16)
                bacc[b] += _fold8(o)
            for g, w in enumerate(POOL_WINDOWS):
                cols = slice(g * gc, (g + 1) * gc)
                acc = q_e[pl.ds(r0, SUB), cols]
                for j in range(1, w):
                    acc = acc + q_e[pl.ds(r0 + j, SUB), cols]
                o = acc - dp_r[rows, cols]
                dpj_o[rows, 5 * d + g * gc:5 * d + (g + 1) * gc] = o.astype(BF16)
                bacc[5, :, cols] += _fold8(o)

        @pl.when(i == n - 1)
        def _():
            for b in range(6):
                dbin_o[:, b * d:(b + 1) * d] = jnp.sum(bacc[b], axis=0, keepdims=True)
            dca_o[...] = jnp.sum(cacc_a[...], axis=1)
            dcb_o[...] = jnp.sum(cacc_b[...], axis=1)

    ka8 = 8
    kb8 = 32
    return pl.pallas_call(
        body, name=name, grid=(n,),
        in_specs=[tile(0), tile(1), tile(2), tile(3), tile(4),
                  _prev_halo(tm, d, 1), _prev_halo(tm, d, 2), _prev_halo(tm, d, 3), _prev_halo(tm, d, 4),
                  tile(), tile(), tile(),
                  _next_halo(tm, d, 0, nh), _next_halo(tm, d, 0, nh), _next_halo(tm, d, 0, nh), _next_halo(tm, d, 0, nh),
                  vec(K_A), vec(K_B), ANY],
        out_specs=[pl.BlockSpec((tm, 6 * d), lambda i: (i, 0)), pl.BlockSpec((1, 6 * d), lambda i: (0, 0)),
                   pl.BlockSpec((ka8, d), lambda i: (0, 0)), pl.BlockSpec((kb8, d), lambda i: (0, 0))],
        out_shape=[jax.ShapeDtypeStruct((s, 9 * d), BF16), jax.ShapeDtypeStruct((1, 6 * d), F32),
                   jax.ShapeDtypeStruct((ka8, d), F32), jax.ShapeDtypeStruct((kb8, d), F32)],
        scratch_shapes=[pltpu.VMEM((HALO + tm, d), F32), pltpu.VMEM((HALO + tm, d), F32),
                        pltpu.VMEM((tm + HALO, d), F32), pltpu.VMEM((tm + HALO, d), F32),
                        pltpu.VMEM((tm + HALO, d), F32),
                        pltpu.VMEM((6, 8, d), F32), pltpu.VMEM((ka8, 8, d), F32), pltpu.VMEM((kb8, 8, d), F32)],
        input_output_aliases={18: 0},
        compiler_params=_params("arbitrary"),
    )(proj, proj, proj, proj, proj, proj, proj, proj, proj, dpa, dv, dp, proj, dpa, dv, dp, conv_a, conv_b, dproj)


def _dh_rms_bwd(dproj, wint, x0, dx1, g, name):
    s, p = dproj.shape
    d = x0.shape[1]
    tm = _tile(s, 512)
    tk = _tile(p, 1536, 128)
    n = s // tm
    nk = p // tk

    def body(dp_r, w_r, x_r, dx1_r, g_r, dx_o, dg_o, acc, gacc):
        i = pl.program_id(0)
        k = pl.program_id(1)

        @pl.when((i == 0) & (k == 0))
        def _():
            gacc[...] = jnp.zeros_like(gacc)

        @pl.when(k == 0)
        def _():
            acc[...] = jnp.zeros_like(acc)

        acc[...] += _dot(dp_r[...], w_r[...])

        @pl.when(k == nk - 1)
        def _():
            dx, dg = _rms_bwd(acc[...], x_r[...], g_r[...])
            gacc[...] += dg
            dx_o[...] = dx1_r[...] + dx

        @pl.when((i == n - 1) & (k == nk - 1))
        def _():
            dg_o[...] = jnp.sum(gacc[...], axis=0, keepdims=True)

    return pl.pallas_call(
        body, name=name, grid=(n, nk),
        in_specs=[pl.BlockSpec((tm, tk), lambda i, k: (i, k)), pl.BlockSpec((tk, d), lambda i, k: (k, 0)),
                  pl.BlockSpec((tm, d), lambda i, k: (i, 0)), pl.BlockSpec((tm, d), lambda i, k: (i, 0)),
                  pl.BlockSpec((1, d), lambda i, k: (0, 0))],
        out_specs=[pl.BlockSpec((tm, d), lambda i, k: (i, 0)), pl.BlockSpec((1, d), lambda i, k: (0, 0))],
        out_shape=[jax.ShapeDtypeStruct((s, d), F32), jax.ShapeDtypeStruct((1, d), F32)],
        scratch_shapes=[pltpu.VMEM((tm, d), F32), pltpu.VMEM((8, d), F32)],
        compiler_params=_params("arbitrary", "arbitrary"),
    )(dproj, wint, x0, dx1, g.reshape(1, d))


def _layer_fwd(x0, w, l):
    h = _rms_fwd(x0, w["g_mix"], f"rms_fwd_{l}")
    proj = _proj(h, w["win3"], w["b_in"], f"proj_{l}")
    pa, sb, p, v = _mix_fwd(proj, w["conv_a"], w["conv_b"], w["conv_b_bias"], w["ln_b_g"], w["ln_b_b"], f"mix_fwd_{l}")
    ya, yb, yc, mg, x1, h2 = _branch_out(pa, sb, p, proj, x0, w["woa"], w["wob"], w["wp"], w["wo"], w["b_out_b"],
                                         w["pool_scale"], w["g_mlp"], f"branch_out_{l}")
    f, a, x2 = _mlp_fwd(h2, x1, w["w1_3"], w["w2"], f"mlp_fwd_{l}")
    saved = dict(x0=x0, h=h, proj=proj, pa=pa, sb=sb, p=p, v=v, ya=ya, yb=yb, yc=yc, mg=mg, x1=x1, h2=h2, f=f, a=a)
    return x2, saved


def _layer_bwd(dx2, w, sv, l):
    d = dx2.shape[1]
    df, dx1, dgm = _mlp_bwd(dx2, sv["f"], sv["x1"], w["w2t"], w["w1t"], w["g_mlp"], f"mlp_bwd_{l}")
    dw2 = _tn(sv["a"], dx2, 1, f"dw2_{l}")
    dw1 = _tn(sv["h2"], df, N_CHIPS, f"dw1_{l}")
    dpa, ds, dp, dproj, dya, dyb, dyc, dbob, dps, dbg = _branch_out_bwd(
        dx1, sv["proj"], sv["ya"], sv["yb"], sv["yc"], w["wot"], w["woat"], w["wobt"], w["wpt"], w["pool_scale"],
        f"branch_out_bwd_{l}")
    dwo = _tn(sv["mg"], dx1, 1, f"dwo_{l}")
    dwoa = _tn(sv["pa"], dya, 1, f"dwoa_{l}")
    dwob = _tn(sv["sb"], dyb, 1, f"dwob_{l}")
    dwp = _tn_groups(sv["p"], dyc, f"dwp_{l}")
    dv, dlg, dlb, dcb = _ln_silu_bwd(sv["v"], ds, w["ln_b_g"], w["ln_b_b"], f"ln_silu_bwd_{l}")
    dproj, dbm, dca, dcvb = _mix_bwd(sv["proj"], dpa, dv, dp, dproj, w["conv_a"], w["conv_b"], f"mix_bwd_{l}")
    dwin = _tn(sv["h"], dproj, N_CHIPS, f"dwin_{l}")
    dx0, dgmix = _dh_rms_bwd(dproj, w["wint"], sv["x0"], dx1, w["g_mix"], f"dh_{l}")
    ff = dw2.shape[1]
    gc = d // N_GROUPS
    big = dict(
        w_in=dwin,
        w_out_a=dwoa.reshape(N_CHIPS, d // N_CHIPS, d),
        w_out_b=dwob.reshape(N_CHIPS, d // N_CHIPS, d),
        w_pool=dwp.reshape(N_GROUPS, N_CHIPS, gc // N_CHIPS, gc).transpose(1, 0, 2, 3).reshape(N_CHIPS, gc, gc),
        w_o=dwo.reshape(N_CHIPS, d // N_CHIPS, d),
        w_mlp1=dw1,
        w_mlp2=dw2.reshape(N_CHIPS, ff // N_CHIPS, d),
    )
    small = dict(
        g_mix=dgmix, b_in=jnp.concatenate([dbm, dbg], axis=1).reshape(9, d), conv_a=dca[:K_A], conv_b=dcvb[:K_B],
        conv_b_bias=dcb, ln_b_g=dlg, ln_b_b=dlb, b_out_b=dbob, pool_scale=dps, g_mlp=dgm,
    )
    return dx0, big, small


def _local_step(x, target, ws, g_final):
    saved = []
    for l, w in enumerate(ws):
        x, sv = _layer_fwd(x, w, l)
        saved.append(sv)
    dx, loss, dgf = _loss_head(x, target, g_final, "loss_head")
    grads = [None] * len(ws)
    for l in reversed(range(len(ws))):
        dx, big, small = _layer_bwd(dx, ws[l], saved[l], l)
        grads[l] = (big, small)
    return loss, dx, grads, dgf


def _place():
    x, y, c = lax.axis_index("x"), lax.axis_index("y"), lax.axis_index("c")
    chips = [(1 - x, y), (x, 1 - y), (1 - x, 1 - y)]
    return x, y, c, chips


def _gather_shards(shards, name):
    n = len(shards)

    def body(*refs):
        ins, outs = refs[:n], refs[n:2 * n]
        send, recv, fsend, frecv, lsem = refs[2 * n:]
        x, y, c, chips = _place()
        me = 2 * x + y
        sib = (x, y, 1 - c)

        def half(a, which):
            h = ins[a].shape[0] // 2
            return pl.ds(pl.multiple_of(which * h, 8), h)

        def ici(a, j, chip, src_chip):
            return pltpu.make_async_remote_copy(
                src_ref=ins[a].at[half(a, c)], dst_ref=outs[a].at[src_chip, half(a, c)],
                send_sem=send.at[a * 3 + j], recv_sem=recv.at[a * 3 + j], device_id=(*chip, c), device_id_type=MESH)

        def d2d(a, j, src_chip, which):
            rows = outs[a].at[src_chip, half(a, which)]
            return pltpu.make_async_remote_copy(
                src_ref=rows, dst_ref=rows, send_sem=fsend.at[a * 3 + j], recv_sem=frecv.at[a * 3 + j],
                device_id=sib, device_id_type=MESH)

        local = [pltpu.make_async_copy(ins[a], outs[a].at[me], lsem.at[a]) for a in range(n)]
        for a in range(n):
            local[a].start()
            for j, chip in enumerate(chips):
                ici(a, j, chip, me).start()
        for a in range(n):
            for j, (px, py) in enumerate(chips):
                ici(a, j, (px, py), 2 * px + py).wait_recv()
                d2d(a, j, 2 * px + py, c).start()
        for a in range(n):
            for j, (px, py) in enumerate(chips):
                d2d(a, j, 2 * px + py, 1 - c).wait_recv()
        for a in range(n):
            local[a].wait()
            for j, (px, py) in enumerate(chips):
                ici(a, j, (px, py), me).wait_send()
                d2d(a, j, 2 * px + py, c).wait_send()

    return pl.pallas_call(
        body, name=name,
        in_specs=[ANY] * n, out_specs=[ANY] * n,
        out_shape=[jax.ShapeDtypeStruct((N_CHIPS, *a.shape), a.dtype) for a in shards],
        scratch_shapes=[pltpu.SemaphoreType.DMA((3 * n,))] * 4 + [pltpu.SemaphoreType.DMA((n,))],
    )(*shards)


def _gather_rows(v, name):
    def body(v_ref, o_ref, send, recv, lsem):
        x, y, c = lax.axis_index("x"), lax.axis_index("y"), lax.axis_index("c")
        me = 4 * x + 2 * y + c
        local = pltpu.make_async_copy(v_ref, o_ref.at[me], lsem)
        local.start()

        def copy(m, src_dev):
            peer = (x ^ ((m >> 2) & 1), y ^ ((m >> 1) & 1), c ^ (m & 1))
            return pltpu.make_async_remote_copy(
                src_ref=v_ref, dst_ref=o_ref.at[src_dev], send_sem=send.at[m - 1], recv_sem=recv.at[m - 1],
                device_id=peer, device_id_type=MESH)

        for m in range(1, N_DEV):
            copy(m, me).start()
        for m in range(1, N_DEV):
            copy(m, me ^ m).wait_recv()
        for m in range(1, N_DEV):
            copy(m, me).wait_send()
        local.wait()

    return pl.pallas_call(
        body, name=name, in_specs=[ANY], out_specs=ANY,
        out_shape=jax.ShapeDtypeStruct((N_DEV, *v.shape), v.dtype),
        scratch_shapes=[pltpu.SemaphoreType.DMA((N_DEV - 1,)), pltpu.SemaphoreType.DMA((N_DEV - 1,)),
                        pltpu.SemaphoreType.DMA],
    )(v)


def _swap_halves(grads, name):
    n = len(grads)

    def body(*refs):
        ins, outs = refs[:n], refs[n:2 * n]
        send, recv = refs[2 * n:]
        x, y, c, _ = _place()
        sib = (x, y, 1 - c)

        def copy(a, k):
            h = ins[a].shape[1] // 2
            return pltpu.make_async_remote_copy(
                src_ref=ins[a].at[k, pl.ds(pl.multiple_of((1 - c) * h, 8), h)], dst_ref=outs[a].at[k],
                send_sem=send.at[a * N_CHIPS + k], recv_sem=recv.at[a * N_CHIPS + k], device_id=sib, device_id_type=MESH)

        for a in range(n):
            for k in range(N_CHIPS):
                copy(a, k).start()
        for a in range(n):
            for k in range(N_CHIPS):
                copy(a, k).wait()

    return pl.pallas_call(
        body, name=name, in_specs=[ANY] * n, out_specs=[ANY] * n,
        out_shape=[jax.ShapeDtypeStruct((N_CHIPS, g.shape[1] // 2, g.shape[2]), g.dtype) for g in grads],
        scratch_shapes=[pltpu.SemaphoreType.DMA((N_CHIPS * n,))] * 2,
    )(*grads)


def _add_halves(g, r, c_arr, name):
    nk, rows, cols = g.shape
    h = rows // 2
    tr = _tile(h, max(8, (1 << 19) // cols), 8)
    nblk = h // tr

    def body(c_ref, g_r, r_r, o_r):
        o_r[...] = (g_r[...] + r_r[...]).astype(BF16)

    return pl.pallas_call(
        body, name=name,
        grid_spec=pltpu.PrefetchScalarGridSpec(
            num_scalar_prefetch=1, grid=(nk, nblk),
            in_specs=[pl.BlockSpec((None, tr, cols), lambda k, i, c_ref: (k, c_ref[0] * nblk + i, 0)),
                      pl.BlockSpec((None, tr, cols), lambda k, i, c_ref: (k, i, 0))],
            out_specs=pl.BlockSpec((None, tr, cols), lambda k, i, c_ref: (k, i, 0))),
        out_shape=jax.ShapeDtypeStruct((nk, h, cols), BF16),
        compiler_params=_params("parallel", "parallel"),
    )(c_arr, g, r)


def _exchange_chips(parts, name):
    n = len(parts)

    def body(*refs):
        ins, outs = refs[:n], refs[n:2 * n]
        send, recv, lsem = refs[2 * n:]
        x, y, c, chips = _place()
        me = 2 * x + y

        def copy(a, j, to_chip, src_chip):
            return pltpu.make_async_remote_copy(
                src_ref=ins[a].at[2 * to_chip[0] + to_chip[1]], dst_ref=outs[a].at[src_chip],
                send_sem=send.at[a * 3 + j], recv_sem=recv.at[a * 3 + j], device_id=(*to_chip, c), device_id_type=MESH)

        local = [pltpu.make_async_copy(ins[a].at[me], outs[a].at[me], lsem.at[a]) for a in range(n)]
        for a in range(n):
            local[a].start()
            for j, chip in enumerate(chips):
                copy(a, j, chip, me).start()
        for a in range(n):
            for j, (px, py) in enumerate(chips):
                copy(a, j, (px, py), 2 * px + py).wait_recv()
        for a in range(n):
            local[a].wait()
            for j, chip in enumerate(chips):
                copy(a, j, chip, me).wait_send()

    return pl.pallas_call(
        body, name=name, in_specs=[ANY] * n, out_specs=[ANY] * n,
        out_shape=[jax.ShapeDtypeStruct(p.shape, p.dtype) for p in parts],
        scratch_shapes=[pltpu.SemaphoreType.DMA((3 * n,))] * 2 + [pltpu.SemaphoreType.DMA((n,))],
    )(*parts)


def _sum_chips(r, name):
    nk, rows, cols = r.shape
    tr = _tile(rows, max(8, (1 << 19) // cols), 8)

    def body(r_r, o_r):
        acc = r_r[0].astype(F32)
        for k in range(1, nk):
            acc = acc + r_r[k].astype(F32)
        o_r[...] = acc

    return pl.pallas_call(
        body, name=name, grid=(rows // tr,),
        in_specs=[pl.BlockSpec((nk, tr, cols), lambda i: (0, i, 0))],
        out_specs=pl.BlockSpec((tr, cols), lambda i: (i, 0)),
        out_shape=jax.ShapeDtypeStruct((rows, cols), F32),
        compiler_params=_params("parallel"),
    )(r)


def _share_halves(halves, n_layers, name):
    flat = [hv for per_w in halves for hv in per_w]
    n = len(flat)
    nw = len(halves)

    def body(*refs):
        ins, outs = refs[:n], refs[n:n + nw]
        send, recv, lsem = refs[n + nw:]
        x, y, c, _ = _place()
        sib = (x, y, 1 - c)

        def rows(a, which):
            h = ins[a].shape[0]
            return outs[a // n_layers].at[a % n_layers, pl.ds(pl.multiple_of(which * h, 8), h)]

        local = [pltpu.make_async_copy(ins[a], rows(a, c), lsem.at[a]) for a in range(n)]
        remote = [pltpu.make_async_remote_copy(src_ref=ins[a], dst_ref=rows(a, c), send_sem=send.at[a],
                                               recv_sem=recv.at[a], device_id=sib, device_id_type=MESH)
                  for a in range(n)]
        for a in range(n):
            local[a].start()
            remote[a].start()
        for a in range(n):
            pltpu.make_async_remote_copy(src_ref=ins[a], dst_ref=rows(a, 1 - c), send_sem=send.at[a],
                                         recv_sem=recv.at[a], device_id=sib, device_id_type=MESH).wait_recv()
        for a in range(n):
            local[a].wait()
            remote[a].wait_send()

    return pl.pallas_call(
        body, name=name, in_specs=[ANY] * n, out_specs=[ANY] * nw,
        out_shape=[jax.ShapeDtypeStruct((n_layers, 2 * per_w[0].shape[0], per_w[0].shape[1]), F32) for per_w in halves],
        scratch_shapes=[pltpu.SemaphoreType.DMA((n,))] * 3,
    )(*flat)


def _adamw(w, g, m, v, name):
    rows, cols = w.shape
    tr = _tile(rows, max(8, (1 << 18) // cols), 8)
    c1 = 1.0 - ADAM_B1 ** ADAM_STEP
    c2 = 1.0 - ADAM_B2 ** ADAM_STEP

    def body(w_r, g_r, m_r, v_r, d_o, m_o, v_o):
        gv = g_r[...]
        mn = ADAM_B1 * m_r[...] + (1.0 - ADAM_B1) * gv
        vn = ADAM_B2 * v_r[...] + (1.0 - ADAM_B2) * (gv * gv)
        m_o[...] = mn
        v_o[...] = vn
        d_o[...] = -ADAM_LR * ((mn / c1) / (jnp.sqrt(vn / c2) + ADAM_EPS) + ADAM_WD * w_r[...])

    spec = pl.BlockSpec((tr, cols), lambda i: (i, 0))
    return pl.pallas_call(
        body, name=name, grid=(rows // tr,), in_specs=[spec] * 4, out_specs=[spec] * 3,
        out_shape=[jax.ShapeDtypeStruct((rows, cols), F32)] * 3,
        compiler_params=_params("parallel"),
    )(w, g, m, v)


BIG = ("w_in", "w_out_a", "w_out_b", "w_pool", "w_o", "w_mlp1", "w_mlp2")
SMALL = ("g_mix", "b_in", "conv_b_bias", "ln_b_g", "ln_b_b", "b_out_b", "pool_scale", "g_mlp")
CONVS = ("conv_a", "conv_b")
WEIGHTS = ("g_mix", "w_in", "b_in", "conv_a", "w_out_a", "conv_b", "conv_b_bias", "ln_b_g", "ln_b_b", "w_out_b", "b_out_b",
           "w_pool", "pool_scale", "w_o", "g_mlp", "w_mlp1", "w_mlp2", "g_final")


def _as2d(a):
    return a.reshape(-1, a.shape[-1])


def _pad_rows(a, rows):
    return jnp.pad(a, ((0, rows - a.shape[0]), (0, 0)))


def _full_weights(big_g, conv_g, rep, l, d):
    gc = d // N_GROUPS
    w = {k: v[l] for k, v in rep.items()}
    w["b_in"] = rep["b_in"][l].reshape(1, -1)
    win3 = big_g["w_in"][l]
    w["win3"] = win3
    w["wint"] = win3.transpose(0, 2, 1).reshape(-1, d)
    for src, dst in (("w_out_a", "woa"), ("w_out_b", "wob"), ("w_o", "wo")):
        full = big_g[src][l].reshape(d, d)
        w[dst] = full
        w[dst + "t"] = full.T
    wp = big_g["w_pool"][l].reshape(N_CHIPS, N_GROUPS, gc // N_CHIPS, gc).transpose(1, 0, 2, 3).reshape(N_GROUPS, gc, gc)
    w["wp"] = wp
    w["wpt"] = wp.transpose(0, 2, 1)
    w1_3 = big_g["w_mlp1"][l]
    w["w1_3"] = w1_3
    w["w1t"] = w1_3.transpose(0, 2, 1).reshape(-1, d)
    w2 = big_g["w_mlp2"][l].reshape(-1, d)
    w["w2"] = w2
    w["w2t"] = w2.T
    ca, cb = conv_g[l]
    w["conv_a"] = ca
    w["conv_b"] = cb
    return w


def kernel(x, g_mix, w_in, b_in, conv_a, w_out_a, conv_b, conv_b_bias, ln_b_g, ln_b_b, w_out_b, b_out_b, w_pool, pool_scale, w_o, g_mlp, w_mlp1, w_mlp2, g_final, loss_target, m_g_mix, m_w_in, m_b_in, m_conv_a, m_w_out_a, m_conv_b, m_conv_b_bias, m_ln_b_g, m_ln_b_b, m_w_out_b, m_b_out_b, m_w_pool, m_pool_scale, m_w_o, m_g_mlp, m_w_mlp1, m_w_mlp2, m_g_final, v_g_mix, v_w_in, v_b_in, v_conv_a, v_w_out_a, v_conv_b, v_conv_b_bias, v_ln_b_g, v_ln_b_b, v_w_out_b, v_b_out_b, v_w_pool, v_pool_scale, v_w_o, v_g_mlp, v_w_mlp1, v_w_mlp2, v_g_final):
    given = dict(g_mix=g_mix, w_in=w_in, b_in=b_in, conv_a=conv_a, w_out_a=w_out_a, conv_b=conv_b, conv_b_bias=conv_b_bias, ln_b_g=ln_b_g, ln_b_b=ln_b_b, w_out_b=w_out_b, b_out_b=b_out_b, w_pool=w_pool, pool_scale=pool_scale, w_o=w_o, g_mlp=g_mlp, w_mlp1=w_mlp1, w_mlp2=w_mlp2, g_final=g_final)
    mom = dict(g_mix=m_g_mix, w_in=m_w_in, b_in=m_b_in, conv_a=m_conv_a, w_out_a=m_w_out_a, conv_b=m_conv_b, conv_b_bias=m_conv_b_bias, ln_b_g=m_ln_b_g, ln_b_b=m_ln_b_b, w_out_b=m_w_out_b, b_out_b=m_b_out_b, w_pool=m_w_pool, pool_scale=m_pool_scale, w_o=m_w_o, g_mlp=m_g_mlp, w_mlp1=m_w_mlp1, w_mlp2=m_w_mlp2, g_final=m_g_final)
    var = dict(g_mix=v_g_mix, w_in=v_w_in, b_in=v_b_in, conv_a=v_conv_a, w_out_a=v_w_out_a, conv_b=v_conv_b, conv_b_bias=v_conv_b_bias, ln_b_g=v_ln_b_g, ln_b_b=v_ln_b_b, w_out_b=v_w_out_b, b_out_b=v_b_out_b, w_pool=v_w_pool, pool_scale=v_pool_scale, w_o=v_w_o, g_mlp=v_g_mlp, w_mlp1=v_w_mlp1, w_mlp2=v_w_mlp2, g_final=v_g_final)
    n_layers = w_in.shape[0]
    s, d = x.shape[1], x.shape[2]
    dq = d // N_CHIPS
    x_idx, y_idx, c_idx = lax.axis_index("x"), lax.axis_index("y"), lax.axis_index("c")
    chip = 2 * x_idx + y_idx
    c_arr = c_idx.astype(jnp.int32).reshape(1)

    conv_rows = n_layers * (K_A + K_B)
    conv_pad = -(-conv_rows // 16) * 16
    conv_pack = _pad_rows(jnp.concatenate([conv_a[l] for l in range(n_layers)] + [conv_b[l] for l in range(n_layers)], axis=0),
                          conv_pad)
    shards = [_as2d(given[k][l]).astype(BF16) for l in range(n_layers) for k in BIG] + [conv_pack]
    gathered = _gather_shards(shards, "gather_weights")
    big_g = {k: [gathered[l * len(BIG) + i] for l in range(n_layers)] for i, k in enumerate(BIG)}
    conv_full = gathered[-1].transpose(1, 0, 2).reshape(conv_pad, d)
    conv_g = [(conv_full[l * K_A:(l + 1) * K_A], conv_full[n_layers * K_A + l * K_B:n_layers * K_A + (l + 1) * K_B])
              for l in range(n_layers)]
    rep = {k: given[k] for k in SMALL}
    ws = [_full_weights(big_g, conv_g, rep, l, d) for l in range(n_layers)]

    loss, dx, grads, dgf = _local_step(x[0], loss_target[0], ws, g_final)
    loss = lax.psum(loss[0, 0], ("x", "y", "c"))

    big_list = [grads[l][0][k] for l in range(n_layers) for k in BIG]
    swapped = _swap_halves(big_list, "grad_swap_halves")
    parts = [_add_halves(g, r, c_arr, f"grad_add_halves_{a}") for a, (g, r) in enumerate(zip(big_list, swapped))]
    arrived = _exchange_chips(parts, "grad_exchange_chips")
    halves = [_sum_chips(r, f"grad_sum_chips_{a}") for a, r in enumerate(arrived)]
    per_w = [[halves[l * len(BIG) + i] for l in range(n_layers)] for i in range(len(BIG))]
    big_grad = dict(zip(BIG, _share_halves(per_w, n_layers, "grad_share_halves")))

    small_rows = []
    for k in SMALL + CONVS:
        for l in range(n_layers):
            small_rows.append(grads[l][1][k])
    small_rows.append(dgf)
    pack = jnp.concatenate(small_rows, axis=0)
    n_small = pack.shape[0]
    pack = _pad_rows(pack, -(-n_small // 8) * 8)
    small_sum = _sum_chips(_gather_rows(pack, "gather_small_grads"), "sum_small_grads")

    out = {}
    for k in BIG:
        shape = given[k].shape
        g2 = big_grad[k].reshape(-1, shape[-1])
        dlt, nm, nv = _adamw(given[k].reshape(g2.shape), g2, mom[k].reshape(g2.shape), var[k].reshape(g2.shape), f"adamw_{k}")
        out[k] = tuple(a.reshape(shape) for a in (g2, dlt, nm, nv))

    def rows_of(k):
        return {"b_in": 9, "conv_a": K_A, "conv_b": K_B}.get(k, 1)

    def pack_rep(src):
        rows = [src[k].reshape(n_layers * rows_of(k), d) for k in SMALL] + [src["g_final"].reshape(1, d)]
        a = jnp.concatenate(rows, axis=0)
        return _pad_rows(a, -(-a.shape[0] // 8) * 8)

    def pack_conv(src):
        a = jnp.concatenate([src[k].reshape(n_layers * rows_of(k), dq) for k in CONVS], axis=0)
        return _pad_rows(a, -(-a.shape[0] // 8) * 8)

    offs = {}
    r = 0
    for k in SMALL + CONVS:
        offs[k] = r
        r += n_layers * rows_of(k)
    offs["g_final"] = r
    n_rep = sum(n_layers * rows_of(k) for k in SMALL)
    g_rep = jnp.concatenate([small_sum[:n_rep], small_sum[offs["g_final"]:offs["g_final"] + 1]], axis=0)
    g_rep = _pad_rows(g_rep, -(-g_rep.shape[0] // 8) * 8)
    g_conv_full = small_sum[offs["conv_a"]:offs["g_final"]]
    g_conv = lax.dynamic_slice_in_dim(g_conv_full, chip * dq, dq, axis=1)
    g_conv = _pad_rows(g_conv, -(-g_conv.shape[0] // 8) * 8)
    rep_res = (g_rep,) + tuple(_adamw(pack_rep(given), g_rep, pack_rep(mom), pack_rep(var), "adamw_small"))
    conv_res = (g_conv,) + tuple(_adamw(pack_conv(given), g_conv, pack_conv(mom), pack_conv(var), "adamw_conv"))
    r = 0
    for k in SMALL:
        nr = n_layers * rows_of(k)
        out[k] = tuple(a[r:r + nr].reshape(given[k].shape) for a in rep_res)
        r += nr
    out["g_final"] = tuple(a[r].reshape(given["g_final"].shape) for a in rep_res)
    r = 0
    for k in CONVS:
        nr = n_layers * rows_of(k)
        out[k] = tuple(a[r:r + nr].reshape(given[k].shape) for a in conv_res)
        r += nr

    res = [loss, dx.reshape(x.shape)]
    for i in range(4):
        res += [out[k][i] for k in WEIGHTS]
    return tuple(res)
```

```python
import functools

import jax
import jax.numpy as jnp
from jax import lax
from jax.experimental import pallas as pl
from jax.experimental.pallas import tpu as pltpu

F32 = jnp.float32
BF16 = jnp.bfloat16
EPS = 1e-6
HALO = 32
SUB = 32
K_A = 3
K_B = 31
POOL_WINDOWS = (2, 4, 8, 16)
N_GROUPS = len(POOL_WINDOWS)
N_CHIPS = 4
N_DEV = 8
ADAM_LR = 0.001
ADAM_B1 = 0.9
ADAM_B2 = 0.999
ADAM_EPS = 1e-08
ADAM_WD = 0.01
ADAM_STEP = 10
VMEM_LIMIT = 56 * 1024 * 1024
MESH = pl.DeviceIdType.MESH
ANY = pl.BlockSpec(memory_space=pl.ANY)


def _params(*sem):
    return pltpu.CompilerParams(dimension_semantics=sem, vmem_limit_bytes=VMEM_LIMIT)


def _tile(n, pref, mult=8):
    if n <= pref:
        return n
    t = (pref // mult) * mult
    while t > mult and n % t:
        t -= mult
    assert n % t == 0, (n, pref, mult)
    return t


def _dot(a, b):
    return jnp.dot(a, b, preferred_element_type=F32)


def _sigmoid(x):
    return 1.0 / (1.0 + jnp.exp(-x))


def _fold8(v):
    r, d = v.shape
    return v.reshape(r // 8, 8, d).sum(axis=0)


def _rms_bwd(dh, xv, g):
    r = lax.rsqrt(jnp.mean(xv * xv, axis=-1, keepdims=True) + EPS)
    xn = xv * r
    dxn = dh * g
    dx = r * (dxn - xn * jnp.mean(dxn * xn, axis=-1, keepdims=True))
    return dx, _fold8(dh * xn)


def _rms_fwd(x, g, name):
    s, d = x.shape
    tm = _tile(s, 512)

    def body(x_ref, g_ref, h_ref):
        xv = x_ref[...]
        r = lax.rsqrt(jnp.mean(xv * xv, axis=-1, keepdims=True) + EPS)
        h_ref[...] = (xv * r * g_ref[...]).astype(BF16)

    return pl.pallas_call(
        body, name=name, grid=(s // tm,),
        in_specs=[pl.BlockSpec((tm, d), lambda i: (i, 0)), pl.BlockSpec((1, d), lambda i: (0, 0))],
        out_specs=pl.BlockSpec((tm, d), lambda i: (i, 0)),
        out_shape=jax.ShapeDtypeStruct((s, d), BF16),
        compiler_params=_params("parallel"),
    )(x, g.reshape(1, d))


def _proj(h, w3, b, name):
    s, k = h.shape
    nb, _, nw = w3.shape
    tm = _tile(s, 1024)
    tn = nw // 3
    per = nw // tn

    def body(h_ref, w_ref, b_ref, o_ref):
        o_ref[...] = _dot(h_ref[...], w_ref[...]) + b_ref[...]

    return pl.pallas_call(
        body, name=name, grid=(s // tm, nb * per),
        in_specs=[
            pl.BlockSpec((tm, k), lambda i, j: (i, 0)),
            pl.BlockSpec((None, k, tn), lambda i, j: (j // per, 0, j % per)),
            pl.BlockSpec((1, tn), lambda i, j: (0, j)),
        ],
        out_specs=pl.BlockSpec((tm, tn), lambda i, j: (i, j)),
        out_shape=jax.ShapeDtypeStruct((s, nb * nw), F32),
        compiler_params=_params("parallel", "parallel"),
    )(h, w3, b)


def _shift_copies(src_e, dst_s):
    n_rows = dst_s.shape[1]
    for b in range(1, 8):
        dst_s[b - 1, :, :] = src_e[pl.ds(b, n_rows), :]


def _window(src_e, dst_s, off, rows):
    b = off % 8
    if b == 0:
        return src_e[pl.ds(off, rows), :]
    return dst_s[b - 1, pl.ds(off - b, rows), :]


def _prev_halo(tm, d, col):
    r = tm // HALO
    return pl.BlockSpec((HALO, d), lambda i: (jnp.maximum(i * r - 1, 0), col))


def _next_halo(tm, d, col, n_halo_blocks):
    r = tm // HALO
    return pl.BlockSpec((HALO, d), lambda i: (jnp.minimum((i + 1) * r, n_halo_blocks - 1), col))


def _mix_fwd(proj, conv_a, conv_b, cb, lg, lb, name):
    s, d9 = proj.shape
    d = d9 // 9
    gc = d // N_GROUPS
    tm = _tile(s, 256, HALO)

    def tile(col):
        return pl.BlockSpec((tm, d), lambda i: (i, col))

    def vec(rows):
        return pl.BlockSpec((rows, d), lambda i: (0, 0))

    def body(ab, ac, ax, bv, bg, ci, ac_h, ax_h, bv_h, bg_h, ci_h, wa, wb, cb_r, lg_r, lb_r,
             pa_o, s_o, p_o, v_o, ua_e, ub_e, ci_e, ub_s):
        i = pl.program_id(0)
        keep = (i > 0).astype(F32)
        ua_e[0:HALO, :] = ac_h[...] * ax_h[...] * keep
        ua_e[HALO:HALO + tm, :] = ac[...] * ax[...]
        ub_e[0:HALO, :] = bv_h[...] * _sigmoid(bg_h[...]) * keep
        ub_e[HALO:HALO + tm, :] = bv[...] * _sigmoid(bg[...])
        ci_e[0:HALO, :] = ci_h[...] * keep
        ci_e[HALO:HALO + tm, :] = ci[...]
        _shift_copies(ub_e, ub_s)
        for r0 in range(0, tm, SUB):
            rows = slice(r0, r0 + SUB)
            z = None
            for k in range(K_A):
                t = wa[k:k + 1, :] * ua_e[pl.ds(HALO - (K_A - 1) + k + r0, SUB), :]
                z = t if z is None else z + t
            pa_o[rows, :] = (ab[rows, :] * z).astype(BF16)
            v = None
            for k in range(K_B):
                t = wb[k:k + 1, :] * _window(ub_e, ub_s, HALO - (K_B - 1) + k + r0, SUB)
                v = t if v is None else v + t
            v = v + cb_r[...]
            v_o[rows, :] = v
            mu = jnp.mean(v, axis=-1, keepdims=True)
            vc = v - mu
            rstd = lax.rsqrt(jnp.mean(vc * vc, axis=-1, keepdims=True) + EPS)
            ln = vc * rstd * lg_r[...] + lb_r[...]
            s_o[rows, :] = (ln * _sigmoid(ln)).astype(BF16)
            t_idx = i * tm + r0 + lax.broadcasted_iota(jnp.int32, (SUB, gc), 0)
            for g, w in enumerate(POOL_WINDOWS):
                cols = slice(g * gc, (g + 1) * gc)
                cur = ci_e[pl.ds(HALO + r0, SUB), cols]
                acc = cur
                for j in range(1, w):
                    acc = acc + ci_e[pl.ds(HALO + r0 - j, SUB), cols]
                cnt = jnp.minimum(t_idx + 1, w).astype(F32)
                p_o[rows, cols] = (acc / cnt - cur).astype(BF16)

    return pl.pallas_call(
        body, name=name, grid=(s // tm,),
        in_specs=[tile(0), tile(1), tile(2), tile(3), tile(4), tile(5),
                  _prev_halo(tm, d, 1), _prev_halo(tm, d, 2), _prev_halo(tm, d, 3), _prev_halo(tm, d, 4),
                  _prev_halo(tm, d, 5),
                  vec(K_A), vec(K_B), vec(1), vec(1), vec(1)],
        out_specs=[pl.BlockSpec((tm, d), lambda i: (i, 0))] * 4,
        out_shape=[jax.ShapeDtypeStruct((s, d), BF16)] * 3 + [jax.ShapeDtypeStruct((s, d), F32)],
        scratch_shapes=[pltpu.VMEM((HALO + tm, d), F32)] * 3 + [pltpu.VMEM((7, tm + HALO - 8, d), F32)],
        compiler_params=_params("parallel"),
    )(proj, proj, proj, proj, proj, proj, proj, proj, proj, proj, proj,
      conv_a, conv_b, cb.reshape(1, d), lg.reshape(1, d), lb.reshape(1, d))


def _branch_out(pa, sb, p, proj, x0, woa, wob, wp, wo, bob, ps, gm, name):
    s, d = pa.shape
    gc = d // N_GROUPS
    tm = _tile(s, 256)

    def tile(col=0):
        return pl.BlockSpec((tm, d), lambda i: (i, col))

    def const(shape):
        return pl.BlockSpec(shape, lambda i: (0,) * len(shape))

    def body(pa_r, s_r, p_r, g0, g1, g2, x0_r, woa_r, wob_r, wp_r, wo_r, bob_r, ps_r, gm_r,
             ya_o, yb_o, yc_o, mg_o, x1_o, h2_o):
        ya = _dot(pa_r[...], woa_r[...])
        yb = _dot(s_r[...], wob_r[...]) + bob_r[...]
        for g in range(N_GROUPS):
            cols = slice(g * gc, (g + 1) * gc)
            yc_o[:, cols] = _dot(p_r[:, cols], wp_r[g])
        ya_o[...] = ya
        yb_o[...] = yb
        m = _sigmoid(g0[...]) * ya + _sigmoid(g1[...]) * yb + _sigmoid(g2[...]) * (yc_o[...] * ps_r[...])
        mb = m.astype(BF16)
        mg_o[...] = mb
        x1 = x0_r[...] + _dot(mb, wo_r[...])
        x1_o[...] = x1
        r = lax.rsqrt(jnp.mean(x1 * x1, axis=-1, keepdims=True) + EPS)
        h2_o[...] = (x1 * r * gm_r[...]).astype(BF16)

    return pl.pallas_call(
        body, name=name, grid=(s // tm,),
        in_specs=[tile(), tile(), tile(), tile(6), tile(7), tile(8), tile(),
                  const((d, d)), const((d, d)), const((N_GROUPS, gc, gc)), const((d, d)),
                  const((1, d)), const((1, d)), const((1, d))],
        out_specs=[tile()] * 6,
        out_shape=[jax.ShapeDtypeStruct((s, d), F32)] * 3 + [jax.ShapeDtypeStruct((s, d), BF16),
                                                              jax.ShapeDtypeStruct((s, d), F32),
                                                              jax.ShapeDtypeStruct((s, d), BF16)],
        compiler_params=_params("parallel"),
    )(pa, sb, p, proj, proj, proj, x0, woa, wob, wp, wo, bob.reshape(1, d), ps.reshape(1, d), gm.reshape(1, d))


def _mlp_fwd(h2, x1, w1_3, w2, name):
    s, d = h2.shape
    nf, _, tf = w1_3.shape
    tm = _tile(s, 512)

    def body(h_r, x1_r, w1_r, w2_r, f_o, a_o, x2_o, acc):
        j = pl.program_id(1)

        @pl.when(j == 0)
        def _():
            acc[...] = jnp.zeros_like(acc)

        f = _dot(h_r[...], w1_r[...])
        f_o[...] = f.astype(BF16)
        rl = jnp.maximum(f, 0.0)
        a = (rl * rl).astype(BF16)
        a_o[...] = a
        acc[...] += _dot(a, w2_r[...])

        @pl.when(j == nf - 1)
        def _():
            x2_o[...] = x1_r[...] + acc[...]

    return pl.pallas_call(
        body, name=name, grid=(s // tm, nf),
        in_specs=[pl.BlockSpec((tm, d), lambda i, j: (i, 0)), pl.BlockSpec((tm, d), lambda i, j: (i, 0)),
                  pl.BlockSpec((None, d, tf), lambda i, j: (j, 0, 0)), pl.BlockSpec((tf, d), lambda i, j: (j, 0))],
        out_specs=[pl.BlockSpec((tm, tf), lambda i, j: (i, j)), pl.BlockSpec((tm, tf), lambda i, j: (i, j)),
                   pl.BlockSpec((tm, d), lambda i, j: (i, 0))],
        out_shape=[jax.ShapeDtypeStruct((s, nf * tf), BF16), jax.ShapeDtypeStruct((s, nf * tf), BF16),
                   jax.ShapeDtypeStruct((s, d), F32)],
        scratch_shapes=[pltpu.VMEM((tm, d), F32)],
        compiler_params=_params("parallel", "arbitrary"),
    )(h2, x1, w1_3, w2)


def _loss_head(x, target, gf, name):
    s, d = x.shape
    tm = _tile(s, 512)
    n = s // tm

    def body(x_r, t_r, g_r, dx_o, loss_o, dg_o, lacc, gacc):
        i = pl.program_id(0)

        @pl.when(i == 0)
        def _():
            lacc[...] = jnp.zeros_like(lacc)
            gacc[...] = jnp.zeros_like(gacc)

        xv = x_r[...]
        r = lax.rsqrt(jnp.mean(xv * xv, axis=-1, keepdims=True) + EPS)
        xn = xv * r
        e = xn * g_r[...] - t_r[...]
        lacc[...] += _fold8(e * e)
        dy = e * (1.0 / d)
        gacc[...] += _fold8(dy * xn)
        dxn = dy * g_r[...]
        dx_o[...] = r * (dxn - xn * jnp.mean(dxn * xn, axis=-1, keepdims=True))

        @pl.when(i == n - 1)
        def _():
            loss_o[...] = jnp.sum(lacc[...]).reshape(1, 1) * (0.5 / d)
            dg_o[...] = jnp.sum(gacc[...], axis=0, keepdims=True)

    return pl.pallas_call(
        body, name=name, grid=(n,),
        in_specs=[pl.BlockSpec((tm, d), lambda i: (i, 0)), pl.BlockSpec((tm, d), lambda i: (i, 0)),
                  pl.BlockSpec((1, d), lambda i: (0, 0))],
        out_specs=[pl.BlockSpec((tm, d), lambda i: (i, 0)), pl.BlockSpec((1, 1), lambda i: (0, 0)),
                   pl.BlockSpec((1, d), lambda i: (0, 0))],
        out_shape=[jax.ShapeDtypeStruct((s, d), F32), jax.ShapeDtypeStruct((1, 1), F32),
                   jax.ShapeDtypeStruct((1, d), F32)],
        scratch_shapes=[pltpu.VMEM((8, d), F32), pltpu.VMEM((8, d), F32)],
        compiler_params=_params("arbitrary"),
    )(x, target, gf.reshape(1, d))


def _tn(a, b, nb, name):
    s, m = a.shape
    n = b.shape[1]
    nw = n // nb
    tmm = _tile(m, 1024, 128)
    tn = nw if nw <= 1024 else nw // 3
    per = nw // tn
    ts = _tile(s, 512)
    ns = s // ts

    def body(a_r, b_r, o_r, acc):
        k = pl.program_id(2)

        @pl.when(k == 0)
        def _():
            acc[...] = jnp.zeros_like(acc)

        acc[...] += lax.dot_general(a_r[...].astype(BF16), b_r[...].astype(BF16), (((0,), (0,)), ((), ())),
                                    preferred_element_type=F32)

        @pl.when(k == ns - 1)
        def _():
            o_r[...] = acc[...]

    return pl.pallas_call(
        body, name=name, grid=(m // tmm, nb * per, ns),
        in_specs=[pl.BlockSpec((ts, tmm), lambda i, j, k: (k, i)), pl.BlockSpec((ts, tn), lambda i, j, k: (k, j))],
        out_specs=pl.BlockSpec((None, tmm, tn), lambda i, j, k: (j // per, i, j % per)),
        out_shape=jax.ShapeDtypeStruct((nb, m, nw), F32),
        scratch_shapes=[pltpu.VMEM((tmm, tn), F32)],
        compiler_params=_params("parallel", "parallel", "arbitrary"),
    )(a, b)


def _tn_groups(a, b, name):
    s, d = a.shape
    gc = d // N_GROUPS
    ts = _tile(s, 512)
    ns = s // ts

    def body(a_r, b_r, o_r, acc):
        k = pl.program_id(1)

        @pl.when(k == 0)
        def _():
            acc[...] = jnp.zeros_like(acc)

        acc[...] += lax.dot_general(a_r[...], b_r[...], (((0,), (0,)), ((), ())), preferred_element_type=F32)

        @pl.when(k == ns - 1)
        def _():
            o_r[...] = acc[...]

    return pl.pallas_call(
        body, name=name, grid=(N_GROUPS, ns),
        in_specs=[pl.BlockSpec((ts, gc), lambda g, k: (k, g)), pl.BlockSpec((ts, gc), lambda g, k: (k, g))],
        out_specs=pl.BlockSpec((None, gc, gc), lambda g, k: (g, 0, 0)),
        out_shape=jax.ShapeDtypeStruct((N_GROUPS, gc, gc), F32),
        scratch_shapes=[pltpu.VMEM((gc, gc), F32)],
        compiler_params=_params("parallel", "arbitrary"),
    )(a, b)


def _mlp_bwd(dx2, f, x1, w2t, w1t, gm, name):
    s, d = dx2.shape
    ff = f.shape[1]
    tf = _tile(ff, 1024, 128)
    nf = ff // tf
    tm = _tile(s, 512)
    n = s // tm

    def body(dx2_r, f_r, x1_r, w2t_r, w1t_r, gm_r, df_o, dx1_o, dg_o, dxb, acc, gacc):
        i = pl.program_id(0)
        j = pl.program_id(1)

        @pl.when((i == 0) & (j == 0))
        def _():
            gacc[...] = jnp.zeros_like(gacc)

        @pl.when(j == 0)
        def _():
            dxb[...] = dx2_r[...].astype(BF16)
            acc[...] = jnp.zeros_like(acc)

        da = _dot(dxb[...], w2t_r[...])
        df = (da * (2.0 * jnp.maximum(f_r[...].astype(F32), 0.0))).astype(BF16)
        df_o[...] = df
        acc[...] += _dot(df, w1t_r[...])

        @pl.when(j == nf - 1)
        def _():
            dx, dg = _rms_bwd(acc[...], x1_r[...], gm_r[...])
            gacc[...] += dg
            dx1_o[...] = dx2_r[...] + dx

        @pl.when((i == n - 1) & (j == nf - 1))
        def _():
            dg_o[...] = jnp.sum(gacc[...], axis=0, keepdims=True)

    return pl.pallas_call(
        body, name=name, grid=(n, nf),
        in_specs=[pl.BlockSpec((tm, d), lambda i, j: (i, 0)), pl.BlockSpec((tm, tf), lambda i, j: (i, j)),
                  pl.BlockSpec((tm, d), lambda i, j: (i, 0)), pl.BlockSpec((d, tf), lambda i, j: (0, j)),
                  pl.BlockSpec((tf, d), lambda i, j: (j, 0)), pl.BlockSpec((1, d), lambda i, j: (0, 0))],
        out_specs=[pl.BlockSpec((tm, tf), lambda i, j: (i, j)), pl.BlockSpec((tm, d), lambda i, j: (i, 0)),
                   pl.BlockSpec((1, d), lambda i, j: (0, 0))],
        out_shape=[jax.ShapeDtypeStruct((s, ff), BF16), jax.ShapeDtypeStruct((s, d), F32),
                   jax.ShapeDtypeStruct((1, d), F32)],
        scratch_shapes=[pltpu.VMEM((tm, d), BF16), pltpu.VMEM((tm, d), F32), pltpu.VMEM((8, d), F32)],
        compiler_params=_params("arbitrary", "arbitrary"),
    )(dx2, f, x1, w2t, w1t, gm.reshape(1, d))


def _branch_out_bwd(dx1, proj, ya, yb, yc, wot, woat, wobt, wpt, ps, name):
    s, d = dx1.shape
    gc = d // N_GROUPS
    tm = _tile(s, 256)
    n = s // tm

    def tile(col=0):
        return pl.BlockSpec((tm, d), lambda i: (i, col))

    def const(shape):
        return pl.BlockSpec(shape, lambda i: (0,) * len(shape))

    def body(dx1_r, g0, g1, g2, ya_r, yb_r, yc_r, wot_r, woat_r, wobt_r, wpt_r, ps_r,
             dpa_o, ds_o, dp_o, dpj_o, dya_o, dyb_o, dyc_o, dbob_o, dps_o, dbg_o, sacc, gacc):
        i = pl.program_id(0)

        @pl.when(i == 0)
        def _():
            sacc[...] = jnp.zeros_like(sacc)
            gacc[...] = jnp.zeros_like(gacc)

        dm = _dot(dx1_r[...].astype(BF16), wot_r[...])
        ycp = yc_r[...]
        ys = (ya_r[...], yb_r[...], ycp * ps_r[...])
        dys = []
        for b, g_r in enumerate((g0, g1, g2)):
            sg = _sigmoid(g_r[...])
            dgt = dm * ys[b] * sg * (1.0 - sg)
            dpj_o[:, b * d:(b + 1) * d] = dgt.astype(BF16)
            gacc[b] += _fold8(dgt)
            dys.append(dm * sg)
        dya, dyb, dyc = dys
        sacc[0] += _fold8(dyb)
        sacc[1] += _fold8(dyc * ycp)
        dyab = dya.astype(BF16)
        dybb = dyb.astype(BF16)
        dycb = (dyc * ps_r[...]).astype(BF16)
        dya_o[...] = dyab
        dyb_o[...] = dybb
        dyc_o[...] = dycb
        dpa_o[...] = _dot(dyab, woat_r[...])
        ds_o[...] = _dot(dybb, wobt_r[...])
        for g in range(N_GROUPS):
            cols = slice(g * gc, (g + 1) * gc)
            dp_o[:, cols] = _dot(dycb[:, cols], wpt_r[g])

        @pl.when(i == n - 1)
        def _():
            dbob_o[...] = jnp.sum(sacc[0], axis=0, keepdims=True)
            dps_o[...] = jnp.sum(sacc[1], axis=0, keepdims=True)
            for b in range(3):
                dbg_o[:, b * d:(b + 1) * d] = jnp.sum(gacc[b], axis=0, keepdims=True)

    return pl.pallas_call(
        body, name=name, grid=(n,),
        in_specs=[tile(), tile(6), tile(7), tile(8), tile(), tile(), tile(),
                  const((d, d)), const((d, d)), const((d, d)), const((N_GROUPS, gc, gc)), const((1, d))],
        out_specs=[tile(), tile(), tile(), pl.BlockSpec((tm, 3 * d), lambda i: (i, 2)), tile(), tile(), tile(),
                   const((1, d)), const((1, d)), const((1, 3 * d))],
        out_shape=[jax.ShapeDtypeStruct((s, d), F32)] * 3 + [jax.ShapeDtypeStruct((s, 9 * d), BF16)]
        + [jax.ShapeDtypeStruct((s, d), BF16)] * 3
        + [jax.ShapeDtypeStruct((1, d), F32), jax.ShapeDtypeStruct((1, d), F32), jax.ShapeDtypeStruct((1, 3 * d), F32)],
        scratch_shapes=[pltpu.VMEM((2, 8, d), F32), pltpu.VMEM((3, 8, d), F32)],
        compiler_params=_params("arbitrary"),
    )(dx1, proj, proj, proj, ya, yb, yc, wot, woat, wobt, wpt, ps.reshape(1, d))


def _ln_silu_bwd(v, ds, lg, lb, name):
    s, d = v.shape
    tm = _tile(s, 256)
    n = s // tm

    def tile():
        return pl.BlockSpec((tm, d), lambda i: (i, 0))

    def vec():
        return pl.BlockSpec((1, d), lambda i: (0, 0))

    def body(v_r, ds_r, lg_r, lb_r, dv_o, dlg_o, dlb_o, dcb_o, acc):
        i = pl.program_id(0)

        @pl.when(i == 0)
        def _():
            acc[...] = jnp.zeros_like(acc)

        for r0 in range(0, tm, SUB):
            rows = slice(r0, r0 + SUB)
            vv = v_r[rows, :]
            mu = jnp.mean(vv, axis=-1, keepdims=True)
            vc = vv - mu
            rstd = lax.rsqrt(jnp.mean(vc * vc, axis=-1, keepdims=True) + EPS)
            nrm = vc * rstd
            ln = nrm * lg_r[...] + lb_r[...]
            sg = _sigmoid(ln)
            dln = ds_r[rows, :] * (sg * (1.0 + ln * (1.0 - sg)))
            acc[0] += _fold8(dln * nrm)
            acc[1] += _fold8(dln)
            dn = dln * lg_r[...]
            dv = rstd * (dn - jnp.mean(dn, axis=-1, keepdims=True)
                         - nrm * jnp.mean(dn * nrm, axis=-1, keepdims=True))
            acc[2] += _fold8(dv)
            dv_o[rows, :] = dv

        @pl.when(i == n - 1)
        def _():
            dlg_o[...] = jnp.sum(acc[0], axis=0, keepdims=True)
            dlb_o[...] = jnp.sum(acc[1], axis=0, keepdims=True)
            dcb_o[...] = jnp.sum(acc[2], axis=0, keepdims=True)

    return pl.pallas_call(
        body, name=name, grid=(n,),
        in_specs=[tile(), tile(), vec(), vec()],
        out_specs=[tile(), vec(), vec(), vec()],
        out_shape=[jax.ShapeDtypeStruct((s, d), F32)] + [jax.ShapeDtypeStruct((1, d), F32)] * 3,
        scratch_shapes=[pltpu.VMEM((3, 8, d), F32)],
        compiler_params=_params("arbitrary"),
    )(v, ds, lg.reshape(1, d), lb.reshape(1, d))


def _mix_bwd(proj, dpa, dv, dp, dproj, conv_a, conv_b, name):
    s, d9 = proj.shape
    d = d9 // 9
    gc = d // N_GROUPS
    tm = _tile(s, 256, HALO)
    n = s // tm
    nh = s // HALO

    def tile(col=0):
        return pl.BlockSpec((tm, d), lambda i: (i, col))

    def vec(rows):
        return pl.BlockSpec((rows, d), lambda i: (0, 0))

    def body(ab, ac, ax, bv, bg, ac_h, ax_h, bv_h, bg_h, dpa_r, dv_r, dp_r, ab_n, dpa_n, dv_n, dp_n, wa, wb, _alias,
             dpj_o, dbin_o, dca_o, dcb_o, ua_e, ub_e, dz_e, dv_e, q_e, bacc, cacc_a, cacc_b, ub_s, dv_s):
        i = pl.program_id(0)

        @pl.when(i == 0)
        def _():
            bacc[...] = jnp.zeros_like(bacc)
            cacc_a[...] = jnp.zeros_like(cacc_a)
            cacc_b[...] = jnp.zeros_like(cacc_b)

        keep_p = (i > 0).astype(F32)
        keep_n = (i < n - 1).astype(F32)
        ua_e[0:HALO, :] = ac_h[...] * ax_h[...] * keep_p
        ua_e[HALO:HALO + tm, :] = ac[...] * ax[...]
        ub_e[0:HALO, :] = bv_h[...] * _sigmoid(bg_h[...]) * keep_p
        ub_e[HALO:HALO + tm, :] = bv[...] * _sigmoid(bg[...])
        dz_e[0:tm, :] = dpa_r[...] * ab[...]
        dz_e[tm:tm + HALO, :] = dpa_n[...] * ab_n[...] * keep_n
        dv_e[0:tm, :] = dv_r[...]
        dv_e[tm:tm + HALO, :] = dv_n[...] * keep_n
        _shift_copies(ub_e, ub_s)
        _shift_copies(dv_e, dv_s)
        for g, w in enumerate(POOL_WINDOWS):
            cols = slice(g * gc, (g + 1) * gc)
            t_idx = i * tm + lax.broadcasted_iota(jnp.int32, (tm + HALO, gc), 0)
            cnt = jnp.minimum(t_idx + 1, w).astype(F32)
            q_e[0:tm, cols] = dp_r[:, cols] / cnt[0:tm]
            q_e[tm:tm + HALO, cols] = dp_n[:, cols] * keep_n / cnt[tm:tm + HALO]
        for r0 in range(0, tm, SUB):
            rows = slice(r0, r0 + SUB)
            dz = dz_e[pl.ds(r0, SUB), :]
            z = None
            du = None
            for k in range(K_A):
                u_k = ua_e[pl.ds(HALO - (K_A - 1) + k + r0, SUB), :]
                t = wa[k:k + 1, :] * u_k
                z = t if z is None else z + t
                cacc_a[k] += _fold8(dz * u_k)
                t = wa[k:k + 1, :] * dz_e[pl.ds(r0 + (K_A - 1) - k, SUB), :]
                du = t if du is None else du + t
            outs = [dpa_r[rows, :] * z, du * ax[rows, :], du * ac[rows, :]]
            dvv = dv_e[pl.ds(r0, SUB), :]
            du = None
            for k in range(K_B):
                cacc_b[k] += _fold8(dvv * _window(ub_e, ub_s, HALO - (K_B - 1) + k + r0, SUB))
                t = wb[k:k + 1, :] * _window(dv_e, dv_s, r0 + (K_B - 1) - k, SUB)
                du = t if du is None else du + t
            sg = _sigmoid(bg[rows, :])
            outs.append(du * sg)
            outs.append(du * bv[rows, :] * sg * (1.0 - sg))
            for b, o in enumerate(outs):
                dpj_o[rows, b * d:(b + 1) * d] = o.astype(BF16)
                bacc[b] += _fold8(o)
            for g, w in enumerate(POOL_WINDOWS):
                cols = slice(g * gc, (g + 1) * gc)
                acc = q_e[pl.ds(r0, SUB), cols]
                for j in range(1, w):
                    acc = acc + q_e[pl.ds(r0 + j, SUB), cols]
                o = acc - dp_r[rows, cols]
                dpj_o[rows, 5 * d + g * gc:5 * d + (g + 1) * gc] = o.astype(BF16)
                bacc[5, :, cols] += _fold8(o)

        @pl.when(i == n - 1)
        def _():
            for b in range(6):
                dbin_o[:, b * d:(b + 1) * d] = jnp.sum(bacc[b], axis=0, keepdims=True)
            dca_o[...] = jnp.sum(cacc_a[...], axis=1)
            dcb_o[...] = jnp.sum(cacc_b[...], axis=1)

    ka8 = 8
    kb8 = 32
    return pl.pallas_call(
        body, name=name, grid=(n,),
        in_specs=[tile(0), tile(1), tile(2), tile(3), tile(4),
                  _prev_halo(tm, d, 1), _prev_halo(tm, d, 2), _prev_halo(tm, d, 3), _prev_halo(tm, d, 4),
                  tile(), tile(), tile(),
                  _next_halo(tm, d, 0, nh), _next_halo(tm, d, 0, nh), _next_halo(tm, d, 0, nh), _next_halo(tm, d, 0, nh),
                  vec(K_A), vec(K_B), ANY],
        out_specs=[pl.BlockSpec((tm, 6 * d), lambda i: (i, 0)), pl.BlockSpec((1, 6 * d), lambda i: (0, 0)),
                   pl.BlockSpec((ka8, d), lambda i: (0, 0)), pl.BlockSpec((kb8, d), lambda i: (0, 0))],
        out_shape=[jax.ShapeDtypeStruct((s, 9 * d), BF16), jax.ShapeDtypeStruct((1, 6 * d), F32),
                   jax.ShapeDtypeStruct((ka8, d), F32), jax.ShapeDtypeStruct((kb8, d), F32)],
        scratch_shapes=[pltpu.VMEM((HALO + tm, d), F32), pltpu.VMEM((HALO + tm, d), F32),
                        pltpu.VMEM((tm + HALO, d), F32), pltpu.VMEM((tm + HALO, d), F32),
                        pltpu.VMEM((tm + HALO, d), F32),
                        pltpu.VMEM((6, 8, d), F32), pltpu.VMEM((ka8, 8, d), F32), pltpu.VMEM((kb8, 8, d), F32),
                        pltpu.VMEM((7, tm + HALO - 8, d), F32), pltpu.VMEM((7, tm + HALO - 8, d), F32)],
        input_output_aliases={18: 0},
        compiler_params=_params("arbitrary"),
    )(proj, proj, proj, proj, proj, proj, proj, proj, proj, dpa, dv, dp, proj, dpa, dv, dp, conv_a, conv_b, dproj)


def _dh_rms_bwd(dproj, wint, x0, dx1, g, name):
    s, p = dproj.shape
    d = x0.shape[1]
    tm = _tile(s, 512)
    tk = _tile(p, 1536, 128)
    n = s // tm
    nk = p // tk

    def body(dp_r, w_r, x_r, dx1_r, g_r, dx_o, dg_o, acc, gacc):
        i = pl.program_id(0)
        k = pl.program_id(1)

        @pl.when((i == 0) & (k == 0))
        def _():
            gacc[...] = jnp.zeros_like(gacc)

        @pl.when(k == 0)
        def _():
            acc[...] = jnp.zeros_like(acc)

        acc[...] += _dot(dp_r[...], w_r[...])

        @pl.when(k == nk - 1)
        def _():
            dx, dg = _rms_bwd(acc[...], x_r[...], g_r[...])
            gacc[...] += dg
            dx_o[...] = dx1_r[...] + dx

        @pl.when((i == n - 1) & (k == nk - 1))
        def _():
            dg_o[...] = jnp.sum(gacc[...], axis=0, keepdims=True)

    return pl.pallas_call(
        body, name=name, grid=(n, nk),
        in_specs=[pl.BlockSpec((tm, tk), lambda i, k: (i, k)), pl.BlockSpec((tk, d), lambda i, k: (k, 0)),
                  pl.BlockSpec((tm, d), lambda i, k: (i, 0)), pl.BlockSpec((tm, d), lambda i, k: (i, 0)),
                  pl.BlockSpec((1, d), lambda i, k: (0, 0))],
        out_specs=[pl.BlockSpec((tm, d), lambda i, k: (i, 0)), pl.BlockSpec((1, d), lambda i, k: (0, 0))],
        out_shape=[jax.ShapeDtypeStruct((s, d), F32), jax.ShapeDtypeStruct((1, d), F32)],
        scratch_shapes=[pltpu.VMEM((tm, d), F32), pltpu.VMEM((8, d), F32)],
        compiler_params=_params("arbitrary", "arbitrary"),
    )(dproj, wint, x0, dx1, g.reshape(1, d))


def _layer_fwd(x0, w, l):
    h = _rms_fwd(x0, w["g_mix"], f"rms_fwd_{l}")
    proj = _proj(h, w["win3"], w["b_in"], f"proj_{l}")
    pa, sb, p, v = _mix_fwd(proj, w["conv_a"], w["conv_b"], w["conv_b_bias"], w["ln_b_g"], w["ln_b_b"], f"mix_fwd_{l}")
    ya, yb, yc, mg, x1, h2 = _branch_out(pa, sb, p, proj, x0, w["woa"], w["wob"], w["wp"], w["wo"], w["b_out_b"],
                                         w["pool_scale"], w["g_mlp"], f"branch_out_{l}")
    f, a, x2 = _mlp_fwd(h2, x1, w["w1_3"], w["w2"], f"mlp_fwd_{l}")
    saved = dict(x0=x0, h=h, proj=proj, pa=pa, sb=sb, p=p, v=v, ya=ya, yb=yb, yc=yc, mg=mg, x1=x1, h2=h2, f=f, a=a)
    return x2, saved


def _layer_bwd(dx2, w, sv, l):
    d = dx2.shape[1]
    df, dx1, dgm = _mlp_bwd(dx2, sv["f"], sv["x1"], w["w2t"], w["w1t"], w["g_mlp"], f"mlp_bwd_{l}")
    dw2 = _tn(sv["a"], dx2, 1, f"dw2_{l}")
    dw1 = _tn(sv["h2"], df, N_CHIPS, f"dw1_{l}")
    dpa, ds, dp, dproj, dya, dyb, dyc, dbob, dps, dbg = _branch_out_bwd(
        dx1, sv["proj"], sv["ya"], sv["yb"], sv["yc"], w["wot"], w["woat"], w["wobt"], w["wpt"], w["pool_scale"],
        f"branch_out_bwd_{l}")
    dwo = _tn(sv["mg"], dx1, 1, f"dwo_{l}")
    dwoa = _tn(sv["pa"], dya, 1, f"dwoa_{l}")
    dwob = _tn(sv["sb"], dyb, 1, f"dwob_{l}")
    dwp = _tn_groups(sv["p"], dyc, f"dwp_{l}")
    dv, dlg, dlb, dcb = _ln_silu_bwd(sv["v"], ds, w["ln_b_g"], w["ln_b_b"], f"ln_silu_bwd_{l}")
    dproj, dbm, dca, dcvb = _mix_bwd(sv["proj"], dpa, dv, dp, dproj, w["conv_a"], w["conv_b"], f"mix_bwd_{l}")
    dwin = _tn(sv["h"], dproj, N_CHIPS, f"dwin_{l}")
    dx0, dgmix = _dh_rms_bwd(dproj, w["wint"], sv["x0"], dx1, w["g_mix"], f"dh_{l}")
    ff = dw2.shape[1]
    gc = d // N_GROUPS
    big = dict(
        w_in=dwin,
        w_out_a=dwoa.reshape(N_CHIPS, d // N_CHIPS, d),
        w_out_b=dwob.reshape(N_CHIPS, d // N_CHIPS, d),
        w_pool=dwp.reshape(N_GROUPS, N_CHIPS, gc // N_CHIPS, gc).transpose(1, 0, 2, 3).reshape(N_CHIPS, gc, gc),
        w_o=dwo.reshape(N_CHIPS, d // N_CHIPS, d),
        w_mlp1=dw1,
        w_mlp2=dw2.reshape(N_CHIPS, ff // N_CHIPS, d),
    )
    small = dict(
        g_mix=dgmix, b_in=jnp.concatenate([dbm, dbg], axis=1).reshape(9, d), conv_a=dca[:K_A], conv_b=dcvb[:K_B],
        conv_b_bias=dcb, ln_b_g=dlg, ln_b_b=dlb, b_out_b=dbob, pool_scale=dps, g_mlp=dgm,
    )
    return dx0, big, small


def _local_step(x, target, ws, g_final):
    saved = []
    for l, w in enumerate(ws):
        x, sv = _layer_fwd(x, w, l)
        saved.append(sv)
    dx, loss, dgf = _loss_head(x, target, g_final, "loss_head")
    grads = [None] * len(ws)
    for l in reversed(range(len(ws))):
        dx, big, small = _layer_bwd(dx, ws[l], saved[l], l)
        grads[l] = (big, small)
    return loss, dx, grads, dgf


def _place():
    x, y, c = lax.axis_index("x"), lax.axis_index("y"), lax.axis_index("c")
    chips = [(1 - x, y), (x, 1 - y), (1 - x, 1 - y)]
    return x, y, c, chips


def _place_shard(w, chip_arr, dtype, name):
    rows, cols = w.shape
    tr = _tile(rows, max(16, (1 << 19) // cols), 16)

    def body(k_ref, w_r, o_r):
        o_r[...] = w_r[...].astype(dtype)

    return pl.pallas_call(
        body, name=name,
        grid_spec=pltpu.PrefetchScalarGridSpec(
            num_scalar_prefetch=1, grid=(rows // tr,),
            in_specs=[pl.BlockSpec((tr, cols), lambda i, k_ref: (i, 0))],
            out_specs=pl.BlockSpec((None, tr, cols), lambda i, k_ref: (k_ref[0], i, 0))),
        out_shape=jax.ShapeDtypeStruct((N_CHIPS, rows, cols), dtype),
        compiler_params=_params("parallel"),
    )(chip_arr, w)


def _gather_shards(placed, name):
    n = len(placed)

    def body(*refs):
        ins, outs = refs[:n], refs[n:2 * n]
        send, recv, fsend, frecv = refs[2 * n:]
        x, y, c, chips = _place()
        me = 2 * x + y
        sib = (x, y, 1 - c)

        def half(a, which):
            h = ins[a].shape[1] // 2
            return pl.ds(pl.multiple_of(which * h, 8), h)

        def ici(a, j, chip, src_chip):
            return pltpu.make_async_remote_copy(
                src_ref=ins[a].at[me, half(a, c)], dst_ref=outs[a].at[src_chip, half(a, c)],
                send_sem=send.at[a * 3 + j], recv_sem=recv.at[a * 3 + j], device_id=(*chip, c), device_id_type=MESH)

        def d2d(a, j, src_chip, which):
            rows = outs[a].at[src_chip, half(a, which)]
            return pltpu.make_async_remote_copy(
                src_ref=rows, dst_ref=rows, send_sem=fsend.at[a * 3 + j], recv_sem=frecv.at[a * 3 + j],
                device_id=sib, device_id_type=MESH)

        for a in range(n):
            for j, chip in enumerate(chips):
                ici(a, j, chip, me).start()
        for a in range(n):
            for j, (px, py) in enumerate(chips):
                ici(a, j, (px, py), 2 * px + py).wait_recv()
                d2d(a, j, 2 * px + py, c).start()
        for a in range(n):
            for j, (px, py) in enumerate(chips):
                d2d(a, j, 2 * px + py, 1 - c).wait_recv()
        for a in range(n):
            for j, (px, py) in enumerate(chips):
                ici(a, j, (px, py), me).wait_send()
                d2d(a, j, 2 * px + py, c).wait_send()

    return pl.pallas_call(
        body, name=name,
        in_specs=[ANY] * n, out_specs=[ANY] * n,
        out_shape=[jax.ShapeDtypeStruct(a.shape, a.dtype) for a in placed],
        scratch_shapes=[pltpu.SemaphoreType.DMA((3 * n,))] * 4,
        input_output_aliases={a: a for a in range(n)},
    )(*placed)


def _gather_rows(v, name):
    def body(v_ref, o_ref, send, recv, lsem):
        x, y, c = lax.axis_index("x"), lax.axis_index("y"), lax.axis_index("c")
        me = 4 * x + 2 * y + c
        local = pltpu.make_async_copy(v_ref, o_ref.at[me], lsem)
        local.start()

        def copy(m, src_dev):
            peer = (x ^ ((m >> 2) & 1), y ^ ((m >> 1) & 1), c ^ (m & 1))
            return pltpu.make_async_remote_copy(
                src_ref=v_ref, dst_ref=o_ref.at[src_dev], send_sem=send.at[m - 1], recv_sem=recv.at[m - 1],
                device_id=peer, device_id_type=MESH)

        for m in range(1, N_DEV):
            copy(m, me).start()
        for m in range(1, N_DEV):
            copy(m, me ^ m).wait_recv()
        for m in range(1, N_DEV):
            copy(m, me).wait_send()
        local.wait()

    return pl.pallas_call(
        body, name=name, in_specs=[ANY], out_specs=ANY,
        out_shape=jax.ShapeDtypeStruct((N_DEV, *v.shape), v.dtype),
        scratch_shapes=[pltpu.SemaphoreType.DMA((N_DEV - 1,)), pltpu.SemaphoreType.DMA((N_DEV - 1,)),
                        pltpu.SemaphoreType.DMA],
    )(v)


def _swap_halves(grads, name):
    n = len(grads)

    def body(*refs):
        ins, outs = refs[:n], refs[n:2 * n]
        send, recv = refs[2 * n:]
        x, y, c, _ = _place()
        sib = (x, y, 1 - c)

        def copy(a, k):
            h = ins[a].shape[1] // 2
            return pltpu.make_async_remote_copy(
                src_ref=ins[a].at[k, pl.ds(pl.multiple_of((1 - c) * h, 8), h)], dst_ref=outs[a].at[k],
                send_sem=send.at[a * N_CHIPS + k], recv_sem=recv.at[a * N_CHIPS + k], device_id=sib, device_id_type=MESH)

        for a in range(n):
            for k in range(N_CHIPS):
                copy(a, k).start()
        for a in range(n):
            for k in range(N_CHIPS):
                copy(a, k).wait()

    return pl.pallas_call(
        body, name=name, in_specs=[ANY] * n, out_specs=[ANY] * n,
        out_shape=[jax.ShapeDtypeStruct((N_CHIPS, g.shape[1] // 2, g.shape[2]), g.dtype) for g in grads],
        scratch_shapes=[pltpu.SemaphoreType.DMA((N_CHIPS * n,))] * 2,
    )(*grads)


def _add_halves(g, r, c_arr, name):
    nk, rows, cols = g.shape
    h = rows // 2
    tr = _tile(h, max(8, (1 << 19) // cols), 8)
    nblk = h // tr

    def body(c_ref, g_r, r_r, o_r):
        o_r[...] = (g_r[...] + r_r[...]).astype(BF16)

    return pl.pallas_call(
        body, name=name,
        grid_spec=pltpu.PrefetchScalarGridSpec(
            num_scalar_prefetch=1, grid=(nk, nblk),
            in_specs=[pl.BlockSpec((None, tr, cols), lambda k, i, c_ref: (k, c_ref[0] * nblk + i, 0)),
                      pl.BlockSpec((None, tr, cols), lambda k, i, c_ref: (k, i, 0))],
            out_specs=pl.BlockSpec((None, tr, cols), lambda k, i, c_ref: (k, i, 0))),
        out_shape=jax.ShapeDtypeStruct((nk, h, cols), BF16),
        compiler_params=_params("parallel", "parallel"),
    )(c_arr, g, r)


def _exchange_chips(parts, name):
    n = len(parts)

    def body(*refs):
        ins, outs = refs[:n], refs[n:2 * n]
        send, recv, lsem = refs[2 * n:]
        x, y, c, chips = _place()
        me = 2 * x + y

        def copy(a, j, to_chip, src_chip):
            return pltpu.make_async_remote_copy(
                src_ref=ins[a].at[2 * to_chip[0] + to_chip[1]], dst_ref=outs[a].at[src_chip],
                send_sem=send.at[a * 3 + j], recv_sem=recv.at[a * 3 + j], device_id=(*to_chip, c), device_id_type=MESH)

        local = [pltpu.make_async_copy(ins[a].at[me], outs[a].at[me], lsem.at[a]) for a in range(n)]
        for a in range(n):
            local[a].start()
            for j, chip in enumerate(chips):
                copy(a, j, chip, me).start()
        for a in range(n):
            for j, (px, py) in enumerate(chips):
                copy(a, j, (px, py), 2 * px + py).wait_recv()
        for a in range(n):
            local[a].wait()
            for j, chip in enumerate(chips):
                copy(a, j, chip, me).wait_send()

    return pl.pallas_call(
        body, name=name, in_specs=[ANY] * n, out_specs=[ANY] * n,
        out_shape=[jax.ShapeDtypeStruct(p.shape, p.dtype) for p in parts],
        scratch_shapes=[pltpu.SemaphoreType.DMA((3 * n,))] * 2 + [pltpu.SemaphoreType.DMA((n,))],
    )(*parts)


def _sum_chips(r, name):
    nk, rows, cols = r.shape
    tr = _tile(rows, max(8, (1 << 19) // cols), 8)

    def body(r_r, o_r):
        acc = r_r[0].astype(F32)
        for k in range(1, nk):
            acc = acc + r_r[k].astype(F32)
        o_r[...] = acc

    return pl.pallas_call(
        body, name=name, grid=(rows // tr,),
        in_specs=[pl.BlockSpec((nk, tr, cols), lambda i: (0, i, 0))],
        out_specs=pl.BlockSpec((tr, cols), lambda i: (i, 0)),
        out_shape=jax.ShapeDtypeStruct((rows, cols), F32),
        compiler_params=_params("parallel"),
    )(r)


def _sum_chips_into(r, dst, layer, n_layers, c_arr, name):
    nk, h, cols = r.shape
    tr = _tile(h, max(8, (1 << 19) // cols), 8)
    nblk = h // tr

    def body(c_ref, r_r, *rest):
        o_r = rest[-1]
        acc = r_r[0].astype(F32)
        for k in range(1, nk):
            acc = acc + r_r[k].astype(F32)
        o_r[...] = acc

    in_specs = [pl.BlockSpec((nk, tr, cols), lambda i, c_ref: (0, i, 0))]
    args = [c_arr, r]
    aliases = {}
    if dst is not None:
        in_specs.append(ANY)
        args.append(dst)
        aliases = {2: 0}
    return pl.pallas_call(
        body, name=name,
        grid_spec=pltpu.PrefetchScalarGridSpec(
            num_scalar_prefetch=1, grid=(nblk,), in_specs=in_specs,
            out_specs=pl.BlockSpec((None, tr, cols), lambda i, c_ref: (layer, c_ref[0] * nblk + i, 0))),
        out_shape=jax.ShapeDtypeStruct((n_layers, 2 * h, cols), F32),
        input_output_aliases=aliases,
        compiler_params=_params("parallel"),
    )(*args)


def _share_halves(shards, name):
    n = len(shards)
    n_layers = shards[0].shape[0]

    def body(*refs):
        ins, outs = refs[:n], refs[n:2 * n]
        send, recv = refs[2 * n:]
        x, y, c, _ = _place()
        sib = (x, y, 1 - c)

        def copy(a, l, which):
            h = ins[a].shape[1] // 2
            rows = pl.ds(pl.multiple_of(which * h, 8), h)
            return pltpu.make_async_remote_copy(
                src_ref=ins[a].at[l, rows], dst_ref=outs[a].at[l, rows], send_sem=send.at[a * n_layers + l],
                recv_sem=recv.at[a * n_layers + l], device_id=sib, device_id_type=MESH)

        for a in range(n):
            for l in range(n_layers):
                copy(a, l, c).start()
        for a in range(n):
            for l in range(n_layers):
                copy(a, l, 1 - c).wait_recv()
        for a in range(n):
            for l in range(n_layers):
                copy(a, l, c).wait_send()

    return pl.pallas_call(
        body, name=name, in_specs=[ANY] * n, out_specs=[ANY] * n,
        out_shape=[jax.ShapeDtypeStruct(a.shape, a.dtype) for a in shards],
        scratch_shapes=[pltpu.SemaphoreType.DMA((n * n_layers,))] * 2,
        input_output_aliases={a: a for a in range(n)},
    )(*shards)


def _adamw(w, g, m, v, name):
    rows, cols = w.shape
    tr = _tile(rows, max(8, (1 << 18) // cols), 8)
    c1 = 1.0 - ADAM_B1 ** ADAM_STEP
    c2 = 1.0 - ADAM_B2 ** ADAM_STEP

    def body(w_r, g_r, m_r, v_r, d_o, m_o, v_o):
        gv = g_r[...]
        mn = ADAM_B1 * m_r[...] + (1.0 - ADAM_B1) * gv
        vn = ADAM_B2 * v_r[...] + (1.0 - ADAM_B2) * (gv * gv)
        m_o[...] = mn
        v_o[...] = vn
        d_o[...] = -ADAM_LR * ((mn / c1) / (jnp.sqrt(vn / c2) + ADAM_EPS) + ADAM_WD * w_r[...])

    spec = pl.BlockSpec((tr, cols), lambda i: (i, 0))
    return pl.pallas_call(
        body, name=name, grid=(rows // tr,), in_specs=[spec] * 4, out_specs=[spec] * 3,
        out_shape=[jax.ShapeDtypeStruct((rows, cols), F32)] * 3,
        compiler_params=_params("parallel"),
    )(w, g, m, v)


BIG = ("w_in", "w_out_a", "w_out_b", "w_pool", "w_o", "w_mlp1", "w_mlp2")
SMALL = ("g_mix", "b_in", "conv_b_bias", "ln_b_g", "ln_b_b", "b_out_b", "pool_scale", "g_mlp")
CONVS = ("conv_a", "conv_b")
WEIGHTS = ("g_mix", "w_in", "b_in", "conv_a", "w_out_a", "conv_b", "conv_b_bias", "ln_b_g", "ln_b_b", "w_out_b", "b_out_b",
           "w_pool", "pool_scale", "w_o", "g_mlp", "w_mlp1", "w_mlp2", "g_final")


def _as2d(a):
    return a.reshape(-1, a.shape[-1])


def _pad_rows(a, rows):
    return jnp.pad(a, ((0, rows - a.shape[0]), (0, 0)))


def _full_weights(big_g, conv_g, rep, l, d):
    gc = d // N_GROUPS
    w = {k: v[l] for k, v in rep.items()}
    w["b_in"] = rep["b_in"][l].reshape(1, -1)
    win3 = big_g["w_in"][l]
    w["win3"] = win3
    w["wint"] = win3.transpose(0, 2, 1).reshape(-1, d)
    for src, dst in (("w_out_a", "woa"), ("w_out_b", "wob"), ("w_o", "wo")):
        full = big_g[src][l].reshape(d, d)
        w[dst] = full
        w[dst + "t"] = full.T
    wp = big_g["w_pool"][l].reshape(N_CHIPS, N_GROUPS, gc // N_CHIPS, gc).transpose(1, 0, 2, 3).reshape(N_GROUPS, gc, gc)
    w["wp"] = wp
    w["wpt"] = wp.transpose(0, 2, 1)
    w1_3 = big_g["w_mlp1"][l]
    w["w1_3"] = w1_3
    w["w1t"] = w1_3.transpose(0, 2, 1).reshape(-1, d)
    w2 = big_g["w_mlp2"][l].reshape(-1, d)
    w["w2"] = w2
    w["w2t"] = w2.T
    ca, cb = conv_g[l]
    w["conv_a"] = ca
    w["conv_b"] = cb
    return w


def kernel(x, g_mix, w_in, b_in, conv_a, w_out_a, conv_b, conv_b_bias, ln_b_g, ln_b_b, w_out_b, b_out_b, w_pool, pool_scale, w_o, g_mlp, w_mlp1, w_mlp2, g_final, loss_target, m_g_mix, m_w_in, m_b_in, m_conv_a, m_w_out_a, m_conv_b, m_conv_b_bias, m_ln_b_g, m_ln_b_b, m_w_out_b, m_b_out_b, m_w_pool, m_pool_scale, m_w_o, m_g_mlp, m_w_mlp1, m_w_mlp2, m_g_final, v_g_mix, v_w_in, v_b_in, v_conv_a, v_w_out_a, v_conv_b, v_conv_b_bias, v_ln_b_g, v_ln_b_b, v_w_out_b, v_b_out_b, v_w_pool, v_pool_scale, v_w_o, v_g_mlp, v_w_mlp1, v_w_mlp2, v_g_final):
    given = dict(g_mix=g_mix, w_in=w_in, b_in=b_in, conv_a=conv_a, w_out_a=w_out_a, conv_b=conv_b, conv_b_bias=conv_b_bias, ln_b_g=ln_b_g, ln_b_b=ln_b_b, w_out_b=w_out_b, b_out_b=b_out_b, w_pool=w_pool, pool_scale=pool_scale, w_o=w_o, g_mlp=g_mlp, w_mlp1=w_mlp1, w_mlp2=w_mlp2, g_final=g_final)
    mom = dict(g_mix=m_g_mix, w_in=m_w_in, b_in=m_b_in, conv_a=m_conv_a, w_out_a=m_w_out_a, conv_b=m_conv_b, conv_b_bias=m_conv_b_bias, ln_b_g=m_ln_b_g, ln_b_b=m_ln_b_b, w_out_b=m_w_out_b, b_out_b=m_b_out_b, w_pool=m_w_pool, pool_scale=m_pool_scale, w_o=m_w_o, g_mlp=m_g_mlp, w_mlp1=m_w_mlp1, w_mlp2=m_w_mlp2, g_final=m_g_final)
    var = dict(g_mix=v_g_mix, w_in=v_w_in, b_in=v_b_in, conv_a=v_conv_a, w_out_a=v_w_out_a, conv_b=v_conv_b, conv_b_bias=v_conv_b_bias, ln_b_g=v_ln_b_g, ln_b_b=v_ln_b_b, w_out_b=v_w_out_b, b_out_b=v_b_out_b, w_pool=v_w_pool, pool_scale=v_pool_scale, w_o=v_w_o, g_mlp=v_g_mlp, w_mlp1=v_w_mlp1, w_mlp2=v_w_mlp2, g_final=v_g_final)
    n_layers = w_in.shape[0]
    s, d = x.shape[1], x.shape[2]
    dq = d // N_CHIPS
    x_idx, y_idx, c_idx = lax.axis_index("x"), lax.axis_index("y"), lax.axis_index("c")
    chip = 2 * x_idx + y_idx
    c_arr = c_idx.astype(jnp.int32).reshape(1)

    conv_rows = n_layers * (K_A + K_B)
    conv_pad = -(-conv_rows // 16) * 16
    conv_pack = _pad_rows(jnp.concatenate([conv_a[l] for l in range(n_layers)] + [conv_b[l] for l in range(n_layers)], axis=0),
                          conv_pad)
    chip_arr = chip.astype(jnp.int32).reshape(1)
    placed = [_place_shard(_as2d(given[k][l]), chip_arr, BF16, f"place_{k}_{l}") for l in range(n_layers) for k in BIG]
    placed.append(_place_shard(conv_pack, chip_arr, F32, "place_convs"))
    gathered = _gather_shards(placed, "gather_weights")
    big_g = {k: [gathered[l * len(BIG) + i] for l in range(n_layers)] for i, k in enumerate(BIG)}
    conv_full = gathered[-1].transpose(1, 0, 2).reshape(conv_pad, d)
    conv_g = [(conv_full[l * K_A:(l + 1) * K_A], conv_full[n_layers * K_A + l * K_B:n_layers * K_A + (l + 1) * K_B])
              for l in range(n_layers)]
    rep = {k: given[k] for k in SMALL}
    ws = [_full_weights(big_g, conv_g, rep, l, d) for l in range(n_layers)]

    loss, dx, grads, dgf = _local_step(x[0], loss_target[0], ws, g_final)
    loss = lax.psum(loss[0, 0], ("x", "y", "c"))

    big_list = [grads[l][0][k] for l in range(n_layers) for k in BIG]
    swapped = _swap_halves(big_list, "grad_swap_halves")
    parts = [_add_halves(g, r, c_arr, f"grad_add_halves_{a}") for a, (g, r) in enumerate(zip(big_list, swapped))]
    arrived = _exchange_chips(parts, "grad_exchange_chips")
    reduced = []
    for i, k in enumerate(BIG):
        dst = None
        for l in range(n_layers):
            dst = _sum_chips_into(arrived[l * len(BIG) + i], dst, l, n_layers, c_arr, f"grad_sum_chips_{k}_{l}")
        reduced.append(dst)
    big_grad = dict(zip(BIG, _share_halves(reduced, "grad_share_halves")))

    small_rows = []
    for k in SMALL + CONVS:
        for l in range(n_layers):
            small_rows.append(grads[l][1][k])
    small_rows.append(dgf)
    pack = jnp.concatenate(small_rows, axis=0)
    n_small = pack.shape[0]
    pack = _pad_rows(pack, -(-n_small // 8) * 8)
    small_sum = _sum_chips(_gather_rows(pack, "gather_small_grads"), "sum_small_grads")

    out = {}
    for k in BIG:
        shape = given[k].shape
        g2 = big_grad[k].reshape(-1, shape[-1])
        dlt, nm, nv = _adamw(given[k].reshape(g2.shape), g2, mom[k].reshape(g2.shape), var[k].reshape(g2.shape), f"adamw_{k}")
        out[k] = tuple(a.reshape(shape) for a in (g2, dlt, nm, nv))

    def rows_of(k):
        return {"b_in": 9, "conv_a": K_A, "conv_b": K_B}.get(k, 1)

    def pack_rep(src):
        rows = [src[k].reshape(n_layers * rows_of(k), d) for k in SMALL] + [src["g_final"].reshape(1, d)]
        a = jnp.concatenate(rows, axis=0)
        return _pad_rows(a, -(-a.shape[0] // 8) * 8)

    def pack_conv(src):
        a = jnp.concatenate([src[k].reshape(n_layers * rows_of(k), dq) for k in CONVS], axis=0)
        return _pad_rows(a, -(-a.shape[0] // 8) * 8)

    offs = {}
    r = 0
    for k in SMALL + CONVS:
        offs[k] = r
        r += n_layers * rows_of(k)
    offs["g_final"] = r
    n_rep = sum(n_layers * rows_of(k) for k in SMALL)
    g_rep = jnp.concatenate([small_sum[:n_rep], small_sum[offs["g_final"]:offs["g_final"] + 1]], axis=0)
    g_rep = _pad_rows(g_rep, -(-g_rep.shape[0] // 8) * 8)
    g_conv_full = small_sum[offs["conv_a"]:offs["g_final"]]
    g_conv = lax.dynamic_slice_in_dim(g_conv_full, chip * dq, dq, axis=1)
    g_conv = _pad_rows(g_conv, -(-g_conv.shape[0] // 8) * 8)
    rep_res = (g_rep,) + tuple(_adamw(pack_rep(given), g_rep, pack_rep(mom), pack_rep(var), "adamw_small"))
    conv_res = (g_conv,) + tuple(_adamw(pack_conv(given), g_conv, pack_conv(mom), pack_conv(var), "adamw_conv"))
    r = 0
    for k in SMALL:
        nr = n_layers * rows_of(k)
        out[k] = tuple(a[r:r + nr].reshape(given[k].shape) for a in rep_res)
        r += nr
    out["g_final"] = tuple(a[r].reshape(given["g_final"].shape) for a in rep_res)
    r = 0
    for k in CONVS:
        nr = n_layers * rows_of(k)
        out[k] = tuple(a[r:r + nr].reshape(given[k].shape) for a in conv_res)
        r += nr

    res = [loss, dx.reshape(x.shape)]
    for i in range(4):
        res += [out[k][i] for k in WEIGHTS]
    return tuple(res)
```

```python
import functools

import jax
import jax.numpy as jnp
from jax import lax
from jax.experimental import pallas as pl
from jax.experimental.pallas import tpu as pltpu

F32 = jnp.float32
BF16 = jnp.bfloat16
EPS = 1e-6
HALO = 32
SUB = 32
K_A = 3
K_B = 31
POOL_WINDOWS = (2, 4, 8, 16)
N_GROUPS = len(POOL_WINDOWS)
N_CHIPS = 4
N_DEV = 8
ADAM_LR = 0.001
ADAM_B1 = 0.9
ADAM_B2 = 0.999
ADAM_EPS = 1e-08
ADAM_WD = 0.01
ADAM_STEP = 10
VMEM_LIMIT = 56 * 1024 * 1024
MESH = pl.DeviceIdType.MESH
ANY = pl.BlockSpec(memory_space=pl.ANY)


def _params(*sem):
    return pltpu.CompilerParams(dimension_semantics=sem, vmem_limit_bytes=VMEM_LIMIT)


def _tile(n, pref, mult=8):
    if n <= pref:
        return n
    t = (pref // mult) * mult
    while t > mult and n % t:
        t -= mult
    assert n % t == 0, (n, pref, mult)
    return t


def _dot(a, b):
    return jnp.dot(a, b, preferred_element_type=F32)


def _sigmoid(x):
    return 1.0 / (1.0 + jnp.exp(-x))


def _f32(ref):
    return ref[...].astype(F32)


def _fold8(v):
    r, d = v.shape
    return v.reshape(r // 8, 8, d).sum(axis=0)


def _rms_bwd(dh, xv, g):
    r = lax.rsqrt(jnp.mean(xv * xv, axis=-1, keepdims=True) + EPS)
    xn = xv * r
    dxn = dh * g
    dx = r * (dxn - xn * jnp.mean(dxn * xn, axis=-1, keepdims=True))
    return dx, _fold8(dh * xn)


def _rms_fwd(x, g, name):
    s, d = x.shape
    tm = _tile(s, 512)

    def body(x_ref, g_ref, h_ref):
        xv = x_ref[...]
        r = lax.rsqrt(jnp.mean(xv * xv, axis=-1, keepdims=True) + EPS)
        h_ref[...] = (xv * r * g_ref[...]).astype(BF16)

    return pl.pallas_call(
        body, name=name, grid=(s // tm,),
        in_specs=[pl.BlockSpec((tm, d), lambda i: (i, 0)), pl.BlockSpec((1, d), lambda i: (0, 0))],
        out_specs=pl.BlockSpec((tm, d), lambda i: (i, 0)),
        out_shape=jax.ShapeDtypeStruct((s, d), BF16),
        compiler_params=_params("parallel"),
    )(x, g.reshape(1, d))


def _proj(h, w3, b, name):
    s, k = h.shape
    nb, _, nw = w3.shape
    tm = _tile(s, 1024)
    tn = nw // 3
    per = nw // tn

    def body(h_ref, w_ref, b_ref, o_ref):
        o_ref[...] = (_dot(h_ref[...], w_ref[...]) + b_ref[...]).astype(BF16)

    return pl.pallas_call(
        body, name=name, grid=(s // tm, nb * per),
        in_specs=[
            pl.BlockSpec((tm, k), lambda i, j: (i, 0)),
            pl.BlockSpec((None, k, tn), lambda i, j: (j // per, 0, j % per)),
            pl.BlockSpec((1, tn), lambda i, j: (0, j)),
        ],
        out_specs=pl.BlockSpec((tm, tn), lambda i, j: (i, j)),
        out_shape=jax.ShapeDtypeStruct((s, nb * nw), BF16),
        compiler_params=_params("parallel", "parallel"),
    )(h, w3, b)


def _shift_copies(src_e, dst_s):
    n_rows = dst_s.shape[1]
    for b in range(1, 8):
        dst_s[b - 1, :, :] = src_e[pl.ds(b, n_rows), :]


def _window(src_e, dst_s, off, rows):
    b = off % 8
    if b == 0:
        return src_e[pl.ds(off, rows), :]
    return dst_s[b - 1, pl.ds(off - b, rows), :]


def _prev_halo(tm, d, col):
    r = tm // HALO
    return pl.BlockSpec((HALO, d), lambda i: (jnp.maximum(i * r - 1, 0), col))


def _next_halo(tm, d, col, n_halo_blocks):
    r = tm // HALO
    return pl.BlockSpec((HALO, d), lambda i: (jnp.minimum((i + 1) * r, n_halo_blocks - 1), col))


def _mix_fwd(proj, conv_a, conv_b, cb, lg, lb, name):
    s, d9 = proj.shape
    d = d9 // 9
    gc = d // N_GROUPS
    tm = _tile(s, 256, HALO)

    def tile(col):
        return pl.BlockSpec((tm, d), lambda i: (i, col))

    def vec(rows):
        return pl.BlockSpec((rows, d), lambda i: (0, 0))

    def body(ab, ac, ax, bv, bg, ci, ac_h, ax_h, bv_h, bg_h, ci_h, wa, wb, cb_r, lg_r, lb_r,
             pa_o, s_o, p_o, v_o, ua_e, ub_e, ci_e, ub_s):
        i = pl.program_id(0)
        keep = (i > 0).astype(F32)
        ua_e[0:HALO, :] = _f32(ac_h) * _f32(ax_h) * keep
        ua_e[HALO:HALO + tm, :] = _f32(ac) * _f32(ax)
        ub_e[0:HALO, :] = _f32(bv_h) * _sigmoid(_f32(bg_h)) * keep
        ub_e[HALO:HALO + tm, :] = _f32(bv) * _sigmoid(_f32(bg))
        ci_e[0:HALO, :] = _f32(ci_h) * keep
        ci_e[HALO:HALO + tm, :] = _f32(ci)
        _shift_copies(ub_e, ub_s)
        for r0 in range(0, tm, SUB):
            rows = slice(r0, r0 + SUB)
            z = None
            for k in range(K_A):
                t = wa[k:k + 1, :] * ua_e[pl.ds(HALO - (K_A - 1) + k + r0, SUB), :]
                z = t if z is None else z + t
            pa_o[rows, :] = (ab[rows, :].astype(F32) * z).astype(BF16)
            v = None
            for k in range(K_B):
                t = wb[k:k + 1, :] * _window(ub_e, ub_s, HALO - (K_B - 1) + k + r0, SUB)
                v = t if v is None else v + t
            v = v + cb_r[...]
            v_o[rows, :] = v
            mu = jnp.mean(v, axis=-1, keepdims=True)
            vc = v - mu
            rstd = lax.rsqrt(jnp.mean(vc * vc, axis=-1, keepdims=True) + EPS)
            ln = vc * rstd * lg_r[...] + lb_r[...]
            s_o[rows, :] = (ln * _sigmoid(ln)).astype(BF16)
            t_idx = i * tm + r0 + lax.broadcasted_iota(jnp.int32, (SUB, gc), 0)
            for g, w in enumerate(POOL_WINDOWS):
                cols = slice(g * gc, (g + 1) * gc)
                cur = ci_e[pl.ds(HALO + r0, SUB), cols]
                acc = cur
                for j in range(1, w):
                    acc = acc + ci_e[pl.ds(HALO + r0 - j, SUB), cols]
                cnt = jnp.minimum(t_idx + 1, w).astype(F32)
                p_o[rows, cols] = (acc / cnt - cur).astype(BF16)

    return pl.pallas_call(
        body, name=name, grid=(s // tm,),
        in_specs=[tile(0), tile(1), tile(2), tile(3), tile(4), tile(5),
                  _prev_halo(tm, d, 1), _prev_halo(tm, d, 2), _prev_halo(tm, d, 3), _prev_halo(tm, d, 4),
                  _prev_halo(tm, d, 5),
                  vec(K_A), vec(K_B), vec(1), vec(1), vec(1)],
        out_specs=[pl.BlockSpec((tm, d), lambda i: (i, 0))] * 4,
        out_shape=[jax.ShapeDtypeStruct((s, d), BF16)] * 3 + [jax.ShapeDtypeStruct((s, d), F32)],
        scratch_shapes=[pltpu.VMEM((HALO + tm, d), F32)] * 3 + [pltpu.VMEM((7, tm + HALO - 8, d), F32)],
        compiler_params=_params("parallel"),
    )(proj, proj, proj, proj, proj, proj, proj, proj, proj, proj, proj,
      conv_a, conv_b, cb.reshape(1, d), lg.reshape(1, d), lb.reshape(1, d))


def _branch_out(pa, sb, p, proj, x0, woa, wob, wp, wo, bob, ps, gm, name):
    s, d = pa.shape
    gc = d // N_GROUPS
    tm = _tile(s, 512)

    def tile(col=0):
        return pl.BlockSpec((tm, d), lambda i: (i, col))

    def const(shape):
        return pl.BlockSpec(shape, lambda i: (0,) * len(shape))

    def body(pa_r, s_r, p_r, g0, g1, g2, x0_r, woa_r, wob_r, wp_r, wo_r, bob_r, ps_r, gm_r,
             ya_o, yb_o, yc_o, mg_o, x1_o, h2_o, yc_s):
        ya = _dot(pa_r[...], woa_r[...])
        yb = _dot(s_r[...], wob_r[...]) + bob_r[...]
        for g in range(N_GROUPS):
            cols = slice(g * gc, (g + 1) * gc)
            yc_s[:, cols] = _dot(p_r[:, cols], wp_r[g])
        yc = yc_s[...]
        ya_o[...] = ya.astype(BF16)
        yb_o[...] = yb.astype(BF16)
        yc_o[...] = yc.astype(BF16)
        m = _sigmoid(_f32(g0)) * ya + _sigmoid(_f32(g1)) * yb + _sigmoid(_f32(g2)) * (yc * ps_r[...])
        mb = m.astype(BF16)
        mg_o[...] = mb
        x1 = x0_r[...] + _dot(mb, wo_r[...])
        x1_o[...] = x1
        r = lax.rsqrt(jnp.mean(x1 * x1, axis=-1, keepdims=True) + EPS)
        h2_o[...] = (x1 * r * gm_r[...]).astype(BF16)

    return pl.pallas_call(
        body, name=name, grid=(s // tm,),
        in_specs=[tile(), tile(), tile(), tile(6), tile(7), tile(8), tile(),
                  const((d, d)), const((d, d)), const((N_GROUPS, gc, gc)), const((d, d)),
                  const((1, d)), const((1, d)), const((1, d))],
        out_specs=[tile()] * 6,
        out_shape=[jax.ShapeDtypeStruct((s, d), BF16)] * 4 + [jax.ShapeDtypeStruct((s, d), F32),
                                                               jax.ShapeDtypeStruct((s, d), BF16)],
        scratch_shapes=[pltpu.VMEM((tm, d), F32)],
        compiler_params=_params("parallel"),
    )(pa, sb, p, proj, proj, proj, x0, woa, wob, wp, wo, bob.reshape(1, d), ps.reshape(1, d), gm.reshape(1, d))


def _mlp_fwd(h2, x1, w1_3, w2, name):
    s, d = h2.shape
    nf, _, tf = w1_3.shape
    tm = _tile(s, 1024)

    def body(h_r, x1_r, w1_r, w2_r, f_o, a_o, x2_o, acc):
        j = pl.program_id(1)

        @pl.when(j == 0)
        def _():
            acc[...] = jnp.zeros_like(acc)

        f = _dot(h_r[...], w1_r[...])
        f_o[...] = f.astype(BF16)
        rl = jnp.maximum(f, 0.0)
        a = (rl * rl).astype(BF16)
        a_o[...] = a
        acc[...] += _dot(a, w2_r[...])

        @pl.when(j == nf - 1)
        def _():
            x2_o[...] = x1_r[...] + acc[...]

    return pl.pallas_call(
        body, name=name, grid=(s // tm, nf),
        in_specs=[pl.BlockSpec((tm, d), lambda i, j: (i, 0)), pl.BlockSpec((tm, d), lambda i, j: (i, 0)),
                  pl.BlockSpec((None, d, tf), lambda i, j: (j, 0, 0)), pl.BlockSpec((tf, d), lambda i, j: (j, 0))],
        out_specs=[pl.BlockSpec((tm, tf), lambda i, j: (i, j)), pl.BlockSpec((tm, tf), lambda i, j: (i, j)),
                   pl.BlockSpec((tm, d), lambda i, j: (i, 0))],
        out_shape=[jax.ShapeDtypeStruct((s, nf * tf), BF16), jax.ShapeDtypeStruct((s, nf * tf), BF16),
                   jax.ShapeDtypeStruct((s, d), F32)],
        scratch_shapes=[pltpu.VMEM((tm, d), F32)],
        compiler_params=_params("parallel", "arbitrary"),
    )(h2, x1, w1_3, w2)


def _loss_head(x, target, gf, name):
    s, d = x.shape
    tm = _tile(s, 512)
    n = s // tm

    def body(x_r, t_r, g_r, dx_o, loss_o, dg_o, lacc, gacc):
        i = pl.program_id(0)

        @pl.when(i == 0)
        def _():
            lacc[...] = jnp.zeros_like(lacc)
            gacc[...] = jnp.zeros_like(gacc)

        xv = x_r[...]
        r = lax.rsqrt(jnp.mean(xv * xv, axis=-1, keepdims=True) + EPS)
        xn = xv * r
        e = xn * g_r[...] - t_r[...]
        lacc[...] += _fold8(e * e)
        dy = e * (1.0 / d)
        gacc[...] += _fold8(dy * xn)
        dxn = dy * g_r[...]
        dx_o[...] = r * (dxn - xn * jnp.mean(dxn * xn, axis=-1, keepdims=True))

        @pl.when(i == n - 1)
        def _():
            loss_o[...] = jnp.sum(lacc[...]).reshape(1, 1) * (0.5 / d)
            dg_o[...] = jnp.sum(gacc[...], axis=0, keepdims=True)

    return pl.pallas_call(
        body, name=name, grid=(n,),
        in_specs=[pl.BlockSpec((tm, d), lambda i: (i, 0)), pl.BlockSpec((tm, d), lambda i: (i, 0)),
                  pl.BlockSpec((1, d), lambda i: (0, 0))],
        out_specs=[pl.BlockSpec((tm, d), lambda i: (i, 0)), pl.BlockSpec((1, 1), lambda i: (0, 0)),
                   pl.BlockSpec((1, d), lambda i: (0, 0))],
        out_shape=[jax.ShapeDtypeStruct((s, d), F32), jax.ShapeDtypeStruct((1, 1), F32),
                   jax.ShapeDtypeStruct((1, d), F32)],
        scratch_shapes=[pltpu.VMEM((8, d), F32), pltpu.VMEM((8, d), F32)],
        compiler_params=_params("arbitrary"),
    )(x, target, gf.reshape(1, d))


def _tn(a, b, nb, name):
    s, m = a.shape
    n = b.shape[1]
    nw = n // nb
    tmm = _tile(m, 1024, 128)
    tn = nw if nw <= 1152 else nw // 2
    per = nw // tn
    ts = _tile(s, 1024)
    ns = s // ts

    def body(a_r, b_r, o_r, acc):
        k = pl.program_id(2)

        @pl.when(k == 0)
        def _():
            acc[...] = jnp.zeros_like(acc)

        acc[...] += lax.dot_general(a_r[...].astype(BF16), b_r[...].astype(BF16), (((0,), (0,)), ((), ())),
                                    preferred_element_type=F32)

        @pl.when(k == ns - 1)
        def _():
            o_r[...] = acc[...]

    return pl.pallas_call(
        body, name=name, grid=(m // tmm, nb * per, ns),
        in_specs=[pl.BlockSpec((ts, tmm), lambda i, j, k: (k, i)), pl.BlockSpec((ts, tn), lambda i, j, k: (k, j))],
        out_specs=pl.BlockSpec((None, tmm, tn), lambda i, j, k: (j // per, i, j % per)),
        out_shape=jax.ShapeDtypeStruct((nb, m, nw), F32),
        scratch_shapes=[pltpu.VMEM((tmm, tn), F32)],
        compiler_params=_params("parallel", "parallel", "arbitrary"),
    )(a, b)


def _tn_groups(a, b, name):
    s, d = a.shape
    gc = d // N_GROUPS
    ts = _tile(s, 512)
    ns = s // ts

    def body(a_r, b_r, o_r, acc):
        k = pl.program_id(1)

        @pl.when(k == 0)
        def _():
            acc[...] = jnp.zeros_like(acc)

        acc[...] += lax.dot_general(a_r[...], b_r[...], (((0,), (0,)), ((), ())), preferred_element_type=F32)

        @pl.when(k == ns - 1)
        def _():
            o_r[...] = acc[...]

    return pl.pallas_call(
        body, name=name, grid=(N_GROUPS, ns),
        in_specs=[pl.BlockSpec((ts, gc), lambda g, k: (k, g)), pl.BlockSpec((ts, gc), lambda g, k: (k, g))],
        out_specs=pl.BlockSpec((None, gc, gc), lambda g, k: (g, 0, 0)),
        out_shape=jax.ShapeDtypeStruct((N_GROUPS, gc, gc), F32),
        scratch_shapes=[pltpu.VMEM((gc, gc), F32)],
        compiler_params=_params("parallel", "arbitrary"),
    )(a, b)


def _mlp_bwd(dx2, f, x1, w2t, w1t, gm, name):
    s, d = dx2.shape
    ff = f.shape[1]
    tf = _tile(ff, 1024, 128)
    nf = ff // tf
    tm = _tile(s, 1024)
    n = s // tm

    def body(dx2_r, f_r, x1_r, w2t_r, w1t_r, gm_r, df_o, dx1_o, dg_o, dxb, acc, gacc):
        i = pl.program_id(0)
        j = pl.program_id(1)

        @pl.when((i == 0) & (j == 0))
        def _():
            gacc[...] = jnp.zeros_like(gacc)

        @pl.when(j == 0)
        def _():
            dxb[...] = dx2_r[...].astype(BF16)
            acc[...] = jnp.zeros_like(acc)

        da = _dot(dxb[...], w2t_r[...])
        df = (da * (2.0 * jnp.maximum(f_r[...].astype(F32), 0.0))).astype(BF16)
        df_o[...] = df
        acc[...] += _dot(df, w1t_r[...])

        @pl.when(j == nf - 1)
        def _():
            dx, dg = _rms_bwd(acc[...], x1_r[...], gm_r[...])
            gacc[...] += dg
            dx1_o[...] = dx2_r[...] + dx

        @pl.when((i == n - 1) & (j == nf - 1))
        def _():
            dg_o[...] = jnp.sum(gacc[...], axis=0, keepdims=True)

    return pl.pallas_call(
        body, name=name, grid=(n, nf),
        in_specs=[pl.BlockSpec((tm, d), lambda i, j: (i, 0)), pl.BlockSpec((tm, tf), lambda i, j: (i, j)),
                  pl.BlockSpec((tm, d), lambda i, j: (i, 0)), pl.BlockSpec((d, tf), lambda i, j: (0, j)),
                  pl.BlockSpec((tf, d), lambda i, j: (j, 0)), pl.BlockSpec((1, d), lambda i, j: (0, 0))],
        out_specs=[pl.BlockSpec((tm, tf), lambda i, j: (i, j)), pl.BlockSpec((tm, d), lambda i, j: (i, 0)),
                   pl.BlockSpec((1, d), lambda i, j: (0, 0))],
        out_shape=[jax.ShapeDtypeStruct((s, ff), BF16), jax.ShapeDtypeStruct((s, d), F32),
                   jax.ShapeDtypeStruct((1, d), F32)],
        scratch_shapes=[pltpu.VMEM((tm, d), BF16), pltpu.VMEM((tm, d), F32), pltpu.VMEM((8, d), F32)],
        compiler_params=_params("arbitrary", "arbitrary"),
    )(dx2, f, x1, w2t, w1t, gm.reshape(1, d))


def _branch_out_bwd(dx1, proj, ya, yb, yc, wot, woat, wobt, wpt, ps, name):
    s, d = dx1.shape
    gc = d // N_GROUPS
    tm = _tile(s, 512)
    n = s // tm

    def tile(col=0):
        return pl.BlockSpec((tm, d), lambda i: (i, col))

    def const(shape):
        return pl.BlockSpec(shape, lambda i: (0,) * len(shape))

    def body(dx1_r, g0, g1, g2, ya_r, yb_r, yc_r, wot_r, woat_r, wobt_r, wpt_r, ps_r,
             dpa_o, ds_o, dp_o, dpj_o, dya_o, dyb_o, dyc_o, dbob_o, dps_o, dbg_o, sacc, gacc):
        i = pl.program_id(0)

        @pl.when(i == 0)
        def _():
            sacc[...] = jnp.zeros_like(sacc)
            gacc[...] = jnp.zeros_like(gacc)

        dm = _dot(dx1_r[...].astype(BF16), wot_r[...])
        ycp = _f32(yc_r)
        ys = (_f32(ya_r), _f32(yb_r), ycp * ps_r[...])
        dys = []
        for b, g_r in enumerate((g0, g1, g2)):
            sg = _sigmoid(_f32(g_r))
            dgt = dm * ys[b] * sg * (1.0 - sg)
            dpj_o[:, b * d:(b + 1) * d] = dgt.astype(BF16)
            gacc[b] += _fold8(dgt)
            dys.append(dm * sg)
        dya, dyb, dyc = dys
        sacc[0] += _fold8(dyb)
        sacc[1] += _fold8(dyc * ycp)
        dyab = dya.astype(BF16)
        dybb = dyb.astype(BF16)
        dycb = (dyc * ps_r[...]).astype(BF16)
        dya_o[...] = dyab
        dyb_o[...] = dybb
        dyc_o[...] = dycb
        dpa_o[...] = _dot(dyab, woat_r[...]).astype(BF16)
        ds_o[...] = _dot(dybb, wobt_r[...]).astype(BF16)
        for g in range(N_GROUPS):
            cols = slice(g * gc, (g + 1) * gc)
            dp_o[:, cols] = _dot(dycb[:, cols], wpt_r[g]).astype(BF16)

        @pl.when(i == n - 1)
        def _():
            dbob_o[...] = jnp.sum(sacc[0], axis=0, keepdims=True)
            dps_o[...] = jnp.sum(sacc[1], axis=0, keepdims=True)
            for b in range(3):
                dbg_o[:, b * d:(b + 1) * d] = jnp.sum(gacc[b], axis=0, keepdims=True)

    return pl.pallas_call(
        body, name=name, grid=(n,),
        in_specs=[tile(), tile(6), tile(7), tile(8), tile(), tile(), tile(),
                  const((d, d)), const((d, d)), const((d, d)), const((N_GROUPS, gc, gc)), const((1, d))],
        out_specs=[tile(), tile(), tile(), pl.BlockSpec((tm, 3 * d), lambda i: (i, 2)), tile(), tile(), tile(),
                   const((1, d)), const((1, d)), const((1, 3 * d))],
        out_shape=[jax.ShapeDtypeStruct((s, d), BF16)] * 3 + [jax.ShapeDtypeStruct((s, 9 * d), BF16)]
        + [jax.ShapeDtypeStruct((s, d), BF16)] * 3
        + [jax.ShapeDtypeStruct((1, d), F32), jax.ShapeDtypeStruct((1, d), F32), jax.ShapeDtypeStruct((1, 3 * d), F32)],
        scratch_shapes=[pltpu.VMEM((2, 8, d), F32), pltpu.VMEM((3, 8, d), F32)],
        compiler_params=_params("arbitrary"),
    )(dx1, proj, proj, proj, ya, yb, yc, wot, woat, wobt, wpt, ps.reshape(1, d))


def _ln_silu_bwd(v, ds, lg, lb, name):
    s, d = v.shape
    tm = _tile(s, 256)
    n = s // tm

    def tile():
        return pl.BlockSpec((tm, d), lambda i: (i, 0))

    def vec():
        return pl.BlockSpec((1, d), lambda i: (0, 0))

    def body(v_r, ds_r, lg_r, lb_r, dv_o, dlg_o, dlb_o, dcb_o, acc):
        i = pl.program_id(0)

        @pl.when(i == 0)
        def _():
            acc[...] = jnp.zeros_like(acc)

        for r0 in range(0, tm, SUB):
            rows = slice(r0, r0 + SUB)
            vv = v_r[rows, :]
            mu = jnp.mean(vv, axis=-1, keepdims=True)
            vc = vv - mu
            rstd = lax.rsqrt(jnp.mean(vc * vc, axis=-1, keepdims=True) + EPS)
            nrm = vc * rstd
            ln = nrm * lg_r[...] + lb_r[...]
            sg = _sigmoid(ln)
            dln = ds_r[rows, :].astype(F32) * (sg * (1.0 + ln * (1.0 - sg)))
            acc[0] += _fold8(dln * nrm)
            acc[1] += _fold8(dln)
            dn = dln * lg_r[...]
            dv = rstd * (dn - jnp.mean(dn, axis=-1, keepdims=True)
                         - nrm * jnp.mean(dn * nrm, axis=-1, keepdims=True))
            acc[2] += _fold8(dv)
            dv_o[rows, :] = dv

        @pl.when(i == n - 1)
        def _():
            dlg_o[...] = jnp.sum(acc[0], axis=0, keepdims=True)
            dlb_o[...] = jnp.sum(acc[1], axis=0, keepdims=True)
            dcb_o[...] = jnp.sum(acc[2], axis=0, keepdims=True)

    return pl.pallas_call(
        body, name=name, grid=(n,),
        in_specs=[tile(), tile(), vec(), vec()],
        out_specs=[tile(), vec(), vec(), vec()],
        out_shape=[jax.ShapeDtypeStruct((s, d), F32)] + [jax.ShapeDtypeStruct((1, d), F32)] * 3,
        scratch_shapes=[pltpu.VMEM((3, 8, d), F32)],
        compiler_params=_params("arbitrary"),
    )(v, ds, lg.reshape(1, d), lb.reshape(1, d))


def _mix_bwd(proj, dpa, dv, dp, dproj, conv_a, conv_b, name):
    s, d9 = proj.shape
    d = d9 // 9
    gc = d // N_GROUPS
    tm = _tile(s, 256, HALO)
    n = s // tm
    nh = s // HALO

    def tile(col=0):
        return pl.BlockSpec((tm, d), lambda i: (i, col))

    def vec(rows):
        return pl.BlockSpec((rows, d), lambda i: (0, 0))

    def body(ab, ac, ax, bv, bg, ac_h, ax_h, bv_h, bg_h, dpa_r, dv_r, dp_r, ab_n, dpa_n, dv_n, dp_n, wa, wb, _alias,
             dpj_o, dbin_o, dca_o, dcb_o, ua_e, ub_e, dz_e, dv_e, q_e, bacc, cacc_a, cacc_b, ub_s, dv_s):
        i = pl.program_id(0)

        @pl.when(i == 0)
        def _():
            bacc[...] = jnp.zeros_like(bacc)
            cacc_a[...] = jnp.zeros_like(cacc_a)
            cacc_b[...] = jnp.zeros_like(cacc_b)

        keep_p = (i > 0).astype(F32)
        keep_n = (i < n - 1).astype(F32)
        ua_e[0:HALO, :] = _f32(ac_h) * _f32(ax_h) * keep_p
        ua_e[HALO:HALO + tm, :] = _f32(ac) * _f32(ax)
        ub_e[0:HALO, :] = _f32(bv_h) * _sigmoid(_f32(bg_h)) * keep_p
        ub_e[HALO:HALO + tm, :] = _f32(bv) * _sigmoid(_f32(bg))
        dz_e[0:tm, :] = _f32(dpa_r) * _f32(ab)
        dz_e[tm:tm + HALO, :] = _f32(dpa_n) * _f32(ab_n) * keep_n
        dv_e[0:tm, :] = dv_r[...]
        dv_e[tm:tm + HALO, :] = dv_n[...] * keep_n
        _shift_copies(ub_e, ub_s)
        _shift_copies(dv_e, dv_s)
        for g, w in enumerate(POOL_WINDOWS):
            cols = slice(g * gc, (g + 1) * gc)
            t_idx = i * tm + lax.broadcasted_iota(jnp.int32, (tm + HALO, gc), 0)
            cnt = jnp.minimum(t_idx + 1, w).astype(F32)
            q_e[0:tm, cols] = dp_r[:, cols].astype(F32) / cnt[0:tm]
            q_e[tm:tm + HALO, cols] = dp_n[:, cols].astype(F32) * keep_n / cnt[tm:tm + HALO]
        for r0 in range(0, tm, SUB):
            rows = slice(r0, r0 + SUB)
            dz = dz_e[pl.ds(r0, SUB), :]
            z = None
            du = None
            for k in range(K_A):
                u_k = ua_e[pl.ds(HALO - (K_A - 1) + k + r0, SUB), :]
                t = wa[k:k + 1, :] * u_k
                z = t if z is None else z + t
                cacc_a[k] += _fold8(dz * u_k)
                t = wa[k:k + 1, :] * dz_e[pl.ds(r0 + (K_A - 1) - k, SUB), :]
                du = t if du is None else du + t
            outs = [dpa_r[rows, :].astype(F32) * z, du * ax[rows, :].astype(F32), du * ac[rows, :].astype(F32)]
            dvv = dv_e[pl.ds(r0, SUB), :]
            du = None
            for k in range(K_B):
                cacc_b[k] += _fold8(dvv * _window(ub_e, ub_s, HALO - (K_B - 1) + k + r0, SUB))
                t = wb[k:k + 1, :] * _window(dv_e, dv_s, r0 + (K_B - 1) - k, SUB)
                du = t if du is None else du + t
            sg = _sigmoid(bg[rows, :].astype(F32))
            outs.append(du * sg)
            outs.append(du * bv[rows, :].astype(F32) * sg * (1.0 - sg))
            for b, o in enumerate(outs):
                dpj_o[rows, b * d:(b + 1) * d] = o.astype(BF16)
                bacc[b] += _fold8(o)
            for g, w in enumerate(POOL_WINDOWS):
                cols = slice(g * gc, (g + 1) * gc)
                acc = q_e[pl.ds(r0, SUB), cols]
                for j in range(1, w):
                    acc = acc + q_e[pl.ds(r0 + j, SUB), cols]
                o = acc - dp_r[rows, cols].astype(F32)
                dpj_o[rows, 5 * d + g * gc:5 * d + (g + 1) * gc] = o.astype(BF16)
                bacc[5, :, cols] += _fold8(o)

        @pl.when(i == n - 1)
        def _():
            for b in range(6):
                dbin_o[:, b * d:(b + 1) * d] = jnp.sum(bacc[b], axis=0, keepdims=True)
            dca_o[...] = jnp.sum(cacc_a[...], axis=1)
            dcb_o[...] = jnp.sum(cacc_b[...], axis=1)

    ka8 = 8
    kb8 = 32
    return pl.pallas_call(
        body, name=name, grid=(n,),
        in_specs=[tile(0), tile(1), tile(2), tile(3), tile(4),
                  _prev_halo(tm, d, 1), _prev_halo(tm, d, 2), _prev_halo(tm, d, 3), _prev_halo(tm, d, 4),
                  tile(), tile(), tile(),
                  _next_halo(tm, d, 0, nh), _next_halo(tm, d, 0, nh), _next_halo(tm, d, 0, nh), _next_halo(tm, d, 0, nh),
                  vec(K_A), vec(K_B), ANY],
        out_specs=[pl.BlockSpec((tm, 6 * d), lambda i: (i, 0)), pl.BlockSpec((1, 6 * d), lambda i: (0, 0)),
                   pl.BlockSpec((ka8, d), lambda i: (0, 0)), pl.BlockSpec((kb8, d), lambda i: (0, 0))],
        out_shape=[jax.ShapeDtypeStruct((s, 9 * d), BF16), jax.ShapeDtypeStruct((1, 6 * d), F32),
                   jax.ShapeDtypeStruct((ka8, d), F32), jax.ShapeDtypeStruct((kb8, d), F32)],
        scratch_shapes=[pltpu.VMEM((HALO + tm, d), F32), pltpu.VMEM((HALO + tm, d), F32),
                        pltpu.VMEM((tm + HALO, d), F32), pltpu.VMEM((tm + HALO, d), F32),
                        pltpu.VMEM((tm + HALO, d), F32),
                        pltpu.VMEM((6, 8, d), F32), pltpu.VMEM((ka8, 8, d), F32), pltpu.VMEM((kb8, 8, d), F32),
                        pltpu.VMEM((7, tm + HALO - 8, d), F32), pltpu.VMEM((7, tm + HALO - 8, d), F32)],
        input_output_aliases={18: 0},
        compiler_params=_params("arbitrary"),
    )(proj, proj, proj, proj, proj, proj, proj, proj, proj, dpa, dv, dp, proj, dpa, dv, dp, conv_a, conv_b, dproj)


def _dh_rms_bwd(dproj, wint, x0, dx1, g, name):
    s, p = dproj.shape
    d = x0.shape[1]
    tm = _tile(s, 1024)
    tk = _tile(p, 1536, 128)
    n = s // tm
    nk = p // tk

    def body(dp_r, w_r, x_r, dx1_r, g_r, dx_o, dg_o, acc, gacc):
        i = pl.program_id(0)
        k = pl.program_id(1)

        @pl.when((i == 0) & (k == 0))
        def _():
            gacc[...] = jnp.zeros_like(gacc)

        @pl.when(k == 0)
        def _():
            acc[...] = jnp.zeros_like(acc)

        acc[...] += _dot(dp_r[...], w_r[...])

        @pl.when(k == nk - 1)
        def _():
            dx, dg = _rms_bwd(acc[...], x_r[...], g_r[...])
            gacc[...] += dg
            dx_o[...] = dx1_r[...] + dx

        @pl.when((i == n - 1) & (k == nk - 1))
        def _():
            dg_o[...] = jnp.sum(gacc[...], axis=0, keepdims=True)

    return pl.pallas_call(
        body, name=name, grid=(n, nk),
        in_specs=[pl.BlockSpec((tm, tk), lambda i, k: (i, k)), pl.BlockSpec((tk, d), lambda i, k: (k, 0)),
                  pl.BlockSpec((tm, d), lambda i, k: (i, 0)), pl.BlockSpec((tm, d), lambda i, k: (i, 0)),
                  pl.BlockSpec((1, d), lambda i, k: (0, 0))],
        out_specs=[pl.BlockSpec((tm, d), lambda i, k: (i, 0)), pl.BlockSpec((1, d), lambda i, k: (0, 0))],
        out_shape=[jax.ShapeDtypeStruct((s, d), F32), jax.ShapeDtypeStruct((1, d), F32)],
        scratch_shapes=[pltpu.VMEM((tm, d), F32), pltpu.VMEM((8, d), F32)],
        compiler_params=_params("arbitrary", "arbitrary"),
    )(dproj, wint, x0, dx1, g.reshape(1, d))


def _layer_fwd(x0, w, l):
    h = _rms_fwd(x0, w["g_mix"], f"rms_fwd_{l}")
    proj = _proj(h, w["win3"], w["b_in"], f"proj_{l}")
    pa, sb, p, v = _mix_fwd(proj, w["conv_a"], w["conv_b"], w["conv_b_bias"], w["ln_b_g"], w["ln_b_b"], f"mix_fwd_{l}")
    ya, yb, yc, mg, x1, h2 = _branch_out(pa, sb, p, proj, x0, w["woa"], w["wob"], w["wp"], w["wo"], w["b_out_b"],
                                         w["pool_scale"], w["g_mlp"], f"branch_out_{l}")
    f, a, x2 = _mlp_fwd(h2, x1, w["w1_3"], w["w2"], f"mlp_fwd_{l}")
    saved = dict(x0=x0, h=h, proj=proj, pa=pa, sb=sb, p=p, v=v, ya=ya, yb=yb, yc=yc, mg=mg, x1=x1, h2=h2, f=f, a=a)
    return x2, saved


def _layer_bwd(dx2, w, sv, l):
    d = dx2.shape[1]
    df, dx1, dgm = _mlp_bwd(dx2, sv["f"], sv["x1"], w["w2t"], w["w1t"], w["g_mlp"], f"mlp_bwd_{l}")
    dw2 = _tn(sv["a"], dx2, 1, f"dw2_{l}")
    dw1 = _tn(sv["h2"], df, N_CHIPS, f"dw1_{l}")
    dpa, ds, dp, dproj, dya, dyb, dyc, dbob, dps, dbg = _branch_out_bwd(
        dx1, sv["proj"], sv["ya"], sv["yb"], sv["yc"], w["wot"], w["woat"], w["wobt"], w["wpt"], w["pool_scale"],
        f"branch_out_bwd_{l}")
    dwo = _tn(sv["mg"], dx1, 1, f"dwo_{l}")
    dwoa = _tn(sv["pa"], dya, 1, f"dwoa_{l}")
    dwob = _tn(sv["sb"], dyb, 1, f"dwob_{l}")
    dwp = _tn_groups(sv["p"], dyc, f"dwp_{l}")
    dv, dlg, dlb, dcb = _ln_silu_bwd(sv["v"], ds, w["ln_b_g"], w["ln_b_b"], f"ln_silu_bwd_{l}")
    dproj, dbm, dca, dcvb = _mix_bwd(sv["proj"], dpa, dv, dp, dproj, w["conv_a"], w["conv_b"], f"mix_bwd_{l}")
    dwin = _tn(sv["h"], dproj, N_CHIPS, f"dwin_{l}")
    dx0, dgmix = _dh_rms_bwd(dproj, w["wint"], sv["x0"], dx1, w["g_mix"], f"dh_{l}")
    ff = dw2.shape[1]
    gc = d // N_GROUPS
    big = dict(
        w_in=dwin,
        w_out_a=dwoa.reshape(N_CHIPS, d // N_CHIPS, d),
        w_out_b=dwob.reshape(N_CHIPS, d // N_CHIPS, d),
        w_pool=dwp.reshape(N_GROUPS, N_CHIPS, gc // N_CHIPS, gc).transpose(1, 0, 2, 3).reshape(N_CHIPS, gc, gc),
        w_o=dwo.reshape(N_CHIPS, d // N_CHIPS, d),
        w_mlp1=dw1,
        w_mlp2=dw2.reshape(N_CHIPS, ff // N_CHIPS, d),
    )
    small = dict(
        g_mix=dgmix, b_in=jnp.concatenate([dbm, dbg], axis=1).reshape(9, d), conv_a=dca[:K_A], conv_b=dcvb[:K_B],
        conv_b_bias=dcb, ln_b_g=dlg, ln_b_b=dlb, b_out_b=dbob, pool_scale=dps, g_mlp=dgm,
    )
    return dx0, big, small


def _local_step(x, target, ws, g_final):
    saved = []
    for l, w in enumerate(ws):
        x, sv = _layer_fwd(x, w, l)
        saved.append(sv)
    dx, loss, dgf = _loss_head(x, target, g_final, "loss_head")
    grads = [None] * len(ws)
    for l in reversed(range(len(ws))):
        dx, big, small = _layer_bwd(dx, ws[l], saved[l], l)
        grads[l] = (big, small)
    return loss, dx, grads, dgf


def _place():
    x, y, c = lax.axis_index("x"), lax.axis_index("y"), lax.axis_index("c")
    chips = [(1 - x, y), (x, 1 - y), (1 - x, 1 - y)]
    return x, y, c, chips


def _place_shard(w, chip_arr, dtype, name):
    rows, cols = w.shape
    tr = _tile(rows, max(16, (1 << 19) // cols), 16)

    def body(k_ref, w_r, o_r):
        o_r[...] = w_r[...].astype(dtype)

    return pl.pallas_call(
        body, name=name,
        grid_spec=pltpu.PrefetchScalarGridSpec(
            num_scalar_prefetch=1, grid=(rows // tr,),
            in_specs=[pl.BlockSpec((tr, cols), lambda i, k_ref: (i, 0))],
            out_specs=pl.BlockSpec((None, tr, cols), lambda i, k_ref: (k_ref[0], i, 0))),
        out_shape=jax.ShapeDtypeStruct((N_CHIPS, rows, cols), dtype),
        compiler_params=_params("parallel"),
    )(chip_arr, w)


def _gather_shards(placed, name):
    n = len(placed)

    def body(*refs):
        ins, outs = refs[:n], refs[n:2 * n]
        send, recv, fsend, frecv = refs[2 * n:]
        x, y, c, chips = _place()
        me = 2 * x + y
        sib = (x, y, 1 - c)

        def half(a, which):
            h = ins[a].shape[1] // 2
            return pl.ds(pl.multiple_of(which * h, 8), h)

        def ici(a, j, chip, src_chip):
            return pltpu.make_async_remote_copy(
                src_ref=ins[a].at[me, half(a, c)], dst_ref=outs[a].at[src_chip, half(a, c)],
                send_sem=send.at[a * 3 + j], recv_sem=recv.at[a * 3 + j], device_id=(*chip, c), device_id_type=MESH)

        def d2d(a, j, src_chip, which):
            rows = outs[a].at[src_chip, half(a, which)]
            return pltpu.make_async_remote_copy(
                src_ref=rows, dst_ref=rows, send_sem=fsend.at[a * 3 + j], recv_sem=frecv.at[a * 3 + j],
                device_id=sib, device_id_type=MESH)

        for a in range(n):
            for j, chip in enumerate(chips):
                ici(a, j, chip, me).start()
        for a in range(n):
            for j, (px, py) in enumerate(chips):
                ici(a, j, (px, py), 2 * px + py).wait_recv()
                d2d(a, j, 2 * px + py, c).start()
        for a in range(n):
            for j, (px, py) in enumerate(chips):
                d2d(a, j, 2 * px + py, 1 - c).wait_recv()
        for a in range(n):
            for j, (px, py) in enumerate(chips):
                ici(a, j, (px, py), me).wait_send()
                d2d(a, j, 2 * px + py, c).wait_send()

    return pl.pallas_call(
        body, name=name,
        in_specs=[ANY] * n, out_specs=[ANY] * n,
        out_shape=[jax.ShapeDtypeStruct(a.shape, a.dtype) for a in placed],
        scratch_shapes=[pltpu.SemaphoreType.DMA((3 * n,))] * 4,
        input_output_aliases={a: a for a in range(n)},
    )(*placed)


def _gather_rows(v, name):
    def body(v_ref, o_ref, send, recv, lsem):
        x, y, c = lax.axis_index("x"), lax.axis_index("y"), lax.axis_index("c")
        me = 4 * x + 2 * y + c
        local = pltpu.make_async_copy(v_ref, o_ref.at[me], lsem)
        local.start()

        def copy(m, src_dev):
            peer = (x ^ ((m >> 2) & 1), y ^ ((m >> 1) & 1), c ^ (m & 1))
            return pltpu.make_async_remote_copy(
                src_ref=v_ref, dst_ref=o_ref.at[src_dev], send_sem=send.at[m - 1], recv_sem=recv.at[m - 1],
                device_id=peer, device_id_type=MESH)

        for m in range(1, N_DEV):
            copy(m, me).start()
        for m in range(1, N_DEV):
            copy(m, me ^ m).wait_recv()
        for m in range(1, N_DEV):
            copy(m, me).wait_send()
        local.wait()

    return pl.pallas_call(
        body, name=name, in_specs=[ANY], out_specs=ANY,
        out_shape=jax.ShapeDtypeStruct((N_DEV, *v.shape), v.dtype),
        scratch_shapes=[pltpu.SemaphoreType.DMA((N_DEV - 1,)), pltpu.SemaphoreType.DMA((N_DEV - 1,)),
                        pltpu.SemaphoreType.DMA],
    )(v)


def _swap_halves(grads, name):
    n = len(grads)

    def body(*refs):
        ins, outs = refs[:n], refs[n:2 * n]
        send, recv = refs[2 * n:]
        x, y, c, _ = _place()
        sib = (x, y, 1 - c)

        def copy(a, k):
            h = ins[a].shape[1] // 2
            return pltpu.make_async_remote_copy(
                src_ref=ins[a].at[k, pl.ds(pl.multiple_of((1 - c) * h, 8), h)], dst_ref=outs[a].at[k],
                send_sem=send.at[a * N_CHIPS + k], recv_sem=recv.at[a * N_CHIPS + k], device_id=sib, device_id_type=MESH)

        for a in range(n):
            for k in range(N_CHIPS):
                copy(a, k).start()
        for a in range(n):
            for k in range(N_CHIPS):
                copy(a, k).wait()

    return pl.pallas_call(
        body, name=name, in_specs=[ANY] * n, out_specs=[ANY] * n,
        out_shape=[jax.ShapeDtypeStruct((N_CHIPS, g.shape[1] // 2, g.shape[2]), g.dtype) for g in grads],
        scratch_shapes=[pltpu.SemaphoreType.DMA((N_CHIPS * n,))] * 2,
    )(*grads)


def _add_halves(g, r, c_arr, name):
    nk, rows, cols = g.shape
    h = rows // 2
    tr = _tile(h, max(8, (1 << 19) // cols), 8)
    nblk = h // tr

    def body(c_ref, g_r, r_r, o_r):
        o_r[...] = (g_r[...] + r_r[...]).astype(BF16)

    return pl.pallas_call(
        body, name=name,
        grid_spec=pltpu.PrefetchScalarGridSpec(
            num_scalar_prefetch=1, grid=(nk, nblk),
            in_specs=[pl.BlockSpec((None, tr, cols), lambda k, i, c_ref: (k, c_ref[0] * nblk + i, 0)),
                      pl.BlockSpec((None, tr, cols), lambda k, i, c_ref: (k, i, 0))],
            out_specs=pl.BlockSpec((None, tr, cols), lambda k, i, c_ref: (k, i, 0))),
        out_shape=jax.ShapeDtypeStruct((nk, h, cols), BF16),
        compiler_params=_params("parallel", "parallel"),
    )(c_arr, g, r)


def _exchange_chips(parts, name):
    n = len(parts)

    def body(*refs):
        ins, outs = refs[:n], refs[n:2 * n]
        send, recv, lsem = refs[2 * n:]
        x, y, c, chips = _place()
        me = 2 * x + y

        def copy(a, j, to_chip, src_chip):
            return pltpu.make_async_remote_copy(
                src_ref=ins[a].at[2 * to_chip[0] + to_chip[1]], dst_ref=outs[a].at[src_chip],
                send_sem=send.at[a * 3 + j], recv_sem=recv.at[a * 3 + j], device_id=(*to_chip, c), device_id_type=MESH)

        local = [pltpu.make_async_copy(ins[a].at[me], outs[a].at[me], lsem.at[a]) for a in range(n)]
        for a in range(n):
            local[a].start()
            for j, chip in enumerate(chips):
                copy(a, j, chip, me).start()
        for a in range(n):
            for j, (px, py) in enumerate(chips):
                copy(a, j, (px, py), 2 * px + py).wait_recv()
        for a in range(n):
            local[a].wait()
            for j, chip in enumerate(chips):
                copy(a, j, chip, me).wait_send()

    return pl.pallas_call(
        body, name=name, in_specs=[ANY] * n, out_specs=[ANY] * n,
        out_shape=[jax.ShapeDtypeStruct(p.shape, p.dtype) for p in parts],
        scratch_shapes=[pltpu.SemaphoreType.DMA((3 * n,))] * 2 + [pltpu.SemaphoreType.DMA((n,))],
    )(*parts)


def _sum_chips(r, name):
    nk, rows, cols = r.shape
    tr = _tile(rows, max(8, (1 << 19) // cols), 8)

    def body(r_r, o_r):
        acc = r_r[0].astype(F32)
        for k in range(1, nk):
            acc = acc + r_r[k].astype(F32)
        o_r[...] = acc

    return pl.pallas_call(
        body, name=name, grid=(rows // tr,),
        in_specs=[pl.BlockSpec((nk, tr, cols), lambda i: (0, i, 0))],
        out_specs=pl.BlockSpec((tr, cols), lambda i: (i, 0)),
        out_shape=jax.ShapeDtypeStruct((rows, cols), F32),
        compiler_params=_params("parallel"),
    )(r)


def _sum_chips_into(r, dst, layer, n_layers, c_arr, name):
    nk, h, cols = r.shape
    tr = _tile(h, max(8, (1 << 19) // cols), 8)
    nblk = h // tr

    def body(c_ref, r_r, *rest):
        o_r = rest[-1]
        acc = r_r[0].astype(F32)
        for k in range(1, nk):
            acc = acc + r_r[k].astype(F32)
        o_r[...] = acc

    in_specs = [pl.BlockSpec((nk, tr, cols), lambda i, c_ref: (0, i, 0))]
    args = [c_arr, r]
    aliases = {}
    if dst is not None:
        in_specs.append(ANY)
        args.append(dst)
        aliases = {2: 0}
    return pl.pallas_call(
        body, name=name,
        grid_spec=pltpu.PrefetchScalarGridSpec(
            num_scalar_prefetch=1, grid=(nblk,), in_specs=in_specs,
            out_specs=pl.BlockSpec((None, tr, cols), lambda i, c_ref: (layer, c_ref[0] * nblk + i, 0))),
        out_shape=jax.ShapeDtypeStruct((n_layers, 2 * h, cols), F32),
        input_output_aliases=aliases,
        compiler_params=_params("parallel"),
    )(*args)


def _share_halves(shards, name):
    n = len(shards)
    n_layers = shards[0].shape[0]

    def body(*refs):
        ins, outs = refs[:n], refs[n:2 * n]
        send, recv = refs[2 * n:]
        x, y, c, _ = _place()
        sib = (x, y, 1 - c)

        def copy(a, l, which):
            h = ins[a].shape[1] // 2
            rows = pl.ds(pl.multiple_of(which * h, 8), h)
            return pltpu.make_async_remote_copy(
                src_ref=ins[a].at[l, rows], dst_ref=outs[a].at[l, rows], send_sem=send.at[a * n_layers + l],
                recv_sem=recv.at[a * n_layers + l], device_id=sib, device_id_type=MESH)

        for a in range(n):
            for l in range(n_layers):
                copy(a, l, c).start()
        for a in range(n):
            for l in range(n_layers):
                copy(a, l, 1 - c).wait_recv()
        for a in range(n):
            for l in range(n_layers):
                copy(a, l, c).wait_send()

    return pl.pallas_call(
        body, name=name, in_specs=[ANY] * n, out_specs=[ANY] * n,
        out_shape=[jax.ShapeDtypeStruct(a.shape, a.dtype) for a in shards],
        scratch_shapes=[pltpu.SemaphoreType.DMA((n * n_layers,))] * 2,
        input_output_aliases={a: a for a in range(n)},
    )(*shards)


def _adamw(w, g, m, v, name):
    rows, cols = w.shape
    tr = _tile(rows, max(8, (1 << 18) // cols), 8)
    c1 = 1.0 - ADAM_B1 ** ADAM_STEP
    c2 = 1.0 - ADAM_B2 ** ADAM_STEP

    def body(w_r, g_r, m_r, v_r, d_o, m_o, v_o):
        gv = g_r[...]
        mn = ADAM_B1 * m_r[...] + (1.0 - ADAM_B1) * gv
        vn = ADAM_B2 * v_r[...] + (1.0 - ADAM_B2) * (gv * gv)
        m_o[...] = mn
        v_o[...] = vn
        d_o[...] = -ADAM_LR * ((mn / c1) / (jnp.sqrt(vn / c2) + ADAM_EPS) + ADAM_WD * w_r[...])

    spec = pl.BlockSpec((tr, cols), lambda i: (i, 0))
    return pl.pallas_call(
        body, name=name, grid=(rows // tr,), in_specs=[spec] * 4, out_specs=[spec] * 3,
        out_shape=[jax.ShapeDtypeStruct((rows, cols), F32)] * 3,
        compiler_params=_params("parallel"),
    )(w, g, m, v)


BIG = ("w_in", "w_out_a", "w_out_b", "w_pool", "w_o", "w_mlp1", "w_mlp2")
SMALL = ("g_mix", "b_in", "conv_b_bias", "ln_b_g", "ln_b_b", "b_out_b", "pool_scale", "g_mlp")
CONVS = ("conv_a", "conv_b")
WEIGHTS = ("g_mix", "w_in", "b_in", "conv_a", "w_out_a", "conv_b", "conv_b_bias", "ln_b_g", "ln_b_b", "w_out_b", "b_out_b",
           "w_pool", "pool_scale", "w_o", "g_mlp", "w_mlp1", "w_mlp2", "g_final")


def _as2d(a):
    return a.reshape(-1, a.shape[-1])


def _pad_rows(a, rows):
    return jnp.pad(a, ((0, rows - a.shape[0]), (0, 0)))


def _full_weights(big_g, conv_g, rep, l, d):
    gc = d // N_GROUPS
    w = {k: v[l] for k, v in rep.items()}
    w["b_in"] = rep["b_in"][l].reshape(1, -1)
    win3 = big_g["w_in"][l]
    w["win3"] = win3
    w["wint"] = win3.transpose(0, 2, 1).reshape(-1, d)
    for src, dst in (("w_out_a", "woa"), ("w_out_b", "wob"), ("w_o", "wo")):
        full = big_g[src][l].reshape(d, d)
        w[dst] = full
        w[dst + "t"] = full.T
    wp = big_g["w_pool"][l].reshape(N_CHIPS, N_GROUPS, gc // N_CHIPS, gc).transpose(1, 0, 2, 3).reshape(N_GROUPS, gc, gc)
    w["wp"] = wp
    w["wpt"] = wp.transpose(0, 2, 1)
    w1_3 = big_g["w_mlp1"][l]
    w["w1_3"] = w1_3
    w["w1t"] = w1_3.transpose(0, 2, 1).reshape(-1, d)
    w2 = big_g["w_mlp2"][l].reshape(-1, d)
    w["w2"] = w2
    w["w2t"] = w2.T
    ca, cb = conv_g[l]
    w["conv_a"] = ca
    w["conv_b"] = cb
    return w


def kernel(x, g_mix, w_in, b_in, conv_a, w_out_a, conv_b, conv_b_bias, ln_b_g, ln_b_b, w_out_b, b_out_b, w_pool, pool_scale, w_o, g_mlp, w_mlp1, w_mlp2, g_final, loss_target, m_g_mix, m_w_in, m_b_in, m_conv_a, m_w_out_a, m_conv_b, m_conv_b_bias, m_ln_b_g, m_ln_b_b, m_w_out_b, m_b_out_b, m_w_pool, m_pool_scale, m_w_o, m_g_mlp, m_w_mlp1, m_w_mlp2, m_g_final, v_g_mix, v_w_in, v_b_in, v_conv_a, v_w_out_a, v_conv_b, v_conv_b_bias, v_ln_b_g, v_ln_b_b, v_w_out_b, v_b_out_b, v_w_pool, v_pool_scale, v_w_o, v_g_mlp, v_w_mlp1, v_w_mlp2, v_g_final):
    given = dict(g_mix=g_mix, w_in=w_in, b_in=b_in, conv_a=conv_a, w_out_a=w_out_a, conv_b=conv_b, conv_b_bias=conv_b_bias, ln_b_g=ln_b_g, ln_b_b=ln_b_b, w_out_b=w_out_b, b_out_b=b_out_b, w_pool=w_pool, pool_scale=pool_scale, w_o=w_o, g_mlp=g_mlp, w_mlp1=w_mlp1, w_mlp2=w_mlp2, g_final=g_final)
    mom = dict(g_mix=m_g_mix, w_in=m_w_in, b_in=m_b_in, conv_a=m_conv_a, w_out_a=m_w_out_a, conv_b=m_conv_b, conv_b_bias=m_conv_b_bias, ln_b_g=m_ln_b_g, ln_b_b=m_ln_b_b, w_out_b=m_w_out_b, b_out_b=m_b_out_b, w_pool=m_w_pool, pool_scale=m_pool_scale, w_o=m_w_o, g_mlp=m_g_mlp, w_mlp1=m_w_mlp1, w_mlp2=m_w_mlp2, g_final=m_g_final)
    var = dict(g_mix=v_g_mix, w_in=v_w_in, b_in=v_b_in, conv_a=v_conv_a, w_out_a=v_w_out_a, conv_b=v_conv_b, conv_b_bias=v_conv_b_bias, ln_b_g=v_ln_b_g, ln_b_b=v_ln_b_b, w_out_b=v_w_out_b, b_out_b=v_b_out_b, w_pool=v_w_pool, pool_scale=v_pool_scale, w_o=v_w_o, g_mlp=v_g_mlp, w_mlp1=v_w_mlp1, w_mlp2=v_w_mlp2, g_final=v_g_final)
    n_layers = w_in.shape[0]
    s, d = x.shape[1], x.shape[2]
    dq = d // N_CHIPS
    x_idx, y_idx, c_idx = lax.axis_index("x"), lax.axis_index("y"), lax.axis_index("c")
    chip = 2 * x_idx + y_idx
    c_arr = c_idx.astype(jnp.int32).reshape(1)

    conv_rows = n_layers * (K_A + K_B)
    conv_pad = -(-conv_rows // 16) * 16
    conv_pack = _pad_rows(jnp.concatenate([conv_a[l] for l in range(n_layers)] + [conv_b[l] for l in range(n_layers)], axis=0),
                          conv_pad)
    chip_arr = chip.astype(jnp.int32).reshape(1)
    placed = [_place_shard(_as2d(given[k][l]), chip_arr, BF16, f"place_{k}_{l}") for l in range(n_layers) for k in BIG]
    placed.append(_place_shard(conv_pack, chip_arr, F32, "place_convs"))
    gathered = _gather_shards(placed, "gather_weights")
    big_g = {k: [gathered[l * len(BIG) + i] for l in range(n_layers)] for i, k in enumerate(BIG)}
    conv_full = gathered[-1].transpose(1, 0, 2).reshape(conv_pad, d)
    conv_g = [(conv_full[l * K_A:(l + 1) * K_A], conv_full[n_layers * K_A + l * K_B:n_layers * K_A + (l + 1) * K_B])
              for l in range(n_layers)]
    rep = {k: given[k] for k in SMALL}
    ws = [_full_weights(big_g, conv_g, rep, l, d) for l in range(n_layers)]

    loss, dx, grads, dgf = _local_step(x[0], loss_target[0], ws, g_final)
    loss = lax.psum(loss[0, 0], ("x", "y", "c"))

    big_list = [grads[l][0][k] for l in range(n_layers) for k in BIG]
    swapped = _swap_halves(big_list, "grad_swap_halves")
    parts = [_add_halves(g, r, c_arr, f"grad_add_halves_{a}") for a, (g, r) in enumerate(zip(big_list, swapped))]
    arrived = _exchange_chips(parts, "grad_exchange_chips")
    reduced = []
    for i, k in enumerate(BIG):
        dst = None
        for l in range(n_layers):
            dst = _sum_chips_into(arrived[l * len(BIG) + i], dst, l, n_layers, c_arr, f"grad_sum_chips_{k}_{l}")
        reduced.append(dst)
    big_grad = dict(zip(BIG, _share_halves(reduced, "grad_share_halves")))

    small_rows = []
    for k in SMALL + CONVS:
        for l in range(n_layers):
            small_rows.append(grads[l][1][k])
    small_rows.append(dgf)
    pack = jnp.concatenate(small_rows, axis=0)
    n_small = pack.shape[0]
    pack = _pad_rows(pack, -(-n_small // 8) * 8)
    small_sum = _sum_chips(_gather_rows(pack, "gather_small_grads"), "sum_small_grads")

    out = {}
    for k in BIG:
        shape = given[k].shape
        g2 = big_grad[k].reshape(-1, shape[-1])
        dlt, nm, nv = _adamw(given[k].reshape(g2.shape), g2, mom[k].reshape(g2.shape), var[k].reshape(g2.shape), f"adamw_{k}")
        out[k] = tuple(a.reshape(shape) for a in (g2, dlt, nm, nv))

    def rows_of(k):
        return {"b_in": 9, "conv_a": K_A, "conv_b": K_B}.get(k, 1)

    def pack_rep(src):
        rows = [src[k].reshape(n_layers * rows_of(k), d) for k in SMALL] + [src["g_final"].reshape(1, d)]
        a = jnp.concatenate(rows, axis=0)
        return _pad_rows(a, -(-a.shape[0] // 8) * 8)

    def pack_conv(src):
        a = jnp.concatenate([src[k].reshape(n_layers * rows_of(k), dq) for k in CONVS], axis=0)
        return _pad_rows(a, -(-a.shape[0] // 8) * 8)

    offs = {}
    r = 0
    for k in SMALL + CONVS:
        offs[k] = r
        r += n_layers * rows_of(k)
    offs["g_final"] = r
    n_rep = sum(n_layers * rows_of(k) for k in SMALL)
    g_rep = jnp.concatenate([small_sum[:n_rep], small_sum[offs["g_final"]:offs["g_final"] + 1]], axis=0)
    g_rep = _pad_rows(g_rep, -(-g_rep.shape[0] // 8) * 8)
    g_conv_full = small_sum[offs["conv_a"]:offs["g_final"]]
    g_conv = lax.dynamic_slice_in_dim(g_conv_full, chip * dq, dq, axis=1)
    g_conv = _pad_rows(g_conv, -(-g_conv.shape[0] // 8) * 8)
    rep_res = (g_rep,) + tuple(_adamw(pack_rep(given), g_rep, pack_rep(mom), pack_rep(var), "adamw_small"))
    conv_res = (g_conv,) + tuple(_adamw(pack_conv(given), g_conv, pack_conv(mom), pack_conv(var), "adamw_conv"))
    r = 0
    for k in SMALL:
        nr = n_layers * rows_of(k)
        out[k] = tuple(a[r:r + nr].reshape(given[k].shape) for a in rep_res)
        r += nr
    out["g_final"] = tuple(a[r].reshape(given["g_final"].shape) for a in rep_res)
    r = 0
    for k in CONVS:
        nr = n_layers * rows_of(k)
        out[k] = tuple(a[r:r + nr].reshape(given[k].shape) for a in conv_res)
        r += nr

    res = [loss, dx.reshape(x.shape)]
    for i in range(4):
        res += [out[k][i] for k in WEIGHTS]
    return tuple(res)
```

```python
import functools

import jax
import jax.numpy as jnp
from jax import lax
from jax.experimental import pallas as pl
from jax.experimental.pallas import tpu as pltpu

F32 = jnp.float32
BF16 = jnp.bfloat16
EPS = 1e-6
HALO = 32
SUB = 32
K_A = 3
K_B = 31
POOL_WINDOWS = (2, 4, 8, 16)
N_GROUPS = len(POOL_WINDOWS)
N_CHIPS = 4
N_DEV = 8
ADAM_LR = 0.001
ADAM_B1 = 0.9
ADAM_B2 = 0.999
ADAM_EPS = 1e-08
ADAM_WD = 0.01
ADAM_STEP = 10
VMEM_LIMIT = 56 * 1024 * 1024
MESH = pl.DeviceIdType.MESH
ANY = pl.BlockSpec(memory_space=pl.ANY)


def _params(*sem):
    return pltpu.CompilerParams(dimension_semantics=sem, vmem_limit_bytes=VMEM_LIMIT)


def _tile(n, pref, mult=8):
    if n <= pref:
        return n
    t = (pref // mult) * mult
    while t > mult and n % t:
        t -= mult
    assert n % t == 0, (n, pref, mult)
    return t


def _dot(a, b):
    return jnp.dot(a, b, preferred_element_type=F32)


def _sigmoid(x):
    return 1.0 / (1.0 + jnp.exp(-x))


def _f32(ref):
    return ref[...].astype(F32)


def _fold8(v):
    r, d = v.shape
    return v.reshape(r // 8, 8, d).sum(axis=0)


def _rms_bwd(dh, xv, g):
    r = lax.rsqrt(jnp.mean(xv * xv, axis=-1, keepdims=True) + EPS)
    xn = xv * r
    dxn = dh * g
    dx = r * (dxn - xn * jnp.mean(dxn * xn, axis=-1, keepdims=True))
    return dx, _fold8(dh * xn)


class _Rider:
    def __init__(self, args, out_shape, aliases, sems, start, finish):
        self.args, self.out_shape, self.aliases, self.sems = list(args), list(out_shape), list(aliases), list(sems)
        self.start, self.finish = start, finish


def _hosted_call(body, *, name, grid, in_specs, out_specs, out_shape, scratch_shapes, args, rider=None):
    n_in, n_out, n_scr = len(in_specs), len(out_shape), len(scratch_shapes)
    params = pltpu.CompilerParams(dimension_semantics=("arbitrary",) * len(grid), vmem_limit_bytes=VMEM_LIMIT)
    if rider is None:
        res = pl.pallas_call(body, name=name, grid=grid, in_specs=in_specs, out_specs=out_specs, out_shape=out_shape,
                             scratch_shapes=scratch_shapes, compiler_params=params)(*args)
        return list(res), []
    k_in, k_out = len(rider.args), len(rider.out_shape)

    def hosted(*refs):
        ins, refs = refs[:n_in], refs[n_in:]
        r_in, refs = refs[:k_in], refs[k_in:]
        outs, refs = refs[:n_out], refs[n_out:]
        r_out, refs = refs[:k_out], refs[k_out:]
        scr, r_sems = refs[:n_scr], refs[n_scr:]
        first = functools.reduce(lambda u, v: u & v, [pl.program_id(a) == 0 for a in range(len(grid))])
        last = functools.reduce(lambda u, v: u & v, [pl.program_id(a) == grid[a] - 1 for a in range(len(grid))])

        @pl.when(first)
        def _():
            rider.start(r_in, r_out, r_sems)

        body(*ins, *outs, *scr)

        @pl.when(last)
        def _():
            rider.finish(r_in, r_out, r_sems)

    res = pl.pallas_call(
        hosted, name=name, grid=grid, in_specs=list(in_specs) + [ANY] * k_in, out_specs=list(out_specs) + [ANY] * k_out,
        out_shape=list(out_shape) + rider.out_shape, scratch_shapes=list(scratch_shapes) + rider.sems,
        input_output_aliases={n_in + i: n_out + o for i, o in rider.aliases}, compiler_params=params,
    )(*args, *rider.args)
    return list(res[:n_out]), list(res[n_out:])


def _comm_call(rider, name):
    def body(*refs):
        k_in, k_out = len(rider.args), len(rider.out_shape)
        r_in, r_out, r_sems = refs[:k_in], refs[k_in:k_in + k_out], refs[k_in + k_out:]
        rider.start(r_in, r_out, r_sems)
        rider.finish(r_in, r_out, r_sems)

    return list(pl.pallas_call(
        body, name=name, in_specs=[ANY] * len(rider.args), out_specs=[ANY] * len(rider.out_shape),
        out_shape=rider.out_shape, scratch_shapes=rider.sems, input_output_aliases=dict(rider.aliases),
    )(*rider.args))


def _rms_fwd(x, g, name):
    s, d = x.shape
    tm = _tile(s, 512)

    def body(x_ref, g_ref, h_ref):
        xv = x_ref[...]
        r = lax.rsqrt(jnp.mean(xv * xv, axis=-1, keepdims=True) + EPS)
        h_ref[...] = (xv * r * g_ref[...]).astype(BF16)

    return pl.pallas_call(
        body, name=name, grid=(s // tm,),
        in_specs=[pl.BlockSpec((tm, d), lambda i: (i, 0)), pl.BlockSpec((1, d), lambda i: (0, 0))],
        out_specs=pl.BlockSpec((tm, d), lambda i: (i, 0)),
        out_shape=jax.ShapeDtypeStruct((s, d), BF16),
        compiler_params=_params("parallel"),
    )(x, g.reshape(1, d))


def _proj(h, w3, b, name, rider=None):
    s, k = h.shape
    nb, _, nw = w3.shape
    tm = _tile(s, 1024)
    tn = nw // 3
    per = nw // tn

    def body(h_ref, w_ref, b_ref, o_ref):
        o_ref[...] = (_dot(h_ref[...], w_ref[...]) + b_ref[...]).astype(BF16)

    (proj,), carried = _hosted_call(
        body, name=name, grid=(s // tm, nb * per),
        in_specs=[
            pl.BlockSpec((tm, k), lambda i, j: (i, 0)),
            pl.BlockSpec((None, k, tn), lambda i, j: (j // per, 0, j % per)),
            pl.BlockSpec((1, tn), lambda i, j: (0, j)),
        ],
        out_specs=[pl.BlockSpec((tm, tn), lambda i, j: (i, j))],
        out_shape=[jax.ShapeDtypeStruct((s, nb * nw), BF16)],
        scratch_shapes=[], args=(h, w3, b), rider=rider)
    return proj, carried


def _shift_copies(src_e, dst_s):
    n_rows = dst_s.shape[1]
    for b in range(1, 8):
        dst_s[b - 1, :, :] = src_e[pl.ds(b, n_rows), :]


def _window(src_e, dst_s, off, rows):
    b = off % 8
    if b == 0:
        return src_e[pl.ds(off, rows), :]
    return dst_s[b - 1, pl.ds(off - b, rows), :]


def _prev_halo(tm, d, col):
    r = tm // HALO
    return pl.BlockSpec((HALO, d), lambda i: (jnp.maximum(i * r - 1, 0), col))


def _next_halo(tm, d, col, n_halo_blocks):
    r = tm // HALO
    return pl.BlockSpec((HALO, d), lambda i: (jnp.minimum((i + 1) * r, n_halo_blocks - 1), col))


def _mix_fwd(proj, conv_a, conv_b, cb, lg, lb, name):
    s, d9 = proj.shape
    d = d9 // 9
    gc = d // N_GROUPS
    tm = _tile(s, 256, HALO)

    def tile(col):
        return pl.BlockSpec((tm, d), lambda i: (i, col))

    def vec(rows):
        return pl.BlockSpec((rows, d), lambda i: (0, 0))

    def body(ab, ac, ax, bv, bg, ci, ac_h, ax_h, bv_h, bg_h, ci_h, wa, wb, cb_r, lg_r, lb_r,
             pa_o, s_o, p_o, v_o, ua_e, ub_e, ci_e, ub_s):
        i = pl.program_id(0)
        keep = (i > 0).astype(F32)
        ua_e[0:HALO, :] = _f32(ac_h) * _f32(ax_h) * keep
        ua_e[HALO:HALO + tm, :] = _f32(ac) * _f32(ax)
        ub_e[0:HALO, :] = _f32(bv_h) * _sigmoid(_f32(bg_h)) * keep
        ub_e[HALO:HALO + tm, :] = _f32(bv) * _sigmoid(_f32(bg))
        ci_e[0:HALO, :] = _f32(ci_h) * keep
        ci_e[HALO:HALO + tm, :] = _f32(ci)
        _shift_copies(ub_e, ub_s)
        for r0 in range(0, tm, SUB):
            rows = slice(r0, r0 + SUB)
            z = None
            for k in range(K_A):
                t = wa[k:k + 1, :] * ua_e[pl.ds(HALO - (K_A - 1) + k + r0, SUB), :]
                z = t if z is None else z + t
            pa_o[rows, :] = (ab[rows, :].astype(F32) * z).astype(BF16)
            v = None
            for k in range(K_B):
                t = wb[k:k + 1, :] * _window(ub_e, ub_s, HALO - (K_B - 1) + k + r0, SUB)
                v = t if v is None else v + t
            v = v + cb_r[...]
            v_o[rows, :] = v
            mu = jnp.mean(v, axis=-1, keepdims=True)
            vc = v - mu
            rstd = lax.rsqrt(jnp.mean(vc * vc, axis=-1, keepdims=True) + EPS)
            ln = vc * rstd * lg_r[...] + lb_r[...]
            s_o[rows, :] = (ln * _sigmoid(ln)).astype(BF16)
            t_idx = i * tm + r0 + lax.broadcasted_iota(jnp.int32, (SUB, gc), 0)
            for g, w in enumerate(POOL_WINDOWS):
                cols = slice(g * gc, (g + 1) * gc)
                cur = ci_e[pl.ds(HALO + r0, SUB), cols]
                acc = cur
                for j in range(1, w):
                    acc = acc + ci_e[pl.ds(HALO + r0 - j, SUB), cols]
                cnt = jnp.minimum(t_idx + 1, w).astype(F32)
                p_o[rows, cols] = (acc / cnt - cur).astype(BF16)

    return pl.pallas_call(
        body, name=name, grid=(s // tm,),
        in_specs=[tile(0), tile(1), tile(2), tile(3), tile(4), tile(5),
                  _prev_halo(tm, d, 1), _prev_halo(tm, d, 2), _prev_halo(tm, d, 3), _prev_halo(tm, d, 4),
                  _prev_halo(tm, d, 5),
                  vec(K_A), vec(K_B), vec(1), vec(1), vec(1)],
        out_specs=[pl.BlockSpec((tm, d), lambda i: (i, 0))] * 4,
        out_shape=[jax.ShapeDtypeStruct((s, d), BF16)] * 3 + [jax.ShapeDtypeStruct((s, d), F32)],
        scratch_shapes=[pltpu.VMEM((HALO + tm, d), F32)] * 3 + [pltpu.VMEM((7, tm + HALO - 8, d), F32)],
        compiler_params=_params("parallel"),
    )(proj, proj, proj, proj, proj, proj, proj, proj, proj, proj, proj,
      conv_a, conv_b, cb.reshape(1, d), lg.reshape(1, d), lb.reshape(1, d))


def _branch_out(pa, sb, p, proj, x0, woa, wob, wp, wo, bob, ps, gm, name):
    s, d = pa.shape
    gc = d // N_GROUPS
    tm = _tile(s, 512)

    def tile(col=0):
        return pl.BlockSpec((tm, d), lambda i: (i, col))

    def const(shape):
        return pl.BlockSpec(shape, lambda i: (0,) * len(shape))

    def body(pa_r, s_r, p_r, g0, g1, g2, x0_r, woa_r, wob_r, wp_r, wo_r, bob_r, ps_r, gm_r,
             ya_o, yb_o, yc_o, mg_o, x1_o, h2_o, yc_s):
        ya = _dot(pa_r[...], woa_r[...])
        yb = _dot(s_r[...], wob_r[...]) + bob_r[...]
        for g in range(N_GROUPS):
            cols = slice(g * gc, (g + 1) * gc)
            yc_s[:, cols] = _dot(p_r[:, cols], wp_r[g])
        yc = yc_s[...]
        ya_o[...] = ya.astype(BF16)
        yb_o[...] = yb.astype(BF16)
        yc_o[...] = yc.astype(BF16)
        m = _sigmoid(_f32(g0)) * ya + _sigmoid(_f32(g1)) * yb + _sigmoid(_f32(g2)) * (yc * ps_r[...])
        mb = m.astype(BF16)
        mg_o[...] = mb
        x1 = x0_r[...] + _dot(mb, wo_r[...])
        x1_o[...] = x1
        r = lax.rsqrt(jnp.mean(x1 * x1, axis=-1, keepdims=True) + EPS)
        h2_o[...] = (x1 * r * gm_r[...]).astype(BF16)

    return pl.pallas_call(
        body, name=name, grid=(s // tm,),
        in_specs=[tile(), tile(), tile(), tile(6), tile(7), tile(8), tile(),
                  const((d, d)), const((d, d)), const((N_GROUPS, gc, gc)), const((d, d)),
                  const((1, d)), const((1, d)), const((1, d))],
        out_specs=[tile()] * 6,
        out_shape=[jax.ShapeDtypeStruct((s, d), BF16)] * 4 + [jax.ShapeDtypeStruct((s, d), F32),
                                                               jax.ShapeDtypeStruct((s, d), BF16)],
        scratch_shapes=[pltpu.VMEM((tm, d), F32)],
        compiler_params=_params("parallel"),
    )(pa, sb, p, proj, proj, proj, x0, woa, wob, wp, wo, bob.reshape(1, d), ps.reshape(1, d), gm.reshape(1, d))


def _mlp_fwd(h2, x1, w1_3, w2, name, rider=None):
    s, d = h2.shape
    nf, _, tf = w1_3.shape
    tm = _tile(s, 1024)

    def body(h_r, x1_r, w1_r, w2_r, f_o, a_o, x2_o, acc):
        j = pl.program_id(1)

        @pl.when(j == 0)
        def _():
            acc[...] = jnp.zeros_like(acc)

        f = _dot(h_r[...], w1_r[...])
        f_o[...] = f.astype(BF16)
        rl = jnp.maximum(f, 0.0)
        a = (rl * rl).astype(BF16)
        a_o[...] = a
        acc[...] += _dot(a, w2_r[...])

        @pl.when(j == nf - 1)
        def _():
            x2_o[...] = x1_r[...] + acc[...]

    return _hosted_call(
        body, name=name, grid=(s // tm, nf),
        in_specs=[pl.BlockSpec((tm, d), lambda i, j: (i, 0)), pl.BlockSpec((tm, d), lambda i, j: (i, 0)),
                  pl.BlockSpec((None, d, tf), lambda i, j: (j, 0, 0)), pl.BlockSpec((tf, d), lambda i, j: (j, 0))],
        out_specs=[pl.BlockSpec((tm, tf), lambda i, j: (i, j)), pl.BlockSpec((tm, tf), lambda i, j: (i, j)),
                   pl.BlockSpec((tm, d), lambda i, j: (i, 0))],
        out_shape=[jax.ShapeDtypeStruct((s, nf * tf), BF16), jax.ShapeDtypeStruct((s, nf * tf), BF16),
                   jax.ShapeDtypeStruct((s, d), F32)],
        scratch_shapes=[pltpu.VMEM((tm, d), F32)], args=(h2, x1, w1_3, w2), rider=rider)


def _loss_head(x, target, gf, name):
    s, d = x.shape
    tm = _tile(s, 512)
    n = s // tm

    def body(x_r, t_r, g_r, dx_o, loss_o, dg_o, lacc, gacc):
        i = pl.program_id(0)

        @pl.when(i == 0)
        def _():
            lacc[...] = jnp.zeros_like(lacc)
            gacc[...] = jnp.zeros_like(gacc)

        xv = x_r[...]
        r = lax.rsqrt(jnp.mean(xv * xv, axis=-1, keepdims=True) + EPS)
        xn = xv * r
        e = xn * g_r[...] - t_r[...]
        lacc[...] += _fold8(e * e)
        dy = e * (1.0 / d)
        gacc[...] += _fold8(dy * xn)
        dxn = dy * g_r[...]
        dx_o[...] = r * (dxn - xn * jnp.mean(dxn * xn, axis=-1, keepdims=True))

        @pl.when(i == n - 1)
        def _():
            loss_o[...] = jnp.sum(lacc[...]).reshape(1, 1) * (0.5 / d)
            dg_o[...] = jnp.sum(gacc[...], axis=0, keepdims=True)

    return pl.pallas_call(
        body, name=name, grid=(n,),
        in_specs=[pl.BlockSpec((tm, d), lambda i: (i, 0)), pl.BlockSpec((tm, d), lambda i: (i, 0)),
                  pl.BlockSpec((1, d), lambda i: (0, 0))],
        out_specs=[pl.BlockSpec((tm, d), lambda i: (i, 0)), pl.BlockSpec((1, 1), lambda i: (0, 0)),
                   pl.BlockSpec((1, d), lambda i: (0, 0))],
        out_shape=[jax.ShapeDtypeStruct((s, d), F32), jax.ShapeDtypeStruct((1, 1), F32),
                   jax.ShapeDtypeStruct((1, d), F32)],
        scratch_shapes=[pltpu.VMEM((8, d), F32), pltpu.VMEM((8, d), F32)],
        compiler_params=_params("arbitrary"),
    )(x, target, gf.reshape(1, d))


def _tn(a, b, nb, name, rider=None):
    s, m = a.shape
    n = b.shape[1]
    nw = n // nb
    tmm = _tile(m, 1024, 128)
    tn = nw if nw <= 1152 else nw // 2
    per = nw // tn
    ts = _tile(s, 1024)
    ns = s // ts

    def body(a_r, b_r, o_r, acc):
        k = pl.program_id(2)

        @pl.when(k == 0)
        def _():
            acc[...] = jnp.zeros_like(acc)

        acc[...] += lax.dot_general(a_r[...].astype(BF16), b_r[...].astype(BF16), (((0,), (0,)), ((), ())),
                                    preferred_element_type=F32)

        @pl.when(k == ns - 1)
        def _():
            o_r[...] = acc[...]

    (out,), carried = _hosted_call(
        body, name=name, grid=(m // tmm, nb * per, ns),
        in_specs=[pl.BlockSpec((ts, tmm), lambda i, j, k: (k, i)), pl.BlockSpec((ts, tn), lambda i, j, k: (k, j))],
        out_specs=[pl.BlockSpec((None, tmm, tn), lambda i, j, k: (j // per, i, j % per))],
        out_shape=[jax.ShapeDtypeStruct((nb, m, nw), F32)],
        scratch_shapes=[pltpu.VMEM((tmm, tn), F32)], args=(a, b), rider=rider)
    return (out, carried) if rider is not None else out


def _tn_groups(a, b, name):
    s, d = a.shape
    gc = d // N_GROUPS
    ts = _tile(s, 512)
    ns = s // ts

    def body(a_r, b_r, o_r, acc):
        k = pl.program_id(1)

        @pl.when(k == 0)
        def _():
            acc[...] = jnp.zeros_like(acc)

        acc[...] += lax.dot_general(a_r[...], b_r[...], (((0,), (0,)), ((), ())), preferred_element_type=F32)

        @pl.when(k == ns - 1)
        def _():
            o_r[...] = acc[...]

    return pl.pallas_call(
        body, name=name, grid=(N_GROUPS, ns),
        in_specs=[pl.BlockSpec((ts, gc), lambda g, k: (k, g)), pl.BlockSpec((ts, gc), lambda g, k: (k, g))],
        out_specs=pl.BlockSpec((None, gc, gc), lambda g, k: (g, 0, 0)),
        out_shape=jax.ShapeDtypeStruct((N_GROUPS, gc, gc), F32),
        scratch_shapes=[pltpu.VMEM((gc, gc), F32)],
        compiler_params=_params("parallel", "arbitrary"),
    )(a, b)


def _mlp_bwd(dx2, f, x1, w2t, w1t, gm, name, rider=None):
    s, d = dx2.shape
    ff = f.shape[1]
    tf = _tile(ff, 1024, 128)
    nf = ff // tf
    tm = _tile(s, 1024)
    n = s // tm

    def body(dx2_r, f_r, x1_r, w2t_r, w1t_r, gm_r, df_o, dx1_o, dg_o, dxb, acc, gacc):
        i = pl.program_id(0)
        j = pl.program_id(1)

        @pl.when((i == 0) & (j == 0))
        def _():
            gacc[...] = jnp.zeros_like(gacc)

        @pl.when(j == 0)
        def _():
            dxb[...] = dx2_r[...].astype(BF16)
            acc[...] = jnp.zeros_like(acc)

        da = _dot(dxb[...], w2t_r[...])
        df = (da * (2.0 * jnp.maximum(f_r[...].astype(F32), 0.0))).astype(BF16)
        df_o[...] = df
        acc[...] += _dot(df, w1t_r[...])

        @pl.when(j == nf - 1)
        def _():
            dx, dg = _rms_bwd(acc[...], x1_r[...], gm_r[...])
            gacc[...] += dg
            dx1_o[...] = dx2_r[...] + dx

        @pl.when((i == n - 1) & (j == nf - 1))
        def _():
            dg_o[...] = jnp.sum(gacc[...], axis=0, keepdims=True)

    return _hosted_call(
        body, name=name, grid=(n, nf),
        in_specs=[pl.BlockSpec((tm, d), lambda i, j: (i, 0)), pl.BlockSpec((tm, tf), lambda i, j: (i, j)),
                  pl.BlockSpec((tm, d), lambda i, j: (i, 0)), pl.BlockSpec((d, tf), lambda i, j: (0, j)),
                  pl.BlockSpec((tf, d), lambda i, j: (j, 0)), pl.BlockSpec((1, d), lambda i, j: (0, 0))],
        out_specs=[pl.BlockSpec((tm, tf), lambda i, j: (i, j)), pl.BlockSpec((tm, d), lambda i, j: (i, 0)),
                   pl.BlockSpec((1, d), lambda i, j: (0, 0))],
        out_shape=[jax.ShapeDtypeStruct((s, ff), BF16), jax.ShapeDtypeStruct((s, d), F32),
                   jax.ShapeDtypeStruct((1, d), F32)],
        scratch_shapes=[pltpu.VMEM((tm, d), BF16), pltpu.VMEM((tm, d), F32), pltpu.VMEM((8, d), F32)],
        args=(dx2, f, x1, w2t, w1t, gm.reshape(1, d)), rider=rider)


def _branch_out_bwd(dx1, proj, ya, yb, yc, wot, woat, wobt, wpt, ps, name):
    s, d = dx1.shape
    gc = d // N_GROUPS
    tm = _tile(s, 512)
    n = s // tm

    def tile(col=0):
        return pl.BlockSpec((tm, d), lambda i: (i, col))

    def const(shape):
        return pl.BlockSpec(shape, lambda i: (0,) * len(shape))

    def body(dx1_r, g0, g1, g2, ya_r, yb_r, yc_r, wot_r, woat_r, wobt_r, wpt_r, ps_r,
             dpa_o, ds_o, dp_o, dpj_o, dya_o, dyb_o, dyc_o, dbob_o, dps_o, dbg_o, sacc, gacc):
        i = pl.program_id(0)

        @pl.when(i == 0)
        def _():
            sacc[...] = jnp.zeros_like(sacc)
            gacc[...] = jnp.zeros_like(gacc)

        dm = _dot(dx1_r[...].astype(BF16), wot_r[...])
        ycp = _f32(yc_r)
        ys = (_f32(ya_r), _f32(yb_r), ycp * ps_r[...])
        dys = []
        for b, g_r in enumerate((g0, g1, g2)):
            sg = _sigmoid(_f32(g_r))
            dgt = dm * ys[b] * sg * (1.0 - sg)
            dpj_o[:, b * d:(b + 1) * d] = dgt.astype(BF16)
            gacc[b] += _fold8(dgt)
            dys.append(dm * sg)
        dya, dyb, dyc = dys
        sacc[0] += _fold8(dyb)
        sacc[1] += _fold8(dyc * ycp)
        dyab = dya.astype(BF16)
        dybb = dyb.astype(BF16)
        dycb = (dyc * ps_r[...]).astype(BF16)
        dya_o[...] = dyab
        dyb_o[...] = dybb
        dyc_o[...] = dycb
        dpa_o[...] = _dot(dyab, woat_r[...]).astype(BF16)
        ds_o[...] = _dot(dybb, wobt_r[...]).astype(BF16)
        for g in range(N_GROUPS):
            cols = slice(g * gc, (g + 1) * gc)
            dp_o[:, cols] = _dot(dycb[:, cols], wpt_r[g]).astype(BF16)

        @pl.when(i == n - 1)
        def _():
            dbob_o[...] = jnp.sum(sacc[0], axis=0, keepdims=True)
            dps_o[...] = jnp.sum(sacc[1], axis=0, keepdims=True)
            for b in range(3):
                dbg_o[:, b * d:(b + 1) * d] = jnp.sum(gacc[b], axis=0, keepdims=True)

    return pl.pallas_call(
        body, name=name, grid=(n,),
        in_specs=[tile(), tile(6), tile(7), tile(8), tile(), tile(), tile(),
                  const((d, d)), const((d, d)), const((d, d)), const((N_GROUPS, gc, gc)), const((1, d))],
        out_specs=[tile(), tile(), tile(), pl.BlockSpec((tm, 3 * d), lambda i: (i, 2)), tile(), tile(), tile(),
                   const((1, d)), const((1, d)), const((1, 3 * d))],
        out_shape=[jax.ShapeDtypeStruct((s, d), BF16)] * 3 + [jax.ShapeDtypeStruct((s, 9 * d), BF16)]
        + [jax.ShapeDtypeStruct((s, d), BF16)] * 3
        + [jax.ShapeDtypeStruct((1, d), F32), jax.ShapeDtypeStruct((1, d), F32), jax.ShapeDtypeStruct((1, 3 * d), F32)],
        scratch_shapes=[pltpu.VMEM((2, 8, d), F32), pltpu.VMEM((3, 8, d), F32)],
        compiler_params=_params("arbitrary"),
    )(dx1, proj, proj, proj, ya, yb, yc, wot, woat, wobt, wpt, ps.reshape(1, d))


def _ln_silu_bwd(v, ds, lg, lb, name):
    s, d = v.shape
    tm = _tile(s, 256)
    n = s // tm

    def tile():
        return pl.BlockSpec((tm, d), lambda i: (i, 0))

    def vec():
        return pl.BlockSpec((1, d), lambda i: (0, 0))

    def body(v_r, ds_r, lg_r, lb_r, dv_o, dlg_o, dlb_o, dcb_o, acc):
        i = pl.program_id(0)

        @pl.when(i == 0)
        def _():
            acc[...] = jnp.zeros_like(acc)

        for r0 in range(0, tm, SUB):
            rows = slice(r0, r0 + SUB)
            vv = v_r[rows, :]
            mu = jnp.mean(vv, axis=-1, keepdims=True)
            vc = vv - mu
            rstd = lax.rsqrt(jnp.mean(vc * vc, axis=-1, keepdims=True) + EPS)
            nrm = vc * rstd
            ln = nrm * lg_r[...] + lb_r[...]
            sg = _sigmoid(ln)
            dln = ds_r[rows, :].astype(F32) * (sg * (1.0 + ln * (1.0 - sg)))
            acc[0] += _fold8(dln * nrm)
            acc[1] += _fold8(dln)
            dn = dln * lg_r[...]
            dv = rstd * (dn - jnp.mean(dn, axis=-1, keepdims=True)
                         - nrm * jnp.mean(dn * nrm, axis=-1, keepdims=True))
            acc[2] += _fold8(dv)
            dv_o[rows, :] = dv

        @pl.when(i == n - 1)
        def _():
            dlg_o[...] = jnp.sum(acc[0], axis=0, keepdims=True)
            dlb_o[...] = jnp.sum(acc[1], axis=0, keepdims=True)
            dcb_o[...] = jnp.sum(acc[2], axis=0, keepdims=True)

    return pl.pallas_call(
        body, name=name, grid=(n,),
        in_specs=[tile(), tile(), vec(), vec()],
        out_specs=[tile(), vec(), vec(), vec()],
        out_shape=[jax.ShapeDtypeStruct((s, d), F32)] + [jax.ShapeDtypeStruct((1, d), F32)] * 3,
        scratch_shapes=[pltpu.VMEM((3, 8, d), F32)],
        compiler_params=_params("arbitrary"),
    )(v, ds, lg.reshape(1, d), lb.reshape(1, d))


def _mix_bwd(proj, dpa, dv, dp, dproj, conv_a, conv_b, name):
    s, d9 = proj.shape
    d = d9 // 9
    gc = d // N_GROUPS
    tm = _tile(s, 256, HALO)
    n = s // tm
    nh = s // HALO

    def tile(col=0):
        return pl.BlockSpec((tm, d), lambda i: (i, col))

    def vec(rows):
        return pl.BlockSpec((rows, d), lambda i: (0, 0))

    def body(ab, ac, ax, bv, bg, ac_h, ax_h, bv_h, bg_h, dpa_r, dv_r, dp_r, ab_n, dpa_n, dv_n, dp_n, wa, wb, _alias,
             dpj_o, dbin_o, dca_o, dcb_o, ua_e, ub_e, dz_e, dv_e, q_e, bacc, cacc_a, cacc_b, ub_s, dv_s):
        i = pl.program_id(0)

        @pl.when(i == 0)
        def _():
            bacc[...] = jnp.zeros_like(bacc)
            cacc_a[...] = jnp.zeros_like(cacc_a)
            cacc_b[...] = jnp.zeros_like(cacc_b)

        keep_p = (i > 0).astype(F32)
        keep_n = (i < n - 1).astype(F32)
        ua_e[0:HALO, :] = _f32(ac_h) * _f32(ax_h) * keep_p
        ua_e[HALO:HALO + tm, :] = _f32(ac) * _f32(ax)
        ub_e[0:HALO, :] = _f32(bv_h) * _sigmoid(_f32(bg_h)) * keep_p
        ub_e[HALO:HALO + tm, :] = _f32(bv) * _sigmoid(_f32(bg))
        dz_e[0:tm, :] = _f32(dpa_r) * _f32(ab)
        dz_e[tm:tm + HALO, :] = _f32(dpa_n) * _f32(ab_n) * keep_n
        dv_e[0:tm, :] = dv_r[...]
        dv_e[tm:tm + HALO, :] = dv_n[...] * keep_n
        _shift_copies(ub_e, ub_s)
        _shift_copies(dv_e, dv_s)
        for g, w in enumerate(POOL_WINDOWS):
            cols = slice(g * gc, (g + 1) * gc)
            t_idx = i * tm + lax.broadcasted_iota(jnp.int32, (tm + HALO, gc), 0)
            cnt = jnp.minimum(t_idx + 1, w).astype(F32)
            q_e[0:tm, cols] = dp_r[:, cols].astype(F32) / cnt[0:tm]
            q_e[tm:tm + HALO, cols] = dp_n[:, cols].astype(F32) * keep_n / cnt[tm:tm + HALO]
        for r0 in range(0, tm, SUB):
            rows = slice(r0, r0 + SUB)
            dz = dz_e[pl.ds(r0, SUB), :]
            z = None
            du = None
            for k in range(K_A):
                u_k = ua_e[pl.ds(HALO - (K_A - 1) + k + r0, SUB), :]
                t = wa[k:k + 1, :] * u_k
                z = t if z is None else z + t
                cacc_a[k] += _fold8(dz * u_k)
                t = wa[k:k + 1, :] * dz_e[pl.ds(r0 + (K_A - 1) - k, SUB), :]
                du = t if du is None else du + t
            outs = [dpa_r[rows, :].astype(F32) * z, du * ax[rows, :].astype(F32), du * ac[rows, :].astype(F32)]
            dvv = dv_e[pl.ds(r0, SUB), :]
            du = None
            for k in range(K_B):
                cacc_b[k] += _fold8(dvv * _window(ub_e, ub_s, HALO - (K_B - 1) + k + r0, SUB))
                t = wb[k:k + 1, :] * _window(dv_e, dv_s, r0 + (K_B - 1) - k, SUB)
                du = t if du is None else du + t
            sg = _sigmoid(bg[rows, :].astype(F32))
            outs.append(du * sg)
            outs.append(du * bv[rows, :].astype(F32) * sg * (1.0 - sg))
            for b, o in enumerate(outs):
                dpj_o[rows, b * d:(b + 1) * d] = o.astype(BF16)
                bacc[b] += _fold8(o)
            for g, w in enumerate(POOL_WINDOWS):
                cols = slice(g * gc, (g + 1) * gc)
                acc = q_e[pl.ds(r0, SUB), cols]
                for j in range(1, w):
                    acc = acc + q_e[pl.ds(r0 + j, SUB), cols]
                o = acc - dp_r[rows, cols].astype(F32)
                dpj_o[rows, 5 * d + g * gc:5 * d + (g + 1) * gc] = o.astype(BF16)
                bacc[5, :, cols] += _fold8(o)

        @pl.when(i == n - 1)
        def _():
            for b in range(6):
                dbin_o[:, b * d:(b + 1) * d] = jnp.sum(bacc[b], axis=0, keepdims=True)
            dca_o[...] = jnp.sum(cacc_a[...], axis=1)
            dcb_o[...] = jnp.sum(cacc_b[...], axis=1)

    ka8 = 8
    kb8 = 32
    return pl.pallas_call(
        body, name=name, grid=(n,),
        in_specs=[tile(0), tile(1), tile(2), tile(3), tile(4),
                  _prev_halo(tm, d, 1), _prev_halo(tm, d, 2), _prev_halo(tm, d, 3), _prev_halo(tm, d, 4),
                  tile(), tile(), tile(),
                  _next_halo(tm, d, 0, nh), _next_halo(tm, d, 0, nh), _next_halo(tm, d, 0, nh), _next_halo(tm, d, 0, nh),
                  vec(K_A), vec(K_B), ANY],
        out_specs=[pl.BlockSpec((tm, 6 * d), lambda i: (i, 0)), pl.BlockSpec((1, 6 * d), lambda i: (0, 0)),
                   pl.BlockSpec((ka8, d), lambda i: (0, 0)), pl.BlockSpec((kb8, d), lambda i: (0, 0))],
        out_shape=[jax.ShapeDtypeStruct((s, 9 * d), BF16), jax.ShapeDtypeStruct((1, 6 * d), F32),
                   jax.ShapeDtypeStruct((ka8, d), F32), jax.ShapeDtypeStruct((kb8, d), F32)],
        scratch_shapes=[pltpu.VMEM((HALO + tm, d), F32), pltpu.VMEM((HALO + tm, d), F32),
                        pltpu.VMEM((tm + HALO, d), F32), pltpu.VMEM((tm + HALO, d), F32),
                        pltpu.VMEM((tm + HALO, d), F32),
                        pltpu.VMEM((6, 8, d), F32), pltpu.VMEM((ka8, 8, d), F32), pltpu.VMEM((kb8, 8, d), F32),
                        pltpu.VMEM((7, tm + HALO - 8, d), F32), pltpu.VMEM((7, tm + HALO - 8, d), F32)],
        input_output_aliases={18: 0},
        compiler_params=_params("arbitrary"),
    )(proj, proj, proj, proj, proj, proj, proj, proj, proj, dpa, dv, dp, proj, dpa, dv, dp, conv_a, conv_b, dproj)


def _dh_rms_bwd(dproj, wint, x0, dx1, g, name, rider=None):
    s, p = dproj.shape
    d = x0.shape[1]
    tm = _tile(s, 1024)
    tk = _tile(p, 1536, 128)
    n = s // tm
    nk = p // tk

    def body(dp_r, w_r, x_r, dx1_r, g_r, dx_o, dg_o, acc, gacc):
        i = pl.program_id(0)
        k = pl.program_id(1)

        @pl.when((i == 0) & (k == 0))
        def _():
            gacc[...] = jnp.zeros_like(gacc)

        @pl.when(k == 0)
        def _():
            acc[...] = jnp.zeros_like(acc)

        acc[...] += _dot(dp_r[...], w_r[...])

        @pl.when(k == nk - 1)
        def _():
            dx, dg = _rms_bwd(acc[...], x_r[...], g_r[...])
            gacc[...] += dg
            dx_o[...] = dx1_r[...] + dx

        @pl.when((i == n - 1) & (k == nk - 1))
        def _():
            dg_o[...] = jnp.sum(gacc[...], axis=0, keepdims=True)

    return _hosted_call(
        body, name=name, grid=(n, nk),
        in_specs=[pl.BlockSpec((tm, tk), lambda i, k: (i, k)), pl.BlockSpec((tk, d), lambda i, k: (k, 0)),
                  pl.BlockSpec((tm, d), lambda i, k: (i, 0)), pl.BlockSpec((tm, d), lambda i, k: (i, 0)),
                  pl.BlockSpec((1, d), lambda i, k: (0, 0))],
        out_specs=[pl.BlockSpec((tm, d), lambda i, k: (i, 0)), pl.BlockSpec((1, d), lambda i, k: (0, 0))],
        out_shape=[jax.ShapeDtypeStruct((s, d), F32), jax.ShapeDtypeStruct((1, d), F32)],
        scratch_shapes=[pltpu.VMEM((tm, d), F32), pltpu.VMEM((8, d), F32)],
        args=(dproj, wint, x0, dx1, g.reshape(1, d)), rider=rider)


EARLY = ("w_out_a", "w_out_b", "w_pool", "w_o", "w_mlp1", "w_mlp2")


def _layer_fwd(x0, w, l, proj_rider=None, mlp_rider=None):
    h = _rms_fwd(x0, w["g_mix"], f"rms_fwd_{l}")
    proj, carried_proj = _proj(h, w["win3"], w["b_in"], f"proj_{l}", proj_rider)
    pa, sb, p, v = _mix_fwd(proj, w["conv_a"], w["conv_b"], w["conv_b_bias"], w["ln_b_g"], w["ln_b_b"], f"mix_fwd_{l}")
    ya, yb, yc, mg, x1, h2 = _branch_out(pa, sb, p, proj, x0, w["woa"], w["wob"], w["wp"], w["wo"], w["b_out_b"],
                                         w["pool_scale"], w["g_mlp"], f"branch_out_{l}")
    rider = mlp_rider(carried_proj) if callable(mlp_rider) else mlp_rider
    (f, a, x2), carried_mlp = _mlp_fwd(h2, x1, w["w1_3"], w["w2"], f"mlp_fwd_{l}", rider)
    saved = dict(x0=x0, h=h, proj=proj, pa=pa, sb=sb, p=p, v=v, ya=ya, yb=yb, yc=yc, mg=mg, x1=x1, h2=h2, f=f, a=a)
    return x2, saved, carried_proj, carried_mlp


def _layer_bwd_early(dx2, w, sv, l, mlp_rider=None):
    d = dx2.shape[1]
    (df, dx1, dgm), carried = _mlp_bwd(dx2, sv["f"], sv["x1"], w["w2t"], w["w1t"], w["g_mlp"], f"mlp_bwd_{l}", mlp_rider)
    dw2 = _tn(sv["a"], dx2, 1, f"dw2_{l}")
    dw1 = _tn(sv["h2"], df, N_CHIPS, f"dw1_{l}")
    dpa, ds, dp, dproj, dya, dyb, dyc, dbob, dps, dbg = _branch_out_bwd(
        dx1, sv["proj"], sv["ya"], sv["yb"], sv["yc"], w["wot"], w["woat"], w["wobt"], w["wpt"], w["pool_scale"],
        f"branch_out_bwd_{l}")
    dwo = _tn(sv["mg"], dx1, 1, f"dwo_{l}")
    dwoa = _tn(sv["pa"], dya, 1, f"dwoa_{l}")
    dwob = _tn(sv["sb"], dyb, 1, f"dwob_{l}")
    dwp = _tn_groups(sv["p"], dyc, f"dwp_{l}")
    dv, dlg, dlb, dcb = _ln_silu_bwd(sv["v"], ds, w["ln_b_g"], w["ln_b_b"], f"ln_silu_bwd_{l}")
    dproj, dbm, dca, dcvb = _mix_bwd(sv["proj"], dpa, dv, dp, dproj, w["conv_a"], w["conv_b"], f"mix_bwd_{l}")
    ff = dw2.shape[1]
    gc = d // N_GROUPS
    big = dict(
        w_out_a=dwoa.reshape(N_CHIPS, d // N_CHIPS, d),
        w_out_b=dwob.reshape(N_CHIPS, d // N_CHIPS, d),
        w_pool=dwp.reshape(N_GROUPS, N_CHIPS, gc // N_CHIPS, gc).transpose(1, 0, 2, 3).reshape(N_CHIPS, gc, gc),
        w_o=dwo.reshape(N_CHIPS, d // N_CHIPS, d),
        w_mlp1=dw1,
        w_mlp2=dw2.reshape(N_CHIPS, ff // N_CHIPS, d),
    )
    small = dict(
        b_in=jnp.concatenate([dbm, dbg], axis=1).reshape(9, d), conv_a=dca[:K_A], conv_b=dcvb[:K_B],
        conv_b_bias=dcb, ln_b_g=dlg, ln_b_b=dlb, b_out_b=dbob, pool_scale=dps, g_mlp=dgm,
    )
    return dx1, dproj, big, small, carried


def _layer_bwd_late(dx1, dproj, w, sv, l, dwin_rider=None, dh_rider=None):
    if dwin_rider is None:
        dwin, carried_dwin = _tn(sv["h"], dproj, N_CHIPS, f"dwin_{l}"), []
    else:
        dwin, carried_dwin = _tn(sv["h"], dproj, N_CHIPS, f"dwin_{l}", dwin_rider)
    rider = dh_rider(dwin) if callable(dh_rider) else dh_rider
    (dx0, dgmix), carried_dh = _dh_rms_bwd(dproj, w["wint"], sv["x0"], dx1, w["g_mix"], f"dh_{l}", rider)
    return dx0, dwin, dgmix, carried_dwin, carried_dh


def _place():
    x, y, c = lax.axis_index("x"), lax.axis_index("y"), lax.axis_index("c")
    chips = [(1 - x, y), (x, 1 - y), (1 - x, 1 - y)]
    return x, y, c, chips


def _place_shard(w, chip_arr, dtype, name):
    rows, cols = w.shape
    tr = _tile(rows, max(16, (1 << 19) // cols), 16)

    def body(k_ref, w_r, o_r):
        o_r[...] = w_r[...].astype(dtype)

    return pl.pallas_call(
        body, name=name,
        grid_spec=pltpu.PrefetchScalarGridSpec(
            num_scalar_prefetch=1, grid=(rows // tr,),
            in_specs=[pl.BlockSpec((tr, cols), lambda i, k_ref: (i, 0))],
            out_specs=pl.BlockSpec((None, tr, cols), lambda i, k_ref: (k_ref[0], i, 0))),
        out_shape=jax.ShapeDtypeStruct((N_CHIPS, rows, cols), dtype),
        compiler_params=_params("parallel"),
    )(chip_arr, w)


def _gather_shards(placed, name):
    n = len(placed)

    def body(*refs):
        ins, outs = refs[:n], refs[n:2 * n]
        send, recv, fsend, frecv = refs[2 * n:]
        x, y, c, chips = _place()
        me = 2 * x + y
        sib = (x, y, 1 - c)

        def half(a, which):
            h = ins[a].shape[1] // 2
            return pl.ds(pl.multiple_of(which * h, 8), h)

        def ici(a, j, chip, src_chip):
            return pltpu.make_async_remote_copy(
                src_ref=ins[a].at[me, half(a, c)], dst_ref=outs[a].at[src_chip, half(a, c)],
                send_sem=send.at[a * 3 + j], recv_sem=recv.at[a * 3 + j], device_id=(*chip, c), device_id_type=MESH)

        def d2d(a, j, src_chip, which):
            rows = outs[a].at[src_chip, half(a, which)]
            return pltpu.make_async_remote_copy(
                src_ref=rows, dst_ref=rows, send_sem=fsend.at[a * 3 + j], recv_sem=frecv.at[a * 3 + j],
                device_id=sib, device_id_type=MESH)

        for a in range(n):
            for j, chip in enumerate(chips):
                ici(a, j, chip, me).start()
        for a in range(n):
            for j, (px, py) in enumerate(chips):
                ici(a, j, (px, py), 2 * px + py).wait_recv()
                d2d(a, j, 2 * px + py, c).start()
        for a in range(n):
            for j, (px, py) in enumerate(chips):
                d2d(a, j, 2 * px + py, 1 - c).wait_recv()
        for a in range(n):
            for j, (px, py) in enumerate(chips):
                ici(a, j, (px, py), me).wait_send()
                d2d(a, j, 2 * px + py, c).wait_send()

    return pl.pallas_call(
        body, name=name,
        in_specs=[ANY] * n, out_specs=[ANY] * n,
        out_shape=[jax.ShapeDtypeStruct(a.shape, a.dtype) for a in placed],
        scratch_shapes=[pltpu.SemaphoreType.DMA((3 * n,))] * 4,
        input_output_aliases={a: a for a in range(n)},
    )(*placed)


def _gather_rows(v, name):
    def body(v_ref, o_ref, send, recv, lsem):
        x, y, c = lax.axis_index("x"), lax.axis_index("y"), lax.axis_index("c")
        me = 4 * x + 2 * y + c
        local = pltpu.make_async_copy(v_ref, o_ref.at[me], lsem)
        local.start()

        def copy(m, src_dev):
            peer = (x ^ ((m >> 2) & 1), y ^ ((m >> 1) & 1), c ^ (m & 1))
            return pltpu.make_async_remote_copy(
                src_ref=v_ref, dst_ref=o_ref.at[src_dev], send_sem=send.at[m - 1], recv_sem=recv.at[m - 1],
                device_id=peer, device_id_type=MESH)

        for m in range(1, N_DEV):
            copy(m, me).start()
        for m in range(1, N_DEV):
            copy(m, me ^ m).wait_recv()
        for m in range(1, N_DEV):
            copy(m, me).wait_send()
        local.wait()

    return pl.pallas_call(
        body, name=name, in_specs=[ANY], out_specs=ANY,
        out_shape=jax.ShapeDtypeStruct((N_DEV, *v.shape), v.dtype),
        scratch_shapes=[pltpu.SemaphoreType.DMA((N_DEV - 1,)), pltpu.SemaphoreType.DMA((N_DEV - 1,)),
                        pltpu.SemaphoreType.DMA],
    )(v)


def _ride_swap(grads):
    n = len(grads)

    def copy(ins, outs, sems, a, k):
        x, y, c, _ = _place()
        h = ins[a].shape[1] // 2
        return pltpu.make_async_remote_copy(
            src_ref=ins[a].at[k, pl.ds(pl.multiple_of((1 - c) * h, 8), h)], dst_ref=outs[a].at[k],
            send_sem=sems[0].at[a * N_CHIPS + k], recv_sem=sems[1].at[a * N_CHIPS + k], device_id=(x, y, 1 - c),
            device_id_type=MESH)

    def start(ins, outs, sems):
        for a in range(n):
            for k in range(N_CHIPS):
                copy(ins, outs, sems, a, k).start()

    def finish(ins, outs, sems):
        for a in range(n):
            for k in range(N_CHIPS):
                copy(ins, outs, sems, a, k).wait()

    return _Rider(grads, [jax.ShapeDtypeStruct((N_CHIPS, g.shape[1] // 2, g.shape[2]), g.dtype) for g in grads], [],
                  [pltpu.SemaphoreType.DMA((N_CHIPS * n,))] * 2, start, finish)


def _add_halves(g, r, c_arr, name):
    nk, rows, cols = g.shape
    h = rows // 2
    tr = _tile(h, max(8, (1 << 19) // cols), 8)
    nblk = h // tr

    def body(c_ref, g_r, r_r, o_r):
        o_r[...] = (g_r[...] + r_r[...]).astype(BF16)

    return pl.pallas_call(
        body, name=name,
        grid_spec=pltpu.PrefetchScalarGridSpec(
            num_scalar_prefetch=1, grid=(nk, nblk),
            in_specs=[pl.BlockSpec((None, tr, cols), lambda k, i, c_ref: (k, c_ref[0] * nblk + i, 0)),
                      pl.BlockSpec((None, tr, cols), lambda k, i, c_ref: (k, i, 0))],
            out_specs=pl.BlockSpec((None, tr, cols), lambda k, i, c_ref: (k, i, 0))),
        out_shape=jax.ShapeDtypeStruct((nk, h, cols), BF16),
        compiler_params=_params("parallel", "parallel"),
    )(c_arr, g, r)


def _ride_exchange(parts):
    n = len(parts)

    def copy(ins, outs, sems, a, j):
        x, y, c, chips = _place()
        px, py = chips[j]
        return pltpu.make_async_remote_copy(
            src_ref=ins[a].at[2 * px + py], dst_ref=outs[a].at[j], send_sem=sems[0].at[a * 3 + j],
            recv_sem=sems[1].at[a * 3 + j], device_id=(px, py, c), device_id_type=MESH)

    def start(ins, outs, sems):
        for a in range(n):
            for j in range(3):
                copy(ins, outs, sems, a, j).start()

    def finish(ins, outs, sems):
        for a in range(n):
            for j in range(3):
                copy(ins, outs, sems, a, j).wait()

    return _Rider(parts, [jax.ShapeDtypeStruct((3, *p.shape[1:]), p.dtype) for p in parts], [],
                  [pltpu.SemaphoreType.DMA((3 * n,))] * 2, start, finish)


def _ride_gather_ici(placed):
    n = len(placed)

    def copy(ins, outs, sems, a, j, src_chip):
        x, y, c, chips = _place()
        h = ins[a].shape[1] // 2
        rows = pl.ds(pl.multiple_of(c * h, 8), h)
        return pltpu.make_async_remote_copy(
            src_ref=ins[a].at[2 * x + y, rows], dst_ref=outs[a].at[src_chip, rows], send_sem=sems[0].at[a * 3 + j],
            recv_sem=sems[1].at[a * 3 + j], device_id=(*chips[j], c), device_id_type=MESH)

    def start(ins, outs, sems):
        x, y, _, _ = _place()
        for a in range(n):
            for j in range(3):
                copy(ins, outs, sems, a, j, 2 * x + y).start()

    def finish(ins, outs, sems):
        x, y, _, chips = _place()
        for a in range(n):
            for j, (px, py) in enumerate(chips):
                copy(ins, outs, sems, a, j, 2 * px + py).wait_recv()
        for a in range(n):
            for j in range(3):
                copy(ins, outs, sems, a, j, 2 * x + y).wait_send()

    return _Rider(placed, [jax.ShapeDtypeStruct(p.shape, p.dtype) for p in placed], [(a, a) for a in range(n)],
                  [pltpu.SemaphoreType.DMA((3 * n,))] * 2, start, finish)


def _ride_gather_d2d(placed):
    n = len(placed)

    def copy(ins, outs, sems, a, j, which):
        x, y, c, chips = _place()
        px, py = chips[j]
        h = ins[a].shape[1] // 2
        rows = pl.ds(pl.multiple_of(which * h, 8), h)
        return pltpu.make_async_remote_copy(
            src_ref=ins[a].at[2 * px + py, rows], dst_ref=outs[a].at[2 * px + py, rows], send_sem=sems[0].at[a * 3 + j],
            recv_sem=sems[1].at[a * 3 + j], device_id=(x, y, 1 - c), device_id_type=MESH)

    def start(ins, outs, sems):
        c = lax.axis_index("c")
        for a in range(n):
            for j in range(3):
                copy(ins, outs, sems, a, j, c).start()

    def finish(ins, outs, sems):
        c = lax.axis_index("c")
        for a in range(n):
            for j in range(3):
                copy(ins, outs, sems, a, j, 1 - c).wait_recv()
        for a in range(n):
            for j in range(3):
                copy(ins, outs, sems, a, j, c).wait_send()

    return _Rider(placed, [jax.ShapeDtypeStruct(p.shape, p.dtype) for p in placed], [(a, a) for a in range(n)],
                  [pltpu.SemaphoreType.DMA((3 * n,))] * 2, start, finish)


def _sum_chips(r, name):
    nk, rows, cols = r.shape
    tr = _tile(rows, max(8, (1 << 19) // cols), 8)

    def body(r_r, o_r):
        acc = r_r[0].astype(F32)
        for k in range(1, nk):
            acc = acc + r_r[k].astype(F32)
        o_r[...] = acc

    return pl.pallas_call(
        body, name=name, grid=(rows // tr,),
        in_specs=[pl.BlockSpec((nk, tr, cols), lambda i: (0, i, 0))],
        out_specs=pl.BlockSpec((tr, cols), lambda i: (i, 0)),
        out_shape=jax.ShapeDtypeStruct((rows, cols), F32),
        compiler_params=_params("parallel"),
    )(r)


def _sum_chips_into(own, arrived, dst, layer, n_layers, place_arr, name):
    _, h, cols = own.shape
    tr = _tile(h, max(8, (1 << 19) // cols), 8)
    nblk = h // tr

    def body(p_ref, own_r, arr_r, *rest):
        o_r = rest[-1]
        acc = own_r[...].astype(F32)
        for j in range(3):
            acc = acc + arr_r[j].astype(F32)
        o_r[...] = acc

    in_specs = [pl.BlockSpec((None, tr, cols), lambda i, p_ref: (p_ref[0], i, 0)),
                pl.BlockSpec((3, tr, cols), lambda i, p_ref: (0, i, 0))]
    args = [place_arr, own, arrived]
    aliases = {}
    if dst is not None:
        in_specs.append(ANY)
        args.append(dst)
        aliases = {3: 0}
    return pl.pallas_call(
        body, name=name,
        grid_spec=pltpu.PrefetchScalarGridSpec(
            num_scalar_prefetch=1, grid=(nblk,), in_specs=in_specs,
            out_specs=pl.BlockSpec((None, tr, cols), lambda i, p_ref: (layer, p_ref[1] * nblk + i, 0))),
        out_shape=jax.ShapeDtypeStruct((n_layers, 2 * h, cols), F32),
        input_output_aliases=aliases,
        compiler_params=_params("parallel"),
    )(*args)


def _share_halves(shards, name):
    n = len(shards)
    n_layers = shards[0].shape[0]

    def body(*refs):
        ins, outs = refs[:n], refs[n:2 * n]
        send, recv = refs[2 * n:]
        x, y, c, _ = _place()
        sib = (x, y, 1 - c)

        def copy(a, l, which):
            h = ins[a].shape[1] // 2
            rows = pl.ds(pl.multiple_of(which * h, 8), h)
            return pltpu.make_async_remote_copy(
                src_ref=ins[a].at[l, rows], dst_ref=outs[a].at[l, rows], send_sem=send.at[a * n_layers + l],
                recv_sem=recv.at[a * n_layers + l], device_id=sib, device_id_type=MESH)

        for a in range(n):
            for l in range(n_layers):
                copy(a, l, c).start()
        for a in range(n):
            for l in range(n_layers):
                copy(a, l, 1 - c).wait_recv()
        for a in range(n):
            for l in range(n_layers):
                copy(a, l, c).wait_send()

    return pl.pallas_call(
        body, name=name, in_specs=[ANY] * n, out_specs=[ANY] * n,
        out_shape=[jax.ShapeDtypeStruct(a.shape, a.dtype) for a in shards],
        scratch_shapes=[pltpu.SemaphoreType.DMA((n * n_layers,))] * 2,
        input_output_aliases={a: a for a in range(n)},
    )(*shards)


def _adamw(w, g, m, v, name):
    rows, cols = w.shape
    tr = _tile(rows, max(8, (1 << 18) // cols), 8)
    c1 = 1.0 - ADAM_B1 ** ADAM_STEP
    c2 = 1.0 - ADAM_B2 ** ADAM_STEP

    def body(w_r, g_r, m_r, v_r, d_o, m_o, v_o):
        gv = g_r[...]
        mn = ADAM_B1 * m_r[...] + (1.0 - ADAM_B1) * gv
        vn = ADAM_B2 * v_r[...] + (1.0 - ADAM_B2) * (gv * gv)
        m_o[...] = mn
        v_o[...] = vn
        d_o[...] = -ADAM_LR * ((mn / c1) / (jnp.sqrt(vn / c2) + ADAM_EPS) + ADAM_WD * w_r[...])

    spec = pl.BlockSpec((tr, cols), lambda i: (i, 0))
    return pl.pallas_call(
        body, name=name, grid=(rows // tr,), in_specs=[spec] * 4, out_specs=[spec] * 3,
        out_shape=[jax.ShapeDtypeStruct((rows, cols), F32)] * 3,
        compiler_params=_params("parallel"),
    )(w, g, m, v)


BIG = ("w_in", "w_out_a", "w_out_b", "w_pool", "w_o", "w_mlp1", "w_mlp2")
SMALL = ("g_mix", "b_in", "conv_b_bias", "ln_b_g", "ln_b_b", "b_out_b", "pool_scale", "g_mlp")
CONVS = ("conv_a", "conv_b")
WEIGHTS = ("g_mix", "w_in", "b_in", "conv_a", "w_out_a", "conv_b", "conv_b_bias", "ln_b_g", "ln_b_b", "w_out_b", "b_out_b",
           "w_pool", "pool_scale", "w_o", "g_mlp", "w_mlp1", "w_mlp2", "g_final")


def _as2d(a):
    return a.reshape(-1, a.shape[-1])


def _pad_rows(a, rows):
    return jnp.pad(a, ((0, rows - a.shape[0]), (0, 0)))


def _full_weights(big_g, conv_g, rep, l, d):
    gc = d // N_GROUPS
    w = {k: v[l] for k, v in rep.items()}
    w["b_in"] = rep["b_in"][l].reshape(1, -1)
    win3 = big_g["w_in"]
    w["win3"] = win3
    w["wint"] = win3.transpose(0, 2, 1).reshape(-1, d)
    for src, dst in (("w_out_a", "woa"), ("w_out_b", "wob"), ("w_o", "wo")):
        full = big_g[src].reshape(d, d)
        w[dst] = full
        w[dst + "t"] = full.T
    wp = big_g["w_pool"].reshape(N_CHIPS, N_GROUPS, gc // N_CHIPS, gc).transpose(1, 0, 2, 3).reshape(N_GROUPS, gc, gc)
    w["wp"] = wp
    w["wpt"] = wp.transpose(0, 2, 1)
    w1_3 = big_g["w_mlp1"]
    w["w1_3"] = w1_3
    w["w1t"] = w1_3.transpose(0, 2, 1).reshape(-1, d)
    w2 = big_g["w_mlp2"].reshape(-1, d)
    w["w2"] = w2
    w["w2t"] = w2.T
    ca, cb = conv_g
    w["conv_a"] = ca
    w["conv_b"] = cb
    return w


def kernel(x, g_mix, w_in, b_in, conv_a, w_out_a, conv_b, conv_b_bias, ln_b_g, ln_b_b, w_out_b, b_out_b, w_pool, pool_scale, w_o, g_mlp, w_mlp1, w_mlp2, g_final, loss_target, m_g_mix, m_w_in, m_b_in, m_conv_a, m_w_out_a, m_conv_b, m_conv_b_bias, m_ln_b_g, m_ln_b_b, m_w_out_b, m_b_out_b, m_w_pool, m_pool_scale, m_w_o, m_g_mlp, m_w_mlp1, m_w_mlp2, m_g_final, v_g_mix, v_w_in, v_b_in, v_conv_a, v_w_out_a, v_conv_b, v_conv_b_bias, v_ln_b_g, v_ln_b_b, v_w_out_b, v_b_out_b, v_w_pool, v_pool_scale, v_w_o, v_g_mlp, v_w_mlp1, v_w_mlp2, v_g_final):
    given = dict(g_mix=g_mix, w_in=w_in, b_in=b_in, conv_a=conv_a, w_out_a=w_out_a, conv_b=conv_b, conv_b_bias=conv_b_bias, ln_b_g=ln_b_g, ln_b_b=ln_b_b, w_out_b=w_out_b, b_out_b=b_out_b, w_pool=w_pool, pool_scale=pool_scale, w_o=w_o, g_mlp=g_mlp, w_mlp1=w_mlp1, w_mlp2=w_mlp2, g_final=g_final)
    mom = dict(g_mix=m_g_mix, w_in=m_w_in, b_in=m_b_in, conv_a=m_conv_a, w_out_a=m_w_out_a, conv_b=m_conv_b, conv_b_bias=m_conv_b_bias, ln_b_g=m_ln_b_g, ln_b_b=m_ln_b_b, w_out_b=m_w_out_b, b_out_b=m_b_out_b, w_pool=m_w_pool, pool_scale=m_pool_scale, w_o=m_w_o, g_mlp=m_g_mlp, w_mlp1=m_w_mlp1, w_mlp2=m_w_mlp2, g_final=m_g_final)
    var = dict(g_mix=v_g_mix, w_in=v_w_in, b_in=v_b_in, conv_a=v_conv_a, w_out_a=v_w_out_a, conv_b=v_conv_b, conv_b_bias=v_conv_b_bias, ln_b_g=v_ln_b_g, ln_b_b=v_ln_b_b, w_out_b=v_w_out_b, b_out_b=v_b_out_b, w_pool=v_w_pool, pool_scale=v_pool_scale, w_o=v_w_o, g_mlp=v_g_mlp, w_mlp1=v_w_mlp1, w_mlp2=v_w_mlp2, g_final=v_g_final)
    n_layers = w_in.shape[0]
    s, d = x.shape[1], x.shape[2]
    dq = d // N_CHIPS
    x_idx, y_idx, c_idx = lax.axis_index("x"), lax.axis_index("y"), lax.axis_index("c")
    chip = 2 * x_idx + y_idx
    c_arr = c_idx.astype(jnp.int32).reshape(1)

    conv_rows = n_layers * (K_A + K_B)
    conv_pad = -(-conv_rows // 16) * 16
    conv_pack = _pad_rows(jnp.concatenate([conv_a[l] for l in range(n_layers)] + [conv_b[l] for l in range(n_layers)], axis=0),
                          conv_pad)
    chip_arr = chip.astype(jnp.int32).reshape(1)
    place_arr = jnp.stack([chip, c_idx]).astype(jnp.int32)

    def placed_layer(l):
        return [_place_shard(_as2d(given[k][l]), chip_arr, BF16, f"place_{k}_{l}") for k in BIG]

    gathered = _gather_shards(placed_layer(0) + [_place_shard(conv_pack, chip_arr, F32, "place_convs")], "gather_weights_0")
    conv_full = gathered[-1].transpose(1, 0, 2).reshape(conv_pad, d)
    conv_g = [(conv_full[l * K_A:(l + 1) * K_A], conv_full[n_layers * K_A + l * K_B:n_layers * K_A + (l + 1) * K_B])
              for l in range(n_layers)]
    rep = {k: given[k] for k in SMALL}

    xl = x[0]
    ws, saved = [], []
    gathered = gathered[:len(BIG)]
    for l in range(n_layers):
        w = _full_weights(dict(zip(BIG, gathered)), conv_g[l], rep, l, d)
        if l + 1 < n_layers:
            xl, sv, _, gathered = _layer_fwd(xl, w, l, _ride_gather_ici(placed_layer(l + 1)), _ride_gather_d2d)
        else:
            xl, sv, _, _ = _layer_fwd(xl, w, l)
        ws.append(w)
        saved.append(sv)
    dx, loss, dgf = _loss_head(xl, loss_target[0], g_final, "loss_head")
    loss = lax.psum(loss[0, 0], ("x", "y", "c"))

    parts, arrived, small = {}, {}, [None] * n_layers
    pending = []
    for l in reversed(range(n_layers)):
        w, sv = ws[l], saved[l]
        rider = _ride_exchange([parts[q] for q in pending]) if pending else None
        dx1, dproj, early, sm, got = _layer_bwd_early(dx, w, sv, l, rider)
        arrived.update(zip(pending, got))
        if l > 0:
            order = ("w_in",) + EARLY
            dx, dwin, dgmix, _, swapped = _layer_bwd_late(
                dx1, dproj, w, sv, l, None, lambda dwin, early=early: _ride_swap([dwin] + [early[k] for k in EARLY]))
            full = dict(early, w_in=dwin)
            for k, r in zip(order, swapped):
                parts[(l, k)] = _add_halves(full[k], r, c_arr, f"grad_add_halves_{k}_{l}")
            pending = [(l, k) for k in order]
        else:
            swapped = _comm_call(_ride_swap([early[k] for k in EARLY]), f"grad_swap_early_{l}")
            for k, r in zip(EARLY, swapped):
                parts[(l, k)] = _add_halves(early[k], r, c_arr, f"grad_add_halves_{k}_{l}")

            def w_in_rider(dwin, l=l):
                (r,) = _comm_call(_ride_swap([dwin]), f"grad_swap_w_in_{l}")
                parts[(l, "w_in")] = _add_halves(dwin, r, c_arr, f"grad_add_halves_w_in_{l}")
                return _ride_exchange([parts[(l, "w_in")]])

            dx, dwin, dgmix, got_early, got_w_in = _layer_bwd_late(
                dx1, dproj, w, sv, l, _ride_exchange([parts[(l, k)] for k in EARLY]), w_in_rider)
            arrived.update(zip([(l, k) for k in EARLY], got_early))
            arrived[(l, "w_in")] = got_w_in[0]
            pending = []
        sm["g_mix"] = dgmix
        small[l] = sm
    reduced = []
    for k in BIG:
        dst = None
        for l in range(n_layers):
            dst = _sum_chips_into(parts[(l, k)], arrived[(l, k)], dst, l, n_layers, place_arr, f"grad_sum_chips_{k}_{l}")
        reduced.append(dst)
    big_grad = dict(zip(BIG, _share_halves(reduced, "grad_share_halves")))

    small_rows = []
    for k in SMALL + CONVS:
        for l in range(n_layers):
            small_rows.append(small[l][k])
    small_rows.append(dgf)
    pack = jnp.concatenate(small_rows, axis=0)
    n_small = pack.shape[0]
    pack = _pad_rows(pack, -(-n_small // 8) * 8)
    small_sum = _sum_chips(_gather_rows(pack, "gather_small_grads"), "sum_small_grads")

    out = {}
    for k in BIG:
        shape = given[k].shape
        g2 = big_grad[k].reshape(-1, shape[-1])
        dlt, nm, nv = _adamw(given[k].reshape(g2.shape), g2, mom[k].reshape(g2.shape), var[k].reshape(g2.shape), f"adamw_{k}")
        out[k] = tuple(a.reshape(shape) for a in (g2, dlt, nm, nv))

    def rows_of(k):
        return {"b_in": 9, "conv_a": K_A, "conv_b": K_B}.get(k, 1)

    def pack_rep(src):
        rows = [src[k].reshape(n_layers * rows_of(k), d) for k in SMALL] + [src["g_final"].reshape(1, d)]
        a = jnp.concatenate(rows, axis=0)
        return _pad_rows(a, -(-a.shape[0] // 8) * 8)

    def pack_conv(src):
        a = jnp.concatenate([src[k].reshape(n_layers * rows_of(k), dq) for k in CONVS], axis=0)
        return _pad_rows(a, -(-a.shape[0] // 8) * 8)

    offs = {}
    r = 0
    for k in SMALL + CONVS:
        offs[k] = r
        r += n_layers * rows_of(k)
    offs["g_final"] = r
    n_rep = sum(n_layers * rows_of(k) for k in SMALL)
    g_rep = jnp.concatenate([small_sum[:n_rep], small_sum[offs["g_final"]:offs["g_final"] + 1]], axis=0)
    g_rep = _pad_rows(g_rep, -(-g_rep.shape[0] // 8) * 8)
    g_conv_full = small_sum[offs["conv_a"]:offs["g_final"]]
    g_conv = lax.dynamic_slice_in_dim(g_conv_full, chip * dq, dq, axis=1)
    g_conv = _pad_rows(g_conv, -(-g_conv.shape[0] // 8) * 8)
    rep_res = (g_rep,) + tuple(_adamw(pack_rep(given), g_rep, pack_rep(mom), pack_rep(var), "adamw_small"))
    conv_res = (g_conv,) + tuple(_adamw(pack_conv(given), g_conv, pack_conv(mom), pack_conv(var), "adamw_conv"))
    r = 0
    for k in SMALL:
        nr = n_layers * rows_of(k)
        out[k] = tuple(a[r:r + nr].reshape(given[k].shape) for a in rep_res)
        r += nr
    out["g_final"] = tuple(a[r].reshape(given["g_final"].shape) for a in rep_res)
    r = 0
    for k in CONVS:
        nr = n_layers * rows_of(k)
        out[k] = tuple(a[r:r + nr].reshape(given[k].shape) for a in conv_res)
        r += nr

    res = [loss, dx.reshape(x.shape)]
    for i in range(4):
        res += [out[k][i] for k in WEIGHTS]
    return tuple(res)
```

```python
import functools

import jax
import jax.numpy as jnp
from jax import lax
from jax.experimental import pallas as pl
from jax.experimental.pallas import tpu as pltpu

F32 = jnp.float32
BF16 = jnp.bfloat16
EPS = 1e-6
HALO = 32
SUB = 32
K_A = 3
K_B = 31
POOL_WINDOWS = (2, 4, 8, 16)
N_GROUPS = len(POOL_WINDOWS)
N_CHIPS = 4
N_DEV = 8
ADAM_LR = 0.001
ADAM_B1 = 0.9
ADAM_B2 = 0.999
ADAM_EPS = 1e-08
ADAM_WD = 0.01
ADAM_STEP = 10
VMEM_LIMIT = 56 * 1024 * 1024
MESH = pl.DeviceIdType.MESH
ANY = pl.BlockSpec(memory_space=pl.ANY)


def _params(*sem):
    return pltpu.CompilerParams(dimension_semantics=sem, vmem_limit_bytes=VMEM_LIMIT)


def _tile(n, pref, mult=8):
    if n <= pref:
        return n
    t = (pref // mult) * mult
    while t > mult and n % t:
        t -= mult
    assert n % t == 0, (n, pref, mult)
    return t


def _dot(a, b):
    return jnp.dot(a, b, preferred_element_type=F32)


def _sigmoid(x):
    return 1.0 / (1.0 + jnp.exp(-x))


def _f32(ref):
    return ref[...].astype(F32)


def _fold8(v):
    r, d = v.shape
    return v.reshape(r // 8, 8, d).sum(axis=0)


def _rms_bwd(dh, xv, g):
    r = lax.rsqrt(jnp.mean(xv * xv, axis=-1, keepdims=True) + EPS)
    xn = xv * r
    dxn = dh * g
    dx = r * (dxn - xn * jnp.mean(dxn * xn, axis=-1, keepdims=True))
    return dx, _fold8(dh * xn)


class _Rider:
    def __init__(self, args, out_shape, aliases, sems, start, finish, middle=None):
        self.args, self.out_shape, self.aliases, self.sems = list(args), list(out_shape), list(aliases), list(sems)
        self.start, self.finish, self.middle = start, finish, middle


def _hosted_call(body, *, name, grid, in_specs, out_specs, out_shape, scratch_shapes, args, rider=None):
    n_in, n_out, n_scr = len(in_specs), len(out_shape), len(scratch_shapes)
    params = pltpu.CompilerParams(dimension_semantics=("arbitrary",) * len(grid), vmem_limit_bytes=VMEM_LIMIT)
    if rider is None:
        res = pl.pallas_call(body, name=name, grid=grid, in_specs=in_specs, out_specs=out_specs, out_shape=out_shape,
                             scratch_shapes=scratch_shapes, compiler_params=params)(*args)
        return list(res), []
    k_in, k_out = len(rider.args), len(rider.out_shape)

    def hosted(*refs):
        ins, refs = refs[:n_in], refs[n_in:]
        r_in, refs = refs[:k_in], refs[k_in:]
        outs, refs = refs[:n_out], refs[n_out:]
        r_out, refs = refs[:k_out], refs[k_out:]
        scr, r_sems = refs[:n_scr], refs[n_scr:]
        first = functools.reduce(lambda u, v: u & v, [pl.program_id(a) == 0 for a in range(len(grid))])
        last = functools.reduce(lambda u, v: u & v, [pl.program_id(a) == grid[a] - 1 for a in range(len(grid))])

        @pl.when(first)
        def _():
            rider.start(r_in, r_out, r_sems)

        if rider.middle is not None:
            step = functools.reduce(lambda u, a: u * grid[a] + pl.program_id(a), range(len(grid)), 0)

            @pl.when(step == (3 * functools.reduce(lambda u, v: u * v, grid)) // 5)
            def _():
                rider.middle(r_in, r_out, r_sems)

        body(*ins, *outs, *scr)

        @pl.when(last)
        def _():
            rider.finish(r_in, r_out, r_sems)

    res = pl.pallas_call(
        hosted, name=name, grid=grid, in_specs=list(in_specs) + [ANY] * k_in, out_specs=list(out_specs) + [ANY] * k_out,
        out_shape=list(out_shape) + rider.out_shape, scratch_shapes=list(scratch_shapes) + rider.sems,
        input_output_aliases={n_in + i: n_out + o for i, o in rider.aliases}, compiler_params=params,
    )(*args, *rider.args)
    return list(res[:n_out]), list(res[n_out:])


def _comm_call(rider, name):
    def body(*refs):
        k_in, k_out = len(rider.args), len(rider.out_shape)
        r_in, r_out, r_sems = refs[:k_in], refs[k_in:k_in + k_out], refs[k_in + k_out:]
        rider.start(r_in, r_out, r_sems)
        if rider.middle is not None:
            rider.middle(r_in, r_out, r_sems)
        rider.finish(r_in, r_out, r_sems)

    return list(pl.pallas_call(
        body, name=name, in_specs=[ANY] * len(rider.args), out_specs=[ANY] * len(rider.out_shape),
        out_shape=rider.out_shape, scratch_shapes=rider.sems, input_output_aliases=dict(rider.aliases),
    )(*rider.args))


def _rms_fwd(x, g, name):
    s, d = x.shape
    tm = _tile(s, 512)

    def body(x_ref, g_ref, h_ref):
        xv = x_ref[...]
        r = lax.rsqrt(jnp.mean(xv * xv, axis=-1, keepdims=True) + EPS)
        h_ref[...] = (xv * r * g_ref[...]).astype(BF16)

    return pl.pallas_call(
        body, name=name, grid=(s // tm,),
        in_specs=[pl.BlockSpec((tm, d), lambda i: (i, 0)), pl.BlockSpec((1, d), lambda i: (0, 0))],
        out_specs=pl.BlockSpec((tm, d), lambda i: (i, 0)),
        out_shape=jax.ShapeDtypeStruct((s, d), BF16),
        compiler_params=_params("parallel"),
    )(x, g.reshape(1, d))


def _proj(h, w3, b, name, rider=None):
    s, k = h.shape
    nb, _, nw = w3.shape
    tm = _tile(s, 1024)
    tn = nw // 3
    per = nw // tn

    def body(h_ref, w_ref, b_ref, o_ref):
        o_ref[...] = (_dot(h_ref[...], w_ref[...]) + b_ref[...]).astype(BF16)

    (proj,), carried = _hosted_call(
        body, name=name, grid=(s // tm, nb * per),
        in_specs=[
            pl.BlockSpec((tm, k), lambda i, j: (i, 0)),
            pl.BlockSpec((None, k, tn), lambda i, j: (j // per, 0, j % per)),
            pl.BlockSpec((1, tn), lambda i, j: (0, j)),
        ],
        out_specs=[pl.BlockSpec((tm, tn), lambda i, j: (i, j))],
        out_shape=[jax.ShapeDtypeStruct((s, nb * nw), BF16)],
        scratch_shapes=[], args=(h, w3, b), rider=rider)
    return proj, carried


def _shift_copies(src_e, dst_s):
    n_rows = dst_s.shape[1]
    for b in range(1, 8):
        dst_s[b - 1, :, :] = src_e[pl.ds(b, n_rows), :]


def _window(src_e, dst_s, off, rows):
    b = off % 8
    if b == 0:
        return src_e[pl.ds(off, rows), :]
    return dst_s[b - 1, pl.ds(off - b, rows), :]


def _prev_halo(tm, d, col):
    r = tm // HALO
    return pl.BlockSpec((HALO, d), lambda i: (jnp.maximum(i * r - 1, 0), col))


def _next_halo(tm, d, col, n_halo_blocks):
    r = tm // HALO
    return pl.BlockSpec((HALO, d), lambda i: (jnp.minimum((i + 1) * r, n_halo_blocks - 1), col))


def _mix_fwd(proj, conv_a, conv_b, cb, lg, lb, name):
    s, d9 = proj.shape
    d = d9 // 9
    gc = d // N_GROUPS
    tm = _tile(s, 256, HALO)

    def tile(col):
        return pl.BlockSpec((tm, d), lambda i: (i, col))

    def vec(rows):
        return pl.BlockSpec((rows, d), lambda i: (0, 0))

    def body(ab, ac, ax, bv, bg, ci, ac_h, ax_h, bv_h, bg_h, ci_h, wa, wb, cb_r, lg_r, lb_r,
             pa_o, s_o, p_o, v_o, ua_e, ub_e, ci_e, ub_s):
        i = pl.program_id(0)
        keep = (i > 0).astype(F32)
        ua_e[0:HALO, :] = _f32(ac_h) * _f32(ax_h) * keep
        ua_e[HALO:HALO + tm, :] = _f32(ac) * _f32(ax)
        ub_e[0:HALO, :] = _f32(bv_h) * _sigmoid(_f32(bg_h)) * keep
        ub_e[HALO:HALO + tm, :] = _f32(bv) * _sigmoid(_f32(bg))
        ci_e[0:HALO, :] = _f32(ci_h) * keep
        ci_e[HALO:HALO + tm, :] = _f32(ci)
        _shift_copies(ub_e, ub_s)
        for r0 in range(0, tm, SUB):
            rows = slice(r0, r0 + SUB)
            z = None
            for k in range(K_A):
                t = wa[k:k + 1, :] * ua_e[pl.ds(HALO - (K_A - 1) + k + r0, SUB), :]
                z = t if z is None else z + t
            pa_o[rows, :] = (ab[rows, :].astype(F32) * z).astype(BF16)
            v = None
            for k in range(K_B):
                t = wb[k:k + 1, :] * _window(ub_e, ub_s, HALO - (K_B - 1) + k + r0, SUB)
                v = t if v is None else v + t
            v = v + cb_r[...]
            v_o[rows, :] = v
            mu = jnp.mean(v, axis=-1, keepdims=True)
            vc = v - mu
            rstd = lax.rsqrt(jnp.mean(vc * vc, axis=-1, keepdims=True) + EPS)
            ln = vc * rstd * lg_r[...] + lb_r[...]
            s_o[rows, :] = (ln * _sigmoid(ln)).astype(BF16)
            t_idx = i * tm + r0 + lax.broadcasted_iota(jnp.int32, (SUB, gc), 0)
            for g, w in enumerate(POOL_WINDOWS):
                cols = slice(g * gc, (g + 1) * gc)
                cur = ci_e[pl.ds(HALO + r0, SUB), cols]
                acc = cur
                for j in range(1, w):
                    acc = acc + ci_e[pl.ds(HALO + r0 - j, SUB), cols]
                cnt = jnp.minimum(t_idx + 1, w).astype(F32)
                p_o[rows, cols] = (acc / cnt - cur).astype(BF16)

    return pl.pallas_call(
        body, name=name, grid=(s // tm,),
        in_specs=[tile(0), tile(1), tile(2), tile(3), tile(4), tile(5),
                  _prev_halo(tm, d, 1), _prev_halo(tm, d, 2), _prev_halo(tm, d, 3), _prev_halo(tm, d, 4),
                  _prev_halo(tm, d, 5),
                  vec(K_A), vec(K_B), vec(1), vec(1), vec(1)],
        out_specs=[pl.BlockSpec((tm, d), lambda i: (i, 0))] * 4,
        out_shape=[jax.ShapeDtypeStruct((s, d), BF16)] * 3 + [jax.ShapeDtypeStruct((s, d), F32)],
        scratch_shapes=[pltpu.VMEM((HALO + tm, d), F32)] * 3 + [pltpu.VMEM((7, tm + HALO - 8, d), F32)],
        compiler_params=_params("parallel"),
    )(proj, proj, proj, proj, proj, proj, proj, proj, proj, proj, proj,
      conv_a, conv_b, cb.reshape(1, d), lg.reshape(1, d), lb.reshape(1, d))


def _branch_out(pa, sb, p, proj, x0, woa, wob, wp, wo, bob, ps, gm, name, rider=None):
    s, d = pa.shape
    gc = d // N_GROUPS
    tm = _tile(s, 512)

    def tile(col=0):
        return pl.BlockSpec((tm, d), lambda i: (i, col))

    def const(shape):
        return pl.BlockSpec(shape, lambda i: (0,) * len(shape))

    def body(pa_r, s_r, p_r, g0, g1, g2, x0_r, woa_r, wob_r, wp_r, wo_r, bob_r, ps_r, gm_r,
             ya_o, yb_o, yc_o, mg_o, x1_o, h2_o, yc_s):
        ya = _dot(pa_r[...], woa_r[...])
        yb = _dot(s_r[...], wob_r[...]) + bob_r[...]
        for g in range(N_GROUPS):
            cols = slice(g * gc, (g + 1) * gc)
            yc_s[:, cols] = _dot(p_r[:, cols], wp_r[g])
        yc = yc_s[...]
        ya_o[...] = ya.astype(BF16)
        yb_o[...] = yb.astype(BF16)
        yc_o[...] = yc.astype(BF16)
        m = _sigmoid(_f32(g0)) * ya + _sigmoid(_f32(g1)) * yb + _sigmoid(_f32(g2)) * (yc * ps_r[...])
        mb = m.astype(BF16)
        mg_o[...] = mb
        x1 = x0_r[...] + _dot(mb, wo_r[...])
        x1_o[...] = x1
        r = lax.rsqrt(jnp.mean(x1 * x1, axis=-1, keepdims=True) + EPS)
        h2_o[...] = (x1 * r * gm_r[...]).astype(BF16)

    return _hosted_call(
        body, name=name, grid=(s // tm,),
        in_specs=[tile(), tile(), tile(), tile(6), tile(7), tile(8), tile(),
                  const((d, d)), const((d, d)), const((N_GROUPS, gc, gc)), const((d, d)),
                  const((1, d)), const((1, d)), const((1, d))],
        out_specs=[tile()] * 6,
        out_shape=[jax.ShapeDtypeStruct((s, d), BF16)] * 4 + [jax.ShapeDtypeStruct((s, d), F32),
                                                               jax.ShapeDtypeStruct((s, d), BF16)],
        scratch_shapes=[pltpu.VMEM((tm, d), F32)],
        args=(pa, sb, p, proj, proj, proj, x0, woa, wob, wp, wo, bob.reshape(1, d), ps.reshape(1, d), gm.reshape(1, d)),
        rider=rider)


def _mlp_fwd(h2, x1, w1_3, w2, name, rider=None):
    s, d = h2.shape
    nf, _, tf = w1_3.shape
    tm = _tile(s, 1024)

    def body(h_r, x1_r, w1_r, w2_r, f_o, a_o, x2_o, acc):
        j = pl.program_id(1)

        @pl.when(j == 0)
        def _():
            acc[...] = jnp.zeros_like(acc)

        f = _dot(h_r[...], w1_r[...])
        f_o[...] = f.astype(BF16)
        rl = jnp.maximum(f, 0.0)
        a = (rl * rl).astype(BF16)
        a_o[...] = a
        acc[...] += _dot(a, w2_r[...])

        @pl.when(j == nf - 1)
        def _():
            x2_o[...] = x1_r[...] + acc[...]

    return _hosted_call(
        body, name=name, grid=(s // tm, nf),
        in_specs=[pl.BlockSpec((tm, d), lambda i, j: (i, 0)), pl.BlockSpec((tm, d), lambda i, j: (i, 0)),
                  pl.BlockSpec((None, d, tf), lambda i, j: (j, 0, 0)), pl.BlockSpec((tf, d), lambda i, j: (j, 0))],
        out_specs=[pl.BlockSpec((tm, tf), lambda i, j: (i, j)), pl.BlockSpec((tm, tf), lambda i, j: (i, j)),
                   pl.BlockSpec((tm, d), lambda i, j: (i, 0))],
        out_shape=[jax.ShapeDtypeStruct((s, nf * tf), BF16), jax.ShapeDtypeStruct((s, nf * tf), BF16),
                   jax.ShapeDtypeStruct((s, d), F32)],
        scratch_shapes=[pltpu.VMEM((tm, d), F32)], args=(h2, x1, w1_3, w2), rider=rider)


def _loss_head(x, target, gf, name):
    s, d = x.shape
    tm = _tile(s, 512)
    n = s // tm

    def body(x_r, t_r, g_r, dx_o, loss_o, dg_o, lacc, gacc):
        i = pl.program_id(0)

        @pl.when(i == 0)
        def _():
            lacc[...] = jnp.zeros_like(lacc)
            gacc[...] = jnp.zeros_like(gacc)

        xv = x_r[...]
        r = lax.rsqrt(jnp.mean(xv * xv, axis=-1, keepdims=True) + EPS)
        xn = xv * r
        e = xn * g_r[...] - t_r[...]
        lacc[...] += _fold8(e * e)
        dy = e * (1.0 / d)
        gacc[...] += _fold8(dy * xn)
        dxn = dy * g_r[...]
        dx_o[...] = r * (dxn - xn * jnp.mean(dxn * xn, axis=-1, keepdims=True))

        @pl.when(i == n - 1)
        def _():
            loss_o[...] = jnp.sum(lacc[...]).reshape(1, 1) * (0.5 / d)
            dg_o[...] = jnp.sum(gacc[...], axis=0, keepdims=True)

    return pl.pallas_call(
        body, name=name, grid=(n,),
        in_specs=[pl.BlockSpec((tm, d), lambda i: (i, 0)), pl.BlockSpec((tm, d), lambda i: (i, 0)),
                  pl.BlockSpec((1, d), lambda i: (0, 0))],
        out_specs=[pl.BlockSpec((tm, d), lambda i: (i, 0)), pl.BlockSpec((1, 1), lambda i: (0, 0)),
                   pl.BlockSpec((1, d), lambda i: (0, 0))],
        out_shape=[jax.ShapeDtypeStruct((s, d), F32), jax.ShapeDtypeStruct((1, 1), F32),
                   jax.ShapeDtypeStruct((1, d), F32)],
        scratch_shapes=[pltpu.VMEM((8, d), F32), pltpu.VMEM((8, d), F32)],
        compiler_params=_params("arbitrary"),
    )(x, target, gf.reshape(1, d))


def _tn(a, b, nb, name, rider=None):
    s, m = a.shape
    n = b.shape[1]
    nw = n // nb
    tmm = _tile(m, 1024, 128)
    tn = nw if nw <= 1152 else nw // 2
    per = nw // tn
    ts = _tile(s, 1024)
    ns = s // ts

    def body(a_r, b_r, o_r, acc):
        k = pl.program_id(2)

        @pl.when(k == 0)
        def _():
            acc[...] = jnp.zeros_like(acc)

        acc[...] += lax.dot_general(a_r[...].astype(BF16), b_r[...].astype(BF16), (((0,), (0,)), ((), ())),
                                    preferred_element_type=F32)

        @pl.when(k == ns - 1)
        def _():
            o_r[...] = acc[...]

    (out,), carried = _hosted_call(
        body, name=name, grid=(m // tmm, nb * per, ns),
        in_specs=[pl.BlockSpec((ts, tmm), lambda i, j, k: (k, i)), pl.BlockSpec((ts, tn), lambda i, j, k: (k, j))],
        out_specs=[pl.BlockSpec((None, tmm, tn), lambda i, j, k: (j // per, i, j % per))],
        out_shape=[jax.ShapeDtypeStruct((nb, m, nw), F32)],
        scratch_shapes=[pltpu.VMEM((tmm, tn), F32)], args=(a, b), rider=rider)
    return (out, carried) if rider is not None else out


def _tn_groups(a, b, name):
    s, d = a.shape
    gc = d // N_GROUPS
    ts = _tile(s, 2048)
    ns = s // ts

    def body(a_r, b_r, o_r, acc):
        k = pl.program_id(1)

        @pl.when(k == 0)
        def _():
            acc[...] = jnp.zeros_like(acc)

        acc[...] += lax.dot_general(a_r[...], b_r[...], (((0,), (0,)), ((), ())), preferred_element_type=F32)

        @pl.when(k == ns - 1)
        def _():
            o_r[...] = acc[...]

    return pl.pallas_call(
        body, name=name, grid=(N_GROUPS, ns),
        in_specs=[pl.BlockSpec((ts, gc), lambda g, k: (k, g)), pl.BlockSpec((ts, gc), lambda g, k: (k, g))],
        out_specs=pl.BlockSpec((None, gc, gc), lambda g, k: (g, 0, 0)),
        out_shape=jax.ShapeDtypeStruct((N_GROUPS, gc, gc), F32),
        scratch_shapes=[pltpu.VMEM((gc, gc), F32)],
        compiler_params=_params("parallel", "arbitrary"),
    )(a, b)


def _mlp_bwd(dx2, f, x1, w2t, w1t, gm, name, rider=None):
    s, d = dx2.shape
    ff = f.shape[1]
    tf = _tile(ff, 1024, 128)
    nf = ff // tf
    tm = _tile(s, 1024)
    n = s // tm

    def body(dx2_r, f_r, x1_r, w2t_r, w1t_r, gm_r, df_o, dx1_o, dg_o, dxb, acc, gacc):
        i = pl.program_id(0)
        j = pl.program_id(1)

        @pl.when((i == 0) & (j == 0))
        def _():
            gacc[...] = jnp.zeros_like(gacc)

        @pl.when(j == 0)
        def _():
            dxb[...] = dx2_r[...].astype(BF16)
            acc[...] = jnp.zeros_like(acc)

        da = _dot(dxb[...], w2t_r[...])
        df = (da * (2.0 * jnp.maximum(f_r[...].astype(F32), 0.0))).astype(BF16)
        df_o[...] = df
        acc[...] += _dot(df, w1t_r[...])

        @pl.when(j == nf - 1)
        def _():
            dx, dg = _rms_bwd(acc[...], x1_r[...], gm_r[...])
            gacc[...] += dg
            dx1_o[...] = dx2_r[...] + dx

        @pl.when((i == n - 1) & (j == nf - 1))
        def _():
            dg_o[...] = jnp.sum(gacc[...], axis=0, keepdims=True)

    return _hosted_call(
        body, name=name, grid=(n, nf),
        in_specs=[pl.BlockSpec((tm, d), lambda i, j: (i, 0)), pl.BlockSpec((tm, tf), lambda i, j: (i, j)),
                  pl.BlockSpec((tm, d), lambda i, j: (i, 0)), pl.BlockSpec((d, tf), lambda i, j: (0, j)),
                  pl.BlockSpec((tf, d), lambda i, j: (j, 0)), pl.BlockSpec((1, d), lambda i, j: (0, 0))],
        out_specs=[pl.BlockSpec((tm, tf), lambda i, j: (i, j)), pl.BlockSpec((tm, d), lambda i, j: (i, 0)),
                   pl.BlockSpec((1, d), lambda i, j: (0, 0))],
        out_shape=[jax.ShapeDtypeStruct((s, ff), BF16), jax.ShapeDtypeStruct((s, d), F32),
                   jax.ShapeDtypeStruct((1, d), F32)],
        scratch_shapes=[pltpu.VMEM((tm, d), BF16), pltpu.VMEM((tm, d), F32), pltpu.VMEM((8, d), F32)],
        args=(dx2, f, x1, w2t, w1t, gm.reshape(1, d)), rider=rider)


def _branch_out_bwd(dx1, proj, ya, yb, yc, wot, woat, wobt, wpt, ps, name):
    s, d = dx1.shape
    gc = d // N_GROUPS
    tm = _tile(s, 512)
    n = s // tm

    def tile(col=0):
        return pl.BlockSpec((tm, d), lambda i: (i, col))

    def const(shape):
        return pl.BlockSpec(shape, lambda i: (0,) * len(shape))

    def body(dx1_r, g0, g1, g2, ya_r, yb_r, yc_r, wot_r, woat_r, wobt_r, wpt_r, ps_r,
             dpa_o, ds_o, dp_o, dpj_o, dya_o, dyb_o, dyc_o, dbob_o, dps_o, dbg_o, sacc, gacc):
        i = pl.program_id(0)

        @pl.when(i == 0)
        def _():
            sacc[...] = jnp.zeros_like(sacc)
            gacc[...] = jnp.zeros_like(gacc)

        dm = _dot(dx1_r[...].astype(BF16), wot_r[...])
        ycp = _f32(yc_r)
        ys = (_f32(ya_r), _f32(yb_r), ycp * ps_r[...])
        dys = []
        for b, g_r in enumerate((g0, g1, g2)):
            sg = _sigmoid(_f32(g_r))
            dgt = dm * ys[b] * sg * (1.0 - sg)
            dpj_o[:, b * d:(b + 1) * d] = dgt.astype(BF16)
            gacc[b] += _fold8(dgt)
            dys.append(dm * sg)
        dya, dyb, dyc = dys
        sacc[0] += _fold8(dyb)
        sacc[1] += _fold8(dyc * ycp)
        dyab = dya.astype(BF16)
        dybb = dyb.astype(BF16)
        dycb = (dyc * ps_r[...]).astype(BF16)
        dya_o[...] = dyab
        dyb_o[...] = dybb
        dyc_o[...] = dycb
        dpa_o[...] = _dot(dyab, woat_r[...]).astype(BF16)
        ds_o[...] = _dot(dybb, wobt_r[...]).astype(BF16)
        for g in range(N_GROUPS):
            cols = slice(g * gc, (g + 1) * gc)
            dp_o[:, cols] = _dot(dycb[:, cols], wpt_r[g]).astype(BF16)

        @pl.when(i == n - 1)
        def _():
            dbob_o[...] = jnp.sum(sacc[0], axis=0, keepdims=True)
            dps_o[...] = jnp.sum(sacc[1], axis=0, keepdims=True)
            for b in range(3):
                dbg_o[:, b * d:(b + 1) * d] = jnp.sum(gacc[b], axis=0, keepdims=True)

    return pl.pallas_call(
        body, name=name, grid=(n,),
        in_specs=[tile(), tile(6), tile(7), tile(8), tile(), tile(), tile(),
                  const((d, d)), const((d, d)), const((d, d)), const((N_GROUPS, gc, gc)), const((1, d))],
        out_specs=[tile(), tile(), tile(), pl.BlockSpec((tm, 3 * d), lambda i: (i, 2)), tile(), tile(), tile(),
                   const((1, d)), const((1, d)), const((1, 3 * d))],
        out_shape=[jax.ShapeDtypeStruct((s, d), BF16)] * 3 + [jax.ShapeDtypeStruct((s, 9 * d), BF16)]
        + [jax.ShapeDtypeStruct((s, d), BF16)] * 3
        + [jax.ShapeDtypeStruct((1, d), F32), jax.ShapeDtypeStruct((1, d), F32), jax.ShapeDtypeStruct((1, 3 * d), F32)],
        scratch_shapes=[pltpu.VMEM((2, 8, d), F32), pltpu.VMEM((3, 8, d), F32)],
        compiler_params=_params("arbitrary"),
    )(dx1, proj, proj, proj, ya, yb, yc, wot, woat, wobt, wpt, ps.reshape(1, d))


def _ln_silu_bwd(v, ds, lg, lb, name, rider=None):
    s, d = v.shape
    tm = _tile(s, 256)
    n = s // tm

    def tile():
        return pl.BlockSpec((tm, d), lambda i: (i, 0))

    def vec():
        return pl.BlockSpec((1, d), lambda i: (0, 0))

    def body(v_r, ds_r, lg_r, lb_r, dv_o, dlg_o, dlb_o, dcb_o, acc):
        i = pl.program_id(0)

        @pl.when(i == 0)
        def _():
            acc[...] = jnp.zeros_like(acc)

        for r0 in range(0, tm, SUB):
            rows = slice(r0, r0 + SUB)
            vv = v_r[rows, :]
            mu = jnp.mean(vv, axis=-1, keepdims=True)
            vc = vv - mu
            rstd = lax.rsqrt(jnp.mean(vc * vc, axis=-1, keepdims=True) + EPS)
            nrm = vc * rstd
            ln = nrm * lg_r[...] + lb_r[...]
            sg = _sigmoid(ln)
            dln = ds_r[rows, :].astype(F32) * (sg * (1.0 + ln * (1.0 - sg)))
            acc[0] += _fold8(dln * nrm)
            acc[1] += _fold8(dln)
            dn = dln * lg_r[...]
            dv = rstd * (dn - jnp.mean(dn, axis=-1, keepdims=True)
                         - nrm * jnp.mean(dn * nrm, axis=-1, keepdims=True))
            acc[2] += _fold8(dv)
            dv_o[rows, :] = dv

        @pl.when(i == n - 1)
        def _():
            dlg_o[...] = jnp.sum(acc[0], axis=0, keepdims=True)
            dlb_o[...] = jnp.sum(acc[1], axis=0, keepdims=True)
            dcb_o[...] = jnp.sum(acc[2], axis=0, keepdims=True)

    return _hosted_call(
        body, name=name, grid=(n,),
        in_specs=[tile(), tile(), vec(), vec()],
        out_specs=[tile(), vec(), vec(), vec()],
        out_shape=[jax.ShapeDtypeStruct((s, d), F32)] + [jax.ShapeDtypeStruct((1, d), F32)] * 3,
        scratch_shapes=[pltpu.VMEM((3, 8, d), F32)],
        args=(v, ds, lg.reshape(1, d), lb.reshape(1, d)), rider=rider)


def _mix_bwd(proj, dpa, dv, dp, dproj, conv_a, conv_b, name):
    s, d9 = proj.shape
    d = d9 // 9
    gc = d // N_GROUPS
    tm = _tile(s, 256, HALO)
    n = s // tm
    nh = s // HALO

    def tile(col=0):
        return pl.BlockSpec((tm, d), lambda i: (i, col))

    def vec(rows):
        return pl.BlockSpec((rows, d), lambda i: (0, 0))

    def body(ab, ac, ax, bv, bg, ac_h, ax_h, bv_h, bg_h, dpa_r, dv_r, dp_r, ab_n, dpa_n, dv_n, dp_n, wa, wb, _alias,
             dpj_o, dbin_o, dca_o, dcb_o, ua_e, ub_e, dz_e, dv_e, q_e, bacc, cacc_a, cacc_b, ub_s, dv_s):
        i = pl.program_id(0)

        @pl.when(i == 0)
        def _():
            bacc[...] = jnp.zeros_like(bacc)
            cacc_a[...] = jnp.zeros_like(cacc_a)
            cacc_b[...] = jnp.zeros_like(cacc_b)

        keep_p = (i > 0).astype(F32)
        keep_n = (i < n - 1).astype(F32)
        ua_e[0:HALO, :] = _f32(ac_h) * _f32(ax_h) * keep_p
        ua_e[HALO:HALO + tm, :] = _f32(ac) * _f32(ax)
        ub_e[0:HALO, :] = _f32(bv_h) * _sigmoid(_f32(bg_h)) * keep_p
        ub_e[HALO:HALO + tm, :] = _f32(bv) * _sigmoid(_f32(bg))
        dz_e[0:tm, :] = _f32(dpa_r) * _f32(ab)
        dz_e[tm:tm + HALO, :] = _f32(dpa_n) * _f32(ab_n) * keep_n
        dv_e[0:tm, :] = dv_r[...]
        dv_e[tm:tm + HALO, :] = dv_n[...] * keep_n
        _shift_copies(ub_e, ub_s)
        _shift_copies(dv_e, dv_s)
        for g, w in enumerate(POOL_WINDOWS):
            cols = slice(g * gc, (g + 1) * gc)
            t_idx = i * tm + lax.broadcasted_iota(jnp.int32, (tm + HALO, gc), 0)
            cnt = jnp.minimum(t_idx + 1, w).astype(F32)
            q_e[0:tm, cols] = dp_r[:, cols].astype(F32) / cnt[0:tm]
            q_e[tm:tm + HALO, cols] = dp_n[:, cols].astype(F32) * keep_n / cnt[tm:tm + HALO]
        for r0 in range(0, tm, SUB):
            rows = slice(r0, r0 + SUB)
            dz = dz_e[pl.ds(r0, SUB), :]
            z = None
            du = None
            for k in range(K_A):
                u_k = ua_e[pl.ds(HALO - (K_A - 1) + k + r0, SUB), :]
                t = wa[k:k + 1, :] * u_k
                z = t if z is None else z + t
                cacc_a[k] += _fold8(dz * u_k)
                t = wa[k:k + 1, :] * dz_e[pl.ds(r0 + (K_A - 1) - k, SUB), :]
                du = t if du is None else du + t
            outs = [dpa_r[rows, :].astype(F32) * z, du * ax[rows, :].astype(F32), du * ac[rows, :].astype(F32)]
            dvv = dv_e[pl.ds(r0, SUB), :]
            du = None
            for k in range(K_B):
                cacc_b[k] += _fold8(dvv * _window(ub_e, ub_s, HALO - (K_B - 1) + k + r0, SUB))
                t = wb[k:k + 1, :] * _window(dv_e, dv_s, r0 + (K_B - 1) - k, SUB)
                du = t if du is None else du + t
            sg = _sigmoid(bg[rows, :].astype(F32))
            outs.append(du * sg)
            outs.append(du * bv[rows, :].astype(F32) * sg * (1.0 - sg))
            for b, o in enumerate(outs):
                dpj_o[rows, b * d:(b + 1) * d] = o.astype(BF16)
                bacc[b] += _fold8(o)
            for g, w in enumerate(POOL_WINDOWS):
                cols = slice(g * gc, (g + 1) * gc)
                acc = q_e[pl.ds(r0, SUB), cols]
                for j in range(1, w):
                    acc = acc + q_e[pl.ds(r0 + j, SUB), cols]
                o = acc - dp_r[rows, cols].astype(F32)
                dpj_o[rows, 5 * d + g * gc:5 * d + (g + 1) * gc] = o.astype(BF16)
                bacc[5, :, cols] += _fold8(o)

        @pl.when(i == n - 1)
        def _():
            for b in range(6):
                dbin_o[:, b * d:(b + 1) * d] = jnp.sum(bacc[b], axis=0, keepdims=True)
            dca_o[...] = jnp.sum(cacc_a[...], axis=1)
            dcb_o[...] = jnp.sum(cacc_b[...], axis=1)

    ka8 = 8
    kb8 = 32
    return pl.pallas_call(
        body, name=name, grid=(n,),
        in_specs=[tile(0), tile(1), tile(2), tile(3), tile(4),
                  _prev_halo(tm, d, 1), _prev_halo(tm, d, 2), _prev_halo(tm, d, 3), _prev_halo(tm, d, 4),
                  tile(), tile(), tile(),
                  _next_halo(tm, d, 0, nh), _next_halo(tm, d, 0, nh), _next_halo(tm, d, 0, nh), _next_halo(tm, d, 0, nh),
                  vec(K_A), vec(K_B), ANY],
        out_specs=[pl.BlockSpec((tm, 6 * d), lambda i: (i, 0)), pl.BlockSpec((1, 6 * d), lambda i: (0, 0)),
                   pl.BlockSpec((ka8, d), lambda i: (0, 0)), pl.BlockSpec((kb8, d), lambda i: (0, 0))],
        out_shape=[jax.ShapeDtypeStruct((s, 9 * d), BF16), jax.ShapeDtypeStruct((1, 6 * d), F32),
                   jax.ShapeDtypeStruct((ka8, d), F32), jax.ShapeDtypeStruct((kb8, d), F32)],
        scratch_shapes=[pltpu.VMEM((HALO + tm, d), F32), pltpu.VMEM((HALO + tm, d), F32),
                        pltpu.VMEM((tm + HALO, d), F32), pltpu.VMEM((tm + HALO, d), F32),
                        pltpu.VMEM((tm + HALO, d), F32),
                        pltpu.VMEM((6, 8, d), F32), pltpu.VMEM((ka8, 8, d), F32), pltpu.VMEM((kb8, 8, d), F32),
                        pltpu.VMEM((7, tm + HALO - 8, d), F32), pltpu.VMEM((7, tm + HALO - 8, d), F32)],
        input_output_aliases={18: 0},
        compiler_params=_params("arbitrary"),
    )(proj, proj, proj, proj, proj, proj, proj, proj, proj, dpa, dv, dp, proj, dpa, dv, dp, conv_a, conv_b, dproj)


def _dh_rms_bwd(dproj, wint, x0, dx1, g, name, rider=None):
    s, p = dproj.shape
    d = x0.shape[1]
    tm = _tile(s, 1024)
    tk = _tile(p, 1536, 128)
    n = s // tm
    nk = p // tk

    def body(dp_r, w_r, x_r, dx1_r, g_r, dx_o, dg_o, acc, gacc):
        i = pl.program_id(0)
        k = pl.program_id(1)

        @pl.when((i == 0) & (k == 0))
        def _():
            gacc[...] = jnp.zeros_like(gacc)

        @pl.when(k == 0)
        def _():
            acc[...] = jnp.zeros_like(acc)

        acc[...] += _dot(dp_r[...], w_r[...])

        @pl.when(k == nk - 1)
        def _():
            dx, dg = _rms_bwd(acc[...], x_r[...], g_r[...])
            gacc[...] += dg
            dx_o[...] = dx1_r[...] + dx

        @pl.when((i == n - 1) & (k == nk - 1))
        def _():
            dg_o[...] = jnp.sum(gacc[...], axis=0, keepdims=True)

    return _hosted_call(
        body, name=name, grid=(n, nk),
        in_specs=[pl.BlockSpec((tm, tk), lambda i, k: (i, k)), pl.BlockSpec((tk, d), lambda i, k: (k, 0)),
                  pl.BlockSpec((tm, d), lambda i, k: (i, 0)), pl.BlockSpec((tm, d), lambda i, k: (i, 0)),
                  pl.BlockSpec((1, d), lambda i, k: (0, 0))],
        out_specs=[pl.BlockSpec((tm, d), lambda i, k: (i, 0)), pl.BlockSpec((1, d), lambda i, k: (0, 0))],
        out_shape=[jax.ShapeDtypeStruct((s, d), F32), jax.ShapeDtypeStruct((1, d), F32)],
        scratch_shapes=[pltpu.VMEM((tm, d), F32), pltpu.VMEM((8, d), F32)],
        args=(dproj, wint, x0, dx1, g.reshape(1, d)), rider=rider)


EARLY = ("w_out_a", "w_out_b", "w_pool", "w_o", "w_mlp1", "w_mlp2")


def _layer_fwd(x0, l, g_mix, win3, b_in, weights, proj_rider=None, bo_rider=None, mlp_rider=None):
    h = _rms_fwd(x0, g_mix, f"rms_fwd_{l}")
    proj, carried_proj = _proj(h, win3, b_in, f"proj_{l}", proj_rider)
    w = weights(carried_proj) if callable(weights) else weights
    pa, sb, p, v = _mix_fwd(proj, w["conv_a"], w["conv_b"], w["conv_b_bias"], w["ln_b_g"], w["ln_b_b"], f"mix_fwd_{l}")
    (ya, yb, yc, mg, x1, h2), carried_bo = _branch_out(pa, sb, p, proj, x0, w["woa"], w["wob"], w["wp"], w["wo"],
                                                       w["b_out_b"], w["pool_scale"], w["g_mlp"], f"branch_out_{l}", bo_rider)
    rider = mlp_rider(carried_bo) if callable(mlp_rider) else mlp_rider
    (f, a, x2), carried_mlp = _mlp_fwd(h2, x1, w["w1_3"], w["w2"], f"mlp_fwd_{l}", rider)
    saved = dict(x0=x0, h=h, proj=proj, pa=pa, sb=sb, p=p, v=v, ya=ya, yb=yb, yc=yc, mg=mg, x1=x1, h2=h2, f=f, a=a)
    return x2, saved, w, carried_mlp


def _layer_bwd_early(dx2, w, sv, l, mlp_rider=None, dw2_rider=None, swap_early=False):
    d = dx2.shape[1]
    (df, dx1, dgm), carried_mlp = _mlp_bwd(dx2, sv["f"], sv["x1"], w["w2t"], w["w1t"], w["g_mlp"], f"mlp_bwd_{l}",
                                           mlp_rider)
    if dw2_rider is None:
        dw2, carried_dw2 = _tn(sv["a"], dx2, 1, f"dw2_{l}"), []
    else:
        dw2, carried_dw2 = _tn(sv["a"], dx2, 1, f"dw2_{l}", dw2_rider)
    dw1 = _tn(sv["h2"], df, N_CHIPS, f"dw1_{l}")
    dpa, ds, dp, dproj, dya, dyb, dyc, dbob, dps, dbg = _branch_out_bwd(
        dx1, sv["proj"], sv["ya"], sv["yb"], sv["yc"], w["wot"], w["woat"], w["wobt"], w["wpt"], w["pool_scale"],
        f"branch_out_bwd_{l}")
    dwo = _tn(sv["mg"], dx1, 1, f"dwo_{l}")
    dwoa = _tn(sv["pa"], dya, 1, f"dwoa_{l}")
    dwob = _tn(sv["sb"], dyb, 1, f"dwob_{l}")
    dwp = _tn_groups(sv["p"], dyc, f"dwp_{l}")
    ff = dw2.shape[1]
    gc = d // N_GROUPS
    big = dict(
        w_out_a=dwoa.reshape(N_CHIPS, d // N_CHIPS, d),
        w_out_b=dwob.reshape(N_CHIPS, d // N_CHIPS, d),
        w_pool=dwp.reshape(N_GROUPS, N_CHIPS, gc // N_CHIPS, gc).transpose(1, 0, 2, 3).reshape(N_CHIPS, gc, gc),
        w_o=dwo.reshape(N_CHIPS, d // N_CHIPS, d),
        w_mlp1=dw1,
        w_mlp2=dw2.reshape(N_CHIPS, ff // N_CHIPS, d),
    )
    swap = _ride_swap([big[k] for k in EARLY]) if swap_early else None
    (dv, dlg, dlb, dcb), swapped = _ln_silu_bwd(sv["v"], ds, w["ln_b_g"], w["ln_b_b"], f"ln_silu_bwd_{l}", swap)
    dproj, dbm, dca, dcvb = _mix_bwd(sv["proj"], dpa, dv, dp, dproj, w["conv_a"], w["conv_b"], f"mix_bwd_{l}")
    small = dict(
        b_in=jnp.concatenate([dbm, dbg], axis=1).reshape(9, d), conv_a=dca[:K_A], conv_b=dcvb[:K_B],
        conv_b_bias=dcb, ln_b_g=dlg, ln_b_b=dlb, b_out_b=dbob, pool_scale=dps, g_mlp=dgm,
    )
    return dx1, dproj, big, small, carried_mlp, carried_dw2, swapped


def _layer_bwd_late(dx1, dproj, w, sv, l, dwin_rider=None, dh_rider=None):
    if dwin_rider is None:
        dwin, carried_dwin = _tn(sv["h"], dproj, N_CHIPS, f"dwin_{l}"), []
    else:
        dwin, carried_dwin = _tn(sv["h"], dproj, N_CHIPS, f"dwin_{l}", dwin_rider)
    rider = dh_rider(dwin) if callable(dh_rider) else dh_rider
    (dx0, dgmix), carried_dh = _dh_rms_bwd(dproj, w["wint"], sv["x0"], dx1, w["g_mix"], f"dh_{l}", rider)
    return dx0, dwin, dgmix, carried_dwin, carried_dh


def _place():
    x, y, c = lax.axis_index("x"), lax.axis_index("y"), lax.axis_index("c")
    chips = [(1 - x, y), (x, 1 - y), (1 - x, 1 - y)]
    return x, y, c, chips


def _place_shard(w, chip_arr, dtype, name):
    rows, cols = w.shape
    tr = _tile(rows, max(16, (1 << 19) // cols), 16)

    def body(k_ref, w_r, o_r):
        o_r[...] = w_r[...].astype(dtype)

    return pl.pallas_call(
        body, name=name,
        grid_spec=pltpu.PrefetchScalarGridSpec(
            num_scalar_prefetch=1, grid=(rows // tr,),
            in_specs=[pl.BlockSpec((tr, cols), lambda i, k_ref: (i, 0))],
            out_specs=pl.BlockSpec((None, tr, cols), lambda i, k_ref: (k_ref[0], i, 0))),
        out_shape=jax.ShapeDtypeStruct((N_CHIPS, rows, cols), dtype),
        compiler_params=_params("parallel"),
    )(chip_arr, w)


def _gather_rows(v, name):
    def body(v_ref, o_ref, send, recv, lsem):
        x, y, c = lax.axis_index("x"), lax.axis_index("y"), lax.axis_index("c")
        me = 4 * x + 2 * y + c
        local = pltpu.make_async_copy(v_ref, o_ref.at[me], lsem)
        local.start()

        def copy(m, src_dev):
            peer = (x ^ ((m >> 2) & 1), y ^ ((m >> 1) & 1), c ^ (m & 1))
            return pltpu.make_async_remote_copy(
                src_ref=v_ref, dst_ref=o_ref.at[src_dev], send_sem=send.at[m - 1], recv_sem=recv.at[m - 1],
                device_id=peer, device_id_type=MESH)

        for m in range(1, N_DEV):
            copy(m, me).start()
        for m in range(1, N_DEV):
            copy(m, me ^ m).wait_recv()
        for m in range(1, N_DEV):
            copy(m, me).wait_send()
        local.wait()

    return pl.pallas_call(
        body, name=name, in_specs=[ANY], out_specs=ANY,
        out_shape=jax.ShapeDtypeStruct((N_DEV, *v.shape), v.dtype),
        scratch_shapes=[pltpu.SemaphoreType.DMA((N_DEV - 1,)), pltpu.SemaphoreType.DMA((N_DEV - 1,)),
                        pltpu.SemaphoreType.DMA],
    )(v)


def _ride_swap(grads):
    n = len(grads)

    def copy(ins, outs, sems, a, k):
        x, y, c, _ = _place()
        h = ins[a].shape[1] // 2
        return pltpu.make_async_remote_copy(
            src_ref=ins[a].at[k, pl.ds(pl.multiple_of((1 - c) * h, 8), h)], dst_ref=outs[a].at[k],
            send_sem=sems[0].at[a * N_CHIPS + k], recv_sem=sems[1].at[a * N_CHIPS + k], device_id=(x, y, 1 - c),
            device_id_type=MESH)

    def start(ins, outs, sems):
        for a in range(n):
            for k in range(N_CHIPS):
                copy(ins, outs, sems, a, k).start()

    def finish(ins, outs, sems):
        for a in range(n):
            for k in range(N_CHIPS):
                copy(ins, outs, sems, a, k).wait()

    return _Rider(grads, [jax.ShapeDtypeStruct((N_CHIPS, g.shape[1] // 2, g.shape[2]), g.dtype) for g in grads], [],
                  [pltpu.SemaphoreType.DMA((N_CHIPS * n,))] * 2, start, finish)


def _add_halves(g, r, c_arr, name):
    nk, rows, cols = g.shape
    h = rows // 2
    tr = _tile(h, max(8, (1 << 19) // cols), 8)
    nblk = h // tr

    def body(c_ref, g_r, r_r, o_r):
        o_r[...] = (g_r[...] + r_r[...]).astype(BF16)

    return pl.pallas_call(
        body, name=name,
        grid_spec=pltpu.PrefetchScalarGridSpec(
            num_scalar_prefetch=1, grid=(nk, nblk),
            in_specs=[pl.BlockSpec((None, tr, cols), lambda k, i, c_ref: (k, c_ref[0] * nblk + i, 0)),
                      pl.BlockSpec((None, tr, cols), lambda k, i, c_ref: (k, i, 0))],
            out_specs=pl.BlockSpec((None, tr, cols), lambda k, i, c_ref: (k, i, 0))),
        out_shape=jax.ShapeDtypeStruct((nk, h, cols), BF16),
        compiler_params=_params("parallel", "parallel"),
    )(c_arr, g, r)


def _ride_exchange(parts):
    n = len(parts)

    def copy(ins, outs, sems, a, j):
        x, y, c, chips = _place()
        px, py = chips[j]
        return pltpu.make_async_remote_copy(
            src_ref=ins[a].at[2 * px + py], dst_ref=outs[a].at[j], send_sem=sems[0].at[a * 3 + j],
            recv_sem=sems[1].at[a * 3 + j], device_id=(px, py, c), device_id_type=MESH)

    def start(ins, outs, sems):
        for a in range(n):
            for j in range(3):
                copy(ins, outs, sems, a, j).start()

    def finish(ins, outs, sems):
        for a in range(n):
            for j in range(3):
                copy(ins, outs, sems, a, j).wait()

    return _Rider(parts, [jax.ShapeDtypeStruct((3, *p.shape[1:]), p.dtype) for p in parts], [],
                  [pltpu.SemaphoreType.DMA((3 * n,))] * 2, start, finish)


def _ride_gather(placed=(), landed=(), forward=True):
    placed, landed = list(placed), list(landed)
    n_ici = len(placed)
    fwd = (list(range(n_ici)) if forward else []) + list(range(n_ici, n_ici + len(landed)))

    def ici(ins, outs, sems, a, j, src_chip):
        x, y, c, chips = _place()
        h = ins[a].shape[1] // 2
        rows = pl.ds(pl.multiple_of(c * h, 8), h)
        return pltpu.make_async_remote_copy(
            src_ref=ins[a].at[2 * x + y, rows], dst_ref=outs[a].at[src_chip, rows], send_sem=sems[0].at[a * 3 + j],
            recv_sem=sems[1].at[a * 3 + j], device_id=(*chips[j], c), device_id_type=MESH)

    def d2d(ins, outs, sems, a, j, which):
        x, y, c, chips = _place()
        px, py = chips[j]
        h = ins[a].shape[1] // 2
        rows = pl.ds(pl.multiple_of(which * h, 8), h)
        return pltpu.make_async_remote_copy(
            src_ref=ins[a].at[2 * px + py, rows], dst_ref=outs[a].at[2 * px + py, rows], send_sem=sems[2].at[a * 3 + j],
            recv_sem=sems[3].at[a * 3 + j], device_id=(x, y, 1 - c), device_id_type=MESH)

    def start(ins, outs, sems):
        x, y, c, _ = _place()
        for a in range(n_ici):
            for j in range(3):
                ici(ins, outs, sems, a, j, 2 * x + y).start()
        for a in range(n_ici, n_ici + len(landed)):
            for j in range(3):
                d2d(ins, outs, sems, a, j, c).start()

    def land(ins, outs, sems, then_forward):
        _, _, c, chips = _place()
        for a in range(n_ici):
            for j, (px, py) in enumerate(chips):
                ici(ins, outs, sems, a, j, 2 * px + py).wait_recv()
                if then_forward:
                    d2d(ins, outs, sems, a, j, c).start()

    def middle(ins, outs, sems):
        land(ins, outs, sems, True)

    def finish(ins, outs, sems):
        x, y, c, _ = _place()
        if not forward:
            land(ins, outs, sems, False)
        for a in fwd:
            for j in range(3):
                d2d(ins, outs, sems, a, j, 1 - c).wait_recv()
        for a in range(n_ici):
            for j in range(3):
                ici(ins, outs, sems, a, j, 2 * x + y).wait_send()
        for a in fwd:
            for j in range(3):
                d2d(ins, outs, sems, a, j, c).wait_send()

    arrays = placed + landed
    n = len(arrays)
    return _Rider(arrays, [jax.ShapeDtypeStruct(p.shape, p.dtype) for p in arrays], [(a, a) for a in range(n)],
                  [pltpu.SemaphoreType.DMA((3 * n,))] * 4, start, finish, middle if (n_ici and forward) else None)


def _sum_chips(r, name):
    nk, rows, cols = r.shape
    tr = _tile(rows, max(8, (1 << 19) // cols), 8)

    def body(r_r, o_r):
        acc = r_r[0].astype(F32)
        for k in range(1, nk):
            acc = acc + r_r[k].astype(F32)
        o_r[...] = acc

    return pl.pallas_call(
        body, name=name, grid=(rows // tr,),
        in_specs=[pl.BlockSpec((nk, tr, cols), lambda i: (0, i, 0))],
        out_specs=pl.BlockSpec((tr, cols), lambda i: (i, 0)),
        out_shape=jax.ShapeDtypeStruct((rows, cols), F32),
        compiler_params=_params("parallel"),
    )(r)


def _sum_chips_into(own, arrived, dst, layer, n_layers, place_arr, name):
    _, h, cols = own.shape
    tr = _tile(h, max(8, (1 << 19) // cols), 8)
    nblk = h // tr

    def body(p_ref, own_r, arr_r, *rest):
        o_r = rest[-1]
        acc = own_r[...].astype(F32)
        for j in range(3):
            acc = acc + arr_r[j].astype(F32)
        o_r[...] = acc

    in_specs = [pl.BlockSpec((None, tr, cols), lambda i, p_ref: (p_ref[0], i, 0)),
                pl.BlockSpec((3, tr, cols), lambda i, p_ref: (0, i, 0))]
    args = [place_arr, own, arrived]
    aliases = {}
    if dst is not None:
        in_specs.append(ANY)
        args.append(dst)
        aliases = {3: 0}
    return pl.pallas_call(
        body, name=name,
        grid_spec=pltpu.PrefetchScalarGridSpec(
            num_scalar_prefetch=1, grid=(nblk,), in_specs=in_specs,
            out_specs=pl.BlockSpec((None, tr, cols), lambda i, p_ref: (layer, p_ref[1] * nblk + i, 0))),
        out_shape=jax.ShapeDtypeStruct((n_layers, 2 * h, cols), F32),
        input_output_aliases=aliases,
        compiler_params=_params("parallel"),
    )(*args)


def _share_halves(shards, name):
    n = len(shards)
    n_layers = shards[0].shape[0]

    def body(*refs):
        ins, outs = refs[:n], refs[n:2 * n]
        send, recv = refs[2 * n:]
        x, y, c, _ = _place()
        sib = (x, y, 1 - c)

        def copy(a, l, which):
            h = ins[a].shape[1] // 2
            rows = pl.ds(pl.multiple_of(which * h, 8), h)
            return pltpu.make_async_remote_copy(
                src_ref=ins[a].at[l, rows], dst_ref=outs[a].at[l, rows], send_sem=send.at[a * n_layers + l],
                recv_sem=recv.at[a * n_layers + l], device_id=sib, device_id_type=MESH)

        for a in range(n):
            for l in range(n_layers):
                copy(a, l, c).start()
        for a in range(n):
            for l in range(n_layers):
                copy(a, l, 1 - c).wait_recv()
        for a in range(n):
            for l in range(n_layers):
                copy(a, l, c).wait_send()

    return pl.pallas_call(
        body, name=name, in_specs=[ANY] * n, out_specs=[ANY] * n,
        out_shape=[jax.ShapeDtypeStruct(a.shape, a.dtype) for a in shards],
        scratch_shapes=[pltpu.SemaphoreType.DMA((n * n_layers,))] * 2,
        input_output_aliases={a: a for a in range(n)},
    )(*shards)


def _adamw(w, g, m, v, name):
    rows, cols = w.shape
    tr = _tile(rows, max(8, (1 << 18) // cols), 8)
    c1 = 1.0 - ADAM_B1 ** ADAM_STEP
    c2 = 1.0 - ADAM_B2 ** ADAM_STEP

    def body(w_r, g_r, m_r, v_r, d_o, m_o, v_o):
        gv = g_r[...]
        mn = ADAM_B1 * m_r[...] + (1.0 - ADAM_B1) * gv
        vn = ADAM_B2 * v_r[...] + (1.0 - ADAM_B2) * (gv * gv)
        m_o[...] = mn
        v_o[...] = vn
        d_o[...] = -ADAM_LR * ((mn / c1) / (jnp.sqrt(vn / c2) + ADAM_EPS) + ADAM_WD * w_r[...])

    spec = pl.BlockSpec((tr, cols), lambda i: (i, 0))
    return pl.pallas_call(
        body, name=name, grid=(rows // tr,), in_specs=[spec] * 4, out_specs=[spec] * 3,
        out_shape=[jax.ShapeDtypeStruct((rows, cols), F32)] * 3,
        compiler_params=_params("parallel"),
    )(w, g, m, v)


BIG = ("w_in", "w_out_a", "w_out_b", "w_pool", "w_o", "w_mlp1", "w_mlp2")
SMALL = ("g_mix", "b_in", "conv_b_bias", "ln_b_g", "ln_b_b", "b_out_b", "pool_scale", "g_mlp")
CONVS = ("conv_a", "conv_b")
WEIGHTS = ("g_mix", "w_in", "b_in", "conv_a", "w_out_a", "conv_b", "conv_b_bias", "ln_b_g", "ln_b_b", "w_out_b", "b_out_b",
           "w_pool", "pool_scale", "w_o", "g_mlp", "w_mlp1", "w_mlp2", "g_final")


def _as2d(a):
    return a.reshape(-1, a.shape[-1])


def _pad_rows(a, rows):
    return jnp.pad(a, ((0, rows - a.shape[0]), (0, 0)))


def _full_weights(big_g, conv_g, rep, l, d):
    gc = d // N_GROUPS
    w = {k: v[l] for k, v in rep.items()}
    w["b_in"] = rep["b_in"][l].reshape(1, -1)
    win3 = big_g["w_in"]
    w["win3"] = win3
    w["wint"] = win3.transpose(0, 2, 1).reshape(-1, d)
    for src, dst in (("w_out_a", "woa"), ("w_out_b", "wob"), ("w_o", "wo")):
        full = big_g[src].reshape(d, d)
        w[dst] = full
        w[dst + "t"] = full.T
    wp = big_g["w_pool"].reshape(N_CHIPS, N_GROUPS, gc // N_CHIPS, gc).transpose(1, 0, 2, 3).reshape(N_GROUPS, gc, gc)
    w["wp"] = wp
    w["wpt"] = wp.transpose(0, 2, 1)
    w1_3 = big_g["w_mlp1"]
    w["w1_3"] = w1_3
    w["w1t"] = w1_3.transpose(0, 2, 1).reshape(-1, d)
    w2 = big_g["w_mlp2"].reshape(-1, d)
    w["w2"] = w2
    w["w2t"] = w2.T
    ca, cb = conv_g
    w["conv_a"] = ca
    w["conv_b"] = cb
    return w


def kernel(x, g_mix, w_in, b_in, conv_a, w_out_a, conv_b, conv_b_bias, ln_b_g, ln_b_b, w_out_b, b_out_b, w_pool, pool_scale, w_o, g_mlp, w_mlp1, w_mlp2, g_final, loss_target, m_g_mix, m_w_in, m_b_in, m_conv_a, m_w_out_a, m_conv_b, m_conv_b_bias, m_ln_b_g, m_ln_b_b, m_w_out_b, m_b_out_b, m_w_pool, m_pool_scale, m_w_o, m_g_mlp, m_w_mlp1, m_w_mlp2, m_g_final, v_g_mix, v_w_in, v_b_in, v_conv_a, v_w_out_a, v_conv_b, v_conv_b_bias, v_ln_b_g, v_ln_b_b, v_w_out_b, v_b_out_b, v_w_pool, v_pool_scale, v_w_o, v_g_mlp, v_w_mlp1, v_w_mlp2, v_g_final):
    given = dict(g_mix=g_mix, w_in=w_in, b_in=b_in, conv_a=conv_a, w_out_a=w_out_a, conv_b=conv_b, conv_b_bias=conv_b_bias, ln_b_g=ln_b_g, ln_b_b=ln_b_b, w_out_b=w_out_b, b_out_b=b_out_b, w_pool=w_pool, pool_scale=pool_scale, w_o=w_o, g_mlp=g_mlp, w_mlp1=w_mlp1, w_mlp2=w_mlp2, g_final=g_final)
    mom = dict(g_mix=m_g_mix, w_in=m_w_in, b_in=m_b_in, conv_a=m_conv_a, w_out_a=m_w_out_a, conv_b=m_conv_b, conv_b_bias=m_conv_b_bias, ln_b_g=m_ln_b_g, ln_b_b=m_ln_b_b, w_out_b=m_w_out_b, b_out_b=m_b_out_b, w_pool=m_w_pool, pool_scale=m_pool_scale, w_o=m_w_o, g_mlp=m_g_mlp, w_mlp1=m_w_mlp1, w_mlp2=m_w_mlp2, g_final=m_g_final)
    var = dict(g_mix=v_g_mix, w_in=v_w_in, b_in=v_b_in, conv_a=v_conv_a, w_out_a=v_w_out_a, conv_b=v_conv_b, conv_b_bias=v_conv_b_bias, ln_b_g=v_ln_b_g, ln_b_b=v_ln_b_b, w_out_b=v_w_out_b, b_out_b=v_b_out_b, w_pool=v_w_pool, pool_scale=v_pool_scale, w_o=v_w_o, g_mlp=v_g_mlp, w_mlp1=v_w_mlp1, w_mlp2=v_w_mlp2, g_final=v_g_final)
    n_layers = w_in.shape[0]
    s, d = x.shape[1], x.shape[2]
    dq = d // N_CHIPS
    x_idx, y_idx, c_idx = lax.axis_index("x"), lax.axis_index("y"), lax.axis_index("c")
    chip = 2 * x_idx + y_idx
    c_arr = c_idx.astype(jnp.int32).reshape(1)

    conv_rows = n_layers * (K_A + K_B)
    conv_pad = -(-conv_rows // 16) * 16
    conv_pack = _pad_rows(jnp.concatenate([conv_a[l] for l in range(n_layers)] + [conv_b[l] for l in range(n_layers)], axis=0),
                          conv_pad)
    chip_arr = chip.astype(jnp.int32).reshape(1)
    place_arr = jnp.stack([chip, c_idx]).astype(jnp.int32)

    def placed(k, l):
        return _place_shard(_as2d(given[k][l]), chip_arr, BF16, f"place_{k}_{l}")

    rest = BIG[1:]
    win_g, conv_all = _comm_call(_ride_gather([placed("w_in", 0), _place_shard(conv_pack, chip_arr, F32, "place_convs")]),
                                 "gather_w_in_0")
    conv_full = conv_all.transpose(1, 0, 2).reshape(conv_pad, d)
    conv_g = [(conv_full[l * K_A:(l + 1) * K_A], conv_full[n_layers * K_A + l * K_B:n_layers * K_A + (l + 1) * K_B])
              for l in range(n_layers)]
    rep = {k: given[k] for k in SMALL}

    xl = x[0]
    ws, saved = [], []
    rest_g = None
    for l in range(n_layers):
        more = l + 1 < n_layers

        def weights(carried, l=l, win_g=win_g, rest_g=rest_g):
            return _full_weights(dict(zip(rest, carried if rest_g is None else rest_g), w_in=win_g), conv_g[l], rep, l, d)

        xl, sv, w, got = _layer_fwd(
            xl, l, given["g_mix"][l], win_g, given["b_in"][l].reshape(1, -1), weights,
            _ride_gather([placed(k, l) for k in rest]) if rest_g is None else None,
            _ride_gather([placed("w_in", l + 1)], forward=False) if more else None,
            (lambda landed, l=l: _ride_gather([placed(k, l + 1) for k in rest], landed)) if more else None)
        if more:
            rest_g, win_g = got[:len(rest)], got[len(rest)]
        ws.append(w)
        saved.append(sv)
    dx, loss, dgf = _loss_head(xl, loss_target[0], g_final, "loss_head")
    loss = lax.psum(loss[0, 0], ("x", "y", "c"))

    parts, arrived, small = {}, {}, [None] * n_layers
    pending = []
    for l in reversed(range(n_layers)):
        w, sv = ws[l], saved[l]
        first = [q for q in pending if q[1] not in ("w_mlp1", "w_mlp2")]
        second = [q for q in pending if q[1] in ("w_mlp1", "w_mlp2")]
        dx1, dproj, early, sm, got, got2, swapped = _layer_bwd_early(
            dx, w, sv, l, _ride_exchange([parts[q] for q in first]) if first else None,
            _ride_exchange([parts[q] for q in second]) if second else None, swap_early=(l == 0))
        arrived.update(zip(first + second, got + got2))
        if l > 0:
            order = ("w_in",) + EARLY
            dx, dwin, dgmix, _, swapped = _layer_bwd_late(
                dx1, dproj, w, sv, l, None, lambda dwin, early=early: _ride_swap([dwin] + [early[k] for k in EARLY]))
            full = dict(early, w_in=dwin)
            for k, r in zip(order, swapped):
                parts[(l, k)] = _add_halves(full[k], r, c_arr, f"grad_add_halves_{k}_{l}")
            pending = [(l, k) for k in order]
        else:
            for k, r in zip(EARLY, swapped):
                parts[(l, k)] = _add_halves(early[k], r, c_arr, f"grad_add_halves_{k}_{l}")

            def w_in_rider(dwin, l=l):
                (r,) = _comm_call(_ride_swap([dwin]), f"grad_swap_w_in_{l}")
                parts[(l, "w_in")] = _add_halves(dwin, r, c_arr, f"grad_add_halves_w_in_{l}")
                return _ride_exchange([parts[(l, "w_in")]])

            dx, dwin, dgmix, got_early, got_w_in = _layer_bwd_late(
                dx1, dproj, w, sv, l, _ride_exchange([parts[(l, k)] for k in EARLY]), w_in_rider)
            arrived.update(zip([(l, k) for k in EARLY], got_early))
            arrived[(l, "w_in")] = got_w_in[0]
            pending = []
        sm["g_mix"] = dgmix
        small[l] = sm
    reduced = []
    for k in BIG:
        dst = None
        for l in range(n_layers):
            dst = _sum_chips_into(parts[(l, k)], arrived[(l, k)], dst, l, n_layers, place_arr, f"grad_sum_chips_{k}_{l}")
        reduced.append(dst)
    big_grad = dict(zip(BIG, _share_halves(reduced, "grad_share_halves")))

    small_rows = []
    for k in SMALL + CONVS:
        for l in range(n_layers):
            small_rows.append(small[l][k])
    small_rows.append(dgf)
    pack = jnp.concatenate(small_rows, axis=0)
    n_small = pack.shape[0]
    pack = _pad_rows(pack, -(-n_small // 8) * 8)
    small_sum = _sum_chips(_gather_rows(pack, "gather_small_grads"), "sum_small_grads")

    out = {}
    for k in BIG:
        shape = given[k].shape
        g2 = big_grad[k].reshape(-1, shape[-1])
        dlt, nm, nv = _adamw(given[k].reshape(g2.shape), g2, mom[k].reshape(g2.shape), var[k].reshape(g2.shape), f"adamw_{k}")
        out[k] = tuple(a.reshape(shape) for a in (g2, dlt, nm, nv))

    def rows_of(k):
        return {"b_in": 9, "conv_a": K_A, "conv_b": K_B}.get(k, 1)

    def pack_rep(src):
        rows = [src[k].reshape(n_layers * rows_of(k), d) for k in SMALL] + [src["g_final"].reshape(1, d)]
        a = jnp.concatenate(rows, axis=0)
        return _pad_rows(a, -(-a.shape[0] // 8) * 8)

    def pack_conv(src):
        a = jnp.concatenate([src[k].reshape(n_layers * rows_of(k), dq) for k in CONVS], axis=0)
        return _pad_rows(a, -(-a.shape[0] // 8) * 8)

    offs = {}
    r = 0
    for k in SMALL + CONVS:
        offs[k] = r
        r += n_layers * rows_of(k)
    offs["g_final"] = r
    n_rep = sum(n_layers * rows_of(k) for k in SMALL)
    g_rep = jnp.concatenate([small_sum[:n_rep], small_sum[offs["g_final"]:offs["g_final"] + 1]], axis=0)
    g_rep = _pad_rows(g_rep, -(-g_rep.shape[0] // 8) * 8)
    g_conv_full = small_sum[offs["conv_a"]:offs["g_final"]]
    g_conv = lax.dynamic_slice_in_dim(g_conv_full, chip * dq, dq, axis=1)
    g_conv = _pad_rows(g_conv, -(-g_conv.shape[0] // 8) * 8)
    rep_res = (g_rep,) + tuple(_adamw(pack_rep(given), g_rep, pack_rep(mom), pack_rep(var), "adamw_small"))
    conv_res = (g_conv,) + tuple(_adamw(pack_conv(given), g_conv, pack_conv(mom), pack_conv(var), "adamw_conv"))
    r = 0
    for k in SMALL:
        nr = n_layers * rows_of(k)
        out[k] = tuple(a[r:r + nr].reshape(given[k].shape) for a in rep_res)
        r += nr
    out["g_final"] = tuple(a[r].reshape(given["g_final"].shape) for a in rep_res)
    r = 0
    for k in CONVS:
        nr = n_layers * rows_of(k)
        out[k] = tuple(a[r:r + nr].reshape(given[k].shape) for a in conv_res)
        r += nr

    res = [loss, dx.reshape(x.shape)]
    for i in range(4):
        res += [out[k][i] for k in WEIGHTS]
    return tuple(res)
```

```python
import functools

import jax
import jax.numpy as jnp
from jax import lax
from jax.experimental import pallas as pl
from jax.experimental.pallas import tpu as pltpu

F32 = jnp.float32
BF16 = jnp.bfloat16
EPS = 1e-6
HALO = 32
SUB = 32
K_A = 3
K_B = 31
POOL_WINDOWS = (2, 4, 8, 16)
N_GROUPS = len(POOL_WINDOWS)
N_CHIPS = 4
N_DEV = 8
ADAM_LR = 0.001
ADAM_B1 = 0.9
ADAM_B2 = 0.999
ADAM_EPS = 1e-08
ADAM_WD = 0.01
ADAM_STEP = 10
VMEM_LIMIT = 56 * 1024 * 1024
MESH = pl.DeviceIdType.MESH
ANY = pl.BlockSpec(memory_space=pl.ANY)


def _params(*sem):
    return pltpu.CompilerParams(dimension_semantics=sem, vmem_limit_bytes=VMEM_LIMIT)


def _tile(n, pref, mult=8):
    if n <= pref:
        return n
    t = (pref // mult) * mult
    while t > mult and n % t:
        t -= mult
    assert n % t == 0, (n, pref, mult)
    return t


def _dot(a, b):
    return jnp.dot(a, b, preferred_element_type=F32)


def _sigmoid(x):
    return 1.0 / (1.0 + jnp.exp(-x))


def _f32(ref):
    return ref[...].astype(F32)


def _fold8(v):
    r, d = v.shape
    return v.reshape(r // 8, 8, d).sum(axis=0)


def _rms_bwd(dh, xv, g):
    r = lax.rsqrt(jnp.mean(xv * xv, axis=-1, keepdims=True) + EPS)
    xn = xv * r
    dxn = dh * g
    dx = r * (dxn - xn * jnp.mean(dxn * xn, axis=-1, keepdims=True))
    return dx, _fold8(dh * xn)


class _Rider:
    def __init__(self, args, out_shape, aliases, sems, start, finish, middle=None):
        self.args, self.out_shape, self.aliases, self.sems = list(args), list(out_shape), list(aliases), list(sems)
        self.start, self.finish, self.middle = start, finish, middle


def _hosted_call(body, *, name, grid, in_specs, out_specs, out_shape, scratch_shapes, args, rider=None):
    n_in, n_out, n_scr = len(in_specs), len(out_shape), len(scratch_shapes)
    params = pltpu.CompilerParams(dimension_semantics=("arbitrary",) * len(grid), vmem_limit_bytes=VMEM_LIMIT)
    if rider is None:
        res = pl.pallas_call(body, name=name, grid=grid, in_specs=in_specs, out_specs=out_specs, out_shape=out_shape,
                             scratch_shapes=scratch_shapes, compiler_params=params)(*args)
        return list(res), []
    k_in, k_out = len(rider.args), len(rider.out_shape)

    def hosted(*refs):
        ins, refs = refs[:n_in], refs[n_in:]
        r_in, refs = refs[:k_in], refs[k_in:]
        outs, refs = refs[:n_out], refs[n_out:]
        r_out, refs = refs[:k_out], refs[k_out:]
        scr, r_sems = refs[:n_scr], refs[n_scr:]
        first = functools.reduce(lambda u, v: u & v, [pl.program_id(a) == 0 for a in range(len(grid))])
        last = functools.reduce(lambda u, v: u & v, [pl.program_id(a) == grid[a] - 1 for a in range(len(grid))])

        @pl.when(first)
        def _():
            rider.start(r_in, r_out, r_sems)

        if rider.middle is not None:
            step = functools.reduce(lambda u, a: u * grid[a] + pl.program_id(a), range(len(grid)), 0)

            @pl.when(step == (3 * functools.reduce(lambda u, v: u * v, grid)) // 4)
            def _():
                rider.middle(r_in, r_out, r_sems)

        body(*ins, *outs, *scr)

        @pl.when(last)
        def _():
            rider.finish(r_in, r_out, r_sems)

    res = pl.pallas_call(
        hosted, name=name, grid=grid, in_specs=list(in_specs) + [ANY] * k_in, out_specs=list(out_specs) + [ANY] * k_out,
        out_shape=list(out_shape) + rider.out_shape, scratch_shapes=list(scratch_shapes) + rider.sems,
        input_output_aliases={n_in + i: n_out + o for i, o in rider.aliases}, compiler_params=params,
    )(*args, *rider.args)
    return list(res[:n_out]), list(res[n_out:])


def _comm_call(rider, name):
    def body(*refs):
        k_in, k_out = len(rider.args), len(rider.out_shape)
        r_in, r_out, r_sems = refs[:k_in], refs[k_in:k_in + k_out], refs[k_in + k_out:]
        rider.start(r_in, r_out, r_sems)
        if rider.middle is not None:
            rider.middle(r_in, r_out, r_sems)
        rider.finish(r_in, r_out, r_sems)

    return list(pl.pallas_call(
        body, name=name, in_specs=[ANY] * len(rider.args), out_specs=[ANY] * len(rider.out_shape),
        out_shape=rider.out_shape, scratch_shapes=rider.sems, input_output_aliases=dict(rider.aliases),
    )(*rider.args))


def _rms_fwd(x, g, name):
    s, d = x.shape
    tm = _tile(s, 512)

    def body(x_ref, g_ref, h_ref):
        xv = x_ref[...]
        r = lax.rsqrt(jnp.mean(xv * xv, axis=-1, keepdims=True) + EPS)
        h_ref[...] = (xv * r * g_ref[...]).astype(BF16)

    return pl.pallas_call(
        body, name=name, grid=(s // tm,),
        in_specs=[pl.BlockSpec((tm, d), lambda i: (i, 0)), pl.BlockSpec((1, d), lambda i: (0, 0))],
        out_specs=pl.BlockSpec((tm, d), lambda i: (i, 0)),
        out_shape=jax.ShapeDtypeStruct((s, d), BF16),
        compiler_params=_params("parallel"),
    )(x, g.reshape(1, d))


def _proj(h, w3, b, name, rider=None):
    s, k = h.shape
    nb, _, nw = w3.shape
    tm = _tile(s, 1024)
    tn = nw
    per = nw // tn

    def body(h_ref, w_ref, b_ref, o_ref):
        o_ref[...] = (_dot(h_ref[...], w_ref[...]) + b_ref[...]).astype(BF16)

    (proj,), carried = _hosted_call(
        body, name=name, grid=(s // tm, nb * per),
        in_specs=[
            pl.BlockSpec((tm, k), lambda i, j: (i, 0)),
            pl.BlockSpec((None, k, tn), lambda i, j: (j // per, 0, j % per)),
            pl.BlockSpec((1, tn), lambda i, j: (0, j)),
        ],
        out_specs=[pl.BlockSpec((tm, tn), lambda i, j: (i, j))],
        out_shape=[jax.ShapeDtypeStruct((s, nb * nw), BF16)],
        scratch_shapes=[], args=(h, w3, b), rider=rider)
    return proj, carried


def _shift_copies(src_e, dst_s):
    n_rows = dst_s.shape[1]
    for b in range(1, 8):
        dst_s[b - 1, :, :] = src_e[pl.ds(b, n_rows), :]


def _window(src_e, dst_s, off, rows):
    b = off % 8
    if b == 0:
        return src_e[pl.ds(off, rows), :]
    return dst_s[b - 1, pl.ds(off - b, rows), :]


def _prev_halo(tm, d, col):
    r = tm // HALO
    return pl.BlockSpec((HALO, d), lambda i: (jnp.maximum(i * r - 1, 0), col))


def _next_halo(tm, d, col, n_halo_blocks):
    r = tm // HALO
    return pl.BlockSpec((HALO, d), lambda i: (jnp.minimum((i + 1) * r, n_halo_blocks - 1), col))


def _mix_fwd(proj, conv_a, conv_b, cb, lg, lb, name):
    s, d9 = proj.shape
    d = d9 // 9
    gc = d // N_GROUPS
    tm = _tile(s, 256, HALO)

    def tile(col):
        return pl.BlockSpec((tm, d), lambda i: (i, col))

    def vec(rows):
        return pl.BlockSpec((rows, d), lambda i: (0, 0))

    def body(ab, ac, ax, bv, bg, ci, ac_h, ax_h, bv_h, bg_h, ci_h, wa, wb, cb_r, lg_r, lb_r,
             pa_o, s_o, p_o, v_o, ua_e, ub_e, ci_e, ub_s):
        i = pl.program_id(0)
        keep = (i > 0).astype(F32)
        ua_e[0:HALO, :] = _f32(ac_h) * _f32(ax_h) * keep
        ua_e[HALO:HALO + tm, :] = _f32(ac) * _f32(ax)
        ub_e[0:HALO, :] = _f32(bv_h) * _sigmoid(_f32(bg_h)) * keep
        ub_e[HALO:HALO + tm, :] = _f32(bv) * _sigmoid(_f32(bg))
        ci_e[0:HALO, :] = _f32(ci_h) * keep
        ci_e[HALO:HALO + tm, :] = _f32(ci)
        _shift_copies(ub_e, ub_s)
        for r0 in range(0, tm, SUB):
            rows = slice(r0, r0 + SUB)
            z = None
            for k in range(K_A):
                t = wa[k:k + 1, :] * ua_e[pl.ds(HALO - (K_A - 1) + k + r0, SUB), :]
                z = t if z is None else z + t
            pa_o[rows, :] = (ab[rows, :].astype(F32) * z).astype(BF16)
            v = None
            for k in range(K_B):
                t = wb[k:k + 1, :] * _window(ub_e, ub_s, HALO - (K_B - 1) + k + r0, SUB)
                v = t if v is None else v + t
            v = v + cb_r[...]
            v_o[rows, :] = v
            mu = jnp.mean(v, axis=-1, keepdims=True)
            vc = v - mu
            rstd = lax.rsqrt(jnp.mean(vc * vc, axis=-1, keepdims=True) + EPS)
            ln = vc * rstd * lg_r[...] + lb_r[...]
            s_o[rows, :] = (ln * _sigmoid(ln)).astype(BF16)
            t_idx = i * tm + r0 + lax.broadcasted_iota(jnp.int32, (SUB, gc), 0)
            for g, w in enumerate(POOL_WINDOWS):
                cols = slice(g * gc, (g + 1) * gc)
                cur = ci_e[pl.ds(HALO + r0, SUB), cols]
                acc = cur
                for j in range(1, w):
                    acc = acc + ci_e[pl.ds(HALO + r0 - j, SUB), cols]
                cnt = jnp.minimum(t_idx + 1, w).astype(F32)
                p_o[rows, cols] = (acc / cnt - cur).astype(BF16)

    return pl.pallas_call(
        body, name=name, grid=(s // tm,),
        in_specs=[tile(0), tile(1), tile(2), tile(3), tile(4), tile(5),
                  _prev_halo(tm, d, 1), _prev_halo(tm, d, 2), _prev_halo(tm, d, 3), _prev_halo(tm, d, 4),
                  _prev_halo(tm, d, 5),
                  vec(K_A), vec(K_B), vec(1), vec(1), vec(1)],
        out_specs=[pl.BlockSpec((tm, d), lambda i: (i, 0))] * 4,
        out_shape=[jax.ShapeDtypeStruct((s, d), BF16)] * 3 + [jax.ShapeDtypeStruct((s, d), F32)],
        scratch_shapes=[pltpu.VMEM((HALO + tm, d), F32)] * 3 + [pltpu.VMEM((7, tm + HALO - 8, d), F32)],
        compiler_params=_params("parallel"),
    )(proj, proj, proj, proj, proj, proj, proj, proj, proj, proj, proj,
      conv_a, conv_b, cb.reshape(1, d), lg.reshape(1, d), lb.reshape(1, d))


def _branch_out(pa, sb, p, proj, x0, woa, wob, wp, wo, bob, ps, gm, name, rider=None):
    s, d = pa.shape
    gc = d // N_GROUPS
    tm = _tile(s, 512)

    def tile(col=0):
        return pl.BlockSpec((tm, d), lambda i: (i, col))

    def const(shape):
        return pl.BlockSpec(shape, lambda i: (0,) * len(shape))

    def body(pa_r, s_r, p_r, g0, g1, g2, x0_r, woa_r, wob_r, wp_r, wo_r, bob_r, ps_r, gm_r,
             ya_o, yb_o, yc_o, mg_o, x1_o, h2_o, yc_s):
        ya = _dot(pa_r[...], woa_r[...])
        yb = _dot(s_r[...], wob_r[...]) + bob_r[...]
        for g in range(N_GROUPS):
            cols = slice(g * gc, (g + 1) * gc)
            yc_s[:, cols] = _dot(p_r[:, cols], wp_r[g])
        yc = yc_s[...]
        ya_o[...] = ya.astype(BF16)
        yb_o[...] = yb.astype(BF16)
        yc_o[...] = yc.astype(BF16)
        m = _sigmoid(_f32(g0)) * ya + _sigmoid(_f32(g1)) * yb + _sigmoid(_f32(g2)) * (yc * ps_r[...])
        mb = m.astype(BF16)
        mg_o[...] = mb
        x1 = x0_r[...] + _dot(mb, wo_r[...])
        x1_o[...] = x1
        r = lax.rsqrt(jnp.mean(x1 * x1, axis=-1, keepdims=True) + EPS)
        h2_o[...] = (x1 * r * gm_r[...]).astype(BF16)

    return _hosted_call(
        body, name=name, grid=(s // tm,),
        in_specs=[tile(), tile(), tile(), tile(6), tile(7), tile(8), tile(),
                  const((d, d)), const((d, d)), const((N_GROUPS, gc, gc)), const((d, d)),
                  const((1, d)), const((1, d)), const((1, d))],
        out_specs=[tile()] * 6,
        out_shape=[jax.ShapeDtypeStruct((s, d), BF16)] * 4 + [jax.ShapeDtypeStruct((s, d), F32),
                                                               jax.ShapeDtypeStruct((s, d), BF16)],
        scratch_shapes=[pltpu.VMEM((tm, d), F32)],
        args=(pa, sb, p, proj, proj, proj, x0, woa, wob, wp, wo, bob.reshape(1, d), ps.reshape(1, d), gm.reshape(1, d)),
        rider=rider)


def _mlp_fwd(h2, x1, w1_3, w2, name, rider=None):
    s, d = h2.shape
    nf, _, tf = w1_3.shape
    tm = _tile(s, 1024)

    def body(h_r, x1_r, w1_r, w2_r, f_o, a_o, x2_o, acc):
        j = pl.program_id(1)

        @pl.when(j == 0)
        def _():
            acc[...] = jnp.zeros_like(acc)

        f = _dot(h_r[...], w1_r[...])
        f_o[...] = f.astype(BF16)
        rl = jnp.maximum(f, 0.0)
        a = (rl * rl).astype(BF16)
        a_o[...] = a
        acc[...] += _dot(a, w2_r[...])

        @pl.when(j == nf - 1)
        def _():
            x2_o[...] = x1_r[...] + acc[...]

    return _hosted_call(
        body, name=name, grid=(s // tm, nf),
        in_specs=[pl.BlockSpec((tm, d), lambda i, j: (i, 0)), pl.BlockSpec((tm, d), lambda i, j: (i, 0)),
                  pl.BlockSpec((None, d, tf), lambda i, j: (j, 0, 0)), pl.BlockSpec((tf, d), lambda i, j: (j, 0))],
        out_specs=[pl.BlockSpec((tm, tf), lambda i, j: (i, j)), pl.BlockSpec((tm, tf), lambda i, j: (i, j)),
                   pl.BlockSpec((tm, d), lambda i, j: (i, 0))],
        out_shape=[jax.ShapeDtypeStruct((s, nf * tf), BF16), jax.ShapeDtypeStruct((s, nf * tf), BF16),
                   jax.ShapeDtypeStruct((s, d), F32)],
        scratch_shapes=[pltpu.VMEM((tm, d), F32)], args=(h2, x1, w1_3, w2), rider=rider)


def _loss_head(x, target, gf, name):
    s, d = x.shape
    tm = _tile(s, 512)
    n = s // tm

    def body(x_r, t_r, g_r, dx_o, loss_o, dg_o, lacc, gacc):
        i = pl.program_id(0)

        @pl.when(i == 0)
        def _():
            lacc[...] = jnp.zeros_like(lacc)
            gacc[...] = jnp.zeros_like(gacc)

        xv = x_r[...]
        r = lax.rsqrt(jnp.mean(xv * xv, axis=-1, keepdims=True) + EPS)
        xn = xv * r
        e = xn * g_r[...] - t_r[...]
        lacc[...] += _fold8(e * e)
        dy = e * (1.0 / d)
        gacc[...] += _fold8(dy * xn)
        dxn = dy * g_r[...]
        dx_o[...] = r * (dxn - xn * jnp.mean(dxn * xn, axis=-1, keepdims=True))

        @pl.when(i == n - 1)
        def _():
            loss_o[...] = jnp.sum(lacc[...]).reshape(1, 1) * (0.5 / d)
            dg_o[...] = jnp.sum(gacc[...], axis=0, keepdims=True)

    return pl.pallas_call(
        body, name=name, grid=(n,),
        in_specs=[pl.BlockSpec((tm, d), lambda i: (i, 0)), pl.BlockSpec((tm, d), lambda i: (i, 0)),
                  pl.BlockSpec((1, d), lambda i: (0, 0))],
        out_specs=[pl.BlockSpec((tm, d), lambda i: (i, 0)), pl.BlockSpec((1, 1), lambda i: (0, 0)),
                   pl.BlockSpec((1, d), lambda i: (0, 0))],
        out_shape=[jax.ShapeDtypeStruct((s, d), F32), jax.ShapeDtypeStruct((1, 1), F32),
                   jax.ShapeDtypeStruct((1, d), F32)],
        scratch_shapes=[pltpu.VMEM((8, d), F32), pltpu.VMEM((8, d), F32)],
        compiler_params=_params("arbitrary"),
    )(x, target, gf.reshape(1, d))


def _tn(a, b, nb, name, rider=None):
    s, m = a.shape
    n = b.shape[1]
    nw = n // nb
    tmm = _tile(m, 1024, 128)
    tn = nw if nw <= 1152 else nw // 2
    per = nw // tn
    ts = _tile(s, 2048)
    ns = s // ts

    def body(a_r, b_r, o_r, acc):
        k = pl.program_id(2)

        @pl.when(k == 0)
        def _():
            acc[...] = jnp.zeros_like(acc)

        acc[...] += lax.dot_general(a_r[...].astype(BF16), b_r[...].astype(BF16), (((0,), (0,)), ((), ())),
                                    preferred_element_type=F32)

        @pl.when(k == ns - 1)
        def _():
            o_r[...] = acc[...]

    (out,), carried = _hosted_call(
        body, name=name, grid=(m // tmm, nb * per, ns),
        in_specs=[pl.BlockSpec((ts, tmm), lambda i, j, k: (k, i)), pl.BlockSpec((ts, tn), lambda i, j, k: (k, j))],
        out_specs=[pl.BlockSpec((None, tmm, tn), lambda i, j, k: (j // per, i, j % per))],
        out_shape=[jax.ShapeDtypeStruct((nb, m, nw), F32)],
        scratch_shapes=[pltpu.VMEM((tmm, tn), F32)], args=(a, b), rider=rider)
    return (out, carried) if rider is not None else out


def _tn_groups(a, b, name):
    s, d = a.shape
    gc = d // N_GROUPS
    ts = _tile(s, 2048)
    ns = s // ts

    def body(a_r, b_r, o_r, acc):
        k = pl.program_id(1)

        @pl.when(k == 0)
        def _():
            acc[...] = jnp.zeros_like(acc)

        acc[...] += lax.dot_general(a_r[...], b_r[...], (((0,), (0,)), ((), ())), preferred_element_type=F32)

        @pl.when(k == ns - 1)
        def _():
            o_r[...] = acc[...]

    return pl.pallas_call(
        body, name=name, grid=(N_GROUPS, ns),
        in_specs=[pl.BlockSpec((ts, gc), lambda g, k: (k, g)), pl.BlockSpec((ts, gc), lambda g, k: (k, g))],
        out_specs=pl.BlockSpec((None, gc, gc), lambda g, k: (g, 0, 0)),
        out_shape=jax.ShapeDtypeStruct((N_GROUPS, gc, gc), F32),
        scratch_shapes=[pltpu.VMEM((gc, gc), F32)],
        compiler_params=_params("parallel", "arbitrary"),
    )(a, b)


def _mlp_bwd(dx2, f, x1, w2t, w1t, gm, name, rider=None):
    s, d = dx2.shape
    ff = f.shape[1]
    tf = _tile(ff, 1024, 128)
    nf = ff // tf
    tm = _tile(s, 1024)
    n = s // tm

    def body(dx2_r, f_r, x1_r, w2t_r, w1t_r, gm_r, df_o, dx1_o, dg_o, dxb, acc, gacc):
        i = pl.program_id(0)
        j = pl.program_id(1)

        @pl.when((i == 0) & (j == 0))
        def _():
            gacc[...] = jnp.zeros_like(gacc)

        @pl.when(j == 0)
        def _():
            dxb[...] = dx2_r[...].astype(BF16)
            acc[...] = jnp.zeros_like(acc)

        da = _dot(dxb[...], w2t_r[...])
        df = (da * (2.0 * jnp.maximum(f_r[...].astype(F32), 0.0))).astype(BF16)
        df_o[...] = df
        acc[...] += _dot(df, w1t_r[...])

        @pl.when(j == nf - 1)
        def _():
            dx, dg = _rms_bwd(acc[...], x1_r[...], gm_r[...])
            gacc[...] += dg
            dx1_o[...] = dx2_r[...] + dx

        @pl.when((i == n - 1) & (j == nf - 1))
        def _():
            dg_o[...] = jnp.sum(gacc[...], axis=0, keepdims=True)

    return _hosted_call(
        body, name=name, grid=(n, nf),
        in_specs=[pl.BlockSpec((tm, d), lambda i, j: (i, 0)), pl.BlockSpec((tm, tf), lambda i, j: (i, j)),
                  pl.BlockSpec((tm, d), lambda i, j: (i, 0)), pl.BlockSpec((d, tf), lambda i, j: (0, j)),
                  pl.BlockSpec((tf, d), lambda i, j: (j, 0)), pl.BlockSpec((1, d), lambda i, j: (0, 0))],
        out_specs=[pl.BlockSpec((tm, tf), lambda i, j: (i, j)), pl.BlockSpec((tm, d), lambda i, j: (i, 0)),
                   pl.BlockSpec((1, d), lambda i, j: (0, 0))],
        out_shape=[jax.ShapeDtypeStruct((s, ff), BF16), jax.ShapeDtypeStruct((s, d), F32),
                   jax.ShapeDtypeStruct((1, d), F32)],
        scratch_shapes=[pltpu.VMEM((tm, d), BF16), pltpu.VMEM((tm, d), F32), pltpu.VMEM((8, d), F32)],
        args=(dx2, f, x1, w2t, w1t, gm.reshape(1, d)), rider=rider)


def _branch_out_bwd(dx1, proj, ya, yb, yc, wot, woat, wobt, wpt, ps, name):
    s, d = dx1.shape
    gc = d // N_GROUPS
    tm = _tile(s, 512)
    n = s // tm

    def tile(col=0):
        return pl.BlockSpec((tm, d), lambda i: (i, col))

    def const(shape):
        return pl.BlockSpec(shape, lambda i: (0,) * len(shape))

    def body(dx1_r, g0, g1, g2, ya_r, yb_r, yc_r, wot_r, woat_r, wobt_r, wpt_r, ps_r,
             dpa_o, ds_o, dp_o, dpj_o, dya_o, dyb_o, dyc_o, dbob_o, dps_o, dbg_o, sacc, gacc):
        i = pl.program_id(0)

        @pl.when(i == 0)
        def _():
            sacc[...] = jnp.zeros_like(sacc)
            gacc[...] = jnp.zeros_like(gacc)

        dm = _dot(dx1_r[...].astype(BF16), wot_r[...])
        ycp = _f32(yc_r)
        ys = (_f32(ya_r), _f32(yb_r), ycp * ps_r[...])
        dys = []
        for b, g_r in enumerate((g0, g1, g2)):
            sg = _sigmoid(_f32(g_r))
            dgt = dm * ys[b] * sg * (1.0 - sg)
            dpj_o[:, b * d:(b + 1) * d] = dgt.astype(BF16)
            gacc[b] += _fold8(dgt)
            dys.append(dm * sg)
        dya, dyb, dyc = dys
        sacc[0] += _fold8(dyb)
        sacc[1] += _fold8(dyc * ycp)
        dyab = dya.astype(BF16)
        dybb = dyb.astype(BF16)
        dycb = (dyc * ps_r[...]).astype(BF16)
        dya_o[...] = dyab
        dyb_o[...] = dybb
        dyc_o[...] = dycb
        dpa_o[...] = _dot(dyab, woat_r[...]).astype(BF16)
        ds_o[...] = _dot(dybb, wobt_r[...]).astype(BF16)
        for g in range(N_GROUPS):
            cols = slice(g * gc, (g + 1) * gc)
            dp_o[:, cols] = _dot(dycb[:, cols], wpt_r[g]).astype(BF16)

        @pl.when(i == n - 1)
        def _():
            dbob_o[...] = jnp.sum(sacc[0], axis=0, keepdims=True)
            dps_o[...] = jnp.sum(sacc[1], axis=0, keepdims=True)
            for b in range(3):
                dbg_o[:, b * d:(b + 1) * d] = jnp.sum(gacc[b], axis=0, keepdims=True)

    return pl.pallas_call(
        body, name=name, grid=(n,),
        in_specs=[tile(), tile(6), tile(7), tile(8), tile(), tile(), tile(),
                  const((d, d)), const((d, d)), const((d, d)), const((N_GROUPS, gc, gc)), const((1, d))],
        out_specs=[tile(), tile(), tile(), pl.BlockSpec((tm, 3 * d), lambda i: (i, 2)), tile(), tile(), tile(),
                   const((1, d)), const((1, d)), const((1, 3 * d))],
        out_shape=[jax.ShapeDtypeStruct((s, d), BF16)] * 3 + [jax.ShapeDtypeStruct((s, 9 * d), BF16)]
        + [jax.ShapeDtypeStruct((s, d), BF16)] * 3
        + [jax.ShapeDtypeStruct((1, d), F32), jax.ShapeDtypeStruct((1, d), F32), jax.ShapeDtypeStruct((1, 3 * d), F32)],
        scratch_shapes=[pltpu.VMEM((2, 8, d), F32), pltpu.VMEM((3, 8, d), F32)],
        compiler_params=_params("arbitrary"),
    )(dx1, proj, proj, proj, ya, yb, yc, wot, woat, wobt, wpt, ps.reshape(1, d))


def _ln_silu_bwd(v, ds, lg, lb, name, rider=None):
    s, d = v.shape
    tm = _tile(s, 256)
    n = s // tm

    def tile():
        return pl.BlockSpec((tm, d), lambda i: (i, 0))

    def vec():
        return pl.BlockSpec((1, d), lambda i: (0, 0))

    def body(v_r, ds_r, lg_r, lb_r, dv_o, dlg_o, dlb_o, dcb_o, acc):
        i = pl.program_id(0)

        @pl.when(i == 0)
        def _():
            acc[...] = jnp.zeros_like(acc)

        for r0 in range(0, tm, SUB):
            rows = slice(r0, r0 + SUB)
            vv = v_r[rows, :]
            mu = jnp.mean(vv, axis=-1, keepdims=True)
            vc = vv - mu
            rstd = lax.rsqrt(jnp.mean(vc * vc, axis=-1, keepdims=True) + EPS)
            nrm = vc * rstd
            ln = nrm * lg_r[...] + lb_r[...]
            sg = _sigmoid(ln)
            dln = ds_r[rows, :].astype(F32) * (sg * (1.0 + ln * (1.0 - sg)))
            acc[0] += _fold8(dln * nrm)
            acc[1] += _fold8(dln)
            dn = dln * lg_r[...]
            dv = rstd * (dn - jnp.mean(dn, axis=-1, keepdims=True)
                         - nrm * jnp.mean(dn * nrm, axis=-1, keepdims=True))
            acc[2] += _fold8(dv)
            dv_o[rows, :] = dv

        @pl.when(i == n - 1)
        def _():
            dlg_o[...] = jnp.sum(acc[0], axis=0, keepdims=True)
            dlb_o[...] = jnp.sum(acc[1], axis=0, keepdims=True)
            dcb_o[...] = jnp.sum(acc[2], axis=0, keepdims=True)

    return _hosted_call(
        body, name=name, grid=(n,),
        in_specs=[tile(), tile(), vec(), vec()],
        out_specs=[tile(), vec(), vec(), vec()],
        out_shape=[jax.ShapeDtypeStruct((s, d), F32)] + [jax.ShapeDtypeStruct((1, d), F32)] * 3,
        scratch_shapes=[pltpu.VMEM((3, 8, d), F32)],
        args=(v, ds, lg.reshape(1, d), lb.reshape(1, d)), rider=rider)


def _mix_bwd(proj, dpa, dv, dp, dproj, conv_a, conv_b, name):
    s, d9 = proj.shape
    d = d9 // 9
    gc = d // N_GROUPS
    tm = _tile(s, 256, HALO)
    n = s // tm
    nh = s // HALO

    def tile(col=0):
        return pl.BlockSpec((tm, d), lambda i: (i, col))

    def vec(rows):
        return pl.BlockSpec((rows, d), lambda i: (0, 0))

    def body(ab, ac, ax, bv, bg, ac_h, ax_h, bv_h, bg_h, dpa_r, dv_r, dp_r, ab_n, dpa_n, dv_n, dp_n, wa, wb, _alias,
             dpj_o, dbin_o, dca_o, dcb_o, ua_e, ub_e, dz_e, dv_e, q_e, bacc, cacc_a, cacc_b, ub_s, dv_s):
        i = pl.program_id(0)

        @pl.when(i == 0)
        def _():
            bacc[...] = jnp.zeros_like(bacc)
            cacc_a[...] = jnp.zeros_like(cacc_a)
            cacc_b[...] = jnp.zeros_like(cacc_b)

        keep_p = (i > 0).astype(F32)
        keep_n = (i < n - 1).astype(F32)
        ua_e[0:HALO, :] = _f32(ac_h) * _f32(ax_h) * keep_p
        ua_e[HALO:HALO + tm, :] = _f32(ac) * _f32(ax)
        ub_e[0:HALO, :] = _f32(bv_h) * _sigmoid(_f32(bg_h)) * keep_p
        ub_e[HALO:HALO + tm, :] = _f32(bv) * _sigmoid(_f32(bg))
        dz_e[0:tm, :] = _f32(dpa_r) * _f32(ab)
        dz_e[tm:tm + HALO, :] = _f32(dpa_n) * _f32(ab_n) * keep_n
        dv_e[0:tm, :] = dv_r[...]
        dv_e[tm:tm + HALO, :] = dv_n[...] * keep_n
        _shift_copies(ub_e, ub_s)
        _shift_copies(dv_e, dv_s)
        for g, w in enumerate(POOL_WINDOWS):
            cols = slice(g * gc, (g + 1) * gc)
            t_idx = i * tm + lax.broadcasted_iota(jnp.int32, (tm + HALO, gc), 0)
            cnt = jnp.minimum(t_idx + 1, w).astype(F32)
            q_e[0:tm, cols] = dp_r[:, cols].astype(F32) / cnt[0:tm]
            q_e[tm:tm + HALO, cols] = dp_n[:, cols].astype(F32) * keep_n / cnt[tm:tm + HALO]
        for r0 in range(0, tm, SUB):
            rows = slice(r0, r0 + SUB)
            dz = dz_e[pl.ds(r0, SUB), :]
            z = None
            du = None
            for k in range(K_A):
                u_k = ua_e[pl.ds(HALO - (K_A - 1) + k + r0, SUB), :]
                t = wa[k:k + 1, :] * u_k
                z = t if z is None else z + t
                cacc_a[k] += _fold8(dz * u_k)
                t = wa[k:k + 1, :] * dz_e[pl.ds(r0 + (K_A - 1) - k, SUB), :]
                du = t if du is None else du + t
            outs = [dpa_r[rows, :].astype(F32) * z, du * ax[rows, :].astype(F32), du * ac[rows, :].astype(F32)]
            dvv = dv_e[pl.ds(r0, SUB), :]
            du = None
            for k in range(K_B):
                cacc_b[k] += _fold8(dvv * _window(ub_e, ub_s, HALO - (K_B - 1) + k + r0, SUB))
                t = wb[k:k + 1, :] * _window(dv_e, dv_s, r0 + (K_B - 1) - k, SUB)
                du = t if du is None else du + t
            sg = _sigmoid(bg[rows, :].astype(F32))
            outs.append(du * sg)
            outs.append(du * bv[rows, :].astype(F32) * sg * (1.0 - sg))
            for b, o in enumerate(outs):
                dpj_o[rows, b * d:(b + 1) * d] = o.astype(BF16)
                bacc[b] += _fold8(o)
            for g, w in enumerate(POOL_WINDOWS):
                cols = slice(g * gc, (g + 1) * gc)
                acc = q_e[pl.ds(r0, SUB), cols]
                for j in range(1, w):
                    acc = acc + q_e[pl.ds(r0 + j, SUB), cols]
                o = acc - dp_r[rows, cols].astype(F32)
                dpj_o[rows, 5 * d + g * gc:5 * d + (g + 1) * gc] = o.astype(BF16)
                bacc[5, :, cols] += _fold8(o)

        @pl.when(i == n - 1)
        def _():
            for b in range(6):
                dbin_o[:, b * d:(b + 1) * d] = jnp.sum(bacc[b], axis=0, keepdims=True)
            dca_o[...] = jnp.sum(cacc_a[...], axis=1)
            dcb_o[...] = jnp.sum(cacc_b[...], axis=1)

    ka8 = 8
    kb8 = 32
    return pl.pallas_call(
        body, name=name, grid=(n,),
        in_specs=[tile(0), tile(1), tile(2), tile(3), tile(4),
                  _prev_halo(tm, d, 1), _prev_halo(tm, d, 2), _prev_halo(tm, d, 3), _prev_halo(tm, d, 4),
                  tile(), tile(), tile(),
                  _next_halo(tm, d, 0, nh), _next_halo(tm, d, 0, nh), _next_halo(tm, d, 0, nh), _next_halo(tm, d, 0, nh),
                  vec(K_A), vec(K_B), ANY],
        out_specs=[pl.BlockSpec((tm, 6 * d), lambda i: (i, 0)), pl.BlockSpec((1, 6 * d), lambda i: (0, 0)),
                   pl.BlockSpec((ka8, d), lambda i: (0, 0)), pl.BlockSpec((kb8, d), lambda i: (0, 0))],
        out_shape=[jax.ShapeDtypeStruct((s, 9 * d), BF16), jax.ShapeDtypeStruct((1, 6 * d), F32),
                   jax.ShapeDtypeStruct((ka8, d), F32), jax.ShapeDtypeStruct((kb8, d), F32)],
        scratch_shapes=[pltpu.VMEM((HALO + tm, d), F32), pltpu.VMEM((HALO + tm, d), F32),
                        pltpu.VMEM((tm + HALO, d), F32), pltpu.VMEM((tm + HALO, d), F32),
                        pltpu.VMEM((tm + HALO, d), F32),
                        pltpu.VMEM((6, 8, d), F32), pltpu.VMEM((ka8, 8, d), F32), pltpu.VMEM((kb8, 8, d), F32),
                        pltpu.VMEM((7, tm + HALO - 8, d), F32), pltpu.VMEM((7, tm + HALO - 8, d), F32)],
        input_output_aliases={18: 0},
        compiler_params=_params("arbitrary"),
    )(proj, proj, proj, proj, proj, proj, proj, proj, proj, dpa, dv, dp, proj, dpa, dv, dp, conv_a, conv_b, dproj)


def _dh_rms_bwd(dproj, wint, x0, dx1, g, name, rider=None):
    s, p = dproj.shape
    d = x0.shape[1]
    tm = _tile(s, 1024)
    tk = _tile(p, 2304, 128)
    n = s // tm
    nk = p // tk

    def body(dp_r, w_r, x_r, dx1_r, g_r, dx_o, dg_o, acc, gacc):
        i = pl.program_id(0)
        k = pl.program_id(1)

        @pl.when((i == 0) & (k == 0))
        def _():
            gacc[...] = jnp.zeros_like(gacc)

        @pl.when(k == 0)
        def _():
            acc[...] = jnp.zeros_like(acc)

        acc[...] += _dot(dp_r[...], w_r[...])

        @pl.when(k == nk - 1)
        def _():
            dx, dg = _rms_bwd(acc[...], x_r[...], g_r[...])
            gacc[...] += dg
            dx_o[...] = dx1_r[...] + dx

        @pl.when((i == n - 1) & (k == nk - 1))
        def _():
            dg_o[...] = jnp.sum(gacc[...], axis=0, keepdims=True)

    return _hosted_call(
        body, name=name, grid=(n, nk),
        in_specs=[pl.BlockSpec((tm, tk), lambda i, k: (i, k)), pl.BlockSpec((tk, d), lambda i, k: (k, 0)),
                  pl.BlockSpec((tm, d), lambda i, k: (i, 0)), pl.BlockSpec((tm, d), lambda i, k: (i, 0)),
                  pl.BlockSpec((1, d), lambda i, k: (0, 0))],
        out_specs=[pl.BlockSpec((tm, d), lambda i, k: (i, 0)), pl.BlockSpec((1, d), lambda i, k: (0, 0))],
        out_shape=[jax.ShapeDtypeStruct((s, d), F32), jax.ShapeDtypeStruct((1, d), F32)],
        scratch_shapes=[pltpu.VMEM((tm, d), F32), pltpu.VMEM((8, d), F32)],
        args=(dproj, wint, x0, dx1, g.reshape(1, d)), rider=rider)


EARLY = ("w_out_a", "w_out_b", "w_pool", "w_o", "w_mlp1", "w_mlp2")


def _layer_fwd(x0, l, g_mix, win3, b_in, weights, proj_rider=None, bo_rider=None, mlp_rider=None):
    h = _rms_fwd(x0, g_mix, f"rms_fwd_{l}")
    proj, carried_proj = _proj(h, win3, b_in, f"proj_{l}", proj_rider)
    w = weights(carried_proj) if callable(weights) else weights
    pa, sb, p, v = _mix_fwd(proj, w["conv_a"], w["conv_b"], w["conv_b_bias"], w["ln_b_g"], w["ln_b_b"], f"mix_fwd_{l}")
    (ya, yb, yc, mg, x1, h2), carried_bo = _branch_out(pa, sb, p, proj, x0, w["woa"], w["wob"], w["wp"], w["wo"],
                                                       w["b_out_b"], w["pool_scale"], w["g_mlp"], f"branch_out_{l}", bo_rider)
    rider = mlp_rider(carried_bo) if callable(mlp_rider) else mlp_rider
    (f, a, x2), carried_mlp = _mlp_fwd(h2, x1, w["w1_3"], w["w2"], f"mlp_fwd_{l}", rider)
    saved = dict(x0=x0, h=h, proj=proj, pa=pa, sb=sb, p=p, v=v, ya=ya, yb=yb, yc=yc, mg=mg, x1=x1, h2=h2, f=f, a=a)
    return x2, saved, w, carried_mlp


def _layer_bwd_early(dx2, w, sv, l, mlp_rider=None, dw2_rider=None, swap_early=False):
    d = dx2.shape[1]
    (df, dx1, dgm), carried_mlp = _mlp_bwd(dx2, sv["f"], sv["x1"], w["w2t"], w["w1t"], w["g_mlp"], f"mlp_bwd_{l}",
                                           mlp_rider)
    if dw2_rider is None:
        dw2, carried_dw2 = _tn(sv["a"], dx2, 1, f"dw2_{l}"), []
    else:
        dw2, carried_dw2 = _tn(sv["a"], dx2, 1, f"dw2_{l}", dw2_rider)
    dw1 = _tn(sv["h2"], df, N_CHIPS, f"dw1_{l}")
    dpa, ds, dp, dproj, dya, dyb, dyc, dbob, dps, dbg = _branch_out_bwd(
        dx1, sv["proj"], sv["ya"], sv["yb"], sv["yc"], w["wot"], w["woat"], w["wobt"], w["wpt"], w["pool_scale"],
        f"branch_out_bwd_{l}")
    dwo = _tn(sv["mg"], dx1, 1, f"dwo_{l}")
    dwoa = _tn(sv["pa"], dya, 1, f"dwoa_{l}")
    dwob = _tn(sv["sb"], dyb, 1, f"dwob_{l}")
    dwp = _tn_groups(sv["p"], dyc, f"dwp_{l}")
    ff = dw2.shape[1]
    gc = d // N_GROUPS
    big = dict(
        w_out_a=dwoa.reshape(N_CHIPS, d // N_CHIPS, d),
        w_out_b=dwob.reshape(N_CHIPS, d // N_CHIPS, d),
        w_pool=dwp.reshape(N_GROUPS, N_CHIPS, gc // N_CHIPS, gc).transpose(1, 0, 2, 3).reshape(N_CHIPS, gc, gc),
        w_o=dwo.reshape(N_CHIPS, d // N_CHIPS, d),
        w_mlp1=dw1,
        w_mlp2=dw2.reshape(N_CHIPS, ff // N_CHIPS, d),
    )
    swap = _ride_swap([big[k] for k in EARLY]) if swap_early else None
    (dv, dlg, dlb, dcb), swapped = _ln_silu_bwd(sv["v"], ds, w["ln_b_g"], w["ln_b_b"], f"ln_silu_bwd_{l}", swap)
    dproj, dbm, dca, dcvb = _mix_bwd(sv["proj"], dpa, dv, dp, dproj, w["conv_a"], w["conv_b"], f"mix_bwd_{l}")
    small = dict(
        b_in=jnp.concatenate([dbm, dbg], axis=1).reshape(9, d), conv_a=dca[:K_A], conv_b=dcvb[:K_B],
        conv_b_bias=dcb, ln_b_g=dlg, ln_b_b=dlb, b_out_b=dbob, pool_scale=dps, g_mlp=dgm,
    )
    return dx1, dproj, big, small, carried_mlp, carried_dw2, swapped


def _layer_bwd_late(dx1, dproj, w, sv, l, dwin_rider=None, dh_rider=None):
    if dwin_rider is None:
        dwin, carried_dwin = _tn(sv["h"], dproj, N_CHIPS, f"dwin_{l}"), []
    else:
        dwin, carried_dwin = _tn(sv["h"], dproj, N_CHIPS, f"dwin_{l}", dwin_rider)
    rider = dh_rider(dwin) if callable(dh_rider) else dh_rider
    (dx0, dgmix), carried_dh = _dh_rms_bwd(dproj, w["wint"], sv["x0"], dx1, w["g_mix"], f"dh_{l}", rider)
    return dx0, dwin, dgmix, carried_dwin, carried_dh


def _place():
    x, y, c = lax.axis_index("x"), lax.axis_index("y"), lax.axis_index("c")
    chips = [(1 - x, y), (x, 1 - y), (1 - x, 1 - y)]
    return x, y, c, chips


def _place_shard(w, chip_arr, dtype, name):
    rows, cols = w.shape
    tr = _tile(rows, max(16, (1 << 19) // cols), 16)

    def body(k_ref, w_r, o_r):
        o_r[...] = w_r[...].astype(dtype)

    return pl.pallas_call(
        body, name=name,
        grid_spec=pltpu.PrefetchScalarGridSpec(
            num_scalar_prefetch=1, grid=(rows // tr,),
            in_specs=[pl.BlockSpec((tr, cols), lambda i, k_ref: (i, 0))],
            out_specs=pl.BlockSpec((None, tr, cols), lambda i, k_ref: (k_ref[0], i, 0))),
        out_shape=jax.ShapeDtypeStruct((N_CHIPS, rows, cols), dtype),
        compiler_params=_params("parallel"),
    )(chip_arr, w)


def _gather_rows(v, name):
    def body(v_ref, o_ref, send, recv, lsem):
        x, y, c = lax.axis_index("x"), lax.axis_index("y"), lax.axis_index("c")
        me = 4 * x + 2 * y + c
        local = pltpu.make_async_copy(v_ref, o_ref.at[me], lsem)
        local.start()

        def copy(m, src_dev):
            peer = (x ^ ((m >> 2) & 1), y ^ ((m >> 1) & 1), c ^ (m & 1))
            return pltpu.make_async_remote_copy(
                src_ref=v_ref, dst_ref=o_ref.at[src_dev], send_sem=send.at[m - 1], recv_sem=recv.at[m - 1],
                device_id=peer, device_id_type=MESH)

        for m in range(1, N_DEV):
            copy(m, me).start()
        for m in range(1, N_DEV):
            copy(m, me ^ m).wait_recv()
        for m in range(1, N_DEV):
            copy(m, me).wait_send()
        local.wait()

    return pl.pallas_call(
        body, name=name, in_specs=[ANY], out_specs=ANY,
        out_shape=jax.ShapeDtypeStruct((N_DEV, *v.shape), v.dtype),
        scratch_shapes=[pltpu.SemaphoreType.DMA((N_DEV - 1,)), pltpu.SemaphoreType.DMA((N_DEV - 1,)),
                        pltpu.SemaphoreType.DMA],
    )(v)


def _ride_swap(grads):
    n = len(grads)

    def copy(ins, outs, sems, a, k):
        x, y, c, _ = _place()
        h = ins[a].shape[1] // 2
        return pltpu.make_async_remote_copy(
            src_ref=ins[a].at[k, pl.ds(pl.multiple_of((1 - c) * h, 8), h)], dst_ref=outs[a].at[k],
            send_sem=sems[0].at[a * N_CHIPS + k], recv_sem=sems[1].at[a * N_CHIPS + k], device_id=(x, y, 1 - c),
            device_id_type=MESH)

    def start(ins, outs, sems):
        for a in range(n):
            for k in range(N_CHIPS):
                copy(ins, outs, sems, a, k).start()

    def finish(ins, outs, sems):
        for a in range(n):
            for k in range(N_CHIPS):
                copy(ins, outs, sems, a, k).wait()

    return _Rider(grads, [jax.ShapeDtypeStruct((N_CHIPS, g.shape[1] // 2, g.shape[2]), g.dtype) for g in grads], [],
                  [pltpu.SemaphoreType.DMA((N_CHIPS * n,))] * 2, start, finish)


def _add_halves(g, r, c_arr, name):
    nk, rows, cols = g.shape
    h = rows // 2
    tr = _tile(h, max(8, (1 << 19) // cols), 8)
    nblk = h // tr

    def body(c_ref, g_r, r_r, o_r):
        o_r[...] = (g_r[...] + r_r[...]).astype(BF16)

    return pl.pallas_call(
        body, name=name,
        grid_spec=pltpu.PrefetchScalarGridSpec(
            num_scalar_prefetch=1, grid=(nk, nblk),
            in_specs=[pl.BlockSpec((None, tr, cols), lambda k, i, c_ref: (k, c_ref[0] * nblk + i, 0)),
                      pl.BlockSpec((None, tr, cols), lambda k, i, c_ref: (k, i, 0))],
            out_specs=pl.BlockSpec((None, tr, cols), lambda k, i, c_ref: (k, i, 0))),
        out_shape=jax.ShapeDtypeStruct((nk, h, cols), BF16),
        compiler_params=_params("parallel", "parallel"),
    )(c_arr, g, r)


def _ride_exchange(parts):
    n = len(parts)

    def copy(ins, outs, sems, a, j):
        x, y, c, chips = _place()
        px, py = chips[j]
        return pltpu.make_async_remote_copy(
            src_ref=ins[a].at[2 * px + py], dst_ref=outs[a].at[j], send_sem=sems[0].at[a * 3 + j],
            recv_sem=sems[1].at[a * 3 + j], device_id=(px, py, c), device_id_type=MESH)

    def start(ins, outs, sems):
        for a in range(n):
            for j in range(3):
                copy(ins, outs, sems, a, j).start()

    def finish(ins, outs, sems):
        for a in range(n):
            for j in range(3):
                copy(ins, outs, sems, a, j).wait()

    return _Rider(parts, [jax.ShapeDtypeStruct((3, *p.shape[1:]), p.dtype) for p in parts], [],
                  [pltpu.SemaphoreType.DMA((3 * n,))] * 2, start, finish)


def _ride_gather(placed=(), landed=(), forward=True):
    placed, landed = list(placed), list(landed)
    n_ici = len(placed)
    fwd = (list(range(n_ici)) if forward else []) + list(range(n_ici, n_ici + len(landed)))

    def ici(ins, outs, sems, a, j, src_chip):
        x, y, c, chips = _place()
        h = ins[a].shape[1] // 2
        rows = pl.ds(pl.multiple_of(c * h, 8), h)
        return pltpu.make_async_remote_copy(
            src_ref=ins[a].at[2 * x + y, rows], dst_ref=outs[a].at[src_chip, rows], send_sem=sems[0].at[a * 3 + j],
            recv_sem=sems[1].at[a * 3 + j], device_id=(*chips[j], c), device_id_type=MESH)

    def d2d(ins, outs, sems, a, j, which):
        x, y, c, chips = _place()
        px, py = chips[j]
        h = ins[a].shape[1] // 2
        rows = pl.ds(pl.multiple_of(which * h, 8), h)
        return pltpu.make_async_remote_copy(
            src_ref=ins[a].at[2 * px + py, rows], dst_ref=outs[a].at[2 * px + py, rows], send_sem=sems[2].at[a * 3 + j],
            recv_sem=sems[3].at[a * 3 + j], device_id=(x, y, 1 - c), device_id_type=MESH)

    def start(ins, outs, sems):
        x, y, c, _ = _place()
        for a in range(n_ici):
            for j in range(3):
                ici(ins, outs, sems, a, j, 2 * x + y).start()
        for a in range(n_ici, n_ici + len(landed)):
            for j in range(3):
                d2d(ins, outs, sems, a, j, c).start()

    def land(ins, outs, sems, then_forward):
        _, _, c, chips = _place()
        for a in range(n_ici):
            for j, (px, py) in enumerate(chips):
                ici(ins, outs, sems, a, j, 2 * px + py).wait_recv()
                if then_forward:
                    d2d(ins, outs, sems, a, j, c).start()

    def middle(ins, outs, sems):
        land(ins, outs, sems, True)

    def finish(ins, outs, sems):
        x, y, c, _ = _place()
        if not forward:
            land(ins, outs, sems, False)
        for a in fwd:
            for j in range(3):
                d2d(ins, outs, sems, a, j, 1 - c).wait_recv()
        for a in range(n_ici):
            for j in range(3):
                ici(ins, outs, sems, a, j, 2 * x + y).wait_send()
        for a in fwd:
            for j in range(3):
                d2d(ins, outs, sems, a, j, c).wait_send()

    arrays = placed + landed
    n = len(arrays)
    return _Rider(arrays, [jax.ShapeDtypeStruct(p.shape, p.dtype) for p in arrays], [(a, a) for a in range(n)],
                  [pltpu.SemaphoreType.DMA((3 * n,))] * 4, start, finish, middle if (n_ici and forward) else None)


def _sum_chips(r, name):
    nk, rows, cols = r.shape
    tr = _tile(rows, max(8, (1 << 19) // cols), 8)

    def body(r_r, o_r):
        acc = r_r[0].astype(F32)
        for k in range(1, nk):
            acc = acc + r_r[k].astype(F32)
        o_r[...] = acc

    return pl.pallas_call(
        body, name=name, grid=(rows // tr,),
        in_specs=[pl.BlockSpec((nk, tr, cols), lambda i: (0, i, 0))],
        out_specs=pl.BlockSpec((tr, cols), lambda i: (i, 0)),
        out_shape=jax.ShapeDtypeStruct((rows, cols), F32),
        compiler_params=_params("parallel"),
    )(r)


def _sum_chips_into(own, arrived, dst, layer, n_layers, place_arr, name):
    _, h, cols = own.shape
    tr = _tile(h, max(8, (1 << 19) // cols), 8)
    nblk = h // tr

    def body(p_ref, own_r, arr_r, *rest):
        o_r = rest[-1]
        acc = own_r[...].astype(F32)
        for j in range(3):
            acc = acc + arr_r[j].astype(F32)
        o_r[...] = acc

    in_specs = [pl.BlockSpec((None, tr, cols), lambda i, p_ref: (p_ref[0], i, 0)),
                pl.BlockSpec((3, tr, cols), lambda i, p_ref: (0, i, 0))]
    args = [place_arr, own, arrived]
    aliases = {}
    if dst is not None:
        in_specs.append(ANY)
        args.append(dst)
        aliases = {3: 0}
    return pl.pallas_call(
        body, name=name,
        grid_spec=pltpu.PrefetchScalarGridSpec(
            num_scalar_prefetch=1, grid=(nblk,), in_specs=in_specs,
            out_specs=pl.BlockSpec((None, tr, cols), lambda i, p_ref: (layer, p_ref[1] * nblk + i, 0))),
        out_shape=jax.ShapeDtypeStruct((n_layers, 2 * h, cols), F32),
        input_output_aliases=aliases,
        compiler_params=_params("parallel"),
    )(*args)


def _share_halves(shards, name):
    n = len(shards)
    n_layers = shards[0].shape[0]

    def body(*refs):
        ins, outs = refs[:n], refs[n:2 * n]
        send, recv = refs[2 * n:]
        x, y, c, _ = _place()
        sib = (x, y, 1 - c)

        def copy(a, l, which):
            h = ins[a].shape[1] // 2
            rows = pl.ds(pl.multiple_of(which * h, 8), h)
            return pltpu.make_async_remote_copy(
                src_ref=ins[a].at[l, rows], dst_ref=outs[a].at[l, rows], send_sem=send.at[a * n_layers + l],
                recv_sem=recv.at[a * n_layers + l], device_id=sib, device_id_type=MESH)

        for a in range(n):
            for l in range(n_layers):
                copy(a, l, c).start()
        for a in range(n):
            for l in range(n_layers):
                copy(a, l, 1 - c).wait_recv()
        for a in range(n):
            for l in range(n_layers):
                copy(a, l, c).wait_send()

    return pl.pallas_call(
        body, name=name, in_specs=[ANY] * n, out_specs=[ANY] * n,
        out_shape=[jax.ShapeDtypeStruct(a.shape, a.dtype) for a in shards],
        scratch_shapes=[pltpu.SemaphoreType.DMA((n * n_layers,))] * 2,
        input_output_aliases={a: a for a in range(n)},
    )(*shards)


def _adamw(w, g, m, v, name):
    rows, cols = w.shape
    tr = _tile(rows, max(8, (1 << 18) // cols), 8)
    c1 = 1.0 - ADAM_B1 ** ADAM_STEP
    c2 = 1.0 - ADAM_B2 ** ADAM_STEP

    def body(w_r, g_r, m_r, v_r, d_o, m_o, v_o):
        gv = g_r[...]
        mn = ADAM_B1 * m_r[...] + (1.0 - ADAM_B1) * gv
        vn = ADAM_B2 * v_r[...] + (1.0 - ADAM_B2) * (gv * gv)
        m_o[...] = mn
        v_o[...] = vn
        d_o[...] = -ADAM_LR * ((mn / c1) / (jnp.sqrt(vn / c2) + ADAM_EPS) + ADAM_WD * w_r[...])

    spec = pl.BlockSpec((tr, cols), lambda i: (i, 0))
    return pl.pallas_call(
        body, name=name, grid=(rows // tr,), in_specs=[spec] * 4, out_specs=[spec] * 3,
        out_shape=[jax.ShapeDtypeStruct((rows, cols), F32)] * 3,
        compiler_params=_params("parallel"),
    )(w, g, m, v)


BIG = ("w_in", "w_out_a", "w_out_b", "w_pool", "w_o", "w_mlp1", "w_mlp2")
SMALL = ("g_mix", "b_in", "conv_b_bias", "ln_b_g", "ln_b_b", "b_out_b", "pool_scale", "g_mlp")
CONVS = ("conv_a", "conv_b")
WEIGHTS = ("g_mix", "w_in", "b_in", "conv_a", "w_out_a", "conv_b", "conv_b_bias", "ln_b_g", "ln_b_b", "w_out_b", "b_out_b",
           "w_pool", "pool_scale", "w_o", "g_mlp", "w_mlp1", "w_mlp2", "g_final")


def _as2d(a):
    return a.reshape(-1, a.shape[-1])


def _pad_rows(a, rows):
    return jnp.pad(a, ((0, rows - a.shape[0]), (0, 0)))


def _full_weights(big_g, conv_g, rep, l, d):
    gc = d // N_GROUPS
    w = {k: v[l] for k, v in rep.items()}
    w["b_in"] = rep["b_in"][l].reshape(1, -1)
    win3 = big_g["w_in"]
    w["win3"] = win3
    w["wint"] = win3.transpose(0, 2, 1).reshape(-1, d)
    for src, dst in (("w_out_a", "woa"), ("w_out_b", "wob"), ("w_o", "wo")):
        full = big_g[src].reshape(d, d)
        w[dst] = full
        w[dst + "t"] = full.T
    wp = big_g["w_pool"].reshape(N_CHIPS, N_GROUPS, gc // N_CHIPS, gc).transpose(1, 0, 2, 3).reshape(N_GROUPS, gc, gc)
    w["wp"] = wp
    w["wpt"] = wp.transpose(0, 2, 1)
    w1_3 = big_g["w_mlp1"]
    w["w1_3"] = w1_3
    w["w1t"] = w1_3.transpose(0, 2, 1).reshape(-1, d)
    w2 = big_g["w_mlp2"].reshape(-1, d)
    w["w2"] = w2
    w["w2t"] = w2.T
    ca, cb = conv_g
    w["conv_a"] = ca
    w["conv_b"] = cb
    return w


def kernel(x, g_mix, w_in, b_in, conv_a, w_out_a, conv_b, conv_b_bias, ln_b_g, ln_b_b, w_out_b, b_out_b, w_pool, pool_scale, w_o, g_mlp, w_mlp1, w_mlp2, g_final, loss_target, m_g_mix, m_w_in, m_b_in, m_conv_a, m_w_out_a, m_conv_b, m_conv_b_bias, m_ln_b_g, m_ln_b_b, m_w_out_b, m_b_out_b, m_w_pool, m_pool_scale, m_w_o, m_g_mlp, m_w_mlp1, m_w_mlp2, m_g_final, v_g_mix, v_w_in, v_b_in, v_conv_a, v_w_out_a, v_conv_b, v_conv_b_bias, v_ln_b_g, v_ln_b_b, v_w_out_b, v_b_out_b, v_w_pool, v_pool_scale, v_w_o, v_g_mlp, v_w_mlp1, v_w_mlp2, v_g_final):
    given = dict(g_mix=g_mix, w_in=w_in, b_in=b_in, conv_a=conv_a, w_out_a=w_out_a, conv_b=conv_b, conv_b_bias=conv_b_bias, ln_b_g=ln_b_g, ln_b_b=ln_b_b, w_out_b=w_out_b, b_out_b=b_out_b, w_pool=w_pool, pool_scale=pool_scale, w_o=w_o, g_mlp=g_mlp, w_mlp1=w_mlp1, w_mlp2=w_mlp2, g_final=g_final)
    mom = dict(g_mix=m_g_mix, w_in=m_w_in, b_in=m_b_in, conv_a=m_conv_a, w_out_a=m_w_out_a, conv_b=m_conv_b, conv_b_bias=m_conv_b_bias, ln_b_g=m_ln_b_g, ln_b_b=m_ln_b_b, w_out_b=m_w_out_b, b_out_b=m_b_out_b, w_pool=m_w_pool, pool_scale=m_pool_scale, w_o=m_w_o, g_mlp=m_g_mlp, w_mlp1=m_w_mlp1, w_mlp2=m_w_mlp2, g_final=m_g_final)
    var = dict(g_mix=v_g_mix, w_in=v_w_in, b_in=v_b_in, conv_a=v_conv_a, w_out_a=v_w_out_a, conv_b=v_conv_b, conv_b_bias=v_conv_b_bias, ln_b_g=v_ln_b_g, ln_b_b=v_ln_b_b, w_out_b=v_w_out_b, b_out_b=v_b_out_b, w_pool=v_w_pool, pool_scale=v_pool_scale, w_o=v_w_o, g_mlp=v_g_mlp, w_mlp1=v_w_mlp1, w_mlp2=v_w_mlp2, g_final=v_g_final)
    n_layers = w_in.shape[0]
    s, d = x.shape[1], x.shape[2]
    dq = d // N_CHIPS
    x_idx, y_idx, c_idx = lax.axis_index("x"), lax.axis_index("y"), lax.axis_index("c")
    chip = 2 * x_idx + y_idx
    c_arr = c_idx.astype(jnp.int32).reshape(1)

    conv_rows = n_layers * (K_A + K_B)
    conv_pad = -(-conv_rows // 16) * 16
    conv_pack = _pad_rows(jnp.concatenate([conv_a[l] for l in range(n_layers)] + [conv_b[l] for l in range(n_layers)], axis=0),
                          conv_pad)
    chip_arr = chip.astype(jnp.int32).reshape(1)
    place_arr = jnp.stack([chip, c_idx]).astype(jnp.int32)

    def placed(k, l):
        return _place_shard(_as2d(given[k][l]), chip_arr, BF16, f"place_{k}_{l}")

    rest = BIG[1:]
    win_g, conv_all = _comm_call(_ride_gather([placed("w_in", 0), _place_shard(conv_pack, chip_arr, F32, "place_convs")]),
                                 "gather_w_in_0")
    conv_full = conv_all.transpose(1, 0, 2).reshape(conv_pad, d)
    conv_g = [(conv_full[l * K_A:(l + 1) * K_A], conv_full[n_layers * K_A + l * K_B:n_layers * K_A + (l + 1) * K_B])
              for l in range(n_layers)]
    rep = {k: given[k] for k in SMALL}

    xl = x[0]
    ws, saved = [], []
    rest_g = None
    for l in range(n_layers):
        more = l + 1 < n_layers

        def weights(carried, l=l, win_g=win_g, rest_g=rest_g):
            return _full_weights(dict(zip(rest, carried if rest_g is None else rest_g), w_in=win_g), conv_g[l], rep, l, d)

        xl, sv, w, got = _layer_fwd(
            xl, l, given["g_mix"][l], win_g, given["b_in"][l].reshape(1, -1), weights,
            _ride_gather([placed(k, l) for k in rest]) if rest_g is None else None,
            _ride_gather([placed("w_in", l + 1)], forward=False) if more else None,
            (lambda landed, l=l: _ride_gather([placed(k, l + 1) for k in rest], landed)) if more else None)
        if more:
            rest_g, win_g = got[:len(rest)], got[len(rest)]
        ws.append(w)
        saved.append(sv)
    dx, loss, dgf = _loss_head(xl, loss_target[0], g_final, "loss_head")
    loss = lax.psum(loss[0, 0], ("x", "y", "c"))

    parts, arrived, small = {}, {}, [None] * n_layers
    pending = []
    for l in reversed(range(n_layers)):
        w, sv = ws[l], saved[l]
        first = [q for q in pending if q[1] not in ("w_mlp1", "w_mlp2")]
        second = [q for q in pending if q[1] in ("w_mlp1", "w_mlp2")]
        dx1, dproj, early, sm, got, got2, swapped = _layer_bwd_early(
            dx, w, sv, l, _ride_exchange([parts[q] for q in first]) if first else None,
            _ride_exchange([parts[q] for q in second]) if second else None, swap_early=(l == 0))
        arrived.update(zip(first + second, got + got2))
        if l > 0:
            order = ("w_in",) + EARLY
            dx, dwin, dgmix, _, swapped = _layer_bwd_late(
                dx1, dproj, w, sv, l, None, lambda dwin, early=early: _ride_swap([dwin] + [early[k] for k in EARLY]))
            full = dict(early, w_in=dwin)
            for k, r in zip(order, swapped):
                parts[(l, k)] = _add_halves(full[k], r, c_arr, f"grad_add_halves_{k}_{l}")
            pending = [(l, k) for k in order]
        else:
            for k, r in zip(EARLY, swapped):
                parts[(l, k)] = _add_halves(early[k], r, c_arr, f"grad_add_halves_{k}_{l}")

            def w_in_rider(dwin, l=l):
                (r,) = _comm_call(_ride_swap([dwin]), f"grad_swap_w_in_{l}")
                parts[(l, "w_in")] = _add_halves(dwin, r, c_arr, f"grad_add_halves_w_in_{l}")
                return _ride_exchange([parts[(l, "w_in")]])

            dx, dwin, dgmix, got_early, got_w_in = _layer_bwd_late(
                dx1, dproj, w, sv, l, _ride_exchange([parts[(l, k)] for k in EARLY]), w_in_rider)
            arrived.update(zip([(l, k) for k in EARLY], got_early))
            arrived[(l, "w_in")] = got_w_in[0]
            pending = []
        sm["g_mix"] = dgmix
        small[l] = sm
    reduced = []
    for k in BIG:
        dst = None
        for l in range(n_layers):
            dst = _sum_chips_into(parts[(l, k)], arrived[(l, k)], dst, l, n_layers, place_arr, f"grad_sum_chips_{k}_{l}")
        reduced.append(dst)
    big_grad = dict(zip(BIG, _share_halves(reduced, "grad_share_halves")))

    small_rows = []
    for k in SMALL + CONVS:
        for l in range(n_layers):
            small_rows.append(small[l][k])
    small_rows.append(dgf)
    pack = jnp.concatenate(small_rows, axis=0)
    n_small = pack.shape[0]
    pack = _pad_rows(pack, -(-n_small // 8) * 8)
    small_sum = _sum_chips(_gather_rows(pack, "gather_small_grads"), "sum_small_grads")

    out = {}
    for k in BIG:
        shape = given[k].shape
        g2 = big_grad[k].reshape(-1, shape[-1])
        dlt, nm, nv = _adamw(given[k].reshape(g2.shape), g2, mom[k].reshape(g2.shape), var[k].reshape(g2.shape), f"adamw_{k}")
        out[k] = tuple(a.reshape(shape) for a in (g2, dlt, nm, nv))

    def rows_of(k):
        return {"b_in": 9, "conv_a": K_A, "conv_b": K_B}.get(k, 1)

    def pack_rep(src):
        rows = [src[k].reshape(n_layers * rows_of(k), d) for k in SMALL] + [src["g_final"].reshape(1, d)]
        a = jnp.concatenate(rows, axis=0)
        return _pad_rows(a, -(-a.shape[0] // 8) * 8)

    def pack_conv(src):
        a = jnp.concatenate([src[k].reshape(n_layers * rows_of(k), dq) for k in CONVS], axis=0)
        return _pad_rows(a, -(-a.shape[0] // 8) * 8)

    offs = {}
    r = 0
    for k in SMALL + CONVS:
        offs[k] = r
        r += n_layers * rows_of(k)
    offs["g_final"] = r
    n_rep = sum(n_layers * rows_of(k) for k in SMALL)
    g_rep = jnp.concatenate([small_sum[:n_rep], small_sum[offs["g_final"]:offs["g_final"] + 1]], axis=0)
    g_rep = _pad_rows(g_rep, -(-g_rep.shape[0] // 8) * 8)
    g_conv_full = small_sum[offs["conv_a"]:offs["g_final"]]
    g_conv = lax.dynamic_slice_in_dim(g_conv_full, chip * dq, dq, axis=1)
    g_conv = _pad_rows(g_conv, -(-g_conv.shape[0] // 8) * 8)
    rep_res = (g_rep,) + tuple(_adamw(pack_rep(given), g_rep, pack_rep(mom), pack_rep(var), "adamw_small"))
    conv_res = (g_conv,) + tuple(_adamw(pack_conv(given), g_conv, pack_conv(mom), pack_conv(var), "adamw_conv"))
    r = 0
    for k in SMALL:
        nr = n_layers * rows_of(k)
        out[k] = tuple(a[r:r + nr].reshape(given[k].shape) for a in rep_res)
        r += nr
    out["g_final"] = tuple(a[r].reshape(given["g_final"].shape) for a in rep_res)
    r = 0
    for k in CONVS:
        nr = n_layers * rows_of(k)
        out[k] = tuple(a[r:r + nr].reshape(given[k].shape) for a in conv_res)
        r += nr

    res = [loss, dx.reshape(x.shape)]
    for i in range(4):
        res += [out[k][i] for k in WEIGHTS]
    return tuple(res)
```

```python
import functools

import jax
import jax.numpy as jnp
from jax import lax
from jax.experimental import pallas as pl
from jax.experimental.pallas import tpu as pltpu

F32 = jnp.float32
BF16 = jnp.bfloat16
EPS = 1e-6
HALO = 32
SUB = 32
K_A = 3
K_B = 31
POOL_WINDOWS = (2, 4, 8, 16)
N_GROUPS = len(POOL_WINDOWS)
N_CHIPS = 4
N_DEV = 8
ADAM_LR = 0.001
ADAM_B1 = 0.9
ADAM_B2 = 0.999
ADAM_EPS = 1e-08
ADAM_WD = 0.01
ADAM_STEP = 10
VMEM_LIMIT = 56 * 1024 * 1024
MESH = pl.DeviceIdType.MESH
ANY = pl.BlockSpec(memory_space=pl.ANY)


def _params(*sem):
    return pltpu.CompilerParams(dimension_semantics=sem, vmem_limit_bytes=VMEM_LIMIT)


def _tile(n, pref, mult=8):
    if n <= pref:
        return n
    t = (pref // mult) * mult
    while t > mult and n % t:
        t -= mult
    assert n % t == 0, (n, pref, mult)
    return t


def _dot(a, b):
    return jnp.dot(a, b, preferred_element_type=F32)


def _sigmoid(x):
    return 1.0 / (1.0 + jnp.exp(-x))


def _f32(ref):
    return ref[...].astype(F32)


def _fold8(v):
    r, d = v.shape
    return v.reshape(r // 8, 8, d).sum(axis=0)


def _rms_bwd(dh, xv, g):
    r = lax.rsqrt(jnp.mean(xv * xv, axis=-1, keepdims=True) + EPS)
    xn = xv * r
    dxn = dh * g
    dx = r * (dxn - xn * jnp.mean(dxn * xn, axis=-1, keepdims=True))
    return dx, _fold8(dh * xn)


class _Rider:
    def __init__(self, args, out_shape, aliases, sems, start, finish, middle=None):
        self.args, self.out_shape, self.aliases, self.sems = list(args), list(out_shape), list(aliases), list(sems)
        self.start, self.finish, self.middle = start, finish, middle


def _hosted_call(body, *, name, grid, in_specs, out_specs, out_shape, scratch_shapes, args, rider=None):
    n_in, n_out, n_scr = len(in_specs), len(out_shape), len(scratch_shapes)
    params = pltpu.CompilerParams(dimension_semantics=("arbitrary",) * len(grid), vmem_limit_bytes=VMEM_LIMIT)
    if rider is None:
        res = pl.pallas_call(body, name=name, grid=grid, in_specs=in_specs, out_specs=out_specs, out_shape=out_shape,
                             scratch_shapes=scratch_shapes, compiler_params=params)(*args)
        return list(res), []
    k_in, k_out = len(rider.args), len(rider.out_shape)

    def hosted(*refs):
        ins, refs = refs[:n_in], refs[n_in:]
        r_in, refs = refs[:k_in], refs[k_in:]
        outs, refs = refs[:n_out], refs[n_out:]
        r_out, refs = refs[:k_out], refs[k_out:]
        scr, r_sems = refs[:n_scr], refs[n_scr:]
        first = functools.reduce(lambda u, v: u & v, [pl.program_id(a) == 0 for a in range(len(grid))])
        last = functools.reduce(lambda u, v: u & v, [pl.program_id(a) == grid[a] - 1 for a in range(len(grid))])

        @pl.when(first)
        def _():
            rider.start(r_in, r_out, r_sems)

        if rider.middle is not None:
            step = functools.reduce(lambda u, a: u * grid[a] + pl.program_id(a), range(len(grid)), 0)

            @pl.when(step == (3 * functools.reduce(lambda u, v: u * v, grid)) // 4)
            def _():
                rider.middle(r_in, r_out, r_sems)

        body(*ins, *outs, *scr)

        @pl.when(last)
        def _():
            rider.finish(r_in, r_out, r_sems)

    res = pl.pallas_call(
        hosted, name=name, grid=grid, in_specs=list(in_specs) + [ANY] * k_in, out_specs=list(out_specs) + [ANY] * k_out,
        out_shape=list(out_shape) + rider.out_shape, scratch_shapes=list(scratch_shapes) + rider.sems,
        input_output_aliases={n_in + i: n_out + o for i, o in rider.aliases}, compiler_params=params,
    )(*args, *rider.args)
    return list(res[:n_out]), list(res[n_out:])


def _comm_call(rider, name):
    def body(*refs):
        k_in, k_out = len(rider.args), len(rider.out_shape)
        r_in, r_out, r_sems = refs[:k_in], refs[k_in:k_in + k_out], refs[k_in + k_out:]
        rider.start(r_in, r_out, r_sems)
        if rider.middle is not None:
            rider.middle(r_in, r_out, r_sems)
        rider.finish(r_in, r_out, r_sems)

    return list(pl.pallas_call(
        body, name=name, in_specs=[ANY] * len(rider.args), out_specs=[ANY] * len(rider.out_shape),
        out_shape=rider.out_shape, scratch_shapes=rider.sems, input_output_aliases=dict(rider.aliases),
    )(*rider.args))


def _rms_fwd(x, g, name):
    s, d = x.shape
    tm = _tile(s, 512)

    def body(x_ref, g_ref, h_ref):
        xv = x_ref[...]
        r = lax.rsqrt(jnp.mean(xv * xv, axis=-1, keepdims=True) + EPS)
        h_ref[...] = (xv * r * g_ref[...]).astype(BF16)

    return pl.pallas_call(
        body, name=name, grid=(s // tm,),
        in_specs=[pl.BlockSpec((tm, d), lambda i: (i, 0)), pl.BlockSpec((1, d), lambda i: (0, 0))],
        out_specs=pl.BlockSpec((tm, d), lambda i: (i, 0)),
        out_shape=jax.ShapeDtypeStruct((s, d), BF16),
        compiler_params=_params("parallel"),
    )(x, g.reshape(1, d))


def _proj(h, w3, b, name, rider=None):
    s, k = h.shape
    nb, _, nw = w3.shape
    tm = _tile(s, 1024)
    tn = nw
    per = nw // tn

    def body(h_ref, w_ref, b_ref, o_ref):
        o_ref[...] = (_dot(h_ref[...], w_ref[...]) + b_ref[...]).astype(BF16)

    (proj,), carried = _hosted_call(
        body, name=name, grid=(s // tm, nb * per),
        in_specs=[
            pl.BlockSpec((tm, k), lambda i, j: (i, 0)),
            pl.BlockSpec((None, k, tn), lambda i, j: (j // per, 0, j % per)),
            pl.BlockSpec((1, tn), lambda i, j: (0, j)),
        ],
        out_specs=[pl.BlockSpec((tm, tn), lambda i, j: (i, j))],
        out_shape=[jax.ShapeDtypeStruct((s, nb * nw), BF16)],
        scratch_shapes=[], args=(h, w3, b), rider=rider)
    return proj, carried


def _shift_copies(src_e, dst_s):
    n_rows = dst_s.shape[1]
    for b in range(1, 8):
        dst_s[b - 1, :, :] = src_e[pl.ds(b, n_rows), :]


def _window(src_e, dst_s, off, rows):
    b = off % 8
    if b == 0:
        return src_e[pl.ds(off, rows), :]
    return dst_s[b - 1, pl.ds(off - b, rows), :]


def _prev_halo(tm, d, col):
    r = tm // HALO
    return pl.BlockSpec((HALO, d), lambda i: (jnp.maximum(i * r - 1, 0), col))


def _next_halo(tm, d, col, n_halo_blocks):
    r = tm // HALO
    return pl.BlockSpec((HALO, d), lambda i: (jnp.minimum((i + 1) * r, n_halo_blocks - 1), col))


def _mix_fwd(proj, conv_a, conv_b, cb, lg, lb, name):
    s, d9 = proj.shape
    d = d9 // 9
    gc = d // N_GROUPS
    tm = _tile(s, 256, HALO)

    def tile(col):
        return pl.BlockSpec((tm, d), lambda i: (i, col))

    def vec(rows):
        return pl.BlockSpec((rows, d), lambda i: (0, 0))

    def body(ab, ac, ax, bv, bg, ci, ac_h, ax_h, bv_h, bg_h, ci_h, wa, wb, cb_r, lg_r, lb_r,
             pa_o, s_o, p_o, v_o, ua_e, ub_e, ci_e, ub_s):
        i = pl.program_id(0)
        keep = (i > 0).astype(F32)
        ua_e[0:HALO, :] = _f32(ac_h) * _f32(ax_h) * keep
        ua_e[HALO:HALO + tm, :] = _f32(ac) * _f32(ax)
        ub_e[0:HALO, :] = _f32(bv_h) * _sigmoid(_f32(bg_h)) * keep
        ub_e[HALO:HALO + tm, :] = _f32(bv) * _sigmoid(_f32(bg))
        ci_e[0:HALO, :] = _f32(ci_h) * keep
        ci_e[HALO:HALO + tm, :] = _f32(ci)
        _shift_copies(ub_e, ub_s)
        for r0 in range(0, tm, SUB):
            rows = slice(r0, r0 + SUB)
            z = None
            for k in range(K_A):
                t = wa[k:k + 1, :] * ua_e[pl.ds(HALO - (K_A - 1) + k + r0, SUB), :]
                z = t if z is None else z + t
            pa_o[rows, :] = (ab[rows, :].astype(F32) * z).astype(BF16)
            v = None
            for k in range(K_B):
                t = wb[k:k + 1, :] * _window(ub_e, ub_s, HALO - (K_B - 1) + k + r0, SUB)
                v = t if v is None else v + t
            v = v + cb_r[...]
            v_o[rows, :] = v
            mu = jnp.mean(v, axis=-1, keepdims=True)
            vc = v - mu
            rstd = lax.rsqrt(jnp.mean(vc * vc, axis=-1, keepdims=True) + EPS)
            ln = vc * rstd * lg_r[...] + lb_r[...]
            s_o[rows, :] = (ln * _sigmoid(ln)).astype(BF16)
            t_idx = i * tm + r0 + lax.broadcasted_iota(jnp.int32, (SUB, gc), 0)
            for g, w in enumerate(POOL_WINDOWS):
                cols = slice(g * gc, (g + 1) * gc)
                cur = ci_e[pl.ds(HALO + r0, SUB), cols]
                acc = cur
                for j in range(1, w):
                    acc = acc + ci_e[pl.ds(HALO + r0 - j, SUB), cols]
                cnt = jnp.minimum(t_idx + 1, w).astype(F32)
                p_o[rows, cols] = (acc / cnt - cur).astype(BF16)

    return pl.pallas_call(
        body, name=name, grid=(s // tm,),
        in_specs=[tile(0), tile(1), tile(2), tile(3), tile(4), tile(5),
                  _prev_halo(tm, d, 1), _prev_halo(tm, d, 2), _prev_halo(tm, d, 3), _prev_halo(tm, d, 4),
                  _prev_halo(tm, d, 5),
                  vec(K_A), vec(K_B), vec(1), vec(1), vec(1)],
        out_specs=[pl.BlockSpec((tm, d), lambda i: (i, 0))] * 4,
        out_shape=[jax.ShapeDtypeStruct((s, d), BF16)] * 3 + [jax.ShapeDtypeStruct((s, d), F32)],
        scratch_shapes=[pltpu.VMEM((HALO + tm, d), F32)] * 3 + [pltpu.VMEM((7, tm + HALO - 8, d), F32)],
        compiler_params=_params("parallel"),
    )(proj, proj, proj, proj, proj, proj, proj, proj, proj, proj, proj,
      conv_a, conv_b, cb.reshape(1, d), lg.reshape(1, d), lb.reshape(1, d))


def _branch_out(pa, sb, p, proj, x0, woa, wob, wp, wo, bob, ps, gm, name, rider=None):
    s, d = pa.shape
    gc = d // N_GROUPS
    tm = _tile(s, 512)

    def tile(col=0):
        return pl.BlockSpec((tm, d), lambda i: (i, col))

    def const(shape):
        return pl.BlockSpec(shape, lambda i: (0,) * len(shape))

    def body(pa_r, s_r, p_r, g0, g1, g2, x0_r, woa_r, wob_r, wp_r, wo_r, bob_r, ps_r, gm_r,
             ya_o, yb_o, yc_o, mg_o, x1_o, h2_o, yc_s):
        ya = _dot(pa_r[...], woa_r[...])
        yb = _dot(s_r[...], wob_r[...]) + bob_r[...]
        for g in range(N_GROUPS):
            cols = slice(g * gc, (g + 1) * gc)
            yc_s[:, cols] = _dot(p_r[:, cols], wp_r[g])
        yc = yc_s[...]
        ya_o[...] = ya.astype(BF16)
        yb_o[...] = yb.astype(BF16)
        yc_o[...] = yc.astype(BF16)
        m = _sigmoid(_f32(g0)) * ya + _sigmoid(_f32(g1)) * yb + _sigmoid(_f32(g2)) * (yc * ps_r[...])
        mb = m.astype(BF16)
        mg_o[...] = mb
        x1 = x0_r[...] + _dot(mb, wo_r[...])
        x1_o[...] = x1
        r = lax.rsqrt(jnp.mean(x1 * x1, axis=-1, keepdims=True) + EPS)
        h2_o[...] = (x1 * r * gm_r[...]).astype(BF16)

    return _hosted_call(
        body, name=name, grid=(s // tm,),
        in_specs=[tile(), tile(), tile(), tile(6), tile(7), tile(8), tile(),
                  const((d, d)), const((d, d)), const((N_GROUPS, gc, gc)), const((d, d)),
                  const((1, d)), const((1, d)), const((1, d))],
        out_specs=[tile()] * 6,
        out_shape=[jax.ShapeDtypeStruct((s, d), BF16)] * 4 + [jax.ShapeDtypeStruct((s, d), F32),
                                                               jax.ShapeDtypeStruct((s, d), BF16)],
        scratch_shapes=[pltpu.VMEM((tm, d), F32)],
        args=(pa, sb, p, proj, proj, proj, x0, woa, wob, wp, wo, bob.reshape(1, d), ps.reshape(1, d), gm.reshape(1, d)),
        rider=rider)


def _mlp_fwd(h2, x1, w1_3, w2, g_next, name, rider=None):
    s, d = h2.shape
    nf, _, tf = w1_3.shape
    tm = _tile(s, 1024)

    chain = g_next is not None

    def body(h_r, x1_r, w1_r, w2_r, *rest):
        gn_r, f_o, a_o, x2_o, hn_o, acc = rest if chain else (None, *rest[:3], None, rest[3])
        j = pl.program_id(1)

        @pl.when(j == 0)
        def _():
            acc[...] = jnp.zeros_like(acc)

        f = _dot(h_r[...], w1_r[...])
        f_o[...] = f.astype(BF16)
        rl = jnp.maximum(f, 0.0)
        a = (rl * rl).astype(BF16)
        a_o[...] = a
        acc[...] += _dot(a, w2_r[...])

        @pl.when(j == nf - 1)
        def _():
            x2 = x1_r[...] + acc[...]
            x2_o[...] = x2
            if chain:
                r = lax.rsqrt(jnp.mean(x2 * x2, axis=-1, keepdims=True) + EPS)
                hn_o[...] = (x2 * r * gn_r[...]).astype(BF16)

    row = pl.BlockSpec((tm, d), lambda i, j: (i, 0))
    return _hosted_call(
        body, name=name, grid=(s // tm, nf),
        in_specs=[row, row, pl.BlockSpec((None, d, tf), lambda i, j: (j, 0, 0)), pl.BlockSpec((tf, d), lambda i, j: (j, 0))]
        + [pl.BlockSpec((1, d), lambda i, j: (0, 0))] * chain,
        out_specs=[pl.BlockSpec((tm, tf), lambda i, j: (i, j)), pl.BlockSpec((tm, tf), lambda i, j: (i, j)), row] + [row] * chain,
        out_shape=[jax.ShapeDtypeStruct((s, nf * tf), BF16), jax.ShapeDtypeStruct((s, nf * tf), BF16),
                   jax.ShapeDtypeStruct((s, d), F32)] + [jax.ShapeDtypeStruct((s, d), BF16)] * chain,
        scratch_shapes=[pltpu.VMEM((tm, d), F32)],
        args=(h2, x1, w1_3, w2) + ((g_next.reshape(1, d),) if chain else ()), rider=rider)


def _loss_head(x, target, gf, name):
    s, d = x.shape
    tm = _tile(s, 512)
    n = s // tm

    def body(x_r, t_r, g_r, dx_o, loss_o, dg_o, lacc, gacc):
        i = pl.program_id(0)

        @pl.when(i == 0)
        def _():
            lacc[...] = jnp.zeros_like(lacc)
            gacc[...] = jnp.zeros_like(gacc)

        xv = x_r[...]
        r = lax.rsqrt(jnp.mean(xv * xv, axis=-1, keepdims=True) + EPS)
        xn = xv * r
        e = xn * g_r[...] - t_r[...]
        lacc[...] += _fold8(e * e)
        dy = e * (1.0 / d)
        gacc[...] += _fold8(dy * xn)
        dxn = dy * g_r[...]
        dx_o[...] = r * (dxn - xn * jnp.mean(dxn * xn, axis=-1, keepdims=True))

        @pl.when(i == n - 1)
        def _():
            loss_o[...] = jnp.sum(lacc[...]).reshape(1, 1) * (0.5 / d)
            dg_o[...] = jnp.sum(gacc[...], axis=0, keepdims=True)

    return pl.pallas_call(
        body, name=name, grid=(n,),
        in_specs=[pl.BlockSpec((tm, d), lambda i: (i, 0)), pl.BlockSpec((tm, d), lambda i: (i, 0)),
                  pl.BlockSpec((1, d), lambda i: (0, 0))],
        out_specs=[pl.BlockSpec((tm, d), lambda i: (i, 0)), pl.BlockSpec((1, 1), lambda i: (0, 0)),
                   pl.BlockSpec((1, d), lambda i: (0, 0))],
        out_shape=[jax.ShapeDtypeStruct((s, d), F32), jax.ShapeDtypeStruct((1, 1), F32),
                   jax.ShapeDtypeStruct((1, d), F32)],
        scratch_shapes=[pltpu.VMEM((8, d), F32), pltpu.VMEM((8, d), F32)],
        compiler_params=_params("arbitrary"),
    )(x, target, gf.reshape(1, d))


def _tn(a, b, nb, name, rider=None):
    s, m = a.shape
    n = b.shape[1]
    nw = n // nb
    tmm = _tile(m, 1024, 128)
    tn = nw if nw <= 1152 else nw // 2
    per = nw // tn
    ts = _tile(s, 2048)
    ns = s // ts

    def body(a_r, b_r, o_r, acc):
        k = pl.program_id(2)

        @pl.when(k == 0)
        def _():
            acc[...] = jnp.zeros_like(acc)

        acc[...] += lax.dot_general(a_r[...].astype(BF16), b_r[...].astype(BF16), (((0,), (0,)), ((), ())),
                                    preferred_element_type=F32)

        @pl.when(k == ns - 1)
        def _():
            o_r[...] = acc[...]

    (out,), carried = _hosted_call(
        body, name=name, grid=(m // tmm, nb * per, ns),
        in_specs=[pl.BlockSpec((ts, tmm), lambda i, j, k: (k, i)), pl.BlockSpec((ts, tn), lambda i, j, k: (k, j))],
        out_specs=[pl.BlockSpec((None, tmm, tn), lambda i, j, k: (j // per, i, j % per))],
        out_shape=[jax.ShapeDtypeStruct((nb, m, nw), F32)],
        scratch_shapes=[pltpu.VMEM((tmm, tn), F32)], args=(a, b), rider=rider)
    return (out, carried) if rider is not None else out


def _tn_groups(a, b, name):
    s, d = a.shape
    gc = d // N_GROUPS
    ts = _tile(s, 2048)
    ns = s // ts

    def body(a_r, b_r, o_r, acc):
        k = pl.program_id(1)

        @pl.when(k == 0)
        def _():
            acc[...] = jnp.zeros_like(acc)

        acc[...] += lax.dot_general(a_r[...], b_r[...], (((0,), (0,)), ((), ())), preferred_element_type=F32)

        @pl.when(k == ns - 1)
        def _():
            o_r[...] = acc[...]

    return pl.pallas_call(
        body, name=name, grid=(N_GROUPS, ns),
        in_specs=[pl.BlockSpec((ts, gc), lambda g, k: (k, g)), pl.BlockSpec((ts, gc), lambda g, k: (k, g))],
        out_specs=pl.BlockSpec((None, gc, gc), lambda g, k: (g, 0, 0)),
        out_shape=jax.ShapeDtypeStruct((N_GROUPS, gc, gc), F32),
        scratch_shapes=[pltpu.VMEM((gc, gc), F32)],
        compiler_params=_params("parallel", "arbitrary"),
    )(a, b)


def _mlp_bwd(dx2, f, x1, w2t, w1t, gm, name, rider=None):
    s, d = dx2.shape
    ff = f.shape[1]
    tf = _tile(ff, 1024, 128)
    nf = ff // tf
    tm = _tile(s, 1024)
    n = s // tm

    def body(dx2_r, f_r, x1_r, w2t_r, w1t_r, gm_r, df_o, dx1_o, dg_o, dxb, acc, gacc):
        i = pl.program_id(0)
        j = pl.program_id(1)

        @pl.when((i == 0) & (j == 0))
        def _():
            gacc[...] = jnp.zeros_like(gacc)

        @pl.when(j == 0)
        def _():
            dxb[...] = dx2_r[...].astype(BF16)
            acc[...] = jnp.zeros_like(acc)

        da = _dot(dxb[...], w2t_r[...])
        df = (da * (2.0 * jnp.maximum(f_r[...].astype(F32), 0.0))).astype(BF16)
        df_o[...] = df
        acc[...] += _dot(df, w1t_r[...])

        @pl.when(j == nf - 1)
        def _():
            dx, dg = _rms_bwd(acc[...], x1_r[...], gm_r[...])
            gacc[...] += dg
            dx1_o[...] = dx2_r[...] + dx

        @pl.when((i == n - 1) & (j == nf - 1))
        def _():
            dg_o[...] = jnp.sum(gacc[...], axis=0, keepdims=True)

    return _hosted_call(
        body, name=name, grid=(n, nf),
        in_specs=[pl.BlockSpec((tm, d), lambda i, j: (i, 0)), pl.BlockSpec((tm, tf), lambda i, j: (i, j)),
                  pl.BlockSpec((tm, d), lambda i, j: (i, 0)), pl.BlockSpec((d, tf), lambda i, j: (0, j)),
                  pl.BlockSpec((tf, d), lambda i, j: (j, 0)), pl.BlockSpec((1, d), lambda i, j: (0, 0))],
        out_specs=[pl.BlockSpec((tm, tf), lambda i, j: (i, j)), pl.BlockSpec((tm, d), lambda i, j: (i, 0)),
                   pl.BlockSpec((1, d), lambda i, j: (0, 0))],
        out_shape=[jax.ShapeDtypeStruct((s, ff), BF16), jax.ShapeDtypeStruct((s, d), F32),
                   jax.ShapeDtypeStruct((1, d), F32)],
        scratch_shapes=[pltpu.VMEM((tm, d), BF16), pltpu.VMEM((tm, d), F32), pltpu.VMEM((8, d), F32)],
        args=(dx2, f, x1, w2t, w1t, gm.reshape(1, d)), rider=rider)


def _branch_out_bwd(dx1, proj, ya, yb, yc, wot, woat, wobt, wpt, ps, name):
    s, d = dx1.shape
    gc = d // N_GROUPS
    tm = _tile(s, 512)
    n = s // tm

    def tile(col=0):
        return pl.BlockSpec((tm, d), lambda i: (i, col))

    def const(shape):
        return pl.BlockSpec(shape, lambda i: (0,) * len(shape))

    def body(dx1_r, g0, g1, g2, ya_r, yb_r, yc_r, wot_r, woat_r, wobt_r, wpt_r, ps_r,
             dpa_o, ds_o, dp_o, dpj_o, dya_o, dyb_o, dyc_o, dbob_o, dps_o, dbg_o, sacc, gacc):
        i = pl.program_id(0)

        @pl.when(i == 0)
        def _():
            sacc[...] = jnp.zeros_like(sacc)
            gacc[...] = jnp.zeros_like(gacc)

        dm = _dot(dx1_r[...].astype(BF16), wot_r[...])
        ycp = _f32(yc_r)
        ys = (_f32(ya_r), _f32(yb_r), ycp * ps_r[...])
        dys = []
        for b, g_r in enumerate((g0, g1, g2)):
            sg = _sigmoid(_f32(g_r))
            dgt = dm * ys[b] * sg * (1.0 - sg)
            dpj_o[:, b * d:(b + 1) * d] = dgt.astype(BF16)
            gacc[b] += _fold8(dgt)
            dys.append(dm * sg)
        dya, dyb, dyc = dys
        sacc[0] += _fold8(dyb)
        sacc[1] += _fold8(dyc * ycp)
        dyab = dya.astype(BF16)
        dybb = dyb.astype(BF16)
        dycb = (dyc * ps_r[...]).astype(BF16)
        dya_o[...] = dyab
        dyb_o[...] = dybb
        dyc_o[...] = dycb
        dpa_o[...] = _dot(dyab, woat_r[...]).astype(BF16)
        ds_o[...] = _dot(dybb, wobt_r[...]).astype(BF16)
        for g in range(N_GROUPS):
            cols = slice(g * gc, (g + 1) * gc)
            dp_o[:, cols] = _dot(dycb[:, cols], wpt_r[g]).astype(BF16)

        @pl.when(i == n - 1)
        def _():
            dbob_o[...] = jnp.sum(sacc[0], axis=0, keepdims=True)
            dps_o[...] = jnp.sum(sacc[1], axis=0, keepdims=True)
            for b in range(3):
                dbg_o[:, b * d:(b + 1) * d] = jnp.sum(gacc[b], axis=0, keepdims=True)

    return pl.pallas_call(
        body, name=name, grid=(n,),
        in_specs=[tile(), tile(6), tile(7), tile(8), tile(), tile(), tile(),
                  const((d, d)), const((d, d)), const((d, d)), const((N_GROUPS, gc, gc)), const((1, d))],
        out_specs=[tile(), tile(), tile(), pl.BlockSpec((tm, 3 * d), lambda i: (i, 2)), tile(), tile(), tile(),
                   const((1, d)), const((1, d)), const((1, 3 * d))],
        out_shape=[jax.ShapeDtypeStruct((s, d), BF16)] * 3 + [jax.ShapeDtypeStruct((s, 9 * d), BF16)]
        + [jax.ShapeDtypeStruct((s, d), BF16)] * 3
        + [jax.ShapeDtypeStruct((1, d), F32), jax.ShapeDtypeStruct((1, d), F32), jax.ShapeDtypeStruct((1, 3 * d), F32)],
        scratch_shapes=[pltpu.VMEM((2, 8, d), F32), pltpu.VMEM((3, 8, d), F32)],
        compiler_params=_params("arbitrary"),
    )(dx1, proj, proj, proj, ya, yb, yc, wot, woat, wobt, wpt, ps.reshape(1, d))


def _ln_silu_bwd(v, ds, lg, lb, name, rider=None):
    s, d = v.shape
    tm = _tile(s, 256)
    n = s // tm

    def tile():
        return pl.BlockSpec((tm, d), lambda i: (i, 0))

    def vec():
        return pl.BlockSpec((1, d), lambda i: (0, 0))

    def body(v_r, ds_r, lg_r, lb_r, dv_o, dlg_o, dlb_o, dcb_o, acc):
        i = pl.program_id(0)

        @pl.when(i == 0)
        def _():
            acc[...] = jnp.zeros_like(acc)

        for r0 in range(0, tm, SUB):
            rows = slice(r0, r0 + SUB)
            vv = v_r[rows, :]
            mu = jnp.mean(vv, axis=-1, keepdims=True)
            vc = vv - mu
            rstd = lax.rsqrt(jnp.mean(vc * vc, axis=-1, keepdims=True) + EPS)
            nrm = vc * rstd
            ln = nrm * lg_r[...] + lb_r[...]
            sg = _sigmoid(ln)
            dln = ds_r[rows, :].astype(F32) * (sg * (1.0 + ln * (1.0 - sg)))
            acc[0] += _fold8(dln * nrm)
            acc[1] += _fold8(dln)
            dn = dln * lg_r[...]
            dv = rstd * (dn - jnp.mean(dn, axis=-1, keepdims=True)
                         - nrm * jnp.mean(dn * nrm, axis=-1, keepdims=True))
            acc[2] += _fold8(dv)
            dv_o[rows, :] = dv

        @pl.when(i == n - 1)
        def _():
            dlg_o[...] = jnp.sum(acc[0], axis=0, keepdims=True)
            dlb_o[...] = jnp.sum(acc[1], axis=0, keepdims=True)
            dcb_o[...] = jnp.sum(acc[2], axis=0, keepdims=True)

    return _hosted_call(
        body, name=name, grid=(n,),
        in_specs=[tile(), tile(), vec(), vec()],
        out_specs=[tile(), vec(), vec(), vec()],
        out_shape=[jax.ShapeDtypeStruct((s, d), F32)] + [jax.ShapeDtypeStruct((1, d), F32)] * 3,
        scratch_shapes=[pltpu.VMEM((3, 8, d), F32)],
        args=(v, ds, lg.reshape(1, d), lb.reshape(1, d)), rider=rider)


def _mix_bwd(proj, dpa, dv, dp, dproj, conv_a, conv_b, name):
    s, d9 = proj.shape
    d = d9 // 9
    gc = d // N_GROUPS
    tm = _tile(s, 256, HALO)
    n = s // tm
    nh = s // HALO

    def tile(col=0):
        return pl.BlockSpec((tm, d), lambda i: (i, col))

    def vec(rows):
        return pl.BlockSpec((rows, d), lambda i: (0, 0))

    def body(ab, ac, ax, bv, bg, ac_h, ax_h, bv_h, bg_h, dpa_r, dv_r, dp_r, ab_n, dpa_n, dv_n, dp_n, wa, wb, _alias,
             dpj_o, dbin_o, dca_o, dcb_o, ua_e, ub_e, dz_e, dv_e, q_e, bacc, cacc_a, cacc_b, ub_s, dv_s):
        i = pl.program_id(0)

        @pl.when(i == 0)
        def _():
            bacc[...] = jnp.zeros_like(bacc)
            cacc_a[...] = jnp.zeros_like(cacc_a)
            cacc_b[...] = jnp.zeros_like(cacc_b)

        keep_p = (i > 0).astype(F32)
        keep_n = (i < n - 1).astype(F32)
        ua_e[0:HALO, :] = _f32(ac_h) * _f32(ax_h) * keep_p
        ua_e[HALO:HALO + tm, :] = _f32(ac) * _f32(ax)
        ub_e[0:HALO, :] = _f32(bv_h) * _sigmoid(_f32(bg_h)) * keep_p
        ub_e[HALO:HALO + tm, :] = _f32(bv) * _sigmoid(_f32(bg))
        dz_e[0:tm, :] = _f32(dpa_r) * _f32(ab)
        dz_e[tm:tm + HALO, :] = _f32(dpa_n) * _f32(ab_n) * keep_n
        dv_e[0:tm, :] = dv_r[...]
        dv_e[tm:tm + HALO, :] = dv_n[...] * keep_n
        _shift_copies(ub_e, ub_s)
        _shift_copies(dv_e, dv_s)
        for g, w in enumerate(POOL_WINDOWS):
            cols = slice(g * gc, (g + 1) * gc)
            t_idx = i * tm + lax.broadcasted_iota(jnp.int32, (tm + HALO, gc), 0)
            cnt = jnp.minimum(t_idx + 1, w).astype(F32)
            q_e[0:tm, cols] = dp_r[:, cols].astype(F32) / cnt[0:tm]
            q_e[tm:tm + HALO, cols] = dp_n[:, cols].astype(F32) * keep_n / cnt[tm:tm + HALO]
        for r0 in range(0, tm, SUB):
            rows = slice(r0, r0 + SUB)
            dz = dz_e[pl.ds(r0, SUB), :]
            z = None
            du = None
            for k in range(K_A):
                u_k = ua_e[pl.ds(HALO - (K_A - 1) + k + r0, SUB), :]
                t = wa[k:k + 1, :] * u_k
                z = t if z is None else z + t
                cacc_a[k] += _fold8(dz * u_k)
                t = wa[k:k + 1, :] * dz_e[pl.ds(r0 + (K_A - 1) - k, SUB), :]
                du = t if du is None else du + t
            outs = [dpa_r[rows, :].astype(F32) * z, du * ax[rows, :].astype(F32), du * ac[rows, :].astype(F32)]
            dvv = dv_e[pl.ds(r0, SUB), :]
            for k in range(K_B):
                cacc_b[k] += _fold8(dvv * _window(ub_e, ub_s, HALO - (K_B - 1) + k + r0, SUB))
            halves = []
            for h0 in range(0, SUB, SUB // 2):
                acc = None
                for k in range(K_B):
                    t = wb[k:k + 1, :] * _window(dv_e, dv_s, r0 + h0 + (K_B - 1) - k, SUB // 2)
                    acc = t if acc is None else acc + t
                halves.append(acc)
            du = jnp.concatenate(halves, axis=0)
            sg = _sigmoid(bg[rows, :].astype(F32))
            outs.append(du * sg)
            outs.append(du * bv[rows, :].astype(F32) * sg * (1.0 - sg))
            for b, o in enumerate(outs):
                dpj_o[rows, b * d:(b + 1) * d] = o.astype(BF16)
                bacc[b] += _fold8(o)
            for g, w in enumerate(POOL_WINDOWS):
                cols = slice(g * gc, (g + 1) * gc)
                acc = q_e[pl.ds(r0, SUB), cols]
                for j in range(1, w):
                    acc = acc + q_e[pl.ds(r0 + j, SUB), cols]
                o = acc - dp_r[rows, cols].astype(F32)
                dpj_o[rows, 5 * d + g * gc:5 * d + (g + 1) * gc] = o.astype(BF16)
                bacc[5, :, cols] += _fold8(o)

        @pl.when(i == n - 1)
        def _():
            for b in range(6):
                dbin_o[:, b * d:(b + 1) * d] = jnp.sum(bacc[b], axis=0, keepdims=True)
            dca_o[...] = jnp.sum(cacc_a[...], axis=1)
            dcb_o[...] = jnp.sum(cacc_b[...], axis=1)

    ka8 = 8
    kb8 = 32
    return pl.pallas_call(
        body, name=name, grid=(n,),
        in_specs=[tile(0), tile(1), tile(2), tile(3), tile(4),
                  _prev_halo(tm, d, 1), _prev_halo(tm, d, 2), _prev_halo(tm, d, 3), _prev_halo(tm, d, 4),
                  tile(), tile(), tile(),
                  _next_halo(tm, d, 0, nh), _next_halo(tm, d, 0, nh), _next_halo(tm, d, 0, nh), _next_halo(tm, d, 0, nh),
                  vec(K_A), vec(K_B), ANY],
        out_specs=[pl.BlockSpec((tm, 6 * d), lambda i: (i, 0)), pl.BlockSpec((1, 6 * d), lambda i: (0, 0)),
                   pl.BlockSpec((ka8, d), lambda i: (0, 0)), pl.BlockSpec((kb8, d), lambda i: (0, 0))],
        out_shape=[jax.ShapeDtypeStruct((s, 9 * d), BF16), jax.ShapeDtypeStruct((1, 6 * d), F32),
                   jax.ShapeDtypeStruct((ka8, d), F32), jax.ShapeDtypeStruct((kb8, d), F32)],
        scratch_shapes=[pltpu.VMEM((HALO + tm, d), F32), pltpu.VMEM((HALO + tm, d), F32),
                        pltpu.VMEM((tm + HALO, d), F32), pltpu.VMEM((tm + HALO, d), F32),
                        pltpu.VMEM((tm + HALO, d), F32),
                        pltpu.VMEM((6, 8, d), F32), pltpu.VMEM((ka8, 8, d), F32), pltpu.VMEM((kb8, 8, d), F32),
                        pltpu.VMEM((7, tm + HALO - 8, d), F32), pltpu.VMEM((7, tm + HALO - 8, d), F32)],
        input_output_aliases={18: 0},
        compiler_params=_params("arbitrary"),
    )(proj, proj, proj, proj, proj, proj, proj, proj, proj, dpa, dv, dp, proj, dpa, dv, dp, conv_a, conv_b, dproj)


def _dh_rms_bwd(dproj, wint, x0, dx1, g, name, rider=None):
    s, p = dproj.shape
    d = x0.shape[1]
    tm = _tile(s, 1024)
    tk = _tile(p, 2304, 128)
    n = s // tm
    nk = p // tk

    def body(dp_r, w_r, x_r, dx1_r, g_r, dx_o, dg_o, acc, gacc):
        i = pl.program_id(0)
        k = pl.program_id(1)

        @pl.when((i == 0) & (k == 0))
        def _():
            gacc[...] = jnp.zeros_like(gacc)

        @pl.when(k == 0)
        def _():
            acc[...] = jnp.zeros_like(acc)

        acc[...] += _dot(dp_r[...], w_r[...])

        @pl.when(k == nk - 1)
        def _():
            dx, dg = _rms_bwd(acc[...], x_r[...], g_r[...])
            gacc[...] += dg
            dx_o[...] = dx1_r[...] + dx

        @pl.when((i == n - 1) & (k == nk - 1))
        def _():
            dg_o[...] = jnp.sum(gacc[...], axis=0, keepdims=True)

    return _hosted_call(
        body, name=name, grid=(n, nk),
        in_specs=[pl.BlockSpec((tm, tk), lambda i, k: (i, k)), pl.BlockSpec((tk, d), lambda i, k: (k, 0)),
                  pl.BlockSpec((tm, d), lambda i, k: (i, 0)), pl.BlockSpec((tm, d), lambda i, k: (i, 0)),
                  pl.BlockSpec((1, d), lambda i, k: (0, 0))],
        out_specs=[pl.BlockSpec((tm, d), lambda i, k: (i, 0)), pl.BlockSpec((1, d), lambda i, k: (0, 0))],
        out_shape=[jax.ShapeDtypeStruct((s, d), F32), jax.ShapeDtypeStruct((1, d), F32)],
        scratch_shapes=[pltpu.VMEM((tm, d), F32), pltpu.VMEM((8, d), F32)],
        args=(dproj, wint, x0, dx1, g.reshape(1, d)), rider=rider)


EARLY = ("w_out_a", "w_out_b", "w_pool", "w_o", "w_mlp1", "w_mlp2")


def _layer_fwd(x0, l, g_mix, win3, b_in, weights, proj_rider=None, bo_rider=None, mlp_rider=None, h=None, g_next=None):
    if h is None:
        h = _rms_fwd(x0, g_mix, f"rms_fwd_{l}")
    proj, carried_proj = _proj(h, win3, b_in, f"proj_{l}", proj_rider)
    w = weights(carried_proj) if callable(weights) else weights
    pa, sb, p, v = _mix_fwd(proj, w["conv_a"], w["conv_b"], w["conv_b_bias"], w["ln_b_g"], w["ln_b_b"], f"mix_fwd_{l}")
    (ya, yb, yc, mg, x1, h2), carried_bo = _branch_out(pa, sb, p, proj, x0, w["woa"], w["wob"], w["wp"], w["wo"],
                                                       w["b_out_b"], w["pool_scale"], w["g_mlp"], f"branch_out_{l}", bo_rider)
    rider = mlp_rider(carried_bo) if callable(mlp_rider) else mlp_rider
    (f, a, x2, *h_next), carried_mlp = _mlp_fwd(h2, x1, w["w1_3"], w["w2"], g_next, f"mlp_fwd_{l}", rider)
    saved = dict(x0=x0, h=h, proj=proj, pa=pa, sb=sb, p=p, v=v, ya=ya, yb=yb, yc=yc, mg=mg, x1=x1, h2=h2, f=f, a=a)
    return x2, saved, w, carried_mlp, (h_next[0] if h_next else None)


def _layer_bwd_early(dx2, w, sv, l, mlp_rider=None, dw2_rider=None, swap_early=False):
    d = dx2.shape[1]
    (df, dx1, dgm), carried_mlp = _mlp_bwd(dx2, sv["f"], sv["x1"], w["w2t"], w["w1t"], w["g_mlp"], f"mlp_bwd_{l}",
                                           mlp_rider)
    if dw2_rider is None:
        dw2, carried_dw2 = _tn(sv["a"], dx2, 1, f"dw2_{l}"), []
    else:
        dw2, carried_dw2 = _tn(sv["a"], dx2, 1, f"dw2_{l}", dw2_rider)
    dw1 = _tn(sv["h2"], df, N_CHIPS, f"dw1_{l}")
    dpa, ds, dp, dproj, dya, dyb, dyc, dbob, dps, dbg = _branch_out_bwd(
        dx1, sv["proj"], sv["ya"], sv["yb"], sv["yc"], w["wot"], w["woat"], w["wobt"], w["wpt"], w["pool_scale"],
        f"branch_out_bwd_{l}")
    dwo = _tn(sv["mg"], dx1, 1, f"dwo_{l}")
    dwoa = _tn(sv["pa"], dya, 1, f"dwoa_{l}")
    dwob = _tn(sv["sb"], dyb, 1, f"dwob_{l}")
    dwp = _tn_groups(sv["p"], dyc, f"dwp_{l}")
    ff = dw2.shape[1]
    gc = d // N_GROUPS
    big = dict(
        w_out_a=dwoa.reshape(N_CHIPS, d // N_CHIPS, d),
        w_out_b=dwob.reshape(N_CHIPS, d // N_CHIPS, d),
        w_pool=dwp.reshape(N_GROUPS, N_CHIPS, gc // N_CHIPS, gc).transpose(1, 0, 2, 3).reshape(N_CHIPS, gc, gc),
        w_o=dwo.reshape(N_CHIPS, d // N_CHIPS, d),
        w_mlp1=dw1,
        w_mlp2=dw2.reshape(N_CHIPS, ff // N_CHIPS, d),
    )
    swap = _ride_swap([big[k] for k in EARLY]) if swap_early else None
    (dv, dlg, dlb, dcb), swapped = _ln_silu_bwd(sv["v"], ds, w["ln_b_g"], w["ln_b_b"], f"ln_silu_bwd_{l}", swap)
    dproj, dbm, dca, dcvb = _mix_bwd(sv["proj"], dpa, dv, dp, dproj, w["conv_a"], w["conv_b"], f"mix_bwd_{l}")
    small = dict(
        b_in=jnp.concatenate([dbm, dbg], axis=1).reshape(9, d), conv_a=dca[:K_A], conv_b=dcvb[:K_B],
        conv_b_bias=dcb, ln_b_g=dlg, ln_b_b=dlb, b_out_b=dbob, pool_scale=dps, g_mlp=dgm,
    )
    return dx1, dproj, big, small, carried_mlp, carried_dw2, swapped


def _layer_bwd_late(dx1, dproj, w, sv, l, dwin_rider=None, dh_rider=None):
    if dwin_rider is None:
        dwin, carried_dwin = _tn(sv["h"], dproj, N_CHIPS, f"dwin_{l}"), []
    else:
        dwin, carried_dwin = _tn(sv["h"], dproj, N_CHIPS, f"dwin_{l}", dwin_rider)
    rider = dh_rider(dwin) if callable(dh_rider) else dh_rider
    (dx0, dgmix), carried_dh = _dh_rms_bwd(dproj, w["wint"], sv["x0"], dx1, w["g_mix"], f"dh_{l}", rider)
    return dx0, dwin, dgmix, carried_dwin, carried_dh


def _place():
    x, y, c = lax.axis_index("x"), lax.axis_index("y"), lax.axis_index("c")
    chips = [(1 - x, y), (x, 1 - y), (1 - x, 1 - y)]
    return x, y, c, chips


def _place_shard(w, chip_arr, dtype, name):
    rows, cols = w.shape
    tr = _tile(rows, max(16, (1 << 19) // cols), 16)

    def body(k_ref, w_r, o_r):
        o_r[...] = w_r[...].astype(dtype)

    return pl.pallas_call(
        body, name=name,
        grid_spec=pltpu.PrefetchScalarGridSpec(
            num_scalar_prefetch=1, grid=(rows // tr,),
            in_specs=[pl.BlockSpec((tr, cols), lambda i, k_ref: (i, 0))],
            out_specs=pl.BlockSpec((None, tr, cols), lambda i, k_ref: (k_ref[0], i, 0))),
        out_shape=jax.ShapeDtypeStruct((N_CHIPS, rows, cols), dtype),
        compiler_params=_params("parallel"),
    )(chip_arr, w)


def _ride_rows(v):
    def copy(ins, outs, sems, m, src_dev):
        x, y, c = lax.axis_index("x"), lax.axis_index("y"), lax.axis_index("c")
        peer = (x ^ ((m >> 2) & 1), y ^ ((m >> 1) & 1), c ^ (m & 1))
        return pltpu.make_async_remote_copy(
            src_ref=ins[0], dst_ref=outs[0].at[src_dev], send_sem=sems[0].at[m - 1], recv_sem=sems[1].at[m - 1],
            device_id=peer, device_id_type=MESH)

    def me():
        return 4 * lax.axis_index("x") + 2 * lax.axis_index("y") + lax.axis_index("c")

    def start(ins, outs, sems):
        pltpu.make_async_copy(ins[0], outs[0].at[me()], sems[2]).start()
        for m in range(1, N_DEV):
            copy(ins, outs, sems, m, me()).start()

    def finish(ins, outs, sems):
        for m in range(1, N_DEV):
            copy(ins, outs, sems, m, me() ^ m).wait_recv()
        for m in range(1, N_DEV):
            copy(ins, outs, sems, m, me()).wait_send()
        pltpu.make_async_copy(ins[0], outs[0].at[me()], sems[2]).wait()

    return _Rider([v], [jax.ShapeDtypeStruct((N_DEV, *v.shape), v.dtype)], [],
                  [pltpu.SemaphoreType.DMA((N_DEV - 1,)), pltpu.SemaphoreType.DMA((N_DEV - 1,)), pltpu.SemaphoreType.DMA],
                  start, finish)


def _ride_swap(grads):
    n = len(grads)

    def copy(ins, outs, sems, a, k):
        x, y, c, _ = _place()
        h = ins[a].shape[1] // 2
        return pltpu.make_async_remote_copy(
            src_ref=ins[a].at[k, pl.ds(pl.multiple_of((1 - c) * h, 8), h)], dst_ref=outs[a].at[k],
            send_sem=sems[0].at[a * N_CHIPS + k], recv_sem=sems[1].at[a * N_CHIPS + k], device_id=(x, y, 1 - c),
            device_id_type=MESH)

    def start(ins, outs, sems):
        for a in range(n):
            for k in range(N_CHIPS):
                copy(ins, outs, sems, a, k).start()

    def finish(ins, outs, sems):
        for a in range(n):
            for k in range(N_CHIPS):
                copy(ins, outs, sems, a, k).wait()

    return _Rider(grads, [jax.ShapeDtypeStruct((N_CHIPS, g.shape[1] // 2, g.shape[2]), g.dtype) for g in grads], [],
                  [pltpu.SemaphoreType.DMA((N_CHIPS * n,))] * 2, start, finish)


def _add_halves(g, r, c_arr, name):
    nk, rows, cols = g.shape
    h = rows // 2
    tr = _tile(h, max(8, (1 << 19) // cols), 8)
    nblk = h // tr

    def body(c_ref, g_r, r_r, o_r):
        o_r[...] = (g_r[...] + r_r[...]).astype(BF16)

    return pl.pallas_call(
        body, name=name,
        grid_spec=pltpu.PrefetchScalarGridSpec(
            num_scalar_prefetch=1, grid=(nk, nblk),
            in_specs=[pl.BlockSpec((None, tr, cols), lambda k, i, c_ref: (k, c_ref[0] * nblk + i, 0)),
                      pl.BlockSpec((None, tr, cols), lambda k, i, c_ref: (k, i, 0))],
            out_specs=pl.BlockSpec((None, tr, cols), lambda k, i, c_ref: (k, i, 0))),
        out_shape=jax.ShapeDtypeStruct((nk, h, cols), BF16),
        compiler_params=_params("parallel", "parallel"),
    )(c_arr, g, r)


def _ride_exchange(parts):
    n = len(parts)

    def copy(ins, outs, sems, a, j):
        x, y, c, chips = _place()
        px, py = chips[j]
        return pltpu.make_async_remote_copy(
            src_ref=ins[a].at[2 * px + py], dst_ref=outs[a].at[j], send_sem=sems[0].at[a * 3 + j],
            recv_sem=sems[1].at[a * 3 + j], device_id=(px, py, c), device_id_type=MESH)

    def start(ins, outs, sems):
        for a in range(n):
            for j in range(3):
                copy(ins, outs, sems, a, j).start()

    def finish(ins, outs, sems):
        for a in range(n):
            for j in range(3):
                copy(ins, outs, sems, a, j).wait()

    return _Rider(parts, [jax.ShapeDtypeStruct((3, *p.shape[1:]), p.dtype) for p in parts], [],
                  [pltpu.SemaphoreType.DMA((3 * n,))] * 2, start, finish)


def _ride_gather(placed=(), landed=(), forward=True):
    placed, landed = list(placed), list(landed)
    n_ici = len(placed)
    fwd = (list(range(n_ici)) if forward else []) + list(range(n_ici, n_ici + len(landed)))

    def ici(ins, outs, sems, a, j, src_chip):
        x, y, c, chips = _place()
        h = ins[a].shape[1] // 2
        rows = pl.ds(pl.multiple_of(c * h, 8), h)
        return pltpu.make_async_remote_copy(
            src_ref=ins[a].at[2 * x + y, rows], dst_ref=outs[a].at[src_chip, rows], send_sem=sems[0].at[a * 3 + j],
            recv_sem=sems[1].at[a * 3 + j], device_id=(*chips[j], c), device_id_type=MESH)

    def d2d(ins, outs, sems, a, j, which):
        x, y, c, chips = _place()
        px, py = chips[j]
        h = ins[a].shape[1] // 2
        rows = pl.ds(pl.multiple_of(which * h, 8), h)
        return pltpu.make_async_remote_copy(
            src_ref=ins[a].at[2 * px + py, rows], dst_ref=outs[a].at[2 * px + py, rows], send_sem=sems[2].at[a * 3 + j],
            recv_sem=sems[3].at[a * 3 + j], device_id=(x, y, 1 - c), device_id_type=MESH)

    def start(ins, outs, sems):
        x, y, c, _ = _place()
        for a in range(n_ici):
            for j in range(3):
                ici(ins, outs, sems, a, j, 2 * x + y).start()
        for a in range(n_ici, n_ici + len(landed)):
            for j in range(3):
                d2d(ins, outs, sems, a, j, c).start()

    def land(ins, outs, sems, then_forward):
        _, _, c, chips = _place()
        for a in range(n_ici):
            for j, (px, py) in enumerate(chips):
                ici(ins, outs, sems, a, j, 2 * px + py).wait_recv()
                if then_forward:
                    d2d(ins, outs, sems, a, j, c).start()

    def middle(ins, outs, sems):
        land(ins, outs, sems, True)

    def finish(ins, outs, sems):
        x, y, c, _ = _place()
        if not forward:
            land(ins, outs, sems, False)
        for a in fwd:
            for j in range(3):
                d2d(ins, outs, sems, a, j, 1 - c).wait_recv()
        for a in range(n_ici):
            for j in range(3):
                ici(ins, outs, sems, a, j, 2 * x + y).wait_send()
        for a in fwd:
            for j in range(3):
                d2d(ins, outs, sems, a, j, c).wait_send()

    arrays = placed + landed
    n = len(arrays)
    return _Rider(arrays, [jax.ShapeDtypeStruct(p.shape, p.dtype) for p in arrays], [(a, a) for a in range(n)],
                  [pltpu.SemaphoreType.DMA((3 * n,))] * 4, start, finish, middle if (n_ici and forward) else None)


def _sum_chips(r, name):
    nk, rows, cols = r.shape
    tr = _tile(rows, max(8, (1 << 19) // cols), 8)

    def body(r_r, o_r):
        acc = r_r[0].astype(F32)
        for k in range(1, nk):
            acc = acc + r_r[k].astype(F32)
        o_r[...] = acc

    return pl.pallas_call(
        body, name=name, grid=(rows // tr,),
        in_specs=[pl.BlockSpec((nk, tr, cols), lambda i: (0, i, 0))],
        out_specs=pl.BlockSpec((tr, cols), lambda i: (i, 0)),
        out_shape=jax.ShapeDtypeStruct((rows, cols), F32),
        compiler_params=_params("parallel"),
    )(r)


def _sum_chips_into(own, arrived, dst, layer, n_layers, place_arr, name):
    _, h, cols = own.shape
    tr = _tile(h, max(8, (1 << 19) // cols), 8)
    nblk = h // tr

    def body(p_ref, own_r, arr_r, *rest):
        o_r = rest[-1]
        acc = own_r[...].astype(F32)
        for j in range(3):
            acc = acc + arr_r[j].astype(F32)
        o_r[...] = acc

    in_specs = [pl.BlockSpec((None, tr, cols), lambda i, p_ref: (p_ref[0], i, 0)),
                pl.BlockSpec((3, tr, cols), lambda i, p_ref: (0, i, 0))]
    args = [place_arr, own, arrived]
    aliases = {}
    if dst is not None:
        in_specs.append(ANY)
        args.append(dst)
        aliases = {3: 0}
    return pl.pallas_call(
        body, name=name,
        grid_spec=pltpu.PrefetchScalarGridSpec(
            num_scalar_prefetch=1, grid=(nblk,), in_specs=in_specs,
            out_specs=pl.BlockSpec((None, tr, cols), lambda i, p_ref: (layer, p_ref[1] * nblk + i, 0))),
        out_shape=jax.ShapeDtypeStruct((n_layers, 2 * h, cols), F32),
        input_output_aliases=aliases,
        compiler_params=_params("parallel"),
    )(*args)


def _ride_share(shards):
    n = len(shards)
    n_layers = shards[0].shape[0]

    def copy(ins, outs, sems, a, l, which):
        x, y, c, _ = _place()
        h = ins[a].shape[1] // 2
        rows = pl.ds(pl.multiple_of(which * h, 8), h)
        return pltpu.make_async_remote_copy(
            src_ref=ins[a].at[l, rows], dst_ref=outs[a].at[l, rows], send_sem=sems[0].at[a * n_layers + l],
            recv_sem=sems[1].at[a * n_layers + l], device_id=(x, y, 1 - c), device_id_type=MESH)

    def start(ins, outs, sems):
        c = lax.axis_index("c")
        for a in range(n):
            for l in range(n_layers):
                copy(ins, outs, sems, a, l, c).start()

    def finish(ins, outs, sems):
        c = lax.axis_index("c")
        for a in range(n):
            for l in range(n_layers):
                copy(ins, outs, sems, a, l, 1 - c).wait_recv()
        for a in range(n):
            for l in range(n_layers):
                copy(ins, outs, sems, a, l, c).wait_send()

    return _Rider(shards, [jax.ShapeDtypeStruct(a.shape, a.dtype) for a in shards], [(a, a) for a in range(n)],
                  [pltpu.SemaphoreType.DMA((n * n_layers,))] * 2, start, finish)


def _join_riders(first, second):
    k_in, k_out, k_sem = len(first.args), len(first.out_shape), len(first.sems)

    def start(ins, outs, sems):
        first.start(ins[:k_in], outs[:k_out], sems[:k_sem])
        second.start(ins[k_in:], outs[k_out:], sems[k_sem:])

    def finish(ins, outs, sems):
        first.finish(ins[:k_in], outs[:k_out], sems[:k_sem])
        second.finish(ins[k_in:], outs[k_out:], sems[k_sem:])

    return _Rider(first.args + second.args, first.out_shape + second.out_shape,
                  first.aliases + [(i + k_in, o + k_out) for i, o in second.aliases], first.sems + second.sems, start, finish)


def _adamw(w, g, m, v, name):
    rows, cols = w.shape
    tr = _tile(rows, max(8, (1 << 18) // cols), 8)
    c1 = 1.0 - ADAM_B1 ** ADAM_STEP
    c2 = 1.0 - ADAM_B2 ** ADAM_STEP

    def body(w_r, g_r, m_r, v_r, d_o, m_o, v_o):
        gv = g_r[...]
        mn = ADAM_B1 * m_r[...] + (1.0 - ADAM_B1) * gv
        vn = ADAM_B2 * v_r[...] + (1.0 - ADAM_B2) * (gv * gv)
        m_o[...] = mn
        v_o[...] = vn
        d_o[...] = -ADAM_LR * ((mn / c1) / (jnp.sqrt(vn / c2) + ADAM_EPS) + ADAM_WD * w_r[...])

    spec = pl.BlockSpec((tr, cols), lambda i: (i, 0))
    return pl.pallas_call(
        body, name=name, grid=(rows // tr,), in_specs=[spec] * 4, out_specs=[spec] * 3,
        out_shape=[jax.ShapeDtypeStruct((rows, cols), F32)] * 3,
        compiler_params=_params("parallel"),
    )(w, g, m, v)


BIG = ("w_in", "w_out_a", "w_out_b", "w_pool", "w_o", "w_mlp1", "w_mlp2")
SMALL = ("g_mix", "b_in", "conv_b_bias", "ln_b_g", "ln_b_b", "b_out_b", "pool_scale", "g_mlp")
CONVS = ("conv_a", "conv_b")
WEIGHTS = ("g_mix", "w_in", "b_in", "conv_a", "w_out_a", "conv_b", "conv_b_bias", "ln_b_g", "ln_b_b", "w_out_b", "b_out_b",
           "w_pool", "pool_scale", "w_o", "g_mlp", "w_mlp1", "w_mlp2", "g_final")


def _as2d(a):
    return a.reshape(-1, a.shape[-1])


def _pad_rows(a, rows):
    return jnp.pad(a, ((0, rows - a.shape[0]), (0, 0)))


def _full_weights(big_g, conv_g, rep, l, d):
    gc = d // N_GROUPS
    w = {k: v[l] for k, v in rep.items()}
    w["b_in"] = rep["b_in"][l].reshape(1, -1)
    win3 = big_g["w_in"]
    w["win3"] = win3
    w["wint"] = win3.transpose(0, 2, 1).reshape(-1, d)
    for src, dst in (("w_out_a", "woa"), ("w_out_b", "wob"), ("w_o", "wo")):
        full = big_g[src].reshape(d, d)
        w[dst] = full
        w[dst + "t"] = full.T
    wp = big_g["w_pool"].reshape(N_CHIPS, N_GROUPS, gc // N_CHIPS, gc).transpose(1, 0, 2, 3).reshape(N_GROUPS, gc, gc)
    w["wp"] = wp
    w["wpt"] = wp.transpose(0, 2, 1)
    w1_3 = big_g["w_mlp1"]
    w["w1_3"] = w1_3
    w["w1t"] = w1_3.transpose(0, 2, 1).reshape(-1, d)
    w2 = big_g["w_mlp2"].reshape(-1, d)
    w["w2"] = w2
    w["w2t"] = w2.T
    ca, cb = conv_g
    w["conv_a"] = ca
    w["conv_b"] = cb
    return w


def kernel(x, g_mix, w_in, b_in, conv_a, w_out_a, conv_b, conv_b_bias, ln_b_g, ln_b_b, w_out_b, b_out_b, w_pool, pool_scale, w_o, g_mlp, w_mlp1, w_mlp2, g_final, loss_target, m_g_mix, m_w_in, m_b_in, m_conv_a, m_w_out_a, m_conv_b, m_conv_b_bias, m_ln_b_g, m_ln_b_b, m_w_out_b, m_b_out_b, m_w_pool, m_pool_scale, m_w_o, m_g_mlp, m_w_mlp1, m_w_mlp2, m_g_final, v_g_mix, v_w_in, v_b_in, v_conv_a, v_w_out_a, v_conv_b, v_conv_b_bias, v_ln_b_g, v_ln_b_b, v_w_out_b, v_b_out_b, v_w_pool, v_pool_scale, v_w_o, v_g_mlp, v_w_mlp1, v_w_mlp2, v_g_final):
    given = dict(g_mix=g_mix, w_in=w_in, b_in=b_in, conv_a=conv_a, w_out_a=w_out_a, conv_b=conv_b, conv_b_bias=conv_b_bias, ln_b_g=ln_b_g, ln_b_b=ln_b_b, w_out_b=w_out_b, b_out_b=b_out_b, w_pool=w_pool, pool_scale=pool_scale, w_o=w_o, g_mlp=g_mlp, w_mlp1=w_mlp1, w_mlp2=w_mlp2, g_final=g_final)
    mom = dict(g_mix=m_g_mix, w_in=m_w_in, b_in=m_b_in, conv_a=m_conv_a, w_out_a=m_w_out_a, conv_b=m_conv_b, conv_b_bias=m_conv_b_bias, ln_b_g=m_ln_b_g, ln_b_b=m_ln_b_b, w_out_b=m_w_out_b, b_out_b=m_b_out_b, w_pool=m_w_pool, pool_scale=m_pool_scale, w_o=m_w_o, g_mlp=m_g_mlp, w_mlp1=m_w_mlp1, w_mlp2=m_w_mlp2, g_final=m_g_final)
    var = dict(g_mix=v_g_mix, w_in=v_w_in, b_in=v_b_in, conv_a=v_conv_a, w_out_a=v_w_out_a, conv_b=v_conv_b, conv_b_bias=v_conv_b_bias, ln_b_g=v_ln_b_g, ln_b_b=v_ln_b_b, w_out_b=v_w_out_b, b_out_b=v_b_out_b, w_pool=v_w_pool, pool_scale=v_pool_scale, w_o=v_w_o, g_mlp=v_g_mlp, w_mlp1=v_w_mlp1, w_mlp2=v_w_mlp2, g_final=v_g_final)
    n_layers = w_in.shape[0]
    s, d = x.shape[1], x.shape[2]
    dq = d // N_CHIPS
    x_idx, y_idx, c_idx = lax.axis_index("x"), lax.axis_index("y"), lax.axis_index("c")
    chip = 2 * x_idx + y_idx
    c_arr = c_idx.astype(jnp.int32).reshape(1)

    conv_rows = n_layers * (K_A + K_B)
    conv_pad = -(-conv_rows // 16) * 16
    conv_pack = _pad_rows(jnp.concatenate([conv_a[l] for l in range(n_layers)] + [conv_b[l] for l in range(n_layers)], axis=0),
                          conv_pad)
    chip_arr = chip.astype(jnp.int32).reshape(1)
    place_arr = jnp.stack([chip, c_idx]).astype(jnp.int32)

    def placed(k, l):
        return _place_shard(_as2d(given[k][l]), chip_arr, BF16, f"place_{k}_{l}")

    rest = BIG[1:]
    win_g, conv_all = _comm_call(_ride_gather([placed("w_in", 0), _place_shard(conv_pack, chip_arr, F32, "place_convs")]),
                                 "gather_w_in_0")
    conv_full = conv_all.transpose(1, 0, 2).reshape(conv_pad, d)
    conv_g = [(conv_full[l * K_A:(l + 1) * K_A], conv_full[n_layers * K_A + l * K_B:n_layers * K_A + (l + 1) * K_B])
              for l in range(n_layers)]
    rep = {k: given[k] for k in SMALL}

    xl = x[0]
    ws, saved = [], []
    rest_g = None
    h = None
    for l in range(n_layers):
        more = l + 1 < n_layers

        def weights(carried, l=l, win_g=win_g, rest_g=rest_g):
            return _full_weights(dict(zip(rest, carried if rest_g is None else rest_g), w_in=win_g), conv_g[l], rep, l, d)

        xl, sv, w, got, h = _layer_fwd(
            xl, l, given["g_mix"][l], win_g, given["b_in"][l].reshape(1, -1), weights,
            _ride_gather([placed(k, l) for k in rest]) if rest_g is None else None,
            _ride_gather([placed("w_in", l + 1)], forward=False) if more else None,
            (lambda landed, l=l: _ride_gather([placed(k, l + 1) for k in rest], landed)) if more else None,
            h, given["g_mix"][l + 1] if more else None)
        if more:
            rest_g, win_g = got[:len(rest)], got[len(rest)]
        ws.append(w)
        saved.append(sv)
    dx, loss, dgf = _loss_head(xl, loss_target[0], g_final, "loss_head")
    loss = lax.psum(loss[0, 0], ("x", "y", "c"))

    parts, arrived, small = {}, {}, [None] * n_layers
    pending = []
    for l in reversed(range(n_layers)):
        w, sv = ws[l], saved[l]
        first = [q for q in pending if q[1] not in ("w_mlp1", "w_mlp2")]
        second = [q for q in pending if q[1] in ("w_mlp1", "w_mlp2")]
        dx1, dproj, early, sm, got, got2, swapped = _layer_bwd_early(
            dx, w, sv, l, _ride_exchange([parts[q] for q in first]) if first else None,
            _ride_exchange([parts[q] for q in second]) if second else None, swap_early=(l == 0))
        arrived.update(zip(first + second, got + got2))
        if l > 0:
            order = ("w_in",) + EARLY
            dx, dwin, dgmix, _, swapped = _layer_bwd_late(
                dx1, dproj, w, sv, l, None, lambda dwin, early=early: _ride_swap([dwin] + [early[k] for k in EARLY]))
            full = dict(early, w_in=dwin)
            for k, r in zip(order, swapped):
                parts[(l, k)] = _add_halves(full[k], r, c_arr, f"grad_add_halves_{k}_{l}")
            pending = [(l, k) for k in order]
        else:
            for k, r in zip(EARLY, swapped):
                parts[(l, k)] = _add_halves(early[k], r, c_arr, f"grad_add_halves_{k}_{l}")

            def w_in_rider(dwin, l=l):
                (r,) = _comm_call(_ride_swap([dwin]), f"grad_swap_w_in_{l}")
                parts[(l, "w_in")] = _add_halves(dwin, r, c_arr, f"grad_add_halves_w_in_{l}")
                return _ride_exchange([parts[(l, "w_in")]])

            dx, dwin, dgmix, got_early, got_w_in = _layer_bwd_late(
                dx1, dproj, w, sv, l, _ride_exchange([parts[(l, k)] for k in EARLY]), w_in_rider)
            arrived.update(zip([(l, k) for k in EARLY], got_early))
            arrived[(l, "w_in")] = got_w_in[0]
            pending = []
        sm["g_mix"] = dgmix
        small[l] = sm
    reduced = []
    for k in BIG:
        dst = None
        for l in range(n_layers):
            dst = _sum_chips_into(parts[(l, k)], arrived[(l, k)], dst, l, n_layers, place_arr, f"grad_sum_chips_{k}_{l}")
        reduced.append(dst)

    small_rows = []
    for k in SMALL + CONVS:
        for l in range(n_layers):
            small_rows.append(small[l][k])
    small_rows.append(dgf)
    pack = jnp.concatenate(small_rows, axis=0)
    n_small = pack.shape[0]
    pack = _pad_rows(pack, -(-n_small // 8) * 8)
    *shared, small_all = _comm_call(_join_riders(_ride_share(reduced), _ride_rows(pack)), "grad_share_and_small")
    big_grad = dict(zip(BIG, shared))
    small_sum = _sum_chips(small_all, "sum_small_grads")

    out = {}
    for k in BIG:
        shape = given[k].shape
        g2 = big_grad[k].reshape(-1, shape[-1])
        dlt, nm, nv = _adamw(given[k].reshape(g2.shape), g2, mom[k].reshape(g2.shape), var[k].reshape(g2.shape), f"adamw_{k}")
        out[k] = tuple(a.reshape(shape) for a in (g2, dlt, nm, nv))

    def rows_of(k):
        return {"b_in": 9, "conv_a": K_A, "conv_b": K_B}.get(k, 1)

    def pack_rep(src):
        rows = [src[k].reshape(n_layers * rows_of(k), d) for k in SMALL] + [src["g_final"].reshape(1, d)]
        a = jnp.concatenate(rows, axis=0)
        return _pad_rows(a, -(-a.shape[0] // 8) * 8)

    def pack_conv(src):
        a = jnp.concatenate([src[k].reshape(n_layers * rows_of(k), dq) for k in CONVS], axis=0)
        return _pad_rows(a, -(-a.shape[0] // 8) * 8)

    offs = {}
    r = 0
    for k in SMALL + CONVS:
        offs[k] = r
        r += n_layers * rows_of(k)
    offs["g_final"] = r
    n_rep = sum(n_layers * rows_of(k) for k in SMALL)
    g_rep = jnp.concatenate([small_sum[:n_rep], small_sum[offs["g_final"]:offs["g_final"] + 1]], axis=0)
    g_rep = _pad_rows(g_rep, -(-g_rep.shape[0] // 8) * 8)
    g_conv_full = small_sum[offs["conv_a"]:offs["g_final"]]
    g_conv = lax.dynamic_slice_in_dim(g_conv_full, chip * dq, dq, axis=1)
    g_conv = _pad_rows(g_conv, -(-g_conv.shape[0] // 8) * 8)
    rep_res = (g_rep,) + tuple(_adamw(pack_rep(given), g_rep, pack_rep(mom), pack_rep(var), "adamw_small"))
    conv_res = (g_conv,) + tuple(_adamw(pack_conv(given), g_conv, pack_conv(mom), pack_conv(var), "adamw_conv"))
    r = 0
    for k in SMALL:
        nr = n_layers * rows_of(k)
        out[k] = tuple(a[r:r + nr].reshape(given[k].shape) for a in rep_res)
        r += nr
    out["g_final"] = tuple(a[r].reshape(given["g_final"].shape) for a in rep_res)
    r = 0
    for k in CONVS:
        nr = n_layers * rows_of(k)
        out[k] = tuple(a[r:r + nr].reshape(given[k].shape) for a in conv_res)
        r += nr

    res = [loss, dx.reshape(x.shape)]
    for i in range(4):
        res += [out[k][i] for k in WEIGHTS]
    return tuple(res)
```

```python
import functools

import jax
import jax.numpy as jnp
from jax import lax
from jax.experimental import pallas as pl
from jax.experimental.pallas import tpu as pltpu

F32 = jnp.float32
BF16 = jnp.bfloat16
EPS = 1e-6
HALO = 32
SUB = 32
K_A = 3
K_B = 31
POOL_WINDOWS = (2, 4, 8, 16)
N_GROUPS = len(POOL_WINDOWS)
N_CHIPS = 4
N_DEV = 8
ADAM_LR = 0.001
ADAM_B1 = 0.9
ADAM_B2 = 0.999
ADAM_EPS = 1e-08
ADAM_WD = 0.01
ADAM_STEP = 10
VMEM_LIMIT = 56 * 1024 * 1024
MESH = pl.DeviceIdType.MESH
ANY = pl.BlockSpec(memory_space=pl.ANY)


def _params(*sem):
    return pltpu.CompilerParams(dimension_semantics=sem, vmem_limit_bytes=VMEM_LIMIT)


def _tile(n, pref, mult=8):
    if n <= pref:
        return n
    t = (pref // mult) * mult
    while t > mult and n % t:
        t -= mult
    assert n % t == 0, (n, pref, mult)
    return t


def _dot(a, b):
    return jnp.dot(a, b, preferred_element_type=F32)


def _sigmoid(x):
    return 1.0 / (1.0 + jnp.exp(-x))


def _f32(ref):
    return ref[...].astype(F32)


def _fold8(v):
    r, d = v.shape
    return v.reshape(r // 8, 8, d).sum(axis=0)


def _rms_bwd(dh, xv, g):
    r = lax.rsqrt(jnp.mean(xv * xv, axis=-1, keepdims=True) + EPS)
    xn = xv * r
    dxn = dh * g
    dx = r * (dxn - xn * jnp.mean(dxn * xn, axis=-1, keepdims=True))
    return dx, _fold8(dh * xn)


class _Rider:
    def __init__(self, args, out_shape, aliases, sems, start, finish, middle=None):
        self.args, self.out_shape, self.aliases, self.sems = list(args), list(out_shape), list(aliases), list(sems)
        self.start, self.finish, self.middle = start, finish, middle


def _hosted_call(body, *, name, grid, in_specs, out_specs, out_shape, scratch_shapes, args, rider=None):
    n_in, n_out, n_scr = len(in_specs), len(out_shape), len(scratch_shapes)
    params = pltpu.CompilerParams(dimension_semantics=("arbitrary",) * len(grid), vmem_limit_bytes=VMEM_LIMIT)
    if rider is None:
        res = pl.pallas_call(body, name=name, grid=grid, in_specs=in_specs, out_specs=out_specs, out_shape=out_shape,
                             scratch_shapes=scratch_shapes, compiler_params=params)(*args)
        return list(res), []
    k_in, k_out = len(rider.args), len(rider.out_shape)

    def hosted(*refs):
        ins, refs = refs[:n_in], refs[n_in:]
        r_in, refs = refs[:k_in], refs[k_in:]
        outs, refs = refs[:n_out], refs[n_out:]
        r_out, refs = refs[:k_out], refs[k_out:]
        scr, r_sems = refs[:n_scr], refs[n_scr:]
        first = functools.reduce(lambda u, v: u & v, [pl.program_id(a) == 0 for a in range(len(grid))])
        last = functools.reduce(lambda u, v: u & v, [pl.program_id(a) == grid[a] - 1 for a in range(len(grid))])

        @pl.when(first)
        def _():
            rider.start(r_in, r_out, r_sems)

        if rider.middle is not None:
            step = functools.reduce(lambda u, a: u * grid[a] + pl.program_id(a), range(len(grid)), 0)

            @pl.when(step == (3 * functools.reduce(lambda u, v: u * v, grid)) // 4)
            def _():
                rider.middle(r_in, r_out, r_sems)

        body(*ins, *outs, *scr)

        @pl.when(last)
        def _():
            rider.finish(r_in, r_out, r_sems)

    res = pl.pallas_call(
        hosted, name=name, grid=grid, in_specs=list(in_specs) + [ANY] * k_in, out_specs=list(out_specs) + [ANY] * k_out,
        out_shape=list(out_shape) + rider.out_shape, scratch_shapes=list(scratch_shapes) + rider.sems,
        input_output_aliases={n_in + i: n_out + o for i, o in rider.aliases}, compiler_params=params,
    )(*args, *rider.args)
    return list(res[:n_out]), list(res[n_out:])


def _comm_call(rider, name):
    def body(*refs):
        k_in, k_out = len(rider.args), len(rider.out_shape)
        r_in, r_out, r_sems = refs[:k_in], refs[k_in:k_in + k_out], refs[k_in + k_out:]
        rider.start(r_in, r_out, r_sems)
        if rider.middle is not None:
            rider.middle(r_in, r_out, r_sems)
        rider.finish(r_in, r_out, r_sems)

    return list(pl.pallas_call(
        body, name=name, in_specs=[ANY] * len(rider.args), out_specs=[ANY] * len(rider.out_shape),
        out_shape=rider.out_shape, scratch_shapes=rider.sems, input_output_aliases=dict(rider.aliases),
    )(*rider.args))


def _rms_fwd(x, g, name, rider=None):
    s, d = x.shape
    tm = _tile(s, 512)

    def body(x_ref, g_ref, h_ref):
        xv = x_ref[...]
        r = lax.rsqrt(jnp.mean(xv * xv, axis=-1, keepdims=True) + EPS)
        h_ref[...] = (xv * r * g_ref[...]).astype(BF16)

    (h,), carried = _hosted_call(
        body, name=name, grid=(s // tm,),
        in_specs=[pl.BlockSpec((tm, d), lambda i: (i, 0)), pl.BlockSpec((1, d), lambda i: (0, 0))],
        out_specs=[pl.BlockSpec((tm, d), lambda i: (i, 0))],
        out_shape=[jax.ShapeDtypeStruct((s, d), BF16)], scratch_shapes=[], args=(x, g.reshape(1, d)), rider=rider)
    return h, carried


def _proj(h, w3, b, name, rider=None):
    s, k = h.shape
    nb, _, nw = w3.shape
    tm = _tile(s, 1024)
    tn = nw
    per = nw // tn

    def body(h_ref, w_ref, b_ref, o_ref):
        o_ref[...] = (_dot(h_ref[...], w_ref[...]) + b_ref[...]).astype(BF16)

    (proj,), carried = _hosted_call(
        body, name=name, grid=(s // tm, nb * per),
        in_specs=[
            pl.BlockSpec((tm, k), lambda i, j: (i, 0)),
            pl.BlockSpec((None, k, tn), lambda i, j: (j // per, 0, j % per)),
            pl.BlockSpec((1, tn), lambda i, j: (0, j)),
        ],
        out_specs=[pl.BlockSpec((tm, tn), lambda i, j: (i, j))],
        out_shape=[jax.ShapeDtypeStruct((s, nb * nw), BF16)],
        scratch_shapes=[], args=(h, w3, b), rider=rider)
    return proj, carried


def _shift_copies(src_e, dst_s):
    n_rows = dst_s.shape[1]
    for b in range(1, 8):
        dst_s[b - 1, :, :] = src_e[pl.ds(b, n_rows), :]


def _window(src_e, dst_s, off, rows):
    b = off % 8
    if b == 0:
        return src_e[pl.ds(off, rows), :]
    return dst_s[b - 1, pl.ds(off - b, rows), :]


def _prev_halo(tm, d, col):
    r = tm // HALO
    return pl.BlockSpec((HALO, d), lambda i: (jnp.maximum(i * r - 1, 0), col))


def _next_halo(tm, d, col, n_halo_blocks):
    r = tm // HALO
    return pl.BlockSpec((HALO, d), lambda i: (jnp.minimum((i + 1) * r, n_halo_blocks - 1), col))


def _mix_fwd(proj, conv_a, conv_b, cb, lg, lb, name):
    s, d9 = proj.shape
    d = d9 // 9
    gc = d // N_GROUPS
    tm = _tile(s, 256, HALO)

    def tile(col):
        return pl.BlockSpec((tm, d), lambda i: (i, col))

    def vec(rows):
        return pl.BlockSpec((rows, d), lambda i: (0, 0))

    def body(ab, ac, ax, bv, bg, ci, ac_h, ax_h, bv_h, bg_h, ci_h, wa, wb, cb_r, lg_r, lb_r,
             pa_o, s_o, p_o, v_o, ua_e, ub_e, ci_e, ub_s):
        i = pl.program_id(0)
        keep = (i > 0).astype(F32)
        ua_e[0:HALO, :] = _f32(ac_h) * _f32(ax_h) * keep
        ua_e[HALO:HALO + tm, :] = _f32(ac) * _f32(ax)
        ub_e[0:HALO, :] = _f32(bv_h) * _sigmoid(_f32(bg_h)) * keep
        ub_e[HALO:HALO + tm, :] = _f32(bv) * _sigmoid(_f32(bg))
        ci_e[0:HALO, :] = _f32(ci_h) * keep
        ci_e[HALO:HALO + tm, :] = _f32(ci)
        _shift_copies(ub_e, ub_s)
        for r0 in range(0, tm, SUB):
            rows = slice(r0, r0 + SUB)
            z = None
            for k in range(K_A):
                t = wa[k:k + 1, :] * ua_e[pl.ds(HALO - (K_A - 1) + k + r0, SUB), :]
                z = t if z is None else z + t
            pa_o[rows, :] = (ab[rows, :].astype(F32) * z).astype(BF16)
            v = None
            for k in range(K_B):
                t = wb[k:k + 1, :] * _window(ub_e, ub_s, HALO - (K_B - 1) + k + r0, SUB)
                v = t if v is None else v + t
            v = v + cb_r[...]
            v_o[rows, :] = v
            mu = jnp.mean(v, axis=-1, keepdims=True)
            vc = v - mu
            rstd = lax.rsqrt(jnp.mean(vc * vc, axis=-1, keepdims=True) + EPS)
            ln = vc * rstd * lg_r[...] + lb_r[...]
            s_o[rows, :] = (ln * _sigmoid(ln)).astype(BF16)
            t_idx = i * tm + r0 + lax.broadcasted_iota(jnp.int32, (SUB, gc), 0)
            for g, w in enumerate(POOL_WINDOWS):
                cols = slice(g * gc, (g + 1) * gc)
                cur = ci_e[pl.ds(HALO + r0, SUB), cols]
                acc = cur
                for j in range(1, w):
                    acc = acc + ci_e[pl.ds(HALO + r0 - j, SUB), cols]
                cnt = jnp.minimum(t_idx + 1, w).astype(F32)
                p_o[rows, cols] = (acc / cnt - cur).astype(BF16)

    return pl.pallas_call(
        body, name=name, grid=(s // tm,),
        in_specs=[tile(0), tile(1), tile(2), tile(3), tile(4), tile(5),
                  _prev_halo(tm, d, 1), _prev_halo(tm, d, 2), _prev_halo(tm, d, 3), _prev_halo(tm, d, 4),
                  _prev_halo(tm, d, 5),
                  vec(K_A), vec(K_B), vec(1), vec(1), vec(1)],
        out_specs=[pl.BlockSpec((tm, d), lambda i: (i, 0))] * 4,
        out_shape=[jax.ShapeDtypeStruct((s, d), BF16)] * 3 + [jax.ShapeDtypeStruct((s, d), F32)],
        scratch_shapes=[pltpu.VMEM((HALO + tm, d), F32)] * 3 + [pltpu.VMEM((7, tm + HALO - 8, d), F32)],
        compiler_params=_params("parallel"),
    )(proj, proj, proj, proj, proj, proj, proj, proj, proj, proj, proj,
      conv_a, conv_b, cb.reshape(1, d), lg.reshape(1, d), lb.reshape(1, d))


def _branch_out(pa, sb, p, proj, x0, woa, wob, wp, wo, bob, ps, gm, name, rider=None):
    s, d = pa.shape
    gc = d // N_GROUPS
    tm = _tile(s, 512)

    def tile(col=0):
        return pl.BlockSpec((tm, d), lambda i: (i, col))

    def const(shape):
        return pl.BlockSpec(shape, lambda i: (0,) * len(shape))

    def body(pa_r, s_r, p_r, g0, g1, g2, x0_r, woa_r, wob_r, wp_r, wo_r, bob_r, ps_r, gm_r,
             ya_o, yb_o, yc_o, mg_o, x1_o, h2_o, yc_s):
        ya = _dot(pa_r[...], woa_r[...])
        yb = _dot(s_r[...], wob_r[...]) + bob_r[...]
        for g in range(N_GROUPS):
            cols = slice(g * gc, (g + 1) * gc)
            yc_s[:, cols] = _dot(p_r[:, cols], wp_r[g])
        yc = yc_s[...]
        ya_o[...] = ya.astype(BF16)
        yb_o[...] = yb.astype(BF16)
        yc_o[...] = yc.astype(BF16)
        m = _sigmoid(_f32(g0)) * ya + _sigmoid(_f32(g1)) * yb + _sigmoid(_f32(g2)) * (yc * ps_r[...])
        mb = m.astype(BF16)
        mg_o[...] = mb
        x1 = x0_r[...] + _dot(mb, wo_r[...])
        x1_o[...] = x1
        r = lax.rsqrt(jnp.mean(x1 * x1, axis=-1, keepdims=True) + EPS)
        h2_o[...] = (x1 * r * gm_r[...]).astype(BF16)

    return _hosted_call(
        body, name=name, grid=(s // tm,),
        in_specs=[tile(), tile(), tile(), tile(6), tile(7), tile(8), tile(),
                  const((d, d)), const((d, d)), const((N_GROUPS, gc, gc)), const((d, d)),
                  const((1, d)), const((1, d)), const((1, d))],
        out_specs=[tile()] * 6,
        out_shape=[jax.ShapeDtypeStruct((s, d), BF16)] * 4 + [jax.ShapeDtypeStruct((s, d), F32),
                                                               jax.ShapeDtypeStruct((s, d), BF16)],
        scratch_shapes=[pltpu.VMEM((tm, d), F32)],
        args=(pa, sb, p, proj, proj, proj, x0, woa, wob, wp, wo, bob.reshape(1, d), ps.reshape(1, d), gm.reshape(1, d)),
        rider=rider)


def _mlp_fwd(h2, x1, w1_3, w2, g_next, name, rider=None):
    s, d = h2.shape
    nf, _, tf = w1_3.shape
    tm = _tile(s, 1024)

    chain = g_next is not None

    def body(h_r, x1_r, w1_r, w2_r, *rest):
        gn_r, f_o, a_o, x2_o, hn_o, acc = rest if chain else (None, *rest[:3], None, rest[3])
        j = pl.program_id(1)

        @pl.when(j == 0)
        def _():
            acc[...] = jnp.zeros_like(acc)

        f = _dot(h_r[...], w1_r[...])
        f_o[...] = f.astype(BF16)
        rl = jnp.maximum(f, 0.0)
        a = (rl * rl).astype(BF16)
        a_o[...] = a
        acc[...] += _dot(a, w2_r[...])

        @pl.when(j == nf - 1)
        def _():
            x2 = x1_r[...] + acc[...]
            x2_o[...] = x2
            if chain:
                r = lax.rsqrt(jnp.mean(x2 * x2, axis=-1, keepdims=True) + EPS)
                hn_o[...] = (x2 * r * gn_r[...]).astype(BF16)

    row = pl.BlockSpec((tm, d), lambda i, j: (i, 0))
    return _hosted_call(
        body, name=name, grid=(s // tm, nf),
        in_specs=[row, row, pl.BlockSpec((None, d, tf), lambda i, j: (j, 0, 0)), pl.BlockSpec((tf, d), lambda i, j: (j, 0))]
        + [pl.BlockSpec((1, d), lambda i, j: (0, 0))] * chain,
        out_specs=[pl.BlockSpec((tm, tf), lambda i, j: (i, j)), pl.BlockSpec((tm, tf), lambda i, j: (i, j)), row] + [row] * chain,
        out_shape=[jax.ShapeDtypeStruct((s, nf * tf), BF16), jax.ShapeDtypeStruct((s, nf * tf), BF16),
                   jax.ShapeDtypeStruct((s, d), F32)] + [jax.ShapeDtypeStruct((s, d), BF16)] * chain,
        scratch_shapes=[pltpu.VMEM((tm, d), F32)],
        args=(h2, x1, w1_3, w2) + ((g_next.reshape(1, d),) if chain else ()), rider=rider)


def _loss_head(x, target, gf, name):
    s, d = x.shape
    tm = _tile(s, 512)
    n = s // tm

    def body(x_r, t_r, g_r, dx_o, loss_o, dg_o, lacc, gacc):
        i = pl.program_id(0)

        @pl.when(i == 0)
        def _():
            lacc[...] = jnp.zeros_like(lacc)
            gacc[...] = jnp.zeros_like(gacc)

        xv = x_r[...]
        r = lax.rsqrt(jnp.mean(xv * xv, axis=-1, keepdims=True) + EPS)
        xn = xv * r
        e = xn * g_r[...] - t_r[...]
        lacc[...] += _fold8(e * e)
        dy = e * (1.0 / d)
        gacc[...] += _fold8(dy * xn)
        dxn = dy * g_r[...]
        dx_o[...] = r * (dxn - xn * jnp.mean(dxn * xn, axis=-1, keepdims=True))

        @pl.when(i == n - 1)
        def _():
            loss_o[...] = jnp.sum(lacc[...]).reshape(1, 1) * (0.5 / d)
            dg_o[...] = jnp.sum(gacc[...], axis=0, keepdims=True)

    return pl.pallas_call(
        body, name=name, grid=(n,),
        in_specs=[pl.BlockSpec((tm, d), lambda i: (i, 0)), pl.BlockSpec((tm, d), lambda i: (i, 0)),
                  pl.BlockSpec((1, d), lambda i: (0, 0))],
        out_specs=[pl.BlockSpec((tm, d), lambda i: (i, 0)), pl.BlockSpec((1, 1), lambda i: (0, 0)),
                   pl.BlockSpec((1, d), lambda i: (0, 0))],
        out_shape=[jax.ShapeDtypeStruct((s, d), F32), jax.ShapeDtypeStruct((1, 1), F32),
                   jax.ShapeDtypeStruct((1, d), F32)],
        scratch_shapes=[pltpu.VMEM((8, d), F32), pltpu.VMEM((8, d), F32)],
        compiler_params=_params("arbitrary"),
    )(x, target, gf.reshape(1, d))


def _tn(a, b, nb, name, rider=None):
    s, m = a.shape
    n = b.shape[1]
    nw = n // nb
    tmm = _tile(m, 1024, 128)
    tn = nw if nw <= 1152 else nw // 2
    per = nw // tn
    ts = _tile(s, 2048)
    ns = s // ts

    def body(a_r, b_r, o_r, acc):
        k = pl.program_id(2)

        @pl.when(k == 0)
        def _():
            acc[...] = jnp.zeros_like(acc)

        acc[...] += lax.dot_general(a_r[...].astype(BF16), b_r[...].astype(BF16), (((0,), (0,)), ((), ())),
                                    preferred_element_type=F32)

        @pl.when(k == ns - 1)
        def _():
            o_r[...] = acc[...]

    (out,), carried = _hosted_call(
        body, name=name, grid=(m // tmm, nb * per, ns),
        in_specs=[pl.BlockSpec((ts, tmm), lambda i, j, k: (k, i)), pl.BlockSpec((ts, tn), lambda i, j, k: (k, j))],
        out_specs=[pl.BlockSpec((None, tmm, tn), lambda i, j, k: (j // per, i, j % per))],
        out_shape=[jax.ShapeDtypeStruct((nb, m, nw), F32)],
        scratch_shapes=[pltpu.VMEM((tmm, tn), F32)], args=(a, b), rider=rider)
    return (out, carried) if rider is not None else out


def _mlp_bwd(dx2, f, x1, w2t, w1t, gm, name, rider=None):
    s, d = dx2.shape
    ff = f.shape[1]
    tf = _tile(ff, 1024, 128)
    nf = ff // tf
    tm = _tile(s, 1024)
    n = s // tm

    def body(dx2_r, f_r, x1_r, w2t_r, w1t_r, gm_r, df_o, dx1_o, dg_o, dxb, acc, gacc):
        i = pl.program_id(0)
        j = pl.program_id(1)

        @pl.when((i == 0) & (j == 0))
        def _():
            gacc[...] = jnp.zeros_like(gacc)

        @pl.when(j == 0)
        def _():
            dxb[...] = dx2_r[...].astype(BF16)
            acc[...] = jnp.zeros_like(acc)

        da = _dot(dxb[...], w2t_r[...])
        df = (da * (2.0 * jnp.maximum(f_r[...].astype(F32), 0.0))).astype(BF16)
        df_o[...] = df
        acc[...] += _dot(df, w1t_r[...])

        @pl.when(j == nf - 1)
        def _():
            dx, dg = _rms_bwd(acc[...], x1_r[...], gm_r[...])
            gacc[...] += dg
            dx1_o[...] = dx2_r[...] + dx

        @pl.when((i == n - 1) & (j == nf - 1))
        def _():
            dg_o[...] = jnp.sum(gacc[...], axis=0, keepdims=True)

    return _hosted_call(
        body, name=name, grid=(n, nf),
        in_specs=[pl.BlockSpec((tm, d), lambda i, j: (i, 0)), pl.BlockSpec((tm, tf), lambda i, j: (i, j)),
                  pl.BlockSpec((tm, d), lambda i, j: (i, 0)), pl.BlockSpec((d, tf), lambda i, j: (0, j)),
                  pl.BlockSpec((tf, d), lambda i, j: (j, 0)), pl.BlockSpec((1, d), lambda i, j: (0, 0))],
        out_specs=[pl.BlockSpec((tm, tf), lambda i, j: (i, j)), pl.BlockSpec((tm, d), lambda i, j: (i, 0)),
                   pl.BlockSpec((1, d), lambda i, j: (0, 0))],
        out_shape=[jax.ShapeDtypeStruct((s, ff), BF16), jax.ShapeDtypeStruct((s, d), F32),
                   jax.ShapeDtypeStruct((1, d), F32)],
        scratch_shapes=[pltpu.VMEM((tm, d), BF16), pltpu.VMEM((tm, d), F32), pltpu.VMEM((8, d), F32)],
        args=(dx2, f, x1, w2t, w1t, gm.reshape(1, d)), rider=rider)


def _branch_out_bwd(dx1, proj, ya, yb, yc, pa, sb, p, wot, woat, wobt, wpt, ps, name):
    s, d = dx1.shape
    gc = d // N_GROUPS
    tm = _tile(s, 512)
    n = s // tm

    def tile(col=0):
        return pl.BlockSpec((tm, d), lambda i: (i, col))

    def const(shape):
        return pl.BlockSpec(shape, lambda i: (0,) * len(shape))

    def once(shape):
        return pl.BlockSpec(shape, lambda i: (0,) * len(shape), pipeline_mode=pl.Buffered(1))

    def tn(a, b):
        return lax.dot_general(a, b, (((0,), (0,)), ((), ())), preferred_element_type=F32)

    def body(dx1_r, g0, g1, g2, ya_r, yb_r, yc_r, pa_r, sb_r, p_r, wot_r, woat_r, wobt_r, wpt_r, ps_r,
             dpa_o, ds_o, dp_o, dpj_o, dbob_o, dps_o, dbg_o, dwoa_o, dwob_o, dwp_o, sacc, gacc, wa_acc, wb_acc, wp_acc, sem):
        i = pl.program_id(0)

        @pl.when(i == 0)
        def _():
            sacc[...] = jnp.zeros_like(sacc)
            gacc[...] = jnp.zeros_like(gacc)
            wa_acc[...] = jnp.zeros_like(wa_acc)
            wb_acc[...] = jnp.zeros_like(wb_acc)
            wp_acc[...] = jnp.zeros_like(wp_acc)

        dm = _dot(dx1_r[...].astype(BF16), wot_r[...])
        ycp = _f32(yc_r)
        ys = (_f32(ya_r), _f32(yb_r), ycp * ps_r[...])
        dys = []
        for b, g_r in enumerate((g0, g1, g2)):
            sg = _sigmoid(_f32(g_r))
            dgt = dm * ys[b] * sg * (1.0 - sg)
            dpj_o[:, b * d:(b + 1) * d] = dgt.astype(BF16)
            gacc[b] += _fold8(dgt)
            dys.append(dm * sg)
        dya, dyb, dyc = dys
        sacc[0] += _fold8(dyb)
        sacc[1] += _fold8(dyc * ycp)
        dyab = dya.astype(BF16)
        dybb = dyb.astype(BF16)
        dycb = (dyc * ps_r[...]).astype(BF16)
        wa_acc[...] += tn(pa_r[...], dyab)
        wb_acc[...] += tn(sb_r[...], dybb)
        dpa_o[...] = _dot(dyab, woat_r[...]).astype(BF16)
        ds_o[...] = _dot(dybb, wobt_r[...]).astype(BF16)
        for g in range(N_GROUPS):
            cols = slice(g * gc, (g + 1) * gc)
            wp_acc[g] += tn(p_r[:, cols], dycb[:, cols])
            dp_o[:, cols] = _dot(dycb[:, cols], wpt_r[g]).astype(BF16)

        @pl.when(i == n - 1)
        def _():
            dbob_o[...] = jnp.sum(sacc[0], axis=0, keepdims=True)
            dps_o[...] = jnp.sum(sacc[1], axis=0, keepdims=True)
            for b in range(3):
                dbg_o[:, b * d:(b + 1) * d] = jnp.sum(gacc[b], axis=0, keepdims=True)
            copies = [pltpu.make_async_copy(src, dst, sem.at[j])
                      for j, (src, dst) in enumerate(((wa_acc, dwoa_o), (wb_acc, dwob_o), (wp_acc, dwp_o)))]
            for cp in copies:
                cp.start()
            for cp in copies:
                cp.wait()

    return pl.pallas_call(
        body, name=name, grid=(n,),
        in_specs=[tile(), tile(6), tile(7), tile(8), tile(), tile(), tile(), tile(), tile(), tile(),
                  once((d, d)), once((d, d)), once((d, d)), once((N_GROUPS, gc, gc)), const((1, d))],
        out_specs=[tile(), tile(), tile(), pl.BlockSpec((tm, 3 * d), lambda i: (i, 2)),
                   const((1, d)), const((1, d)), const((1, 3 * d)), ANY, ANY, ANY],
        out_shape=[jax.ShapeDtypeStruct((s, d), BF16)] * 3 + [jax.ShapeDtypeStruct((s, 9 * d), BF16)]
        + [jax.ShapeDtypeStruct((1, d), F32), jax.ShapeDtypeStruct((1, d), F32), jax.ShapeDtypeStruct((1, 3 * d), F32)]
        + [jax.ShapeDtypeStruct((d, d), F32), jax.ShapeDtypeStruct((d, d), F32), jax.ShapeDtypeStruct((N_GROUPS, gc, gc), F32)],
        scratch_shapes=[pltpu.VMEM((2, 8, d), F32), pltpu.VMEM((3, 8, d), F32), pltpu.VMEM((d, d), F32),
                        pltpu.VMEM((d, d), F32), pltpu.VMEM((N_GROUPS, gc, gc), F32), pltpu.SemaphoreType.DMA((3,))],
        compiler_params=_params("arbitrary"),
    )(dx1, proj, proj, proj, ya, yb, yc, pa, sb, p, wot, woat, wobt, wpt, ps.reshape(1, d))


def _ln_silu_bwd(v, ds, lg, lb, name, rider=None):
    s, d = v.shape
    tm = _tile(s, 256)
    n = s // tm

    def tile():
        return pl.BlockSpec((tm, d), lambda i: (i, 0))

    def vec():
        return pl.BlockSpec((1, d), lambda i: (0, 0))

    def body(v_r, ds_r, lg_r, lb_r, dv_o, dlg_o, dlb_o, dcb_o, acc):
        i = pl.program_id(0)

        @pl.when(i == 0)
        def _():
            acc[...] = jnp.zeros_like(acc)

        for r0 in range(0, tm, SUB):
            rows = slice(r0, r0 + SUB)
            vv = v_r[rows, :]
            mu = jnp.mean(vv, axis=-1, keepdims=True)
            vc = vv - mu
            rstd = lax.rsqrt(jnp.mean(vc * vc, axis=-1, keepdims=True) + EPS)
            nrm = vc * rstd
            ln = nrm * lg_r[...] + lb_r[...]
            sg = _sigmoid(ln)
            dln = ds_r[rows, :].astype(F32) * (sg * (1.0 + ln * (1.0 - sg)))
            acc[0] += _fold8(dln * nrm)
            acc[1] += _fold8(dln)
            dn = dln * lg_r[...]
            dv = rstd * (dn - jnp.mean(dn, axis=-1, keepdims=True)
                         - nrm * jnp.mean(dn * nrm, axis=-1, keepdims=True))
            acc[2] += _fold8(dv)
            dv_o[rows, :] = dv

        @pl.when(i == n - 1)
        def _():
            dlg_o[...] = jnp.sum(acc[0], axis=0, keepdims=True)
            dlb_o[...] = jnp.sum(acc[1], axis=0, keepdims=True)
            dcb_o[...] = jnp.sum(acc[2], axis=0, keepdims=True)

    return _hosted_call(
        body, name=name, grid=(n,),
        in_specs=[tile(), tile(), vec(), vec()],
        out_specs=[tile(), vec(), vec(), vec()],
        out_shape=[jax.ShapeDtypeStruct((s, d), F32)] + [jax.ShapeDtypeStruct((1, d), F32)] * 3,
        scratch_shapes=[pltpu.VMEM((3, 8, d), F32)],
        args=(v, ds, lg.reshape(1, d), lb.reshape(1, d)), rider=rider)


def _mix_bwd(proj, dpa, dv, dp, dproj, conv_a, conv_b, name):
    s, d9 = proj.shape
    d = d9 // 9
    gc = d // N_GROUPS
    tm = _tile(s, 256, HALO)
    n = s // tm
    nh = s // HALO

    def tile(col=0):
        return pl.BlockSpec((tm, d), lambda i: (i, col))

    def vec(rows):
        return pl.BlockSpec((rows, d), lambda i: (0, 0))

    def body(ab, ac, ax, bv, bg, ac_h, ax_h, bv_h, bg_h, dpa_r, dv_r, dp_r, ab_n, dpa_n, dv_n, dp_n, wa, wb, _alias,
             dpj_o, dbin_o, dca_o, dcb_o, ua_e, ub_e, dz_e, dv_e, q_e, bacc, cacc_a, cacc_b, ub_s, dv_s):
        i = pl.program_id(0)

        @pl.when(i == 0)
        def _():
            bacc[...] = jnp.zeros_like(bacc)
            cacc_a[...] = jnp.zeros_like(cacc_a)
            cacc_b[...] = jnp.zeros_like(cacc_b)

        keep_p = (i > 0).astype(F32)
        keep_n = (i < n - 1).astype(F32)
        ua_e[0:HALO, :] = _f32(ac_h) * _f32(ax_h) * keep_p
        ua_e[HALO:HALO + tm, :] = _f32(ac) * _f32(ax)
        ub_e[0:HALO, :] = _f32(bv_h) * _sigmoid(_f32(bg_h)) * keep_p
        ub_e[HALO:HALO + tm, :] = _f32(bv) * _sigmoid(_f32(bg))
        dz_e[0:tm, :] = _f32(dpa_r) * _f32(ab)
        dz_e[tm:tm + HALO, :] = _f32(dpa_n) * _f32(ab_n) * keep_n
        dv_e[0:tm, :] = dv_r[...]
        dv_e[tm:tm + HALO, :] = dv_n[...] * keep_n
        _shift_copies(ub_e, ub_s)
        _shift_copies(dv_e, dv_s)
        for g, w in enumerate(POOL_WINDOWS):
            cols = slice(g * gc, (g + 1) * gc)
            t_idx = i * tm + lax.broadcasted_iota(jnp.int32, (tm + HALO, gc), 0)
            cnt = jnp.minimum(t_idx + 1, w).astype(F32)
            q_e[0:tm, cols] = dp_r[:, cols].astype(F32) / cnt[0:tm]
            q_e[tm:tm + HALO, cols] = dp_n[:, cols].astype(F32) * keep_n / cnt[tm:tm + HALO]
        for r0 in range(0, tm, SUB):
            rows = slice(r0, r0 + SUB)
            dz = dz_e[pl.ds(r0, SUB), :]
            z = None
            du = None
            for k in range(K_A):
                u_k = ua_e[pl.ds(HALO - (K_A - 1) + k + r0, SUB), :]
                t = wa[k:k + 1, :] * u_k
                z = t if z is None else z + t
                cacc_a[k] += _fold8(dz * u_k)
                t = wa[k:k + 1, :] * dz_e[pl.ds(r0 + (K_A - 1) - k, SUB), :]
                du = t if du is None else du + t
            outs = [dpa_r[rows, :].astype(F32) * z, du * ax[rows, :].astype(F32), du * ac[rows, :].astype(F32)]
            dvv = dv_e[pl.ds(r0, SUB), :]
            for k in range(K_B):
                cacc_b[k] += _fold8(dvv * _window(ub_e, ub_s, HALO - (K_B - 1) + k + r0, SUB))
            halves = []
            for h0 in range(0, SUB, SUB // 2):
                acc = None
                for k in range(K_B):
                    t = wb[k:k + 1, :] * _window(dv_e, dv_s, r0 + h0 + (K_B - 1) - k, SUB // 2)
                    acc = t if acc is None else acc + t
                halves.append(acc)
            du = jnp.concatenate(halves, axis=0)
            sg = _sigmoid(bg[rows, :].astype(F32))
            outs.append(du * sg)
            outs.append(du * bv[rows, :].astype(F32) * sg * (1.0 - sg))
            for b, o in enumerate(outs):
                dpj_o[rows, b * d:(b + 1) * d] = o.astype(BF16)
                bacc[b] += _fold8(o)
            for g, w in enumerate(POOL_WINDOWS):
                cols = slice(g * gc, (g + 1) * gc)
                acc = q_e[pl.ds(r0, SUB), cols]
                for j in range(1, w):
                    acc = acc + q_e[pl.ds(r0 + j, SUB), cols]
                o = acc - dp_r[rows, cols].astype(F32)
                dpj_o[rows, 5 * d + g * gc:5 * d + (g + 1) * gc] = o.astype(BF16)
                bacc[5, :, cols] += _fold8(o)

        @pl.when(i == n - 1)
        def _():
            for b in range(6):
                dbin_o[:, b * d:(b + 1) * d] = jnp.sum(bacc[b], axis=0, keepdims=True)
            dca_o[...] = jnp.sum(cacc_a[...], axis=1)
            dcb_o[...] = jnp.sum(cacc_b[...], axis=1)

    ka8 = 8
    kb8 = 32
    return pl.pallas_call(
        body, name=name, grid=(n,),
        in_specs=[tile(0), tile(1), tile(2), tile(3), tile(4),
                  _prev_halo(tm, d, 1), _prev_halo(tm, d, 2), _prev_halo(tm, d, 3), _prev_halo(tm, d, 4),
                  tile(), tile(), tile(),
                  _next_halo(tm, d, 0, nh), _next_halo(tm, d, 0, nh), _next_halo(tm, d, 0, nh), _next_halo(tm, d, 0, nh),
                  vec(K_A), vec(K_B), ANY],
        out_specs=[pl.BlockSpec((tm, 6 * d), lambda i: (i, 0)), pl.BlockSpec((1, 6 * d), lambda i: (0, 0)),
                   pl.BlockSpec((ka8, d), lambda i: (0, 0)), pl.BlockSpec((kb8, d), lambda i: (0, 0))],
        out_shape=[jax.ShapeDtypeStruct((s, 9 * d), BF16), jax.ShapeDtypeStruct((1, 6 * d), F32),
                   jax.ShapeDtypeStruct((ka8, d), F32), jax.ShapeDtypeStruct((kb8, d), F32)],
        scratch_shapes=[pltpu.VMEM((HALO + tm, d), F32), pltpu.VMEM((HALO + tm, d), F32),
                        pltpu.VMEM((tm + HALO, d), F32), pltpu.VMEM((tm + HALO, d), F32),
                        pltpu.VMEM((tm + HALO, d), F32),
                        pltpu.VMEM((6, 8, d), F32), pltpu.VMEM((ka8, 8, d), F32), pltpu.VMEM((kb8, 8, d), F32),
                        pltpu.VMEM((7, tm + HALO - 8, d), F32), pltpu.VMEM((7, tm + HALO - 8, d), F32)],
        input_output_aliases={18: 0},
        compiler_params=_params("arbitrary"),
    )(proj, proj, proj, proj, proj, proj, proj, proj, proj, dpa, dv, dp, proj, dpa, dv, dp, conv_a, conv_b, dproj)


def _dh_rms_bwd(dproj, wint, x0, dx1, g, name, rider=None):
    s, p = dproj.shape
    d = x0.shape[1]
    tm = _tile(s, 1024)
    tk = _tile(p, 2304, 128)
    n = s // tm
    nk = p // tk

    def body(dp_r, w_r, x_r, dx1_r, g_r, dx_o, dg_o, acc, gacc):
        i = pl.program_id(0)
        k = pl.program_id(1)

        @pl.when((i == 0) & (k == 0))
        def _():
            gacc[...] = jnp.zeros_like(gacc)

        @pl.when(k == 0)
        def _():
            acc[...] = jnp.zeros_like(acc)

        acc[...] += _dot(dp_r[...], w_r[...])

        @pl.when(k == nk - 1)
        def _():
            dx, dg = _rms_bwd(acc[...], x_r[...], g_r[...])
            gacc[...] += dg
            dx_o[...] = dx1_r[...] + dx

        @pl.when((i == n - 1) & (k == nk - 1))
        def _():
            dg_o[...] = jnp.sum(gacc[...], axis=0, keepdims=True)

    return _hosted_call(
        body, name=name, grid=(n, nk),
        in_specs=[pl.BlockSpec((tm, tk), lambda i, k: (i, k)), pl.BlockSpec((tk, d), lambda i, k: (k, 0)),
                  pl.BlockSpec((tm, d), lambda i, k: (i, 0)), pl.BlockSpec((tm, d), lambda i, k: (i, 0)),
                  pl.BlockSpec((1, d), lambda i, k: (0, 0))],
        out_specs=[pl.BlockSpec((tm, d), lambda i, k: (i, 0)), pl.BlockSpec((1, d), lambda i, k: (0, 0))],
        out_shape=[jax.ShapeDtypeStruct((s, d), F32), jax.ShapeDtypeStruct((1, d), F32)],
        scratch_shapes=[pltpu.VMEM((tm, d), F32), pltpu.VMEM((8, d), F32)],
        args=(dproj, wint, x0, dx1, g.reshape(1, d)), rider=rider)


EARLY = ("w_out_a", "w_out_b", "w_pool", "w_o", "w_mlp1", "w_mlp2")


def _layer_fwd(x0, l, g_mix, win3, b_in, weights, proj_rider=None, bo_rider=None, mlp_rider=None, h=None, g_next=None):
    if h is None:
        h, _ = _rms_fwd(x0, g_mix, f"rms_fwd_{l}")
    proj, carried_proj = _proj(h, win3, b_in, f"proj_{l}", proj_rider)
    w = weights(carried_proj) if callable(weights) else weights
    pa, sb, p, v = _mix_fwd(proj, w["conv_a"], w["conv_b"], w["conv_b_bias"], w["ln_b_g"], w["ln_b_b"], f"mix_fwd_{l}")
    (ya, yb, yc, mg, x1, h2), carried_bo = _branch_out(pa, sb, p, proj, x0, w["woa"], w["wob"], w["wp"], w["wo"],
                                                       w["b_out_b"], w["pool_scale"], w["g_mlp"], f"branch_out_{l}", bo_rider)
    rider = mlp_rider(carried_bo) if callable(mlp_rider) else mlp_rider
    (f, a, x2, *h_next), carried_mlp = _mlp_fwd(h2, x1, w["w1_3"], w["w2"], g_next, f"mlp_fwd_{l}", rider)
    saved = dict(x0=x0, h=h, proj=proj, pa=pa, sb=sb, p=p, v=v, ya=ya, yb=yb, yc=yc, mg=mg, x1=x1, h2=h2, f=f, a=a)
    return x2, saved, w, carried_mlp, (h_next[0] if h_next else None)


def _layer_bwd_early(dx2, w, sv, l, mlp_rider=None, dw2_rider=None, swap_early=False):
    d = dx2.shape[1]
    (df, dx1, dgm), carried_mlp = _mlp_bwd(dx2, sv["f"], sv["x1"], w["w2t"], w["w1t"], w["g_mlp"], f"mlp_bwd_{l}",
                                           mlp_rider)
    if dw2_rider is None:
        dw2, carried_dw2 = _tn(sv["a"], dx2, 1, f"dw2_{l}"), []
    else:
        dw2, carried_dw2 = _tn(sv["a"], dx2, 1, f"dw2_{l}", dw2_rider)
    dw1 = _tn(sv["h2"], df, N_CHIPS, f"dw1_{l}")
    dpa, ds, dp, dproj, dbob, dps, dbg, dwoa, dwob, dwp = _branch_out_bwd(
        dx1, sv["proj"], sv["ya"], sv["yb"], sv["yc"], sv["pa"], sv["sb"], sv["p"], w["wot"], w["woat"], w["wobt"],
        w["wpt"], w["pool_scale"], f"branch_out_bwd_{l}")
    dwo = _tn(sv["mg"], dx1, 1, f"dwo_{l}")
    ff = dw2.shape[1]
    gc = d // N_GROUPS
    big = dict(
        w_out_a=dwoa.reshape(N_CHIPS, d // N_CHIPS, d),
        w_out_b=dwob.reshape(N_CHIPS, d // N_CHIPS, d),
        w_pool=dwp.reshape(N_GROUPS, N_CHIPS, gc // N_CHIPS, gc).transpose(1, 0, 2, 3).reshape(N_CHIPS, gc, gc),
        w_o=dwo.reshape(N_CHIPS, d // N_CHIPS, d),
        w_mlp1=dw1,
        w_mlp2=dw2.reshape(N_CHIPS, ff // N_CHIPS, d),
    )
    swap = _ride_swap([big[k] for k in EARLY]) if swap_early else None
    (dv, dlg, dlb, dcb), swapped = _ln_silu_bwd(sv["v"], ds, w["ln_b_g"], w["ln_b_b"], f"ln_silu_bwd_{l}", swap)
    dproj, dbm, dca, dcvb = _mix_bwd(sv["proj"], dpa, dv, dp, dproj, w["conv_a"], w["conv_b"], f"mix_bwd_{l}")
    small = dict(
        b_in=jnp.concatenate([dbm, dbg], axis=1).reshape(9, d), conv_a=dca[:K_A], conv_b=dcvb[:K_B],
        conv_b_bias=dcb, ln_b_g=dlg, ln_b_b=dlb, b_out_b=dbob, pool_scale=dps, g_mlp=dgm,
    )
    return dx1, dproj, big, small, carried_mlp, carried_dw2, swapped


def _layer_bwd_late(dx1, dproj, w, sv, l, dwin_rider=None, dh_rider=None):
    if dwin_rider is None:
        dwin, carried_dwin = _tn(sv["h"], dproj, N_CHIPS, f"dwin_{l}"), []
    else:
        dwin, carried_dwin = _tn(sv["h"], dproj, N_CHIPS, f"dwin_{l}", dwin_rider)
    rider = dh_rider(dwin) if callable(dh_rider) else dh_rider
    (dx0, dgmix), carried_dh = _dh_rms_bwd(dproj, w["wint"], sv["x0"], dx1, w["g_mix"], f"dh_{l}", rider)
    return dx0, dwin, dgmix, carried_dwin, carried_dh


def _place():
    x, y, c = lax.axis_index("x"), lax.axis_index("y"), lax.axis_index("c")
    chips = [(1 - x, y), (x, 1 - y), (1 - x, 1 - y)]
    return x, y, c, chips


def _place_shard(w, chip_arr, dtype, name):
    rows, cols = w.shape
    tr = _tile(rows, max(16, (1 << 19) // cols), 16)

    def body(k_ref, w_r, o_r):
        o_r[...] = w_r[...].astype(dtype)

    return pl.pallas_call(
        body, name=name,
        grid_spec=pltpu.PrefetchScalarGridSpec(
            num_scalar_prefetch=1, grid=(rows // tr,),
            in_specs=[pl.BlockSpec((tr, cols), lambda i, k_ref: (i, 0))],
            out_specs=pl.BlockSpec((None, tr, cols), lambda i, k_ref: (k_ref[0], i, 0))),
        out_shape=jax.ShapeDtypeStruct((N_CHIPS, rows, cols), dtype),
        compiler_params=_params("parallel"),
    )(chip_arr, w)


def _ride_rows(v):
    def copy(ins, outs, sems, m, src_dev):
        x, y, c = lax.axis_index("x"), lax.axis_index("y"), lax.axis_index("c")
        peer = (x ^ ((m >> 2) & 1), y ^ ((m >> 1) & 1), c ^ (m & 1))
        return pltpu.make_async_remote_copy(
            src_ref=ins[0], dst_ref=outs[0].at[src_dev], send_sem=sems[0].at[m - 1], recv_sem=sems[1].at[m - 1],
            device_id=peer, device_id_type=MESH)

    def me():
        return 4 * lax.axis_index("x") + 2 * lax.axis_index("y") + lax.axis_index("c")

    def start(ins, outs, sems):
        pltpu.make_async_copy(ins[0], outs[0].at[me()], sems[2]).start()
        for m in range(1, N_DEV):
            copy(ins, outs, sems, m, me()).start()

    def finish(ins, outs, sems):
        for m in range(1, N_DEV):
            copy(ins, outs, sems, m, me() ^ m).wait_recv()
        for m in range(1, N_DEV):
            copy(ins, outs, sems, m, me()).wait_send()
        pltpu.make_async_copy(ins[0], outs[0].at[me()], sems[2]).wait()

    return _Rider([v], [jax.ShapeDtypeStruct((N_DEV, *v.shape), v.dtype)], [],
                  [pltpu.SemaphoreType.DMA((N_DEV - 1,)), pltpu.SemaphoreType.DMA((N_DEV - 1,)), pltpu.SemaphoreType.DMA],
                  start, finish)


def _ride_swap(grads):
    n = len(grads)

    def copy(ins, outs, sems, a, k):
        x, y, c, _ = _place()
        h = ins[a].shape[1] // 2
        return pltpu.make_async_remote_copy(
            src_ref=ins[a].at[k, pl.ds(pl.multiple_of((1 - c) * h, 8), h)], dst_ref=outs[a].at[k],
            send_sem=sems[0].at[a * N_CHIPS + k], recv_sem=sems[1].at[a * N_CHIPS + k], device_id=(x, y, 1 - c),
            device_id_type=MESH)

    def start(ins, outs, sems):
        for a in range(n):
            for k in range(N_CHIPS):
                copy(ins, outs, sems, a, k).start()

    def finish(ins, outs, sems):
        for a in range(n):
            for k in range(N_CHIPS):
                copy(ins, outs, sems, a, k).wait()

    return _Rider(grads, [jax.ShapeDtypeStruct((N_CHIPS, g.shape[1] // 2, g.shape[2]), g.dtype) for g in grads], [],
                  [pltpu.SemaphoreType.DMA((N_CHIPS * n,))] * 2, start, finish)


def _add_halves(g, r, c_arr, name):
    nk, rows, cols = g.shape
    h = rows // 2
    tr = _tile(h, max(8, (1 << 19) // cols), 8)
    nblk = h // tr

    def body(c_ref, g_r, r_r, o_r):
        o_r[...] = (g_r[...] + r_r[...]).astype(BF16)

    return pl.pallas_call(
        body, name=name,
        grid_spec=pltpu.PrefetchScalarGridSpec(
            num_scalar_prefetch=1, grid=(nk, nblk),
            in_specs=[pl.BlockSpec((None, tr, cols), lambda k, i, c_ref: (k, c_ref[0] * nblk + i, 0)),
                      pl.BlockSpec((None, tr, cols), lambda k, i, c_ref: (k, i, 0))],
            out_specs=pl.BlockSpec((None, tr, cols), lambda k, i, c_ref: (k, i, 0))),
        out_shape=jax.ShapeDtypeStruct((nk, h, cols), BF16),
        compiler_params=_params("parallel", "parallel"),
    )(c_arr, g, r)


def _ride_exchange(parts):
    n = len(parts)

    def copy(ins, outs, sems, a, j):
        x, y, c, chips = _place()
        px, py = chips[j]
        return pltpu.make_async_remote_copy(
            src_ref=ins[a].at[2 * px + py], dst_ref=outs[a].at[j], send_sem=sems[0].at[a * 3 + j],
            recv_sem=sems[1].at[a * 3 + j], device_id=(px, py, c), device_id_type=MESH)

    def start(ins, outs, sems):
        for a in range(n):
            for j in range(3):
                copy(ins, outs, sems, a, j).start()

    def finish(ins, outs, sems):
        for a in range(n):
            for j in range(3):
                copy(ins, outs, sems, a, j).wait()

    return _Rider(parts, [jax.ShapeDtypeStruct((3, *p.shape[1:]), p.dtype) for p in parts], [],
                  [pltpu.SemaphoreType.DMA((3 * n,))] * 2, start, finish)


def _ride_gather(placed=(), landed=(), forward=True):
    placed, landed = list(placed), list(landed)
    n_ici = len(placed)
    fwd = (list(range(n_ici)) if forward else []) + list(range(n_ici, n_ici + len(landed)))

    def ici(ins, outs, sems, a, j, src_chip):
        x, y, c, chips = _place()
        h = ins[a].shape[1] // 2
        rows = pl.ds(pl.multiple_of(c * h, 8), h)
        return pltpu.make_async_remote_copy(
            src_ref=ins[a].at[2 * x + y, rows], dst_ref=outs[a].at[src_chip, rows], send_sem=sems[0].at[a * 3 + j],
            recv_sem=sems[1].at[a * 3 + j], device_id=(*chips[j], c), device_id_type=MESH)

    def d2d(ins, outs, sems, a, j, which):
        x, y, c, chips = _place()
        px, py = chips[j]
        h = ins[a].shape[1] // 2
        rows = pl.ds(pl.multiple_of(which * h, 8), h)
        return pltpu.make_async_remote_copy(
            src_ref=ins[a].at[2 * px + py, rows], dst_ref=outs[a].at[2 * px + py, rows], send_sem=sems[2].at[a * 3 + j],
            recv_sem=sems[3].at[a * 3 + j], device_id=(x, y, 1 - c), device_id_type=MESH)

    def start(ins, outs, sems):
        x, y, c, _ = _place()
        for a in range(n_ici):
            for j in range(3):
                ici(ins, outs, sems, a, j, 2 * x + y).start()
        for a in range(n_ici, n_ici + len(landed)):
            for j in range(3):
                d2d(ins, outs, sems, a, j, c).start()

    def land(ins, outs, sems, then_forward):
        _, _, c, chips = _place()
        for a in range(n_ici):
            for j, (px, py) in enumerate(chips):
                ici(ins, outs, sems, a, j, 2 * px + py).wait_recv()
                if then_forward:
                    d2d(ins, outs, sems, a, j, c).start()

    def middle(ins, outs, sems):
        land(ins, outs, sems, True)

    def finish(ins, outs, sems):
        x, y, c, _ = _place()
        if not forward:
            land(ins, outs, sems, False)
        for a in fwd:
            for j in range(3):
                d2d(ins, outs, sems, a, j, 1 - c).wait_recv()
        for a in range(n_ici):
            for j in range(3):
                ici(ins, outs, sems, a, j, 2 * x + y).wait_send()
        for a in fwd:
            for j in range(3):
                d2d(ins, outs, sems, a, j, c).wait_send()

    arrays = placed + landed
    n = len(arrays)
    return _Rider(arrays, [jax.ShapeDtypeStruct(p.shape, p.dtype) for p in arrays], [(a, a) for a in range(n)],
                  [pltpu.SemaphoreType.DMA((3 * n,))] * 4, start, finish, middle if (n_ici and forward) else None)


def _sum_chips(r, name):
    nk, rows, cols = r.shape
    tr = _tile(rows, max(8, (1 << 19) // cols), 8)

    def body(r_r, o_r):
        acc = r_r[0].astype(F32)
        for k in range(1, nk):
            acc = acc + r_r[k].astype(F32)
        o_r[...] = acc

    return pl.pallas_call(
        body, name=name, grid=(rows // tr,),
        in_specs=[pl.BlockSpec((nk, tr, cols), lambda i: (0, i, 0))],
        out_specs=pl.BlockSpec((tr, cols), lambda i: (i, 0)),
        out_shape=jax.ShapeDtypeStruct((rows, cols), F32),
        compiler_params=_params("parallel"),
    )(r)


def _sum_chips_into(own, arrived, dst, layer, n_layers, place_arr, name):
    _, h, cols = own.shape
    tr = _tile(h, max(8, (1 << 19) // cols), 8)
    nblk = h // tr

    def body(p_ref, own_r, arr_r, *rest):
        o_r = rest[-1]
        acc = own_r[...].astype(F32)
        for j in range(3):
            acc = acc + arr_r[j].astype(F32)
        o_r[...] = acc

    in_specs = [pl.BlockSpec((None, tr, cols), lambda i, p_ref: (p_ref[0], i, 0)),
                pl.BlockSpec((3, tr, cols), lambda i, p_ref: (0, i, 0))]
    args = [place_arr, own, arrived]
    aliases = {}
    if dst is not None:
        in_specs.append(ANY)
        args.append(dst)
        aliases = {3: 0}
    return pl.pallas_call(
        body, name=name,
        grid_spec=pltpu.PrefetchScalarGridSpec(
            num_scalar_prefetch=1, grid=(nblk,), in_specs=in_specs,
            out_specs=pl.BlockSpec((None, tr, cols), lambda i, p_ref: (layer, p_ref[1] * nblk + i, 0))),
        out_shape=jax.ShapeDtypeStruct((n_layers, 2 * h, cols), F32),
        input_output_aliases=aliases,
        compiler_params=_params("parallel"),
    )(*args)


def _ride_share(shards):
    n = len(shards)
    n_layers = shards[0].shape[0]

    def copy(ins, outs, sems, a, l, which):
        x, y, c, _ = _place()
        h = ins[a].shape[1] // 2
        rows = pl.ds(pl.multiple_of(which * h, 8), h)
        return pltpu.make_async_remote_copy(
            src_ref=ins[a].at[l, rows], dst_ref=outs[a].at[l, rows], send_sem=sems[0].at[a * n_layers + l],
            recv_sem=sems[1].at[a * n_layers + l], device_id=(x, y, 1 - c), device_id_type=MESH)

    def start(ins, outs, sems):
        c = lax.axis_index("c")
        for a in range(n):
            for l in range(n_layers):
                copy(ins, outs, sems, a, l, c).start()

    def finish(ins, outs, sems):
        c = lax.axis_index("c")
        for a in range(n):
            for l in range(n_layers):
                copy(ins, outs, sems, a, l, 1 - c).wait_recv()
        for a in range(n):
            for l in range(n_layers):
                copy(ins, outs, sems, a, l, c).wait_send()

    return _Rider(shards, [jax.ShapeDtypeStruct(a.shape, a.dtype) for a in shards], [(a, a) for a in range(n)],
                  [pltpu.SemaphoreType.DMA((n * n_layers,))] * 2, start, finish)


def _join_riders(first, second):
    k_in, k_out, k_sem = len(first.args), len(first.out_shape), len(first.sems)

    def start(ins, outs, sems):
        first.start(ins[:k_in], outs[:k_out], sems[:k_sem])
        second.start(ins[k_in:], outs[k_out:], sems[k_sem:])

    def finish(ins, outs, sems):
        first.finish(ins[:k_in], outs[:k_out], sems[:k_sem])
        second.finish(ins[k_in:], outs[k_out:], sems[k_sem:])

    return _Rider(first.args + second.args, first.out_shape + second.out_shape,
                  first.aliases + [(i + k_in, o + k_out) for i, o in second.aliases], first.sems + second.sems, start, finish)


def _adamw(w, g, m, v, name):
    rows, cols = w.shape
    tr = _tile(rows, max(8, (1 << 18) // cols), 8)
    c1 = 1.0 - ADAM_B1 ** ADAM_STEP
    c2 = 1.0 - ADAM_B2 ** ADAM_STEP

    def body(w_r, g_r, m_r, v_r, d_o, m_o, v_o):
        gv = g_r[...]
        mn = ADAM_B1 * m_r[...] + (1.0 - ADAM_B1) * gv
        vn = ADAM_B2 * v_r[...] + (1.0 - ADAM_B2) * (gv * gv)
        m_o[...] = mn
        v_o[...] = vn
        d_o[...] = -ADAM_LR * ((mn / c1) / (jnp.sqrt(vn / c2) + ADAM_EPS) + ADAM_WD * w_r[...])

    spec = pl.BlockSpec((tr, cols), lambda i: (i, 0))
    return pl.pallas_call(
        body, name=name, grid=(rows // tr,), in_specs=[spec] * 4, out_specs=[spec] * 3,
        out_shape=[jax.ShapeDtypeStruct((rows, cols), F32)] * 3,
        compiler_params=_params("parallel"),
    )(w, g, m, v)


BIG = ("w_in", "w_out_a", "w_out_b", "w_pool", "w_o", "w_mlp1", "w_mlp2")
SMALL = ("g_mix", "b_in", "conv_b_bias", "ln_b_g", "ln_b_b", "b_out_b", "pool_scale", "g_mlp")
CONVS = ("conv_a", "conv_b")
WEIGHTS = ("g_mix", "w_in", "b_in", "conv_a", "w_out_a", "conv_b", "conv_b_bias", "ln_b_g", "ln_b_b", "w_out_b", "b_out_b",
           "w_pool", "pool_scale", "w_o", "g_mlp", "w_mlp1", "w_mlp2", "g_final")


def _as2d(a):
    return a.reshape(-1, a.shape[-1])


def _pad_rows(a, rows):
    return jnp.pad(a, ((0, rows - a.shape[0]), (0, 0)))


def _full_weights(big_g, conv_g, rep, l, d):
    gc = d // N_GROUPS
    w = {k: v[l] for k, v in rep.items()}
    w["b_in"] = rep["b_in"][l].reshape(1, -1)
    win3 = big_g["w_in"]
    w["win3"] = win3
    w["wint"] = win3.transpose(0, 2, 1).reshape(-1, d)
    for src, dst in (("w_out_a", "woa"), ("w_out_b", "wob"), ("w_o", "wo")):
        full = big_g[src].reshape(d, d)
        w[dst] = full
        w[dst + "t"] = full.T
    wp = big_g["w_pool"].reshape(N_CHIPS, N_GROUPS, gc // N_CHIPS, gc).transpose(1, 0, 2, 3).reshape(N_GROUPS, gc, gc)
    w["wp"] = wp
    w["wpt"] = wp.transpose(0, 2, 1)
    w1_3 = big_g["w_mlp1"]
    w["w1_3"] = w1_3
    w["w1t"] = w1_3.transpose(0, 2, 1).reshape(-1, d)
    w2 = big_g["w_mlp2"].reshape(-1, d)
    w["w2"] = w2
    w["w2t"] = w2.T
    ca, cb = conv_g
    w["conv_a"] = ca
    w["conv_b"] = cb
    return w


def kernel(x, g_mix, w_in, b_in, conv_a, w_out_a, conv_b, conv_b_bias, ln_b_g, ln_b_b, w_out_b, b_out_b, w_pool, pool_scale, w_o, g_mlp, w_mlp1, w_mlp2, g_final, loss_target, m_g_mix, m_w_in, m_b_in, m_conv_a, m_w_out_a, m_conv_b, m_conv_b_bias, m_ln_b_g, m_ln_b_b, m_w_out_b, m_b_out_b, m_w_pool, m_pool_scale, m_w_o, m_g_mlp, m_w_mlp1, m_w_mlp2, m_g_final, v_g_mix, v_w_in, v_b_in, v_conv_a, v_w_out_a, v_conv_b, v_conv_b_bias, v_ln_b_g, v_ln_b_b, v_w_out_b, v_b_out_b, v_w_pool, v_pool_scale, v_w_o, v_g_mlp, v_w_mlp1, v_w_mlp2, v_g_final):
    given = dict(g_mix=g_mix, w_in=w_in, b_in=b_in, conv_a=conv_a, w_out_a=w_out_a, conv_b=conv_b, conv_b_bias=conv_b_bias, ln_b_g=ln_b_g, ln_b_b=ln_b_b, w_out_b=w_out_b, b_out_b=b_out_b, w_pool=w_pool, pool_scale=pool_scale, w_o=w_o, g_mlp=g_mlp, w_mlp1=w_mlp1, w_mlp2=w_mlp2, g_final=g_final)
    mom = dict(g_mix=m_g_mix, w_in=m_w_in, b_in=m_b_in, conv_a=m_conv_a, w_out_a=m_w_out_a, conv_b=m_conv_b, conv_b_bias=m_conv_b_bias, ln_b_g=m_ln_b_g, ln_b_b=m_ln_b_b, w_out_b=m_w_out_b, b_out_b=m_b_out_b, w_pool=m_w_pool, pool_scale=m_pool_scale, w_o=m_w_o, g_mlp=m_g_mlp, w_mlp1=m_w_mlp1, w_mlp2=m_w_mlp2, g_final=m_g_final)
    var = dict(g_mix=v_g_mix, w_in=v_w_in, b_in=v_b_in, conv_a=v_conv_a, w_out_a=v_w_out_a, conv_b=v_conv_b, conv_b_bias=v_conv_b_bias, ln_b_g=v_ln_b_g, ln_b_b=v_ln_b_b, w_out_b=v_w_out_b, b_out_b=v_b_out_b, w_pool=v_w_pool, pool_scale=v_pool_scale, w_o=v_w_o, g_mlp=v_g_mlp, w_mlp1=v_w_mlp1, w_mlp2=v_w_mlp2, g_final=v_g_final)
    n_layers = w_in.shape[0]
    s, d = x.shape[1], x.shape[2]
    dq = d // N_CHIPS
    x_idx, y_idx, c_idx = lax.axis_index("x"), lax.axis_index("y"), lax.axis_index("c")
    chip = 2 * x_idx + y_idx
    c_arr = c_idx.astype(jnp.int32).reshape(1)

    conv_rows = n_layers * (K_A + K_B)
    conv_pad = -(-conv_rows // 16) * 16
    conv_pack = _pad_rows(jnp.concatenate([conv_a[l] for l in range(n_layers)] + [conv_b[l] for l in range(n_layers)], axis=0),
                          conv_pad)
    chip_arr = chip.astype(jnp.int32).reshape(1)
    place_arr = jnp.stack([chip, c_idx]).astype(jnp.int32)

    def placed(k, l):
        return _place_shard(_as2d(given[k][l]), chip_arr, BF16, f"place_{k}_{l}")

    rest = BIG[1:]
    h, (win_g, conv_all) = _rms_fwd(
        x[0], given["g_mix"][0], "rms_fwd_0",
        _ride_gather([placed("w_in", 0), _place_shard(conv_pack, chip_arr, F32, "place_convs")]))
    conv_full = conv_all.transpose(1, 0, 2).reshape(conv_pad, d)
    conv_g = [(conv_full[l * K_A:(l + 1) * K_A], conv_full[n_layers * K_A + l * K_B:n_layers * K_A + (l + 1) * K_B])
              for l in range(n_layers)]
    rep = {k: given[k] for k in SMALL}

    xl = x[0]
    ws, saved = [], []
    rest_g = None
    for l in range(n_layers):
        more = l + 1 < n_layers

        def weights(carried, l=l, win_g=win_g, rest_g=rest_g):
            return _full_weights(dict(zip(rest, carried if rest_g is None else rest_g), w_in=win_g), conv_g[l], rep, l, d)

        xl, sv, w, got, h = _layer_fwd(
            xl, l, given["g_mix"][l], win_g, given["b_in"][l].reshape(1, -1), weights,
            _ride_gather([placed(k, l) for k in rest]) if rest_g is None else None,
            _ride_gather([placed("w_in", l + 1)], forward=False) if more else None,
            (lambda landed, l=l: _ride_gather([placed(k, l + 1) for k in rest], landed)) if more else None,
            h, given["g_mix"][l + 1] if more else None)
        if more:
            rest_g, win_g = got[:len(rest)], got[len(rest)]
        ws.append(w)
        saved.append(sv)
    dx, loss, dgf = _loss_head(xl, loss_target[0], g_final, "loss_head")
    loss = lax.psum(loss[0, 0], ("x", "y", "c"))

    parts, arrived, small = {}, {}, [None] * n_layers
    pending = []
    for l in reversed(range(n_layers)):
        w, sv = ws[l], saved[l]
        first = [q for q in pending if q[1] not in ("w_mlp1", "w_mlp2")]
        second = [q for q in pending if q[1] in ("w_mlp1", "w_mlp2")]
        dx1, dproj, early, sm, got, got2, swapped = _layer_bwd_early(
            dx, w, sv, l, _ride_exchange([parts[q] for q in first]) if first else None,
            _ride_exchange([parts[q] for q in second]) if second else None, swap_early=(l == 0))
        arrived.update(zip(first + second, got + got2))
        if l > 0:
            order = ("w_in",) + EARLY
            dx, dwin, dgmix, _, swapped = _layer_bwd_late(
                dx1, dproj, w, sv, l, None, lambda dwin, early=early: _ride_swap([dwin] + [early[k] for k in EARLY]))
            full = dict(early, w_in=dwin)
            for k, r in zip(order, swapped):
                parts[(l, k)] = _add_halves(full[k], r, c_arr, f"grad_add_halves_{k}_{l}")
            pending = [(l, k) for k in order]
        else:
            for k, r in zip(EARLY, swapped):
                parts[(l, k)] = _add_halves(early[k], r, c_arr, f"grad_add_halves_{k}_{l}")

            def w_in_rider(dwin, l=l):
                (r,) = _comm_call(_ride_swap([dwin]), f"grad_swap_w_in_{l}")
                parts[(l, "w_in")] = _add_halves(dwin, r, c_arr, f"grad_add_halves_w_in_{l}")
                return _ride_exchange([parts[(l, "w_in")]])

            dx, dwin, dgmix, got_early, got_w_in = _layer_bwd_late(
                dx1, dproj, w, sv, l, _ride_exchange([parts[(l, k)] for k in EARLY]), w_in_rider)
            arrived.update(zip([(l, k) for k in EARLY], got_early))
            arrived[(l, "w_in")] = got_w_in[0]
            pending = []
        sm["g_mix"] = dgmix
        small[l] = sm
    reduced = []
    for k in BIG:
        dst = None
        for l in range(n_layers):
            dst = _sum_chips_into(parts[(l, k)], arrived[(l, k)], dst, l, n_layers, place_arr, f"grad_sum_chips_{k}_{l}")
        reduced.append(dst)

    small_rows = []
    for k in SMALL + CONVS:
        for l in range(n_layers):
            small_rows.append(small[l][k])
    small_rows.append(dgf)
    pack = jnp.concatenate(small_rows, axis=0)
    n_small = pack.shape[0]
    pack = _pad_rows(pack, -(-n_small // 8) * 8)
    *shared, small_all = _comm_call(_join_riders(_ride_share(reduced), _ride_rows(pack)), "grad_share_and_small")
    big_grad = dict(zip(BIG, shared))
    small_sum = _sum_chips(small_all, "sum_small_grads")

    out = {}
    for k in BIG:
        shape = given[k].shape
        g2 = big_grad[k].reshape(-1, shape[-1])
        dlt, nm, nv = _adamw(given[k].reshape(g2.shape), g2, mom[k].reshape(g2.shape), var[k].reshape(g2.shape), f"adamw_{k}")
        out[k] = tuple(a.reshape(shape) for a in (g2, dlt, nm, nv))

    def rows_of(k):
        return {"b_in": 9, "conv_a": K_A, "conv_b": K_B}.get(k, 1)

    def pack_rep(src):
        rows = [src[k].reshape(n_layers * rows_of(k), d) for k in SMALL] + [src["g_final"].reshape(1, d)]
        a = jnp.concatenate(rows, axis=0)
        return _pad_rows(a, -(-a.shape[0] // 8) * 8)

    def pack_conv(src):
        a = jnp.concatenate([src[k].reshape(n_layers * rows_of(k), dq) for k in CONVS], axis=0)
        return _pad_rows(a, -(-a.shape[0] // 8) * 8)

    offs = {}
    r = 0
    for k in SMALL + CONVS:
        offs[k] = r
        r += n_layers * rows_of(k)
    offs["g_final"] = r
    n_rep = sum(n_layers * rows_of(k) for k in SMALL)
    g_rep = jnp.concatenate([small_sum[:n_rep], small_sum[offs["g_final"]:offs["g_final"] + 1]], axis=0)
    g_rep = _pad_rows(g_rep, -(-g_rep.shape[0] // 8) * 8)
    g_conv_full = small_sum[offs["conv_a"]:offs["g_final"]]
    g_conv = lax.dynamic_slice_in_dim(g_conv_full, chip * dq, dq, axis=1)
    g_conv = _pad_rows(g_conv, -(-g_conv.shape[0] // 8) * 8)
    rep_res = (g_rep,) + tuple(_adamw(pack_rep(given), g_rep, pack_rep(mom), pack_rep(var), "adamw_small"))
    conv_res = (g_conv,) + tuple(_adamw(pack_conv(given), g_conv, pack_conv(mom), pack_conv(var), "adamw_conv"))
    r = 0
    for k in SMALL:
        nr = n_layers * rows_of(k)
        out[k] = tuple(a[r:r + nr].reshape(given[k].shape) for a in rep_res)
        r += nr
    out["g_final"] = tuple(a[r].reshape(given["g_final"].shape) for a in rep_res)
    r = 0
    for k in CONVS:
        nr = n_layers * rows_of(k)
        out[k] = tuple(a[r:r + nr].reshape(given[k].shape) for a in conv_res)
        r += nr

    res = [loss, dx.reshape(x.shape)]
    for i in range(4):
        res += [out[k][i] for k in WEIGHTS]
    return tuple(res)
```

```python
import functools

import jax
import jax.numpy as jnp
from jax import lax
from jax.experimental import pallas as pl
from jax.experimental.pallas import tpu as pltpu

F32 = jnp.float32
BF16 = jnp.bfloat16
EPS = 1e-6
HALO = 32
SUB = 32
K_A = 3
K_B = 31
POOL_WINDOWS = (2, 4, 8, 16)
N_GROUPS = len(POOL_WINDOWS)
N_CHIPS = 4
N_DEV = 8
ADAM_LR = 0.001
ADAM_B1 = 0.9
ADAM_B2 = 0.999
ADAM_EPS = 1e-08
ADAM_WD = 0.01
ADAM_STEP = 10
VMEM_LIMIT = 56 * 1024 * 1024
MESH = pl.DeviceIdType.MESH
ANY = pl.BlockSpec(memory_space=pl.ANY)


def _params(*sem):
    return pltpu.CompilerParams(dimension_semantics=sem, vmem_limit_bytes=VMEM_LIMIT)


def _tile(n, pref, mult=8):
    if n <= pref:
        return n
    t = (pref // mult) * mult
    while t > mult and n % t:
        t -= mult
    assert n % t == 0, (n, pref, mult)
    return t


def _dot(a, b):
    return jnp.dot(a, b, preferred_element_type=F32)


def _sigmoid(x):
    return 1.0 / (1.0 + jnp.exp(-x))


def _f32(ref):
    return ref[...].astype(F32)


def _fold8(v):
    r, d = v.shape
    return v.reshape(r // 8, 8, d).sum(axis=0)


def _rms_bwd(dh, xv, g):
    r = lax.rsqrt(jnp.mean(xv * xv, axis=-1, keepdims=True) + EPS)
    xn = xv * r
    dxn = dh * g
    dx = r * (dxn - xn * jnp.mean(dxn * xn, axis=-1, keepdims=True))
    return dx, _fold8(dh * xn)


class _Rider:
    def __init__(self, args, out_shape, aliases, sems, start, finish, middle=None):
        self.args, self.out_shape, self.aliases, self.sems = list(args), list(out_shape), list(aliases), list(sems)
        self.start, self.finish, self.middle = start, finish, middle


def _hosted_call(body, *, name, grid, in_specs, out_specs, out_shape, scratch_shapes, args, rider=None):
    n_in, n_out, n_scr = len(in_specs), len(out_shape), len(scratch_shapes)
    params = pltpu.CompilerParams(dimension_semantics=("arbitrary",) * len(grid), vmem_limit_bytes=VMEM_LIMIT)
    if rider is None:
        res = pl.pallas_call(body, name=name, grid=grid, in_specs=in_specs, out_specs=out_specs, out_shape=out_shape,
                             scratch_shapes=scratch_shapes, compiler_params=params)(*args)
        return list(res), []
    k_in, k_out = len(rider.args), len(rider.out_shape)

    def hosted(*refs):
        ins, refs = refs[:n_in], refs[n_in:]
        r_in, refs = refs[:k_in], refs[k_in:]
        outs, refs = refs[:n_out], refs[n_out:]
        r_out, refs = refs[:k_out], refs[k_out:]
        scr, r_sems = refs[:n_scr], refs[n_scr:]
        first = functools.reduce(lambda u, v: u & v, [pl.program_id(a) == 0 for a in range(len(grid))])
        last = functools.reduce(lambda u, v: u & v, [pl.program_id(a) == grid[a] - 1 for a in range(len(grid))])

        @pl.when(first)
        def _():
            rider.start(r_in, r_out, r_sems)

        if rider.middle is not None:
            step = functools.reduce(lambda u, a: u * grid[a] + pl.program_id(a), range(len(grid)), 0)

            @pl.when(step == (3 * functools.reduce(lambda u, v: u * v, grid)) // 4)
            def _():
                rider.middle(r_in, r_out, r_sems)

        body(*ins, *outs, *scr)

        @pl.when(last)
        def _():
            rider.finish(r_in, r_out, r_sems)

    res = pl.pallas_call(
        hosted, name=name, grid=grid, in_specs=list(in_specs) + [ANY] * k_in, out_specs=list(out_specs) + [ANY] * k_out,
        out_shape=list(out_shape) + rider.out_shape, scratch_shapes=list(scratch_shapes) + rider.sems,
        input_output_aliases={n_in + i: n_out + o for i, o in rider.aliases}, compiler_params=params,
    )(*args, *rider.args)
    return list(res[:n_out]), list(res[n_out:])


def _comm_call(rider, name):
    def body(*refs):
        k_in, k_out = len(rider.args), len(rider.out_shape)
        r_in, r_out, r_sems = refs[:k_in], refs[k_in:k_in + k_out], refs[k_in + k_out:]
        rider.start(r_in, r_out, r_sems)
        if rider.middle is not None:
            rider.middle(r_in, r_out, r_sems)
        rider.finish(r_in, r_out, r_sems)

    return list(pl.pallas_call(
        body, name=name, in_specs=[ANY] * len(rider.args), out_specs=[ANY] * len(rider.out_shape),
        out_shape=rider.out_shape, scratch_shapes=rider.sems, input_output_aliases=dict(rider.aliases),
    )(*rider.args))


def _rms_fwd(x, g, name, rider=None):
    s, d = x.shape
    tm = _tile(s, 512)

    def body(x_ref, g_ref, h_ref):
        xv = x_ref[...]
        r = lax.rsqrt(jnp.mean(xv * xv, axis=-1, keepdims=True) + EPS)
        h_ref[...] = (xv * r * g_ref[...]).astype(BF16)

    (h,), carried = _hosted_call(
        body, name=name, grid=(s // tm,),
        in_specs=[pl.BlockSpec((tm, d), lambda i: (i, 0)), pl.BlockSpec((1, d), lambda i: (0, 0))],
        out_specs=[pl.BlockSpec((tm, d), lambda i: (i, 0))],
        out_shape=[jax.ShapeDtypeStruct((s, d), BF16)], scratch_shapes=[], args=(x, g.reshape(1, d)), rider=rider)
    return h, carried


def _proj(h, w3, b, name, rider=None):
    s, k = h.shape
    nb, _, nw = w3.shape
    tm = _tile(s, 1024)
    tn = nw
    per = nw // tn

    def body(h_ref, w_ref, b_ref, o_ref):
        o_ref[...] = (_dot(h_ref[...], w_ref[...]) + b_ref[...]).astype(BF16)

    (proj,), carried = _hosted_call(
        body, name=name, grid=(s // tm, nb * per),
        in_specs=[
            pl.BlockSpec((tm, k), lambda i, j: (i, 0)),
            pl.BlockSpec((None, k, tn), lambda i, j: (j // per, 0, j % per)),
            pl.BlockSpec((1, tn), lambda i, j: (0, j)),
        ],
        out_specs=[pl.BlockSpec((tm, tn), lambda i, j: (i, j))],
        out_shape=[jax.ShapeDtypeStruct((s, nb * nw), BF16)],
        scratch_shapes=[], args=(h, w3, b), rider=rider)
    return proj, carried


def _shift_copies(src_e, dst_s):
    n_rows = dst_s.shape[1]
    for b in range(1, 8):
        dst_s[b - 1, :, :] = src_e[pl.ds(b, n_rows), :]


def _window(src_e, dst_s, off, rows):
    b = off % 8
    if b == 0:
        return src_e[pl.ds(off, rows), :]
    return dst_s[b - 1, pl.ds(off - b, rows), :]


def _window_sums(src, levels, n, gc, trailing):
    prev = src
    for j in range(N_GROUPS):
        reach = 2 ** j
        lo = 8 * (j + 1) if trailing else 0
        rows = n - 8 * (j + 1)
        cols = slice(j * gc, N_GROUPS * gc)
        other = lo - reach if trailing else reach
        levels[j][pl.ds(lo, rows), cols] = prev[pl.ds(lo, rows), cols] + prev[pl.ds(other, rows), cols]
        prev = levels[j]


def _prev_halo(tm, d, col):
    r = tm // HALO
    return pl.BlockSpec((HALO, d), lambda i: (jnp.maximum(i * r - 1, 0), col))


def _next_halo(tm, d, col, n_halo_blocks):
    r = tm // HALO
    return pl.BlockSpec((HALO, d), lambda i: (jnp.minimum((i + 1) * r, n_halo_blocks - 1), col))


def _mix_fwd(proj, conv_a, conv_b, cb, lg, lb, name):
    s, d9 = proj.shape
    d = d9 // 9
    gc = d // N_GROUPS
    tm = _tile(s, 256, HALO)

    def tile(col):
        return pl.BlockSpec((tm, d), lambda i: (i, col))

    def vec(rows):
        return pl.BlockSpec((rows, d), lambda i: (0, 0))

    def body(ab, ac, ax, bv, bg, ci, ac_h, ax_h, bv_h, bg_h, ci_h, wa, wb, cb_r, lg_r, lb_r,
             pa_o, s_o, p_o, v_o, ua_e, ub_e, ci_e, ub_s, *levels):
        i = pl.program_id(0)
        keep = (i > 0).astype(F32)
        ua_e[0:HALO, :] = _f32(ac_h) * _f32(ax_h) * keep
        ua_e[HALO:HALO + tm, :] = _f32(ac) * _f32(ax)
        ub_e[0:HALO, :] = _f32(bv_h) * _sigmoid(_f32(bg_h)) * keep
        ub_e[HALO:HALO + tm, :] = _f32(bv) * _sigmoid(_f32(bg))
        ci_e[0:HALO, :] = _f32(ci_h) * keep
        ci_e[HALO:HALO + tm, :] = _f32(ci)
        _window_sums(ci_e, levels, HALO + tm, gc, trailing=True)
        _shift_copies(ub_e, ub_s)
        for r0 in range(0, tm, SUB):
            rows = slice(r0, r0 + SUB)
            z = None
            for k in range(K_A):
                t = wa[k:k + 1, :] * ua_e[pl.ds(HALO - (K_A - 1) + k + r0, SUB), :]
                z = t if z is None else z + t
            pa_o[rows, :] = (ab[rows, :].astype(F32) * z).astype(BF16)
            v = None
            for k in range(K_B):
                t = wb[k:k + 1, :] * _window(ub_e, ub_s, HALO - (K_B - 1) + k + r0, SUB)
                v = t if v is None else v + t
            v = v + cb_r[...]
            v_o[rows, :] = v
            mu = jnp.mean(v, axis=-1, keepdims=True)
            vc = v - mu
            rstd = lax.rsqrt(jnp.mean(vc * vc, axis=-1, keepdims=True) + EPS)
            ln = vc * rstd * lg_r[...] + lb_r[...]
            s_o[rows, :] = (ln * _sigmoid(ln)).astype(BF16)
            t_idx = i * tm + r0 + lax.broadcasted_iota(jnp.int32, (SUB, gc), 0)
            for g, w in enumerate(POOL_WINDOWS):
                cols = slice(g * gc, (g + 1) * gc)
                cur = ci_e[pl.ds(HALO + r0, SUB), cols]
                acc = levels[g][pl.ds(HALO + r0, SUB), cols]
                cnt = jnp.minimum(t_idx + 1, w).astype(F32)
                p_o[rows, cols] = (acc / cnt - cur).astype(BF16)

    return pl.pallas_call(
        body, name=name, grid=(s // tm,),
        in_specs=[tile(0), tile(1), tile(2), tile(3), tile(4), tile(5),
                  _prev_halo(tm, d, 1), _prev_halo(tm, d, 2), _prev_halo(tm, d, 3), _prev_halo(tm, d, 4),
                  _prev_halo(tm, d, 5),
                  vec(K_A), vec(K_B), vec(1), vec(1), vec(1)],
        out_specs=[pl.BlockSpec((tm, d), lambda i: (i, 0))] * 4,
        out_shape=[jax.ShapeDtypeStruct((s, d), BF16)] * 3 + [jax.ShapeDtypeStruct((s, d), F32)],
        scratch_shapes=[pltpu.VMEM((HALO + tm, d), F32)] * 3 + [pltpu.VMEM((7, tm + HALO - 8, d), F32)]
        + [pltpu.VMEM((HALO + tm, d), F32)] * N_GROUPS,
        compiler_params=_params("parallel"),
    )(proj, proj, proj, proj, proj, proj, proj, proj, proj, proj, proj,
      conv_a, conv_b, cb.reshape(1, d), lg.reshape(1, d), lb.reshape(1, d))


def _branch_out(pa, sb, p, proj, x0, woa, wob, wp, wo, bob, ps, gm, name, rider=None):
    s, d = pa.shape
    gc = d // N_GROUPS
    tm = _tile(s, 512)

    def tile(col=0):
        return pl.BlockSpec((tm, d), lambda i: (i, col))

    def const(shape):
        return pl.BlockSpec(shape, lambda i: (0,) * len(shape))

    def body(pa_r, s_r, p_r, g0, g1, g2, x0_r, woa_r, wob_r, wp_r, wo_r, bob_r, ps_r, gm_r,
             ya_o, yb_o, yc_o, mg_o, x1_o, h2_o, yc_s):
        ya = _dot(pa_r[...], woa_r[...])
        yb = _dot(s_r[...], wob_r[...]) + bob_r[...]
        for g in range(N_GROUPS):
            cols = slice(g * gc, (g + 1) * gc)
            yc_s[:, cols] = _dot(p_r[:, cols], wp_r[g])
        yc = yc_s[...]
        ya_o[...] = ya.astype(BF16)
        yb_o[...] = yb.astype(BF16)
        yc_o[...] = yc.astype(BF16)
        m = _sigmoid(_f32(g0)) * ya + _sigmoid(_f32(g1)) * yb + _sigmoid(_f32(g2)) * (yc * ps_r[...])
        mb = m.astype(BF16)
        mg_o[...] = mb
        x1 = x0_r[...] + _dot(mb, wo_r[...])
        x1_o[...] = x1
        r = lax.rsqrt(jnp.mean(x1 * x1, axis=-1, keepdims=True) + EPS)
        h2_o[...] = (x1 * r * gm_r[...]).astype(BF16)

    return _hosted_call(
        body, name=name, grid=(s // tm,),
        in_specs=[tile(), tile(), tile(), tile(6), tile(7), tile(8), tile(),
                  const((d, d)), const((d, d)), const((N_GROUPS, gc, gc)), const((d, d)),
                  const((1, d)), const((1, d)), const((1, d))],
        out_specs=[tile()] * 6,
        out_shape=[jax.ShapeDtypeStruct((s, d), BF16)] * 4 + [jax.ShapeDtypeStruct((s, d), F32),
                                                               jax.ShapeDtypeStruct((s, d), BF16)],
        scratch_shapes=[pltpu.VMEM((tm, d), F32)],
        args=(pa, sb, p, proj, proj, proj, x0, woa, wob, wp, wo, bob.reshape(1, d), ps.reshape(1, d), gm.reshape(1, d)),
        rider=rider)


def _mlp_fwd(h2, x1, w1_3, w2, g_next, name, rider=None):
    s, d = h2.shape
    nf, _, tf = w1_3.shape
    tm = _tile(s, 1024)

    chain = g_next is not None

    def body(h_r, x1_r, w1_r, w2_r, *rest):
        gn_r, f_o, a_o, x2_o, hn_o, acc = rest if chain else (None, *rest[:3], None, rest[3])
        j = pl.program_id(1)

        @pl.when(j == 0)
        def _():
            acc[...] = jnp.zeros_like(acc)

        f = _dot(h_r[...], w1_r[...])
        f_o[...] = f.astype(BF16)
        rl = jnp.maximum(f, 0.0)
        a = (rl * rl).astype(BF16)
        a_o[...] = a
        acc[...] += _dot(a, w2_r[...])

        @pl.when(j == nf - 1)
        def _():
            x2 = x1_r[...] + acc[...]
            x2_o[...] = x2
            if chain:
                r = lax.rsqrt(jnp.mean(x2 * x2, axis=-1, keepdims=True) + EPS)
                hn_o[...] = (x2 * r * gn_r[...]).astype(BF16)

    row = pl.BlockSpec((tm, d), lambda i, j: (i, 0))
    return _hosted_call(
        body, name=name, grid=(s // tm, nf),
        in_specs=[row, row, pl.BlockSpec((None, d, tf), lambda i, j: (j, 0, 0)), pl.BlockSpec((tf, d), lambda i, j: (j, 0))]
        + [pl.BlockSpec((1, d), lambda i, j: (0, 0))] * chain,
        out_specs=[pl.BlockSpec((tm, tf), lambda i, j: (i, j)), pl.BlockSpec((tm, tf), lambda i, j: (i, j)), row] + [row] * chain,
        out_shape=[jax.ShapeDtypeStruct((s, nf * tf), BF16), jax.ShapeDtypeStruct((s, nf * tf), BF16),
                   jax.ShapeDtypeStruct((s, d), F32)] + [jax.ShapeDtypeStruct((s, d), BF16)] * chain,
        scratch_shapes=[pltpu.VMEM((tm, d), F32)],
        args=(h2, x1, w1_3, w2) + ((g_next.reshape(1, d),) if chain else ()), rider=rider)


def _loss_head(x, target, gf, name):
    s, d = x.shape
    tm = _tile(s, 512)
    n = s // tm

    def body(x_r, t_r, g_r, dx_o, loss_o, dg_o, lacc, gacc):
        i = pl.program_id(0)

        @pl.when(i == 0)
        def _():
            lacc[...] = jnp.zeros_like(lacc)
            gacc[...] = jnp.zeros_like(gacc)

        xv = x_r[...]
        r = lax.rsqrt(jnp.mean(xv * xv, axis=-1, keepdims=True) + EPS)
        xn = xv * r
        e = xn * g_r[...] - t_r[...]
        lacc[...] += _fold8(e * e)
        dy = e * (1.0 / d)
        gacc[...] += _fold8(dy * xn)
        dxn = dy * g_r[...]
        dx_o[...] = r * (dxn - xn * jnp.mean(dxn * xn, axis=-1, keepdims=True))

        @pl.when(i == n - 1)
        def _():
            loss_o[...] = jnp.sum(lacc[...]).reshape(1, 1) * (0.5 / d)
            dg_o[...] = jnp.sum(gacc[...], axis=0, keepdims=True)

    return pl.pallas_call(
        body, name=name, grid=(n,),
        in_specs=[pl.BlockSpec((tm, d), lambda i: (i, 0)), pl.BlockSpec((tm, d), lambda i: (i, 0)),
                  pl.BlockSpec((1, d), lambda i: (0, 0))],
        out_specs=[pl.BlockSpec((tm, d), lambda i: (i, 0)), pl.BlockSpec((1, 1), lambda i: (0, 0)),
                   pl.BlockSpec((1, d), lambda i: (0, 0))],
        out_shape=[jax.ShapeDtypeStruct((s, d), F32), jax.ShapeDtypeStruct((1, 1), F32),
                   jax.ShapeDtypeStruct((1, d), F32)],
        scratch_shapes=[pltpu.VMEM((8, d), F32), pltpu.VMEM((8, d), F32)],
        compiler_params=_params("arbitrary"),
    )(x, target, gf.reshape(1, d))


def _tn(a, b, nb, name, rider=None):
    s, m = a.shape
    n = b.shape[1]
    nw = n // nb
    tmm = _tile(m, 1024, 128)
    tn = nw if nw <= 1152 else nw // 2
    per = nw // tn
    ts = _tile(s, 2048)
    ns = s // ts

    def body(a_r, b_r, o_r, acc):
        k = pl.program_id(2)

        @pl.when(k == 0)
        def _():
            acc[...] = jnp.zeros_like(acc)

        acc[...] += lax.dot_general(a_r[...].astype(BF16), b_r[...].astype(BF16), (((0,), (0,)), ((), ())),
                                    preferred_element_type=F32)

        @pl.when(k == ns - 1)
        def _():
            o_r[...] = acc[...]

    (out,), carried = _hosted_call(
        body, name=name, grid=(m // tmm, nb * per, ns),
        in_specs=[pl.BlockSpec((ts, tmm), lambda i, j, k: (k, i)), pl.BlockSpec((ts, tn), lambda i, j, k: (k, j))],
        out_specs=[pl.BlockSpec((None, tmm, tn), lambda i, j, k: (j // per, i, j % per))],
        out_shape=[jax.ShapeDtypeStruct((nb, m, nw), F32)],
        scratch_shapes=[pltpu.VMEM((tmm, tn), F32)], args=(a, b), rider=rider)
    return (out, carried) if rider is not None else out


def _mlp_bwd(dx2, f, x1, w2t, w1t, gm, name, rider=None):
    s, d = dx2.shape
    ff = f.shape[1]
    tf = _tile(ff, 1024, 128)
    nf = ff // tf
    tm = _tile(s, 1024)
    n = s // tm

    def body(dx2_r, f_r, x1_r, w2t_r, w1t_r, gm_r, df_o, dx1_o, dg_o, dxb, acc, gacc):
        i = pl.program_id(0)
        j = pl.program_id(1)

        @pl.when((i == 0) & (j == 0))
        def _():
            gacc[...] = jnp.zeros_like(gacc)

        @pl.when(j == 0)
        def _():
            dxb[...] = dx2_r[...].astype(BF16)
            acc[...] = jnp.zeros_like(acc)

        da = _dot(dxb[...], w2t_r[...])
        df = (da * (2.0 * jnp.maximum(f_r[...].astype(F32), 0.0))).astype(BF16)
        df_o[...] = df
        acc[...] += _dot(df, w1t_r[...])

        @pl.when(j == nf - 1)
        def _():
            dx, dg = _rms_bwd(acc[...], x1_r[...], gm_r[...])
            gacc[...] += dg
            dx1_o[...] = dx2_r[...] + dx

        @pl.when((i == n - 1) & (j == nf - 1))
        def _():
            dg_o[...] = jnp.sum(gacc[...], axis=0, keepdims=True)

    return _hosted_call(
        body, name=name, grid=(n, nf),
        in_specs=[pl.BlockSpec((tm, d), lambda i, j: (i, 0)), pl.BlockSpec((tm, tf), lambda i, j: (i, j)),
                  pl.BlockSpec((tm, d), lambda i, j: (i, 0)), pl.BlockSpec((d, tf), lambda i, j: (0, j)),
                  pl.BlockSpec((tf, d), lambda i, j: (j, 0)), pl.BlockSpec((1, d), lambda i, j: (0, 0))],
        out_specs=[pl.BlockSpec((tm, tf), lambda i, j: (i, j)), pl.BlockSpec((tm, d), lambda i, j: (i, 0)),
                   pl.BlockSpec((1, d), lambda i, j: (0, 0))],
        out_shape=[jax.ShapeDtypeStruct((s, ff), BF16), jax.ShapeDtypeStruct((s, d), F32),
                   jax.ShapeDtypeStruct((1, d), F32)],
        scratch_shapes=[pltpu.VMEM((tm, d), BF16), pltpu.VMEM((tm, d), F32), pltpu.VMEM((8, d), F32)],
        args=(dx2, f, x1, w2t, w1t, gm.reshape(1, d)), rider=rider)


def _branch_out_bwd(dx1, proj, ya, yb, yc, pa, sb, p, wot, woat, wobt, wpt, ps, name):
    s, d = dx1.shape
    gc = d // N_GROUPS
    tm = _tile(s, 512)
    n = s // tm

    def tile(col=0):
        return pl.BlockSpec((tm, d), lambda i: (i, col))

    def const(shape):
        return pl.BlockSpec(shape, lambda i: (0,) * len(shape))

    def once(shape):
        return pl.BlockSpec(shape, lambda i: (0,) * len(shape), pipeline_mode=pl.Buffered(1))

    def tn(a, b):
        return lax.dot_general(a, b, (((0,), (0,)), ((), ())), preferred_element_type=F32)

    def body(dx1_r, g0, g1, g2, ya_r, yb_r, yc_r, pa_r, sb_r, p_r, wot_r, woat_r, wobt_r, wpt_r, ps_r,
             dpa_o, ds_o, dp_o, dpj_o, dbob_o, dps_o, dbg_o, dwoa_o, dwob_o, dwp_o, sacc, gacc, wa_acc, wb_acc, wp_acc, sem):
        i = pl.program_id(0)

        @pl.when(i == 0)
        def _():
            sacc[...] = jnp.zeros_like(sacc)
            gacc[...] = jnp.zeros_like(gacc)
            wa_acc[...] = jnp.zeros_like(wa_acc)
            wb_acc[...] = jnp.zeros_like(wb_acc)
            wp_acc[...] = jnp.zeros_like(wp_acc)

        dm = _dot(dx1_r[...].astype(BF16), wot_r[...])
        ycp = _f32(yc_r)
        ys = (_f32(ya_r), _f32(yb_r), ycp * ps_r[...])
        dys = []
        for b, g_r in enumerate((g0, g1, g2)):
            sg = _sigmoid(_f32(g_r))
            dgt = dm * ys[b] * sg * (1.0 - sg)
            dpj_o[:, b * d:(b + 1) * d] = dgt.astype(BF16)
            gacc[b] += _fold8(dgt)
            dys.append(dm * sg)
        dya, dyb, dyc = dys
        sacc[0] += _fold8(dyb)
        sacc[1] += _fold8(dyc * ycp)
        dyab = dya.astype(BF16)
        dybb = dyb.astype(BF16)
        dycb = (dyc * ps_r[...]).astype(BF16)
        wa_acc[...] += tn(pa_r[...], dyab)
        wb_acc[...] += tn(sb_r[...], dybb)
        dpa_o[...] = _dot(dyab, woat_r[...]).astype(BF16)
        ds_o[...] = _dot(dybb, wobt_r[...]).astype(BF16)
        for g in range(N_GROUPS):
            cols = slice(g * gc, (g + 1) * gc)
            wp_acc[g] += tn(p_r[:, cols], dycb[:, cols])
            dp_o[:, cols] = _dot(dycb[:, cols], wpt_r[g]).astype(BF16)

        @pl.when(i == n - 1)
        def _():
            dbob_o[...] = jnp.sum(sacc[0], axis=0, keepdims=True)
            dps_o[...] = jnp.sum(sacc[1], axis=0, keepdims=True)
            for b in range(3):
                dbg_o[:, b * d:(b + 1) * d] = jnp.sum(gacc[b], axis=0, keepdims=True)
            copies = [pltpu.make_async_copy(src, dst, sem.at[j])
                      for j, (src, dst) in enumerate(((wa_acc, dwoa_o), (wb_acc, dwob_o), (wp_acc, dwp_o)))]
            for cp in copies:
                cp.start()
            for cp in copies:
                cp.wait()

    return pl.pallas_call(
        body, name=name, grid=(n,),
        in_specs=[tile(), tile(6), tile(7), tile(8), tile(), tile(), tile(), tile(), tile(), tile(),
                  once((d, d)), once((d, d)), once((d, d)), once((N_GROUPS, gc, gc)), const((1, d))],
        out_specs=[tile(), tile(), tile(), pl.BlockSpec((tm, 3 * d), lambda i: (i, 2)),
                   const((1, d)), const((1, d)), const((1, 3 * d)), ANY, ANY, ANY],
        out_shape=[jax.ShapeDtypeStruct((s, d), BF16)] * 3 + [jax.ShapeDtypeStruct((s, 9 * d), BF16)]
        + [jax.ShapeDtypeStruct((1, d), F32), jax.ShapeDtypeStruct((1, d), F32), jax.ShapeDtypeStruct((1, 3 * d), F32)]
        + [jax.ShapeDtypeStruct((d, d), F32), jax.ShapeDtypeStruct((d, d), F32), jax.ShapeDtypeStruct((N_GROUPS, gc, gc), F32)],
        scratch_shapes=[pltpu.VMEM((2, 8, d), F32), pltpu.VMEM((3, 8, d), F32), pltpu.VMEM((d, d), F32),
                        pltpu.VMEM((d, d), F32), pltpu.VMEM((N_GROUPS, gc, gc), F32), pltpu.SemaphoreType.DMA((3,))],
        compiler_params=_params("arbitrary"),
    )(dx1, proj, proj, proj, ya, yb, yc, pa, sb, p, wot, woat, wobt, wpt, ps.reshape(1, d))


def _ln_silu_bwd(v, ds, lg, lb, name, rider=None):
    s, d = v.shape
    tm = _tile(s, 256)
    n = s // tm

    def tile():
        return pl.BlockSpec((tm, d), lambda i: (i, 0))

    def vec():
        return pl.BlockSpec((1, d), lambda i: (0, 0))

    def body(v_r, ds_r, lg_r, lb_r, dv_o, dlg_o, dlb_o, dcb_o, acc):
        i = pl.program_id(0)

        @pl.when(i == 0)
        def _():
            acc[...] = jnp.zeros_like(acc)

        for r0 in range(0, tm, SUB):
            rows = slice(r0, r0 + SUB)
            vv = v_r[rows, :]
            mu = jnp.mean(vv, axis=-1, keepdims=True)
            vc = vv - mu
            rstd = lax.rsqrt(jnp.mean(vc * vc, axis=-1, keepdims=True) + EPS)
            nrm = vc * rstd
            ln = nrm * lg_r[...] + lb_r[...]
            sg = _sigmoid(ln)
            dln = ds_r[rows, :].astype(F32) * (sg * (1.0 + ln * (1.0 - sg)))
            acc[0] += _fold8(dln * nrm)
            acc[1] += _fold8(dln)
            dn = dln * lg_r[...]
            dv = rstd * (dn - jnp.mean(dn, axis=-1, keepdims=True)
                         - nrm * jnp.mean(dn * nrm, axis=-1, keepdims=True))
            acc[2] += _fold8(dv)
            dv_o[rows, :] = dv

        @pl.when(i == n - 1)
        def _():
            dlg_o[...] = jnp.sum(acc[0], axis=0, keepdims=True)
            dlb_o[...] = jnp.sum(acc[1], axis=0, keepdims=True)
            dcb_o[...] = jnp.sum(acc[2], axis=0, keepdims=True)

    return _hosted_call(
        body, name=name, grid=(n,),
        in_specs=[tile(), tile(), vec(), vec()],
        out_specs=[tile(), vec(), vec(), vec()],
        out_shape=[jax.ShapeDtypeStruct((s, d), F32)] + [jax.ShapeDtypeStruct((1, d), F32)] * 3,
        scratch_shapes=[pltpu.VMEM((3, 8, d), F32)],
        args=(v, ds, lg.reshape(1, d), lb.reshape(1, d)), rider=rider)


def _mix_bwd(proj, dpa, dv, dp, dproj, conv_a, conv_b, name):
    s, d9 = proj.shape
    d = d9 // 9
    gc = d // N_GROUPS
    tm = _tile(s, 256, HALO)
    n = s // tm
    nh = s // HALO

    def tile(col=0):
        return pl.BlockSpec((tm, d), lambda i: (i, col))

    def vec(rows):
        return pl.BlockSpec((rows, d), lambda i: (0, 0))

    def body(ab, ac, ax, bv, bg, ac_h, ax_h, bv_h, bg_h, dpa_r, dv_r, dp_r, ab_n, dpa_n, dv_n, dp_n, wa, wb, _alias,
             dpj_o, dbin_o, dca_o, dcb_o, ua_e, ub_e, dz_e, dv_e, q_e, bacc, cacc_a, cacc_b, ub_s, dv_s, *levels):
        i = pl.program_id(0)

        @pl.when(i == 0)
        def _():
            bacc[...] = jnp.zeros_like(bacc)
            cacc_a[...] = jnp.zeros_like(cacc_a)
            cacc_b[...] = jnp.zeros_like(cacc_b)

        keep_p = (i > 0).astype(F32)
        keep_n = (i < n - 1).astype(F32)
        ua_e[0:HALO, :] = _f32(ac_h) * _f32(ax_h) * keep_p
        ua_e[HALO:HALO + tm, :] = _f32(ac) * _f32(ax)
        ub_e[0:HALO, :] = _f32(bv_h) * _sigmoid(_f32(bg_h)) * keep_p
        ub_e[HALO:HALO + tm, :] = _f32(bv) * _sigmoid(_f32(bg))
        dz_e[0:tm, :] = _f32(dpa_r) * _f32(ab)
        dz_e[tm:tm + HALO, :] = _f32(dpa_n) * _f32(ab_n) * keep_n
        dv_e[0:tm, :] = dv_r[...]
        dv_e[tm:tm + HALO, :] = dv_n[...] * keep_n
        _shift_copies(ub_e, ub_s)
        _shift_copies(dv_e, dv_s)
        for g, w in enumerate(POOL_WINDOWS):
            cols = slice(g * gc, (g + 1) * gc)
            t_idx = i * tm + lax.broadcasted_iota(jnp.int32, (tm + HALO, gc), 0)
            cnt = jnp.minimum(t_idx + 1, w).astype(F32)
            q_e[0:tm, cols] = dp_r[:, cols].astype(F32) / cnt[0:tm]
            q_e[tm:tm + HALO, cols] = dp_n[:, cols].astype(F32) * keep_n / cnt[tm:tm + HALO]
        _window_sums(q_e, levels, tm + HALO, gc, trailing=False)
        for r0 in range(0, tm, SUB):
            rows = slice(r0, r0 + SUB)
            dz = dz_e[pl.ds(r0, SUB), :]
            z = None
            du = None
            for k in range(K_A):
                u_k = ua_e[pl.ds(HALO - (K_A - 1) + k + r0, SUB), :]
                t = wa[k:k + 1, :] * u_k
                z = t if z is None else z + t
                cacc_a[k] += _fold8(dz * u_k)
                t = wa[k:k + 1, :] * dz_e[pl.ds(r0 + (K_A - 1) - k, SUB), :]
                du = t if du is None else du + t
            outs = [dpa_r[rows, :].astype(F32) * z, du * ax[rows, :].astype(F32), du * ac[rows, :].astype(F32)]
            dvv = dv_e[pl.ds(r0, SUB), :]
            for k in range(K_B):
                cacc_b[k] += _fold8(dvv * _window(ub_e, ub_s, HALO - (K_B - 1) + k + r0, SUB))
            halves = []
            for h0 in range(0, SUB, SUB // 2):
                acc = None
                for k in range(K_B):
                    t = wb[k:k + 1, :] * _window(dv_e, dv_s, r0 + h0 + (K_B - 1) - k, SUB // 2)
                    acc = t if acc is None else acc + t
                halves.append(acc)
            du = jnp.concatenate(halves, axis=0)
            sg = _sigmoid(bg[rows, :].astype(F32))
            outs.append(du * sg)
            outs.append(du * bv[rows, :].astype(F32) * sg * (1.0 - sg))
            for b, o in enumerate(outs):
                dpj_o[rows, b * d:(b + 1) * d] = o.astype(BF16)
                bacc[b] += _fold8(o)
            for g, w in enumerate(POOL_WINDOWS):
                cols = slice(g * gc, (g + 1) * gc)
                o = levels[g][pl.ds(r0, SUB), cols] - dp_r[rows, cols].astype(F32)
                dpj_o[rows, 5 * d + g * gc:5 * d + (g + 1) * gc] = o.astype(BF16)
                bacc[5, :, cols] += _fold8(o)

        @pl.when(i == n - 1)
        def _():
            for b in range(6):
                dbin_o[:, b * d:(b + 1) * d] = jnp.sum(bacc[b], axis=0, keepdims=True)
            dca_o[...] = jnp.sum(cacc_a[...], axis=1)
            dcb_o[...] = jnp.sum(cacc_b[...], axis=1)

    ka8 = 8
    kb8 = 32
    return pl.pallas_call(
        body, name=name, grid=(n,),
        in_specs=[tile(0), tile(1), tile(2), tile(3), tile(4),
                  _prev_halo(tm, d, 1), _prev_halo(tm, d, 2), _prev_halo(tm, d, 3), _prev_halo(tm, d, 4),
                  tile(), tile(), tile(),
                  _next_halo(tm, d, 0, nh), _next_halo(tm, d, 0, nh), _next_halo(tm, d, 0, nh), _next_halo(tm, d, 0, nh),
                  vec(K_A), vec(K_B), ANY],
        out_specs=[pl.BlockSpec((tm, 6 * d), lambda i: (i, 0)), pl.BlockSpec((1, 6 * d), lambda i: (0, 0)),
                   pl.BlockSpec((ka8, d), lambda i: (0, 0)), pl.BlockSpec((kb8, d), lambda i: (0, 0))],
        out_shape=[jax.ShapeDtypeStruct((s, 9 * d), BF16), jax.ShapeDtypeStruct((1, 6 * d), F32),
                   jax.ShapeDtypeStruct((ka8, d), F32), jax.ShapeDtypeStruct((kb8, d), F32)],
        scratch_shapes=[pltpu.VMEM((HALO + tm, d), F32), pltpu.VMEM((HALO + tm, d), F32),
                        pltpu.VMEM((tm + HALO, d), F32), pltpu.VMEM((tm + HALO, d), F32),
                        pltpu.VMEM((tm + HALO, d), F32),
                        pltpu.VMEM((6, 8, d), F32), pltpu.VMEM((ka8, 8, d), F32), pltpu.VMEM((kb8, 8, d), F32),
                        pltpu.VMEM((7, tm + HALO - 8, d), F32), pltpu.VMEM((7, tm + HALO - 8, d), F32)]
        + [pltpu.VMEM((tm + HALO, d), F32)] * N_GROUPS,
        input_output_aliases={18: 0},
        compiler_params=_params("arbitrary"),
    )(proj, proj, proj, proj, proj, proj, proj, proj, proj, dpa, dv, dp, proj, dpa, dv, dp, conv_a, conv_b, dproj)


def _dh_rms_bwd(dproj, wint, x0, dx1, g, name, rider=None):
    s, p = dproj.shape
    d = x0.shape[1]
    tm = _tile(s, 1024)
    tk = _tile(p, 2304, 128)
    n = s // tm
    nk = p // tk

    def body(dp_r, w_r, x_r, dx1_r, g_r, dx_o, dg_o, acc, gacc):
        i = pl.program_id(0)
        k = pl.program_id(1)

        @pl.when((i == 0) & (k == 0))
        def _():
            gacc[...] = jnp.zeros_like(gacc)

        @pl.when(k == 0)
        def _():
            acc[...] = jnp.zeros_like(acc)

        acc[...] += _dot(dp_r[...], w_r[...])

        @pl.when(k == nk - 1)
        def _():
            dx, dg = _rms_bwd(acc[...], x_r[...], g_r[...])
            gacc[...] += dg
            dx_o[...] = dx1_r[...] + dx

        @pl.when((i == n - 1) & (k == nk - 1))
        def _():
            dg_o[...] = jnp.sum(gacc[...], axis=0, keepdims=True)

    return _hosted_call(
        body, name=name, grid=(n, nk),
        in_specs=[pl.BlockSpec((tm, tk), lambda i, k: (i, k)), pl.BlockSpec((tk, d), lambda i, k: (k, 0)),
                  pl.BlockSpec((tm, d), lambda i, k: (i, 0)), pl.BlockSpec((tm, d), lambda i, k: (i, 0)),
                  pl.BlockSpec((1, d), lambda i, k: (0, 0))],
        out_specs=[pl.BlockSpec((tm, d), lambda i, k: (i, 0)), pl.BlockSpec((1, d), lambda i, k: (0, 0))],
        out_shape=[jax.ShapeDtypeStruct((s, d), F32), jax.ShapeDtypeStruct((1, d), F32)],
        scratch_shapes=[pltpu.VMEM((tm, d), F32), pltpu.VMEM((8, d), F32)],
        args=(dproj, wint, x0, dx1, g.reshape(1, d)), rider=rider)


EARLY = ("w_out_a", "w_out_b", "w_pool", "w_o", "w_mlp1", "w_mlp2")


def _layer_fwd(x0, l, g_mix, win3, b_in, weights, proj_rider=None, bo_rider=None, mlp_rider=None, h=None, g_next=None):
    if h is None:
        h, _ = _rms_fwd(x0, g_mix, f"rms_fwd_{l}")
    proj, carried_proj = _proj(h, win3, b_in, f"proj_{l}", proj_rider)
    w = weights(carried_proj) if callable(weights) else weights
    pa, sb, p, v = _mix_fwd(proj, w["conv_a"], w["conv_b"], w["conv_b_bias"], w["ln_b_g"], w["ln_b_b"], f"mix_fwd_{l}")
    (ya, yb, yc, mg, x1, h2), carried_bo = _branch_out(pa, sb, p, proj, x0, w["woa"], w["wob"], w["wp"], w["wo"],
                                                       w["b_out_b"], w["pool_scale"], w["g_mlp"], f"branch_out_{l}", bo_rider)
    rider = mlp_rider(carried_bo) if callable(mlp_rider) else mlp_rider
    (f, a, x2, *h_next), carried_mlp = _mlp_fwd(h2, x1, w["w1_3"], w["w2"], g_next, f"mlp_fwd_{l}", rider)
    saved = dict(x0=x0, h=h, proj=proj, pa=pa, sb=sb, p=p, v=v, ya=ya, yb=yb, yc=yc, mg=mg, x1=x1, h2=h2, f=f, a=a)
    return x2, saved, w, carried_mlp, (h_next[0] if h_next else None)


def _layer_bwd_early(dx2, w, sv, l, mlp_rider=None, dw2_rider=None, dw1_rider=None, swap_early=False):
    d = dx2.shape[1]
    (df, dx1, dgm), carried_mlp = _mlp_bwd(dx2, sv["f"], sv["x1"], w["w2t"], w["w1t"], w["g_mlp"], f"mlp_bwd_{l}",
                                           mlp_rider)
    if dw2_rider is None:
        dw2, carried_dw2 = _tn(sv["a"], dx2, 1, f"dw2_{l}"), []
    else:
        dw2, carried_dw2 = _tn(sv["a"], dx2, 1, f"dw2_{l}", dw2_rider)
    if dw1_rider is None:
        dw1, carried_dw1 = _tn(sv["h2"], df, N_CHIPS, f"dw1_{l}"), []
    else:
        dw1, carried_dw1 = _tn(sv["h2"], df, N_CHIPS, f"dw1_{l}", dw1_rider)
    dpa, ds, dp, dproj, dbob, dps, dbg, dwoa, dwob, dwp = _branch_out_bwd(
        dx1, sv["proj"], sv["ya"], sv["yb"], sv["yc"], sv["pa"], sv["sb"], sv["p"], w["wot"], w["woat"], w["wobt"],
        w["wpt"], w["pool_scale"], f"branch_out_bwd_{l}")
    dwo = _tn(sv["mg"], dx1, 1, f"dwo_{l}")
    ff = dw2.shape[1]
    gc = d // N_GROUPS
    big = dict(
        w_out_a=dwoa.reshape(N_CHIPS, d // N_CHIPS, d),
        w_out_b=dwob.reshape(N_CHIPS, d // N_CHIPS, d),
        w_pool=dwp.reshape(N_GROUPS, N_CHIPS, gc // N_CHIPS, gc).transpose(1, 0, 2, 3).reshape(N_CHIPS, gc, gc),
        w_o=dwo.reshape(N_CHIPS, d // N_CHIPS, d),
        w_mlp1=dw1,
        w_mlp2=dw2.reshape(N_CHIPS, ff // N_CHIPS, d),
    )
    swap = _ride_swap([big[k] for k in EARLY]) if swap_early else None
    (dv, dlg, dlb, dcb), swapped = _ln_silu_bwd(sv["v"], ds, w["ln_b_g"], w["ln_b_b"], f"ln_silu_bwd_{l}", swap)
    dproj, dbm, dca, dcvb = _mix_bwd(sv["proj"], dpa, dv, dp, dproj, w["conv_a"], w["conv_b"], f"mix_bwd_{l}")
    small = dict(
        b_in=jnp.concatenate([dbm, dbg], axis=1).reshape(9, d), conv_a=dca[:K_A], conv_b=dcvb[:K_B],
        conv_b_bias=dcb, ln_b_g=dlg, ln_b_b=dlb, b_out_b=dbob, pool_scale=dps, g_mlp=dgm,
    )
    return dx1, dproj, big, small, carried_mlp + carried_dw2 + carried_dw1, swapped


def _layer_bwd_late(dx1, dproj, w, sv, l, dwin_rider=None, dh_rider=None):
    if dwin_rider is None:
        dwin, carried_dwin = _tn(sv["h"], dproj, N_CHIPS, f"dwin_{l}"), []
    else:
        dwin, carried_dwin = _tn(sv["h"], dproj, N_CHIPS, f"dwin_{l}", dwin_rider)
    rider = dh_rider(dwin) if callable(dh_rider) else dh_rider
    (dx0, dgmix), carried_dh = _dh_rms_bwd(dproj, w["wint"], sv["x0"], dx1, w["g_mix"], f"dh_{l}", rider)
    return dx0, dwin, dgmix, carried_dwin, carried_dh


def _place():
    x, y, c = lax.axis_index("x"), lax.axis_index("y"), lax.axis_index("c")
    chips = [(1 - x, y), (x, 1 - y), (1 - x, 1 - y)]
    return x, y, c, chips


def _place_shard(w, chip_arr, dtype, name):
    rows, cols = w.shape
    tr = _tile(rows, max(16, (1 << 19) // cols), 16)

    def body(k_ref, w_r, o_r):
        o_r[...] = w_r[...].astype(dtype)

    return pl.pallas_call(
        body, name=name,
        grid_spec=pltpu.PrefetchScalarGridSpec(
            num_scalar_prefetch=1, grid=(rows // tr,),
            in_specs=[pl.BlockSpec((tr, cols), lambda i, k_ref: (i, 0))],
            out_specs=pl.BlockSpec((None, tr, cols), lambda i, k_ref: (k_ref[0], i, 0))),
        out_shape=jax.ShapeDtypeStruct((N_CHIPS, rows, cols), dtype),
        compiler_params=_params("parallel"),
    )(chip_arr, w)


def _ride_rows(v):
    def copy(ins, outs, sems, m, src_dev):
        x, y, c = lax.axis_index("x"), lax.axis_index("y"), lax.axis_index("c")
        peer = (x ^ ((m >> 2) & 1), y ^ ((m >> 1) & 1), c ^ (m & 1))
        return pltpu.make_async_remote_copy(
            src_ref=ins[0], dst_ref=outs[0].at[src_dev], send_sem=sems[0].at[m - 1], recv_sem=sems[1].at[m - 1],
            device_id=peer, device_id_type=MESH)

    def me():
        return 4 * lax.axis_index("x") + 2 * lax.axis_index("y") + lax.axis_index("c")

    def start(ins, outs, sems):
        pltpu.make_async_copy(ins[0], outs[0].at[me()], sems[2]).start()
        for m in range(1, N_DEV):
            copy(ins, outs, sems, m, me()).start()

    def finish(ins, outs, sems):
        for m in range(1, N_DEV):
            copy(ins, outs, sems, m, me() ^ m).wait_recv()
        for m in range(1, N_DEV):
            copy(ins, outs, sems, m, me()).wait_send()
        pltpu.make_async_copy(ins[0], outs[0].at[me()], sems[2]).wait()

    return _Rider([v], [jax.ShapeDtypeStruct((N_DEV, *v.shape), v.dtype)], [],
                  [pltpu.SemaphoreType.DMA((N_DEV - 1,)), pltpu.SemaphoreType.DMA((N_DEV - 1,)), pltpu.SemaphoreType.DMA],
                  start, finish)


def _ride_swap(grads):
    n = len(grads)

    def copy(ins, outs, sems, a, k):
        x, y, c, _ = _place()
        h = ins[a].shape[1] // 2
        return pltpu.make_async_remote_copy(
            src_ref=ins[a].at[k, pl.ds(pl.multiple_of((1 - c) * h, 8), h)], dst_ref=outs[a].at[k],
            send_sem=sems[0].at[a * N_CHIPS + k], recv_sem=sems[1].at[a * N_CHIPS + k], device_id=(x, y, 1 - c),
            device_id_type=MESH)

    def start(ins, outs, sems):
        for a in range(n):
            for k in range(N_CHIPS):
                copy(ins, outs, sems, a, k).start()

    def finish(ins, outs, sems):
        for a in range(n):
            for k in range(N_CHIPS):
                copy(ins, outs, sems, a, k).wait()

    return _Rider(grads, [jax.ShapeDtypeStruct((N_CHIPS, g.shape[1] // 2, g.shape[2]), g.dtype) for g in grads], [],
                  [pltpu.SemaphoreType.DMA((N_CHIPS * n,))] * 2, start, finish)


def _add_halves(g, r, c_arr, name):
    nk, rows, cols = g.shape
    h = rows // 2
    tr = _tile(h, max(8, (1 << 19) // cols), 8)
    nblk = h // tr

    def body(c_ref, g_r, r_r, o_r):
        o_r[...] = (g_r[...] + r_r[...]).astype(BF16)

    return pl.pallas_call(
        body, name=name,
        grid_spec=pltpu.PrefetchScalarGridSpec(
            num_scalar_prefetch=1, grid=(nk, nblk),
            in_specs=[pl.BlockSpec((None, tr, cols), lambda k, i, c_ref: (k, c_ref[0] * nblk + i, 0)),
                      pl.BlockSpec((None, tr, cols), lambda k, i, c_ref: (k, i, 0))],
            out_specs=pl.BlockSpec((None, tr, cols), lambda k, i, c_ref: (k, i, 0))),
        out_shape=jax.ShapeDtypeStruct((nk, h, cols), BF16),
        compiler_params=_params("parallel", "parallel"),
    )(c_arr, g, r)


def _ride_exchange(parts):
    n = len(parts)

    def copy(ins, outs, sems, a, j):
        x, y, c, chips = _place()
        px, py = chips[j]
        return pltpu.make_async_remote_copy(
            src_ref=ins[a].at[2 * px + py], dst_ref=outs[a].at[j], send_sem=sems[0].at[a * 3 + j],
            recv_sem=sems[1].at[a * 3 + j], device_id=(px, py, c), device_id_type=MESH)

    def start(ins, outs, sems):
        for a in range(n):
            for j in range(3):
                copy(ins, outs, sems, a, j).start()

    def finish(ins, outs, sems):
        for a in range(n):
            for j in range(3):
                copy(ins, outs, sems, a, j).wait()

    return _Rider(parts, [jax.ShapeDtypeStruct((3, *p.shape[1:]), p.dtype) for p in parts], [],
                  [pltpu.SemaphoreType.DMA((3 * n,))] * 2, start, finish)


def _ride_gather(placed=(), landed=(), forward=True):
    placed, landed = list(placed), list(landed)
    n_ici = len(placed)
    fwd = (list(range(n_ici)) if forward else []) + list(range(n_ici, n_ici + len(landed)))

    def ici(ins, outs, sems, a, j, src_chip):
        x, y, c, chips = _place()
        h = ins[a].shape[1] // 2
        rows = pl.ds(pl.multiple_of(c * h, 8), h)
        return pltpu.make_async_remote_copy(
            src_ref=ins[a].at[2 * x + y, rows], dst_ref=outs[a].at[src_chip, rows], send_sem=sems[0].at[a * 3 + j],
            recv_sem=sems[1].at[a * 3 + j], device_id=(*chips[j], c), device_id_type=MESH)

    def d2d(ins, outs, sems, a, j, which):
        x, y, c, chips = _place()
        px, py = chips[j]
        h = ins[a].shape[1] // 2
        rows = pl.ds(pl.multiple_of(which * h, 8), h)
        return pltpu.make_async_remote_copy(
            src_ref=ins[a].at[2 * px + py, rows], dst_ref=outs[a].at[2 * px + py, rows], send_sem=sems[2].at[a * 3 + j],
            recv_sem=sems[3].at[a * 3 + j], device_id=(x, y, 1 - c), device_id_type=MESH)

    def start(ins, outs, sems):
        x, y, c, _ = _place()
        for a in range(n_ici):
            for j in range(3):
                ici(ins, outs, sems, a, j, 2 * x + y).start()
        for a in range(n_ici, n_ici + len(landed)):
            for j in range(3):
                d2d(ins, outs, sems, a, j, c).start()

    def land(ins, outs, sems, then_forward):
        _, _, c, chips = _place()
        for a in range(n_ici):
            for j, (px, py) in enumerate(chips):
                ici(ins, outs, sems, a, j, 2 * px + py).wait_recv()
                if then_forward:
                    d2d(ins, outs, sems, a, j, c).start()

    def middle(ins, outs, sems):
        land(ins, outs, sems, True)

    def finish(ins, outs, sems):
        x, y, c, _ = _place()
        if not forward:
            land(ins, outs, sems, False)
        for a in fwd:
            for j in range(3):
                d2d(ins, outs, sems, a, j, 1 - c).wait_recv()
        for a in range(n_ici):
            for j in range(3):
                ici(ins, outs, sems, a, j, 2 * x + y).wait_send()
        for a in fwd:
            for j in range(3):
                d2d(ins, outs, sems, a, j, c).wait_send()

    arrays = placed + landed
    n = len(arrays)
    return _Rider(arrays, [jax.ShapeDtypeStruct(p.shape, p.dtype) for p in arrays], [(a, a) for a in range(n)],
                  [pltpu.SemaphoreType.DMA((3 * n,))] * 4, start, finish, middle if (n_ici and forward) else None)


def _sum_chips(r, name):
    nk, rows, cols = r.shape
    tr = _tile(rows, max(8, (1 << 19) // cols), 8)

    def body(r_r, o_r):
        acc = r_r[0].astype(F32)
        for k in range(1, nk):
            acc = acc + r_r[k].astype(F32)
        o_r[...] = acc

    return pl.pallas_call(
        body, name=name, grid=(rows // tr,),
        in_specs=[pl.BlockSpec((nk, tr, cols), lambda i: (0, i, 0))],
        out_specs=pl.BlockSpec((tr, cols), lambda i: (i, 0)),
        out_shape=jax.ShapeDtypeStruct((rows, cols), F32),
        compiler_params=_params("parallel"),
    )(r)


def _sum_chips_into(own, arrived, dst, layer, n_layers, place_arr, name):
    _, h, cols = own.shape
    tr = _tile(h, max(8, (1 << 19) // cols), 8)
    nblk = h // tr

    def body(p_ref, own_r, arr_r, *rest):
        o_r = rest[-1]
        acc = own_r[...].astype(F32)
        for j in range(3):
            acc = acc + arr_r[j].astype(F32)
        o_r[...] = acc

    in_specs = [pl.BlockSpec((None, tr, cols), lambda i, p_ref: (p_ref[0], i, 0)),
                pl.BlockSpec((3, tr, cols), lambda i, p_ref: (0, i, 0))]
    args = [place_arr, own, arrived]
    aliases = {}
    if dst is not None:
        in_specs.append(ANY)
        args.append(dst)
        aliases = {3: 0}
    return pl.pallas_call(
        body, name=name,
        grid_spec=pltpu.PrefetchScalarGridSpec(
            num_scalar_prefetch=1, grid=(nblk,), in_specs=in_specs,
            out_specs=pl.BlockSpec((None, tr, cols), lambda i, p_ref: (layer, p_ref[1] * nblk + i, 0))),
        out_shape=jax.ShapeDtypeStruct((n_layers, 2 * h, cols), F32),
        input_output_aliases=aliases,
        compiler_params=_params("parallel"),
    )(*args)


def _ride_share(shards):
    n = len(shards)
    n_layers = shards[0].shape[0]

    def copy(ins, outs, sems, a, l, which):
        x, y, c, _ = _place()
        h = ins[a].shape[1] // 2
        rows = pl.ds(pl.multiple_of(which * h, 8), h)
        return pltpu.make_async_remote_copy(
            src_ref=ins[a].at[l, rows], dst_ref=outs[a].at[l, rows], send_sem=sems[0].at[a * n_layers + l],
            recv_sem=sems[1].at[a * n_layers + l], device_id=(x, y, 1 - c), device_id_type=MESH)

    def start(ins, outs, sems):
        c = lax.axis_index("c")
        for a in range(n):
            for l in range(n_layers):
                copy(ins, outs, sems, a, l, c).start()

    def finish(ins, outs, sems):
        c = lax.axis_index("c")
        for a in range(n):
            for l in range(n_layers):
                copy(ins, outs, sems, a, l, 1 - c).wait_recv()
        for a in range(n):
            for l in range(n_layers):
                copy(ins, outs, sems, a, l, c).wait_send()

    return _Rider(shards, [jax.ShapeDtypeStruct(a.shape, a.dtype) for a in shards], [(a, a) for a in range(n)],
                  [pltpu.SemaphoreType.DMA((n * n_layers,))] * 2, start, finish)


def _join_riders(first, second):
    k_in, k_out, k_sem = len(first.args), len(first.out_shape), len(first.sems)

    def start(ins, outs, sems):
        first.start(ins[:k_in], outs[:k_out], sems[:k_sem])
        second.start(ins[k_in:], outs[k_out:], sems[k_sem:])

    def finish(ins, outs, sems):
        first.finish(ins[:k_in], outs[:k_out], sems[:k_sem])
        second.finish(ins[k_in:], outs[k_out:], sems[k_sem:])

    return _Rider(first.args + second.args, first.out_shape + second.out_shape,
                  first.aliases + [(i + k_in, o + k_out) for i, o in second.aliases], first.sems + second.sems, start, finish)


def _adamw(w, g, m, v, name):
    rows, cols = w.shape
    tr = _tile(rows, max(8, (1 << 18) // cols), 8)
    c1 = 1.0 - ADAM_B1 ** ADAM_STEP
    c2 = 1.0 - ADAM_B2 ** ADAM_STEP

    def body(w_r, g_r, m_r, v_r, d_o, m_o, v_o):
        gv = g_r[...]
        mn = ADAM_B1 * m_r[...] + (1.0 - ADAM_B1) * gv
        vn = ADAM_B2 * v_r[...] + (1.0 - ADAM_B2) * (gv * gv)
        m_o[...] = mn
        v_o[...] = vn
        d_o[...] = -ADAM_LR * ((mn / c1) / (jnp.sqrt(vn / c2) + ADAM_EPS) + ADAM_WD * w_r[...])

    spec = pl.BlockSpec((tr, cols), lambda i: (i, 0))
    return pl.pallas_call(
        body, name=name, grid=(rows // tr,), in_specs=[spec] * 4, out_specs=[spec] * 3,
        out_shape=[jax.ShapeDtypeStruct((rows, cols), F32)] * 3,
        compiler_params=_params("parallel"),
    )(w, g, m, v)


BIG = ("w_in", "w_out_a", "w_out_b", "w_pool", "w_o", "w_mlp1", "w_mlp2")
SMALL = ("g_mix", "b_in", "conv_b_bias", "ln_b_g", "ln_b_b", "b_out_b", "pool_scale", "g_mlp")
CONVS = ("conv_a", "conv_b")
WEIGHTS = ("g_mix", "w_in", "b_in", "conv_a", "w_out_a", "conv_b", "conv_b_bias", "ln_b_g", "ln_b_b", "w_out_b", "b_out_b",
           "w_pool", "pool_scale", "w_o", "g_mlp", "w_mlp1", "w_mlp2", "g_final")


def _as2d(a):
    return a.reshape(-1, a.shape[-1])


def _pad_rows(a, rows):
    return jnp.pad(a, ((0, rows - a.shape[0]), (0, 0)))


def _full_weights(big_g, conv_g, rep, l, d):
    gc = d // N_GROUPS
    w = {k: v[l] for k, v in rep.items()}
    w["b_in"] = rep["b_in"][l].reshape(1, -1)
    win3 = big_g["w_in"]
    w["win3"] = win3
    w["wint"] = win3.transpose(0, 2, 1).reshape(-1, d)
    for src, dst in (("w_out_a", "woa"), ("w_out_b", "wob"), ("w_o", "wo")):
        full = big_g[src].reshape(d, d)
        w[dst] = full
        w[dst + "t"] = full.T
    wp = big_g["w_pool"].reshape(N_CHIPS, N_GROUPS, gc // N_CHIPS, gc).transpose(1, 0, 2, 3).reshape(N_GROUPS, gc, gc)
    w["wp"] = wp
    w["wpt"] = wp.transpose(0, 2, 1)
    w1_3 = big_g["w_mlp1"]
    w["w1_3"] = w1_3
    w["w1t"] = w1_3.transpose(0, 2, 1).reshape(-1, d)
    w2 = big_g["w_mlp2"].reshape(-1, d)
    w["w2"] = w2
    w["w2t"] = w2.T
    ca, cb = conv_g
    w["conv_a"] = ca
    w["conv_b"] = cb
    return w


def kernel(x, g_mix, w_in, b_in, conv_a, w_out_a, conv_b, conv_b_bias, ln_b_g, ln_b_b, w_out_b, b_out_b, w_pool, pool_scale, w_o, g_mlp, w_mlp1, w_mlp2, g_final, loss_target, m_g_mix, m_w_in, m_b_in, m_conv_a, m_w_out_a, m_conv_b, m_conv_b_bias, m_ln_b_g, m_ln_b_b, m_w_out_b, m_b_out_b, m_w_pool, m_pool_scale, m_w_o, m_g_mlp, m_w_mlp1, m_w_mlp2, m_g_final, v_g_mix, v_w_in, v_b_in, v_conv_a, v_w_out_a, v_conv_b, v_conv_b_bias, v_ln_b_g, v_ln_b_b, v_w_out_b, v_b_out_b, v_w_pool, v_pool_scale, v_w_o, v_g_mlp, v_w_mlp1, v_w_mlp2, v_g_final):
    given = dict(g_mix=g_mix, w_in=w_in, b_in=b_in, conv_a=conv_a, w_out_a=w_out_a, conv_b=conv_b, conv_b_bias=conv_b_bias, ln_b_g=ln_b_g, ln_b_b=ln_b_b, w_out_b=w_out_b, b_out_b=b_out_b, w_pool=w_pool, pool_scale=pool_scale, w_o=w_o, g_mlp=g_mlp, w_mlp1=w_mlp1, w_mlp2=w_mlp2, g_final=g_final)
    mom = dict(g_mix=m_g_mix, w_in=m_w_in, b_in=m_b_in, conv_a=m_conv_a, w_out_a=m_w_out_a, conv_b=m_conv_b, conv_b_bias=m_conv_b_bias, ln_b_g=m_ln_b_g, ln_b_b=m_ln_b_b, w_out_b=m_w_out_b, b_out_b=m_b_out_b, w_pool=m_w_pool, pool_scale=m_pool_scale, w_o=m_w_o, g_mlp=m_g_mlp, w_mlp1=m_w_mlp1, w_mlp2=m_w_mlp2, g_final=m_g_final)
    var = dict(g_mix=v_g_mix, w_in=v_w_in, b_in=v_b_in, conv_a=v_conv_a, w_out_a=v_w_out_a, conv_b=v_conv_b, conv_b_bias=v_conv_b_bias, ln_b_g=v_ln_b_g, ln_b_b=v_ln_b_b, w_out_b=v_w_out_b, b_out_b=v_b_out_b, w_pool=v_w_pool, pool_scale=v_pool_scale, w_o=v_w_o, g_mlp=v_g_mlp, w_mlp1=v_w_mlp1, w_mlp2=v_w_mlp2, g_final=v_g_final)
    n_layers = w_in.shape[0]
    s, d = x.shape[1], x.shape[2]
    dq = d // N_CHIPS
    x_idx, y_idx, c_idx = lax.axis_index("x"), lax.axis_index("y"), lax.axis_index("c")
    chip = 2 * x_idx + y_idx
    c_arr = c_idx.astype(jnp.int32).reshape(1)

    conv_rows = n_layers * (K_A + K_B)
    conv_pad = -(-conv_rows // 16) * 16
    conv_pack = _pad_rows(jnp.concatenate([conv_a[l] for l in range(n_layers)] + [conv_b[l] for l in range(n_layers)], axis=0),
                          conv_pad)
    chip_arr = chip.astype(jnp.int32).reshape(1)
    place_arr = jnp.stack([chip, c_idx]).astype(jnp.int32)

    def placed(k, l):
        return _place_shard(_as2d(given[k][l]), chip_arr, BF16, f"place_{k}_{l}")

    rest = BIG[1:]
    h, (win_g, conv_all) = _rms_fwd(
        x[0], given["g_mix"][0], "rms_fwd_0",
        _ride_gather([placed("w_in", 0), _place_shard(conv_pack, chip_arr, F32, "place_convs")]))
    conv_full = conv_all.transpose(1, 0, 2).reshape(conv_pad, d)
    conv_g = [(conv_full[l * K_A:(l + 1) * K_A], conv_full[n_layers * K_A + l * K_B:n_layers * K_A + (l + 1) * K_B])
              for l in range(n_layers)]
    rep = {k: given[k] for k in SMALL}

    xl = x[0]
    ws, saved = [], []
    rest_g = None
    for l in range(n_layers):
        more = l + 1 < n_layers

        def weights(carried, l=l, win_g=win_g, rest_g=rest_g):
            return _full_weights(dict(zip(rest, carried if rest_g is None else rest_g), w_in=win_g), conv_g[l], rep, l, d)

        xl, sv, w, got, h = _layer_fwd(
            xl, l, given["g_mix"][l], win_g, given["b_in"][l].reshape(1, -1), weights,
            _ride_gather([placed(k, l) for k in rest]) if rest_g is None else None,
            _ride_gather([placed("w_in", l + 1)], forward=False) if more else None,
            (lambda landed, l=l: _ride_gather([placed(k, l + 1) for k in rest], landed)) if more else None,
            h, given["g_mix"][l + 1] if more else None)
        if more:
            rest_g, win_g = got[:len(rest)], got[len(rest)]
        ws.append(w)
        saved.append(sv)
    dx, loss, dgf = _loss_head(xl, loss_target[0], g_final, "loss_head")
    loss = lax.psum(loss[0, 0], ("x", "y", "c"))

    parts, arrived, small = {}, {}, [None] * n_layers
    pending = []
    for l in reversed(range(n_layers)):
        w, sv = ws[l], saved[l]
        hosts = [[q for q in pending if q[1] not in ("w_mlp1", "w_mlp2")],
                 [q for q in pending if q[1] == "w_mlp2"],
                 [q for q in pending if q[1] == "w_mlp1"]]
        dx1, dproj, early, sm, got, swapped = _layer_bwd_early(
            dx, w, sv, l, *[_ride_exchange([parts[q] for q in qs]) if qs else None for qs in hosts], swap_early=(l == 0))
        arrived.update(zip(hosts[0] + hosts[1] + hosts[2], got))
        if l > 0:
            order = ("w_in",) + EARLY
            dx, dwin, dgmix, _, swapped = _layer_bwd_late(
                dx1, dproj, w, sv, l, None, lambda dwin, early=early: _ride_swap([dwin] + [early[k] for k in EARLY]))
            full = dict(early, w_in=dwin)
            for k, r in zip(order, swapped):
                parts[(l, k)] = _add_halves(full[k], r, c_arr, f"grad_add_halves_{k}_{l}")
            pending = [(l, k) for k in order]
        else:
            for k, r in zip(EARLY, swapped):
                parts[(l, k)] = _add_halves(early[k], r, c_arr, f"grad_add_halves_{k}_{l}")

            def w_in_rider(dwin, l=l):
                (r,) = _comm_call(_ride_swap([dwin]), f"grad_swap_w_in_{l}")
                parts[(l, "w_in")] = _add_halves(dwin, r, c_arr, f"grad_add_halves_w_in_{l}")
                return _ride_exchange([parts[(l, "w_in")]])

            dx, dwin, dgmix, got_early, got_w_in = _layer_bwd_late(
                dx1, dproj, w, sv, l, _ride_exchange([parts[(l, k)] for k in EARLY]), w_in_rider)
            arrived.update(zip([(l, k) for k in EARLY], got_early))
            arrived[(l, "w_in")] = got_w_in[0]
            pending = []
        sm["g_mix"] = dgmix
        small[l] = sm
    reduced = []
    for k in BIG:
        dst = None
        for l in range(n_layers):
            dst = _sum_chips_into(parts[(l, k)], arrived[(l, k)], dst, l, n_layers, place_arr, f"grad_sum_chips_{k}_{l}")
        reduced.append(dst)

    small_rows = []
    for k in SMALL + CONVS:
        for l in range(n_layers):
            small_rows.append(small[l][k])
    small_rows.append(dgf)
    pack = jnp.concatenate(small_rows, axis=0)
    n_small = pack.shape[0]
    pack = _pad_rows(pack, -(-n_small // 8) * 8)
    *shared, small_all = _comm_call(_join_riders(_ride_share(reduced), _ride_rows(pack)), "grad_share_and_small")
    big_grad = dict(zip(BIG, shared))
    small_sum = _sum_chips(small_all, "sum_small_grads")

    out = {}
    for k in BIG:
        shape = given[k].shape
        g2 = big_grad[k].reshape(-1, shape[-1])
        dlt, nm, nv = _adamw(given[k].reshape(g2.shape), g2, mom[k].reshape(g2.shape), var[k].reshape(g2.shape), f"adamw_{k}")
        out[k] = tuple(a.reshape(shape) for a in (g2, dlt, nm, nv))

    def rows_of(k):
        return {"b_in": 9, "conv_a": K_A, "conv_b": K_B}.get(k, 1)

    def pack_rep(src):
        rows = [src[k].reshape(n_layers * rows_of(k), d) for k in SMALL] + [src["g_final"].reshape(1, d)]
        a = jnp.concatenate(rows, axis=0)
        return _pad_rows(a, -(-a.shape[0] // 8) * 8)

    def pack_conv(src):
        a = jnp.concatenate([src[k].reshape(n_layers * rows_of(k), dq) for k in CONVS], axis=0)
        return _pad_rows(a, -(-a.shape[0] // 8) * 8)

    offs = {}
    r = 0
    for k in SMALL + CONVS:
        offs[k] = r
        r += n_layers * rows_of(k)
    offs["g_final"] = r
    n_rep = sum(n_layers * rows_of(k) for k in SMALL)
    g_rep = jnp.concatenate([small_sum[:n_rep], small_sum[offs["g_final"]:offs["g_final"] + 1]], axis=0)
    g_rep = _pad_rows(g_rep, -(-g_rep.shape[0] // 8) * 8)
    g_conv_full = small_sum[offs["conv_a"]:offs["g_final"]]
    g_conv = lax.dynamic_slice_in_dim(g_conv_full, chip * dq, dq, axis=1)
    g_conv = _pad_rows(g_conv, -(-g_conv.shape[0] // 8) * 8)
    rep_res = (g_rep,) + tuple(_adamw(pack_rep(given), g_rep, pack_rep(mom), pack_rep(var), "adamw_small"))
    conv_res = (g_conv,) + tuple(_adamw(pack_conv(given), g_conv, pack_conv(mom), pack_conv(var), "adamw_conv"))
    r = 0
    for k in SMALL:
        nr = n_layers * rows_of(k)
        out[k] = tuple(a[r:r + nr].reshape(given[k].shape) for a in rep_res)
        r += nr
    out["g_final"] = tuple(a[r].reshape(given["g_final"].shape) for a in rep_res)
    r = 0
    for k in CONVS:
        nr = n_layers * rows_of(k)
        out[k] = tuple(a[r:r + nr].reshape(given[k].shape) for a in conv_res)
        r += nr

    res = [loss, dx.reshape(x.shape)]
    for i in range(4):
        res += [out[k][i] for k in WEIGHTS]
    return tuple(res)
```

```python
import functools

import jax
import jax.numpy as jnp
from jax import lax
from jax.experimental import pallas as pl
from jax.experimental.pallas import tpu as pltpu

F32 = jnp.float32
BF16 = jnp.bfloat16
EPS = 1e-6
HALO = 32
SUB = 32
K_A = 3
K_B = 31
POOL_WINDOWS = (2, 4, 8, 16)
N_GROUPS = len(POOL_WINDOWS)
N_CHIPS = 4
N_DEV = 8
ADAM_LR = 0.001
ADAM_B1 = 0.9
ADAM_B2 = 0.999
ADAM_EPS = 1e-08
ADAM_WD = 0.01
ADAM_STEP = 10
VMEM_LIMIT = 56 * 1024 * 1024
MESH = pl.DeviceIdType.MESH
ANY = pl.BlockSpec(memory_space=pl.ANY)


def _params(*sem):
    return pltpu.CompilerParams(dimension_semantics=sem, vmem_limit_bytes=VMEM_LIMIT)


def _tile(n, pref, mult=8):
    if n <= pref:
        return n
    t = (pref // mult) * mult
    while t > mult and n % t:
        t -= mult
    assert n % t == 0, (n, pref, mult)
    return t


def _dot(a, b):
    return jnp.dot(a, b, preferred_element_type=F32)


def _sigmoid(x):
    return 1.0 / (1.0 + jnp.exp(-x))


def _f32(ref):
    return ref[...].astype(F32)


def _fold8(v):
    r, d = v.shape
    return v.reshape(r // 8, 8, d).sum(axis=0)


def _rms_bwd(dh, xv, g):
    r = lax.rsqrt(jnp.mean(xv * xv, axis=-1, keepdims=True) + EPS)
    xn = xv * r
    dxn = dh * g
    dx = r * (dxn - xn * jnp.mean(dxn * xn, axis=-1, keepdims=True))
    return dx, _fold8(dh * xn)


class _Rider:
    def __init__(self, args, out_shape, aliases, sems, start, finish, middle=None):
        self.args, self.out_shape, self.aliases, self.sems = list(args), list(out_shape), list(aliases), list(sems)
        self.start, self.finish, self.middle = start, finish, middle


def _hosted_call(body, *, name, grid, in_specs, out_specs, out_shape, scratch_shapes, args, rider=None):
    n_in, n_out, n_scr = len(in_specs), len(out_shape), len(scratch_shapes)
    params = pltpu.CompilerParams(dimension_semantics=("arbitrary",) * len(grid), vmem_limit_bytes=VMEM_LIMIT)
    if rider is None:
        res = pl.pallas_call(body, name=name, grid=grid, in_specs=in_specs, out_specs=out_specs, out_shape=out_shape,
                             scratch_shapes=scratch_shapes, compiler_params=params)(*args)
        return list(res), []
    k_in, k_out = len(rider.args), len(rider.out_shape)

    def hosted(*refs):
        ins, refs = refs[:n_in], refs[n_in:]
        r_in, refs = refs[:k_in], refs[k_in:]
        outs, refs = refs[:n_out], refs[n_out:]
        r_out, refs = refs[:k_out], refs[k_out:]
        scr, r_sems = refs[:n_scr], refs[n_scr:]
        first = functools.reduce(lambda u, v: u & v, [pl.program_id(a) == 0 for a in range(len(grid))])
        last = functools.reduce(lambda u, v: u & v, [pl.program_id(a) == grid[a] - 1 for a in range(len(grid))])

        @pl.when(first)
        def _():
            rider.start(r_in, r_out, r_sems)

        if rider.middle is not None:
            step = functools.reduce(lambda u, a: u * grid[a] + pl.program_id(a), range(len(grid)), 0)

            @pl.when(step == (3 * functools.reduce(lambda u, v: u * v, grid)) // 4)
            def _():
                rider.middle(r_in, r_out, r_sems)

        body(*ins, *outs, *scr)

        @pl.when(last)
        def _():
            rider.finish(r_in, r_out, r_sems)

    res = pl.pallas_call(
        hosted, name=name, grid=grid, in_specs=list(in_specs) + [ANY] * k_in, out_specs=list(out_specs) + [ANY] * k_out,
        out_shape=list(out_shape) + rider.out_shape, scratch_shapes=list(scratch_shapes) + rider.sems,
        input_output_aliases={n_in + i: n_out + o for i, o in rider.aliases}, compiler_params=params,
    )(*args, *rider.args)
    return list(res[:n_out]), list(res[n_out:])


def _comm_call(rider, name):
    def body(*refs):
        k_in, k_out = len(rider.args), len(rider.out_shape)
        r_in, r_out, r_sems = refs[:k_in], refs[k_in:k_in + k_out], refs[k_in + k_out:]
        rider.start(r_in, r_out, r_sems)
        if rider.middle is not None:
            rider.middle(r_in, r_out, r_sems)
        rider.finish(r_in, r_out, r_sems)

    return list(pl.pallas_call(
        body, name=name, in_specs=[ANY] * len(rider.args), out_specs=[ANY] * len(rider.out_shape),
        out_shape=rider.out_shape, scratch_shapes=rider.sems, input_output_aliases=dict(rider.aliases),
    )(*rider.args))


def _rms_fwd(x, g, name, rider=None):
    s, d = x.shape
    tm = _tile(s, 512)

    def body(x_ref, g_ref, h_ref):
        xv = x_ref[...]
        r = lax.rsqrt(jnp.mean(xv * xv, axis=-1, keepdims=True) + EPS)
        h_ref[...] = (xv * r * g_ref[...]).astype(BF16)

    (h,), carried = _hosted_call(
        body, name=name, grid=(s // tm,),
        in_specs=[pl.BlockSpec((tm, d), lambda i: (i, 0)), pl.BlockSpec((1, d), lambda i: (0, 0))],
        out_specs=[pl.BlockSpec((tm, d), lambda i: (i, 0))],
        out_shape=[jax.ShapeDtypeStruct((s, d), BF16)], scratch_shapes=[], args=(x, g.reshape(1, d)), rider=rider)
    return h, carried


def _proj(h, w3, b, name, rider=None):
    s, k = h.shape
    nb, _, nw = w3.shape
    tm = _tile(s, 1024)
    tn = nw
    per = nw // tn

    def body(h_ref, w_ref, b_ref, o_ref):
        o_ref[...] = (_dot(h_ref[...], w_ref[...]) + b_ref[...]).astype(BF16)

    (proj,), carried = _hosted_call(
        body, name=name, grid=(s // tm, nb * per),
        in_specs=[
            pl.BlockSpec((tm, k), lambda i, j: (i, 0)),
            pl.BlockSpec((None, k, tn), lambda i, j: (j // per, 0, j % per)),
            pl.BlockSpec((1, tn), lambda i, j: (0, j)),
        ],
        out_specs=[pl.BlockSpec((tm, tn), lambda i, j: (i, j))],
        out_shape=[jax.ShapeDtypeStruct((s, nb * nw), BF16)],
        scratch_shapes=[], args=(h, w3, b), rider=rider)
    return proj, carried


def _shift_copies(src_e, dst_s):
    n_rows = dst_s.shape[1]
    for b in range(1, 8):
        dst_s[b - 1, :, :] = src_e[pl.ds(b, n_rows), :]


def _window(src_e, dst_s, off, rows):
    b = off % 8
    if b == 0:
        return src_e[pl.ds(off, rows), :]
    return dst_s[b - 1, pl.ds(off - b, rows), :]


def _window_sums(src, levels, n, gc, trailing):
    prev = src
    for j in range(N_GROUPS):
        reach = 2 ** j
        lo = 8 * (j + 1) if trailing else 0
        rows = n - 8 * (j + 1)
        cols = slice(j * gc, N_GROUPS * gc)
        other = lo - reach if trailing else reach
        levels[j][pl.ds(lo, rows), cols] = prev[pl.ds(lo, rows), cols] + prev[pl.ds(other, rows), cols]
        prev = levels[j]


def _prev_halo(tm, d, col):
    r = tm // HALO
    return pl.BlockSpec((HALO, d), lambda i: (jnp.maximum(i * r - 1, 0), col))


def _next_halo(tm, d, col, n_halo_blocks):
    r = tm // HALO
    return pl.BlockSpec((HALO, d), lambda i: (jnp.minimum((i + 1) * r, n_halo_blocks - 1), col))


def _mix_fwd(proj, conv_a, conv_b, cb, lg, lb, name):
    s, d9 = proj.shape
    d = d9 // 9
    gc = d // N_GROUPS
    tm = _tile(s, 256, HALO)

    def tile(col):
        return pl.BlockSpec((tm, d), lambda i: (i, col))

    def vec(rows):
        return pl.BlockSpec((rows, d), lambda i: (0, 0))

    def body(ab, ac, ax, bv, bg, ci, ac_h, ax_h, bv_h, bg_h, ci_h, wa, wb, cb_r, lg_r, lb_r,
             pa_o, s_o, p_o, v_o, ua_e, ub_e, ci_e, ub_s, *levels):
        i = pl.program_id(0)
        keep = (i > 0).astype(F32)
        ua_e[0:HALO, :] = _f32(ac_h) * _f32(ax_h) * keep
        ua_e[HALO:HALO + tm, :] = _f32(ac) * _f32(ax)
        ub_e[0:HALO, :] = _f32(bv_h) * _sigmoid(_f32(bg_h)) * keep
        ub_e[HALO:HALO + tm, :] = _f32(bv) * _sigmoid(_f32(bg))
        ci_e[0:HALO, :] = _f32(ci_h) * keep
        ci_e[HALO:HALO + tm, :] = _f32(ci)
        _window_sums(ci_e, levels, HALO + tm, gc, trailing=True)
        _shift_copies(ub_e, ub_s)
        for r0 in range(0, tm, SUB):
            rows = slice(r0, r0 + SUB)
            z = None
            for k in range(K_A):
                t = wa[k:k + 1, :] * ua_e[pl.ds(HALO - (K_A - 1) + k + r0, SUB), :]
                z = t if z is None else z + t
            pa_o[rows, :] = (ab[rows, :].astype(F32) * z).astype(BF16)
            v = None
            for k in range(K_B):
                t = wb[k:k + 1, :] * _window(ub_e, ub_s, HALO - (K_B - 1) + k + r0, SUB)
                v = t if v is None else v + t
            v = v + cb_r[...]
            v_o[rows, :] = v
            mu = jnp.mean(v, axis=-1, keepdims=True)
            vc = v - mu
            rstd = lax.rsqrt(jnp.mean(vc * vc, axis=-1, keepdims=True) + EPS)
            ln = vc * rstd * lg_r[...] + lb_r[...]
            s_o[rows, :] = (ln * _sigmoid(ln)).astype(BF16)
            t_idx = i * tm + r0 + lax.broadcasted_iota(jnp.int32, (SUB, gc), 0)
            for g, w in enumerate(POOL_WINDOWS):
                cols = slice(g * gc, (g + 1) * gc)
                cur = ci_e[pl.ds(HALO + r0, SUB), cols]
                acc = levels[g][pl.ds(HALO + r0, SUB), cols]
                cnt = jnp.minimum(t_idx + 1, w).astype(F32)
                p_o[rows, cols] = (acc / cnt - cur).astype(BF16)

    return pl.pallas_call(
        body, name=name, grid=(s // tm,),
        in_specs=[tile(0), tile(1), tile(2), tile(3), tile(4), tile(5),
                  _prev_halo(tm, d, 1), _prev_halo(tm, d, 2), _prev_halo(tm, d, 3), _prev_halo(tm, d, 4),
                  _prev_halo(tm, d, 5),
                  vec(K_A), vec(K_B), vec(1), vec(1), vec(1)],
        out_specs=[pl.BlockSpec((tm, d), lambda i: (i, 0))] * 4,
        out_shape=[jax.ShapeDtypeStruct((s, d), BF16)] * 3 + [jax.ShapeDtypeStruct((s, d), F32)],
        scratch_shapes=[pltpu.VMEM((HALO + tm, d), F32)] * 3 + [pltpu.VMEM((7, tm + HALO - 8, d), F32)]
        + [pltpu.VMEM((HALO + tm, d), F32)] * N_GROUPS,
        compiler_params=_params("parallel"),
    )(proj, proj, proj, proj, proj, proj, proj, proj, proj, proj, proj,
      conv_a, conv_b, cb.reshape(1, d), lg.reshape(1, d), lb.reshape(1, d))


def _branch_out(pa, sb, p, proj, x0, woa, wob, wp, wo, bob, ps, gm, name, rider=None):
    s, d = pa.shape
    gc = d // N_GROUPS
    tm = _tile(s, 512)

    def tile(col=0):
        return pl.BlockSpec((tm, d), lambda i: (i, col))

    def const(shape):
        return pl.BlockSpec(shape, lambda i: (0,) * len(shape))

    def body(pa_r, s_r, p_r, g0, g1, g2, x0_r, woa_r, wob_r, wp_r, wo_r, bob_r, ps_r, gm_r,
             ya_o, yb_o, yc_o, mg_o, x1_o, h2_o, yc_s):
        ya = _dot(pa_r[...], woa_r[...])
        yb = _dot(s_r[...], wob_r[...]) + bob_r[...]
        for g in range(N_GROUPS):
            cols = slice(g * gc, (g + 1) * gc)
            yc_s[:, cols] = _dot(p_r[:, cols], wp_r[g])
        yc = yc_s[...]
        ya_o[...] = ya.astype(BF16)
        yb_o[...] = yb.astype(BF16)
        yc_o[...] = yc.astype(BF16)
        m = _sigmoid(_f32(g0)) * ya + _sigmoid(_f32(g1)) * yb + _sigmoid(_f32(g2)) * (yc * ps_r[...])
        mb = m.astype(BF16)
        mg_o[...] = mb
        x1 = x0_r[...] + _dot(mb, wo_r[...])
        x1_o[...] = x1
        r = lax.rsqrt(jnp.mean(x1 * x1, axis=-1, keepdims=True) + EPS)
        h2_o[...] = (x1 * r * gm_r[...]).astype(BF16)

    return _hosted_call(
        body, name=name, grid=(s // tm,),
        in_specs=[tile(), tile(), tile(), tile(6), tile(7), tile(8), tile(),
                  const((d, d)), const((d, d)), const((N_GROUPS, gc, gc)), const((d, d)),
                  const((1, d)), const((1, d)), const((1, d))],
        out_specs=[tile()] * 6,
        out_shape=[jax.ShapeDtypeStruct((s, d), BF16)] * 4 + [jax.ShapeDtypeStruct((s, d), F32),
                                                               jax.ShapeDtypeStruct((s, d), BF16)],
        scratch_shapes=[pltpu.VMEM((tm, d), F32)],
        args=(pa, sb, p, proj, proj, proj, x0, woa, wob, wp, wo, bob.reshape(1, d), ps.reshape(1, d), gm.reshape(1, d)),
        rider=rider)


def _mlp_fwd(h2, x1, w1_3, w2, g_next, name, rider=None):
    s, d = h2.shape
    nf, _, tf = w1_3.shape
    tm = _tile(s, 1024)

    chain = g_next is not None

    def body(h_r, x1_r, w1_r, w2_r, *rest):
        gn_r, f_o, a_o, x2_o, hn_o, acc = rest if chain else (None, *rest[:3], None, rest[3])
        j = pl.program_id(1)

        @pl.when(j == 0)
        def _():
            acc[...] = jnp.zeros_like(acc)

        f = _dot(h_r[...], w1_r[...])
        f_o[...] = f.astype(BF16)
        rl = jnp.maximum(f, 0.0)
        a = (rl * rl).astype(BF16)
        a_o[...] = a
        acc[...] += _dot(a, w2_r[...])

        @pl.when(j == nf - 1)
        def _():
            x2 = x1_r[...] + acc[...]
            x2_o[...] = x2
            if chain:
                r = lax.rsqrt(jnp.mean(x2 * x2, axis=-1, keepdims=True) + EPS)
                hn_o[...] = (x2 * r * gn_r[...]).astype(BF16)

    row = pl.BlockSpec((tm, d), lambda i, j: (i, 0))
    return _hosted_call(
        body, name=name, grid=(s // tm, nf),
        in_specs=[row, row, pl.BlockSpec((None, d, tf), lambda i, j: (j, 0, 0)), pl.BlockSpec((tf, d), lambda i, j: (j, 0))]
        + [pl.BlockSpec((1, d), lambda i, j: (0, 0))] * chain,
        out_specs=[pl.BlockSpec((tm, tf), lambda i, j: (i, j)), pl.BlockSpec((tm, tf), lambda i, j: (i, j)), row] + [row] * chain,
        out_shape=[jax.ShapeDtypeStruct((s, nf * tf), BF16), jax.ShapeDtypeStruct((s, nf * tf), BF16),
                   jax.ShapeDtypeStruct((s, d), F32)] + [jax.ShapeDtypeStruct((s, d), BF16)] * chain,
        scratch_shapes=[pltpu.VMEM((tm, d), F32)],
        args=(h2, x1, w1_3, w2) + ((g_next.reshape(1, d),) if chain else ()), rider=rider)


def _loss_head(x, target, gf, name):
    s, d = x.shape
    tm = _tile(s, 512)
    n = s // tm

    def body(x_r, t_r, g_r, dx_o, loss_o, dg_o, lacc, gacc):
        i = pl.program_id(0)

        @pl.when(i == 0)
        def _():
            lacc[...] = jnp.zeros_like(lacc)
            gacc[...] = jnp.zeros_like(gacc)

        xv = x_r[...]
        r = lax.rsqrt(jnp.mean(xv * xv, axis=-1, keepdims=True) + EPS)
        xn = xv * r
        e = xn * g_r[...] - t_r[...]
        lacc[...] += _fold8(e * e)
        dy = e * (1.0 / d)
        gacc[...] += _fold8(dy * xn)
        dxn = dy * g_r[...]
        dx_o[...] = r * (dxn - xn * jnp.mean(dxn * xn, axis=-1, keepdims=True))

        @pl.when(i == n - 1)
        def _():
            loss_o[...] = jnp.sum(lacc[...]).reshape(1, 1) * (0.5 / d)
            dg_o[...] = jnp.sum(gacc[...], axis=0, keepdims=True)

    return pl.pallas_call(
        body, name=name, grid=(n,),
        in_specs=[pl.BlockSpec((tm, d), lambda i: (i, 0)), pl.BlockSpec((tm, d), lambda i: (i, 0)),
                  pl.BlockSpec((1, d), lambda i: (0, 0))],
        out_specs=[pl.BlockSpec((tm, d), lambda i: (i, 0)), pl.BlockSpec((1, 1), lambda i: (0, 0)),
                   pl.BlockSpec((1, d), lambda i: (0, 0))],
        out_shape=[jax.ShapeDtypeStruct((s, d), F32), jax.ShapeDtypeStruct((1, 1), F32),
                   jax.ShapeDtypeStruct((1, d), F32)],
        scratch_shapes=[pltpu.VMEM((8, d), F32), pltpu.VMEM((8, d), F32)],
        compiler_params=_params("arbitrary"),
    )(x, target, gf.reshape(1, d))


def _tn(a, b, nb, name, rider=None):
    s, m = a.shape
    n = b.shape[1]
    nw = n // nb
    tmm = _tile(m, 1024, 128)
    tn = nw if nw <= 1152 else nw // 2
    per = nw // tn
    ts = _tile(s, 2048)
    ns = s // ts

    def body(a_r, b_r, o_r, acc):
        k = pl.program_id(2)

        @pl.when(k == 0)
        def _():
            acc[...] = jnp.zeros_like(acc)

        acc[...] += lax.dot_general(a_r[...].astype(BF16), b_r[...].astype(BF16), (((0,), (0,)), ((), ())),
                                    preferred_element_type=F32)

        @pl.when(k == ns - 1)
        def _():
            o_r[...] = acc[...]

    (out,), carried = _hosted_call(
        body, name=name, grid=(m // tmm, nb * per, ns),
        in_specs=[pl.BlockSpec((ts, tmm), lambda i, j, k: (k, i)), pl.BlockSpec((ts, tn), lambda i, j, k: (k, j))],
        out_specs=[pl.BlockSpec((None, tmm, tn), lambda i, j, k: (j // per, i, j % per))],
        out_shape=[jax.ShapeDtypeStruct((nb, m, nw), F32)],
        scratch_shapes=[pltpu.VMEM((tmm, tn), F32)], args=(a, b), rider=rider)
    return (out, carried) if rider is not None else out


def _mlp_bwd(dx2, f, x1, w2t, w1t, gm, name, rider=None):
    s, d = dx2.shape
    ff = f.shape[1]
    tf = _tile(ff, 1024, 128)
    nf = ff // tf
    tm = _tile(s, 1024)
    n = s // tm

    def body(dx2_r, f_r, x1_r, w2t_r, w1t_r, gm_r, df_o, dx1_o, dg_o, dxb, acc, gacc):
        i = pl.program_id(0)
        j = pl.program_id(1)

        @pl.when((i == 0) & (j == 0))
        def _():
            gacc[...] = jnp.zeros_like(gacc)

        @pl.when(j == 0)
        def _():
            dxb[...] = dx2_r[...].astype(BF16)
            acc[...] = jnp.zeros_like(acc)

        da = _dot(dxb[...], w2t_r[...])
        df = (da * (2.0 * jnp.maximum(f_r[...].astype(F32), 0.0))).astype(BF16)
        df_o[...] = df
        acc[...] += _dot(df, w1t_r[...])

        @pl.when(j == nf - 1)
        def _():
            dx, dg = _rms_bwd(acc[...], x1_r[...], gm_r[...])
            gacc[...] += dg
            dx1_o[...] = dx2_r[...] + dx

        @pl.when((i == n - 1) & (j == nf - 1))
        def _():
            dg_o[...] = jnp.sum(gacc[...], axis=0, keepdims=True)

    return _hosted_call(
        body, name=name, grid=(n, nf),
        in_specs=[pl.BlockSpec((tm, d), lambda i, j: (i, 0)), pl.BlockSpec((tm, tf), lambda i, j: (i, j)),
                  pl.BlockSpec((tm, d), lambda i, j: (i, 0)), pl.BlockSpec((d, tf), lambda i, j: (0, j)),
                  pl.BlockSpec((tf, d), lambda i, j: (j, 0)), pl.BlockSpec((1, d), lambda i, j: (0, 0))],
        out_specs=[pl.BlockSpec((tm, tf), lambda i, j: (i, j)), pl.BlockSpec((tm, d), lambda i, j: (i, 0)),
                   pl.BlockSpec((1, d), lambda i, j: (0, 0))],
        out_shape=[jax.ShapeDtypeStruct((s, ff), BF16), jax.ShapeDtypeStruct((s, d), F32),
                   jax.ShapeDtypeStruct((1, d), F32)],
        scratch_shapes=[pltpu.VMEM((tm, d), BF16), pltpu.VMEM((tm, d), F32), pltpu.VMEM((8, d), F32)],
        args=(dx2, f, x1, w2t, w1t, gm.reshape(1, d)), rider=rider)


def _branch_out_bwd(dx1, proj, ya, yb, yc, pa, sb, p, wot, woat, wobt, wpt, ps, name):
    s, d = dx1.shape
    gc = d // N_GROUPS
    tm = _tile(s, 512)
    n = s // tm

    def tile(col=0):
        return pl.BlockSpec((tm, d), lambda i: (i, col))

    def const(shape):
        return pl.BlockSpec(shape, lambda i: (0,) * len(shape))

    def once(shape):
        return pl.BlockSpec(shape, lambda i: (0,) * len(shape), pipeline_mode=pl.Buffered(1))

    def tn(a, b):
        return lax.dot_general(a, b, (((0,), (0,)), ((), ())), preferred_element_type=F32)

    def body(dx1_r, g0, g1, g2, ya_r, yb_r, yc_r, pa_r, sb_r, p_r, wot_r, woat_r, wobt_r, wpt_r, ps_r,
             dpa_o, ds_o, dp_o, dpj_o, dbob_o, dps_o, dbg_o, dwoa_o, dwob_o, dwp_o, sacc, gacc, wa_acc, wb_acc, wp_acc, sem):
        i = pl.program_id(0)

        @pl.when(i == 0)
        def _():
            sacc[...] = jnp.zeros_like(sacc)
            gacc[...] = jnp.zeros_like(gacc)
            wa_acc[...] = jnp.zeros_like(wa_acc)
            wb_acc[...] = jnp.zeros_like(wb_acc)
            wp_acc[...] = jnp.zeros_like(wp_acc)

        dm = _dot(dx1_r[...].astype(BF16), wot_r[...])
        ycp = _f32(yc_r)
        ys = (_f32(ya_r), _f32(yb_r), ycp * ps_r[...])
        dys = []
        for b, g_r in enumerate((g0, g1, g2)):
            sg = _sigmoid(_f32(g_r))
            dgt = dm * ys[b] * sg * (1.0 - sg)
            dpj_o[:, b * d:(b + 1) * d] = dgt.astype(BF16)
            gacc[b] += _fold8(dgt)
            dys.append(dm * sg)
        dya, dyb, dyc = dys
        sacc[0] += _fold8(dyb)
        sacc[1] += _fold8(dyc * ycp)
        dyab = dya.astype(BF16)
        dybb = dyb.astype(BF16)
        dycb = (dyc * ps_r[...]).astype(BF16)
        wa_acc[...] += tn(pa_r[...], dyab)
        wb_acc[...] += tn(sb_r[...], dybb)
        dpa_o[...] = _dot(dyab, woat_r[...]).astype(BF16)
        ds_o[...] = _dot(dybb, wobt_r[...]).astype(BF16)
        for g in range(N_GROUPS):
            cols = slice(g * gc, (g + 1) * gc)
            wp_acc[g] += tn(p_r[:, cols], dycb[:, cols])
            dp_o[:, cols] = _dot(dycb[:, cols], wpt_r[g]).astype(BF16)

        @pl.when(i == n - 1)
        def _():
            dbob_o[...] = jnp.sum(sacc[0], axis=0, keepdims=True)
            dps_o[...] = jnp.sum(sacc[1], axis=0, keepdims=True)
            for b in range(3):
                dbg_o[:, b * d:(b + 1) * d] = jnp.sum(gacc[b], axis=0, keepdims=True)
            copies = [pltpu.make_async_copy(src, dst, sem.at[j])
                      for j, (src, dst) in enumerate(((wa_acc, dwoa_o), (wb_acc, dwob_o), (wp_acc, dwp_o)))]
            for cp in copies:
                cp.start()
            for cp in copies:
                cp.wait()

    return pl.pallas_call(
        body, name=name, grid=(n,),
        in_specs=[tile(), tile(6), tile(7), tile(8), tile(), tile(), tile(), tile(), tile(), tile(),
                  once((d, d)), once((d, d)), once((d, d)), once((N_GROUPS, gc, gc)), const((1, d))],
        out_specs=[tile(), tile(), tile(), pl.BlockSpec((tm, 3 * d), lambda i: (i, 2)),
                   const((1, d)), const((1, d)), const((1, 3 * d)), ANY, ANY, ANY],
        out_shape=[jax.ShapeDtypeStruct((s, d), BF16)] * 3 + [jax.ShapeDtypeStruct((s, 9 * d), BF16)]
        + [jax.ShapeDtypeStruct((1, d), F32), jax.ShapeDtypeStruct((1, d), F32), jax.ShapeDtypeStruct((1, 3 * d), F32)]
        + [jax.ShapeDtypeStruct((d, d), F32), jax.ShapeDtypeStruct((d, d), F32), jax.ShapeDtypeStruct((N_GROUPS, gc, gc), F32)],
        scratch_shapes=[pltpu.VMEM((2, 8, d), F32), pltpu.VMEM((3, 8, d), F32), pltpu.VMEM((d, d), F32),
                        pltpu.VMEM((d, d), F32), pltpu.VMEM((N_GROUPS, gc, gc), F32), pltpu.SemaphoreType.DMA((3,))],
        compiler_params=_params("arbitrary"),
    )(dx1, proj, proj, proj, ya, yb, yc, pa, sb, p, wot, woat, wobt, wpt, ps.reshape(1, d))


def _ln_silu_bwd(v, ds, lg, lb, name, rider=None):
    s, d = v.shape
    tm = _tile(s, 256)
    n = s // tm

    def tile():
        return pl.BlockSpec((tm, d), lambda i: (i, 0))

    def vec():
        return pl.BlockSpec((1, d), lambda i: (0, 0))

    def body(v_r, ds_r, lg_r, lb_r, dv_o, dlg_o, dlb_o, dcb_o, acc):
        i = pl.program_id(0)

        @pl.when(i == 0)
        def _():
            acc[...] = jnp.zeros_like(acc)

        for r0 in range(0, tm, SUB):
            rows = slice(r0, r0 + SUB)
            vv = v_r[rows, :]
            mu = jnp.mean(vv, axis=-1, keepdims=True)
            vc = vv - mu
            rstd = lax.rsqrt(jnp.mean(vc * vc, axis=-1, keepdims=True) + EPS)
            nrm = vc * rstd
            ln = nrm * lg_r[...] + lb_r[...]
            sg = _sigmoid(ln)
            dln = ds_r[rows, :].astype(F32) * (sg * (1.0 + ln * (1.0 - sg)))
            acc[0] += _fold8(dln * nrm)
            acc[1] += _fold8(dln)
            dn = dln * lg_r[...]
            dv = rstd * (dn - jnp.mean(dn, axis=-1, keepdims=True)
                         - nrm * jnp.mean(dn * nrm, axis=-1, keepdims=True))
            acc[2] += _fold8(dv)
            dv_o[rows, :] = dv

        @pl.when(i == n - 1)
        def _():
            dlg_o[...] = jnp.sum(acc[0], axis=0, keepdims=True)
            dlb_o[...] = jnp.sum(acc[1], axis=0, keepdims=True)
            dcb_o[...] = jnp.sum(acc[2], axis=0, keepdims=True)

    return _hosted_call(
        body, name=name, grid=(n,),
        in_specs=[tile(), tile(), vec(), vec()],
        out_specs=[tile(), vec(), vec(), vec()],
        out_shape=[jax.ShapeDtypeStruct((s, d), F32)] + [jax.ShapeDtypeStruct((1, d), F32)] * 3,
        scratch_shapes=[pltpu.VMEM((3, 8, d), F32)],
        args=(v, ds, lg.reshape(1, d), lb.reshape(1, d)), rider=rider)


def _mix_bwd(proj, dpa, dv, dp, dproj, conv_a, conv_b, name):
    s, d9 = proj.shape
    d = d9 // 9
    gc = d // N_GROUPS
    tm = _tile(s, 256, HALO)
    n = s // tm
    nh = s // HALO

    def tile(col=0):
        return pl.BlockSpec((tm, d), lambda i: (i, col))

    def vec(rows):
        return pl.BlockSpec((rows, d), lambda i: (0, 0))

    def body(ab, ac, ax, bv, bg, ac_h, ax_h, bv_h, bg_h, dpa_r, dv_r, dp_r, ab_n, dpa_n, dv_n, dp_n, wa, wb, _alias,
             dpj_o, dbin_o, dca_o, dcb_o, ua_e, ub_e, dz_e, dv_e, q_e, bacc, cacc_a, cacc_b, ub_s, dv_s, *levels):
        i = pl.program_id(0)

        @pl.when(i == 0)
        def _():
            bacc[...] = jnp.zeros_like(bacc)
            cacc_a[...] = jnp.zeros_like(cacc_a)
            cacc_b[...] = jnp.zeros_like(cacc_b)

        keep_p = (i > 0).astype(F32)
        keep_n = (i < n - 1).astype(F32)
        ua_e[0:HALO, :] = _f32(ac_h) * _f32(ax_h) * keep_p
        ua_e[HALO:HALO + tm, :] = _f32(ac) * _f32(ax)
        ub_e[0:HALO, :] = _f32(bv_h) * _sigmoid(_f32(bg_h)) * keep_p
        ub_e[HALO:HALO + tm, :] = _f32(bv) * _sigmoid(_f32(bg))
        dz_e[0:tm, :] = _f32(dpa_r) * _f32(ab)
        dz_e[tm:tm + HALO, :] = _f32(dpa_n) * _f32(ab_n) * keep_n
        dv_e[0:tm, :] = dv_r[...]
        dv_e[tm:tm + HALO, :] = dv_n[...] * keep_n
        _shift_copies(ub_e, ub_s)
        _shift_copies(dv_e, dv_s)
        for g, w in enumerate(POOL_WINDOWS):
            cols = slice(g * gc, (g + 1) * gc)
            t_idx = i * tm + lax.broadcasted_iota(jnp.int32, (tm + HALO, gc), 0)
            cnt = jnp.minimum(t_idx + 1, w).astype(F32)
            q_e[0:tm, cols] = dp_r[:, cols].astype(F32) / cnt[0:tm]
            q_e[tm:tm + HALO, cols] = dp_n[:, cols].astype(F32) * keep_n / cnt[tm:tm + HALO]
        _window_sums(q_e, levels, tm + HALO, gc, trailing=False)
        for r0 in range(0, tm, SUB):
            rows = slice(r0, r0 + SUB)
            dz = dz_e[pl.ds(r0, SUB), :]
            z = None
            du = None
            for k in range(K_A):
                u_k = ua_e[pl.ds(HALO - (K_A - 1) + k + r0, SUB), :]
                t = wa[k:k + 1, :] * u_k
                z = t if z is None else z + t
                cacc_a[k] += _fold8(dz * u_k)
                t = wa[k:k + 1, :] * dz_e[pl.ds(r0 + (K_A - 1) - k, SUB), :]
                du = t if du is None else du + t
            outs = [dpa_r[rows, :].astype(F32) * z, du * ax[rows, :].astype(F32), du * ac[rows, :].astype(F32)]
            dvv = dv_e[pl.ds(r0, SUB), :]
            for k in range(K_B):
                cacc_b[k] += _fold8(dvv * _window(ub_e, ub_s, HALO - (K_B - 1) + k + r0, SUB))
            halves = []
            for h0 in range(0, SUB, SUB // 2):
                acc = None
                for k in range(K_B):
                    t = wb[k:k + 1, :] * _window(dv_e, dv_s, r0 + h0 + (K_B - 1) - k, SUB // 2)
                    acc = t if acc is None else acc + t
                halves.append(acc)
            du = jnp.concatenate(halves, axis=0)
            sg = _sigmoid(bg[rows, :].astype(F32))
            outs.append(du * sg)
            outs.append(du * bv[rows, :].astype(F32) * sg * (1.0 - sg))
            for b, o in enumerate(outs):
                dpj_o[rows, b * d:(b + 1) * d] = o.astype(BF16)
                bacc[b] += _fold8(o)
            for g, w in enumerate(POOL_WINDOWS):
                cols = slice(g * gc, (g + 1) * gc)
                o = levels[g][pl.ds(r0, SUB), cols] - dp_r[rows, cols].astype(F32)
                dpj_o[rows, 5 * d + g * gc:5 * d + (g + 1) * gc] = o.astype(BF16)
                bacc[5, :, cols] += _fold8(o)

        @pl.when(i == n - 1)
        def _():
            for b in range(6):
                dbin_o[:, b * d:(b + 1) * d] = jnp.sum(bacc[b], axis=0, keepdims=True)
            dca_o[...] = jnp.sum(cacc_a[...], axis=1)
            dcb_o[...] = jnp.sum(cacc_b[...], axis=1)

    ka8 = 8
    kb8 = 32
    return pl.pallas_call(
        body, name=name, grid=(n,),
        in_specs=[tile(0), tile(1), tile(2), tile(3), tile(4),
                  _prev_halo(tm, d, 1), _prev_halo(tm, d, 2), _prev_halo(tm, d, 3), _prev_halo(tm, d, 4),
                  tile(), tile(), tile(),
                  _next_halo(tm, d, 0, nh), _next_halo(tm, d, 0, nh), _next_halo(tm, d, 0, nh), _next_halo(tm, d, 0, nh),
                  vec(K_A), vec(K_B), ANY],
        out_specs=[pl.BlockSpec((tm, 6 * d), lambda i: (i, 0)), pl.BlockSpec((1, 6 * d), lambda i: (0, 0)),
                   pl.BlockSpec((ka8, d), lambda i: (0, 0)), pl.BlockSpec((kb8, d), lambda i: (0, 0))],
        out_shape=[jax.ShapeDtypeStruct((s, 9 * d), BF16), jax.ShapeDtypeStruct((1, 6 * d), F32),
                   jax.ShapeDtypeStruct((ka8, d), F32), jax.ShapeDtypeStruct((kb8, d), F32)],
        scratch_shapes=[pltpu.VMEM((HALO + tm, d), F32), pltpu.VMEM((HALO + tm, d), F32),
                        pltpu.VMEM((tm + HALO, d), F32), pltpu.VMEM((tm + HALO, d), F32),
                        pltpu.VMEM((tm + HALO, d), F32),
                        pltpu.VMEM((6, 8, d), F32), pltpu.VMEM((ka8, 8, d), F32), pltpu.VMEM((kb8, 8, d), F32),
                        pltpu.VMEM((7, tm + HALO - 8, d), F32), pltpu.VMEM((7, tm + HALO - 8, d), F32)]
        + [pltpu.VMEM((tm + HALO, d), F32)] * N_GROUPS,
        input_output_aliases={18: 0},
        compiler_params=_params("arbitrary"),
    )(proj, proj, proj, proj, proj, proj, proj, proj, proj, dpa, dv, dp, proj, dpa, dv, dp, conv_a, conv_b, dproj)


def _dh_rms_bwd(dproj, win3, x0, dx1, g, name, rider=None):
    s, p = dproj.shape
    d = x0.shape[1]
    tm = _tile(s, 1024)
    nk, _, tk = win3.shape
    n = s // tm

    def body(dp_r, w_r, x_r, dx1_r, g_r, dx_o, dg_o, acc, gacc):
        i = pl.program_id(0)
        k = pl.program_id(1)

        @pl.when((i == 0) & (k == 0))
        def _():
            gacc[...] = jnp.zeros_like(gacc)

        @pl.when(k == 0)
        def _():
            acc[...] = jnp.zeros_like(acc)

        acc[...] += lax.dot_general(dp_r[...], w_r[...], (((1,), (1,)), ((), ())), preferred_element_type=F32)

        @pl.when(k == nk - 1)
        def _():
            dx, dg = _rms_bwd(acc[...], x_r[...], g_r[...])
            gacc[...] += dg
            dx_o[...] = dx1_r[...] + dx

        @pl.when((i == n - 1) & (k == nk - 1))
        def _():
            dg_o[...] = jnp.sum(gacc[...], axis=0, keepdims=True)

    return _hosted_call(
        body, name=name, grid=(n, nk),
        in_specs=[pl.BlockSpec((tm, tk), lambda i, k: (i, k)), pl.BlockSpec((None, d, tk), lambda i, k: (k, 0, 0)),
                  pl.BlockSpec((tm, d), lambda i, k: (i, 0)), pl.BlockSpec((tm, d), lambda i, k: (i, 0)),
                  pl.BlockSpec((1, d), lambda i, k: (0, 0))],
        out_specs=[pl.BlockSpec((tm, d), lambda i, k: (i, 0)), pl.BlockSpec((1, d), lambda i, k: (0, 0))],
        out_shape=[jax.ShapeDtypeStruct((s, d), F32), jax.ShapeDtypeStruct((1, d), F32)],
        scratch_shapes=[pltpu.VMEM((tm, d), F32), pltpu.VMEM((8, d), F32)],
        args=(dproj, win3, x0, dx1, g.reshape(1, d)), rider=rider)


EARLY = ("w_out_a", "w_out_b", "w_pool", "w_o", "w_mlp1", "w_mlp2")


def _layer_fwd(x0, l, g_mix, win3, b_in, weights, proj_rider=None, bo_rider=None, mlp_rider=None, h=None, g_next=None):
    if h is None:
        h, _ = _rms_fwd(x0, g_mix, f"rms_fwd_{l}")
    proj, carried_proj = _proj(h, win3, b_in, f"proj_{l}", proj_rider)
    w = weights(carried_proj) if callable(weights) else weights
    pa, sb, p, v = _mix_fwd(proj, w["conv_a"], w["conv_b"], w["conv_b_bias"], w["ln_b_g"], w["ln_b_b"], f"mix_fwd_{l}")
    (ya, yb, yc, mg, x1, h2), carried_bo = _branch_out(pa, sb, p, proj, x0, w["woa"], w["wob"], w["wp"], w["wo"],
                                                       w["b_out_b"], w["pool_scale"], w["g_mlp"], f"branch_out_{l}", bo_rider)
    rider = mlp_rider(carried_bo) if callable(mlp_rider) else mlp_rider
    (f, a, x2, *h_next), carried_mlp = _mlp_fwd(h2, x1, w["w1_3"], w["w2"], g_next, f"mlp_fwd_{l}", rider)
    saved = dict(x0=x0, h=h, proj=proj, pa=pa, sb=sb, p=p, v=v, ya=ya, yb=yb, yc=yc, mg=mg, x1=x1, h2=h2, f=f, a=a)
    return x2, saved, w, carried_mlp, (h_next[0] if h_next else None)


def _layer_bwd_early(dx2, w, sv, l, mlp_rider=None, dw2_rider=None, dw1_rider=None, swap_early=False):
    d = dx2.shape[1]
    (df, dx1, dgm), carried_mlp = _mlp_bwd(dx2, sv["f"], sv["x1"], w["w2t"], w["w1t"], w["g_mlp"], f"mlp_bwd_{l}",
                                           mlp_rider)
    if dw2_rider is None:
        dw2, carried_dw2 = _tn(sv["a"], dx2, 1, f"dw2_{l}"), []
    else:
        dw2, carried_dw2 = _tn(sv["a"], dx2, 1, f"dw2_{l}", dw2_rider)
    if dw1_rider is None:
        dw1, carried_dw1 = _tn(sv["h2"], df, N_CHIPS, f"dw1_{l}"), []
    else:
        dw1, carried_dw1 = _tn(sv["h2"], df, N_CHIPS, f"dw1_{l}", dw1_rider)
    dpa, ds, dp, dproj, dbob, dps, dbg, dwoa, dwob, dwp = _branch_out_bwd(
        dx1, sv["proj"], sv["ya"], sv["yb"], sv["yc"], sv["pa"], sv["sb"], sv["p"], w["wot"], w["woat"], w["wobt"],
        w["wpt"], w["pool_scale"], f"branch_out_bwd_{l}")
    dwo = _tn(sv["mg"], dx1, 1, f"dwo_{l}")
    ff = dw2.shape[1]
    gc = d // N_GROUPS
    big = dict(
        w_out_a=dwoa.reshape(N_CHIPS, d // N_CHIPS, d),
        w_out_b=dwob.reshape(N_CHIPS, d // N_CHIPS, d),
        w_pool=dwp.reshape(N_GROUPS, N_CHIPS, gc // N_CHIPS, gc).transpose(1, 0, 2, 3).reshape(N_CHIPS, gc, gc),
        w_o=dwo.reshape(N_CHIPS, d // N_CHIPS, d),
        w_mlp1=dw1,
        w_mlp2=dw2.reshape(N_CHIPS, ff // N_CHIPS, d),
    )
    swap = _ride_swap([big[k] for k in EARLY]) if swap_early else None
    (dv, dlg, dlb, dcb), swapped = _ln_silu_bwd(sv["v"], ds, w["ln_b_g"], w["ln_b_b"], f"ln_silu_bwd_{l}", swap)
    dproj, dbm, dca, dcvb = _mix_bwd(sv["proj"], dpa, dv, dp, dproj, w["conv_a"], w["conv_b"], f"mix_bwd_{l}")
    small = dict(
        b_in=jnp.concatenate([dbm, dbg], axis=1).reshape(9, d), conv_a=dca[:K_A], conv_b=dcvb[:K_B],
        conv_b_bias=dcb, ln_b_g=dlg, ln_b_b=dlb, b_out_b=dbob, pool_scale=dps, g_mlp=dgm,
    )
    return dx1, dproj, big, small, carried_mlp + carried_dw2 + carried_dw1, swapped


def _layer_bwd_late(dx1, dproj, w, sv, l, dwin_rider=None, dh_rider=None):
    if dwin_rider is None:
        dwin, carried_dwin = _tn(sv["h"], dproj, N_CHIPS, f"dwin_{l}"), []
    else:
        dwin, carried_dwin = _tn(sv["h"], dproj, N_CHIPS, f"dwin_{l}", dwin_rider)
    rider = dh_rider(dwin) if callable(dh_rider) else dh_rider
    (dx0, dgmix), carried_dh = _dh_rms_bwd(dproj, w["win3"], sv["x0"], dx1, w["g_mix"], f"dh_{l}", rider)
    return dx0, dwin, dgmix, carried_dwin, carried_dh


def _place():
    x, y, c = lax.axis_index("x"), lax.axis_index("y"), lax.axis_index("c")
    chips = [(1 - x, y), (x, 1 - y), (1 - x, 1 - y)]
    return x, y, c, chips


def _place_shard(w, chip_arr, dtype, name):
    rows, cols = w.shape
    tr = _tile(rows, max(16, (1 << 19) // cols), 16)

    def body(k_ref, w_r, o_r):
        o_r[...] = w_r[...].astype(dtype)

    return pl.pallas_call(
        body, name=name,
        grid_spec=pltpu.PrefetchScalarGridSpec(
            num_scalar_prefetch=1, grid=(rows // tr,),
            in_specs=[pl.BlockSpec((tr, cols), lambda i, k_ref: (i, 0))],
            out_specs=pl.BlockSpec((None, tr, cols), lambda i, k_ref: (k_ref[0], i, 0))),
        out_shape=jax.ShapeDtypeStruct((N_CHIPS, rows, cols), dtype),
        compiler_params=_params("parallel"),
    )(chip_arr, w)


def _ride_rows(v):
    def copy(ins, outs, sems, m, src_dev):
        x, y, c = lax.axis_index("x"), lax.axis_index("y"), lax.axis_index("c")
        peer = (x ^ ((m >> 2) & 1), y ^ ((m >> 1) & 1), c ^ (m & 1))
        return pltpu.make_async_remote_copy(
            src_ref=ins[0], dst_ref=outs[0].at[src_dev], send_sem=sems[0].at[m - 1], recv_sem=sems[1].at[m - 1],
            device_id=peer, device_id_type=MESH)

    def me():
        return 4 * lax.axis_index("x") + 2 * lax.axis_index("y") + lax.axis_index("c")

    def start(ins, outs, sems):
        pltpu.make_async_copy(ins[0], outs[0].at[me()], sems[2]).start()
        for m in range(1, N_DEV):
            copy(ins, outs, sems, m, me()).start()

    def finish(ins, outs, sems):
        for m in range(1, N_DEV):
            copy(ins, outs, sems, m, me() ^ m).wait_recv()
        for m in range(1, N_DEV):
            copy(ins, outs, sems, m, me()).wait_send()
        pltpu.make_async_copy(ins[0], outs[0].at[me()], sems[2]).wait()

    return _Rider([v], [jax.ShapeDtypeStruct((N_DEV, *v.shape), v.dtype)], [],
                  [pltpu.SemaphoreType.DMA((N_DEV - 1,)), pltpu.SemaphoreType.DMA((N_DEV - 1,)), pltpu.SemaphoreType.DMA],
                  start, finish)


def _ride_swap(grads):
    n = len(grads)

    def copy(ins, outs, sems, a, k):
        x, y, c, _ = _place()
        h = ins[a].shape[1] // 2
        return pltpu.make_async_remote_copy(
            src_ref=ins[a].at[k, pl.ds(pl.multiple_of((1 - c) * h, 8), h)], dst_ref=outs[a].at[k],
            send_sem=sems[0].at[a * N_CHIPS + k], recv_sem=sems[1].at[a * N_CHIPS + k], device_id=(x, y, 1 - c),
            device_id_type=MESH)

    def start(ins, outs, sems):
        for a in range(n):
            for k in range(N_CHIPS):
                copy(ins, outs, sems, a, k).start()

    def finish(ins, outs, sems):
        for a in range(n):
            for k in range(N_CHIPS):
                copy(ins, outs, sems, a, k).wait()

    return _Rider(grads, [jax.ShapeDtypeStruct((N_CHIPS, g.shape[1] // 2, g.shape[2]), g.dtype) for g in grads], [],
                  [pltpu.SemaphoreType.DMA((N_CHIPS * n,))] * 2, start, finish)


def _add_halves(g, r, c_arr, name):
    nk, rows, cols = g.shape
    h = rows // 2
    tr = _tile(h, max(8, (1 << 19) // cols), 8)
    nblk = h // tr

    def body(c_ref, g_r, r_r, o_r):
        o_r[...] = (g_r[...] + r_r[...]).astype(BF16)

    return pl.pallas_call(
        body, name=name,
        grid_spec=pltpu.PrefetchScalarGridSpec(
            num_scalar_prefetch=1, grid=(nk, nblk),
            in_specs=[pl.BlockSpec((None, tr, cols), lambda k, i, c_ref: (k, c_ref[0] * nblk + i, 0)),
                      pl.BlockSpec((None, tr, cols), lambda k, i, c_ref: (k, i, 0))],
            out_specs=pl.BlockSpec((None, tr, cols), lambda k, i, c_ref: (k, i, 0))),
        out_shape=jax.ShapeDtypeStruct((nk, h, cols), BF16),
        compiler_params=_params("parallel", "parallel"),
    )(c_arr, g, r)


def _ride_exchange(parts):
    n = len(parts)

    def copy(ins, outs, sems, a, j):
        x, y, c, chips = _place()
        px, py = chips[j]
        return pltpu.make_async_remote_copy(
            src_ref=ins[a].at[2 * px + py], dst_ref=outs[a].at[j], send_sem=sems[0].at[a * 3 + j],
            recv_sem=sems[1].at[a * 3 + j], device_id=(px, py, c), device_id_type=MESH)

    def start(ins, outs, sems):
        for a in range(n):
            for j in range(3):
                copy(ins, outs, sems, a, j).start()

    def finish(ins, outs, sems):
        for a in range(n):
            for j in range(3):
                copy(ins, outs, sems, a, j).wait()

    return _Rider(parts, [jax.ShapeDtypeStruct((3, *p.shape[1:]), p.dtype) for p in parts], [],
                  [pltpu.SemaphoreType.DMA((3 * n,))] * 2, start, finish)


def _ride_gather(placed=(), landed=(), forward=True):
    placed, landed = list(placed), list(landed)
    n_ici = len(placed)
    fwd = (list(range(n_ici)) if forward else []) + list(range(n_ici, n_ici + len(landed)))

    def ici(ins, outs, sems, a, j, src_chip):
        x, y, c, chips = _place()
        h = ins[a].shape[1] // 2
        rows = pl.ds(pl.multiple_of(c * h, 8), h)
        return pltpu.make_async_remote_copy(
            src_ref=ins[a].at[2 * x + y, rows], dst_ref=outs[a].at[src_chip, rows], send_sem=sems[0].at[a * 3 + j],
            recv_sem=sems[1].at[a * 3 + j], device_id=(*chips[j], c), device_id_type=MESH)

    def d2d(ins, outs, sems, a, j, which):
        x, y, c, chips = _place()
        px, py = chips[j]
        h = ins[a].shape[1] // 2
        rows = pl.ds(pl.multiple_of(which * h, 8), h)
        return pltpu.make_async_remote_copy(
            src_ref=ins[a].at[2 * px + py, rows], dst_ref=outs[a].at[2 * px + py, rows], send_sem=sems[2].at[a * 3 + j],
            recv_sem=sems[3].at[a * 3 + j], device_id=(x, y, 1 - c), device_id_type=MESH)

    def start(ins, outs, sems):
        x, y, c, _ = _place()
        for a in range(n_ici):
            for j in range(3):
                ici(ins, outs, sems, a, j, 2 * x + y).start()
        for a in range(n_ici, n_ici + len(landed)):
            for j in range(3):
                d2d(ins, outs, sems, a, j, c).start()

    def land(ins, outs, sems, then_forward):
        _, _, c, chips = _place()
        for a in range(n_ici):
            for j, (px, py) in enumerate(chips):
                ici(ins, outs, sems, a, j, 2 * px + py).wait_recv()
                if then_forward:
                    d2d(ins, outs, sems, a, j, c).start()

    def middle(ins, outs, sems):
        land(ins, outs, sems, True)

    def finish(ins, outs, sems):
        x, y, c, _ = _place()
        if not forward:
            land(ins, outs, sems, False)
        for a in fwd:
            for j in range(3):
                d2d(ins, outs, sems, a, j, 1 - c).wait_recv()
        for a in range(n_ici):
            for j in range(3):
                ici(ins, outs, sems, a, j, 2 * x + y).wait_send()
        for a in fwd:
            for j in range(3):
                d2d(ins, outs, sems, a, j, c).wait_send()

    arrays = placed + landed
    n = len(arrays)
    return _Rider(arrays, [jax.ShapeDtypeStruct(p.shape, p.dtype) for p in arrays], [(a, a) for a in range(n)],
                  [pltpu.SemaphoreType.DMA((3 * n,))] * 4, start, finish, middle if (n_ici and forward) else None)


def _sum_chips(r, name):
    nk, rows, cols = r.shape
    tr = _tile(rows, max(8, (1 << 19) // cols), 8)

    def body(r_r, o_r):
        acc = r_r[0].astype(F32)
        for k in range(1, nk):
            acc = acc + r_r[k].astype(F32)
        o_r[...] = acc

    return pl.pallas_call(
        body, name=name, grid=(rows // tr,),
        in_specs=[pl.BlockSpec((nk, tr, cols), lambda i: (0, i, 0))],
        out_specs=pl.BlockSpec((tr, cols), lambda i: (i, 0)),
        out_shape=jax.ShapeDtypeStruct((rows, cols), F32),
        compiler_params=_params("parallel"),
    )(r)


def _sum_chips_into(own, arrived, dst, layer, n_layers, place_arr, name):
    _, h, cols = own.shape
    tr = _tile(h, max(8, (1 << 19) // cols), 8)
    nblk = h // tr

    def body(p_ref, own_r, arr_r, *rest):
        o_r = rest[-1]
        acc = own_r[...].astype(F32)
        for j in range(3):
            acc = acc + arr_r[j].astype(F32)
        o_r[...] = acc

    in_specs = [pl.BlockSpec((None, tr, cols), lambda i, p_ref: (p_ref[0], i, 0)),
                pl.BlockSpec((3, tr, cols), lambda i, p_ref: (0, i, 0))]
    args = [place_arr, own, arrived]
    aliases = {}
    if dst is not None:
        in_specs.append(ANY)
        args.append(dst)
        aliases = {3: 0}
    return pl.pallas_call(
        body, name=name,
        grid_spec=pltpu.PrefetchScalarGridSpec(
            num_scalar_prefetch=1, grid=(nblk,), in_specs=in_specs,
            out_specs=pl.BlockSpec((None, tr, cols), lambda i, p_ref: (layer, p_ref[1] * nblk + i, 0))),
        out_shape=jax.ShapeDtypeStruct((n_layers, 2 * h, cols), F32),
        input_output_aliases=aliases,
        compiler_params=_params("parallel"),
    )(*args)


def _ride_share(shards):
    n = len(shards)
    n_layers = shards[0].shape[0]

    def copy(ins, outs, sems, a, l, which):
        x, y, c, _ = _place()
        h = ins[a].shape[1] // 2
        rows = pl.ds(pl.multiple_of(which * h, 8), h)
        return pltpu.make_async_remote_copy(
            src_ref=ins[a].at[l, rows], dst_ref=outs[a].at[l, rows], send_sem=sems[0].at[a * n_layers + l],
            recv_sem=sems[1].at[a * n_layers + l], device_id=(x, y, 1 - c), device_id_type=MESH)

    def start(ins, outs, sems):
        c = lax.axis_index("c")
        for a in range(n):
            for l in range(n_layers):
                copy(ins, outs, sems, a, l, c).start()

    def finish(ins, outs, sems):
        c = lax.axis_index("c")
        for a in range(n):
            for l in range(n_layers):
                copy(ins, outs, sems, a, l, 1 - c).wait_recv()
        for a in range(n):
            for l in range(n_layers):
                copy(ins, outs, sems, a, l, c).wait_send()

    return _Rider(shards, [jax.ShapeDtypeStruct(a.shape, a.dtype) for a in shards], [(a, a) for a in range(n)],
                  [pltpu.SemaphoreType.DMA((n * n_layers,))] * 2, start, finish)


def _join_riders(first, second):
    k_in, k_out, k_sem = len(first.args), len(first.out_shape), len(first.sems)

    def start(ins, outs, sems):
        first.start(ins[:k_in], outs[:k_out], sems[:k_sem])
        second.start(ins[k_in:], outs[k_out:], sems[k_sem:])

    def finish(ins, outs, sems):
        first.finish(ins[:k_in], outs[:k_out], sems[:k_sem])
        second.finish(ins[k_in:], outs[k_out:], sems[k_sem:])

    return _Rider(first.args + second.args, first.out_shape + second.out_shape,
                  first.aliases + [(i + k_in, o + k_out) for i, o in second.aliases], first.sems + second.sems, start, finish)


def _adamw(w, g, m, v, name):
    rows, cols = w.shape
    tr = _tile(rows, max(8, (1 << 18) // cols), 8)
    c1 = 1.0 - ADAM_B1 ** ADAM_STEP
    c2 = 1.0 - ADAM_B2 ** ADAM_STEP

    def body(w_r, g_r, m_r, v_r, d_o, m_o, v_o):
        gv = g_r[...]
        mn = ADAM_B1 * m_r[...] + (1.0 - ADAM_B1) * gv
        vn = ADAM_B2 * v_r[...] + (1.0 - ADAM_B2) * (gv * gv)
        m_o[...] = mn
        v_o[...] = vn
        d_o[...] = -ADAM_LR * ((mn / c1) / (jnp.sqrt(vn / c2) + ADAM_EPS) + ADAM_WD * w_r[...])

    spec = pl.BlockSpec((tr, cols), lambda i: (i, 0))
    return pl.pallas_call(
        body, name=name, grid=(rows // tr,), in_specs=[spec] * 4, out_specs=[spec] * 3,
        out_shape=[jax.ShapeDtypeStruct((rows, cols), F32)] * 3,
        compiler_params=_params("parallel"),
    )(w, g, m, v)


BIG = ("w_in", "w_out_a", "w_out_b", "w_pool", "w_o", "w_mlp1", "w_mlp2")
SMALL = ("g_mix", "b_in", "conv_b_bias", "ln_b_g", "ln_b_b", "b_out_b", "pool_scale", "g_mlp")
CONVS = ("conv_a", "conv_b")
WEIGHTS = ("g_mix", "w_in", "b_in", "conv_a", "w_out_a", "conv_b", "conv_b_bias", "ln_b_g", "ln_b_b", "w_out_b", "b_out_b",
           "w_pool", "pool_scale", "w_o", "g_mlp", "w_mlp1", "w_mlp2", "g_final")


def _as2d(a):
    return a.reshape(-1, a.shape[-1])


def _pad_rows(a, rows):
    return jnp.pad(a, ((0, rows - a.shape[0]), (0, 0)))


def _full_weights(big_g, conv_g, rep, l, d):
    gc = d // N_GROUPS
    w = {k: v[l] for k, v in rep.items()}
    w["b_in"] = rep["b_in"][l].reshape(1, -1)
    win3 = big_g["w_in"]
    w["win3"] = win3
    for src, dst in (("w_out_a", "woa"), ("w_out_b", "wob"), ("w_o", "wo")):
        full = big_g[src].reshape(d, d)
        w[dst] = full
        w[dst + "t"] = full.T
    wp = big_g["w_pool"].reshape(N_CHIPS, N_GROUPS, gc // N_CHIPS, gc).transpose(1, 0, 2, 3).reshape(N_GROUPS, gc, gc)
    w["wp"] = wp
    w["wpt"] = wp.transpose(0, 2, 1)
    w1_3 = big_g["w_mlp1"]
    w["w1_3"] = w1_3
    w["w1t"] = w1_3.transpose(0, 2, 1).reshape(-1, d)
    w2 = big_g["w_mlp2"].reshape(-1, d)
    w["w2"] = w2
    w["w2t"] = w2.T
    ca, cb = conv_g
    w["conv_a"] = ca
    w["conv_b"] = cb
    return w


def kernel(x, g_mix, w_in, b_in, conv_a, w_out_a, conv_b, conv_b_bias, ln_b_g, ln_b_b, w_out_b, b_out_b, w_pool, pool_scale, w_o, g_mlp, w_mlp1, w_mlp2, g_final, loss_target, m_g_mix, m_w_in, m_b_in, m_conv_a, m_w_out_a, m_conv_b, m_conv_b_bias, m_ln_b_g, m_ln_b_b, m_w_out_b, m_b_out_b, m_w_pool, m_pool_scale, m_w_o, m_g_mlp, m_w_mlp1, m_w_mlp2, m_g_final, v_g_mix, v_w_in, v_b_in, v_conv_a, v_w_out_a, v_conv_b, v_conv_b_bias, v_ln_b_g, v_ln_b_b, v_w_out_b, v_b_out_b, v_w_pool, v_pool_scale, v_w_o, v_g_mlp, v_w_mlp1, v_w_mlp2, v_g_final):
    given = dict(g_mix=g_mix, w_in=w_in, b_in=b_in, conv_a=conv_a, w_out_a=w_out_a, conv_b=conv_b, conv_b_bias=conv_b_bias, ln_b_g=ln_b_g, ln_b_b=ln_b_b, w_out_b=w_out_b, b_out_b=b_out_b, w_pool=w_pool, pool_scale=pool_scale, w_o=w_o, g_mlp=g_mlp, w_mlp1=w_mlp1, w_mlp2=w_mlp2, g_final=g_final)
    mom = dict(g_mix=m_g_mix, w_in=m_w_in, b_in=m_b_in, conv_a=m_conv_a, w_out_a=m_w_out_a, conv_b=m_conv_b, conv_b_bias=m_conv_b_bias, ln_b_g=m_ln_b_g, ln_b_b=m_ln_b_b, w_out_b=m_w_out_b, b_out_b=m_b_out_b, w_pool=m_w_pool, pool_scale=m_pool_scale, w_o=m_w_o, g_mlp=m_g_mlp, w_mlp1=m_w_mlp1, w_mlp2=m_w_mlp2, g_final=m_g_final)
    var = dict(g_mix=v_g_mix, w_in=v_w_in, b_in=v_b_in, conv_a=v_conv_a, w_out_a=v_w_out_a, conv_b=v_conv_b, conv_b_bias=v_conv_b_bias, ln_b_g=v_ln_b_g, ln_b_b=v_ln_b_b, w_out_b=v_w_out_b, b_out_b=v_b_out_b, w_pool=v_w_pool, pool_scale=v_pool_scale, w_o=v_w_o, g_mlp=v_g_mlp, w_mlp1=v_w_mlp1, w_mlp2=v_w_mlp2, g_final=v_g_final)
    n_layers = w_in.shape[0]
    s, d = x.shape[1], x.shape[2]
    dq = d // N_CHIPS
    x_idx, y_idx, c_idx = lax.axis_index("x"), lax.axis_index("y"), lax.axis_index("c")
    chip = 2 * x_idx + y_idx
    c_arr = c_idx.astype(jnp.int32).reshape(1)

    conv_rows = n_layers * (K_A + K_B)
    conv_pad = -(-conv_rows // 16) * 16
    conv_pack = _pad_rows(jnp.concatenate([conv_a[l] for l in range(n_layers)] + [conv_b[l] for l in range(n_layers)], axis=0),
                          conv_pad)
    chip_arr = chip.astype(jnp.int32).reshape(1)
    place_arr = jnp.stack([chip, c_idx]).astype(jnp.int32)

    def placed(k, l):
        return _place_shard(_as2d(given[k][l]), chip_arr, BF16, f"place_{k}_{l}")

    rest = BIG[1:]
    h, (win_g, conv_all) = _rms_fwd(
        x[0], given["g_mix"][0], "rms_fwd_0",
        _ride_gather([placed("w_in", 0), _place_shard(conv_pack, chip_arr, F32, "place_convs")]))
    conv_full = conv_all.transpose(1, 0, 2).reshape(conv_pad, d)
    conv_g = [(conv_full[l * K_A:(l + 1) * K_A], conv_full[n_layers * K_A + l * K_B:n_layers * K_A + (l + 1) * K_B])
              for l in range(n_layers)]
    rep = {k: given[k] for k in SMALL}

    xl = x[0]
    ws, saved = [], []
    rest_g = None
    for l in range(n_layers):
        more = l + 1 < n_layers

        def weights(carried, l=l, win_g=win_g, rest_g=rest_g):
            return _full_weights(dict(zip(rest, carried if rest_g is None else rest_g), w_in=win_g), conv_g[l], rep, l, d)

        xl, sv, w, got, h = _layer_fwd(
            xl, l, given["g_mix"][l], win_g, given["b_in"][l].reshape(1, -1), weights,
            _ride_gather([placed(k, l) for k in rest]) if rest_g is None else None,
            _ride_gather([placed("w_in", l + 1)], forward=False) if more else None,
            (lambda landed, l=l: _ride_gather([placed(k, l + 1) for k in rest], landed)) if more else None,
            h, given["g_mix"][l + 1] if more else None)
        if more:
            rest_g, win_g = got[:len(rest)], got[len(rest)]
        ws.append(w)
        saved.append(sv)
    dx, loss, dgf = _loss_head(xl, loss_target[0], g_final, "loss_head")
    loss = lax.psum(loss[0, 0], ("x", "y", "c"))

    parts, arrived, small = {}, {}, [None] * n_layers
    pending = []
    for l in reversed(range(n_layers)):
        w, sv = ws[l], saved[l]
        hosts = [[q for q in pending if q[1] not in ("w_mlp1", "w_mlp2")],
                 [q for q in pending if q[1] == "w_mlp2"],
                 [q for q in pending if q[1] == "w_mlp1"]]
        dx1, dproj, early, sm, got, swapped = _layer_bwd_early(
            dx, w, sv, l, *[_ride_exchange([parts[q] for q in qs]) if qs else None for qs in hosts], swap_early=(l == 0))
        arrived.update(zip(hosts[0] + hosts[1] + hosts[2], got))
        if l > 0:
            order = ("w_in",) + EARLY
            dx, dwin, dgmix, _, swapped = _layer_bwd_late(
                dx1, dproj, w, sv, l, None, lambda dwin, early=early: _ride_swap([dwin] + [early[k] for k in EARLY]))
            full = dict(early, w_in=dwin)
            for k, r in zip(order, swapped):
                parts[(l, k)] = _add_halves(full[k], r, c_arr, f"grad_add_halves_{k}_{l}")
            pending = [(l, k) for k in order]
        else:
            for k, r in zip(EARLY, swapped):
                parts[(l, k)] = _add_halves(early[k], r, c_arr, f"grad_add_halves_{k}_{l}")

            def w_in_rider(dwin, l=l):
                (r,) = _comm_call(_ride_swap([dwin]), f"grad_swap_w_in_{l}")
                parts[(l, "w_in")] = _add_halves(dwin, r, c_arr, f"grad_add_halves_w_in_{l}")
                return _ride_exchange([parts[(l, "w_in")]])

            dx, dwin, dgmix, got_early, got_w_in = _layer_bwd_late(
                dx1, dproj, w, sv, l, _ride_exchange([parts[(l, k)] for k in EARLY]), w_in_rider)
            arrived.update(zip([(l, k) for k in EARLY], got_early))
            arrived[(l, "w_in")] = got_w_in[0]
            pending = []
        sm["g_mix"] = dgmix
        small[l] = sm
    reduced = []
    for k in BIG:
        dst = None
        for l in range(n_layers):
            dst = _sum_chips_into(parts[(l, k)], arrived[(l, k)], dst, l, n_layers, place_arr, f"grad_sum_chips_{k}_{l}")
        reduced.append(dst)

    small_rows = []
    for k in SMALL + CONVS:
        for l in range(n_layers):
            small_rows.append(small[l][k])
    small_rows.append(dgf)
    pack = jnp.concatenate(small_rows, axis=0)
    n_small = pack.shape[0]
    pack = _pad_rows(pack, -(-n_small // 8) * 8)
    *shared, small_all = _comm_call(_join_riders(_ride_share(reduced), _ride_rows(pack)), "grad_share_and_small")
    big_grad = dict(zip(BIG, shared))
    small_sum = _sum_chips(small_all, "sum_small_grads")

    out = {}
    for k in BIG:
        shape = given[k].shape
        g2 = big_grad[k].reshape(-1, shape[-1])
        dlt, nm, nv = _adamw(given[k].reshape(g2.shape), g2, mom[k].reshape(g2.shape), var[k].reshape(g2.shape), f"adamw_{k}")
        out[k] = tuple(a.reshape(shape) for a in (g2, dlt, nm, nv))

    def rows_of(k):
        return {"b_in": 9, "conv_a": K_A, "conv_b": K_B}.get(k, 1)

    def pack_rep(src):
        rows = [src[k].reshape(n_layers * rows_of(k), d) for k in SMALL] + [src["g_final"].reshape(1, d)]
        a = jnp.concatenate(rows, axis=0)
        return _pad_rows(a, -(-a.shape[0] // 8) * 8)

    def pack_conv(src):
        a = jnp.concatenate([src[k].reshape(n_layers * rows_of(k), dq) for k in CONVS], axis=0)
        return _pad_rows(a, -(-a.shape[0] // 8) * 8)

    offs = {}
    r = 0
    for k in SMALL + CONVS:
        offs[k] = r
        r += n_layers * rows_of(k)
    offs["g_final"] = r
    n_rep = sum(n_layers * rows_of(k) for k in SMALL)
    g_rep = jnp.concatenate([small_sum[:n_rep], small_sum[offs["g_final"]:offs["g_final"] + 1]], axis=0)
    g_rep = _pad_rows(g_rep, -(-g_rep.shape[0] // 8) * 8)
    g_conv_full = small_sum[offs["conv_a"]:offs["g_final"]]
    g_conv = lax.dynamic_slice_in_dim(g_conv_full, chip * dq, dq, axis=1)
    g_conv = _pad_rows(g_conv, -(-g_conv.shape[0] // 8) * 8)
    rep_res = (g_rep,) + tuple(_adamw(pack_rep(given), g_rep, pack_rep(mom), pack_rep(var), "adamw_small"))
    conv_res = (g_conv,) + tuple(_adamw(pack_conv(given), g_conv, pack_conv(mom), pack_conv(var), "adamw_conv"))
    r = 0
    for k in SMALL:
        nr = n_layers * rows_of(k)
        out[k] = tuple(a[r:r + nr].reshape(given[k].shape) for a in rep_res)
        r += nr
    out["g_final"] = tuple(a[r].reshape(given["g_final"].shape) for a in rep_res)
    r = 0
    for k in CONVS:
        nr = n_layers * rows_of(k)
        out[k] = tuple(a[r:r + nr].reshape(given[k].shape) for a in conv_res)
        r += nr

    res = [loss, dx.reshape(x.shape)]
    for i in range(4):
        res += [out[k][i] for k in WEIGHTS]
    return tuple(res)
```

```python
import functools

import jax
import jax.numpy as jnp
from jax import lax
from jax.experimental import pallas as pl
from jax.experimental.pallas import tpu as pltpu

F32 = jnp.float32
BF16 = jnp.bfloat16
EPS = 1e-6
HALO = 32
SUB = 32
K_A = 3
K_B = 31
POOL_WINDOWS = (2, 4, 8, 16)
N_GROUPS = len(POOL_WINDOWS)
N_CHIPS = 4
N_DEV = 8
ADAM_LR = 0.001
ADAM_B1 = 0.9
ADAM_B2 = 0.999
ADAM_EPS = 1e-08
ADAM_WD = 0.01
ADAM_STEP = 10
VMEM_LIMIT = 56 * 1024 * 1024
MESH = pl.DeviceIdType.MESH
ANY = pl.BlockSpec(memory_space=pl.ANY)


def _params(*sem):
    return pltpu.CompilerParams(dimension_semantics=sem, vmem_limit_bytes=VMEM_LIMIT)


def _tile(n, pref, mult=8):
    if n <= pref:
        return n
    t = (pref // mult) * mult
    while t > mult and n % t:
        t -= mult
    assert n % t == 0, (n, pref, mult)
    return t


def _dot(a, b):
    return jnp.dot(a, b, preferred_element_type=F32)


def _sigmoid(x):
    return 1.0 / (1.0 + jnp.exp(-x))


def _f32(ref):
    return ref[...].astype(F32)


def _fold8(v):
    r, d = v.shape
    return v.reshape(r // 8, 8, d).sum(axis=0)


def _rms_bwd(dh, xv, g):
    r = lax.rsqrt(jnp.mean(xv * xv, axis=-1, keepdims=True) + EPS)
    xn = xv * r
    dxn = dh * g
    dx = r * (dxn - xn * jnp.mean(dxn * xn, axis=-1, keepdims=True))
    return dx, _fold8(dh * xn)


class _Rider:
    def __init__(self, args, out_shape, aliases, sems, start, finish, middle=None):
        self.args, self.out_shape, self.aliases, self.sems = list(args), list(out_shape), list(aliases), list(sems)
        self.start, self.finish, self.middle = start, finish, middle


def _hosted_call(body, *, name, grid, in_specs, out_specs, out_shape, scratch_shapes, args, rider=None):
    n_in, n_out, n_scr = len(in_specs), len(out_shape), len(scratch_shapes)
    params = pltpu.CompilerParams(dimension_semantics=("arbitrary",) * len(grid), vmem_limit_bytes=VMEM_LIMIT)
    if rider is None:
        res = pl.pallas_call(body, name=name, grid=grid, in_specs=in_specs, out_specs=out_specs, out_shape=out_shape,
                             scratch_shapes=scratch_shapes, compiler_params=params)(*args)
        return list(res), []
    k_in, k_out = len(rider.args), len(rider.out_shape)

    def hosted(*refs):
        ins, refs = refs[:n_in], refs[n_in:]
        r_in, refs = refs[:k_in], refs[k_in:]
        outs, refs = refs[:n_out], refs[n_out:]
        r_out, refs = refs[:k_out], refs[k_out:]
        scr, r_sems = refs[:n_scr], refs[n_scr:]
        first = functools.reduce(lambda u, v: u & v, [pl.program_id(a) == 0 for a in range(len(grid))])
        last = functools.reduce(lambda u, v: u & v, [pl.program_id(a) == grid[a] - 1 for a in range(len(grid))])

        @pl.when(first)
        def _():
            rider.start(r_in, r_out, r_sems)

        if rider.middle is not None:
            step = functools.reduce(lambda u, a: u * grid[a] + pl.program_id(a), range(len(grid)), 0)

            @pl.when(step == (3 * functools.reduce(lambda u, v: u * v, grid)) // 4)
            def _():
                rider.middle(r_in, r_out, r_sems)

        body(*ins, *outs, *scr)

        @pl.when(last)
        def _():
            rider.finish(r_in, r_out, r_sems)

    res = pl.pallas_call(
        hosted, name=name, grid=grid, in_specs=list(in_specs) + [ANY] * k_in, out_specs=list(out_specs) + [ANY] * k_out,
        out_shape=list(out_shape) + rider.out_shape, scratch_shapes=list(scratch_shapes) + rider.sems,
        input_output_aliases={n_in + i: n_out + o for i, o in rider.aliases}, compiler_params=params,
    )(*args, *rider.args)
    return list(res[:n_out]), list(res[n_out:])


def _comm_call(rider, name):
    def body(*refs):
        k_in, k_out = len(rider.args), len(rider.out_shape)
        r_in, r_out, r_sems = refs[:k_in], refs[k_in:k_in + k_out], refs[k_in + k_out:]
        rider.start(r_in, r_out, r_sems)
        if rider.middle is not None:
            rider.middle(r_in, r_out, r_sems)
        rider.finish(r_in, r_out, r_sems)

    return list(pl.pallas_call(
        body, name=name, in_specs=[ANY] * len(rider.args), out_specs=[ANY] * len(rider.out_shape),
        out_shape=rider.out_shape, scratch_shapes=rider.sems, input_output_aliases=dict(rider.aliases),
    )(*rider.args))


def _rms_fwd(x, g, name, rider=None):
    s, d = x.shape
    tm = _tile(s, 512)

    def body(x_ref, g_ref, h_ref):
        xv = x_ref[...]
        r = lax.rsqrt(jnp.mean(xv * xv, axis=-1, keepdims=True) + EPS)
        h_ref[...] = (xv * r * g_ref[...]).astype(BF16)

    (h,), carried = _hosted_call(
        body, name=name, grid=(s // tm,),
        in_specs=[pl.BlockSpec((tm, d), lambda i: (i, 0)), pl.BlockSpec((1, d), lambda i: (0, 0))],
        out_specs=[pl.BlockSpec((tm, d), lambda i: (i, 0))],
        out_shape=[jax.ShapeDtypeStruct((s, d), BF16)], scratch_shapes=[], args=(x, g.reshape(1, d)), rider=rider)
    return h, carried


def _proj(h, w3, b, name, rider=None):
    s, k = h.shape
    nb, _, nw = w3.shape
    tm = _tile(s, 1024)
    tn = nw
    per = nw // tn

    def body(h_ref, w_ref, b_ref, o_ref):
        o_ref[...] = (_dot(h_ref[...], w_ref[...]) + b_ref[...]).astype(BF16)

    (proj,), carried = _hosted_call(
        body, name=name, grid=(s // tm, nb * per),
        in_specs=[
            pl.BlockSpec((tm, k), lambda i, j: (i, 0)),
            pl.BlockSpec((None, k, tn), lambda i, j: (j // per, 0, j % per)),
            pl.BlockSpec((1, tn), lambda i, j: (0, j)),
        ],
        out_specs=[pl.BlockSpec((tm, tn), lambda i, j: (i, j))],
        out_shape=[jax.ShapeDtypeStruct((s, nb * nw), BF16)],
        scratch_shapes=[], args=(h, w3, b), rider=rider)
    return proj, carried


def _shift_copies(src_e, dst_s):
    n_rows = dst_s.shape[1]
    for b in range(1, 8):
        dst_s[b - 1, :, :] = src_e[pl.ds(b, n_rows), :]


def _window(src_e, dst_s, off, rows):
    b = off % 8
    if b == 0:
        return src_e[pl.ds(off, rows), :]
    return dst_s[b - 1, pl.ds(off - b, rows), :]


def _window_sums(src, levels, n, gc, trailing):
    prev = src
    for j in range(N_GROUPS):
        reach = 2 ** j
        lo = 8 * (j + 1) if trailing else 0
        rows = n - 8 * (j + 1)
        cols = slice(j * gc, N_GROUPS * gc)
        other = lo - reach if trailing else reach
        levels[j][pl.ds(lo, rows), cols] = prev[pl.ds(lo, rows), cols] + prev[pl.ds(other, rows), cols]
        prev = levels[j]


def _prev_halo(tm, d, col):
    r = tm // HALO
    return pl.BlockSpec((HALO, d), lambda i: (jnp.maximum(i * r - 1, 0), col))


def _next_halo(tm, d, col, n_halo_blocks):
    r = tm // HALO
    return pl.BlockSpec((HALO, d), lambda i: (jnp.minimum((i + 1) * r, n_halo_blocks - 1), col))


def _mix_fwd(proj, conv_a, conv_b, cb, lg, lb, name):
    s, d9 = proj.shape
    d = d9 // 9
    gc = d // N_GROUPS
    tm = _tile(s, 256, HALO)

    def tile(col):
        return pl.BlockSpec((tm, d), lambda i: (i, col))

    def vec(rows):
        return pl.BlockSpec((rows, d), lambda i: (0, 0))

    def body(ab, ac, ax, bv, bg, ci, ac_h, ax_h, bv_h, bg_h, ci_h, wa, wb, cb_r, lg_r, lb_r,
             pa_o, s_o, p_o, v_o, ua_e, ub_e, ci_e, ub_s, *levels):
        i = pl.program_id(0)
        keep = (i > 0).astype(F32)
        ua_e[0:HALO, :] = _f32(ac_h) * _f32(ax_h) * keep
        ua_e[HALO:HALO + tm, :] = _f32(ac) * _f32(ax)
        ub_e[0:HALO, :] = _f32(bv_h) * _sigmoid(_f32(bg_h)) * keep
        ub_e[HALO:HALO + tm, :] = _f32(bv) * _sigmoid(_f32(bg))
        ci_e[0:HALO, :] = _f32(ci_h) * keep
        ci_e[HALO:HALO + tm, :] = _f32(ci)
        _window_sums(ci_e, levels, HALO + tm, gc, trailing=True)
        _shift_copies(ub_e, ub_s)
        for r0 in range(0, tm, SUB):
            rows = slice(r0, r0 + SUB)
            z = None
            for k in range(K_A):
                t = wa[k:k + 1, :] * ua_e[pl.ds(HALO - (K_A - 1) + k + r0, SUB), :]
                z = t if z is None else z + t
            pa_o[rows, :] = (ab[rows, :].astype(F32) * z).astype(BF16)
            v = None
            for k in range(K_B):
                t = wb[k:k + 1, :] * _window(ub_e, ub_s, HALO - (K_B - 1) + k + r0, SUB)
                v = t if v is None else v + t
            v = v + cb_r[...]
            v_o[rows, :] = v
            mu = jnp.mean(v, axis=-1, keepdims=True)
            vc = v - mu
            rstd = lax.rsqrt(jnp.mean(vc * vc, axis=-1, keepdims=True) + EPS)
            ln = vc * rstd * lg_r[...] + lb_r[...]
            s_o[rows, :] = (ln * _sigmoid(ln)).astype(BF16)
            t_idx = i * tm + r0 + lax.broadcasted_iota(jnp.int32, (SUB, gc), 0)
            for g, w in enumerate(POOL_WINDOWS):
                cols = slice(g * gc, (g + 1) * gc)
                cur = ci_e[pl.ds(HALO + r0, SUB), cols]
                acc = levels[g][pl.ds(HALO + r0, SUB), cols]
                cnt = jnp.minimum(t_idx + 1, w).astype(F32)
                p_o[rows, cols] = (acc / cnt - cur).astype(BF16)

    return pl.pallas_call(
        body, name=name, grid=(s // tm,),
        in_specs=[tile(0), tile(1), tile(2), tile(3), tile(4), tile(5),
                  _prev_halo(tm, d, 1), _prev_halo(tm, d, 2), _prev_halo(tm, d, 3), _prev_halo(tm, d, 4),
                  _prev_halo(tm, d, 5),
                  vec(K_A), vec(K_B), vec(1), vec(1), vec(1)],
        out_specs=[pl.BlockSpec((tm, d), lambda i: (i, 0))] * 4,
        out_shape=[jax.ShapeDtypeStruct((s, d), BF16)] * 3 + [jax.ShapeDtypeStruct((s, d), F32)],
        scratch_shapes=[pltpu.VMEM((HALO + tm, d), F32)] * 3 + [pltpu.VMEM((7, tm + HALO - 8, d), F32)]
        + [pltpu.VMEM((HALO + tm, d), F32)] * N_GROUPS,
        compiler_params=_params("parallel"),
    )(proj, proj, proj, proj, proj, proj, proj, proj, proj, proj, proj,
      conv_a, conv_b, cb.reshape(1, d), lg.reshape(1, d), lb.reshape(1, d))


def _branch_out(pa, sb, p, proj, x0, woa, wob, wp, wo, bob, ps, gm, name, rider=None):
    s, d = pa.shape
    gc = d // N_GROUPS
    tm = _tile(s, 512)

    def tile(col=0):
        return pl.BlockSpec((tm, d), lambda i: (i, col))

    def const(shape):
        return pl.BlockSpec(shape, lambda i: (0,) * len(shape))

    def body(pa_r, s_r, p_r, g0, g1, g2, x0_r, woa_r, wob_r, wp_r, wo_r, bob_r, ps_r, gm_r,
             ya_o, yb_o, yc_o, mg_o, x1_o, h2_o, yc_s):
        ya = _dot(pa_r[...], woa_r[...])
        yb = _dot(s_r[...], wob_r[...]) + bob_r[...]
        for g in range(N_GROUPS):
            cols = slice(g * gc, (g + 1) * gc)
            yc_s[:, cols] = _dot(p_r[:, cols], wp_r[g])
        yc = yc_s[...]
        ya_o[...] = ya.astype(BF16)
        yb_o[...] = yb.astype(BF16)
        yc_o[...] = yc.astype(BF16)
        m = _sigmoid(_f32(g0)) * ya + _sigmoid(_f32(g1)) * yb + _sigmoid(_f32(g2)) * (yc * ps_r[...])
        mb = m.astype(BF16)
        mg_o[...] = mb
        x1 = x0_r[...] + _dot(mb, wo_r[...])
        x1_o[...] = x1
        r = lax.rsqrt(jnp.mean(x1 * x1, axis=-1, keepdims=True) + EPS)
        h2_o[...] = (x1 * r * gm_r[...]).astype(BF16)

    return _hosted_call(
        body, name=name, grid=(s // tm,),
        in_specs=[tile(), tile(), tile(), tile(6), tile(7), tile(8), tile(),
                  const((d, d)), const((d, d)), const((N_GROUPS, gc, gc)), const((d, d)),
                  const((1, d)), const((1, d)), const((1, d))],
        out_specs=[tile()] * 6,
        out_shape=[jax.ShapeDtypeStruct((s, d), BF16)] * 4 + [jax.ShapeDtypeStruct((s, d), F32),
                                                               jax.ShapeDtypeStruct((s, d), BF16)],
        scratch_shapes=[pltpu.VMEM((tm, d), F32)],
        args=(pa, sb, p, proj, proj, proj, x0, woa, wob, wp, wo, bob.reshape(1, d), ps.reshape(1, d), gm.reshape(1, d)),
        rider=rider)


def _mlp_fwd(h2, x1, w1_3, w2, g_next, name, rider=None):
    s, d = h2.shape
    nf, _, tf = w1_3.shape
    tm = _tile(s, 1024)

    chain = g_next is not None

    def body(h_r, x1_r, w1_r, w2_r, *rest):
        gn_r, f_o, a_o, x2_o, hn_o, acc = rest if chain else (None, *rest[:3], None, rest[3])
        j = pl.program_id(1)

        @pl.when(j == 0)
        def _():
            acc[...] = jnp.zeros_like(acc)

        f = _dot(h_r[...], w1_r[...])
        f_o[...] = f.astype(BF16)
        rl = jnp.maximum(f, 0.0)
        a = (rl * rl).astype(BF16)
        a_o[...] = a
        acc[...] += _dot(a, w2_r[...])

        @pl.when(j == nf - 1)
        def _():
            x2 = x1_r[...] + acc[...]
            x2_o[...] = x2
            if chain:
                r = lax.rsqrt(jnp.mean(x2 * x2, axis=-1, keepdims=True) + EPS)
                hn_o[...] = (x2 * r * gn_r[...]).astype(BF16)

    row = pl.BlockSpec((tm, d), lambda i, j: (i, 0))
    return _hosted_call(
        body, name=name, grid=(s // tm, nf),
        in_specs=[row, row, pl.BlockSpec((None, d, tf), lambda i, j: (j, 0, 0)), pl.BlockSpec((tf, d), lambda i, j: (j, 0))]
        + [pl.BlockSpec((1, d), lambda i, j: (0, 0))] * chain,
        out_specs=[pl.BlockSpec((tm, tf), lambda i, j: (i, j)), pl.BlockSpec((tm, tf), lambda i, j: (i, j)), row] + [row] * chain,
        out_shape=[jax.ShapeDtypeStruct((s, nf * tf), BF16), jax.ShapeDtypeStruct((s, nf * tf), BF16),
                   jax.ShapeDtypeStruct((s, d), F32)] + [jax.ShapeDtypeStruct((s, d), BF16)] * chain,
        scratch_shapes=[pltpu.VMEM((tm, d), F32)],
        args=(h2, x1, w1_3, w2) + ((g_next.reshape(1, d),) if chain else ()), rider=rider)


def _loss_head(x, target, gf, name):
    s, d = x.shape
    tm = _tile(s, 512)
    n = s // tm

    def body(x_r, t_r, g_r, dx_o, loss_o, dg_o, lacc, gacc):
        i = pl.program_id(0)

        @pl.when(i == 0)
        def _():
            lacc[...] = jnp.zeros_like(lacc)
            gacc[...] = jnp.zeros_like(gacc)

        xv = x_r[...]
        r = lax.rsqrt(jnp.mean(xv * xv, axis=-1, keepdims=True) + EPS)
        xn = xv * r
        e = xn * g_r[...] - t_r[...]
        lacc[...] += _fold8(e * e)
        dy = e * (1.0 / d)
        gacc[...] += _fold8(dy * xn)
        dxn = dy * g_r[...]
        dx_o[...] = r * (dxn - xn * jnp.mean(dxn * xn, axis=-1, keepdims=True))

        @pl.when(i == n - 1)
        def _():
            loss_o[...] = jnp.sum(lacc[...]).reshape(1, 1) * (0.5 / d)
            dg_o[...] = jnp.sum(gacc[...], axis=0, keepdims=True)

    return pl.pallas_call(
        body, name=name, grid=(n,),
        in_specs=[pl.BlockSpec((tm, d), lambda i: (i, 0)), pl.BlockSpec((tm, d), lambda i: (i, 0)),
                  pl.BlockSpec((1, d), lambda i: (0, 0))],
        out_specs=[pl.BlockSpec((tm, d), lambda i: (i, 0)), pl.BlockSpec((1, 1), lambda i: (0, 0)),
                   pl.BlockSpec((1, d), lambda i: (0, 0))],
        out_shape=[jax.ShapeDtypeStruct((s, d), F32), jax.ShapeDtypeStruct((1, 1), F32),
                   jax.ShapeDtypeStruct((1, d), F32)],
        scratch_shapes=[pltpu.VMEM((8, d), F32), pltpu.VMEM((8, d), F32)],
        compiler_params=_params("arbitrary"),
    )(x, target, gf.reshape(1, d))


def _tn(a, b, nb, name, rider=None):
    s, m = a.shape
    n = b.shape[1]
    nw = n // nb
    tmm = _tile(m, 1024, 128)
    tn = nw if nw <= 1152 else nw // 2
    per = nw // tn
    ts = _tile(s, 2048)
    ns = s // ts

    def body(a_r, b_r, o_r, acc):
        k = pl.program_id(2)

        @pl.when(k == 0)
        def _():
            acc[...] = jnp.zeros_like(acc)

        acc[...] += lax.dot_general(a_r[...].astype(BF16), b_r[...].astype(BF16), (((0,), (0,)), ((), ())),
                                    preferred_element_type=F32)

        @pl.when(k == ns - 1)
        def _():
            o_r[...] = acc[...]

    (out,), carried = _hosted_call(
        body, name=name, grid=(m // tmm, nb * per, ns),
        in_specs=[pl.BlockSpec((ts, tmm), lambda i, j, k: (k, i)), pl.BlockSpec((ts, tn), lambda i, j, k: (k, j))],
        out_specs=[pl.BlockSpec((None, tmm, tn), lambda i, j, k: (j // per, i, j % per))],
        out_shape=[jax.ShapeDtypeStruct((nb, m, nw), F32)],
        scratch_shapes=[pltpu.VMEM((tmm, tn), F32)], args=(a, b), rider=rider)
    return (out, carried) if rider is not None else out


def _mlp_bwd(dx2, f, x1, w2t, w1t, gm, name, rider=None):
    s, d = dx2.shape
    ff = f.shape[1]
    tf = _tile(ff, 1024, 128)
    nf = ff // tf
    tm = _tile(s, 1024)
    n = s // tm

    def body(dx2_r, f_r, x1_r, w2t_r, w1t_r, gm_r, df_o, dx1_o, dg_o, dxb, acc, gacc):
        i = pl.program_id(0)
        j = pl.program_id(1)

        @pl.when((i == 0) & (j == 0))
        def _():
            gacc[...] = jnp.zeros_like(gacc)

        @pl.when(j == 0)
        def _():
            dxb[...] = dx2_r[...].astype(BF16)
            acc[...] = jnp.zeros_like(acc)

        da = _dot(dxb[...], w2t_r[...])
        df = (da * (2.0 * jnp.maximum(f_r[...].astype(F32), 0.0))).astype(BF16)
        df_o[...] = df
        acc[...] += _dot(df, w1t_r[...])

        @pl.when(j == nf - 1)
        def _():
            dx, dg = _rms_bwd(acc[...], x1_r[...], gm_r[...])
            gacc[...] += dg
            dx1_o[...] = dx2_r[...] + dx

        @pl.when((i == n - 1) & (j == nf - 1))
        def _():
            dg_o[...] = jnp.sum(gacc[...], axis=0, keepdims=True)

    return _hosted_call(
        body, name=name, grid=(n, nf),
        in_specs=[pl.BlockSpec((tm, d), lambda i, j: (i, 0)), pl.BlockSpec((tm, tf), lambda i, j: (i, j)),
                  pl.BlockSpec((tm, d), lambda i, j: (i, 0)), pl.BlockSpec((d, tf), lambda i, j: (0, j)),
                  pl.BlockSpec((tf, d), lambda i, j: (j, 0)), pl.BlockSpec((1, d), lambda i, j: (0, 0))],
        out_specs=[pl.BlockSpec((tm, tf), lambda i, j: (i, j)), pl.BlockSpec((tm, d), lambda i, j: (i, 0)),
                   pl.BlockSpec((1, d), lambda i, j: (0, 0))],
        out_shape=[jax.ShapeDtypeStruct((s, ff), BF16), jax.ShapeDtypeStruct((s, d), F32),
                   jax.ShapeDtypeStruct((1, d), F32)],
        scratch_shapes=[pltpu.VMEM((tm, d), BF16), pltpu.VMEM((tm, d), F32), pltpu.VMEM((8, d), F32)],
        args=(dx2, f, x1, w2t, w1t, gm.reshape(1, d)), rider=rider)


def _branch_out_bwd(dx1, proj, ya, yb, yc, pa, sb, p, wot, woat, wobt, wpt, ps, name):
    s, d = dx1.shape
    gc = d // N_GROUPS
    tm = _tile(s, 512)
    n = s // tm

    def tile(col=0):
        return pl.BlockSpec((tm, d), lambda i: (i, col))

    def const(shape):
        return pl.BlockSpec(shape, lambda i: (0,) * len(shape))

    def once(shape):
        return pl.BlockSpec(shape, lambda i: (0,) * len(shape), pipeline_mode=pl.Buffered(1))

    def tn(a, b):
        return lax.dot_general(a, b, (((0,), (0,)), ((), ())), preferred_element_type=F32)

    def body(dx1_r, g0, g1, g2, ya_r, yb_r, yc_r, pa_r, sb_r, p_r, wot_r, woat_r, wobt_r, wpt_r, ps_r,
             dpa_o, ds_o, dp_o, dpj_o, dbob_o, dps_o, dbg_o, dwoa_o, dwob_o, dwp_o, sacc, gacc, wa_acc, wb_acc, wp_acc, sem):
        i = pl.program_id(0)

        @pl.when(i == 0)
        def _():
            sacc[...] = jnp.zeros_like(sacc)
            gacc[...] = jnp.zeros_like(gacc)
            wa_acc[...] = jnp.zeros_like(wa_acc)
            wb_acc[...] = jnp.zeros_like(wb_acc)
            wp_acc[...] = jnp.zeros_like(wp_acc)

        dm = _dot(dx1_r[...].astype(BF16), wot_r[...])
        ycp = _f32(yc_r)
        ys = (_f32(ya_r), _f32(yb_r), ycp * ps_r[...])
        dys = []
        for b, g_r in enumerate((g0, g1, g2)):
            sg = _sigmoid(_f32(g_r))
            dgt = dm * ys[b] * sg * (1.0 - sg)
            dpj_o[:, b * d:(b + 1) * d] = dgt.astype(BF16)
            gacc[b] += _fold8(dgt)
            dys.append(dm * sg)
        dya, dyb, dyc = dys
        sacc[0] += _fold8(dyb)
        sacc[1] += _fold8(dyc * ycp)
        dyab = dya.astype(BF16)
        dybb = dyb.astype(BF16)
        dycb = (dyc * ps_r[...]).astype(BF16)
        wa_acc[...] += tn(pa_r[...], dyab)
        wb_acc[...] += tn(sb_r[...], dybb)
        dpa_o[...] = _dot(dyab, woat_r[...]).astype(BF16)
        ds_o[...] = _dot(dybb, wobt_r[...]).astype(BF16)
        for g in range(N_GROUPS):
            cols = slice(g * gc, (g + 1) * gc)
            wp_acc[g] += tn(p_r[:, cols], dycb[:, cols])
            dp_o[:, cols] = _dot(dycb[:, cols], wpt_r[g]).astype(BF16)

        @pl.when(i == n - 1)
        def _():
            dbob_o[...] = jnp.sum(sacc[0], axis=0, keepdims=True)
            dps_o[...] = jnp.sum(sacc[1], axis=0, keepdims=True)
            for b in range(3):
                dbg_o[:, b * d:(b + 1) * d] = jnp.sum(gacc[b], axis=0, keepdims=True)
            copies = [pltpu.make_async_copy(src, dst, sem.at[j])
                      for j, (src, dst) in enumerate(((wa_acc, dwoa_o), (wb_acc, dwob_o), (wp_acc, dwp_o)))]
            for cp in copies:
                cp.start()
            for cp in copies:
                cp.wait()

    return pl.pallas_call(
        body, name=name, grid=(n,),
        in_specs=[tile(), tile(6), tile(7), tile(8), tile(), tile(), tile(), tile(), tile(), tile(),
                  once((d, d)), once((d, d)), once((d, d)), once((N_GROUPS, gc, gc)), const((1, d))],
        out_specs=[tile(), tile(), tile(), pl.BlockSpec((tm, 3 * d), lambda i: (i, 2)),
                   const((1, d)), const((1, d)), const((1, 3 * d)), ANY, ANY, ANY],
        out_shape=[jax.ShapeDtypeStruct((s, d), BF16)] * 3 + [jax.ShapeDtypeStruct((s, 9 * d), BF16)]
        + [jax.ShapeDtypeStruct((1, d), F32), jax.ShapeDtypeStruct((1, d), F32), jax.ShapeDtypeStruct((1, 3 * d), F32)]
        + [jax.ShapeDtypeStruct((d, d), F32), jax.ShapeDtypeStruct((d, d), F32), jax.ShapeDtypeStruct((N_GROUPS, gc, gc), F32)],
        scratch_shapes=[pltpu.VMEM((2, 8, d), F32), pltpu.VMEM((3, 8, d), F32), pltpu.VMEM((d, d), F32),
                        pltpu.VMEM((d, d), F32), pltpu.VMEM((N_GROUPS, gc, gc), F32), pltpu.SemaphoreType.DMA((3,))],
        compiler_params=_params("arbitrary"),
    )(dx1, proj, proj, proj, ya, yb, yc, pa, sb, p, wot, woat, wobt, wpt, ps.reshape(1, d))


def _ln_silu_bwd(v, ds, lg, lb, name, rider=None):
    s, d = v.shape
    tm = _tile(s, 256)
    n = s // tm

    def tile():
        return pl.BlockSpec((tm, d), lambda i: (i, 0))

    def vec():
        return pl.BlockSpec((1, d), lambda i: (0, 0))

    def body(v_r, ds_r, lg_r, lb_r, dv_o, dlg_o, dlb_o, dcb_o, acc):
        i = pl.program_id(0)

        @pl.when(i == 0)
        def _():
            acc[...] = jnp.zeros_like(acc)

        for r0 in range(0, tm, SUB):
            rows = slice(r0, r0 + SUB)
            vv = v_r[rows, :]
            mu = jnp.mean(vv, axis=-1, keepdims=True)
            vc = vv - mu
            rstd = lax.rsqrt(jnp.mean(vc * vc, axis=-1, keepdims=True) + EPS)
            nrm = vc * rstd
            ln = nrm * lg_r[...] + lb_r[...]
            sg = _sigmoid(ln)
            dln = ds_r[rows, :].astype(F32) * (sg * (1.0 + ln * (1.0 - sg)))
            acc[0] += _fold8(dln * nrm)
            acc[1] += _fold8(dln)
            dn = dln * lg_r[...]
            dv = rstd * (dn - jnp.mean(dn, axis=-1, keepdims=True)
                         - nrm * jnp.mean(dn * nrm, axis=-1, keepdims=True))
            acc[2] += _fold8(dv)
            dv_o[rows, :] = dv

        @pl.when(i == n - 1)
        def _():
            dlg_o[...] = jnp.sum(acc[0], axis=0, keepdims=True)
            dlb_o[...] = jnp.sum(acc[1], axis=0, keepdims=True)
            dcb_o[...] = jnp.sum(acc[2], axis=0, keepdims=True)

    return _hosted_call(
        body, name=name, grid=(n,),
        in_specs=[tile(), tile(), vec(), vec()],
        out_specs=[tile(), vec(), vec(), vec()],
        out_shape=[jax.ShapeDtypeStruct((s, d), F32)] + [jax.ShapeDtypeStruct((1, d), F32)] * 3,
        scratch_shapes=[pltpu.VMEM((3, 8, d), F32)],
        args=(v, ds, lg.reshape(1, d), lb.reshape(1, d)), rider=rider)


def _mix_bwd(proj, dpa, dv, dp, dproj, conv_a, conv_b, name):
    s, d9 = proj.shape
    d = d9 // 9
    gc = d // N_GROUPS
    tm = _tile(s, 256, HALO)
    n = s // tm
    nh = s // HALO

    def tile(col=0):
        return pl.BlockSpec((tm, d), lambda i: (i, col))

    def vec(rows):
        return pl.BlockSpec((rows, d), lambda i: (0, 0))

    def body(ab, ac, ax, bv, bg, ac_h, ax_h, bv_h, bg_h, dpa_r, dv_r, dp_r, ab_n, dpa_n, dv_n, dp_n, wa, wb, _alias,
             dpj_o, dbin_o, dca_o, dcb_o, ua_e, ub_e, dz_e, dv_e, q_e, bacc, cacc_a, cacc_b, ub_s, dv_s, *levels):
        i = pl.program_id(0)

        @pl.when(i == 0)
        def _():
            bacc[...] = jnp.zeros_like(bacc)
            cacc_a[...] = jnp.zeros_like(cacc_a)
            cacc_b[...] = jnp.zeros_like(cacc_b)

        keep_p = (i > 0).astype(F32)
        keep_n = (i < n - 1).astype(F32)
        ua_e[0:HALO, :] = _f32(ac_h) * _f32(ax_h) * keep_p
        ua_e[HALO:HALO + tm, :] = _f32(ac) * _f32(ax)
        ub_e[0:HALO, :] = _f32(bv_h) * _sigmoid(_f32(bg_h)) * keep_p
        ub_e[HALO:HALO + tm, :] = _f32(bv) * _sigmoid(_f32(bg))
        dz_e[0:tm, :] = _f32(dpa_r) * _f32(ab)
        dz_e[tm:tm + HALO, :] = _f32(dpa_n) * _f32(ab_n) * keep_n
        dv_e[0:tm, :] = dv_r[...]
        dv_e[tm:tm + HALO, :] = dv_n[...] * keep_n
        _shift_copies(ub_e, ub_s)
        _shift_copies(dv_e, dv_s)
        for g, w in enumerate(POOL_WINDOWS):
            cols = slice(g * gc, (g + 1) * gc)
            t_idx = i * tm + lax.broadcasted_iota(jnp.int32, (tm + HALO, gc), 0)
            cnt = jnp.minimum(t_idx + 1, w).astype(F32)
            q_e[0:tm, cols] = dp_r[:, cols].astype(F32) / cnt[0:tm]
            q_e[tm:tm + HALO, cols] = dp_n[:, cols].astype(F32) * keep_n / cnt[tm:tm + HALO]
        _window_sums(q_e, levels, tm + HALO, gc, trailing=False)
        for r0 in range(0, tm, SUB):
            rows = slice(r0, r0 + SUB)
            dz = dz_e[pl.ds(r0, SUB), :]
            z = None
            du = None
            for k in range(K_A):
                u_k = ua_e[pl.ds(HALO - (K_A - 1) + k + r0, SUB), :]
                t = wa[k:k + 1, :] * u_k
                z = t if z is None else z + t
                cacc_a[k] += _fold8(dz * u_k)
                t = wa[k:k + 1, :] * dz_e[pl.ds(r0 + (K_A - 1) - k, SUB), :]
                du = t if du is None else du + t
            outs = [dpa_r[rows, :].astype(F32) * z, du * ax[rows, :].astype(F32), du * ac[rows, :].astype(F32)]
            dvv = dv_e[pl.ds(r0, SUB), :]
            for k in range(K_B):
                cacc_b[k] += _fold8(dvv * _window(ub_e, ub_s, HALO - (K_B - 1) + k + r0, SUB))
            halves = []
            for h0 in range(0, SUB, SUB // 2):
                acc = None
                for k in range(K_B):
                    t = wb[k:k + 1, :] * _window(dv_e, dv_s, r0 + h0 + (K_B - 1) - k, SUB // 2)
                    acc = t if acc is None else acc + t
                halves.append(acc)
            du = jnp.concatenate(halves, axis=0)
            sg = _sigmoid(bg[rows, :].astype(F32))
            outs.append(du * sg)
            outs.append(du * bv[rows, :].astype(F32) * sg * (1.0 - sg))
            for b, o in enumerate(outs):
                dpj_o[rows, b * d:(b + 1) * d] = o.astype(BF16)
                bacc[b] += _fold8(o)
            for g, w in enumerate(POOL_WINDOWS):
                cols = slice(g * gc, (g + 1) * gc)
                o = levels[g][pl.ds(r0, SUB), cols] - dp_r[rows, cols].astype(F32)
                dpj_o[rows, 5 * d + g * gc:5 * d + (g + 1) * gc] = o.astype(BF16)
                bacc[5, :, cols] += _fold8(o)

        @pl.when(i == n - 1)
        def _():
            for b in range(6):
                dbin_o[:, b * d:(b + 1) * d] = jnp.sum(bacc[b], axis=0, keepdims=True)
            dca_o[...] = jnp.sum(cacc_a[...], axis=1)
            dcb_o[...] = jnp.sum(cacc_b[...], axis=1)

    ka8 = 8
    kb8 = 32
    return pl.pallas_call(
        body, name=name, grid=(n,),
        in_specs=[tile(0), tile(1), tile(2), tile(3), tile(4),
                  _prev_halo(tm, d, 1), _prev_halo(tm, d, 2), _prev_halo(tm, d, 3), _prev_halo(tm, d, 4),
                  tile(), tile(), tile(),
                  _next_halo(tm, d, 0, nh), _next_halo(tm, d, 0, nh), _next_halo(tm, d, 0, nh), _next_halo(tm, d, 0, nh),
                  vec(K_A), vec(K_B), ANY],
        out_specs=[pl.BlockSpec((tm, 6 * d), lambda i: (i, 0)), pl.BlockSpec((1, 6 * d), lambda i: (0, 0)),
                   pl.BlockSpec((ka8, d), lambda i: (0, 0)), pl.BlockSpec((kb8, d), lambda i: (0, 0))],
        out_shape=[jax.ShapeDtypeStruct((s, 9 * d), BF16), jax.ShapeDtypeStruct((1, 6 * d), F32),
                   jax.ShapeDtypeStruct((ka8, d), F32), jax.ShapeDtypeStruct((kb8, d), F32)],
        scratch_shapes=[pltpu.VMEM((HALO + tm, d), F32), pltpu.VMEM((HALO + tm, d), F32),
                        pltpu.VMEM((tm + HALO, d), F32), pltpu.VMEM((tm + HALO, d), F32),
                        pltpu.VMEM((tm + HALO, d), F32),
                        pltpu.VMEM((6, 8, d), F32), pltpu.VMEM((ka8, 8, d), F32), pltpu.VMEM((kb8, 8, d), F32),
                        pltpu.VMEM((7, tm + HALO - 8, d), F32), pltpu.VMEM((7, tm + HALO - 8, d), F32)]
        + [pltpu.VMEM((tm + HALO, d), F32)] * N_GROUPS,
        input_output_aliases={18: 0},
        compiler_params=_params("arbitrary"),
    )(proj, proj, proj, proj, proj, proj, proj, proj, proj, dpa, dv, dp, proj, dpa, dv, dp, conv_a, conv_b, dproj)


def _dh_rms_bwd(dproj, win3, x0, dx1, g, name, rider=None):
    s, p = dproj.shape
    d = x0.shape[1]
    tm = _tile(s, 1024)
    nk, _, tk = win3.shape
    n = s // tm

    def body(dp_r, w_r, x_r, dx1_r, g_r, dx_o, dg_o, acc, gacc):
        i = pl.program_id(0)
        k = pl.program_id(1)

        @pl.when((i == 0) & (k == 0))
        def _():
            gacc[...] = jnp.zeros_like(gacc)

        @pl.when(k == 0)
        def _():
            acc[...] = jnp.zeros_like(acc)

        acc[...] += lax.dot_general(dp_r[...], w_r[...], (((1,), (1,)), ((), ())), preferred_element_type=F32)

        @pl.when(k == nk - 1)
        def _():
            dx, dg = _rms_bwd(acc[...], x_r[...], g_r[...])
            gacc[...] += dg
            dx_o[...] = dx1_r[...] + dx

        @pl.when((i == n - 1) & (k == nk - 1))
        def _():
            dg_o[...] = jnp.sum(gacc[...], axis=0, keepdims=True)

    return _hosted_call(
        body, name=name, grid=(n, nk),
        in_specs=[pl.BlockSpec((tm, tk), lambda i, k: (i, k)), pl.BlockSpec((None, d, tk), lambda i, k: (k, 0, 0)),
                  pl.BlockSpec((tm, d), lambda i, k: (i, 0)), pl.BlockSpec((tm, d), lambda i, k: (i, 0)),
                  pl.BlockSpec((1, d), lambda i, k: (0, 0))],
        out_specs=[pl.BlockSpec((tm, d), lambda i, k: (i, 0)), pl.BlockSpec((1, d), lambda i, k: (0, 0))],
        out_shape=[jax.ShapeDtypeStruct((s, d), F32), jax.ShapeDtypeStruct((1, d), F32)],
        scratch_shapes=[pltpu.VMEM((tm, d), F32), pltpu.VMEM((8, d), F32)],
        args=(dproj, win3, x0, dx1, g.reshape(1, d)), rider=rider)


EARLY = ("w_out_a", "w_out_b", "w_pool", "w_o", "w_mlp1", "w_mlp2")


def _layer_fwd(x0, l, g_mix, win3, b_in, weights, proj_rider=None, bo_rider=None, mlp_rider=None, h=None, g_next=None):
    if h is None:
        h, _ = _rms_fwd(x0, g_mix, f"rms_fwd_{l}")
    proj, carried_proj = _proj(h, win3, b_in, f"proj_{l}", proj_rider)
    w = weights(carried_proj) if callable(weights) else weights
    pa, sb, p, v = _mix_fwd(proj, w["conv_a"], w["conv_b"], w["conv_b_bias"], w["ln_b_g"], w["ln_b_b"], f"mix_fwd_{l}")
    (ya, yb, yc, mg, x1, h2), carried_bo = _branch_out(pa, sb, p, proj, x0, w["woa"], w["wob"], w["wp"], w["wo"],
                                                       w["b_out_b"], w["pool_scale"], w["g_mlp"], f"branch_out_{l}", bo_rider)
    rider = mlp_rider(carried_bo) if callable(mlp_rider) else mlp_rider
    (f, a, x2, *h_next), carried_mlp = _mlp_fwd(h2, x1, w["w1_3"], w["w2"], g_next, f"mlp_fwd_{l}", rider)
    saved = dict(x0=x0, h=h, proj=proj, pa=pa, sb=sb, p=p, v=v, ya=ya, yb=yb, yc=yc, mg=mg, x1=x1, h2=h2, f=f, a=a)
    return x2, saved, w, carried_mlp, (h_next[0] if h_next else None)


def _layer_bwd_early(dx2, w, sv, l, mlp_rider=None, dw2_rider=None, dw1_rider=None, swap_early=False):
    d = dx2.shape[1]
    (df, dx1, dgm), carried_mlp = _mlp_bwd(dx2, sv["f"], sv["x1"], w["w2t"], w["w1t"], w["g_mlp"], f"mlp_bwd_{l}",
                                           mlp_rider)
    if dw2_rider is None:
        dw2, carried_dw2 = _tn(sv["a"], dx2, 1, f"dw2_{l}"), []
    else:
        dw2, carried_dw2 = _tn(sv["a"], dx2, 1, f"dw2_{l}", dw2_rider)
    if dw1_rider is None:
        dw1, carried_dw1 = _tn(sv["h2"], df, N_CHIPS, f"dw1_{l}"), []
    else:
        dw1, carried_dw1 = _tn(sv["h2"], df, N_CHIPS, f"dw1_{l}", dw1_rider)
    dpa, ds, dp, dproj, dbob, dps, dbg, dwoa, dwob, dwp = _branch_out_bwd(
        dx1, sv["proj"], sv["ya"], sv["yb"], sv["yc"], sv["pa"], sv["sb"], sv["p"], w["wot"], w["woat"], w["wobt"],
        w["wpt"], w["pool_scale"], f"branch_out_bwd_{l}")
    dwo = _tn(sv["mg"], dx1, 1, f"dwo_{l}")
    ff = dw2.shape[1]
    gc = d // N_GROUPS
    big = dict(
        w_out_a=dwoa.reshape(N_CHIPS, d // N_CHIPS, d),
        w_out_b=dwob.reshape(N_CHIPS, d // N_CHIPS, d),
        w_pool=dwp.reshape(N_GROUPS, N_CHIPS, gc // N_CHIPS, gc).transpose(1, 0, 2, 3).reshape(N_CHIPS, gc, gc),
        w_o=dwo.reshape(N_CHIPS, d // N_CHIPS, d),
        w_mlp1=dw1,
        w_mlp2=dw2.reshape(N_CHIPS, ff // N_CHIPS, d),
    )
    swap = _ride_swap([big[k] for k in EARLY]) if swap_early else None
    (dv, dlg, dlb, dcb), swapped = _ln_silu_bwd(sv["v"], ds, w["ln_b_g"], w["ln_b_b"], f"ln_silu_bwd_{l}", swap)
    dproj, dbm, dca, dcvb = _mix_bwd(sv["proj"], dpa, dv, dp, dproj, w["conv_a"], w["conv_b"], f"mix_bwd_{l}")
    small = dict(
        b_in=jnp.concatenate([dbm, dbg], axis=1).reshape(9, d), conv_a=dca[:K_A], conv_b=dcvb[:K_B],
        conv_b_bias=dcb, ln_b_g=dlg, ln_b_b=dlb, b_out_b=dbob, pool_scale=dps, g_mlp=dgm,
    )
    return dx1, dproj, big, small, carried_mlp + carried_dw2 + carried_dw1, swapped


def _layer_bwd_late(dx1, dproj, w, sv, l, dwin_rider=None, dh_rider=None):
    if dwin_rider is None:
        dwin, carried_dwin = _tn(sv["h"], dproj, N_CHIPS, f"dwin_{l}"), []
    else:
        dwin, carried_dwin = _tn(sv["h"], dproj, N_CHIPS, f"dwin_{l}", dwin_rider)
    rider = dh_rider(dwin) if callable(dh_rider) else dh_rider
    (dx0, dgmix), carried_dh = _dh_rms_bwd(dproj, w["win3"], sv["x0"], dx1, w["g_mix"], f"dh_{l}", rider)
    return dx0, dwin, dgmix, carried_dwin, carried_dh


def _place():
    x, y, c = lax.axis_index("x"), lax.axis_index("y"), lax.axis_index("c")
    chips = [(1 - x, y), (x, 1 - y), (1 - x, 1 - y)]
    return x, y, c, chips


def _place_shard(w, chip_arr, dtype, name):
    rows, cols = w.shape
    tr = _tile(rows, max(16, (1 << 19) // cols), 16)

    def body(k_ref, w_r, o_r):
        o_r[...] = w_r[...].astype(dtype)

    return pl.pallas_call(
        body, name=name,
        grid_spec=pltpu.PrefetchScalarGridSpec(
            num_scalar_prefetch=1, grid=(rows // tr,),
            in_specs=[pl.BlockSpec((tr, cols), lambda i, k_ref: (i, 0))],
            out_specs=pl.BlockSpec((None, tr, cols), lambda i, k_ref: (k_ref[0], i, 0))),
        out_shape=jax.ShapeDtypeStruct((N_CHIPS, rows, cols), dtype),
        compiler_params=_params("parallel"),
    )(chip_arr, w)


def _ride_rows(v):
    def copy(ins, outs, sems, m, src_dev):
        x, y, c = lax.axis_index("x"), lax.axis_index("y"), lax.axis_index("c")
        peer = (x ^ ((m >> 2) & 1), y ^ ((m >> 1) & 1), c ^ (m & 1))
        return pltpu.make_async_remote_copy(
            src_ref=ins[0], dst_ref=outs[0].at[src_dev], send_sem=sems[0].at[m - 1], recv_sem=sems[1].at[m - 1],
            device_id=peer, device_id_type=MESH)

    def me():
        return 4 * lax.axis_index("x") + 2 * lax.axis_index("y") + lax.axis_index("c")

    def start(ins, outs, sems):
        pltpu.make_async_copy(ins[0], outs[0].at[me()], sems[2]).start()
        for m in range(1, N_DEV):
            copy(ins, outs, sems, m, me()).start()

    def finish(ins, outs, sems):
        for m in range(1, N_DEV):
            copy(ins, outs, sems, m, me() ^ m).wait_recv()
        for m in range(1, N_DEV):
            copy(ins, outs, sems, m, me()).wait_send()
        pltpu.make_async_copy(ins[0], outs[0].at[me()], sems[2]).wait()

    return _Rider([v], [jax.ShapeDtypeStruct((N_DEV, *v.shape), v.dtype)], [],
                  [pltpu.SemaphoreType.DMA((N_DEV - 1,)), pltpu.SemaphoreType.DMA((N_DEV - 1,)), pltpu.SemaphoreType.DMA],
                  start, finish)


def _ride_swap(grads):
    n = len(grads)

    def copy(ins, outs, sems, a, k):
        x, y, c, _ = _place()
        h = ins[a].shape[1] // 2
        return pltpu.make_async_remote_copy(
            src_ref=ins[a].at[k, pl.ds(pl.multiple_of((1 - c) * h, 8), h)], dst_ref=outs[a].at[k],
            send_sem=sems[0].at[a * N_CHIPS + k], recv_sem=sems[1].at[a * N_CHIPS + k], device_id=(x, y, 1 - c),
            device_id_type=MESH)

    def start(ins, outs, sems):
        for a in range(n):
            for k in range(N_CHIPS):
                copy(ins, outs, sems, a, k).start()

    def finish(ins, outs, sems):
        for a in range(n):
            for k in range(N_CHIPS):
                copy(ins, outs, sems, a, k).wait()

    return _Rider(grads, [jax.ShapeDtypeStruct((N_CHIPS, g.shape[1] // 2, g.shape[2]), g.dtype) for g in grads], [],
                  [pltpu.SemaphoreType.DMA((N_CHIPS * n,))] * 2, start, finish)


def _add_halves(g, r, c_arr, name):
    nk, rows, cols = g.shape
    h = rows // 2
    tr = _tile(h, max(8, (1 << 19) // cols), 8)
    nblk = h // tr

    def body(c_ref, g_r, r_r, o_r):
        o_r[...] = (g_r[...] + r_r[...]).astype(BF16)

    return pl.pallas_call(
        body, name=name,
        grid_spec=pltpu.PrefetchScalarGridSpec(
            num_scalar_prefetch=1, grid=(nk, nblk),
            in_specs=[pl.BlockSpec((None, tr, cols), lambda k, i, c_ref: (k, c_ref[0] * nblk + i, 0)),
                      pl.BlockSpec((None, tr, cols), lambda k, i, c_ref: (k, i, 0))],
            out_specs=pl.BlockSpec((None, tr, cols), lambda k, i, c_ref: (k, i, 0))),
        out_shape=jax.ShapeDtypeStruct((nk, h, cols), BF16),
        compiler_params=_params("parallel", "parallel"),
    )(c_arr, g, r)


def _ride_exchange(parts):
    n = len(parts)

    def copy(ins, outs, sems, a, j):
        x, y, c, chips = _place()
        px, py = chips[j]
        return pltpu.make_async_remote_copy(
            src_ref=ins[a].at[2 * px + py], dst_ref=outs[a].at[j], send_sem=sems[0].at[a * 3 + j],
            recv_sem=sems[1].at[a * 3 + j], device_id=(px, py, c), device_id_type=MESH)

    def start(ins, outs, sems):
        for a in range(n):
            for j in range(3):
                copy(ins, outs, sems, a, j).start()

    def finish(ins, outs, sems):
        for a in range(n):
            for j in range(3):
                copy(ins, outs, sems, a, j).wait()

    return _Rider(parts, [jax.ShapeDtypeStruct((3, *p.shape[1:]), p.dtype) for p in parts], [],
                  [pltpu.SemaphoreType.DMA((3 * n,))] * 2, start, finish)


def _ride_gather(placed=(), landed=(), forward=True):
    placed, landed = list(placed), list(landed)
    n_ici = len(placed)
    fwd = (list(range(n_ici)) if forward else []) + list(range(n_ici, n_ici + len(landed)))

    def ici(ins, outs, sems, a, j, src_chip):
        x, y, c, chips = _place()
        h = ins[a].shape[1] // 2
        rows = pl.ds(pl.multiple_of(c * h, 8), h)
        return pltpu.make_async_remote_copy(
            src_ref=ins[a].at[2 * x + y, rows], dst_ref=outs[a].at[src_chip, rows], send_sem=sems[0].at[a * 3 + j],
            recv_sem=sems[1].at[a * 3 + j], device_id=(*chips[j], c), device_id_type=MESH)

    def d2d(ins, outs, sems, a, j, which):
        x, y, c, chips = _place()
        px, py = chips[j]
        h = ins[a].shape[1] // 2
        rows = pl.ds(pl.multiple_of(which * h, 8), h)
        return pltpu.make_async_remote_copy(
            src_ref=ins[a].at[2 * px + py, rows], dst_ref=outs[a].at[2 * px + py, rows], send_sem=sems[2].at[a * 3 + j],
            recv_sem=sems[3].at[a * 3 + j], device_id=(x, y, 1 - c), device_id_type=MESH)

    def start(ins, outs, sems):
        x, y, c, _ = _place()
        for a in range(n_ici):
            for j in range(3):
                ici(ins, outs, sems, a, j, 2 * x + y).start()
        for a in range(n_ici, n_ici + len(landed)):
            for j in range(3):
                d2d(ins, outs, sems, a, j, c).start()

    def land(ins, outs, sems, then_forward):
        _, _, c, chips = _place()
        for a in range(n_ici):
            for j, (px, py) in enumerate(chips):
                ici(ins, outs, sems, a, j, 2 * px + py).wait_recv()
                if then_forward:
                    d2d(ins, outs, sems, a, j, c).start()

    def middle(ins, outs, sems):
        land(ins, outs, sems, True)

    def finish(ins, outs, sems):
        x, y, c, _ = _place()
        if not forward:
            land(ins, outs, sems, False)
        for a in fwd:
            for j in range(3):
                d2d(ins, outs, sems, a, j, 1 - c).wait_recv()
        for a in range(n_ici):
            for j in range(3):
                ici(ins, outs, sems, a, j, 2 * x + y).wait_send()
        for a in fwd:
            for j in range(3):
                d2d(ins, outs, sems, a, j, c).wait_send()

    arrays = placed + landed
    n = len(arrays)
    return _Rider(arrays, [jax.ShapeDtypeStruct(p.shape, p.dtype) for p in arrays], [(a, a) for a in range(n)],
                  [pltpu.SemaphoreType.DMA((3 * n,))] * 4, start, finish, middle if (n_ici and forward) else None)


def _sum_chips(r, name):
    nk, rows, cols = r.shape
    tr = _tile(rows, max(8, (1 << 19) // cols), 8)

    def body(r_r, o_r):
        acc = r_r[0].astype(F32)
        for k in range(1, nk):
            acc = acc + r_r[k].astype(F32)
        o_r[...] = acc

    return pl.pallas_call(
        body, name=name, grid=(rows // tr,),
        in_specs=[pl.BlockSpec((nk, tr, cols), lambda i: (0, i, 0))],
        out_specs=pl.BlockSpec((tr, cols), lambda i: (i, 0)),
        out_shape=jax.ShapeDtypeStruct((rows, cols), F32),
        compiler_params=_params("parallel"),
    )(r)


def _sum_chips_into(own, arrived, dst, layer, n_layers, place_arr, name):
    _, h, cols = own.shape
    tr = _tile(h, max(8, (1 << 19) // cols), 8)
    nblk = h // tr

    def body(p_ref, own_r, arr_r, *rest):
        o_r = rest[-1]
        acc = own_r[...].astype(F32)
        for j in range(3):
            acc = acc + arr_r[j].astype(F32)
        o_r[...] = acc

    in_specs = [pl.BlockSpec((None, tr, cols), lambda i, p_ref: (p_ref[0], i, 0)),
                pl.BlockSpec((3, tr, cols), lambda i, p_ref: (0, i, 0))]
    args = [place_arr, own, arrived]
    aliases = {}
    if dst is not None:
        in_specs.append(ANY)
        args.append(dst)
        aliases = {3: 0}
    return pl.pallas_call(
        body, name=name,
        grid_spec=pltpu.PrefetchScalarGridSpec(
            num_scalar_prefetch=1, grid=(nblk,), in_specs=in_specs,
            out_specs=pl.BlockSpec((None, tr, cols), lambda i, p_ref: (layer, p_ref[1] * nblk + i, 0))),
        out_shape=jax.ShapeDtypeStruct((n_layers, 2 * h, cols), F32),
        input_output_aliases=aliases,
        compiler_params=_params("parallel"),
    )(*args)


def _ride_share(shards):
    n = len(shards)
    n_layers = shards[0].shape[0]

    def copy(ins, outs, sems, a, l, which):
        x, y, c, _ = _place()
        h = ins[a].shape[1] // 2
        rows = pl.ds(pl.multiple_of(which * h, 8), h)
        return pltpu.make_async_remote_copy(
            src_ref=ins[a].at[l, rows], dst_ref=outs[a].at[l, rows], send_sem=sems[0].at[a * n_layers + l],
            recv_sem=sems[1].at[a * n_layers + l], device_id=(x, y, 1 - c), device_id_type=MESH)

    def start(ins, outs, sems):
        c = lax.axis_index("c")
        for a in range(n):
            for l in range(n_layers):
                copy(ins, outs, sems, a, l, c).start()

    def finish(ins, outs, sems):
        c = lax.axis_index("c")
        for a in range(n):
            for l in range(n_layers):
                copy(ins, outs, sems, a, l, 1 - c).wait_recv()
        for a in range(n):
            for l in range(n_layers):
                copy(ins, outs, sems, a, l, c).wait_send()

    return _Rider(shards, [jax.ShapeDtypeStruct(a.shape, a.dtype) for a in shards], [(a, a) for a in range(n)],
                  [pltpu.SemaphoreType.DMA((n * n_layers,))] * 2, start, finish)


def _join_riders(first, second):
    k_in, k_out, k_sem = len(first.args), len(first.out_shape), len(first.sems)

    def start(ins, outs, sems):
        first.start(ins[:k_in], outs[:k_out], sems[:k_sem])
        second.start(ins[k_in:], outs[k_out:], sems[k_sem:])

    def finish(ins, outs, sems):
        first.finish(ins[:k_in], outs[:k_out], sems[:k_sem])
        second.finish(ins[k_in:], outs[k_out:], sems[k_sem:])

    return _Rider(first.args + second.args, first.out_shape + second.out_shape,
                  first.aliases + [(i + k_in, o + k_out) for i, o in second.aliases], first.sems + second.sems, start, finish)


def _adamw(w, g, m, v, name):
    rows, cols = w.shape
    tr = _tile(rows, max(8, (1 << 18) // cols), 8)
    c1 = 1.0 - ADAM_B1 ** ADAM_STEP
    c2 = 1.0 - ADAM_B2 ** ADAM_STEP

    def body(w_r, g_r, m_r, v_r, g_o, d_o, m_o, v_o):
        gv = g_r[...]
        g_o[...] = gv
        mn = ADAM_B1 * m_r[...] + (1.0 - ADAM_B1) * gv
        vn = ADAM_B2 * v_r[...] + (1.0 - ADAM_B2) * (gv * gv)
        m_o[...] = mn
        v_o[...] = vn
        d_o[...] = -ADAM_LR * ((mn / c1) / (jnp.sqrt(vn / c2) + ADAM_EPS) + ADAM_WD * w_r[...])

    spec = pl.BlockSpec((tr, cols), lambda i: (i, 0))
    return pl.pallas_call(
        body, name=name, grid=(rows // tr,), in_specs=[spec] * 4, out_specs=[spec] * 4,
        out_shape=[jax.ShapeDtypeStruct((rows, cols), F32)] * 4,
        compiler_params=_params("parallel"),
    )(w, g, m, v)


BIG = ("w_in", "w_out_a", "w_out_b", "w_pool", "w_o", "w_mlp1", "w_mlp2")
SMALL = ("g_mix", "b_in", "conv_b_bias", "ln_b_g", "ln_b_b", "b_out_b", "pool_scale", "g_mlp")
CONVS = ("conv_a", "conv_b")
WEIGHTS = ("g_mix", "w_in", "b_in", "conv_a", "w_out_a", "conv_b", "conv_b_bias", "ln_b_g", "ln_b_b", "w_out_b", "b_out_b",
           "w_pool", "pool_scale", "w_o", "g_mlp", "w_mlp1", "w_mlp2", "g_final")


def _as2d(a):
    return a.reshape(-1, a.shape[-1])


def _pad_rows(a, rows):
    return jnp.pad(a, ((0, rows - a.shape[0]), (0, 0)))


def _full_weights(big_g, conv_g, rep, l, d):
    gc = d // N_GROUPS
    w = {k: v[l] for k, v in rep.items()}
    w["b_in"] = rep["b_in"][l].reshape(1, -1)
    win3 = big_g["w_in"]
    w["win3"] = win3
    for src, dst in (("w_out_a", "woa"), ("w_out_b", "wob"), ("w_o", "wo")):
        full = big_g[src].reshape(d, d)
        w[dst] = full
        w[dst + "t"] = full.T
    wp = big_g["w_pool"].reshape(N_CHIPS, N_GROUPS, gc // N_CHIPS, gc).transpose(1, 0, 2, 3).reshape(N_GROUPS, gc, gc)
    w["wp"] = wp
    w["wpt"] = wp.transpose(0, 2, 1)
    w1_3 = big_g["w_mlp1"]
    w["w1_3"] = w1_3
    w["w1t"] = w1_3.transpose(0, 2, 1).reshape(-1, d)
    w2 = big_g["w_mlp2"].reshape(-1, d)
    w["w2"] = w2
    w["w2t"] = w2.T
    ca, cb = conv_g
    w["conv_a"] = ca
    w["conv_b"] = cb
    return w


def kernel(x, g_mix, w_in, b_in, conv_a, w_out_a, conv_b, conv_b_bias, ln_b_g, ln_b_b, w_out_b, b_out_b, w_pool, pool_scale, w_o, g_mlp, w_mlp1, w_mlp2, g_final, loss_target, m_g_mix, m_w_in, m_b_in, m_conv_a, m_w_out_a, m_conv_b, m_conv_b_bias, m_ln_b_g, m_ln_b_b, m_w_out_b, m_b_out_b, m_w_pool, m_pool_scale, m_w_o, m_g_mlp, m_w_mlp1, m_w_mlp2, m_g_final, v_g_mix, v_w_in, v_b_in, v_conv_a, v_w_out_a, v_conv_b, v_conv_b_bias, v_ln_b_g, v_ln_b_b, v_w_out_b, v_b_out_b, v_w_pool, v_pool_scale, v_w_o, v_g_mlp, v_w_mlp1, v_w_mlp2, v_g_final):
    given = dict(g_mix=g_mix, w_in=w_in, b_in=b_in, conv_a=conv_a, w_out_a=w_out_a, conv_b=conv_b, conv_b_bias=conv_b_bias, ln_b_g=ln_b_g, ln_b_b=ln_b_b, w_out_b=w_out_b, b_out_b=b_out_b, w_pool=w_pool, pool_scale=pool_scale, w_o=w_o, g_mlp=g_mlp, w_mlp1=w_mlp1, w_mlp2=w_mlp2, g_final=g_final)
    mom = dict(g_mix=m_g_mix, w_in=m_w_in, b_in=m_b_in, conv_a=m_conv_a, w_out_a=m_w_out_a, conv_b=m_conv_b, conv_b_bias=m_conv_b_bias, ln_b_g=m_ln_b_g, ln_b_b=m_ln_b_b, w_out_b=m_w_out_b, b_out_b=m_b_out_b, w_pool=m_w_pool, pool_scale=m_pool_scale, w_o=m_w_o, g_mlp=m_g_mlp, w_mlp1=m_w_mlp1, w_mlp2=m_w_mlp2, g_final=m_g_final)
    var = dict(g_mix=v_g_mix, w_in=v_w_in, b_in=v_b_in, conv_a=v_conv_a, w_out_a=v_w_out_a, conv_b=v_conv_b, conv_b_bias=v_conv_b_bias, ln_b_g=v_ln_b_g, ln_b_b=v_ln_b_b, w_out_b=v_w_out_b, b_out_b=v_b_out_b, w_pool=v_w_pool, pool_scale=v_pool_scale, w_o=v_w_o, g_mlp=v_g_mlp, w_mlp1=v_w_mlp1, w_mlp2=v_w_mlp2, g_final=v_g_final)
    n_layers = w_in.shape[0]
    s, d = x.shape[1], x.shape[2]
    dq = d // N_CHIPS
    x_idx, y_idx, c_idx = lax.axis_index("x"), lax.axis_index("y"), lax.axis_index("c")
    chip = 2 * x_idx + y_idx
    c_arr = c_idx.astype(jnp.int32).reshape(1)

    conv_rows = n_layers * (K_A + K_B)
    conv_pad = -(-conv_rows // 16) * 16
    conv_pack = _pad_rows(jnp.concatenate([conv_a[l] for l in range(n_layers)] + [conv_b[l] for l in range(n_layers)], axis=0),
                          conv_pad)
    chip_arr = chip.astype(jnp.int32).reshape(1)
    place_arr = jnp.stack([chip, c_idx]).astype(jnp.int32)

    def placed(k, l):
        return _place_shard(_as2d(given[k][l]), chip_arr, BF16, f"place_{k}_{l}")

    rest = BIG[1:]
    h, (win_g, conv_all) = _rms_fwd(
        x[0], given["g_mix"][0], "rms_fwd_0",
        _ride_gather([placed("w_in", 0), _place_shard(conv_pack, chip_arr, F32, "place_convs")]))
    conv_full = conv_all.transpose(1, 0, 2).reshape(conv_pad, d)
    conv_g = [(conv_full[l * K_A:(l + 1) * K_A], conv_full[n_layers * K_A + l * K_B:n_layers * K_A + (l + 1) * K_B])
              for l in range(n_layers)]
    rep = {k: given[k] for k in SMALL}

    xl = x[0]
    ws, saved = [], []
    rest_g = None
    for l in range(n_layers):
        more = l + 1 < n_layers

        def weights(carried, l=l, win_g=win_g, rest_g=rest_g):
            return _full_weights(dict(zip(rest, carried if rest_g is None else rest_g), w_in=win_g), conv_g[l], rep, l, d)

        xl, sv, w, got, h = _layer_fwd(
            xl, l, given["g_mix"][l], win_g, given["b_in"][l].reshape(1, -1), weights,
            _ride_gather([placed(k, l) for k in rest]) if rest_g is None else None,
            _ride_gather([placed("w_in", l + 1)], forward=False) if more else None,
            (lambda landed, l=l: _ride_gather([placed(k, l + 1) for k in rest], landed)) if more else None,
            h, given["g_mix"][l + 1] if more else None)
        if more:
            rest_g, win_g = got[:len(rest)], got[len(rest)]
        ws.append(w)
        saved.append(sv)
    dx, loss, dgf = _loss_head(xl, loss_target[0], g_final, "loss_head")
    loss = lax.psum(loss[0, 0], ("x", "y", "c"))

    parts, arrived, small = {}, {}, [None] * n_layers
    pending = []
    for l in reversed(range(n_layers)):
        w, sv = ws[l], saved[l]
        hosts = [[q for q in pending if q[1] not in ("w_mlp1", "w_mlp2")],
                 [q for q in pending if q[1] == "w_mlp2"],
                 [q for q in pending if q[1] == "w_mlp1"]]
        dx1, dproj, early, sm, got, swapped = _layer_bwd_early(
            dx, w, sv, l, *[_ride_exchange([parts[q] for q in qs]) if qs else None for qs in hosts], swap_early=(l == 0))
        arrived.update(zip(hosts[0] + hosts[1] + hosts[2], got))
        if l > 0:
            order = ("w_in",) + EARLY
            dx, dwin, dgmix, _, swapped = _layer_bwd_late(
                dx1, dproj, w, sv, l, None, lambda dwin, early=early: _ride_swap([dwin] + [early[k] for k in EARLY]))
            full = dict(early, w_in=dwin)
            for k, r in zip(order, swapped):
                parts[(l, k)] = _add_halves(full[k], r, c_arr, f"grad_add_halves_{k}_{l}")
            pending = [(l, k) for k in order]
        else:
            for k, r in zip(EARLY, swapped):
                parts[(l, k)] = _add_halves(early[k], r, c_arr, f"grad_add_halves_{k}_{l}")

            def w_in_rider(dwin, l=l):
                (r,) = _comm_call(_ride_swap([dwin]), f"grad_swap_w_in_{l}")
                parts[(l, "w_in")] = _add_halves(dwin, r, c_arr, f"grad_add_halves_w_in_{l}")
                return _ride_exchange([parts[(l, "w_in")]])

            dx, dwin, dgmix, got_early, got_w_in = _layer_bwd_late(
                dx1, dproj, w, sv, l, _ride_exchange([parts[(l, k)] for k in EARLY]), w_in_rider)
            arrived.update(zip([(l, k) for k in EARLY], got_early))
            arrived[(l, "w_in")] = got_w_in[0]
            pending = []
        sm["g_mix"] = dgmix
        small[l] = sm
    reduced = []
    for k in BIG:
        dst = None
        for l in range(n_layers):
            dst = _sum_chips_into(parts[(l, k)], arrived[(l, k)], dst, l, n_layers, place_arr, f"grad_sum_chips_{k}_{l}")
        reduced.append(dst)

    small_rows = []
    for k in SMALL + CONVS:
        for l in range(n_layers):
            small_rows.append(small[l][k])
    small_rows.append(dgf)
    pack = jnp.concatenate(small_rows, axis=0)
    n_small = pack.shape[0]
    pack = _pad_rows(pack, -(-n_small // 8) * 8)
    *shared, small_all = _comm_call(_join_riders(_ride_share(reduced), _ride_rows(pack)), "grad_share_and_small")
    big_grad = dict(zip(BIG, shared))
    small_sum = _sum_chips(small_all, "sum_small_grads")

    out = {}
    for k in BIG:
        shape = given[k].shape
        g2 = big_grad[k].reshape(-1, shape[-1])
        res = _adamw(given[k].reshape(g2.shape), g2, mom[k].reshape(g2.shape), var[k].reshape(g2.shape), f"adamw_{k}")
        out[k] = tuple(a.reshape(shape) for a in res)

    def rows_of(k):
        return {"b_in": 9, "conv_a": K_A, "conv_b": K_B}.get(k, 1)

    def pack_rep(src):
        rows = [src[k].reshape(n_layers * rows_of(k), d) for k in SMALL] + [src["g_final"].reshape(1, d)]
        a = jnp.concatenate(rows, axis=0)
        return _pad_rows(a, -(-a.shape[0] // 8) * 8)

    def pack_conv(src):
        a = jnp.concatenate([src[k].reshape(n_layers * rows_of(k), dq) for k in CONVS], axis=0)
        return _pad_rows(a, -(-a.shape[0] // 8) * 8)

    offs = {}
    r = 0
    for k in SMALL + CONVS:
        offs[k] = r
        r += n_layers * rows_of(k)
    offs["g_final"] = r
    n_rep = sum(n_layers * rows_of(k) for k in SMALL)
    g_rep = jnp.concatenate([small_sum[:n_rep], small_sum[offs["g_final"]:offs["g_final"] + 1]], axis=0)
    g_rep = _pad_rows(g_rep, -(-g_rep.shape[0] // 8) * 8)
    g_conv_full = small_sum[offs["conv_a"]:offs["g_final"]]
    g_conv = lax.dynamic_slice_in_dim(g_conv_full, chip * dq, dq, axis=1)
    g_conv = _pad_rows(g_conv, -(-g_conv.shape[0] // 8) * 8)
    rep_res = tuple(_adamw(pack_rep(given), g_rep, pack_rep(mom), pack_rep(var), "adamw_small"))
    conv_res = tuple(_adamw(pack_conv(given), g_conv, pack_conv(mom), pack_conv(var), "adamw_conv"))
    r = 0
    for k in SMALL:
        nr = n_layers * rows_of(k)
        out[k] = tuple(a[r:r + nr].reshape(given[k].shape) for a in rep_res)
        r += nr
    out["g_final"] = tuple(a[r].reshape(given["g_final"].shape) for a in rep_res)
    r = 0
    for k in CONVS:
        nr = n_layers * rows_of(k)
        out[k] = tuple(a[r:r + nr].reshape(given[k].shape) for a in conv_res)
        r += nr

    res = [loss, dx.reshape(x.shape)]
    for i in range(4):
        res += [out[k][i] for k in WEIGHTS]
    return tuple(res)
```

```python
import functools

import jax
import jax.numpy as jnp
from jax import lax
from jax.experimental import pallas as pl
from jax.experimental.pallas import tpu as pltpu

F32 = jnp.float32
BF16 = jnp.bfloat16
EPS = 1e-6
HALO = 32
SUB = 32
K_A = 3
K_B = 31
POOL_WINDOWS = (2, 4, 8, 16)
N_GROUPS = len(POOL_WINDOWS)
N_CHIPS = 4
N_DEV = 8
ADAM_LR = 0.001
ADAM_B1 = 0.9
ADAM_B2 = 0.999
ADAM_EPS = 1e-08
ADAM_WD = 0.01
ADAM_STEP = 10
VMEM_LIMIT = 56 * 1024 * 1024
MESH = pl.DeviceIdType.MESH
ANY = pl.BlockSpec(memory_space=pl.ANY)


def _params(*sem):
    return pltpu.CompilerParams(dimension_semantics=sem, vmem_limit_bytes=VMEM_LIMIT)


def _tile(n, pref, mult=8):
    if n <= pref:
        return n
    t = (pref // mult) * mult
    while t > mult and n % t:
        t -= mult
    assert n % t == 0, (n, pref, mult)
    return t


def _dot(a, b):
    return jnp.dot(a, b, preferred_element_type=F32)


def _sigmoid(x):
    return 1.0 / (1.0 + jnp.exp(-x))


def _f32(ref):
    return ref[...].astype(F32)


def _fold8(v):
    r, d = v.shape
    return v.reshape(r // 8, 8, d).sum(axis=0)


def _rms_bwd(dh, xv, g):
    r = lax.rsqrt(jnp.mean(xv * xv, axis=-1, keepdims=True) + EPS)
    xn = xv * r
    dxn = dh * g
    dx = r * (dxn - xn * jnp.mean(dxn * xn, axis=-1, keepdims=True))
    return dx, _fold8(dh * xn)


class _Rider:
    def __init__(self, args, out_shape, aliases, sems, start, finish, middle=None):
        self.args, self.out_shape, self.aliases, self.sems = list(args), list(out_shape), list(aliases), list(sems)
        self.start, self.finish, self.middle = start, finish, middle


def _hosted_call(body, *, name, grid, in_specs, out_specs, out_shape, scratch_shapes, args, rider=None):
    n_in, n_out, n_scr = len(in_specs), len(out_shape), len(scratch_shapes)
    params = pltpu.CompilerParams(dimension_semantics=("arbitrary",) * len(grid), vmem_limit_bytes=VMEM_LIMIT)
    if rider is None:
        res = pl.pallas_call(body, name=name, grid=grid, in_specs=in_specs, out_specs=out_specs, out_shape=out_shape,
                             scratch_shapes=scratch_shapes, compiler_params=params)(*args)
        return list(res), []
    k_in, k_out = len(rider.args), len(rider.out_shape)

    def hosted(*refs):
        ins, refs = refs[:n_in], refs[n_in:]
        r_in, refs = refs[:k_in], refs[k_in:]
        outs, refs = refs[:n_out], refs[n_out:]
        r_out, refs = refs[:k_out], refs[k_out:]
        scr, r_sems = refs[:n_scr], refs[n_scr:]
        first = functools.reduce(lambda u, v: u & v, [pl.program_id(a) == 0 for a in range(len(grid))])
        last = functools.reduce(lambda u, v: u & v, [pl.program_id(a) == grid[a] - 1 for a in range(len(grid))])

        @pl.when(first)
        def _():
            rider.start(r_in, r_out, r_sems)

        if rider.middle is not None:
            step = functools.reduce(lambda u, a: u * grid[a] + pl.program_id(a), range(len(grid)), 0)

            @pl.when(step == (3 * functools.reduce(lambda u, v: u * v, grid)) // 4)
            def _():
                rider.middle(r_in, r_out, r_sems)

        body(*ins, *outs, *scr)

        @pl.when(last)
        def _():
            rider.finish(r_in, r_out, r_sems)

    res = pl.pallas_call(
        hosted, name=name, grid=grid, in_specs=list(in_specs) + [ANY] * k_in, out_specs=list(out_specs) + [ANY] * k_out,
        out_shape=list(out_shape) + rider.out_shape, scratch_shapes=list(scratch_shapes) + rider.sems,
        input_output_aliases={n_in + i: n_out + o for i, o in rider.aliases}, compiler_params=params,
    )(*args, *rider.args)
    return list(res[:n_out]), list(res[n_out:])


def _comm_call(rider, name):
    def body(*refs):
        k_in, k_out = len(rider.args), len(rider.out_shape)
        r_in, r_out, r_sems = refs[:k_in], refs[k_in:k_in + k_out], refs[k_in + k_out:]
        rider.start(r_in, r_out, r_sems)
        if rider.middle is not None:
            rider.middle(r_in, r_out, r_sems)
        rider.finish(r_in, r_out, r_sems)

    return list(pl.pallas_call(
        body, name=name, in_specs=[ANY] * len(rider.args), out_specs=[ANY] * len(rider.out_shape),
        out_shape=rider.out_shape, scratch_shapes=rider.sems, input_output_aliases=dict(rider.aliases),
    )(*rider.args))


def _rms_fwd(x, g, name, rider=None):
    s, d = x.shape
    tm = _tile(s, 512)

    def body(x_ref, g_ref, h_ref):
        xv = x_ref[...]
        r = lax.rsqrt(jnp.mean(xv * xv, axis=-1, keepdims=True) + EPS)
        h_ref[...] = (xv * r * g_ref[...]).astype(BF16)

    (h,), carried = _hosted_call(
        body, name=name, grid=(s // tm,),
        in_specs=[pl.BlockSpec((tm, d), lambda i: (i, 0)), pl.BlockSpec((1, d), lambda i: (0, 0))],
        out_specs=[pl.BlockSpec((tm, d), lambda i: (i, 0))],
        out_shape=[jax.ShapeDtypeStruct((s, d), BF16)], scratch_shapes=[], args=(x, g.reshape(1, d)), rider=rider)
    return h, carried


def _proj(h, w3, b, name, rider=None):
    s, k = h.shape
    nb, _, nw = w3.shape
    tm = _tile(s, 1024)
    tn = nw
    per = nw // tn

    def body(h_ref, w_ref, b_ref, o_ref):
        o_ref[...] = (_dot(h_ref[...], w_ref[...]) + b_ref[...]).astype(BF16)

    (proj,), carried = _hosted_call(
        body, name=name, grid=(s // tm, nb * per),
        in_specs=[
            pl.BlockSpec((tm, k), lambda i, j: (i, 0)),
            pl.BlockSpec((None, k, tn), lambda i, j: (j // per, 0, j % per)),
            pl.BlockSpec((1, tn), lambda i, j: (0, j)),
        ],
        out_specs=[pl.BlockSpec((tm, tn), lambda i, j: (i, j))],
        out_shape=[jax.ShapeDtypeStruct((s, nb * nw), BF16)],
        scratch_shapes=[], args=(h, w3, b), rider=rider)
    return proj, carried


def _shift_copies(src_e, dst_s):
    n_rows = dst_s.shape[1]
    for b in range(1, 8):
        dst_s[b - 1, :, :] = src_e[pl.ds(b, n_rows), :]


def _window(src_e, dst_s, off, rows):
    b = off % 8
    if b == 0:
        return src_e[pl.ds(off, rows), :]
    return dst_s[b - 1, pl.ds(off - b, rows), :]


def _window_sums(src, levels, n, gc, trailing):
    prev = src
    for j in range(N_GROUPS):
        reach = 2 ** j
        lo = 8 * (j + 1) if trailing else 0
        rows = n - 8 * (j + 1)
        cols = slice(j * gc, N_GROUPS * gc)
        other = lo - reach if trailing else reach
        levels[j][pl.ds(lo, rows), cols] = prev[pl.ds(lo, rows), cols] + prev[pl.ds(other, rows), cols]
        prev = levels[j]


def _prev_halo(tm, d, col):
    r = tm // HALO
    return pl.BlockSpec((HALO, d), lambda i: (jnp.maximum(i * r - 1, 0), col))


def _next_halo(tm, d, col, n_halo_blocks):
    r = tm // HALO
    return pl.BlockSpec((HALO, d), lambda i: (jnp.minimum((i + 1) * r, n_halo_blocks - 1), col))


def _mix_fwd(proj, conv_a, conv_b, cb, lg, lb, name):
    s, d9 = proj.shape
    d = d9 // 9
    gc = d // N_GROUPS
    tm = _tile(s, 256, HALO)

    def tile(col):
        return pl.BlockSpec((tm, d), lambda i: (i, col))

    def vec(rows):
        return pl.BlockSpec((rows, d), lambda i: (0, 0))

    def body(ab, ac, ax, bv, bg, ci, ac_h, ax_h, bv_h, bg_h, ci_h, wa, wb, cb_r, lg_r, lb_r,
             pa_o, s_o, p_o, v_o, ua_e, ub_e, ci_e, ub_s, *levels):
        i = pl.program_id(0)
        keep = (i > 0).astype(F32)
        ua_e[0:HALO, :] = _f32(ac_h) * _f32(ax_h) * keep
        ua_e[HALO:HALO + tm, :] = _f32(ac) * _f32(ax)
        ub_e[0:HALO, :] = _f32(bv_h) * _sigmoid(_f32(bg_h)) * keep
        ub_e[HALO:HALO + tm, :] = _f32(bv) * _sigmoid(_f32(bg))
        ci_e[0:HALO, :] = _f32(ci_h) * keep
        ci_e[HALO:HALO + tm, :] = _f32(ci)
        _window_sums(ci_e, levels, HALO + tm, gc, trailing=True)
        _shift_copies(ub_e, ub_s)
        for r0 in range(0, tm, SUB):
            rows = slice(r0, r0 + SUB)
            z = None
            for k in range(K_A):
                t = wa[k:k + 1, :] * ua_e[pl.ds(HALO - (K_A - 1) + k + r0, SUB), :]
                z = t if z is None else z + t
            pa_o[rows, :] = (ab[rows, :].astype(F32) * z).astype(BF16)
            v = None
            for k in range(K_B):
                t = wb[k:k + 1, :] * _window(ub_e, ub_s, HALO - (K_B - 1) + k + r0, SUB)
                v = t if v is None else v + t
            v = v + cb_r[...]
            v_o[rows, :] = v
            mu = jnp.mean(v, axis=-1, keepdims=True)
            vc = v - mu
            rstd = lax.rsqrt(jnp.mean(vc * vc, axis=-1, keepdims=True) + EPS)
            ln = vc * rstd * lg_r[...] + lb_r[...]
            s_o[rows, :] = (ln * _sigmoid(ln)).astype(BF16)
            t_idx = i * tm + r0 + lax.broadcasted_iota(jnp.int32, (SUB, gc), 0)
            for g, w in enumerate(POOL_WINDOWS):
                cols = slice(g * gc, (g + 1) * gc)
                cur = ci_e[pl.ds(HALO + r0, SUB), cols]
                acc = levels[g][pl.ds(HALO + r0, SUB), cols]
                cnt = jnp.minimum(t_idx + 1, w).astype(F32)
                p_o[rows, cols] = (acc / cnt - cur).astype(BF16)

    return pl.pallas_call(
        body, name=name, grid=(s // tm,),
        in_specs=[tile(0), tile(1), tile(2), tile(3), tile(4), tile(5),
                  _prev_halo(tm, d, 1), _prev_halo(tm, d, 2), _prev_halo(tm, d, 3), _prev_halo(tm, d, 4),
                  _prev_halo(tm, d, 5),
                  vec(K_A), vec(K_B), vec(1), vec(1), vec(1)],
        out_specs=[pl.BlockSpec((tm, d), lambda i: (i, 0))] * 4,
        out_shape=[jax.ShapeDtypeStruct((s, d), BF16)] * 3 + [jax.ShapeDtypeStruct((s, d), F32)],
        scratch_shapes=[pltpu.VMEM((HALO + tm, d), F32)] * 3 + [pltpu.VMEM((7, tm + HALO - 8, d), F32)]
        + [pltpu.VMEM((HALO + tm, d), F32)] * N_GROUPS,
        compiler_params=_params("parallel"),
    )(proj, proj, proj, proj, proj, proj, proj, proj, proj, proj, proj,
      conv_a, conv_b, cb.reshape(1, d), lg.reshape(1, d), lb.reshape(1, d))


def _branch_out(pa, sb, p, proj, x0, woa, wob, wp, wo, bob, ps, gm, name, rider=None):
    s, d = pa.shape
    gc = d // N_GROUPS
    tm = _tile(s, 512)

    def tile(col=0):
        return pl.BlockSpec((tm, d), lambda i: (i, col))

    def const(shape):
        return pl.BlockSpec(shape, lambda i: (0,) * len(shape))

    def body(pa_r, s_r, p_r, g0, g1, g2, x0_r, woa_r, wob_r, wp_r, wo_r, bob_r, ps_r, gm_r,
             ya_o, yb_o, yc_o, mg_o, x1_o, h2_o, yc_s):
        ya = _dot(pa_r[...], woa_r[...])
        yb = _dot(s_r[...], wob_r[...]) + bob_r[...]
        for g in range(N_GROUPS):
            cols = slice(g * gc, (g + 1) * gc)
            yc_s[:, cols] = _dot(p_r[:, cols], wp_r[g])
        yc = yc_s[...]
        ya_o[...] = ya.astype(BF16)
        yb_o[...] = yb.astype(BF16)
        yc_o[...] = yc.astype(BF16)
        m = _sigmoid(_f32(g0)) * ya + _sigmoid(_f32(g1)) * yb + _sigmoid(_f32(g2)) * (yc * ps_r[...])
        mb = m.astype(BF16)
        mg_o[...] = mb
        x1 = x0_r[...] + _dot(mb, wo_r[...])
        x1_o[...] = x1
        r = lax.rsqrt(jnp.mean(x1 * x1, axis=-1, keepdims=True) + EPS)
        h2_o[...] = (x1 * r * gm_r[...]).astype(BF16)

    return _hosted_call(
        body, name=name, grid=(s // tm,),
        in_specs=[tile(), tile(), tile(), tile(6), tile(7), tile(8), tile(),
                  const((d, d)), const((d, d)), const((N_GROUPS, gc, gc)), const((d, d)),
                  const((1, d)), const((1, d)), const((1, d))],
        out_specs=[tile()] * 6,
        out_shape=[jax.ShapeDtypeStruct((s, d), BF16)] * 4 + [jax.ShapeDtypeStruct((s, d), F32),
                                                               jax.ShapeDtypeStruct((s, d), BF16)],
        scratch_shapes=[pltpu.VMEM((tm, d), F32)],
        args=(pa, sb, p, proj, proj, proj, x0, woa, wob, wp, wo, bob.reshape(1, d), ps.reshape(1, d), gm.reshape(1, d)),
        rider=rider)


def _mlp_fwd(h2, x1, w1_3, w2, g_next, name, rider=None):
    s, d = h2.shape
    nf, _, tf = w1_3.shape
    tm = _tile(s, 1024)

    chain = g_next is not None

    def body(h_r, x1_r, w1_r, w2_r, *rest):
        gn_r, f_o, a_o, x2_o, hn_o, acc = rest if chain else (None, *rest[:3], None, rest[3])
        j = pl.program_id(1)

        @pl.when(j == 0)
        def _():
            acc[...] = jnp.zeros_like(acc)

        f = _dot(h_r[...], w1_r[...])
        f_o[...] = f.astype(BF16)
        rl = jnp.maximum(f, 0.0)
        a = (rl * rl).astype(BF16)
        a_o[...] = a
        acc[...] += _dot(a, w2_r[...])

        @pl.when(j == nf - 1)
        def _():
            x2 = x1_r[...] + acc[...]
            x2_o[...] = x2
            if chain:
                r = lax.rsqrt(jnp.mean(x2 * x2, axis=-1, keepdims=True) + EPS)
                hn_o[...] = (x2 * r * gn_r[...]).astype(BF16)

    row = pl.BlockSpec((tm, d), lambda i, j: (i, 0))
    return _hosted_call(
        body, name=name, grid=(s // tm, nf),
        in_specs=[row, row, pl.BlockSpec((None, d, tf), lambda i, j: (j, 0, 0)), pl.BlockSpec((tf, d), lambda i, j: (j, 0))]
        + [pl.BlockSpec((1, d), lambda i, j: (0, 0))] * chain,
        out_specs=[pl.BlockSpec((tm, tf), lambda i, j: (i, j)), pl.BlockSpec((tm, tf), lambda i, j: (i, j)), row] + [row] * chain,
        out_shape=[jax.ShapeDtypeStruct((s, nf * tf), BF16), jax.ShapeDtypeStruct((s, nf * tf), BF16),
                   jax.ShapeDtypeStruct((s, d), F32)] + [jax.ShapeDtypeStruct((s, d), BF16)] * chain,
        scratch_shapes=[pltpu.VMEM((tm, d), F32)],
        args=(h2, x1, w1_3, w2) + ((g_next.reshape(1, d),) if chain else ()), rider=rider)


def _loss_head(x, target, gf, name):
    s, d = x.shape
    tm = _tile(s, 1024)
    n = s // tm

    def body(x_r, t_r, g_r, dx_o, loss_o, dg_o, lacc, gacc):
        i = pl.program_id(0)

        @pl.when(i == 0)
        def _():
            lacc[...] = jnp.zeros_like(lacc)
            gacc[...] = jnp.zeros_like(gacc)

        xv = x_r[...]
        r = lax.rsqrt(jnp.mean(xv * xv, axis=-1, keepdims=True) + EPS)
        xn = xv * r
        e = xn * g_r[...] - t_r[...]
        lacc[...] += _fold8(e * e)
        dy = e * (1.0 / d)
        gacc[...] += _fold8(dy * xn)
        dxn = dy * g_r[...]
        dx_o[...] = r * (dxn - xn * jnp.mean(dxn * xn, axis=-1, keepdims=True))

        @pl.when(i == n - 1)
        def _():
            loss_o[...] = jnp.sum(lacc[...]).reshape(1, 1) * (0.5 / d)
            dg_o[...] = jnp.sum(gacc[...], axis=0, keepdims=True)

    return pl.pallas_call(
        body, name=name, grid=(n,),
        in_specs=[pl.BlockSpec((tm, d), lambda i: (i, 0)), pl.BlockSpec((tm, d), lambda i: (i, 0)),
                  pl.BlockSpec((1, d), lambda i: (0, 0))],
        out_specs=[pl.BlockSpec((tm, d), lambda i: (i, 0)), pl.BlockSpec((1, 1), lambda i: (0, 0)),
                   pl.BlockSpec((1, d), lambda i: (0, 0))],
        out_shape=[jax.ShapeDtypeStruct((s, d), F32), jax.ShapeDtypeStruct((1, 1), F32),
                   jax.ShapeDtypeStruct((1, d), F32)],
        scratch_shapes=[pltpu.VMEM((8, d), F32), pltpu.VMEM((8, d), F32)],
        compiler_params=_params("arbitrary"),
    )(x, target, gf.reshape(1, d))


def _tn(a, b, nb, name, rider=None):
    s, m = a.shape
    n = b.shape[1]
    nw = n // nb
    tmm = _tile(m, 1024, 128)
    tn = nw if nw <= 1152 else nw // 2
    per = nw // tn
    ts = _tile(s, 2048)
    ns = s // ts

    def body(a_r, b_r, o_r, acc):
        k = pl.program_id(2)

        @pl.when(k == 0)
        def _():
            acc[...] = jnp.zeros_like(acc)

        acc[...] += lax.dot_general(a_r[...].astype(BF16), b_r[...].astype(BF16), (((0,), (0,)), ((), ())),
                                    preferred_element_type=F32)

        @pl.when(k == ns - 1)
        def _():
            o_r[...] = acc[...]

    (out,), carried = _hosted_call(
        body, name=name, grid=(m // tmm, nb * per, ns),
        in_specs=[pl.BlockSpec((ts, tmm), lambda i, j, k: (k, i)), pl.BlockSpec((ts, tn), lambda i, j, k: (k, j))],
        out_specs=[pl.BlockSpec((None, tmm, tn), lambda i, j, k: (j // per, i, j % per))],
        out_shape=[jax.ShapeDtypeStruct((nb, m, nw), F32)],
        scratch_shapes=[pltpu.VMEM((tmm, tn), F32)], args=(a, b), rider=rider)
    return (out, carried) if rider is not None else out


def _mlp_bwd(dx2, f, x1, w2t, w1t, gm, name, rider=None):
    s, d = dx2.shape
    ff = f.shape[1]
    tf = _tile(ff, 1024, 128)
    nf = ff // tf
    tm = _tile(s, 1024)
    n = s // tm

    def body(dx2_r, f_r, x1_r, w2t_r, w1t_r, gm_r, df_o, dx1_o, dg_o, dxb, acc, gacc):
        i = pl.program_id(0)
        j = pl.program_id(1)

        @pl.when((i == 0) & (j == 0))
        def _():
            gacc[...] = jnp.zeros_like(gacc)

        @pl.when(j == 0)
        def _():
            dxb[...] = dx2_r[...].astype(BF16)
            acc[...] = jnp.zeros_like(acc)

        da = _dot(dxb[...], w2t_r[...])
        df = (da * (2.0 * jnp.maximum(f_r[...].astype(F32), 0.0))).astype(BF16)
        df_o[...] = df
        acc[...] += _dot(df, w1t_r[...])

        @pl.when(j == nf - 1)
        def _():
            dx, dg = _rms_bwd(acc[...], x1_r[...], gm_r[...])
            gacc[...] += dg
            dx1_o[...] = dx2_r[...] + dx

        @pl.when((i == n - 1) & (j == nf - 1))
        def _():
            dg_o[...] = jnp.sum(gacc[...], axis=0, keepdims=True)

    return _hosted_call(
        body, name=name, grid=(n, nf),
        in_specs=[pl.BlockSpec((tm, d), lambda i, j: (i, 0)), pl.BlockSpec((tm, tf), lambda i, j: (i, j)),
                  pl.BlockSpec((tm, d), lambda i, j: (i, 0)), pl.BlockSpec((d, tf), lambda i, j: (0, j)),
                  pl.BlockSpec((tf, d), lambda i, j: (j, 0)), pl.BlockSpec((1, d), lambda i, j: (0, 0))],
        out_specs=[pl.BlockSpec((tm, tf), lambda i, j: (i, j)), pl.BlockSpec((tm, d), lambda i, j: (i, 0)),
                   pl.BlockSpec((1, d), lambda i, j: (0, 0))],
        out_shape=[jax.ShapeDtypeStruct((s, ff), BF16), jax.ShapeDtypeStruct((s, d), F32),
                   jax.ShapeDtypeStruct((1, d), F32)],
        scratch_shapes=[pltpu.VMEM((tm, d), BF16), pltpu.VMEM((tm, d), F32), pltpu.VMEM((8, d), F32)],
        args=(dx2, f, x1, w2t, w1t, gm.reshape(1, d)), rider=rider)


def _branch_out_bwd(dx1, proj, ya, yb, yc, pa, sb, p, wot, woat, wobt, wpt, ps, name):
    s, d = dx1.shape
    gc = d // N_GROUPS
    tm = _tile(s, 512)
    n = s // tm

    def tile(col=0):
        return pl.BlockSpec((tm, d), lambda i: (i, col))

    def const(shape):
        return pl.BlockSpec(shape, lambda i: (0,) * len(shape))

    def once(shape):
        return pl.BlockSpec(shape, lambda i: (0,) * len(shape), pipeline_mode=pl.Buffered(1))

    def tn(a, b):
        return lax.dot_general(a, b, (((0,), (0,)), ((), ())), preferred_element_type=F32)

    def body(dx1_r, g0, g1, g2, ya_r, yb_r, yc_r, pa_r, sb_r, p_r, wot_r, woat_r, wobt_r, wpt_r, ps_r,
             dpa_o, ds_o, dp_o, dpj_o, dbob_o, dps_o, dbg_o, dwoa_o, dwob_o, dwp_o, sacc, gacc, wa_acc, wb_acc, wp_acc, sem):
        i = pl.program_id(0)

        @pl.when(i == 0)
        def _():
            sacc[...] = jnp.zeros_like(sacc)
            gacc[...] = jnp.zeros_like(gacc)
            wa_acc[...] = jnp.zeros_like(wa_acc)
            wb_acc[...] = jnp.zeros_like(wb_acc)
            wp_acc[...] = jnp.zeros_like(wp_acc)

        dm = _dot(dx1_r[...].astype(BF16), wot_r[...])
        ycp = _f32(yc_r)
        ys = (_f32(ya_r), _f32(yb_r), ycp * ps_r[...])
        dys = []
        for b, g_r in enumerate((g0, g1, g2)):
            sg = _sigmoid(_f32(g_r))
            dgt = dm * ys[b] * sg * (1.0 - sg)
            dpj_o[:, b * d:(b + 1) * d] = dgt.astype(BF16)
            gacc[b] += _fold8(dgt)
            dys.append(dm * sg)
        dya, dyb, dyc = dys
        sacc[0] += _fold8(dyb)
        sacc[1] += _fold8(dyc * ycp)
        dyab = dya.astype(BF16)
        dybb = dyb.astype(BF16)
        dycb = (dyc * ps_r[...]).astype(BF16)
        wa_acc[...] += tn(pa_r[...], dyab)
        wb_acc[...] += tn(sb_r[...], dybb)
        dpa_o[...] = _dot(dyab, woat_r[...]).astype(BF16)
        ds_o[...] = _dot(dybb, wobt_r[...]).astype(BF16)
        for g in range(N_GROUPS):
            cols = slice(g * gc, (g + 1) * gc)
            wp_acc[g] += tn(p_r[:, cols], dycb[:, cols])
            dp_o[:, cols] = _dot(dycb[:, cols], wpt_r[g]).astype(BF16)

        @pl.when(i == n - 1)
        def _():
            dbob_o[...] = jnp.sum(sacc[0], axis=0, keepdims=True)
            dps_o[...] = jnp.sum(sacc[1], axis=0, keepdims=True)
            for b in range(3):
                dbg_o[:, b * d:(b + 1) * d] = jnp.sum(gacc[b], axis=0, keepdims=True)
            copies = [pltpu.make_async_copy(src, dst, sem.at[j])
                      for j, (src, dst) in enumerate(((wa_acc, dwoa_o), (wb_acc, dwob_o), (wp_acc, dwp_o)))]
            for cp in copies:
                cp.start()
            for cp in copies:
                cp.wait()

    return pl.pallas_call(
        body, name=name, grid=(n,),
        in_specs=[tile(), tile(6), tile(7), tile(8), tile(), tile(), tile(), tile(), tile(), tile(),
                  once((d, d)), once((d, d)), once((d, d)), once((N_GROUPS, gc, gc)), const((1, d))],
        out_specs=[tile(), tile(), tile(), pl.BlockSpec((tm, 3 * d), lambda i: (i, 2)),
                   const((1, d)), const((1, d)), const((1, 3 * d)), ANY, ANY, ANY],
        out_shape=[jax.ShapeDtypeStruct((s, d), BF16)] * 3 + [jax.ShapeDtypeStruct((s, 9 * d), BF16)]
        + [jax.ShapeDtypeStruct((1, d), F32), jax.ShapeDtypeStruct((1, d), F32), jax.ShapeDtypeStruct((1, 3 * d), F32)]
        + [jax.ShapeDtypeStruct((d, d), F32), jax.ShapeDtypeStruct((d, d), F32), jax.ShapeDtypeStruct((N_GROUPS, gc, gc), F32)],
        scratch_shapes=[pltpu.VMEM((2, 8, d), F32), pltpu.VMEM((3, 8, d), F32), pltpu.VMEM((d, d), F32),
                        pltpu.VMEM((d, d), F32), pltpu.VMEM((N_GROUPS, gc, gc), F32), pltpu.SemaphoreType.DMA((3,))],
        compiler_params=_params("arbitrary"),
    )(dx1, proj, proj, proj, ya, yb, yc, pa, sb, p, wot, woat, wobt, wpt, ps.reshape(1, d))


def _ln_silu_bwd(v, ds, lg, lb, name, rider=None):
    s, d = v.shape
    tm = _tile(s, 512)
    n = s // tm

    def tile():
        return pl.BlockSpec((tm, d), lambda i: (i, 0))

    def vec():
        return pl.BlockSpec((1, d), lambda i: (0, 0))

    def body(v_r, ds_r, lg_r, lb_r, dv_o, dlg_o, dlb_o, dcb_o, acc):
        i = pl.program_id(0)

        @pl.when(i == 0)
        def _():
            acc[...] = jnp.zeros_like(acc)

        for r0 in range(0, tm, SUB):
            rows = slice(r0, r0 + SUB)
            vv = v_r[rows, :]
            mu = jnp.mean(vv, axis=-1, keepdims=True)
            vc = vv - mu
            rstd = lax.rsqrt(jnp.mean(vc * vc, axis=-1, keepdims=True) + EPS)
            nrm = vc * rstd
            ln = nrm * lg_r[...] + lb_r[...]
            sg = _sigmoid(ln)
            dln = ds_r[rows, :].astype(F32) * (sg * (1.0 + ln * (1.0 - sg)))
            acc[0] += _fold8(dln * nrm)
            acc[1] += _fold8(dln)
            dn = dln * lg_r[...]
            dv = rstd * (dn - jnp.mean(dn, axis=-1, keepdims=True)
                         - nrm * jnp.mean(dn * nrm, axis=-1, keepdims=True))
            acc[2] += _fold8(dv)
            dv_o[rows, :] = dv

        @pl.when(i == n - 1)
        def _():
            dlg_o[...] = jnp.sum(acc[0], axis=0, keepdims=True)
            dlb_o[...] = jnp.sum(acc[1], axis=0, keepdims=True)
            dcb_o[...] = jnp.sum(acc[2], axis=0, keepdims=True)

    return _hosted_call(
        body, name=name, grid=(n,),
        in_specs=[tile(), tile(), vec(), vec()],
        out_specs=[tile(), vec(), vec(), vec()],
        out_shape=[jax.ShapeDtypeStruct((s, d), F32)] + [jax.ShapeDtypeStruct((1, d), F32)] * 3,
        scratch_shapes=[pltpu.VMEM((3, 8, d), F32)],
        args=(v, ds, lg.reshape(1, d), lb.reshape(1, d)), rider=rider)


def _mix_bwd(proj, dpa, dv, dp, dproj, conv_a, conv_b, name):
    s, d9 = proj.shape
    d = d9 // 9
    gc = d // N_GROUPS
    tm = _tile(s, 256, HALO)
    n = s // tm
    nh = s // HALO

    def tile(col=0):
        return pl.BlockSpec((tm, d), lambda i: (i, col))

    def vec(rows):
        return pl.BlockSpec((rows, d), lambda i: (0, 0))

    def body(ab, ac, ax, bv, bg, ac_h, ax_h, bv_h, bg_h, dpa_r, dv_r, dp_r, ab_n, dpa_n, dv_n, dp_n, wa, wb, _alias,
             dpj_o, dbin_o, dca_o, dcb_o, ua_e, ub_e, dz_e, dv_e, q_e, bacc, cacc_a, cacc_b, ub_s, dv_s, *levels):
        i = pl.program_id(0)

        @pl.when(i == 0)
        def _():
            bacc[...] = jnp.zeros_like(bacc)
            cacc_a[...] = jnp.zeros_like(cacc_a)
            cacc_b[...] = jnp.zeros_like(cacc_b)

        keep_p = (i > 0).astype(F32)
        keep_n = (i < n - 1).astype(F32)
        ua_e[0:HALO, :] = _f32(ac_h) * _f32(ax_h) * keep_p
        ua_e[HALO:HALO + tm, :] = _f32(ac) * _f32(ax)
        ub_e[0:HALO, :] = _f32(bv_h) * _sigmoid(_f32(bg_h)) * keep_p
        ub_e[HALO:HALO + tm, :] = _f32(bv) * _sigmoid(_f32(bg))
        dz_e[0:tm, :] = _f32(dpa_r) * _f32(ab)
        dz_e[tm:tm + HALO, :] = _f32(dpa_n) * _f32(ab_n) * keep_n
        dv_e[0:tm, :] = dv_r[...]
        dv_e[tm:tm + HALO, :] = dv_n[...] * keep_n
        _shift_copies(ub_e, ub_s)
        _shift_copies(dv_e, dv_s)
        for g, w in enumerate(POOL_WINDOWS):
            cols = slice(g * gc, (g + 1) * gc)
            t_idx = i * tm + lax.broadcasted_iota(jnp.int32, (tm + HALO, gc), 0)
            cnt = jnp.minimum(t_idx + 1, w).astype(F32)
            q_e[0:tm, cols] = dp_r[:, cols].astype(F32) / cnt[0:tm]
            q_e[tm:tm + HALO, cols] = dp_n[:, cols].astype(F32) * keep_n / cnt[tm:tm + HALO]
        _window_sums(q_e, levels, tm + HALO, gc, trailing=False)
        for r0 in range(0, tm, SUB):
            rows = slice(r0, r0 + SUB)
            dz = dz_e[pl.ds(r0, SUB), :]
            z = None
            du = None
            for k in range(K_A):
                u_k = ua_e[pl.ds(HALO - (K_A - 1) + k + r0, SUB), :]
                t = wa[k:k + 1, :] * u_k
                z = t if z is None else z + t
                cacc_a[k] += _fold8(dz * u_k)
                t = wa[k:k + 1, :] * dz_e[pl.ds(r0 + (K_A - 1) - k, SUB), :]
                du = t if du is None else du + t
            outs = [dpa_r[rows, :].astype(F32) * z, du * ax[rows, :].astype(F32), du * ac[rows, :].astype(F32)]
            dvv = dv_e[pl.ds(r0, SUB), :]
            for k in range(K_B):
                cacc_b[k] += _fold8(dvv * _window(ub_e, ub_s, HALO - (K_B - 1) + k + r0, SUB))
            halves = []
            for h0 in range(0, SUB, SUB // 2):
                acc = None
                for k in range(K_B):
                    t = wb[k:k + 1, :] * _window(dv_e, dv_s, r0 + h0 + (K_B - 1) - k, SUB // 2)
                    acc = t if acc is None else acc + t
                halves.append(acc)
            du = jnp.concatenate(halves, axis=0)
            sg = _sigmoid(bg[rows, :].astype(F32))
            outs.append(du * sg)
            outs.append(du * bv[rows, :].astype(F32) * sg * (1.0 - sg))
            for b, o in enumerate(outs):
                dpj_o[rows, b * d:(b + 1) * d] = o.astype(BF16)
                bacc[b] += _fold8(o)
            for g, w in enumerate(POOL_WINDOWS):
                cols = slice(g * gc, (g + 1) * gc)
                o = levels[g][pl.ds(r0, SUB), cols] - dp_r[rows, cols].astype(F32)
                dpj_o[rows, 5 * d + g * gc:5 * d + (g + 1) * gc] = o.astype(BF16)
                bacc[5, :, cols] += _fold8(o)

        @pl.when(i == n - 1)
        def _():
            for b in range(6):
                dbin_o[:, b * d:(b + 1) * d] = jnp.sum(bacc[b], axis=0, keepdims=True)
            dca_o[...] = jnp.sum(cacc_a[...], axis=1)
            dcb_o[...] = jnp.sum(cacc_b[...], axis=1)

    ka8 = 8
    kb8 = 32
    return pl.pallas_call(
        body, name=name, grid=(n,),
        in_specs=[tile(0), tile(1), tile(2), tile(3), tile(4),
                  _prev_halo(tm, d, 1), _prev_halo(tm, d, 2), _prev_halo(tm, d, 3), _prev_halo(tm, d, 4),
                  tile(), tile(), tile(),
                  _next_halo(tm, d, 0, nh), _next_halo(tm, d, 0, nh), _next_halo(tm, d, 0, nh), _next_halo(tm, d, 0, nh),
                  vec(K_A), vec(K_B), ANY],
        out_specs=[pl.BlockSpec((tm, 6 * d), lambda i: (i, 0)), pl.BlockSpec((1, 6 * d), lambda i: (0, 0)),
                   pl.BlockSpec((ka8, d), lambda i: (0, 0)), pl.BlockSpec((kb8, d), lambda i: (0, 0))],
        out_shape=[jax.ShapeDtypeStruct((s, 9 * d), BF16), jax.ShapeDtypeStruct((1, 6 * d), F32),
                   jax.ShapeDtypeStruct((ka8, d), F32), jax.ShapeDtypeStruct((kb8, d), F32)],
        scratch_shapes=[pltpu.VMEM((HALO + tm, d), F32), pltpu.VMEM((HALO + tm, d), F32),
                        pltpu.VMEM((tm + HALO, d), F32), pltpu.VMEM((tm + HALO, d), F32),
                        pltpu.VMEM((tm + HALO, d), F32),
                        pltpu.VMEM((6, 8, d), F32), pltpu.VMEM((ka8, 8, d), F32), pltpu.VMEM((kb8, 8, d), F32),
                        pltpu.VMEM((7, tm + HALO - 8, d), F32), pltpu.VMEM((7, tm + HALO - 8, d), F32)]
        + [pltpu.VMEM((tm + HALO, d), F32)] * N_GROUPS,
        input_output_aliases={18: 0},
        compiler_params=_params("arbitrary"),
    )(proj, proj, proj, proj, proj, proj, proj, proj, proj, dpa, dv, dp, proj, dpa, dv, dp, conv_a, conv_b, dproj)


def _dh_rms_bwd(dproj, win3, x0, dx1, g, name, rider=None):
    s, p = dproj.shape
    d = x0.shape[1]
    tm = _tile(s, 1024)
    nk, _, tk = win3.shape
    n = s // tm

    def body(dp_r, w_r, x_r, dx1_r, g_r, dx_o, dg_o, acc, gacc):
        i = pl.program_id(0)
        k = pl.program_id(1)

        @pl.when((i == 0) & (k == 0))
        def _():
            gacc[...] = jnp.zeros_like(gacc)

        @pl.when(k == 0)
        def _():
            acc[...] = jnp.zeros_like(acc)

        acc[...] += lax.dot_general(dp_r[...], w_r[...], (((1,), (1,)), ((), ())), preferred_element_type=F32)

        @pl.when(k == nk - 1)
        def _():
            dx, dg = _rms_bwd(acc[...], x_r[...], g_r[...])
            gacc[...] += dg
            dx_o[...] = dx1_r[...] + dx

        @pl.when((i == n - 1) & (k == nk - 1))
        def _():
            dg_o[...] = jnp.sum(gacc[...], axis=0, keepdims=True)

    return _hosted_call(
        body, name=name, grid=(n, nk),
        in_specs=[pl.BlockSpec((tm, tk), lambda i, k: (i, k)), pl.BlockSpec((None, d, tk), lambda i, k: (k, 0, 0)),
                  pl.BlockSpec((tm, d), lambda i, k: (i, 0)), pl.BlockSpec((tm, d), lambda i, k: (i, 0)),
                  pl.BlockSpec((1, d), lambda i, k: (0, 0))],
        out_specs=[pl.BlockSpec((tm, d), lambda i, k: (i, 0)), pl.BlockSpec((1, d), lambda i, k: (0, 0))],
        out_shape=[jax.ShapeDtypeStruct((s, d), F32), jax.ShapeDtypeStruct((1, d), F32)],
        scratch_shapes=[pltpu.VMEM((tm, d), F32), pltpu.VMEM((8, d), F32)],
        args=(dproj, win3, x0, dx1, g.reshape(1, d)), rider=rider)


EARLY = ("w_out_a", "w_out_b", "w_pool", "w_o", "w_mlp1", "w_mlp2")


def _layer_fwd(x0, l, g_mix, win3, b_in, weights, proj_rider=None, bo_rider=None, mlp_rider=None, h=None, g_next=None):
    if h is None:
        h, _ = _rms_fwd(x0, g_mix, f"rms_fwd_{l}")
    proj, carried_proj = _proj(h, win3, b_in, f"proj_{l}", proj_rider)
    w = weights(carried_proj) if callable(weights) else weights
    pa, sb, p, v = _mix_fwd(proj, w["conv_a"], w["conv_b"], w["conv_b_bias"], w["ln_b_g"], w["ln_b_b"], f"mix_fwd_{l}")
    (ya, yb, yc, mg, x1, h2), carried_bo = _branch_out(pa, sb, p, proj, x0, w["woa"], w["wob"], w["wp"], w["wo"],
                                                       w["b_out_b"], w["pool_scale"], w["g_mlp"], f"branch_out_{l}", bo_rider)
    rider = mlp_rider(carried_bo) if callable(mlp_rider) else mlp_rider
    (f, a, x2, *h_next), carried_mlp = _mlp_fwd(h2, x1, w["w1_3"], w["w2"], g_next, f"mlp_fwd_{l}", rider)
    saved = dict(x0=x0, h=h, proj=proj, pa=pa, sb=sb, p=p, v=v, ya=ya, yb=yb, yc=yc, mg=mg, x1=x1, h2=h2, f=f, a=a)
    return x2, saved, w, carried_mlp, (h_next[0] if h_next else None)


def _layer_bwd_early(dx2, w, sv, l, mlp_rider=None, dw2_rider=None, dw1_rider=None, swap_early=False):
    d = dx2.shape[1]
    (df, dx1, dgm), carried_mlp = _mlp_bwd(dx2, sv["f"], sv["x1"], w["w2t"], w["w1t"], w["g_mlp"], f"mlp_bwd_{l}",
                                           mlp_rider)
    if dw2_rider is None:
        dw2, carried_dw2 = _tn(sv["a"], dx2, 1, f"dw2_{l}"), []
    else:
        dw2, carried_dw2 = _tn(sv["a"], dx2, 1, f"dw2_{l}", dw2_rider)
    if dw1_rider is None:
        dw1, carried_dw1 = _tn(sv["h2"], df, N_CHIPS, f"dw1_{l}"), []
    else:
        dw1, carried_dw1 = _tn(sv["h2"], df, N_CHIPS, f"dw1_{l}", dw1_rider)
    dpa, ds, dp, dproj, dbob, dps, dbg, dwoa, dwob, dwp = _branch_out_bwd(
        dx1, sv["proj"], sv["ya"], sv["yb"], sv["yc"], sv["pa"], sv["sb"], sv["p"], w["wot"], w["woat"], w["wobt"],
        w["wpt"], w["pool_scale"], f"branch_out_bwd_{l}")
    dwo = _tn(sv["mg"], dx1, 1, f"dwo_{l}")
    ff = dw2.shape[1]
    gc = d // N_GROUPS
    big = dict(
        w_out_a=dwoa.reshape(N_CHIPS, d // N_CHIPS, d),
        w_out_b=dwob.reshape(N_CHIPS, d // N_CHIPS, d),
        w_pool=dwp.reshape(N_GROUPS, N_CHIPS, gc // N_CHIPS, gc).transpose(1, 0, 2, 3).reshape(N_CHIPS, gc, gc),
        w_o=dwo.reshape(N_CHIPS, d // N_CHIPS, d),
        w_mlp1=dw1,
        w_mlp2=dw2.reshape(N_CHIPS, ff // N_CHIPS, d),
    )
    swap = _ride_swap([big[k] for k in EARLY]) if swap_early else None
    (dv, dlg, dlb, dcb), swapped = _ln_silu_bwd(sv["v"], ds, w["ln_b_g"], w["ln_b_b"], f"ln_silu_bwd_{l}", swap)
    dproj, dbm, dca, dcvb = _mix_bwd(sv["proj"], dpa, dv, dp, dproj, w["conv_a"], w["conv_b"], f"mix_bwd_{l}")
    small = dict(
        b_in=jnp.concatenate([dbm, dbg], axis=1).reshape(9, d), conv_a=dca[:K_A], conv_b=dcvb[:K_B],
        conv_b_bias=dcb, ln_b_g=dlg, ln_b_b=dlb, b_out_b=dbob, pool_scale=dps, g_mlp=dgm,
    )
    return dx1, dproj, big, small, carried_mlp + carried_dw2 + carried_dw1, swapped


def _layer_bwd_late(dx1, dproj, w, sv, l, dwin_rider=None, dh_rider=None):
    if dwin_rider is None:
        dwin, carried_dwin = _tn(sv["h"], dproj, N_CHIPS, f"dwin_{l}"), []
    else:
        dwin, carried_dwin = _tn(sv["h"], dproj, N_CHIPS, f"dwin_{l}", dwin_rider)
    rider = dh_rider(dwin) if callable(dh_rider) else dh_rider
    (dx0, dgmix), carried_dh = _dh_rms_bwd(dproj, w["win3"], sv["x0"], dx1, w["g_mix"], f"dh_{l}", rider)
    return dx0, dwin, dgmix, carried_dwin, carried_dh


def _place():
    x, y, c = lax.axis_index("x"), lax.axis_index("y"), lax.axis_index("c")
    chips = [(1 - x, y), (x, 1 - y), (1 - x, 1 - y)]
    return x, y, c, chips


def _place_shard(w, chip_arr, dtype, name):
    rows, cols = w.shape
    tr = _tile(rows, max(16, (1 << 19) // cols), 16)

    def body(k_ref, w_r, o_r):
        o_r[...] = w_r[...].astype(dtype)

    return pl.pallas_call(
        body, name=name,
        grid_spec=pltpu.PrefetchScalarGridSpec(
            num_scalar_prefetch=1, grid=(rows // tr,),
            in_specs=[pl.BlockSpec((tr, cols), lambda i, k_ref: (i, 0))],
            out_specs=pl.BlockSpec((None, tr, cols), lambda i, k_ref: (k_ref[0], i, 0))),
        out_shape=jax.ShapeDtypeStruct((N_CHIPS, rows, cols), dtype),
        compiler_params=_params("parallel"),
    )(chip_arr, w)


def _ride_rows(v):
    def copy(ins, outs, sems, m, src_dev):
        x, y, c = lax.axis_index("x"), lax.axis_index("y"), lax.axis_index("c")
        peer = (x ^ ((m >> 2) & 1), y ^ ((m >> 1) & 1), c ^ (m & 1))
        return pltpu.make_async_remote_copy(
            src_ref=ins[0], dst_ref=outs[0].at[src_dev], send_sem=sems[0].at[m - 1], recv_sem=sems[1].at[m - 1],
            device_id=peer, device_id_type=MESH)

    def me():
        return 4 * lax.axis_index("x") + 2 * lax.axis_index("y") + lax.axis_index("c")

    def start(ins, outs, sems):
        pltpu.make_async_copy(ins[0], outs[0].at[me()], sems[2]).start()
        for m in range(1, N_DEV):
            copy(ins, outs, sems, m, me()).start()

    def finish(ins, outs, sems):
        for m in range(1, N_DEV):
            copy(ins, outs, sems, m, me() ^ m).wait_recv()
        for m in range(1, N_DEV):
            copy(ins, outs, sems, m, me()).wait_send()
        pltpu.make_async_copy(ins[0], outs[0].at[me()], sems[2]).wait()

    return _Rider([v], [jax.ShapeDtypeStruct((N_DEV, *v.shape), v.dtype)], [],
                  [pltpu.SemaphoreType.DMA((N_DEV - 1,)), pltpu.SemaphoreType.DMA((N_DEV - 1,)), pltpu.SemaphoreType.DMA],
                  start, finish)


def _ride_swap(grads):
    n = len(grads)

    def copy(ins, outs, sems, a, k):
        x, y, c, _ = _place()
        h = ins[a].shape[1] // 2
        return pltpu.make_async_remote_copy(
            src_ref=ins[a].at[k, pl.ds(pl.multiple_of((1 - c) * h, 8), h)], dst_ref=outs[a].at[k],
            send_sem=sems[0].at[a * N_CHIPS + k], recv_sem=sems[1].at[a * N_CHIPS + k], device_id=(x, y, 1 - c),
            device_id_type=MESH)

    def start(ins, outs, sems):
        for a in range(n):
            for k in range(N_CHIPS):
                copy(ins, outs, sems, a, k).start()

    def finish(ins, outs, sems):
        for a in range(n):
            for k in range(N_CHIPS):
                copy(ins, outs, sems, a, k).wait()

    return _Rider(grads, [jax.ShapeDtypeStruct((N_CHIPS, g.shape[1] // 2, g.shape[2]), g.dtype) for g in grads], [],
                  [pltpu.SemaphoreType.DMA((N_CHIPS * n,))] * 2, start, finish)


def _add_halves(g, r, c_arr, name):
    nk, rows, cols = g.shape
    h = rows // 2
    tr = _tile(h, max(8, (1 << 19) // cols), 8)
    nblk = h // tr

    def body(c_ref, g_r, r_r, o_r):
        o_r[...] = (g_r[...] + r_r[...]).astype(BF16)

    return pl.pallas_call(
        body, name=name,
        grid_spec=pltpu.PrefetchScalarGridSpec(
            num_scalar_prefetch=1, grid=(nk, nblk),
            in_specs=[pl.BlockSpec((None, tr, cols), lambda k, i, c_ref: (k, c_ref[0] * nblk + i, 0)),
                      pl.BlockSpec((None, tr, cols), lambda k, i, c_ref: (k, i, 0))],
            out_specs=pl.BlockSpec((None, tr, cols), lambda k, i, c_ref: (k, i, 0))),
        out_shape=jax.ShapeDtypeStruct((nk, h, cols), BF16),
        compiler_params=_params("parallel", "parallel"),
    )(c_arr, g, r)


def _ride_exchange(parts):
    n = len(parts)

    def copy(ins, outs, sems, a, j):
        x, y, c, chips = _place()
        px, py = chips[j]
        return pltpu.make_async_remote_copy(
            src_ref=ins[a].at[2 * px + py], dst_ref=outs[a].at[j], send_sem=sems[0].at[a * 3 + j],
            recv_sem=sems[1].at[a * 3 + j], device_id=(px, py, c), device_id_type=MESH)

    def start(ins, outs, sems):
        for a in range(n):
            for j in range(3):
                copy(ins, outs, sems, a, j).start()

    def finish(ins, outs, sems):
        for a in range(n):
            for j in range(3):
                copy(ins, outs, sems, a, j).wait()

    return _Rider(parts, [jax.ShapeDtypeStruct((3, *p.shape[1:]), p.dtype) for p in parts], [],
                  [pltpu.SemaphoreType.DMA((3 * n,))] * 2, start, finish)


def _ride_gather(placed=(), landed=(), forward=True):
    placed, landed = list(placed), list(landed)
    n_ici = len(placed)
    fwd = (list(range(n_ici)) if forward else []) + list(range(n_ici, n_ici + len(landed)))

    def ici(ins, outs, sems, a, j, src_chip):
        x, y, c, chips = _place()
        h = ins[a].shape[1] // 2
        rows = pl.ds(pl.multiple_of(c * h, 8), h)
        return pltpu.make_async_remote_copy(
            src_ref=ins[a].at[2 * x + y, rows], dst_ref=outs[a].at[src_chip, rows], send_sem=sems[0].at[a * 3 + j],
            recv_sem=sems[1].at[a * 3 + j], device_id=(*chips[j], c), device_id_type=MESH)

    def d2d(ins, outs, sems, a, j, which):
        x, y, c, chips = _place()
        px, py = chips[j]
        h = ins[a].shape[1] // 2
        rows = pl.ds(pl.multiple_of(which * h, 8), h)
        return pltpu.make_async_remote_copy(
            src_ref=ins[a].at[2 * px + py, rows], dst_ref=outs[a].at[2 * px + py, rows], send_sem=sems[2].at[a * 3 + j],
            recv_sem=sems[3].at[a * 3 + j], device_id=(x, y, 1 - c), device_id_type=MESH)

    def start(ins, outs, sems):
        x, y, c, _ = _place()
        for a in range(n_ici):
            for j in range(3):
                ici(ins, outs, sems, a, j, 2 * x + y).start()
        for a in range(n_ici, n_ici + len(landed)):
            for j in range(3):
                d2d(ins, outs, sems, a, j, c).start()

    def land(ins, outs, sems, then_forward):
        _, _, c, chips = _place()
        for a in range(n_ici):
            for j, (px, py) in enumerate(chips):
                ici(ins, outs, sems, a, j, 2 * px + py).wait_recv()
                if then_forward:
                    d2d(ins, outs, sems, a, j, c).start()

    def middle(ins, outs, sems):
        land(ins, outs, sems, True)

    def finish(ins, outs, sems):
        x, y, c, _ = _place()
        if not forward:
            land(ins, outs, sems, False)
        for a in fwd:
            for j in range(3):
                d2d(ins, outs, sems, a, j, 1 - c).wait_recv()
        for a in range(n_ici):
            for j in range(3):
                ici(ins, outs, sems, a, j, 2 * x + y).wait_send()
        for a in fwd:
            for j in range(3):
                d2d(ins, outs, sems, a, j, c).wait_send()

    arrays = placed + landed
    n = len(arrays)
    return _Rider(arrays, [jax.ShapeDtypeStruct(p.shape, p.dtype) for p in arrays], [(a, a) for a in range(n)],
                  [pltpu.SemaphoreType.DMA((3 * n,))] * 4, start, finish, middle if (n_ici and forward) else None)


def _sum_chips(r, name):
    nk, rows, cols = r.shape
    tr = _tile(rows, max(8, (1 << 19) // cols), 8)

    def body(r_r, o_r):
        acc = r_r[0].astype(F32)
        for k in range(1, nk):
            acc = acc + r_r[k].astype(F32)
        o_r[...] = acc

    return pl.pallas_call(
        body, name=name, grid=(rows // tr,),
        in_specs=[pl.BlockSpec((nk, tr, cols), lambda i: (0, i, 0))],
        out_specs=pl.BlockSpec((tr, cols), lambda i: (i, 0)),
        out_shape=jax.ShapeDtypeStruct((rows, cols), F32),
        compiler_params=_params("parallel"),
    )(r)


def _sum_chips_into(own, arrived, dst, layer, n_layers, place_arr, name):
    _, h, cols = own.shape
    tr = _tile(h, max(8, (1 << 19) // cols), 8)
    nblk = h // tr

    def body(p_ref, own_r, arr_r, *rest):
        o_r = rest[-1]
        acc = own_r[...].astype(F32)
        for j in range(3):
            acc = acc + arr_r[j].astype(F32)
        o_r[...] = acc

    in_specs = [pl.BlockSpec((None, tr, cols), lambda i, p_ref: (p_ref[0], i, 0)),
                pl.BlockSpec((3, tr, cols), lambda i, p_ref: (0, i, 0))]
    args = [place_arr, own, arrived]
    aliases = {}
    if dst is not None:
        in_specs.append(ANY)
        args.append(dst)
        aliases = {3: 0}
    return pl.pallas_call(
        body, name=name,
        grid_spec=pltpu.PrefetchScalarGridSpec(
            num_scalar_prefetch=1, grid=(nblk,), in_specs=in_specs,
            out_specs=pl.BlockSpec((None, tr, cols), lambda i, p_ref: (layer, p_ref[1] * nblk + i, 0))),
        out_shape=jax.ShapeDtypeStruct((n_layers, 2 * h, cols), F32),
        input_output_aliases=aliases,
        compiler_params=_params("parallel"),
    )(*args)


def _ride_share(shards):
    n = len(shards)
    n_layers = shards[0].shape[0]

    def copy(ins, outs, sems, a, l, which):
        x, y, c, _ = _place()
        h = ins[a].shape[1] // 2
        rows = pl.ds(pl.multiple_of(which * h, 8), h)
        return pltpu.make_async_remote_copy(
            src_ref=ins[a].at[l, rows], dst_ref=outs[a].at[l, rows], send_sem=sems[0].at[a * n_layers + l],
            recv_sem=sems[1].at[a * n_layers + l], device_id=(x, y, 1 - c), device_id_type=MESH)

    def start(ins, outs, sems):
        c = lax.axis_index("c")
        for a in range(n):
            for l in range(n_layers):
                copy(ins, outs, sems, a, l, c).start()

    def finish(ins, outs, sems):
        c = lax.axis_index("c")
        for a in range(n):
            for l in range(n_layers):
                copy(ins, outs, sems, a, l, 1 - c).wait_recv()
        for a in range(n):
            for l in range(n_layers):
                copy(ins, outs, sems, a, l, c).wait_send()

    return _Rider(shards, [jax.ShapeDtypeStruct(a.shape, a.dtype) for a in shards], [(a, a) for a in range(n)],
                  [pltpu.SemaphoreType.DMA((n * n_layers,))] * 2, start, finish)


def _join_riders(first, second):
    k_in, k_out, k_sem = len(first.args), len(first.out_shape), len(first.sems)

    def start(ins, outs, sems):
        first.start(ins[:k_in], outs[:k_out], sems[:k_sem])
        second.start(ins[k_in:], outs[k_out:], sems[k_sem:])

    def finish(ins, outs, sems):
        first.finish(ins[:k_in], outs[:k_out], sems[:k_sem])
        second.finish(ins[k_in:], outs[k_out:], sems[k_sem:])

    return _Rider(first.args + second.args, first.out_shape + second.out_shape,
                  first.aliases + [(i + k_in, o + k_out) for i, o in second.aliases], first.sems + second.sems, start, finish)


def _adamw(w, g, m, v, name):
    rows, cols = w.shape
    tr = _tile(rows, max(8, (1 << 18) // cols), 8)
    c1 = 1.0 - ADAM_B1 ** ADAM_STEP
    c2 = 1.0 - ADAM_B2 ** ADAM_STEP

    def body(w_r, g_r, m_r, v_r, g_o, d_o, m_o, v_o):
        gv = g_r[...]
        g_o[...] = gv
        mn = ADAM_B1 * m_r[...] + (1.0 - ADAM_B1) * gv
        vn = ADAM_B2 * v_r[...] + (1.0 - ADAM_B2) * (gv * gv)
        m_o[...] = mn
        v_o[...] = vn
        d_o[...] = -ADAM_LR * ((mn / c1) / (jnp.sqrt(vn / c2) + ADAM_EPS) + ADAM_WD * w_r[...])

    spec = pl.BlockSpec((tr, cols), lambda i: (i, 0))
    return pl.pallas_call(
        body, name=name, grid=(rows // tr,), in_specs=[spec] * 4, out_specs=[spec] * 4,
        out_shape=[jax.ShapeDtypeStruct((rows, cols), F32)] * 4,
        compiler_params=_params("parallel"),
    )(w, g, m, v)


BIG = ("w_in", "w_out_a", "w_out_b", "w_pool", "w_o", "w_mlp1", "w_mlp2")
SMALL = ("g_mix", "b_in", "conv_b_bias", "ln_b_g", "ln_b_b", "b_out_b", "pool_scale", "g_mlp")
CONVS = ("conv_a", "conv_b")
WEIGHTS = ("g_mix", "w_in", "b_in", "conv_a", "w_out_a", "conv_b", "conv_b_bias", "ln_b_g", "ln_b_b", "w_out_b", "b_out_b",
           "w_pool", "pool_scale", "w_o", "g_mlp", "w_mlp1", "w_mlp2", "g_final")


def _as2d(a):
    return a.reshape(-1, a.shape[-1])


def _pad_rows(a, rows):
    return jnp.pad(a, ((0, rows - a.shape[0]), (0, 0)))


def _full_weights(big_g, conv_g, rep, l, d):
    gc = d // N_GROUPS
    w = {k: v[l] for k, v in rep.items()}
    w["b_in"] = rep["b_in"][l].reshape(1, -1)
    win3 = big_g["w_in"]
    w["win3"] = win3
    for src, dst in (("w_out_a", "woa"), ("w_out_b", "wob"), ("w_o", "wo")):
        full = big_g[src].reshape(d, d)
        w[dst] = full
        w[dst + "t"] = full.T
    wp = big_g["w_pool"].reshape(N_CHIPS, N_GROUPS, gc // N_CHIPS, gc).transpose(1, 0, 2, 3).reshape(N_GROUPS, gc, gc)
    w["wp"] = wp
    w["wpt"] = wp.transpose(0, 2, 1)
    w1_3 = big_g["w_mlp1"]
    w["w1_3"] = w1_3
    w["w1t"] = w1_3.transpose(0, 2, 1).reshape(-1, d)
    w2 = big_g["w_mlp2"].reshape(-1, d)
    w["w2"] = w2
    w["w2t"] = w2.T
    ca, cb = conv_g
    w["conv_a"] = ca
    w["conv_b"] = cb
    return w


def kernel(x, g_mix, w_in, b_in, conv_a, w_out_a, conv_b, conv_b_bias, ln_b_g, ln_b_b, w_out_b, b_out_b, w_pool, pool_scale, w_o, g_mlp, w_mlp1, w_mlp2, g_final, loss_target, m_g_mix, m_w_in, m_b_in, m_conv_a, m_w_out_a, m_conv_b, m_conv_b_bias, m_ln_b_g, m_ln_b_b, m_w_out_b, m_b_out_b, m_w_pool, m_pool_scale, m_w_o, m_g_mlp, m_w_mlp1, m_w_mlp2, m_g_final, v_g_mix, v_w_in, v_b_in, v_conv_a, v_w_out_a, v_conv_b, v_conv_b_bias, v_ln_b_g, v_ln_b_b, v_w_out_b, v_b_out_b, v_w_pool, v_pool_scale, v_w_o, v_g_mlp, v_w_mlp1, v_w_mlp2, v_g_final):
    given = dict(g_mix=g_mix, w_in=w_in, b_in=b_in, conv_a=conv_a, w_out_a=w_out_a, conv_b=conv_b, conv_b_bias=conv_b_bias, ln_b_g=ln_b_g, ln_b_b=ln_b_b, w_out_b=w_out_b, b_out_b=b_out_b, w_pool=w_pool, pool_scale=pool_scale, w_o=w_o, g_mlp=g_mlp, w_mlp1=w_mlp1, w_mlp2=w_mlp2, g_final=g_final)
    mom = dict(g_mix=m_g_mix, w_in=m_w_in, b_in=m_b_in, conv_a=m_conv_a, w_out_a=m_w_out_a, conv_b=m_conv_b, conv_b_bias=m_conv_b_bias, ln_b_g=m_ln_b_g, ln_b_b=m_ln_b_b, w_out_b=m_w_out_b, b_out_b=m_b_out_b, w_pool=m_w_pool, pool_scale=m_pool_scale, w_o=m_w_o, g_mlp=m_g_mlp, w_mlp1=m_w_mlp1, w_mlp2=m_w_mlp2, g_final=m_g_final)
    var = dict(g_mix=v_g_mix, w_in=v_w_in, b_in=v_b_in, conv_a=v_conv_a, w_out_a=v_w_out_a, conv_b=v_conv_b, conv_b_bias=v_conv_b_bias, ln_b_g=v_ln_b_g, ln_b_b=v_ln_b_b, w_out_b=v_w_out_b, b_out_b=v_b_out_b, w_pool=v_w_pool, pool_scale=v_pool_scale, w_o=v_w_o, g_mlp=v_g_mlp, w_mlp1=v_w_mlp1, w_mlp2=v_w_mlp2, g_final=v_g_final)
    n_layers = w_in.shape[0]
    s, d = x.shape[1], x.shape[2]
    dq = d // N_CHIPS
    x_idx, y_idx, c_idx = lax.axis_index("x"), lax.axis_index("y"), lax.axis_index("c")
    chip = 2 * x_idx + y_idx
    c_arr = c_idx.astype(jnp.int32).reshape(1)

    conv_rows = n_layers * (K_A + K_B)
    conv_pad = -(-conv_rows // 16) * 16
    conv_pack = _pad_rows(jnp.concatenate([conv_a[l] for l in range(n_layers)] + [conv_b[l] for l in range(n_layers)], axis=0),
                          conv_pad)
    chip_arr = chip.astype(jnp.int32).reshape(1)
    place_arr = jnp.stack([chip, c_idx]).astype(jnp.int32)

    def placed(k, l):
        return _place_shard(_as2d(given[k][l]), chip_arr, BF16, f"place_{k}_{l}")

    rest = BIG[1:]
    h, (win_g, conv_all) = _rms_fwd(
        x[0], given["g_mix"][0], "rms_fwd_0",
        _ride_gather([placed("w_in", 0), _place_shard(conv_pack, chip_arr, F32, "place_convs")]))
    conv_full = conv_all.transpose(1, 0, 2).reshape(conv_pad, d)
    conv_g = [(conv_full[l * K_A:(l + 1) * K_A], conv_full[n_layers * K_A + l * K_B:n_layers * K_A + (l + 1) * K_B])
              for l in range(n_layers)]
    rep = {k: given[k] for k in SMALL}

    xl = x[0]
    ws, saved = [], []
    rest_g = None
    for l in range(n_layers):
        more = l + 1 < n_layers

        def weights(carried, l=l, win_g=win_g, rest_g=rest_g):
            return _full_weights(dict(zip(rest, carried if rest_g is None else rest_g), w_in=win_g), conv_g[l], rep, l, d)

        xl, sv, w, got, h = _layer_fwd(
            xl, l, given["g_mix"][l], win_g, given["b_in"][l].reshape(1, -1), weights,
            _ride_gather([placed(k, l) for k in rest]) if rest_g is None else None,
            _ride_gather([placed("w_in", l + 1)], forward=False) if more else None,
            (lambda landed, l=l: _ride_gather([placed(k, l + 1) for k in rest], landed)) if more else None,
            h, given["g_mix"][l + 1] if more else None)
        if more:
            rest_g, win_g = got[:len(rest)], got[len(rest)]
        ws.append(w)
        saved.append(sv)
    dx, loss, dgf = _loss_head(xl, loss_target[0], g_final, "loss_head")
    loss = lax.psum(loss[0, 0], ("x", "y", "c"))

    parts, arrived, small = {}, {}, [None] * n_layers
    pending = []
    for l in reversed(range(n_layers)):
        w, sv = ws[l], saved[l]
        hosts = [[q for q in pending if q[1] not in ("w_mlp1", "w_mlp2")],
                 [q for q in pending if q[1] == "w_mlp2"],
                 [q for q in pending if q[1] == "w_mlp1"]]
        dx1, dproj, early, sm, got, swapped = _layer_bwd_early(
            dx, w, sv, l, *[_ride_exchange([parts[q] for q in qs]) if qs else None for qs in hosts], swap_early=(l == 0))
        arrived.update(zip(hosts[0] + hosts[1] + hosts[2], got))
        if l > 0:
            order = ("w_in",) + EARLY
            dx, dwin, dgmix, _, swapped = _layer_bwd_late(
                dx1, dproj, w, sv, l, None, lambda dwin, early=early: _ride_swap([dwin] + [early[k] for k in EARLY]))
            full = dict(early, w_in=dwin)
            for k, r in zip(order, swapped):
                parts[(l, k)] = _add_halves(full[k], r, c_arr, f"grad_add_halves_{k}_{l}")
            pending = [(l, k) for k in order]
        else:
            for k, r in zip(EARLY, swapped):
                parts[(l, k)] = _add_halves(early[k], r, c_arr, f"grad_add_halves_{k}_{l}")

            def w_in_rider(dwin, l=l):
                (r,) = _comm_call(_ride_swap([dwin]), f"grad_swap_w_in_{l}")
                parts[(l, "w_in")] = _add_halves(dwin, r, c_arr, f"grad_add_halves_w_in_{l}")
                return _ride_exchange([parts[(l, "w_in")]])

            dx, dwin, dgmix, got_early, got_w_in = _layer_bwd_late(
                dx1, dproj, w, sv, l, _ride_exchange([parts[(l, k)] for k in EARLY]), w_in_rider)
            arrived.update(zip([(l, k) for k in EARLY], got_early))
            arrived[(l, "w_in")] = got_w_in[0]
            pending = []
        sm["g_mix"] = dgmix
        small[l] = sm
    reduced = []
    for k in BIG:
        dst = None
        for l in range(n_layers):
            dst = _sum_chips_into(parts[(l, k)], arrived[(l, k)], dst, l, n_layers, place_arr, f"grad_sum_chips_{k}_{l}")
        reduced.append(dst)

    small_rows = []
    for k in SMALL + CONVS:
        for l in range(n_layers):
            small_rows.append(small[l][k])
    small_rows.append(dgf)
    pack = jnp.concatenate(small_rows, axis=0)
    n_small = pack.shape[0]
    pack = _pad_rows(pack, -(-n_small // 8) * 8)
    *shared, small_all = _comm_call(_join_riders(_ride_share(reduced), _ride_rows(pack)), "grad_share_and_small")
    big_grad = dict(zip(BIG, shared))
    small_sum = _sum_chips(small_all, "sum_small_grads")

    out = {}
    for k in BIG:
        shape = given[k].shape
        g2 = big_grad[k].reshape(-1, shape[-1])
        res = _adamw(given[k].reshape(g2.shape), g2, mom[k].reshape(g2.shape), var[k].reshape(g2.shape), f"adamw_{k}")
        out[k] = tuple(a.reshape(shape) for a in res)

    def rows_of(k):
        return {"b_in": 9, "conv_a": K_A, "conv_b": K_B}.get(k, 1)

    def pack_rep(src):
        rows = [src[k].reshape(n_layers * rows_of(k), d) for k in SMALL] + [src["g_final"].reshape(1, d)]
        a = jnp.concatenate(rows, axis=0)
        return _pad_rows(a, -(-a.shape[0] // 8) * 8)

    def pack_conv(src):
        a = jnp.concatenate([src[k].reshape(n_layers * rows_of(k), dq) for k in CONVS], axis=0)
        return _pad_rows(a, -(-a.shape[0] // 8) * 8)

    offs = {}
    r = 0
    for k in SMALL + CONVS:
        offs[k] = r
        r += n_layers * rows_of(k)
    offs["g_final"] = r
    n_rep = sum(n_layers * rows_of(k) for k in SMALL)
    g_rep = jnp.concatenate([small_sum[:n_rep], small_sum[offs["g_final"]:offs["g_final"] + 1]], axis=0)
    g_rep = _pad_rows(g_rep, -(-g_rep.shape[0] // 8) * 8)
    g_conv_full = small_sum[offs["conv_a"]:offs["g_final"]]
    g_conv = lax.dynamic_slice_in_dim(g_conv_full, chip * dq, dq, axis=1)
    g_conv = _pad_rows(g_conv, -(-g_conv.shape[0] // 8) * 8)
    rep_res = tuple(_adamw(pack_rep(given), g_rep, pack_rep(mom), pack_rep(var), "adamw_small"))
    conv_res = tuple(_adamw(pack_conv(given), g_conv, pack_conv(mom), pack_conv(var), "adamw_conv"))
    r = 0
    for k in SMALL:
        nr = n_layers * rows_of(k)
        out[k] = tuple(a[r:r + nr].reshape(given[k].shape) for a in rep_res)
        r += nr
    out["g_final"] = tuple(a[r].reshape(given["g_final"].shape) for a in rep_res)
    r = 0
    for k in CONVS:
        nr = n_layers * rows_of(k)
        out[k] = tuple(a[r:r + nr].reshape(given[k].shape) for a in conv_res)
        r += nr

    res = [loss, dx.reshape(x.shape)]
    for i in range(4):
        res += [out[k][i] for k in WEIGHTS]
    return tuple(res)
```

```python
import functools

import jax
import jax.numpy as jnp
from jax import lax
from jax.experimental import pallas as pl
from jax.experimental.pallas import tpu as pltpu

F32 = jnp.float32
BF16 = jnp.bfloat16
EPS = 1e-6
HALO = 32
SUB = 32
K_A = 3
K_B = 31
POOL_WINDOWS = (2, 4, 8, 16)
N_GROUPS = len(POOL_WINDOWS)
N_CHIPS = 4
N_DEV = 8
ADAM_LR = 0.001
ADAM_B1 = 0.9
ADAM_B2 = 0.999
ADAM_EPS = 1e-08
ADAM_WD = 0.01
ADAM_STEP = 10
VMEM_LIMIT = 56 * 1024 * 1024
MESH = pl.DeviceIdType.MESH
ANY = pl.BlockSpec(memory_space=pl.ANY)


def _params(*sem):
    return pltpu.CompilerParams(dimension_semantics=sem, vmem_limit_bytes=VMEM_LIMIT)


def _tile(n, pref, mult=8):
    if n <= pref:
        return n
    t = (pref // mult) * mult
    while t > mult and n % t:
        t -= mult
    assert n % t == 0, (n, pref, mult)
    return t


def _dot(a, b):
    return jnp.dot(a, b, preferred_element_type=F32)


def _sigmoid(x):
    return 1.0 / (1.0 + jnp.exp(-x))


def _f32(ref):
    return ref[...].astype(F32)


def _fold8(v):
    r, d = v.shape
    return v.reshape(r // 8, 8, d).sum(axis=0)


def _rms_bwd(dh, xv, g):
    r = lax.rsqrt(jnp.mean(xv * xv, axis=-1, keepdims=True) + EPS)
    xn = xv * r
    dxn = dh * g
    dx = r * (dxn - xn * jnp.mean(dxn * xn, axis=-1, keepdims=True))
    return dx, _fold8(dh * xn)


class _Rider:
    def __init__(self, args, out_shape, aliases, sems, start, finish, middle=None):
        self.args, self.out_shape, self.aliases, self.sems = list(args), list(out_shape), list(aliases), list(sems)
        self.start, self.finish, self.middle = start, finish, middle


def _hosted_call(body, *, name, grid, in_specs, out_specs, out_shape, scratch_shapes, args, rider=None):
    n_in, n_out, n_scr = len(in_specs), len(out_shape), len(scratch_shapes)
    params = pltpu.CompilerParams(dimension_semantics=("arbitrary",) * len(grid), vmem_limit_bytes=VMEM_LIMIT)
    if rider is None:
        res = pl.pallas_call(body, name=name, grid=grid, in_specs=in_specs, out_specs=out_specs, out_shape=out_shape,
                             scratch_shapes=scratch_shapes, compiler_params=params)(*args)
        return list(res), []
    k_in, k_out = len(rider.args), len(rider.out_shape)

    def hosted(*refs):
        ins, refs = refs[:n_in], refs[n_in:]
        r_in, refs = refs[:k_in], refs[k_in:]
        outs, refs = refs[:n_out], refs[n_out:]
        r_out, refs = refs[:k_out], refs[k_out:]
        scr, r_sems = refs[:n_scr], refs[n_scr:]
        first = functools.reduce(lambda u, v: u & v, [pl.program_id(a) == 0 for a in range(len(grid))])
        last = functools.reduce(lambda u, v: u & v, [pl.program_id(a) == grid[a] - 1 for a in range(len(grid))])

        @pl.when(first)
        def _():
            rider.start(r_in, r_out, r_sems)

        if rider.middle is not None:
            step = functools.reduce(lambda u, a: u * grid[a] + pl.program_id(a), range(len(grid)), 0)

            @pl.when(step == (3 * functools.reduce(lambda u, v: u * v, grid)) // 4)
            def _():
                rider.middle(r_in, r_out, r_sems)

        body(*ins, *outs, *scr)

        @pl.when(last)
        def _():
            rider.finish(r_in, r_out, r_sems)

    res = pl.pallas_call(
        hosted, name=name, grid=grid, in_specs=list(in_specs) + [ANY] * k_in, out_specs=list(out_specs) + [ANY] * k_out,
        out_shape=list(out_shape) + rider.out_shape, scratch_shapes=list(scratch_shapes) + rider.sems,
        input_output_aliases={n_in + i: n_out + o for i, o in rider.aliases}, compiler_params=params,
    )(*args, *rider.args)
    return list(res[:n_out]), list(res[n_out:])


def _comm_call(rider, name):
    def body(*refs):
        k_in, k_out = len(rider.args), len(rider.out_shape)
        r_in, r_out, r_sems = refs[:k_in], refs[k_in:k_in + k_out], refs[k_in + k_out:]
        rider.start(r_in, r_out, r_sems)
        if rider.middle is not None:
            rider.middle(r_in, r_out, r_sems)
        rider.finish(r_in, r_out, r_sems)

    return list(pl.pallas_call(
        body, name=name, in_specs=[ANY] * len(rider.args), out_specs=[ANY] * len(rider.out_shape),
        out_shape=rider.out_shape, scratch_shapes=rider.sems, input_output_aliases=dict(rider.aliases),
    )(*rider.args))


def _rms_fwd(x, g, name, rider=None):
    s, d = x.shape
    tm = _tile(s, 512)

    def body(x_ref, g_ref, h_ref):
        xv = x_ref[...]
        r = lax.rsqrt(jnp.mean(xv * xv, axis=-1, keepdims=True) + EPS)
        h_ref[...] = (xv * r * g_ref[...]).astype(BF16)

    (h,), carried = _hosted_call(
        body, name=name, grid=(s // tm,),
        in_specs=[pl.BlockSpec((tm, d), lambda i: (i, 0)), pl.BlockSpec((1, d), lambda i: (0, 0))],
        out_specs=[pl.BlockSpec((tm, d), lambda i: (i, 0))],
        out_shape=[jax.ShapeDtypeStruct((s, d), BF16)], scratch_shapes=[], args=(x, g.reshape(1, d)), rider=rider)
    return h, carried


def _proj(h, w3, b, name, rider=None):
    s, k = h.shape
    nb, _, nw = w3.shape
    tm = _tile(s, 1024)
    tn = nw
    per = nw // tn

    def body(h_ref, w_ref, b_ref, o_ref):
        o_ref[...] = (_dot(h_ref[...], w_ref[...]) + b_ref[...]).astype(BF16)

    (proj,), carried = _hosted_call(
        body, name=name, grid=(s // tm, nb * per),
        in_specs=[
            pl.BlockSpec((tm, k), lambda i, j: (i, 0)),
            pl.BlockSpec((None, k, tn), lambda i, j: (j // per, 0, j % per)),
            pl.BlockSpec((1, tn), lambda i, j: (0, j)),
        ],
        out_specs=[pl.BlockSpec((tm, tn), lambda i, j: (i, j))],
        out_shape=[jax.ShapeDtypeStruct((s, nb * nw), BF16)],
        scratch_shapes=[], args=(h, w3, b), rider=rider)
    return proj, carried


def _shift_copies(src_e, dst_s):
    n_rows = dst_s.shape[1]
    for b in range(1, 8):
        dst_s[b - 1, :, :] = src_e[pl.ds(b, n_rows), :]


def _window(src_e, dst_s, off, rows):
    b = off % 8
    if b == 0:
        return src_e[pl.ds(off, rows), :]
    return dst_s[b - 1, pl.ds(off - b, rows), :]


def _window_sums(src, levels, n, gc, trailing):
    prev = src
    for j in range(N_GROUPS):
        reach = 2 ** j
        lo = 8 * (j + 1) if trailing else 0
        rows = n - 8 * (j + 1)
        cols = slice(j * gc, N_GROUPS * gc)
        other = lo - reach if trailing else reach
        levels[j][pl.ds(lo, rows), cols] = prev[pl.ds(lo, rows), cols] + prev[pl.ds(other, rows), cols]
        prev = levels[j]


def _prev_halo(tm, d, col):
    r = tm // HALO
    return pl.BlockSpec((HALO, d), lambda i: (jnp.maximum(i * r - 1, 0), col))


def _next_halo(tm, d, col, n_halo_blocks):
    r = tm // HALO
    return pl.BlockSpec((HALO, d), lambda i: (jnp.minimum((i + 1) * r, n_halo_blocks - 1), col))


def _mix_fwd(proj, conv_a, conv_b, cb, lg, lb, name):
    s, d9 = proj.shape
    d = d9 // 9
    gc = d // N_GROUPS
    tm = _tile(s, 256, HALO)

    def tile(col):
        return pl.BlockSpec((tm, d), lambda i: (i, col))

    def vec(rows):
        return pl.BlockSpec((rows, d), lambda i: (0, 0))

    def body(ab, ac, ax, bv, bg, ci, ac_h, ax_h, bv_h, bg_h, ci_h, wa, wb, cb_r, lg_r, lb_r,
             pa_o, s_o, p_o, v_o, ua_e, ub_e, ci_e, ub_s, *levels):
        i = pl.program_id(0)
        keep = (i > 0).astype(F32)
        ua_e[0:HALO, :] = _f32(ac_h) * _f32(ax_h) * keep
        ua_e[HALO:HALO + tm, :] = _f32(ac) * _f32(ax)
        ub_e[0:HALO, :] = _f32(bv_h) * _sigmoid(_f32(bg_h)) * keep
        ub_e[HALO:HALO + tm, :] = _f32(bv) * _sigmoid(_f32(bg))
        ci_e[0:HALO, :] = _f32(ci_h) * keep
        ci_e[HALO:HALO + tm, :] = _f32(ci)
        _window_sums(ci_e, levels, HALO + tm, gc, trailing=True)
        _shift_copies(ub_e, ub_s)
        for r0 in range(0, tm, SUB):
            rows = slice(r0, r0 + SUB)
            z = None
            for k in range(K_A):
                t = wa[k:k + 1, :] * ua_e[pl.ds(HALO - (K_A - 1) + k + r0, SUB), :]
                z = t if z is None else z + t
            pa_o[rows, :] = (ab[rows, :].astype(F32) * z).astype(BF16)
            v = None
            for k in range(K_B):
                t = wb[k:k + 1, :] * _window(ub_e, ub_s, HALO - (K_B - 1) + k + r0, SUB)
                v = t if v is None else v + t
            v = v + cb_r[...]
            v_o[rows, :] = v
            mu = jnp.mean(v, axis=-1, keepdims=True)
            vc = v - mu
            rstd = lax.rsqrt(jnp.mean(vc * vc, axis=-1, keepdims=True) + EPS)
            ln = vc * rstd * lg_r[...] + lb_r[...]
            s_o[rows, :] = (ln * _sigmoid(ln)).astype(BF16)
            t_idx = i * tm + r0 + lax.broadcasted_iota(jnp.int32, (SUB, gc), 0)
            for g, w in enumerate(POOL_WINDOWS):
                cols = slice(g * gc, (g + 1) * gc)
                cur = ci_e[pl.ds(HALO + r0, SUB), cols]
                acc = levels[g][pl.ds(HALO + r0, SUB), cols]
                cnt = jnp.minimum(t_idx + 1, w).astype(F32)
                p_o[rows, cols] = (acc / cnt - cur).astype(BF16)

    return pl.pallas_call(
        body, name=name, grid=(s // tm,),
        in_specs=[tile(0), tile(1), tile(2), tile(3), tile(4), tile(5),
                  _prev_halo(tm, d, 1), _prev_halo(tm, d, 2), _prev_halo(tm, d, 3), _prev_halo(tm, d, 4),
                  _prev_halo(tm, d, 5),
                  vec(K_A), vec(K_B), vec(1), vec(1), vec(1)],
        out_specs=[pl.BlockSpec((tm, d), lambda i: (i, 0))] * 4,
        out_shape=[jax.ShapeDtypeStruct((s, d), BF16)] * 3 + [jax.ShapeDtypeStruct((s, d), F32)],
        scratch_shapes=[pltpu.VMEM((HALO + tm, d), F32)] * 3 + [pltpu.VMEM((7, tm + HALO - 8, d), F32)]
        + [pltpu.VMEM((HALO + tm, d), F32)] * N_GROUPS,
        compiler_params=_params("parallel"),
    )(proj, proj, proj, proj, proj, proj, proj, proj, proj, proj, proj,
      conv_a, conv_b, cb.reshape(1, d), lg.reshape(1, d), lb.reshape(1, d))


def _branch_out(pa, sb, p, proj, x0, woa, wob, wp, wo, bob, ps, gm, name, rider=None):
    s, d = pa.shape
    gc = d // N_GROUPS
    tm = _tile(s, 512)

    def tile(col=0):
        return pl.BlockSpec((tm, d), lambda i: (i, col))

    def const(shape):
        return pl.BlockSpec(shape, lambda i: (0,) * len(shape))

    def body(pa_r, s_r, p_r, g0, g1, g2, x0_r, woa_r, wob_r, wp_r, wo_r, bob_r, ps_r, gm_r,
             ya_o, yb_o, yc_o, mg_o, x1_o, h2_o, yc_s):
        ya = _dot(pa_r[...], woa_r[...])
        yb = _dot(s_r[...], wob_r[...]) + bob_r[...]
        for g in range(N_GROUPS):
            cols = slice(g * gc, (g + 1) * gc)
            yc_s[:, cols] = _dot(p_r[:, cols], wp_r[g])
        yc = yc_s[...]
        ya_o[...] = ya.astype(BF16)
        yb_o[...] = yb.astype(BF16)
        yc_o[...] = yc.astype(BF16)
        m = _sigmoid(_f32(g0)) * ya + _sigmoid(_f32(g1)) * yb + _sigmoid(_f32(g2)) * (yc * ps_r[...])
        mb = m.astype(BF16)
        mg_o[...] = mb
        x1 = x0_r[...] + _dot(mb, wo_r[...])
        x1_o[...] = x1
        r = lax.rsqrt(jnp.mean(x1 * x1, axis=-1, keepdims=True) + EPS)
        h2_o[...] = (x1 * r * gm_r[...]).astype(BF16)

    return _hosted_call(
        body, name=name, grid=(s // tm,),
        in_specs=[tile(), tile(), tile(), tile(6), tile(7), tile(8), tile(),
                  const((d, d)), const((d, d)), const((N_GROUPS, gc, gc)), const((d, d)),
                  const((1, d)), const((1, d)), const((1, d))],
        out_specs=[tile()] * 6,
        out_shape=[jax.ShapeDtypeStruct((s, d), BF16)] * 4 + [jax.ShapeDtypeStruct((s, d), F32),
                                                               jax.ShapeDtypeStruct((s, d), BF16)],
        scratch_shapes=[pltpu.VMEM((tm, d), F32)],
        args=(pa, sb, p, proj, proj, proj, x0, woa, wob, wp, wo, bob.reshape(1, d), ps.reshape(1, d), gm.reshape(1, d)),
        rider=rider)


def _mlp_fwd(h2, x1, w1_3, w2, g_next, name, rider=None):
    s, d = h2.shape
    nf, _, tf = w1_3.shape
    tm = _tile(s, 1024)

    chain = g_next is not None

    def body(h_r, x1_r, w1_r, w2_r, *rest):
        gn_r, f_o, a_o, x2_o, hn_o, acc = rest if chain else (None, *rest[:3], None, rest[3])
        j = pl.program_id(1)

        @pl.when(j == 0)
        def _():
            acc[...] = jnp.zeros_like(acc)

        f = _dot(h_r[...], w1_r[...])
        f_o[...] = f.astype(BF16)
        rl = jnp.maximum(f, 0.0)
        a = (rl * rl).astype(BF16)
        a_o[...] = a
        acc[...] += _dot(a, w2_r[...])

        @pl.when(j == nf - 1)
        def _():
            x2 = x1_r[...] + acc[...]
            x2_o[...] = x2
            if chain:
                r = lax.rsqrt(jnp.mean(x2 * x2, axis=-1, keepdims=True) + EPS)
                hn_o[...] = (x2 * r * gn_r[...]).astype(BF16)

    row = pl.BlockSpec((tm, d), lambda i, j: (i, 0))
    return _hosted_call(
        body, name=name, grid=(s // tm, nf),
        in_specs=[row, row, pl.BlockSpec((None, d, tf), lambda i, j: (j, 0, 0)), pl.BlockSpec((tf, d), lambda i, j: (j, 0))]
        + [pl.BlockSpec((1, d), lambda i, j: (0, 0))] * chain,
        out_specs=[pl.BlockSpec((tm, tf), lambda i, j: (i, j)), pl.BlockSpec((tm, tf), lambda i, j: (i, j)), row] + [row] * chain,
        out_shape=[jax.ShapeDtypeStruct((s, nf * tf), BF16), jax.ShapeDtypeStruct((s, nf * tf), BF16),
                   jax.ShapeDtypeStruct((s, d), F32)] + [jax.ShapeDtypeStruct((s, d), BF16)] * chain,
        scratch_shapes=[pltpu.VMEM((tm, d), F32)],
        args=(h2, x1, w1_3, w2) + ((g_next.reshape(1, d),) if chain else ()), rider=rider)


def _loss_head(x, target, gf, name):
    s, d = x.shape
    tm = _tile(s, 1024)
    n = s // tm

    def body(x_r, t_r, g_r, dx_o, loss_o, dg_o, lacc, gacc):
        i = pl.program_id(0)

        @pl.when(i == 0)
        def _():
            lacc[...] = jnp.zeros_like(lacc)
            gacc[...] = jnp.zeros_like(gacc)

        xv = x_r[...]
        r = lax.rsqrt(jnp.mean(xv * xv, axis=-1, keepdims=True) + EPS)
        xn = xv * r
        e = xn * g_r[...] - t_r[...]
        lacc[...] += _fold8(e * e)
        dy = e * (1.0 / d)
        gacc[...] += _fold8(dy * xn)
        dxn = dy * g_r[...]
        dx_o[...] = r * (dxn - xn * jnp.mean(dxn * xn, axis=-1, keepdims=True))

        @pl.when(i == n - 1)
        def _():
            loss_o[...] = jnp.sum(lacc[...]).reshape(1, 1) * (0.5 / d)
            dg_o[...] = jnp.sum(gacc[...], axis=0, keepdims=True)

    return pl.pallas_call(
        body, name=name, grid=(n,),
        in_specs=[pl.BlockSpec((tm, d), lambda i: (i, 0)), pl.BlockSpec((tm, d), lambda i: (i, 0)),
                  pl.BlockSpec((1, d), lambda i: (0, 0))],
        out_specs=[pl.BlockSpec((tm, d), lambda i: (i, 0)), pl.BlockSpec((1, 1), lambda i: (0, 0)),
                   pl.BlockSpec((1, d), lambda i: (0, 0))],
        out_shape=[jax.ShapeDtypeStruct((s, d), F32), jax.ShapeDtypeStruct((1, 1), F32),
                   jax.ShapeDtypeStruct((1, d), F32)],
        scratch_shapes=[pltpu.VMEM((8, d), F32), pltpu.VMEM((8, d), F32)],
        compiler_params=_params("arbitrary"),
    )(x, target, gf.reshape(1, d))


def _tn(a, b, nb, name, rider=None):
    s, m = a.shape
    n = b.shape[1]
    nw = n // nb
    tmm = _tile(m, 1024, 128)
    tn = nw if nw <= 1152 else nw // 2
    per = nw // tn
    ts = _tile(s, 2048)
    ns = s // ts

    def body(a_r, b_r, o_r, acc):
        k = pl.program_id(2)

        @pl.when(k == 0)
        def _():
            acc[...] = jnp.zeros_like(acc)

        acc[...] += lax.dot_general(a_r[...].astype(BF16), b_r[...].astype(BF16), (((0,), (0,)), ((), ())),
                                    preferred_element_type=F32)

        @pl.when(k == ns - 1)
        def _():
            o_r[...] = acc[...]

    (out,), carried = _hosted_call(
        body, name=name, grid=(m // tmm, nb * per, ns),
        in_specs=[pl.BlockSpec((ts, tmm), lambda i, j, k: (k, i)), pl.BlockSpec((ts, tn), lambda i, j, k: (k, j))],
        out_specs=[pl.BlockSpec((None, tmm, tn), lambda i, j, k: (j // per, i, j % per))],
        out_shape=[jax.ShapeDtypeStruct((nb, m, nw), F32)],
        scratch_shapes=[pltpu.VMEM((tmm, tn), F32)], args=(a, b), rider=rider)
    return (out, carried) if rider is not None else out


def _mlp_bwd(dx2, f, x1, w2t, w1t, gm, name, rider=None):
    s, d = dx2.shape
    ff = f.shape[1]
    tf = _tile(ff, 1024, 128)
    nf = ff // tf
    tm = _tile(s, 1024)
    n = s // tm

    def body(dx2_r, f_r, x1_r, w2t_r, w1t_r, gm_r, df_o, dx1_o, dg_o, dxb, acc, gacc):
        i = pl.program_id(0)
        j = pl.program_id(1)

        @pl.when((i == 0) & (j == 0))
        def _():
            gacc[...] = jnp.zeros_like(gacc)

        @pl.when(j == 0)
        def _():
            dxb[...] = dx2_r[...].astype(BF16)
            acc[...] = jnp.zeros_like(acc)

        da = _dot(dxb[...], w2t_r[...])
        df = (da * (2.0 * jnp.maximum(f_r[...].astype(F32), 0.0))).astype(BF16)
        df_o[...] = df
        acc[...] += _dot(df, w1t_r[...])

        @pl.when(j == nf - 1)
        def _():
            dx, dg = _rms_bwd(acc[...], x1_r[...], gm_r[...])
            gacc[...] += dg
            dx1_o[...] = dx2_r[...] + dx

        @pl.when((i == n - 1) & (j == nf - 1))
        def _():
            dg_o[...] = jnp.sum(gacc[...], axis=0, keepdims=True)

    return _hosted_call(
        body, name=name, grid=(n, nf),
        in_specs=[pl.BlockSpec((tm, d), lambda i, j: (i, 0)), pl.BlockSpec((tm, tf), lambda i, j: (i, j)),
                  pl.BlockSpec((tm, d), lambda i, j: (i, 0)), pl.BlockSpec((d, tf), lambda i, j: (0, j)),
                  pl.BlockSpec((tf, d), lambda i, j: (j, 0)), pl.BlockSpec((1, d), lambda i, j: (0, 0))],
        out_specs=[pl.BlockSpec((tm, tf), lambda i, j: (i, j)), pl.BlockSpec((tm, d), lambda i, j: (i, 0)),
                   pl.BlockSpec((1, d), lambda i, j: (0, 0))],
        out_shape=[jax.ShapeDtypeStruct((s, ff), BF16), jax.ShapeDtypeStruct((s, d), F32),
                   jax.ShapeDtypeStruct((1, d), F32)],
        scratch_shapes=[pltpu.VMEM((tm, d), BF16), pltpu.VMEM((tm, d), F32), pltpu.VMEM((8, d), F32)],
        args=(dx2, f, x1, w2t, w1t, gm.reshape(1, d)), rider=rider)


def _branch_out_bwd(dx1, proj, ya, yb, yc, pa, sb, p, wot, woat, wobt, wpt, ps, name):
    s, d = dx1.shape
    gc = d // N_GROUPS
    tm = _tile(s, 512)
    n = s // tm

    def tile(col=0):
        return pl.BlockSpec((tm, d), lambda i: (i, col))

    def const(shape):
        return pl.BlockSpec(shape, lambda i: (0,) * len(shape))

    def once(shape):
        return pl.BlockSpec(shape, lambda i: (0,) * len(shape), pipeline_mode=pl.Buffered(1))

    def tn(a, b):
        return lax.dot_general(a, b, (((0,), (0,)), ((), ())), preferred_element_type=F32)

    def body(dx1_r, g0, g1, g2, ya_r, yb_r, yc_r, pa_r, sb_r, p_r, wot_r, woat_r, wobt_r, wpt_r, ps_r,
             dpa_o, ds_o, dp_o, dpj_o, dbob_o, dps_o, dbg_o, dwoa_o, dwob_o, dwp_o, sacc, gacc, wa_acc, wb_acc, wp_acc, sem):
        i = pl.program_id(0)

        @pl.when(i == 0)
        def _():
            sacc[...] = jnp.zeros_like(sacc)
            gacc[...] = jnp.zeros_like(gacc)
            wa_acc[...] = jnp.zeros_like(wa_acc)
            wb_acc[...] = jnp.zeros_like(wb_acc)
            wp_acc[...] = jnp.zeros_like(wp_acc)

        dm = _dot(dx1_r[...].astype(BF16), wot_r[...])
        ycp = _f32(yc_r)
        ys = (_f32(ya_r), _f32(yb_r), ycp * ps_r[...])
        dys = []
        for b, g_r in enumerate((g0, g1, g2)):
            sg = _sigmoid(_f32(g_r))
            dgt = dm * ys[b] * sg * (1.0 - sg)
            dpj_o[:, b * d:(b + 1) * d] = dgt.astype(BF16)
            gacc[b] += _fold8(dgt)
            dys.append(dm * sg)
        dya, dyb, dyc = dys
        sacc[0] += _fold8(dyb)
        sacc[1] += _fold8(dyc * ycp)
        dyab = dya.astype(BF16)
        dybb = dyb.astype(BF16)
        dycb = (dyc * ps_r[...]).astype(BF16)
        wa_acc[...] += tn(pa_r[...], dyab)
        wb_acc[...] += tn(sb_r[...], dybb)
        dpa_o[...] = _dot(dyab, woat_r[...]).astype(BF16)
        ds_o[...] = _dot(dybb, wobt_r[...]).astype(BF16)
        for g in range(N_GROUPS):
            cols = slice(g * gc, (g + 1) * gc)
            wp_acc[g] += tn(p_r[:, cols], dycb[:, cols])
            dp_o[:, cols] = _dot(dycb[:, cols], wpt_r[g]).astype(BF16)

        @pl.when(i == n - 1)
        def _():
            dbob_o[...] = jnp.sum(sacc[0], axis=0, keepdims=True)
            dps_o[...] = jnp.sum(sacc[1], axis=0, keepdims=True)
            for b in range(3):
                dbg_o[:, b * d:(b + 1) * d] = jnp.sum(gacc[b], axis=0, keepdims=True)
            copies = [pltpu.make_async_copy(src, dst, sem.at[j])
                      for j, (src, dst) in enumerate(((wa_acc, dwoa_o), (wb_acc, dwob_o), (wp_acc, dwp_o)))]
            for cp in copies:
                cp.start()
            for cp in copies:
                cp.wait()

    return pl.pallas_call(
        body, name=name, grid=(n,),
        in_specs=[tile(), tile(6), tile(7), tile(8), tile(), tile(), tile(), tile(), tile(), tile(),
                  once((d, d)), once((d, d)), once((d, d)), once((N_GROUPS, gc, gc)), const((1, d))],
        out_specs=[tile(), tile(), tile(), pl.BlockSpec((tm, 3 * d), lambda i: (i, 2)),
                   const((1, d)), const((1, d)), const((1, 3 * d)), ANY, ANY, ANY],
        out_shape=[jax.ShapeDtypeStruct((s, d), BF16)] * 3 + [jax.ShapeDtypeStruct((s, 9 * d), BF16)]
        + [jax.ShapeDtypeStruct((1, d), F32), jax.ShapeDtypeStruct((1, d), F32), jax.ShapeDtypeStruct((1, 3 * d), F32)]
        + [jax.ShapeDtypeStruct((d, d), F32), jax.ShapeDtypeStruct((d, d), F32), jax.ShapeDtypeStruct((N_GROUPS, gc, gc), F32)],
        scratch_shapes=[pltpu.VMEM((2, 8, d), F32), pltpu.VMEM((3, 8, d), F32), pltpu.VMEM((d, d), F32),
                        pltpu.VMEM((d, d), F32), pltpu.VMEM((N_GROUPS, gc, gc), F32), pltpu.SemaphoreType.DMA((3,))],
        compiler_params=_params("arbitrary"),
    )(dx1, proj, proj, proj, ya, yb, yc, pa, sb, p, wot, woat, wobt, wpt, ps.reshape(1, d))


def _ln_silu_bwd(v, ds, lg, lb, name, rider=None):
    s, d = v.shape
    tm = _tile(s, 512)
    n = s // tm

    def tile():
        return pl.BlockSpec((tm, d), lambda i: (i, 0))

    def vec():
        return pl.BlockSpec((1, d), lambda i: (0, 0))

    def body(v_r, ds_r, lg_r, lb_r, dv_o, dlg_o, dlb_o, dcb_o, acc):
        i = pl.program_id(0)

        @pl.when(i == 0)
        def _():
            acc[...] = jnp.zeros_like(acc)

        for r0 in range(0, tm, SUB):
            rows = slice(r0, r0 + SUB)
            vv = v_r[rows, :]
            mu = jnp.mean(vv, axis=-1, keepdims=True)
            vc = vv - mu
            rstd = lax.rsqrt(jnp.mean(vc * vc, axis=-1, keepdims=True) + EPS)
            nrm = vc * rstd
            ln = nrm * lg_r[...] + lb_r[...]
            sg = _sigmoid(ln)
            dln = ds_r[rows, :].astype(F32) * (sg * (1.0 + ln * (1.0 - sg)))
            acc[0] += _fold8(dln * nrm)
            acc[1] += _fold8(dln)
            dn = dln * lg_r[...]
            dv = rstd * (dn - jnp.mean(dn, axis=-1, keepdims=True)
                         - nrm * jnp.mean(dn * nrm, axis=-1, keepdims=True))
            acc[2] += _fold8(dv)
            dv_o[rows, :] = dv

        @pl.when(i == n - 1)
        def _():
            dlg_o[...] = jnp.sum(acc[0], axis=0, keepdims=True)
            dlb_o[...] = jnp.sum(acc[1], axis=0, keepdims=True)
            dcb_o[...] = jnp.sum(acc[2], axis=0, keepdims=True)

    return _hosted_call(
        body, name=name, grid=(n,),
        in_specs=[tile(), tile(), vec(), vec()],
        out_specs=[tile(), vec(), vec(), vec()],
        out_shape=[jax.ShapeDtypeStruct((s, d), F32)] + [jax.ShapeDtypeStruct((1, d), F32)] * 3,
        scratch_shapes=[pltpu.VMEM((3, 8, d), F32)],
        args=(v, ds, lg.reshape(1, d), lb.reshape(1, d)), rider=rider)


def _mix_bwd(proj, dpa, dv, dp, dproj, conv_a, conv_b, name):
    s, d9 = proj.shape
    d = d9 // 9
    gc = d // N_GROUPS
    tm = _tile(s, 256, HALO)
    n = s // tm
    nh = s // HALO

    def tile(col=0):
        return pl.BlockSpec((tm, d), lambda i: (i, col))

    def vec(rows):
        return pl.BlockSpec((rows, d), lambda i: (0, 0))

    def body(ab, ac, ax, bv, bg, ac_h, ax_h, bv_h, bg_h, dpa_r, dv_r, dp_r, ab_n, dpa_n, dv_n, dp_n, wa, wb, _alias,
             dpj_o, dbin_o, dca_o, dcb_o, ua_e, ub_e, dz_e, dv_e, q_e, bacc, cacc_a, cacc_b, ub_s, dv_s, *levels):
        i = pl.program_id(0)

        @pl.when(i == 0)
        def _():
            bacc[...] = jnp.zeros_like(bacc)
            cacc_a[...] = jnp.zeros_like(cacc_a)
            cacc_b[...] = jnp.zeros_like(cacc_b)

        keep_p = (i > 0).astype(F32)
        keep_n = (i < n - 1).astype(F32)
        ua_e[0:HALO, :] = _f32(ac_h) * _f32(ax_h) * keep_p
        ua_e[HALO:HALO + tm, :] = _f32(ac) * _f32(ax)
        ub_e[0:HALO, :] = _f32(bv_h) * _sigmoid(_f32(bg_h)) * keep_p
        ub_e[HALO:HALO + tm, :] = _f32(bv) * _sigmoid(_f32(bg))
        dz_e[0:tm, :] = _f32(dpa_r) * _f32(ab)
        dz_e[tm:tm + HALO, :] = _f32(dpa_n) * _f32(ab_n) * keep_n
        dv_e[0:tm, :] = dv_r[...]
        dv_e[tm:tm + HALO, :] = dv_n[...] * keep_n
        _shift_copies(ub_e, ub_s)
        _shift_copies(dv_e, dv_s)
        for g, w in enumerate(POOL_WINDOWS):
            cols = slice(g * gc, (g + 1) * gc)
            t_idx = i * tm + lax.broadcasted_iota(jnp.int32, (tm + HALO, gc), 0)
            cnt = jnp.minimum(t_idx + 1, w).astype(F32)
            q_e[0:tm, cols] = dp_r[:, cols].astype(F32) / cnt[0:tm]
            q_e[tm:tm + HALO, cols] = dp_n[:, cols].astype(F32) * keep_n / cnt[tm:tm + HALO]
        _window_sums(q_e, levels, tm + HALO, gc, trailing=False)
        for r0 in range(0, tm, SUB):
            rows = slice(r0, r0 + SUB)
            dz = dz_e[pl.ds(r0, SUB), :]
            z = None
            du = None
            for k in range(K_A):
                u_k = ua_e[pl.ds(HALO - (K_A - 1) + k + r0, SUB), :]
                t = wa[k:k + 1, :] * u_k
                z = t if z is None else z + t
                cacc_a[k] += _fold8(dz * u_k)
                t = wa[k:k + 1, :] * dz_e[pl.ds(r0 + (K_A - 1) - k, SUB), :]
                du = t if du is None else du + t
            outs = [dpa_r[rows, :].astype(F32) * z, du * ax[rows, :].astype(F32), du * ac[rows, :].astype(F32)]
            dvv = dv_e[pl.ds(r0, SUB), :]
            for k in range(K_B):
                cacc_b[k] += _fold8(dvv * _window(ub_e, ub_s, HALO - (K_B - 1) + k + r0, SUB))
            halves = []
            for h0 in range(0, SUB, SUB // 2):
                acc = None
                for k in range(K_B):
                    t = wb[k:k + 1, :] * _window(dv_e, dv_s, r0 + h0 + (K_B - 1) - k, SUB // 2)
                    acc = t if acc is None else acc + t
                halves.append(acc)
            du = jnp.concatenate(halves, axis=0)
            sg = _sigmoid(bg[rows, :].astype(F32))
            outs.append(du * sg)
            outs.append(du * bv[rows, :].astype(F32) * sg * (1.0 - sg))
            for b, o in enumerate(outs):
                dpj_o[rows, b * d:(b + 1) * d] = o.astype(BF16)
                bacc[b] += _fold8(o)
            for g, w in enumerate(POOL_WINDOWS):
                cols = slice(g * gc, (g + 1) * gc)
                o = levels[g][pl.ds(r0, SUB), cols] - dp_r[rows, cols].astype(F32)
                dpj_o[rows, 5 * d + g * gc:5 * d + (g + 1) * gc] = o.astype(BF16)
                bacc[5, :, cols] += _fold8(o)

        @pl.when(i == n - 1)
        def _():
            for b in range(6):
                dbin_o[:, b * d:(b + 1) * d] = jnp.sum(bacc[b], axis=0, keepdims=True)
            dca_o[...] = jnp.sum(cacc_a[...], axis=1)
            dcb_o[...] = jnp.sum(cacc_b[...], axis=1)

    ka8 = 8
    kb8 = 32
    return pl.pallas_call(
        body, name=name, grid=(n,),
        in_specs=[tile(0), tile(1), tile(2), tile(3), tile(4),
                  _prev_halo(tm, d, 1), _prev_halo(tm, d, 2), _prev_halo(tm, d, 3), _prev_halo(tm, d, 4),
                  tile(), tile(), tile(),
                  _next_halo(tm, d, 0, nh), _next_halo(tm, d, 0, nh), _next_halo(tm, d, 0, nh), _next_halo(tm, d, 0, nh),
                  vec(K_A), vec(K_B), ANY],
        out_specs=[pl.BlockSpec((tm, 6 * d), lambda i: (i, 0)), pl.BlockSpec((1, 6 * d), lambda i: (0, 0)),
                   pl.BlockSpec((ka8, d), lambda i: (0, 0)), pl.BlockSpec((kb8, d), lambda i: (0, 0))],
        out_shape=[jax.ShapeDtypeStruct((s, 9 * d), BF16), jax.ShapeDtypeStruct((1, 6 * d), F32),
                   jax.ShapeDtypeStruct((ka8, d), F32), jax.ShapeDtypeStruct((kb8, d), F32)],
        scratch_shapes=[pltpu.VMEM((HALO + tm, d), F32), pltpu.VMEM((HALO + tm, d), F32),
                        pltpu.VMEM((tm + HALO, d), F32), pltpu.VMEM((tm + HALO, d), F32),
                        pltpu.VMEM((tm + HALO, d), F32),
                        pltpu.VMEM((6, 8, d), F32), pltpu.VMEM((ka8, 8, d), F32), pltpu.VMEM((kb8, 8, d), F32),
                        pltpu.VMEM((7, tm + HALO - 8, d), F32), pltpu.VMEM((7, tm + HALO - 8, d), F32)]
        + [pltpu.VMEM((tm + HALO, d), F32)] * N_GROUPS,
        input_output_aliases={18: 0},
        compiler_params=_params("arbitrary"),
    )(proj, proj, proj, proj, proj, proj, proj, proj, proj, dpa, dv, dp, proj, dpa, dv, dp, conv_a, conv_b, dproj)


def _dh_rms_bwd(dproj, win3, x0, dx1, g, name, rider=None):
    s, p = dproj.shape
    d = x0.shape[1]
    tm = _tile(s, 1024)
    nk, _, tk = win3.shape
    n = s // tm

    def body(dp_r, w_r, x_r, dx1_r, g_r, dx_o, dg_o, acc, gacc):
        i = pl.program_id(0)
        k = pl.program_id(1)

        @pl.when((i == 0) & (k == 0))
        def _():
            gacc[...] = jnp.zeros_like(gacc)

        @pl.when(k == 0)
        def _():
            acc[...] = jnp.zeros_like(acc)

        acc[...] += lax.dot_general(dp_r[...], w_r[...], (((1,), (1,)), ((), ())), preferred_element_type=F32)

        @pl.when(k == nk - 1)
        def _():
            dx, dg = _rms_bwd(acc[...], x_r[...], g_r[...])
            gacc[...] += dg
            dx_o[...] = dx1_r[...] + dx

        @pl.when((i == n - 1) & (k == nk - 1))
        def _():
            dg_o[...] = jnp.sum(gacc[...], axis=0, keepdims=True)

    return _hosted_call(
        body, name=name, grid=(n, nk),
        in_specs=[pl.BlockSpec((tm, tk), lambda i, k: (i, k)), pl.BlockSpec((None, d, tk), lambda i, k: (k, 0, 0)),
                  pl.BlockSpec((tm, d), lambda i, k: (i, 0)), pl.BlockSpec((tm, d), lambda i, k: (i, 0)),
                  pl.BlockSpec((1, d), lambda i, k: (0, 0))],
        out_specs=[pl.BlockSpec((tm, d), lambda i, k: (i, 0)), pl.BlockSpec((1, d), lambda i, k: (0, 0))],
        out_shape=[jax.ShapeDtypeStruct((s, d), F32), jax.ShapeDtypeStruct((1, d), F32)],
        scratch_shapes=[pltpu.VMEM((tm, d), F32), pltpu.VMEM((8, d), F32)],
        args=(dproj, win3, x0, dx1, g.reshape(1, d)), rider=rider)


EARLY = ("w_out_a", "w_out_b", "w_pool", "w_o", "w_mlp1", "w_mlp2")


def _layer_fwd(x0, l, g_mix, win3, b_in, weights, proj_rider=None, bo_rider=None, mlp_rider=None, h=None, g_next=None):
    if h is None:
        h, _ = _rms_fwd(x0, g_mix, f"rms_fwd_{l}")
    proj, carried_proj = _proj(h, win3, b_in, f"proj_{l}", proj_rider)
    w = weights(carried_proj) if callable(weights) else weights
    pa, sb, p, v = _mix_fwd(proj, w["conv_a"], w["conv_b"], w["conv_b_bias"], w["ln_b_g"], w["ln_b_b"], f"mix_fwd_{l}")
    (ya, yb, yc, mg, x1, h2), carried_bo = _branch_out(pa, sb, p, proj, x0, w["woa"], w["wob"], w["wp"], w["wo"],
                                                       w["b_out_b"], w["pool_scale"], w["g_mlp"], f"branch_out_{l}", bo_rider)
    rider = mlp_rider(carried_bo) if callable(mlp_rider) else mlp_rider
    (f, a, x2, *h_next), carried_mlp = _mlp_fwd(h2, x1, w["w1_3"], w["w2"], g_next, f"mlp_fwd_{l}", rider)
    saved = dict(x0=x0, h=h, proj=proj, pa=pa, sb=sb, p=p, v=v, ya=ya, yb=yb, yc=yc, mg=mg, x1=x1, h2=h2, f=f, a=a)
    return x2, saved, w, carried_mlp, (h_next[0] if h_next else None)


def _layer_bwd_early(dx2, w, sv, l, mlp_rider=None, dw2_rider=None, dw1_rider=None, swap_early=False):
    d = dx2.shape[1]
    (df, dx1, dgm), carried_mlp = _mlp_bwd(dx2, sv["f"], sv["x1"], w["w2t"], w["w1t"], w["g_mlp"], f"mlp_bwd_{l}",
                                           mlp_rider)
    if dw2_rider is None:
        dw2, carried_dw2 = _tn(sv["a"], dx2, 1, f"dw2_{l}"), []
    else:
        dw2, carried_dw2 = _tn(sv["a"], dx2, 1, f"dw2_{l}", dw2_rider)
    if dw1_rider is None:
        dw1, carried_dw1 = _tn(sv["h2"], df, N_CHIPS, f"dw1_{l}"), []
    else:
        dw1, carried_dw1 = _tn(sv["h2"], df, N_CHIPS, f"dw1_{l}", dw1_rider)
    dpa, ds, dp, dproj, dbob, dps, dbg, dwoa, dwob, dwp = _branch_out_bwd(
        dx1, sv["proj"], sv["ya"], sv["yb"], sv["yc"], sv["pa"], sv["sb"], sv["p"], w["wot"], w["woat"], w["wobt"],
        w["wpt"], w["pool_scale"], f"branch_out_bwd_{l}")
    dwo = _tn(sv["mg"], dx1, 1, f"dwo_{l}")
    ff = dw2.shape[1]
    gc = d // N_GROUPS
    big = dict(
        w_out_a=dwoa.reshape(N_CHIPS, d // N_CHIPS, d),
        w_out_b=dwob.reshape(N_CHIPS, d // N_CHIPS, d),
        w_pool=dwp.reshape(N_GROUPS, N_CHIPS, gc // N_CHIPS, gc).transpose(1, 0, 2, 3).reshape(N_CHIPS, gc, gc),
        w_o=dwo.reshape(N_CHIPS, d // N_CHIPS, d),
        w_mlp1=dw1,
        w_mlp2=dw2.reshape(N_CHIPS, ff // N_CHIPS, d),
    )
    swap = _ride_swap([big[k] for k in EARLY]) if swap_early else None
    (dv, dlg, dlb, dcb), swapped = _ln_silu_bwd(sv["v"], ds, w["ln_b_g"], w["ln_b_b"], f"ln_silu_bwd_{l}", swap)
    dproj, dbm, dca, dcvb = _mix_bwd(sv["proj"], dpa, dv, dp, dproj, w["conv_a"], w["conv_b"], f"mix_bwd_{l}")
    small = dict(
        b_in=jnp.concatenate([dbm, dbg], axis=1).reshape(9, d), conv_a=dca[:K_A], conv_b=dcvb[:K_B],
        conv_b_bias=dcb, ln_b_g=dlg, ln_b_b=dlb, b_out_b=dbob, pool_scale=dps, g_mlp=dgm,
    )
    return dx1, dproj, big, small, carried_mlp + carried_dw2 + carried_dw1, swapped


def _layer_bwd_late(dx1, dproj, w, sv, l, dwin_rider=None, dh_rider=None):
    if dwin_rider is None:
        dwin, carried_dwin = _tn(sv["h"], dproj, N_CHIPS, f"dwin_{l}"), []
    else:
        dwin, carried_dwin = _tn(sv["h"], dproj, N_CHIPS, f"dwin_{l}", dwin_rider)
    rider = dh_rider(dwin) if callable(dh_rider) else dh_rider
    (dx0, dgmix), carried_dh = _dh_rms_bwd(dproj, w["win3"], sv["x0"], dx1, w["g_mix"], f"dh_{l}", rider)
    return dx0, dwin, dgmix, carried_dwin, carried_dh


def _place():
    x, y, c = lax.axis_index("x"), lax.axis_index("y"), lax.axis_index("c")
    chips = [(1 - x, y), (x, 1 - y), (1 - x, 1 - y)]
    return x, y, c, chips


def _place_shard(w, chip_arr, dtype, name):
    rows, cols = w.shape
    tr = _tile(rows, max(16, (1 << 19) // cols), 16)

    def body(k_ref, w_r, o_r):
        o_r[...] = w_r[...].astype(dtype)

    return pl.pallas_call(
        body, name=name,
        grid_spec=pltpu.PrefetchScalarGridSpec(
            num_scalar_prefetch=1, grid=(rows // tr,),
            in_specs=[pl.BlockSpec((tr, cols), lambda i, k_ref: (i, 0))],
            out_specs=pl.BlockSpec((None, tr, cols), lambda i, k_ref: (k_ref[0], i, 0))),
        out_shape=jax.ShapeDtypeStruct((N_CHIPS, rows, cols), dtype),
        compiler_params=_params("parallel"),
    )(chip_arr, w)


def _ride_rows(v):
    def copy(ins, outs, sems, m, src_dev):
        x, y, c = lax.axis_index("x"), lax.axis_index("y"), lax.axis_index("c")
        peer = (x ^ ((m >> 2) & 1), y ^ ((m >> 1) & 1), c ^ (m & 1))
        return pltpu.make_async_remote_copy(
            src_ref=ins[0], dst_ref=outs[0].at[src_dev], send_sem=sems[0].at[m - 1], recv_sem=sems[1].at[m - 1],
            device_id=peer, device_id_type=MESH)

    def me():
        return 4 * lax.axis_index("x") + 2 * lax.axis_index("y") + lax.axis_index("c")

    def start(ins, outs, sems):
        pltpu.make_async_copy(ins[0], outs[0].at[me()], sems[2]).start()
        for m in range(1, N_DEV):
            copy(ins, outs, sems, m, me()).start()

    def finish(ins, outs, sems):
        for m in range(1, N_DEV):
            copy(ins, outs, sems, m, me() ^ m).wait_recv()
        for m in range(1, N_DEV):
            copy(ins, outs, sems, m, me()).wait_send()
        pltpu.make_async_copy(ins[0], outs[0].at[me()], sems[2]).wait()

    return _Rider([v], [jax.ShapeDtypeStruct((N_DEV, *v.shape), v.dtype)], [],
                  [pltpu.SemaphoreType.DMA((N_DEV - 1,)), pltpu.SemaphoreType.DMA((N_DEV - 1,)), pltpu.SemaphoreType.DMA],
                  start, finish)


def _ride_swap(grads):
    n = len(grads)

    def copy(ins, outs, sems, a, k):
        x, y, c, _ = _place()
        h = ins[a].shape[1] // 2
        return pltpu.make_async_remote_copy(
            src_ref=ins[a].at[k, pl.ds(pl.multiple_of((1 - c) * h, 8), h)], dst_ref=outs[a].at[k],
            send_sem=sems[0].at[a * N_CHIPS + k], recv_sem=sems[1].at[a * N_CHIPS + k], device_id=(x, y, 1 - c),
            device_id_type=MESH)

    def start(ins, outs, sems):
        for a in range(n):
            for k in range(N_CHIPS):
                copy(ins, outs, sems, a, k).start()

    def finish(ins, outs, sems):
        for a in range(n):
            for k in range(N_CHIPS):
                copy(ins, outs, sems, a, k).wait()

    return _Rider(grads, [jax.ShapeDtypeStruct((N_CHIPS, g.shape[1] // 2, g.shape[2]), g.dtype) for g in grads], [],
                  [pltpu.SemaphoreType.DMA((N_CHIPS * n,))] * 2, start, finish)


def _add_halves(g, r, c_arr, name):
    nk, rows, cols = g.shape
    h = rows // 2
    tr = _tile(h, max(8, (1 << 20) // cols), 8)
    nblk = h // tr

    def body(c_ref, g_r, r_r, o_r):
        o_r[...] = (g_r[...] + r_r[...]).astype(BF16)

    return pl.pallas_call(
        body, name=name,
        grid_spec=pltpu.PrefetchScalarGridSpec(
            num_scalar_prefetch=1, grid=(nk, nblk),
            in_specs=[pl.BlockSpec((None, tr, cols), lambda k, i, c_ref: (k, c_ref[0] * nblk + i, 0)),
                      pl.BlockSpec((None, tr, cols), lambda k, i, c_ref: (k, i, 0))],
            out_specs=pl.BlockSpec((None, tr, cols), lambda k, i, c_ref: (k, i, 0))),
        out_shape=jax.ShapeDtypeStruct((nk, h, cols), BF16),
        compiler_params=_params("parallel", "parallel"),
    )(c_arr, g, r)


def _ride_exchange(parts):
    n = len(parts)

    def copy(ins, outs, sems, a, j):
        x, y, c, chips = _place()
        px, py = chips[j]
        return pltpu.make_async_remote_copy(
            src_ref=ins[a].at[2 * px + py], dst_ref=outs[a].at[j], send_sem=sems[0].at[a * 3 + j],
            recv_sem=sems[1].at[a * 3 + j], device_id=(px, py, c), device_id_type=MESH)

    def start(ins, outs, sems):
        for a in range(n):
            for j in range(3):
                copy(ins, outs, sems, a, j).start()

    def finish(ins, outs, sems):
        for a in range(n):
            for j in range(3):
                copy(ins, outs, sems, a, j).wait()

    return _Rider(parts, [jax.ShapeDtypeStruct((3, *p.shape[1:]), p.dtype) for p in parts], [],
                  [pltpu.SemaphoreType.DMA((3 * n,))] * 2, start, finish)


def _ride_gather(placed=(), landed=(), forward=True):
    placed, landed = list(placed), list(landed)
    n_ici = len(placed)
    fwd = (list(range(n_ici)) if forward else []) + list(range(n_ici, n_ici + len(landed)))

    def ici(ins, outs, sems, a, j, src_chip):
        x, y, c, chips = _place()
        h = ins[a].shape[1] // 2
        rows = pl.ds(pl.multiple_of(c * h, 8), h)
        return pltpu.make_async_remote_copy(
            src_ref=ins[a].at[2 * x + y, rows], dst_ref=outs[a].at[src_chip, rows], send_sem=sems[0].at[a * 3 + j],
            recv_sem=sems[1].at[a * 3 + j], device_id=(*chips[j], c), device_id_type=MESH)

    def d2d(ins, outs, sems, a, j, which):
        x, y, c, chips = _place()
        px, py = chips[j]
        h = ins[a].shape[1] // 2
        rows = pl.ds(pl.multiple_of(which * h, 8), h)
        return pltpu.make_async_remote_copy(
            src_ref=ins[a].at[2 * px + py, rows], dst_ref=outs[a].at[2 * px + py, rows], send_sem=sems[2].at[a * 3 + j],
            recv_sem=sems[3].at[a * 3 + j], device_id=(x, y, 1 - c), device_id_type=MESH)

    def start(ins, outs, sems):
        x, y, c, _ = _place()
        for a in range(n_ici):
            for j in range(3):
                ici(ins, outs, sems, a, j, 2 * x + y).start()
        for a in range(n_ici, n_ici + len(landed)):
            for j in range(3):
                d2d(ins, outs, sems, a, j, c).start()

    def land(ins, outs, sems, then_forward):
        _, _, c, chips = _place()
        for a in range(n_ici):
            for j, (px, py) in enumerate(chips):
                ici(ins, outs, sems, a, j, 2 * px + py).wait_recv()
                if then_forward:
                    d2d(ins, outs, sems, a, j, c).start()

    def middle(ins, outs, sems):
        land(ins, outs, sems, True)

    def finish(ins, outs, sems):
        x, y, c, _ = _place()
        if not forward:
            land(ins, outs, sems, False)
        for a in fwd:
            for j in range(3):
                d2d(ins, outs, sems, a, j, 1 - c).wait_recv()
        for a in range(n_ici):
            for j in range(3):
                ici(ins, outs, sems, a, j, 2 * x + y).wait_send()
        for a in fwd:
            for j in range(3):
                d2d(ins, outs, sems, a, j, c).wait_send()

    arrays = placed + landed
    n = len(arrays)
    return _Rider(arrays, [jax.ShapeDtypeStruct(p.shape, p.dtype) for p in arrays], [(a, a) for a in range(n)],
                  [pltpu.SemaphoreType.DMA((3 * n,))] * 4, start, finish, middle if (n_ici and forward) else None)


def _sum_chips(r, name):
    nk, rows, cols = r.shape
    tr = _tile(rows, max(8, (1 << 19) // cols), 8)

    def body(r_r, o_r):
        acc = r_r[0].astype(F32)
        for k in range(1, nk):
            acc = acc + r_r[k].astype(F32)
        o_r[...] = acc

    return pl.pallas_call(
        body, name=name, grid=(rows // tr,),
        in_specs=[pl.BlockSpec((nk, tr, cols), lambda i: (0, i, 0))],
        out_specs=pl.BlockSpec((tr, cols), lambda i: (i, 0)),
        out_shape=jax.ShapeDtypeStruct((rows, cols), F32),
        compiler_params=_params("parallel"),
    )(r)


def _sum_chips_into(own, arrived, dst, layer, n_layers, place_arr, name):
    _, h, cols = own.shape
    tr = _tile(h, max(8, (1 << 20) // cols), 8)
    nblk = h // tr

    def body(p_ref, own_r, arr_r, *rest):
        o_r = rest[-1]
        acc = own_r[...].astype(F32)
        for j in range(3):
            acc = acc + arr_r[j].astype(F32)
        o_r[...] = acc

    in_specs = [pl.BlockSpec((None, tr, cols), lambda i, p_ref: (p_ref[0], i, 0)),
                pl.BlockSpec((3, tr, cols), lambda i, p_ref: (0, i, 0))]
    args = [place_arr, own, arrived]
    aliases = {}
    if dst is not None:
        in_specs.append(ANY)
        args.append(dst)
        aliases = {3: 0}
    return pl.pallas_call(
        body, name=name,
        grid_spec=pltpu.PrefetchScalarGridSpec(
            num_scalar_prefetch=1, grid=(nblk,), in_specs=in_specs,
            out_specs=pl.BlockSpec((None, tr, cols), lambda i, p_ref: (layer, p_ref[1] * nblk + i, 0))),
        out_shape=jax.ShapeDtypeStruct((n_layers, 2 * h, cols), F32),
        input_output_aliases=aliases,
        compiler_params=_params("parallel"),
    )(*args)


def _ride_share(shards):
    n = len(shards)
    n_layers = shards[0].shape[0]

    def copy(ins, outs, sems, a, l, which):
        x, y, c, _ = _place()
        h = ins[a].shape[1] // 2
        rows = pl.ds(pl.multiple_of(which * h, 8), h)
        return pltpu.make_async_remote_copy(
            src_ref=ins[a].at[l, rows], dst_ref=outs[a].at[l, rows], send_sem=sems[0].at[a * n_layers + l],
            recv_sem=sems[1].at[a * n_layers + l], device_id=(x, y, 1 - c), device_id_type=MESH)

    def start(ins, outs, sems):
        c = lax.axis_index("c")
        for a in range(n):
            for l in range(n_layers):
                copy(ins, outs, sems, a, l, c).start()

    def finish(ins, outs, sems):
        c = lax.axis_index("c")
        for a in range(n):
            for l in range(n_layers):
                copy(ins, outs, sems, a, l, 1 - c).wait_recv()
        for a in range(n):
            for l in range(n_layers):
                copy(ins, outs, sems, a, l, c).wait_send()

    return _Rider(shards, [jax.ShapeDtypeStruct(a.shape, a.dtype) for a in shards], [(a, a) for a in range(n)],
                  [pltpu.SemaphoreType.DMA((n * n_layers,))] * 2, start, finish)


def _join_riders(first, second):
    k_in, k_out, k_sem = len(first.args), len(first.out_shape), len(first.sems)

    def start(ins, outs, sems):
        first.start(ins[:k_in], outs[:k_out], sems[:k_sem])
        second.start(ins[k_in:], outs[k_out:], sems[k_sem:])

    def finish(ins, outs, sems):
        first.finish(ins[:k_in], outs[:k_out], sems[:k_sem])
        second.finish(ins[k_in:], outs[k_out:], sems[k_sem:])

    return _Rider(first.args + second.args, first.out_shape + second.out_shape,
                  first.aliases + [(i + k_in, o + k_out) for i, o in second.aliases], first.sems + second.sems, start, finish)


def _adamw(w, g, m, v, name):
    rows, cols = w.shape
    tr = _tile(rows, max(8, (1 << 19) // cols), 8)
    c1 = 1.0 - ADAM_B1 ** ADAM_STEP
    c2 = 1.0 - ADAM_B2 ** ADAM_STEP

    def body(w_r, g_r, m_r, v_r, g_o, d_o, m_o, v_o):
        gv = g_r[...]
        g_o[...] = gv
        mn = ADAM_B1 * m_r[...] + (1.0 - ADAM_B1) * gv
        vn = ADAM_B2 * v_r[...] + (1.0 - ADAM_B2) * (gv * gv)
        m_o[...] = mn
        v_o[...] = vn
        d_o[...] = -ADAM_LR * ((mn / c1) / (jnp.sqrt(vn / c2) + ADAM_EPS) + ADAM_WD * w_r[...])

    spec = pl.BlockSpec((tr, cols), lambda i: (i, 0))
    return pl.pallas_call(
        body, name=name, grid=(rows // tr,), in_specs=[spec] * 4, out_specs=[spec] * 4,
        out_shape=[jax.ShapeDtypeStruct((rows, cols), F32)] * 4,
        compiler_params=_params("parallel"),
    )(w, g, m, v)


BIG = ("w_in", "w_out_a", "w_out_b", "w_pool", "w_o", "w_mlp1", "w_mlp2")
SMALL = ("g_mix", "b_in", "conv_b_bias", "ln_b_g", "ln_b_b", "b_out_b", "pool_scale", "g_mlp")
CONVS = ("conv_a", "conv_b")
WEIGHTS = ("g_mix", "w_in", "b_in", "conv_a", "w_out_a", "conv_b", "conv_b_bias", "ln_b_g", "ln_b_b", "w_out_b", "b_out_b",
           "w_pool", "pool_scale", "w_o", "g_mlp", "w_mlp1", "w_mlp2", "g_final")


def _as2d(a):
    return a.reshape(-1, a.shape[-1])


def _pad_rows(a, rows):
    return jnp.pad(a, ((0, rows - a.shape[0]), (0, 0)))


def _full_weights(big_g, conv_g, rep, l, d):
    gc = d // N_GROUPS
    w = {k: v[l] for k, v in rep.items()}
    w["b_in"] = rep["b_in"][l].reshape(1, -1)
    win3 = big_g["w_in"]
    w["win3"] = win3
    for src, dst in (("w_out_a", "woa"), ("w_out_b", "wob"), ("w_o", "wo")):
        full = big_g[src].reshape(d, d)
        w[dst] = full
        w[dst + "t"] = full.T
    wp = big_g["w_pool"].reshape(N_CHIPS, N_GROUPS, gc // N_CHIPS, gc).transpose(1, 0, 2, 3).reshape(N_GROUPS, gc, gc)
    w["wp"] = wp
    w["wpt"] = wp.transpose(0, 2, 1)
    w1_3 = big_g["w_mlp1"]
    w["w1_3"] = w1_3
    w["w1t"] = w1_3.transpose(0, 2, 1).reshape(-1, d)
    w2 = big_g["w_mlp2"].reshape(-1, d)
    w["w2"] = w2
    w["w2t"] = w2.T
    ca, cb = conv_g
    w["conv_a"] = ca
    w["conv_b"] = cb
    return w


def kernel(x, g_mix, w_in, b_in, conv_a, w_out_a, conv_b, conv_b_bias, ln_b_g, ln_b_b, w_out_b, b_out_b, w_pool, pool_scale, w_o, g_mlp, w_mlp1, w_mlp2, g_final, loss_target, m_g_mix, m_w_in, m_b_in, m_conv_a, m_w_out_a, m_conv_b, m_conv_b_bias, m_ln_b_g, m_ln_b_b, m_w_out_b, m_b_out_b, m_w_pool, m_pool_scale, m_w_o, m_g_mlp, m_w_mlp1, m_w_mlp2, m_g_final, v_g_mix, v_w_in, v_b_in, v_conv_a, v_w_out_a, v_conv_b, v_conv_b_bias, v_ln_b_g, v_ln_b_b, v_w_out_b, v_b_out_b, v_w_pool, v_pool_scale, v_w_o, v_g_mlp, v_w_mlp1, v_w_mlp2, v_g_final):
    given = dict(g_mix=g_mix, w_in=w_in, b_in=b_in, conv_a=conv_a, w_out_a=w_out_a, conv_b=conv_b, conv_b_bias=conv_b_bias, ln_b_g=ln_b_g, ln_b_b=ln_b_b, w_out_b=w_out_b, b_out_b=b_out_b, w_pool=w_pool, pool_scale=pool_scale, w_o=w_o, g_mlp=g_mlp, w_mlp1=w_mlp1, w_mlp2=w_mlp2, g_final=g_final)
    mom = dict(g_mix=m_g_mix, w_in=m_w_in, b_in=m_b_in, conv_a=m_conv_a, w_out_a=m_w_out_a, conv_b=m_conv_b, conv_b_bias=m_conv_b_bias, ln_b_g=m_ln_b_g, ln_b_b=m_ln_b_b, w_out_b=m_w_out_b, b_out_b=m_b_out_b, w_pool=m_w_pool, pool_scale=m_pool_scale, w_o=m_w_o, g_mlp=m_g_mlp, w_mlp1=m_w_mlp1, w_mlp2=m_w_mlp2, g_final=m_g_final)
    var = dict(g_mix=v_g_mix, w_in=v_w_in, b_in=v_b_in, conv_a=v_conv_a, w_out_a=v_w_out_a, conv_b=v_conv_b, conv_b_bias=v_conv_b_bias, ln_b_g=v_ln_b_g, ln_b_b=v_ln_b_b, w_out_b=v_w_out_b, b_out_b=v_b_out_b, w_pool=v_w_pool, pool_scale=v_pool_scale, w_o=v_w_o, g_mlp=v_g_mlp, w_mlp1=v_w_mlp1, w_mlp2=v_w_mlp2, g_final=v_g_final)
    n_layers = w_in.shape[0]
    s, d = x.shape[1], x.shape[2]
    dq = d // N_CHIPS
    x_idx, y_idx, c_idx = lax.axis_index("x"), lax.axis_index("y"), lax.axis_index("c")
    chip = 2 * x_idx + y_idx
    c_arr = c_idx.astype(jnp.int32).reshape(1)

    conv_rows = n_layers * (K_A + K_B)
    conv_pad = -(-conv_rows // 16) * 16
    conv_pack = _pad_rows(jnp.concatenate([conv_a[l] for l in range(n_layers)] + [conv_b[l] for l in range(n_layers)], axis=0),
                          conv_pad)
    chip_arr = chip.astype(jnp.int32).reshape(1)
    place_arr = jnp.stack([chip, c_idx]).astype(jnp.int32)

    def placed(k, l):
        return _place_shard(_as2d(given[k][l]), chip_arr, BF16, f"place_{k}_{l}")

    rest = BIG[1:]
    h, (win_g, conv_all) = _rms_fwd(
        x[0], given["g_mix"][0], "rms_fwd_0",
        _ride_gather([placed("w_in", 0), _place_shard(conv_pack, chip_arr, F32, "place_convs")]))
    conv_full = conv_all.transpose(1, 0, 2).reshape(conv_pad, d)
    conv_g = [(conv_full[l * K_A:(l + 1) * K_A], conv_full[n_layers * K_A + l * K_B:n_layers * K_A + (l + 1) * K_B])
              for l in range(n_layers)]
    rep = {k: given[k] for k in SMALL}

    xl = x[0]
    ws, saved = [], []
    rest_g = None
    for l in range(n_layers):
        more = l + 1 < n_layers

        def weights(carried, l=l, win_g=win_g, rest_g=rest_g):
            return _full_weights(dict(zip(rest, carried if rest_g is None else rest_g), w_in=win_g), conv_g[l], rep, l, d)

        xl, sv, w, got, h = _layer_fwd(
            xl, l, given["g_mix"][l], win_g, given["b_in"][l].reshape(1, -1), weights,
            _ride_gather([placed(k, l) for k in rest]) if rest_g is None else None,
            _ride_gather([placed("w_in", l + 1)], forward=False) if more else None,
            (lambda landed, l=l: _ride_gather([placed(k, l + 1) for k in rest], landed)) if more else None,
            h, given["g_mix"][l + 1] if more else None)
        if more:
            rest_g, win_g = got[:len(rest)], got[len(rest)]
        ws.append(w)
        saved.append(sv)
    dx, loss, dgf = _loss_head(xl, loss_target[0], g_final, "loss_head")
    loss = lax.psum(loss[0, 0], ("x", "y", "c"))

    parts, arrived, small = {}, {}, [None] * n_layers
    pending = []
    for l in reversed(range(n_layers)):
        w, sv = ws[l], saved[l]
        hosts = [[q for q in pending if q[1] not in ("w_mlp1", "w_mlp2")],
                 [q for q in pending if q[1] == "w_mlp2"],
                 [q for q in pending if q[1] == "w_mlp1"]]
        dx1, dproj, early, sm, got, swapped = _layer_bwd_early(
            dx, w, sv, l, *[_ride_exchange([parts[q] for q in qs]) if qs else None for qs in hosts], swap_early=(l == 0))
        arrived.update(zip(hosts[0] + hosts[1] + hosts[2], got))
        if l > 0:
            order = ("w_in",) + EARLY
            dx, dwin, dgmix, _, swapped = _layer_bwd_late(
                dx1, dproj, w, sv, l, None, lambda dwin, early=early: _ride_swap([dwin] + [early[k] for k in EARLY]))
            full = dict(early, w_in=dwin)
            for k, r in zip(order, swapped):
                parts[(l, k)] = _add_halves(full[k], r, c_arr, f"grad_add_halves_{k}_{l}")
            pending = [(l, k) for k in order]
        else:
            for k, r in zip(EARLY, swapped):
                parts[(l, k)] = _add_halves(early[k], r, c_arr, f"grad_add_halves_{k}_{l}")

            def w_in_rider(dwin, l=l):
                (r,) = _comm_call(_ride_swap([dwin]), f"grad_swap_w_in_{l}")
                parts[(l, "w_in")] = _add_halves(dwin, r, c_arr, f"grad_add_halves_w_in_{l}")
                return _ride_exchange([parts[(l, "w_in")]])

            dx, dwin, dgmix, got_early, got_w_in = _layer_bwd_late(
                dx1, dproj, w, sv, l, _ride_exchange([parts[(l, k)] for k in EARLY]), w_in_rider)
            arrived.update(zip([(l, k) for k in EARLY], got_early))
            arrived[(l, "w_in")] = got_w_in[0]
            pending = []
        sm["g_mix"] = dgmix
        small[l] = sm
    reduced = []
    for k in BIG:
        dst = None
        for l in range(n_layers):
            dst = _sum_chips_into(parts[(l, k)], arrived[(l, k)], dst, l, n_layers, place_arr, f"grad_sum_chips_{k}_{l}")
        reduced.append(dst)

    small_rows = []
    for k in SMALL + CONVS:
        for l in range(n_layers):
            small_rows.append(small[l][k])
    small_rows.append(dgf)
    pack = jnp.concatenate(small_rows, axis=0)
    n_small = pack.shape[0]
    pack = _pad_rows(pack, -(-n_small // 8) * 8)
    *shared, small_all = _comm_call(_join_riders(_ride_share(reduced), _ride_rows(pack)), "grad_share_and_small")
    big_grad = dict(zip(BIG, shared))
    small_sum = _sum_chips(small_all, "sum_small_grads")

    out = {}
    for k in BIG:
        shape = given[k].shape
        g2 = big_grad[k].reshape(-1, shape[-1])
        res = _adamw(given[k].reshape(g2.shape), g2, mom[k].reshape(g2.shape), var[k].reshape(g2.shape), f"adamw_{k}")
        out[k] = tuple(a.reshape(shape) for a in res)

    def rows_of(k):
        return {"b_in": 9, "conv_a": K_A, "conv_b": K_B}.get(k, 1)

    def pack_rep(src):
        rows = [src[k].reshape(n_layers * rows_of(k), d) for k in SMALL] + [src["g_final"].reshape(1, d)]
        a = jnp.concatenate(rows, axis=0)
        return _pad_rows(a, -(-a.shape[0] // 8) * 8)

    def pack_conv(src):
        a = jnp.concatenate([src[k].reshape(n_layers * rows_of(k), dq) for k in CONVS], axis=0)
        return _pad_rows(a, -(-a.shape[0] // 8) * 8)

    offs = {}
    r = 0
    for k in SMALL + CONVS:
        offs[k] = r
        r += n_layers * rows_of(k)
    offs["g_final"] = r
    n_rep = sum(n_layers * rows_of(k) for k in SMALL)
    g_rep = jnp.concatenate([small_sum[:n_rep], small_sum[offs["g_final"]:offs["g_final"] + 1]], axis=0)
    g_rep = _pad_rows(g_rep, -(-g_rep.shape[0] // 8) * 8)
    g_conv_full = small_sum[offs["conv_a"]:offs["g_final"]]
    g_conv = lax.dynamic_slice_in_dim(g_conv_full, chip * dq, dq, axis=1)
    g_conv = _pad_rows(g_conv, -(-g_conv.shape[0] // 8) * 8)
    rep_res = tuple(_adamw(pack_rep(given), g_rep, pack_rep(mom), pack_rep(var), "adamw_small"))
    conv_res = tuple(_adamw(pack_conv(given), g_conv, pack_conv(mom), pack_conv(var), "adamw_conv"))
    r = 0
    for k in SMALL:
        nr = n_layers * rows_of(k)
        out[k] = tuple(a[r:r + nr].reshape(given[k].shape) for a in rep_res)
        r += nr
    out["g_final"] = tuple(a[r].reshape(given["g_final"].shape) for a in rep_res)
    r = 0
    for k in CONVS:
        nr = n_layers * rows_of(k)
        out[k] = tuple(a[r:r + nr].reshape(given[k].shape) for a in conv_res)
        r += nr

    res = [loss, dx.reshape(x.shape)]
    for i in range(4):
        res += [out[k][i] for k in WEIGHTS]
    return tuple(res)
```
